```python
import math
import jax, jax.numpy as jnp
from jax import lax
import numpy as np

D_MODEL = 2048
BATCH = 8
SEQ = 2048
DEPTH = 2

HEAD_DIM = 64
BLOCK = 128
A_Q_HEADS = 16
A_KV_HEADS = 2
A_GROUP = A_Q_HEADS // A_KV_HEADS
A_WINDOW = 128
B_HEADS = 8
B_PATTERNS = ((128, 1), (512, 4), (2048, 16))
C_HEADS = 16
D_HEADS = 16
D_Q_RANK = 512
D_KV_RANK = 256
D_NOPE = 64
D_ROPE = 32
D_V = 64
ROPE_BASE = 10000.0
D_FF = 4 * D_MODEL
LN_EPS = 1e-5
RMS_EPS = 1e-6
ALPHA = (2 * DEPTH) ** 0.25
BETA = (8 * DEPTH) ** -0.25
N_EVEN = (DEPTH + 1) // 2
N_ODD = DEPTH // 2
A_Q_W = A_Q_HEADS * HEAD_DIM
A_KV_W = A_KV_HEADS * HEAD_DIM
B_W = B_HEADS * HEAD_DIM
EVEN_IN = A_Q_W + 2 * A_KV_W + 3 * B_W * len(B_PATTERNS)
EVEN_OUT = A_Q_W + B_W
C_W = C_HEADS * HEAD_DIM
ODD_IN = 3 * C_W + D_Q_RANK + D_KV_RANK + D_ROPE
ODD_OUT = C_W + D_HEADS * D_V

kernel_name = 'hybrid_swa_dilated_stickbreak_mla_deepnorm'


def layer_norm(x, g, b):
    xf = x.astype(jnp.float32)
    mu = jnp.mean(xf, axis=-1, keepdims=True)
    xc = xf - mu
    var = jnp.mean(xc * xc, axis=-1, keepdims=True)
    y = xc * lax.rsqrt(var + LN_EPS) * g.astype(jnp.float32) + b.astype(jnp.float32)
    return y.astype(x.dtype)


def rms_norm(x, g):
    xf = x.astype(jnp.float32)
    y = xf * lax.rsqrt(jnp.mean(xf * xf, axis=-1, keepdims=True) + RMS_EPS) * g.astype(jnp.float32)
    return y.astype(x.dtype)


def alibi_slopes(n):
    return 2.0 ** (-8.0 * jnp.arange(1, n + 1, dtype=jnp.float32) / n)


def banded_attention(q, k, v, n_back, dist_scale, slopes, sinks=None):
    bsz, n, kh, g, dh = q.shape
    nb = -(-n // BLOCK)
    n_prev = -(-n_back // BLOCK)
    pad = nb * BLOCK - n
    qb = jnp.pad(q, ((0, 0), (0, pad), (0, 0), (0, 0), (0, 0))).reshape(bsz, nb, BLOCK, kh, g, dh)
    kv_pad = ((0, 0), (n_prev * BLOCK, pad), (0, 0), (0, 0))
    kp = jnp.pad(k, kv_pad).reshape(bsz, nb + n_prev, BLOCK, kh, dh)
    vp = jnp.pad(v, kv_pad).reshape(bsz, nb + n_prev, BLOCK, kh, dh)
    kb = jnp.concatenate([kp[:, p:p + nb] for p in range(n_prev + 1)], axis=2)
    vb = jnp.concatenate([vp[:, p:p + nb] for p in range(n_prev + 1)], axis=2)
    scores = jnp.einsum('bnqkgd,bnskd->bnkgqs', qb, kb).astype(jnp.float32) * (1.0 / math.sqrt(dh))
    n_keys = (n_prev + 1) * BLOCK
    qpos = jnp.arange(BLOCK)
    spos = jnp.arange(n_keys)
    rel = n_prev * BLOCK + qpos[:, None] - spos[None, :]
    key_idx = (jnp.arange(nb)[:, None] - n_prev) * BLOCK + spos[None, :]
    valid = (rel >= 0)[None] & (rel <= n_back)[None] & (key_idx >= 0)[:, None, :]
    bias = -slopes.astype(jnp.float32)[:, :, None, None] * (rel * dist_scale).astype(jnp.float32)
    scores = jnp.where(valid[None, :, None, None], scores + bias[None, None], -jnp.inf)
    m = jnp.max(scores, axis=-1)
    if sinks is not None:
        sink = sinks.astype(jnp.float32)[None, None, :, :, None]
        m = jnp.maximum(m, sink)
    p = jnp.exp(scores - m[..., None])
    denom = jnp.sum(p, axis=-1)
    if sinks is not None:
        denom = denom + jnp.exp(sink - m)
    out = jnp.einsum('bnkgqs,bnskd->bnqkgd', p.astype(v.dtype), vb).astype(jnp.float32)
    out = out / jnp.moveaxis(denom, -1, 2)[..., None]
    lse = jnp.moveaxis(m + jnp.log(denom), -1, 2)
    out = out.reshape(bsz, nb * BLOCK, kh, g, dh)[:, :n].astype(q.dtype)
    lse = lse.reshape(bsz, nb * BLOCK, kh, g)[:, :n]
    return out, lse


def to_strided(t, d):
    b, s = t.shape[:2]
    rest = t.shape[2:]
    t = jnp.moveaxis(t.reshape(b, s // d, d, *rest), 2, 1)
    return t.reshape(b * d, s // d, *rest)


def from_strided(t, b):
    bd, n = t.shape[:2]
    d = bd // b
    rest = t.shape[2:]
    t = jnp.moveaxis(t.reshape(b, d, n, *rest), 1, 2)
    return t.reshape(b, n * d, *rest)


def even_mixer(x, w_in, sinks, w_out):
    bsz, seq, _ = x.shape
    h = jnp.einsum('bsd,de->bse', x, w_in)
    qa, ka, va, hb = jnp.split(h, [A_Q_W, A_Q_W + A_KV_W, A_Q_W + 2 * A_KV_W], axis=-1)
    qa = qa.reshape(bsz, seq, A_KV_HEADS, A_GROUP, HEAD_DIM)
    ka = ka.reshape(bsz, seq, A_KV_HEADS, HEAD_DIM)
    va = va.reshape(bsz, seq, A_KV_HEADS, HEAD_DIM)
    oa, _ = banded_attention(qa, ka, va, A_WINDOW - 1, 1,
                             alibi_slopes(A_Q_HEADS).reshape(A_KV_HEADS, A_GROUP),
                             sinks.reshape(A_KV_HEADS, A_GROUP))
    slopes_b = alibi_slopes(B_HEADS).reshape(B_HEADS, 1)
    outs, lses = [], []
    for gi, (window, dil) in enumerate(B_PATTERNS):
        blk = hb[..., gi * 3 * B_W:(gi + 1) * 3 * B_W].reshape(bsz, seq, 3, B_HEADS, HEAD_DIM)
        qb, kb, vb = blk[:, :, 0], blk[:, :, 1], blk[:, :, 2]
        o, lse = banded_attention(to_strided(qb[:, :, :, None], dil), to_strided(kb, dil),
                                  to_strided(vb, dil), window // dil, dil, slopes_b)
        outs.append(from_strided(o[:, :, :, 0], bsz))
        lses.append(from_strided(lse[:, :, :, 0], bsz))
    mix = jax.nn.softmax(jnp.stack(lses), axis=0)
    ob = jnp.einsum('gbsh,gbshd->bshd', mix, jnp.stack(outs).astype(jnp.float32)).astype(x.dtype)
    y = jnp.concatenate([oa.reshape(bsz, seq, A_Q_W), ob.reshape(bsz, seq, B_W)], axis=-1)
    return jnp.einsum('bse,ed->bsd', y, w_out)


def stick_breaking_attention(q, k, v):
    bsz, seq, nh, dh = q.shape
    scale = 1.0 / math.sqrt(dh)
    outs = []
    for i in range(seq // BLOCK):
        lo, hi = i * BLOCK, (i + 1) * BLOCK
        z = jnp.einsum('bqhd,bshd->bhqs', q[:, lo:hi], k[:, :hi]).astype(jnp.float32) * scale
        strict = jnp.arange(hi)[None, :] < (lo + jnp.arange(BLOCK))[:, None]
        log_beta = jax.nn.log_sigmoid(z)
        log_keep = jnp.where(strict, jax.nn.log_sigmoid(-z), 0.0)
        after = lax.cumsum(log_keep, axis=3, reverse=True) - log_keep
        w = jnp.where(strict, jnp.exp(log_beta + after), 0.0)
        outs.append(jnp.einsum('bhqs,bshd->bqhd', w.astype(v.dtype), v[:, :hi]))
    return jnp.concatenate(outs, axis=1)


def apply_rope(x, cos, sin):
    half = x.shape[-1] // 2
    shape = (1, cos.shape[0]) + (1,) * (x.ndim - 3) + (half,)
    c = cos.reshape(shape).astype(x.dtype)
    s = sin.reshape(shape).astype(x.dtype)
    x1, x2 = x[..., :half], x[..., half:]
    return jnp.concatenate([x1 * c - x2 * s, x1 * s + x2 * c], axis=-1)


def mla_attention(q_nope, q_rope, k_nope, k_rope, v):
    seq = q_nope.shape[1]
    scale = 1.0 / math.sqrt(D_NOPE + D_ROPE)
    outs = []
    for i in range(seq // BLOCK):
        lo, hi = i * BLOCK, (i + 1) * BLOCK
        s = (jnp.einsum('bqhd,bshd->bhqs', q_nope[:, lo:hi], k_nope[:, :hi]).astype(jnp.float32)
             + jnp.einsum('bqhr,bsr->bhqs', q_rope[:, lo:hi], k_rope[:, :hi]).astype(jnp.float32)) * scale
        causal = jnp.arange(hi)[None, :] <= (lo + jnp.arange(BLOCK))[:, None]
        p = jax.nn.softmax(jnp.where(causal, s, -jnp.inf), axis=-1)
        outs.append(jnp.einsum('bhqs,bshd->bqhd', p.astype(v.dtype), v[:, :hi]))
    return jnp.concatenate(outs, axis=1)


def odd_mixer(x, w_in, q_norm_g, kv_norm_g, w_uq, w_ukv, w_out):
    bsz, seq, _ = x.shape
    h = jnp.einsum('bsd,de->bse', x, w_in)
    qc, kc, vc, cq, ckv, kr = jnp.split(
        h, [C_W, 2 * C_W, 3 * C_W, 3 * C_W + D_Q_RANK, 3 * C_W + D_Q_RANK + D_KV_RANK], axis=-1)
    oc = stick_breaking_attention(qc.reshape(bsz, seq, C_HEADS, HEAD_DIM),
                                  kc.reshape(bsz, seq, C_HEADS, HEAD_DIM),
                                  vc.reshape(bsz, seq, C_HEADS, HEAD_DIM))
    q = jnp.einsum('bsr,re->bse', rms_norm(cq, q_norm_g), w_uq).reshape(bsz, seq, D_HEADS, D_NOPE + D_ROPE)
    kv = jnp.einsum('bsr,re->bse', rms_norm(ckv, kv_norm_g), w_ukv).reshape(bsz, seq, D_HEADS, D_NOPE + D_V)
    q_nope, q_rope = q[..., :D_NOPE], q[..., D_NOPE:]
    k_nope, v = kv[..., :D_NOPE], kv[..., D_NOPE:]
    inv_freq = ROPE_BASE ** (-jnp.arange(0, D_ROPE, 2, dtype=jnp.float32) / D_ROPE)
    ang = jnp.arange(seq, dtype=jnp.float32)[:, None] * inv_freq[None, :]
    cos, sin = jnp.cos(ang), jnp.sin(ang)
    od = mla_attention(q_nope, apply_rope(q_rope, cos, sin), k_nope, apply_rope(kr, cos, sin), v)
    y = jnp.concatenate([oc.reshape(bsz, seq, C_W), od.reshape(bsz, seq, D_HEADS * D_V)], axis=-1)
    return jnp.einsum('bse,ed->bsd', y, w_out)


def sqrelu_mlp(x, w1, w2):
    hid = jax.nn.relu(jnp.einsum('bsd,df->bsf', x, w1))
    return jnp.einsum('bsf,fd->bsd', hid * hid, w2)


def _fwd_setup_inputs(seed: int = 0) -> dict:
    key = jax.random.key(seed)
    ks = jax.random.split(key, 16)
    f32 = jnp.float32

    def normal(k, shape, scale):
        return jax.random.normal(k, shape, f32) * scale

    x = normal(ks[0], (BATCH, SEQ, D_MODEL), 1.0)
    b_group_col = jnp.concatenate([jnp.ones(2 * B_W, f32), jnp.full((B_W,), BETA, f32)])
    even_col = jnp.concatenate([jnp.ones(A_Q_W + A_KV_W, f32), jnp.full((A_KV_W,), BETA, f32)]
                               + [b_group_col] * len(B_PATTERNS))
    even_w_in = normal(ks[1], (N_EVEN, D_MODEL, EVEN_IN), D_MODEL ** -0.5) * even_col
    even_sinks = 1.0 + normal(ks[2], (N_EVEN, A_Q_HEADS), 1.0)
    even_w_out = normal(ks[3], (N_EVEN, EVEN_OUT, D_MODEL), BETA * EVEN_OUT ** -0.5)
    odd_col = jnp.concatenate([jnp.ones(2 * C_W, f32), jnp.full((C_W,), BETA, f32),
                               jnp.ones(D_Q_RANK + D_KV_RANK + D_ROPE, f32)])
    odd_w_in = normal(ks[4], (N_ODD, D_MODEL, ODD_IN), D_MODEL ** -0.5) * odd_col
    odd_q_norm_g = 1.0 + normal(ks[5], (N_ODD, D_Q_RANK), 0.02)
    odd_kv_norm_g = 1.0 + normal(ks[6], (N_ODD, D_KV_RANK), 0.02)
    odd_w_uq = normal(ks[7], (N_ODD, D_Q_RANK, D_HEADS * (D_NOPE + D_ROPE)), D_Q_RANK ** -0.5)
    ukv_col = jnp.tile(jnp.concatenate([jnp.ones(D_NOPE, f32), jnp.full((D_V,), BETA, f32)]), D_HEADS)
    odd_w_ukv = normal(ks[8], (N_ODD, D_KV_RANK, D_HEADS * (D_NOPE + D_V)), D_KV_RANK ** -0.5) * ukv_col
    odd_w_out = normal(ks[9], (N_ODD, ODD_OUT, D_MODEL), BETA * ODD_OUT ** -0.5)
    ln1_g = 1.0 + normal(ks[10], (DEPTH, D_MODEL), 0.02)
    ln1_b = normal(ks[11], (DEPTH, D_MODEL), 0.02)
    mlp_w1 = normal(ks[12], (DEPTH, D_MODEL, D_FF), D_MODEL ** -0.5)
    mlp_w2 = normal(ks[13], (DEPTH, D_FF, D_MODEL), BETA * D_FF ** -0.5)
    ln2_g = 1.0 + normal(ks[14], (DEPTH, D_MODEL), 0.02)
    ln2_b = normal(ks[15], (DEPTH, D_MODEL), 0.02)
    return {'x': x, 'even_w_in': even_w_in, 'even_sinks': even_sinks, 'even_w_out': even_w_out,
            'odd_w_in': odd_w_in, 'odd_q_norm_g': odd_q_norm_g, 'odd_kv_norm_g': odd_kv_norm_g,
            'odd_w_uq': odd_w_uq, 'odd_w_ukv': odd_w_ukv, 'odd_w_out': odd_w_out,
            'ln1_g': ln1_g, 'ln1_b': ln1_b, 'mlp_w1': mlp_w1, 'mlp_w2': mlp_w2,
            'ln2_g': ln2_g, 'ln2_b': ln2_b}


def _fwd_reference(x, even_w_in, even_sinks, even_w_out, odd_w_in, odd_q_norm_g, odd_kv_norm_g,
              odd_w_uq, odd_w_ukv, odd_w_out, ln1_g, ln1_b, mlp_w1, mlp_w2, ln2_g, ln2_b):
    for layer in range(DEPTH):
        j = layer // 2
        if layer % 2 == 0:
            mixed = even_mixer(x, even_w_in[j], even_sinks[j], even_w_out[j])
        else:
            mixed = odd_mixer(x, odd_w_in[j], odd_q_norm_g[j], odd_kv_norm_g[j],
                              odd_w_uq[j], odd_w_ukv[j], odd_w_out[j])
        x = layer_norm(ALPHA * x + mixed, ln1_g[layer], ln1_b[layer])
        x = layer_norm(ALPHA * x + sqrelu_mlp(x, mlp_w1[layer], mlp_w2[layer]), ln2_g[layer], ln2_b[layer])
    return x


import jax as _jax
import jax.numpy as _jnp

TWIN_FORMAT = 'train_step'
FWD_PARAMS = ['x', 'even_w_in', 'even_sinks', 'even_w_out', 'odd_w_in', 'odd_q_norm_g', 'odd_kv_norm_g', 'odd_w_uq', 'odd_w_ukv', 'odd_w_out', 'ln1_g', 'ln1_b', 'mlp_w1', 'mlp_w2', 'ln2_g', 'ln2_b']
TWIN_WEIGHTS = ['even_w_in', 'even_sinks', 'even_w_out', 'odd_w_in', 'odd_q_norm_g', 'odd_kv_norm_g', 'odd_w_uq', 'odd_w_ukv', 'odd_w_out', 'ln1_g', 'ln1_b', 'mlp_w1', 'mlp_w2', 'ln2_g', 'ln2_b']
TWIN_DIFF_INPUT = 'x'
TWIN_INPUTS = ['x', 'even_w_in', 'even_sinks', 'even_w_out', 'odd_w_in', 'odd_q_norm_g', 'odd_kv_norm_g', 'odd_w_uq', 'odd_w_ukv', 'odd_w_out', 'ln1_g', 'ln1_b', 'mlp_w1', 'mlp_w2', 'ln2_g', 'ln2_b', 'loss_target', 'm_even_w_in', 'm_even_sinks', 'm_even_w_out', 'm_odd_w_in', 'm_odd_q_norm_g', 'm_odd_kv_norm_g', 'm_odd_w_uq', 'm_odd_w_ukv', 'm_odd_w_out', 'm_ln1_g', 'm_ln1_b', 'm_mlp_w1', 'm_mlp_w2', 'm_ln2_g', 'm_ln2_b', 'v_even_w_in', 'v_even_sinks', 'v_even_w_out', 'v_odd_w_in', 'v_odd_q_norm_g', 'v_odd_kv_norm_g', 'v_odd_w_uq', 'v_odd_w_ukv', 'v_odd_w_out', 'v_ln1_g', 'v_ln1_b', 'v_mlp_w1', 'v_mlp_w2', 'v_ln2_g', 'v_ln2_b']
TWIN_OUTPUTS = ['loss', 'grad_x', 'grad_even_w_in', 'grad_even_sinks', 'grad_even_w_out', 'grad_odd_w_in', 'grad_odd_q_norm_g', 'grad_odd_kv_norm_g', 'grad_odd_w_uq', 'grad_odd_w_ukv', 'grad_odd_w_out', 'grad_ln1_g', 'grad_ln1_b', 'grad_mlp_w1', 'grad_mlp_w2', 'grad_ln2_g', 'grad_ln2_b', 'delta_even_w_in', 'delta_even_sinks', 'delta_even_w_out', 'delta_odd_w_in', 'delta_odd_q_norm_g', 'delta_odd_kv_norm_g', 'delta_odd_w_uq', 'delta_odd_w_ukv', 'delta_odd_w_out', 'delta_ln1_g', 'delta_ln1_b', 'delta_mlp_w1', 'delta_mlp_w2', 'delta_ln2_g', 'delta_ln2_b', 'new_m_even_w_in', 'new_m_even_sinks', 'new_m_even_w_out', 'new_m_odd_w_in', 'new_m_odd_q_norm_g', 'new_m_odd_kv_norm_g', 'new_m_odd_w_uq', 'new_m_odd_w_ukv', 'new_m_odd_w_out', 'new_m_ln1_g', 'new_m_ln1_b', 'new_m_mlp_w1', 'new_m_mlp_w2', 'new_m_ln2_g', 'new_m_ln2_b', 'new_v_even_w_in', 'new_v_even_sinks', 'new_v_even_w_out', 'new_v_odd_w_in', 'new_v_odd_q_norm_g', 'new_v_odd_kv_norm_g', 'new_v_odd_w_uq', 'new_v_odd_w_ukv', 'new_v_odd_w_out', 'new_v_ln1_g', 'new_v_ln1_b', 'new_v_mlp_w1', 'new_v_mlp_w2', 'new_v_ln2_g', 'new_v_ln2_b']
TWIN_LEAF_KINDS = {'loss': 'loss', 'grad_x': 'grad_x', 'grad_even_w_in': 'grad_w', 'grad_even_sinks': 'grad_w', 'grad_even_w_out': 'grad_w', 'grad_odd_w_in': 'grad_w', 'grad_odd_q_norm_g': 'grad_w', 'grad_odd_kv_norm_g': 'grad_w', 'grad_odd_w_uq': 'grad_w', 'grad_odd_w_ukv': 'grad_w', 'grad_odd_w_out': 'grad_w', 'grad_ln1_g': 'grad_w', 'grad_ln1_b': 'grad_w', 'grad_mlp_w1': 'grad_w', 'grad_mlp_w2': 'grad_w', 'grad_ln2_g': 'grad_w', 'grad_ln2_b': 'grad_w', 'delta_even_w_in': 'delta_w', 'delta_even_sinks': 'delta_w', 'delta_even_w_out': 'delta_w', 'delta_odd_w_in': 'delta_w', 'delta_odd_q_norm_g': 'delta_w', 'delta_odd_kv_norm_g': 'delta_w', 'delta_odd_w_uq': 'delta_w', 'delta_odd_w_ukv': 'delta_w', 'delta_odd_w_out': 'delta_w', 'delta_ln1_g': 'delta_w', 'delta_ln1_b': 'delta_w', 'delta_mlp_w1': 'delta_w', 'delta_mlp_w2': 'delta_w', 'delta_ln2_g': 'delta_w', 'delta_ln2_b': 'delta_w', 'new_m_even_w_in': 'new_m', 'new_m_even_sinks': 'new_m', 'new_m_even_w_out': 'new_m', 'new_m_odd_w_in': 'new_m', 'new_m_odd_q_norm_g': 'new_m', 'new_m_odd_kv_norm_g': 'new_m', 'new_m_odd_w_uq': 'new_m', 'new_m_odd_w_ukv': 'new_m', 'new_m_odd_w_out': 'new_m', 'new_m_ln1_g': 'new_m', 'new_m_ln1_b': 'new_m', 'new_m_mlp_w1': 'new_m', 'new_m_mlp_w2': 'new_m', 'new_m_ln2_g': 'new_m', 'new_m_ln2_b': 'new_m', 'new_v_even_w_in': 'new_v', 'new_v_even_sinks': 'new_v', 'new_v_even_w_out': 'new_v', 'new_v_odd_w_in': 'new_v', 'new_v_odd_q_norm_g': 'new_v', 'new_v_odd_kv_norm_g': 'new_v', 'new_v_odd_w_uq': 'new_v', 'new_v_odd_w_ukv': 'new_v', 'new_v_odd_w_out': 'new_v', 'new_v_ln1_g': 'new_v', 'new_v_ln1_b': 'new_v', 'new_v_mlp_w1': 'new_v', 'new_v_mlp_w2': 'new_v', 'new_v_ln2_g': 'new_v', 'new_v_ln2_b': 'new_v'}


def _forward(args):
    return _fwd_reference(*[args[k] for k in FWD_PARAMS])


def _output_shape():
    out = _jax.eval_shape(lambda: _forward(_fwd_setup_inputs(0)))
    return out.shape, out.dtype

N_MICROBATCH = 1
ADAM_LR = 0.001
ADAM_B1 = 0.9
ADAM_B2 = 0.999
ADAM_EPS = 1e-08
ADAM_WD = 0.01
ADAM_STEP = 10
PER_EXAMPLE_BATCH_AXIS = {'x': 0, 'loss_target': 0}
SHARED_INPUTS = []
_WEIGHT_DTYPES = {'even_w_in': _jnp.float32, 'even_sinks': _jnp.float32, 'even_w_out': _jnp.float32, 'odd_w_in': _jnp.float32, 'odd_q_norm_g': _jnp.float32, 'odd_kv_norm_g': _jnp.float32, 'odd_w_uq': _jnp.float32, 'odd_w_ukv': _jnp.float32, 'odd_w_out': _jnp.float32, 'ln1_g': _jnp.float32, 'ln1_b': _jnp.float32, 'mlp_w1': _jnp.float32, 'mlp_w2': _jnp.float32, 'ln2_g': _jnp.float32, 'ln2_b': _jnp.float32}
MOMENT_SCALE = {'even_w_in': 4.315109e-03, 'even_sinks': 6.593240e-03, 'even_w_out': 6.357589e-03, 'odd_w_in': 8.774871e-03, 'odd_q_norm_g': 2.918938e-03, 'odd_kv_norm_g': 7.739764e-03, 'odd_w_uq': 1.624627e-03, 'odd_w_ukv': 5.288069e-03, 'odd_w_out': 1.256639e-02, 'ln1_g': 2.197082e-01, 'ln1_b': 1.439603e-01, 'mlp_w1': 1.470205e-02, 'mlp_w2': 6.431784e-02, 'ln2_g': 5.683155e+00, 'ln2_b': 1.276570e+00}


def _to_microbatches(a, axis):
    t = _jnp.moveaxis(a, axis, 0)
    t = t.reshape((N_MICROBATCH, t.shape[0] // N_MICROBATCH) + t.shape[1:])
    return _jnp.moveaxis(t, 1, axis + 1)


def setup_inputs(seed: int = 0) -> dict:
    inp = _fwd_setup_inputs(seed)
    key = _jax.random.fold_in(_jax.random.key(seed), 7919)
    shape, _ = _output_shape()
    out = dict(inp)
    out["loss_target"] = _jax.random.normal(_jax.random.fold_in(key, 0), shape, _jnp.float32)
    for i, name in enumerate(TWIN_WEIGHTS):
        w = inp[name].astype(_jnp.float32)
        if MOMENT_SCALE is None:
            s = _jnp.sqrt(_jnp.mean(_jnp.square(w)) + 1e-30)
        else:
            s = MOMENT_SCALE[name]
        km, kv = _jax.random.split(_jax.random.fold_in(key, i + 1))
        out[name] = w
        out["m_" + name] = s * _jax.random.normal(km, w.shape, _jnp.float32)
        out["v_" + name] = (s * s) * _jax.random.uniform(kv, w.shape, _jnp.float32, 0.5, 1.5)
    if N_MICROBATCH > 1:
        for name, axis in PER_EXAMPLE_BATCH_AXIS.items():
            out[name] = _to_microbatches(out[name], axis)
    return {'x': out['x'], 'even_w_in': out['even_w_in'], 'even_sinks': out['even_sinks'], 'even_w_out': out['even_w_out'], 'odd_w_in': out['odd_w_in'], 'odd_q_norm_g': out['odd_q_norm_g'], 'odd_kv_norm_g': out['odd_kv_norm_g'], 'odd_w_uq': out['odd_w_uq'], 'odd_w_ukv': out['odd_w_ukv'], 'odd_w_out': out['odd_w_out'], 'ln1_g': out['ln1_g'], 'ln1_b': out['ln1_b'], 'mlp_w1': out['mlp_w1'], 'mlp_w2': out['mlp_w2'], 'ln2_g': out['ln2_g'], 'ln2_b': out['ln2_b'], 'loss_target': out['loss_target'], 'm_even_w_in': out['m_even_w_in'], 'm_even_sinks': out['m_even_sinks'], 'm_even_w_out': out['m_even_w_out'], 'm_odd_w_in': out['m_odd_w_in'], 'm_odd_q_norm_g': out['m_odd_q_norm_g'], 'm_odd_kv_norm_g': out['m_odd_kv_norm_g'], 'm_odd_w_uq': out['m_odd_w_uq'], 'm_odd_w_ukv': out['m_odd_w_ukv'], 'm_odd_w_out': out['m_odd_w_out'], 'm_ln1_g': out['m_ln1_g'], 'm_ln1_b': out['m_ln1_b'], 'm_mlp_w1': out['m_mlp_w1'], 'm_mlp_w2': out['m_mlp_w2'], 'm_ln2_g': out['m_ln2_g'], 'm_ln2_b': out['m_ln2_b'], 'v_even_w_in': out['v_even_w_in'], 'v_even_sinks': out['v_even_sinks'], 'v_even_w_out': out['v_even_w_out'], 'v_odd_w_in': out['v_odd_w_in'], 'v_odd_q_norm_g': out['v_odd_q_norm_g'], 'v_odd_kv_norm_g': out['v_odd_kv_norm_g'], 'v_odd_w_uq': out['v_odd_w_uq'], 'v_odd_w_ukv': out['v_odd_w_ukv'], 'v_odd_w_out': out['v_odd_w_out'], 'v_ln1_g': out['v_ln1_g'], 'v_ln1_b': out['v_ln1_b'], 'v_mlp_w1': out['v_mlp_w1'], 'v_mlp_w2': out['v_mlp_w2'], 'v_ln2_g': out['v_ln2_g'], 'v_ln2_b': out['v_ln2_b']}


def _loss(weights, diff, rest, loss_target):
    with _jax.named_scope("forward"):
        args = {**rest, TWIN_DIFF_INPUT: diff, **{k: w.astype(_WEIGHT_DTYPES[k]) for k, w in weights.items()}}
        y = _forward(args)
    with _jax.named_scope("loss_head"):
        err = _jnp.square(y.astype(_jnp.float32) - loss_target)
        return 0.5 * _jnp.sum(_jnp.mean(err, axis=-1)) if err.ndim else 0.5 * err


def _adamw(w, g, m, v):
    m = ADAM_B1 * m + (1.0 - ADAM_B1) * g
    v = ADAM_B2 * v + (1.0 - ADAM_B2) * _jnp.square(g)
    m_hat = m / (1.0 - ADAM_B1 ** ADAM_STEP)
    v_hat = v / (1.0 - ADAM_B2 ** ADAM_STEP)
    delta = -ADAM_LR * (m_hat / (_jnp.sqrt(v_hat) + ADAM_EPS) + ADAM_WD * w)
    return delta, m, v


def reference(x, even_w_in, even_sinks, even_w_out, odd_w_in, odd_q_norm_g, odd_kv_norm_g, odd_w_uq, odd_w_ukv, odd_w_out, ln1_g, ln1_b, mlp_w1, mlp_w2, ln2_g, ln2_b, loss_target, m_even_w_in, m_even_sinks, m_even_w_out, m_odd_w_in, m_odd_q_norm_g, m_odd_kv_norm_g, m_odd_w_uq, m_odd_w_ukv, m_odd_w_out, m_ln1_g, m_ln1_b, m_mlp_w1, m_mlp_w2, m_ln2_g, m_ln2_b, v_even_w_in, v_even_sinks, v_even_w_out, v_odd_w_in, v_odd_q_norm_g, v_odd_kv_norm_g, v_odd_w_uq, v_odd_w_ukv, v_odd_w_out, v_ln1_g, v_ln1_b, v_mlp_w1, v_mlp_w2, v_ln2_g, v_ln2_b):
    given = dict(x=x, even_w_in=even_w_in, even_sinks=even_sinks, even_w_out=even_w_out, odd_w_in=odd_w_in, odd_q_norm_g=odd_q_norm_g, odd_kv_norm_g=odd_kv_norm_g, odd_w_uq=odd_w_uq, odd_w_ukv=odd_w_ukv, odd_w_out=odd_w_out, ln1_g=ln1_g, ln1_b=ln1_b, mlp_w1=mlp_w1, mlp_w2=mlp_w2, ln2_g=ln2_g, ln2_b=ln2_b, loss_target=loss_target, m_even_w_in=m_even_w_in, m_even_sinks=m_even_sinks, m_even_w_out=m_even_w_out, m_odd_w_in=m_odd_w_in, m_odd_q_norm_g=m_odd_q_norm_g, m_odd_kv_norm_g=m_odd_kv_norm_g, m_odd_w_uq=m_odd_w_uq, m_odd_w_ukv=m_odd_w_ukv, m_odd_w_out=m_odd_w_out, m_ln1_g=m_ln1_g, m_ln1_b=m_ln1_b, m_mlp_w1=m_mlp_w1, m_mlp_w2=m_mlp_w2, m_ln2_g=m_ln2_g, m_ln2_b=m_ln2_b, v_even_w_in=v_even_w_in, v_even_sinks=v_even_sinks, v_even_w_out=v_even_w_out, v_odd_w_in=v_odd_w_in, v_odd_q_norm_g=v_odd_q_norm_g, v_odd_kv_norm_g=v_odd_kv_norm_g, v_odd_w_uq=v_odd_w_uq, v_odd_w_ukv=v_odd_w_ukv, v_odd_w_out=v_odd_w_out, v_ln1_g=v_ln1_g, v_ln1_b=v_ln1_b, v_mlp_w1=v_mlp_w1, v_mlp_w2=v_mlp_w2, v_ln2_g=v_ln2_g, v_ln2_b=v_ln2_b)
    weights = {n: given[n] for n in TWIN_WEIGHTS}
    shared = {n: given[n] for n in SHARED_INPUTS}
    per_example = {n: given[n] for n in ['x']}
    grad_fn = _jax.value_and_grad(_loss, argnums=(0, 1))

    def one_microbatch(ex, loss_target):
        ex = dict(ex)
        diff = ex.pop(TWIN_DIFF_INPUT)
        return grad_fn(weights, diff, {**shared, **ex}, loss_target)

    if N_MICROBATCH == 1:
        loss, (grad_w, grad_x) = one_microbatch(per_example, given["loss_target"])
    else:
        def body(carry, xs):
            loss_sum, grad_sum = carry
            l_k, (gw_k, gx_k) = one_microbatch(xs[0], xs[1])
            with _jax.named_scope("update"):
                return (loss_sum + l_k, _jax.tree.map(_jnp.add, grad_sum, gw_k)), gx_k

        init = (_jnp.zeros((), _jnp.float32), _jax.tree.map(_jnp.zeros_like, weights))
        (loss, grad_w), grad_x = _jax.lax.scan(body, init, (per_example, given["loss_target"]))
    with _jax.named_scope("update"):
        delta_w, new_m, new_v = {}, {}, {}
        for n in TWIN_WEIGHTS:
            delta_w[n], new_m[n], new_v[n] = _adamw(weights[n], grad_w[n], given["m_" + n], given["v_" + n])
    return (loss, grad_x, *[grad_w[n] for n in TWIN_WEIGHTS], *[delta_w[n] for n in TWIN_WEIGHTS],
            *[new_m[n] for n in TWIN_WEIGHTS], *[new_v[n] for n in TWIN_WEIGHTS])
```

```python
import functools
import math

import jax
import jax.numpy as jnp
from jax import lax
from jax.experimental import pallas as pl
from jax.experimental.pallas import tpu as pltpu

F32 = jnp.float32
BF16 = jnp.bfloat16
MESH = pl.DeviceIdType.MESH

D_MODEL = 2048
SEQ = 2048
DEPTH = 2
HEAD_DIM = 64
A_Q_HEADS, A_KV_HEADS, A_WINDOW = 16, 2, 128
B_HEADS = 8
B_PATTERNS = ((128, 1), (512, 4), (2048, 16))
C_HEADS = 16
D_HEADS, D_Q_RANK, D_KV_RANK, D_NOPE, D_ROPE, D_V = 16, 512, 256, 64, 32, 64
ROPE_BASE = 10000.0
D_FF = 4 * D_MODEL
LN_EPS = 1e-5
RMS_EPS = 1e-6
ALPHA = (2 * DEPTH) ** 0.25
A_Q_W = A_Q_HEADS * HEAD_DIM
A_KV_W = A_KV_HEADS * HEAD_DIM
B_W = B_HEADS * HEAD_DIM
EVEN_IN = A_Q_W + 2 * A_KV_W + 3 * B_W * len(B_PATTERNS)
EVEN_OUT = A_Q_W + B_W
C_W = C_HEADS * HEAD_DIM
ODD_IN = 3 * C_W + D_Q_RANK + D_KV_RANK + D_ROPE
ADAM_LR, ADAM_B1, ADAM_B2, ADAM_EPS, ADAM_WD, ADAM_STEP = 0.001, 0.9, 0.999, 1e-08, 0.01, 10

N_CHIPS = 4
EVEN_IN_PAD = 1536
ODD_IN_PAD = 1024
ATT_BLK = 128
NEG = -1e30
V7X_VMEM_LIMIT = 48 * 1024 * 1024


def _params(*sem):
    return pltpu.CompilerParams(dimension_semantics=sem, vmem_limit_bytes=V7X_VMEM_LIMIT)


def _tile(n, cap):
    if n <= cap:
        return n
    t = cap - cap % 128
    while n % t:
        t -= 128
    return t


def _matmul(a, b, *, mode, out_dtype, name, epilogue=None, extra=None, alpha=1.0, tm=512, tn=512, tk=512):
    if mode == "nn":
        M, K = a.shape
        P, _, Np = b.shape
        tm, tn, tk = _tile(M, tm), _tile(Np, tn), _tile(K, tk)
        nj = Np // tn
        grid = (M // tm, P * nj, K // tk)
        a_spec = pl.BlockSpec((tm, tk), lambda i, j, k: (i, k))
        b_spec = pl.BlockSpec((None, tk, tn), lambda i, j, k: (j // nj, k, j % nj))
        o_spec = pl.BlockSpec((tm, tn), lambda i, j, k: (i, j))
        out_shape = (M, P * Np)
        dims = (((1,), (0,)), ((), ()))
    elif mode == "nt":
        M, N = a.shape
        P, K, Np = b.shape
        tm, tn, tk = _tile(M, tm), _tile(K, tn), _tile(Np, tk)
        nkk = Np // tk
        grid = (M // tm, K // tn, P * nkk)
        a_spec = pl.BlockSpec((tm, tk), lambda i, j, k: (i, k))
        b_spec = pl.BlockSpec((None, tn, tk), lambda i, j, k: (k // nkk, j, k % nkk))
        o_spec = pl.BlockSpec((tm, tn), lambda i, j, k: (i, j))
        out_shape = (M, K)
        dims = (((1,), (1,)), ((), ()))
    else:
        raise ValueError(mode)
    n_k = grid[2]
    n_extra = 0 if extra is None else 1
    n_out = 2 if epilogue == "relu2" else 1

    def body(*refs):
        a_ref, b_ref = refs[:2]
        e_ref = refs[2] if n_extra else None
        outs = refs[2 + n_extra:2 + n_extra + n_out]
        acc = refs[-1]
        k = pl.program_id(2)

        @pl.when(k == 0)
        def _():
            acc[...] = jnp.zeros_like(acc)

        acc[...] += lax.dot_general(a_ref[...], b_ref[...], dims, preferred_element_type=F32)

        @pl.when(k == n_k - 1)
        def _():
            r = acc[...]
            if epilogue == "relu2":
                outs[0][...] = r.astype(outs[0].dtype)
                rp = jnp.maximum(r, 0.0)
                outs[1][...] = (rp * rp).astype(outs[1].dtype)
            elif epilogue == "drelu2":
                outs[0][...] = (r * (2.0 * jnp.maximum(e_ref[...].astype(F32), 0.0))).astype(outs[0].dtype)
            elif epilogue == "add":
                outs[0][...] = (r + alpha * e_ref[...].astype(F32)).astype(outs[0].dtype)
            else:
                outs[0][...] = r.astype(outs[0].dtype)

    in_specs = [a_spec, b_spec] + ([o_spec] if n_extra else [])
    shapes = [jax.ShapeDtypeStruct(out_shape, out_dtype)] * n_out
    res = pl.pallas_call(
        body, name=name, grid=grid, in_specs=in_specs,
        out_specs=[o_spec] * n_out, out_shape=shapes,
        scratch_shapes=[pltpu.VMEM((tm, tn), F32)],
        compiler_params=_params("parallel", "parallel", "arbitrary"),
    )(a, b, *([extra] if n_extra else []))
    return res if n_out > 1 else res[0]


def _matmul_tn(a, g, *, shards, name, tm=512, tn=512, tk=512):
    M, K = a.shape
    P = shards
    Np = g.shape[1] // P
    tm, tn, tk = _tile(K, tm), _tile(Np, tn), _tile(M, tk)
    nj = Np // tn
    grid = (K // tm, P * nj, M // tk)
    n_k = grid[2]

    def body(a_ref, g_ref, o_ref, acc):
        k = pl.program_id(2)

        @pl.when(k == 0)
        def _():
            acc[...] = jnp.zeros_like(acc)

        acc[...] += lax.dot_general(a_ref[...], g_ref[...], (((0,), (0,)), ((), ())), preferred_element_type=F32)

        @pl.when(k == n_k - 1)
        def _():
            o_ref[...] = acc[...]

    return pl.pallas_call(
        body, name=name, grid=grid,
        in_specs=[pl.BlockSpec((tk, tm), lambda i, j, k: (k, i)), pl.BlockSpec((tk, tn), lambda i, j, k: (k, j))],
        out_specs=pl.BlockSpec((None, tm, tn), lambda i, j, k: (j // nj, i, j % nj)),
        out_shape=jax.ShapeDtypeStruct((P, K, Np), F32),
        scratch_shapes=[pltpu.VMEM((tm, tn), F32)],
        compiler_params=_params("parallel", "parallel", "arbitrary"),
    )(a, g)


def _norm_fwd(x, res, g, b, *, center, eps, alpha, name, rows=256):
    S, W = x.shape
    has_res = res is not None
    rows = _tile(S, rows)

    def body(*refs):
        x_ref = refs[0]
        r_ref = refs[1] if has_res else None
        g_ref = refs[1 + has_res]
        b_ref = refs[2 + has_res] if center else None
        y_ref, y16_ref, u_ref = refs[-3:]
        u = x_ref[...]
        if has_res:
            u = alpha * u + r_ref[...]
        if center:
            mu = jnp.mean(u, axis=1, keepdims=True)
            xc = u - mu
        else:
            xc = u
        var = jnp.mean(xc * xc, axis=1, keepdims=True)
        y = xc * lax.rsqrt(var + eps) * g_ref[...]
        if center:
            y = y + b_ref[...]
        y_ref[...] = y
        y16_ref[...] = y.astype(BF16)
        u_ref[...] = u

    row_spec = pl.BlockSpec((rows, W), lambda i: (i, 0))
    vec_spec = pl.BlockSpec((1, W), lambda i: (0, 0))
    ins = [x] + ([res] if has_res else []) + [g.reshape(1, W)] + ([b.reshape(1, W)] if center else [])
    specs = [row_spec] + ([row_spec] if has_res else []) + [vec_spec] + ([vec_spec] if center else [])
    return pl.pallas_call(
        body, name=name, grid=(S // rows,), in_specs=specs,
        out_specs=[row_spec, row_spec, row_spec],
        out_shape=[jax.ShapeDtypeStruct((S, W), F32), jax.ShapeDtypeStruct((S, W), BF16), jax.ShapeDtypeStruct((S, W), F32)],
        compiler_params=_params("parallel"),
    )(*ins)


def _norm_bwd(dy, u, g, *, center, eps, name, rows=256):
    S, W = u.shape
    rows = _tile(S, rows)

    def body(dy_ref, u_ref, g_ref, du_ref, du16_ref, dg_ref, db_ref):
        i = pl.program_id(0)
        uu = u_ref[...]
        dyv = dy_ref[...]
        if center:
            xc = uu - jnp.mean(uu, axis=1, keepdims=True)
        else:
            xc = uu
        rstd = lax.rsqrt(jnp.mean(xc * xc, axis=1, keepdims=True) + eps)
        xhat = xc * rstd
        dxh = dyv * g_ref[...]
        c2 = jnp.mean(dxh * xhat, axis=1, keepdims=True)
        du = dxh - xhat * c2
        if center:
            du = du - jnp.mean(dxh, axis=1, keepdims=True)
        du = du * rstd
        du_ref[...] = du
        du16_ref[...] = du.astype(BF16)

        @pl.when(i == 0)
        def _():
            dg_ref[...] = jnp.zeros_like(dg_ref)
            db_ref[...] = jnp.zeros_like(db_ref)

        dg_ref[...] += jnp.sum(dyv * xhat, axis=0, keepdims=True)
        db_ref[...] += jnp.sum(dyv, axis=0, keepdims=True)

    row_spec = pl.BlockSpec((rows, W), lambda i: (i, 0))
    vec_spec = pl.BlockSpec((1, W), lambda i: (0, 0))
    return pl.pallas_call(
        body, name=name, grid=(S // rows,), in_specs=[row_spec, row_spec, vec_spec],
        out_specs=[row_spec, row_spec, vec_spec, vec_spec],
        out_shape=[jax.ShapeDtypeStruct((S, W), F32), jax.ShapeDtypeStruct((S, W), BF16),
                   jax.ShapeDtypeStruct((1, W), F32), jax.ShapeDtypeStruct((1, W), F32)],
        compiler_params=_params("arbitrary"),
    )(dy, u, g.reshape(1, W))


def _loss_head(y, target, *, name, rows=256):
    S, W = y.shape
    rows = _tile(S, rows)

    def body(y_ref, t_ref, dy_ref, l_ref):
        i = pl.program_id(0)
        e = y_ref[...] - t_ref[...]
        dy_ref[...] = e * (1.0 / W)

        @pl.when(i == 0)
        def _():
            l_ref[...] = jnp.zeros_like(l_ref)

        l_ref[...] += jnp.full(l_ref.shape, 0.5 / W * jnp.sum(e * e), F32)

    row_spec = pl.BlockSpec((rows, W), lambda i: (i, 0))
    return pl.pallas_call(
        body, name=name, grid=(S // rows,), in_specs=[row_spec, row_spec],
        out_specs=[row_spec, pl.BlockSpec((1, 128), lambda i: (0, 0))],
        out_shape=[jax.ShapeDtypeStruct((S, W), F32), jax.ShapeDtypeStruct((1, 128), F32)],
        compiler_params=_params("arbitrary"),
    )(y, target)


def _adamw(w, g, m, v, *, name, rows=256):
    R, C = w.shape
    rows = _tile(R, rows) if R % 8 == 0 else R
    c1 = 1.0 / (1.0 - ADAM_B1 ** ADAM_STEP)
    c2 = 1.0 / (1.0 - ADAM_B2 ** ADAM_STEP)

    def body(w_ref, g_ref, m_ref, v_ref, d_ref, mo_ref, vo_ref):
        gg = g_ref[...]
        mn = ADAM_B1 * m_ref[...] + (1.0 - ADAM_B1) * gg
        vn = ADAM_B2 * v_ref[...] + (1.0 - ADAM_B2) * (gg * gg)
        d_ref[...] = -ADAM_LR * ((mn * c1) / (jnp.sqrt(vn * c2) + ADAM_EPS) + ADAM_WD * w_ref[...])
        mo_ref[...] = mn
        vo_ref[...] = vn

    spec = pl.BlockSpec((rows, C), lambda i: (i, 0))
    shp = jax.ShapeDtypeStruct((R, C), F32)
    return pl.pallas_call(
        body, name=name, grid=(R // rows,), in_specs=[spec] * 4, out_specs=[spec] * 3, out_shape=[shp] * 3,
        compiler_params=_params("parallel"),
    )(w, g, m, v)


def _rope(x1, x2, cos, sin, *, name, rows=512):
    H, S, W = x1.shape
    rows = _tile(S, rows)

    def body(x1_ref, x2_ref, c_ref, s_ref, y1_ref, y2_ref):
        t1 = jnp.sum(x1_ref[...], axis=0)
        t2 = jnp.sum(x2_ref[...], axis=0)
        c = c_ref[...]
        s = s_ref[...]
        y1_ref[...] = t1 * c - t2 * s
        y2_ref[...] = t1 * s + t2 * c

    xs = pl.BlockSpec((H, rows, W), lambda i: (0, i, 0))
    ts = pl.BlockSpec((rows, W), lambda i: (i, 0))
    shp = jax.ShapeDtypeStruct((S, W), F32)
    return pl.pallas_call(
        body, name=name, grid=(S // rows,), in_specs=[xs, xs, ts, ts], out_specs=[ts, ts], out_shape=[shp, shp],
        compiler_params=_params("parallel"),
    )(x1, x2, cos, sin)


def _band_geometry(blk, nb_seq, n_prev):
    def geo(i):
        seq = i // nb_seq
        n = i % nb_seq
        return seq, n, jnp.maximum(n - n_prev, 0)
    return geo


def _attn_fwd(q, k, v, slopes, sinks, *, blk, nb_seq, n_prev, n_back, scale, dist_scale, name):
    Hq, T, dqk = q.shape
    Hk, _, dv = v.shape
    group = Hq // Hk
    use_bias, use_sink = slopes is not None, sinks is not None
    geo = _band_geometry(blk, nb_seq, n_prev)

    def body(*refs):
        q_ref, k_ref, v_ref = refs[:3]
        slope_ref = refs[3] if use_bias else None
        sink_ref = refs[3 + use_bias] if use_sink else None
        o_ref, lse_ref = refs[-2:]
        h = pl.program_id(0)
        seq, n, lo = geo(pl.program_id(1))
        qb = q_ref[...]
        diff = lax.broadcasted_iota(jnp.int32, (blk, blk), 0) - lax.broadcasted_iota(jnp.int32, (blk, blk), 1)
        if use_sink:
            m0 = jnp.full((blk, 1), sink_ref[h], F32)
            l0 = jnp.ones((blk, 1), F32)
        else:
            m0 = jnp.full((blk, 1), NEG, F32)
            l0 = jnp.zeros((blk, 1), F32)

        def step(j, carry):
            m, l, acc = carry
            start = pl.multiple_of((seq * nb_seq + j) * blk, blk)
            kb = k_ref[pl.ds(start, blk), :]
            vb = v_ref[pl.ds(start, blk), :]
            s = lax.dot_general(qb, kb, (((1,), (1,)), ((), ())), preferred_element_type=F32) * scale
            rel = (n - j) * blk + diff
            valid = (rel >= 0) & (rel <= n_back)
            if use_bias:
                s = s - (slope_ref[h] * dist_scale) * rel.astype(F32)
            s = jnp.where(valid, s, NEG)
            m_new = jnp.maximum(m, jnp.max(s, axis=1, keepdims=True))
            p = jnp.where(valid, jnp.exp(s - m_new), 0.0)
            a = jnp.exp(m - m_new)
            l = a * l + jnp.sum(p, axis=1, keepdims=True)
            acc = a * acc + jnp.dot(p.astype(BF16), vb, preferred_element_type=F32)
            return m_new, l, acc

        m, l, acc = lax.fori_loop(lo, n + 1, step, (m0, l0, jnp.zeros((blk, dv), F32)))
        o_ref[...] = acc / l
        lse_ref[...] = m + jnp.log(l)

    smem = pl.BlockSpec(memory_space=pltpu.SMEM)
    in_specs = [pl.BlockSpec((None, blk, dqk), lambda h, i: (h, i, 0)),
                pl.BlockSpec((None, T, dqk), lambda h, i: (h // group, 0, 0)),
                pl.BlockSpec((None, T, dv), lambda h, i: (h // group, 0, 0))]
    ins = [q, k, v]
    if use_bias:
        in_specs.append(smem)
        ins.append(slopes)
    if use_sink:
        in_specs.append(smem)
        ins.append(sinks)
    return pl.pallas_call(
        body, name=name, grid=(Hq, T // blk), in_specs=in_specs,
        out_specs=[pl.BlockSpec((None, blk, dv), lambda h, i: (h, i, 0)), pl.BlockSpec((None, blk, 1), lambda h, i: (h, i, 0))],
        out_shape=[jax.ShapeDtypeStruct((Hq, T, dv), F32), jax.ShapeDtypeStruct((Hq, T, 1), F32)],
        compiler_params=_params("parallel", "parallel"),
    )(*ins)


def _attn_bwd(q, k, v, o, do, lse, dlse, slopes, sinks, *, blk, nb_seq, n_prev, n_back, scale, dist_scale, name):
    Hq, T, dqk = q.shape
    Hk, _, dv = v.shape
    group = Hq // Hk
    use_bias, use_sink, use_dlse = slopes is not None, sinks is not None, dlse is not None
    geo = _band_geometry(blk, nb_seq, n_prev)

    def body(*refs):
        q_ref, k_ref, v_ref, o_ref, do_ref, lse_ref = refs[:6]
        pos = 6
        dlse_ref = refs[pos] if use_dlse else None
        pos += use_dlse
        slope_ref = refs[pos] if use_bias else None
        pos += use_bias
        sink_ref = refs[pos] if use_sink else None
        pos += use_sink
        dq_ref, dk_ref, dv_ref = refs[pos:pos + 3]
        dsink_ref = refs[pos + 3] if use_sink else None
        h = pl.program_id(0)
        i = pl.program_id(1)
        seq, n, lo = geo(i)
        qb = q_ref[...]
        dob = do_ref[...]
        do16 = dob.astype(BF16)
        lse = lse_ref[...]
        c = -jnp.sum(dob * o_ref[...], axis=1, keepdims=True)
        if use_dlse:
            c = c + dlse_ref[...]
        diff = lax.broadcasted_iota(jnp.int32, (blk, blk), 0) - lax.broadcasted_iota(jnp.int32, (blk, blk), 1)

        @pl.when((h % group == 0) & (i == 0))
        def _():
            dk_ref[...] = jnp.zeros_like(dk_ref)
            dv_ref[...] = jnp.zeros_like(dv_ref)

        def step(j, dq):
            start = pl.multiple_of((seq * nb_seq + j) * blk, blk)
            kb = k_ref[pl.ds(start, blk), :]
            vb = v_ref[pl.ds(start, blk), :]
            s = lax.dot_general(qb, kb, (((1,), (1,)), ((), ())), preferred_element_type=F32) * scale
            rel = (n - j) * blk + diff
            valid = (rel >= 0) & (rel <= n_back)
            if use_bias:
                s = s - (slope_ref[h] * dist_scale) * rel.astype(F32)
            p = jnp.where(valid, jnp.exp(jnp.where(valid, s, NEG) - lse), 0.0)
            dp = lax.dot_general(do16, vb, (((1,), (1,)), ((), ())), preferred_element_type=F32)
            ds16 = (p * (dp + c) * scale).astype(BF16)
            dk_ref[pl.ds(start, blk), :] += lax.dot_general(ds16, qb, (((0,), (0,)), ((), ())), preferred_element_type=F32)
            dv_ref[pl.ds(start, blk), :] += lax.dot_general(p.astype(BF16), do16, (((0,), (0,)), ((), ())), preferred_element_type=F32)
            return dq + jnp.dot(ds16, kb, preferred_element_type=F32)

        dq_ref[...] = lax.fori_loop(lo, n + 1, step, jnp.zeros((blk, dqk), F32))
        if use_sink:
            @pl.when(i == 0)
            def _():
                dsink_ref[...] = jnp.zeros_like(dsink_ref)

            dsink_ref[...] += jnp.full(dsink_ref.shape, jnp.sum(jnp.exp(sink_ref[h] - lse) * c), F32)

    smem = pl.BlockSpec(memory_space=pltpu.SMEM)
    qs = pl.BlockSpec((None, blk, dqk), lambda h, i: (h, i, 0))
    os_ = pl.BlockSpec((None, blk, dv), lambda h, i: (h, i, 0))
    ls = pl.BlockSpec((None, blk, 1), lambda h, i: (h, i, 0))
    ks = pl.BlockSpec((None, T, dqk), lambda h, i: (h // group, 0, 0))
    vs = pl.BlockSpec((None, T, dv), lambda h, i: (h // group, 0, 0))
    in_specs = [qs, ks, vs, os_, os_, ls]
    ins = [q, k, v, o, do, lse]
    if use_dlse:
        in_specs.append(ls)
        ins.append(dlse)
    if use_bias:
        in_specs.append(smem)
        ins.append(slopes)
    if use_sink:
        in_specs.append(smem)
        ins.append(sinks)
    out_specs = [qs, ks, vs]
    out_shape = [jax.ShapeDtypeStruct((Hq, T, dqk), F32), jax.ShapeDtypeStruct((Hk, T, dqk), F32),
                 jax.ShapeDtypeStruct((Hk, T, dv), F32)]
    if use_sink:
        out_specs.append(pl.BlockSpec((None, 1, 128), lambda h, i: (h, 0, 0)))
        out_shape.append(jax.ShapeDtypeStruct((Hq, 1, 128), F32))
    return pl.pallas_call(
        body, name=name, grid=(Hq, T // blk), in_specs=in_specs, out_specs=out_specs, out_shape=out_shape,
        compiler_params=_params("arbitrary", "arbitrary"),
    )(*ins)


def _tri_sum(x, upper):
    hi = x.astype(BF16)
    r = x - hi.astype(F32)
    mid = r.astype(BF16)
    lo = (r - mid.astype(F32)).astype(BF16)
    return (jnp.dot(hi, upper, preferred_element_type=F32) + jnp.dot(mid, upper, preferred_element_type=F32)
            + jnp.dot(lo, upper, preferred_element_type=F32))


def _stick_terms(qb, kb, n, j, blk, scale, diff):
    z = lax.dot_general(qb, kb, (((1,), (1,)), ((), ())), preferred_element_type=F32) * scale
    e = jnp.exp(-jnp.abs(z))
    l1p = jnp.log(1.0 + e)
    lb = jnp.minimum(z, 0.0) - l1p
    strict = ((n - j) * blk + diff) > 0
    lk = jnp.where(strict, lb - z, 0.0)
    return z, e, lb, lk, strict


def _stick_fwd(q, k, v, *, blk, scale, name):
    H, T, dh = q.shape

    def body(q_ref, k_ref, v_ref, o_ref, tot_ref):
        n = pl.program_id(1)
        qb = q_ref[...]
        r_i = lax.broadcasted_iota(jnp.int32, (blk, blk), 0)
        c_i = lax.broadcasted_iota(jnp.int32, (blk, blk), 1)
        diff = r_i - c_i
        upper = (r_i > c_i).astype(BF16)

        def step(t, carry):
            run, acc = carry
            j = n - t
            start = pl.multiple_of(j * blk, blk)
            kb = k_ref[pl.ds(start, blk), :]
            vb = v_ref[pl.ds(start, blk), :]
            _, _, lb, lk, strict = _stick_terms(qb, kb, n, j, blk, scale, diff)
            w = jnp.where(strict, jnp.exp(lb + run + _tri_sum(lk, upper)), 0.0)
            acc = acc + jnp.dot(w.astype(BF16), vb, preferred_element_type=F32)
            return run + jnp.sum(lk, axis=1, keepdims=True), acc

        run, acc = lax.fori_loop(0, n + 1, step, (jnp.zeros((blk, 1), F32), jnp.zeros((blk, dh), F32)))
        o_ref[...] = acc
        tot_ref[...] = run

    qs = pl.BlockSpec((None, blk, dh), lambda h, i: (h, i, 0))
    ks = pl.BlockSpec((None, T, dh), lambda h, i: (h, 0, 0))
    ts = pl.BlockSpec((None, blk, 1), lambda h, i: (h, i, 0))
    return pl.pallas_call(
        body, name=name, grid=(H, T // blk), in_specs=[qs, ks, ks], out_specs=[qs, ts],
        out_shape=[jax.ShapeDtypeStruct((H, T, dh), F32), jax.ShapeDtypeStruct((H, T, 1), F32)],
        compiler_params=_params("parallel", "parallel"),
    )(q, k, v)


def _stick_bwd(q, k, v, tot, do, *, blk, scale, name):
    H, T, dh = q.shape

    def body(q_ref, k_ref, v_ref, tot_ref, do_ref, dq_ref, dk_ref, dv_ref):
        n = pl.program_id(1)
        qb = q_ref[...]
        do16 = do_ref[...].astype(BF16)
        tot = tot_ref[...]
        r_i = lax.broadcasted_iota(jnp.int32, (blk, blk), 0)
        c_i = lax.broadcasted_iota(jnp.int32, (blk, blk), 1)
        diff = r_i - c_i
        upto = (r_i <= c_i).astype(BF16)
        below = (r_i < c_i).astype(BF16)

        @pl.when(n == 0)
        def _():
            dk_ref[...] = jnp.zeros_like(dk_ref)
            dv_ref[...] = jnp.zeros_like(dv_ref)

        def step(j, carry):
            run, grun, dq = carry
            start = pl.multiple_of(j * blk, blk)
            kb = k_ref[pl.ds(start, blk), :]
            vb = v_ref[pl.ds(start, blk), :]
            z, e, lb, lk, strict = _stick_terms(qb, kb, n, j, blk, scale, diff)
            w = jnp.where(strict, jnp.exp(lb + tot - (run + _tri_sum(lk, upto))), 0.0)
            dw = lax.dot_general(do16, vb, (((1,), (1,)), ((), ())), preferred_element_type=F32)
            g = w * dw
            before = grun + _tri_sum(g, below)
            inv = 1.0 / (1.0 + e)
            sig = jnp.where(z >= 0, inv, e * inv)
            dz16 = (jnp.where(strict, g * (1.0 - sig) - sig * before, 0.0) * scale).astype(BF16)
            dk_ref[pl.ds(start, blk), :] += lax.dot_general(dz16, qb, (((0,), (0,)), ((), ())), preferred_element_type=F32)
            dv_ref[pl.ds(start, blk), :] += lax.dot_general(w.astype(BF16), do16, (((0,), (0,)), ((), ())), preferred_element_type=F32)
            dq = dq + jnp.dot(dz16, kb, preferred_element_type=F32)
            return run + jnp.sum(lk, axis=1, keepdims=True), grun + jnp.sum(g, axis=1, keepdims=True), dq

        zero = jnp.zeros((blk, 1), F32)
        _, _, dq = lax.fori_loop(0, n + 1, step, (zero, zero, jnp.zeros((blk, dh), F32)))
        dq_ref[...] = dq

    qs = pl.BlockSpec((None, blk, dh), lambda h, i: (h, i, 0))
    ks = pl.BlockSpec((None, T, dh), lambda h, i: (h, 0, 0))
    ts = pl.BlockSpec((None, blk, 1), lambda h, i: (h, i, 0))
    shp = jax.ShapeDtypeStruct((H, T, dh), F32)
    return pl.pallas_call(
        body, name=name, grid=(H, T // blk), in_specs=[qs, ks, ks, ts, qs], out_specs=[qs, ks, ks],
        out_shape=[shp, shp, shp],
        compiler_params=_params("arbitrary", "arbitrary"),
    )(q, k, v, tot, do)


def _mix_fwd(outs, lses, *, name, rows=512):
    H, T, dh = outs[0].shape
    rows = _tile(T, rows)

    def body(o0, o1, o2, l0, l1, l2, y_ref):
        ls = [l0[...], l1[...], l2[...]]
        mx = jnp.maximum(jnp.maximum(ls[0], ls[1]), ls[2])
        es = [jnp.exp(x - mx) for x in ls]
        den = es[0] + es[1] + es[2]
        y_ref[...] = (es[0] * o0[...] + es[1] * o1[...] + es[2] * o2[...]) / den

    os_ = pl.BlockSpec((None, rows, dh), lambda h, i: (h, i, 0))
    ls_ = pl.BlockSpec((None, rows, 1), lambda h, i: (h, i, 0))
    return pl.pallas_call(
        body, name=name, grid=(H, T // rows), in_specs=[os_] * 3 + [ls_] * 3, out_specs=os_,
        out_shape=jax.ShapeDtypeStruct((H, T, dh), F32),
        compiler_params=_params("parallel", "parallel"),
    )(*outs, *lses)


def _mix_bwd(outs, lses, dy, *, name, rows=512):
    H, T, dh = outs[0].shape
    rows = _tile(T, rows)

    def body(o0, o1, o2, l0, l1, l2, dy_ref, d0, d1, d2, g0, g1, g2):
        ls = [l0[...], l1[...], l2[...]]
        mx = jnp.maximum(jnp.maximum(ls[0], ls[1]), ls[2])
        es = [jnp.exp(x - mx) for x in ls]
        den = es[0] + es[1] + es[2]
        mix = [e / den for e in es]
        dyv = dy_ref[...]
        dm = [jnp.sum(dyv * o[...], axis=1, keepdims=True) for o in (o0, o1, o2)]
        avg = mix[0] * dm[0] + mix[1] * dm[1] + mix[2] * dm[2]
        for d_ref, g_ref, w, t in zip((d0, d1, d2), (g0, g1, g2), mix, dm):
            d_ref[...] = w * dyv
            g_ref[...] = w * (t - avg)

    os_ = pl.BlockSpec((None, rows, dh), lambda h, i: (h, i, 0))
    ls_ = pl.BlockSpec((None, rows, 1), lambda h, i: (h, i, 0))
    so = jax.ShapeDtypeStruct((H, T, dh), F32)
    sl = jax.ShapeDtypeStruct((H, T, 1), F32)
    res = pl.pallas_call(
        body, name=name, grid=(H, T // rows), in_specs=[os_] * 3 + [ls_] * 3 + [os_], out_specs=[os_] * 3 + [ls_] * 3,
        out_shape=[so] * 3 + [sl] * 3,
        compiler_params=_params("parallel", "parallel"),
    )(*outs, *lses, dy)
    return res[:3], res[3:]


def _position():
    return lax.axis_index("x"), lax.axis_index("y"), lax.axis_index("c")


def _other_chips(x, y):
    return [(1 - x, y), (x, 1 - y), (1 - x, 1 - y)]


def _gather_weights(shards, *, name):
    n = len(shards)

    def body(*refs):
        ins, outs = refs[:n], refs[n:2 * n]
        send_sems, recv_sems, local_sems = refs[2 * n:]
        x, y, c = _position()
        me = 2 * x + y
        chips = _other_chips(x, y)

        def remote(w, k, chip, core, to, src=None):
            half = ins[w].shape[0] // 2
            slab = outs[w].at[2 * chip[0] + chip[1], pl.ds(core * half, half), :]
            return pltpu.make_async_remote_copy(
                src_ref=slab if src is None else src, dst_ref=slab,
                send_sem=send_sems.at[w, k], recv_sem=recv_sems.at[w, k], device_id=to, device_id_type=MESH)

        local, sends = [], []
        for w in range(n):
            half = ins[w].shape[0] // 2
            local.append(pltpu.make_async_copy(ins[w], outs[w].at[me], local_sems.at[w]))
            local[-1].start()
            for j, chip in enumerate(chips):
                sends.append(remote(w, j, (x, y), c, (chip[0], chip[1], c), src=ins[w].at[pl.ds(c * half, half), :]))
                sends[-1].start()
        for w in range(n):
            for j, chip in enumerate(chips):
                remote(w, j, chip, c, (x, y, c)).wait_recv()
                sends.append(remote(w, 3 + j, chip, c, (x, y, 1 - c)))
                sends[-1].start()
        for w in range(n):
            for j, chip in enumerate(chips):
                remote(w, 3 + j, chip, 1 - c, (x, y, c)).wait_recv()
        for cp in sends:
            cp.wait_send()
        for cp in local:
            cp.wait()

    hbm = pl.BlockSpec(memory_space=pl.ANY)
    return pl.pallas_call(
        body, name=name, in_specs=[hbm] * n, out_specs=[hbm] * n,
        out_shape=[jax.ShapeDtypeStruct((N_CHIPS,) + s.shape, s.dtype) for s in shards],
        scratch_shapes=[pltpu.SemaphoreType.DMA((n, 6)), pltpu.SemaphoreType.DMA((n, 6)), pltpu.SemaphoreType.DMA((n,))],
    )(*shards)


def _sibling_swap(arrs, *, name):
    n = len(arrs)

    def body(*refs):
        ins, outs = refs[:n], refs[n:2 * n]
        send_sems, recv_sems = refs[2 * n:]
        x, y, c = _position()
        cps = [pltpu.make_async_remote_copy(src_ref=ins[w], dst_ref=outs[w], send_sem=send_sems.at[w], recv_sem=recv_sems.at[w],
                                            device_id=(x, y, 1 - c), device_id_type=MESH) for w in range(n)]
        for cp in cps:
            cp.start()
        for cp in cps:
            cp.wait()

    hbm = pl.BlockSpec(memory_space=pl.ANY)
    return pl.pallas_call(
        body, name=name, in_specs=[hbm] * n, out_specs=[hbm] * n,
        out_shape=[jax.ShapeDtypeStruct(a.shape, a.dtype) for a in arrs],
        scratch_shapes=[pltpu.SemaphoreType.DMA((n,)), pltpu.SemaphoreType.DMA((n,))],
    )(*arrs)


def _share_halves(arrs, *, name):
    n = len(arrs)

    def body(*refs):
        ins, outs = refs[:n], refs[n:2 * n]
        send_sems, recv_sems, local_sems = refs[2 * n:]
        x, y, c = _position()
        local, sends = [], []
        for w in range(n):
            local.append(pltpu.make_async_copy(ins[w], outs[w].at[c], local_sems.at[w]))
            local[-1].start()
            sends.append(pltpu.make_async_remote_copy(src_ref=ins[w], dst_ref=outs[w].at[c], send_sem=send_sems.at[w],
                                                      recv_sem=recv_sems.at[w], device_id=(x, y, 1 - c), device_id_type=MESH))
            sends[-1].start()
        for w in range(n):
            pltpu.make_async_remote_copy(src_ref=ins[w], dst_ref=outs[w].at[1 - c], send_sem=send_sems.at[w],
                                         recv_sem=recv_sems.at[w], device_id=(x, y, 1 - c), device_id_type=MESH).wait_recv()
        for cp in sends:
            cp.wait_send()
        for cp in local:
            cp.wait()

    hbm = pl.BlockSpec(memory_space=pl.ANY)
    return pl.pallas_call(
        body, name=name, in_specs=[hbm] * n, out_specs=[hbm] * n,
        out_shape=[jax.ShapeDtypeStruct((2,) + a.shape, a.dtype) for a in arrs],
        scratch_shapes=[pltpu.SemaphoreType.DMA((n,)), pltpu.SemaphoreType.DMA((n,)), pltpu.SemaphoreType.DMA((n,))],
    )(*arrs)


def _chip_all_to_all(arrs, *, name):
    n = len(arrs)

    def body(*refs):
        ins, outs = refs[:n], refs[n:2 * n]
        send_sems, recv_sems, local_sems = refs[2 * n:]
        x, y, c = _position()
        me = 2 * x + y
        chips = _other_chips(x, y)
        local, sends = [], []
        for w in range(n):
            local.append(pltpu.make_async_copy(ins[w].at[me], outs[w].at[me], local_sems.at[w]))
            local[-1].start()
            for j, chip in enumerate(chips):
                sends.append(pltpu.make_async_remote_copy(
                    src_ref=ins[w].at[2 * chip[0] + chip[1]], dst_ref=outs[w].at[me], send_sem=send_sems.at[w, j],
                    recv_sem=recv_sems.at[w, j], device_id=(chip[0], chip[1], c), device_id_type=MESH))
                sends[-1].start()
        for w in range(n):
            for j, chip in enumerate(chips):
                slab = outs[w].at[2 * chip[0] + chip[1]]
                pltpu.make_async_remote_copy(src_ref=slab, dst_ref=slab, send_sem=send_sems.at[w, j], recv_sem=recv_sems.at[w, j],
                                             device_id=(chip[0], chip[1], c), device_id_type=MESH).wait_recv()
        for cp in sends:
            cp.wait_send()
        for cp in local:
            cp.wait()

    hbm = pl.BlockSpec(memory_space=pl.ANY)
    return pl.pallas_call(
        body, name=name, in_specs=[hbm] * n, out_specs=[hbm] * n,
        out_shape=[jax.ShapeDtypeStruct(a.shape, a.dtype) for a in arrs],
        scratch_shapes=[pltpu.SemaphoreType.DMA((n, 3)), pltpu.SemaphoreType.DMA((n, 3)), pltpu.SemaphoreType.DMA((n,))],
    )(*arrs)


def _all_gather8(v, *, name):
    m, n = v.shape

    def body(v_ref, out_ref, send_sems, recv_sems, local_sem):
        x, y, c = _position()
        me, sibling = (x, y, c), (x, y, 1 - c)
        chips = _other_chips(x, y)

        def rows(px, py, pc):
            return out_ref.at[pl.ds((4 * px + 2 * py + pc) * m, m), :]

        def copy(k, block, to, src=None):
            return pltpu.make_async_remote_copy(src_ref=rows(*block) if src is None else src, dst_ref=rows(*block),
                                                send_sem=send_sems.at[k], recv_sem=recv_sems.at[k], device_id=to, device_id_type=MESH)

        mine = pltpu.make_async_copy(v_ref, rows(*me), local_sem)
        mine.start()
        first = [copy(0, me, sibling, src=v_ref)]
        first += [copy(1 + j, me, (*chip, c), src=v_ref) for j, chip in enumerate(chips)]
        for cp in first:
            cp.start()
        passed = [copy(4 + j, (*chip, c), sibling) for j, chip in enumerate(chips)]
        for j, chip in enumerate(chips):
            copy(1 + j, (*chip, c), me).wait_recv()
            passed[j].start()
        copy(0, sibling, me).wait_recv()
        for j, chip in enumerate(chips):
            copy(4 + j, (*chip, 1 - c), me).wait_recv()
        for cp in first + passed:
            cp.wait_send()
        mine.wait()

    vmem = pl.BlockSpec(memory_space=pltpu.VMEM)
    return pl.pallas_call(
        body, name=name, in_specs=[vmem], out_specs=vmem, out_shape=jax.ShapeDtypeStruct((8 * m, n), v.dtype),
        scratch_shapes=[pltpu.SemaphoreType.DMA((7,)), pltpu.SemaphoreType.DMA((7,)), pltpu.SemaphoreType.DMA],
    )(v)


def _add_to_bf16(a, b, *, name, rows=256):
    P, R, C = a.shape
    rows = _tile(R, rows)

    def body(a_ref, b_ref, o_ref):
        o_ref[...] = (a_ref[...] + b_ref[...].astype(F32)).astype(BF16)

    spec = pl.BlockSpec((None, rows, C), lambda p, i: (p, i, 0))
    return pl.pallas_call(
        body, name=name, grid=(P, R // rows), in_specs=[spec, spec], out_specs=spec,
        out_shape=jax.ShapeDtypeStruct((P, R, C), BF16), compiler_params=_params("parallel", "parallel"),
    )(a, b)


def _sum_slabs(a, *, name, rows=256):
    P, R, C = a.shape
    rows = _tile(R, rows)

    def body(a_ref, o_ref):
        acc = a_ref[0].astype(F32)
        for p in range(1, P):
            acc = acc + a_ref[p].astype(F32)
        o_ref[...] = acc

    return pl.pallas_call(
        body, name=name, grid=(R // rows,), in_specs=[pl.BlockSpec((P, rows, C), lambda i: (0, i, 0))],
        out_specs=pl.BlockSpec((rows, C), lambda i: (i, 0)),
        out_shape=jax.ShapeDtypeStruct((R, C), F32), compiler_params=_params("parallel"),
    )(a)


def _reduce_scatter(grads, *, name):
    c = lax.axis_index("c")
    halves = [g.shape[1] // 2 for g in grads]
    other = [lax.dynamic_slice_in_dim(g, (1 - c) * h, h, axis=1).astype(BF16) for g, h in zip(grads, halves)]
    got = _sibling_swap(other, name=name + "_swap")
    chip_sums = [_add_to_bf16(lax.dynamic_slice_in_dim(g, c * h, h, axis=1), r, name=f"{name}_add{w}")
                 for w, (g, h, r) in enumerate(zip(grads, halves, got))]
    landed = _chip_all_to_all(chip_sums, name=name + "_a2a")
    reduced = [_sum_slabs(a, name=f"{name}_sum{w}") for w, a in enumerate(landed)]
    both = _share_halves(reduced, name=name + "_share")
    return [b.reshape(2 * b.shape[1], b.shape[2]) for b in both]


def _to_heads(t, heads, dil=1):
    S = t.shape[0]
    d = t.shape[1] // heads
    return jnp.transpose(t.reshape(S // dil, dil, heads, d), (2, 1, 0, 3)).reshape(heads, S, d)


def _from_heads(t, dil=1):
    heads, S, d = t.shape
    return jnp.transpose(t.reshape(heads, dil, S // dil, d), (2, 1, 0, 3)).reshape(S, heads * d)


def _unstride(t, dil):
    heads, S, d = t.shape
    return jnp.transpose(t.reshape(heads, dil, S // dil, d), (0, 2, 1, 3)).reshape(heads, S, d)


def _restride(t, dil):
    heads, S, d = t.shape
    return jnp.transpose(t.reshape(heads, S // dil, dil, d), (0, 2, 1, 3)).reshape(heads, S, d)


def _pad_cols(t, width, padded):
    S = t.shape[0]
    return jnp.pad(t.reshape(S, N_CHIPS, width), ((0, 0), (0, 0), (0, padded - width))).reshape(S, N_CHIPS * padded)


def _unpad_cols(t, width, padded):
    S = t.shape[0]
    return t.reshape(S, N_CHIPS, padded)[:, :, :width].reshape(S, N_CHIPS * width)


def _alibi(n):
    return 2.0 ** (-8.0 * jnp.arange(1, n + 1, dtype=F32) / n)


B_KW = [dict(blk=ATT_BLK, nb_seq=SEQ // d // ATT_BLK, n_prev=1, n_back=w // d, scale=1.0 / math.sqrt(HEAD_DIM), dist_scale=d)
        for w, d in B_PATTERNS]
A_KW = dict(blk=ATT_BLK, nb_seq=SEQ // ATT_BLK, n_prev=1, n_back=A_WINDOW - 1, scale=1.0 / math.sqrt(HEAD_DIM), dist_scale=1)
D_BLK = 256
D_KW = dict(blk=D_BLK, nb_seq=SEQ // D_BLK, n_prev=SEQ // D_BLK, n_back=SEQ, scale=1.0 / math.sqrt(D_NOPE + D_ROPE), dist_scale=1)
C_KW = dict(blk=ATT_BLK, scale=1.0 / math.sqrt(HEAD_DIM))


def _rope_tables(reps):
    inv_freq = ROPE_BASE ** (-jnp.arange(0, D_ROPE, 2, dtype=F32) / D_ROPE)
    ang = jnp.arange(SEQ, dtype=F32)[:, None] * inv_freq[None, :]
    return jnp.tile(jnp.cos(ang), (1, reps)), jnp.tile(jnp.sin(ang), (1, reps))


def _mlp_fwd(x, x16, w1, w2, g, b, tag):
    hid, act = _matmul(x16, w1, mode="nn", out_dtype=BF16, epilogue="relu2", name=f"mlp{tag}_up")
    m = _matmul(act, w2, mode="nn", out_dtype=F32, name=f"mlp{tag}_down")
    y, y16, u = _norm_fwd(x, m, g, b, center=True, eps=LN_EPS, alpha=ALPHA, name=f"ln2_{tag}")
    return y, y16, (x16, hid, act, u)


def _mlp_bwd(dy, saved, w1, w2, g, tag):
    x16, hid, act, u = saved
    du, du16, dg, db = _norm_bwd(dy, u, g, center=True, eps=LN_EPS, name=f"ln2_{tag}_bwd")
    dw2 = _matmul_tn(act, du16, shards=1, name=f"mlp{tag}_dw2")
    dhid = _matmul(du16, w2, mode="nt", out_dtype=BF16, epilogue="drelu2", extra=hid, name=f"mlp{tag}_dact")
    dw1 = _matmul_tn(x16, dhid, shards=N_CHIPS, name=f"mlp{tag}_dw1")
    dx = _matmul(dhid, w1, mode="nt", out_dtype=F32, epilogue="add", extra=du, alpha=ALPHA, name=f"mlp{tag}_dx")
    return dx, dw1, dw2, dg, db


def _even_fwd(x16, w_in, sinks, w_out):
    S = x16.shape[0]
    h = _unpad_cols(_matmul(x16, w_in, mode="nn", out_dtype=BF16, name="even_in"), EVEN_IN // N_CHIPS, EVEN_IN_PAD)
    qa = _to_heads(h[:, :A_Q_W], A_Q_HEADS)
    ka = _to_heads(h[:, A_Q_W:A_Q_W + A_KV_W], A_KV_HEADS)
    va = _to_heads(h[:, A_Q_W + A_KV_W:A_Q_W + 2 * A_KV_W], A_KV_HEADS)
    slopes_a, slopes_b = _alibi(A_Q_HEADS), _alibi(B_HEADS)
    oa, lse_a = _attn_fwd(qa, ka, va, slopes_a, sinks, name="swa_fwd", **A_KW)
    base = A_Q_W + 2 * A_KV_W
    qkv_b, outs, lses = [], [], []
    for gi, (_, dil) in enumerate(B_PATTERNS):
        cols = h[:, base + gi * 3 * B_W:base + (gi + 1) * 3 * B_W]
        q, k, v = (_to_heads(cols[:, i * B_W:(i + 1) * B_W], B_HEADS, dil) for i in range(3))
        o, lse = _attn_fwd(q, k, v, slopes_b, None, name=f"dil{gi}_fwd", **B_KW[gi])
        qkv_b.append((q, k, v, o, lse))
        outs.append(_unstride(o, dil))
        lses.append(_unstride(lse, dil))
    ob = _mix_fwd(outs, lses, name="dil_mix")
    y16 = jnp.concatenate([_from_heads(oa), _from_heads(ob)], axis=-1).astype(BF16)
    mixed = _matmul(y16, w_out, mode="nn", out_dtype=F32, name="even_out")
    return mixed, (x16, qa, ka, va, oa, lse_a, qkv_b, outs, lses, y16)


def _even_bwd(dmix16, du, saved, w_in, sinks, w_out):
    x16, qa, ka, va, oa, lse_a, qkv_b, outs, lses, y16 = saved
    slopes_a, slopes_b = _alibi(A_Q_HEADS), _alibi(B_HEADS)
    dw_out = _matmul_tn(y16, dmix16, shards=N_CHIPS, name="even_dwout")
    dy = _matmul(dmix16, w_out, mode="nt", out_dtype=F32, name="even_dy")
    doa = _to_heads(dy[:, :A_Q_W], A_Q_HEADS)
    dob = _to_heads(dy[:, A_Q_W:], B_HEADS)
    dqa, dka, dva, dsink = _attn_bwd(qa, ka, va, oa, doa, lse_a, None, slopes_a, sinks, name="swa_bwd", **A_KW)
    douts, dlses = _mix_bwd(outs, lses, dob, name="dil_mix_bwd")
    parts = [_from_heads(dqa), _from_heads(dka), _from_heads(dva)]
    for gi, (_, dil) in enumerate(B_PATTERNS):
        q, k, v, o, lse = qkv_b[gi]
        dq, dk, dv = _attn_bwd(q, k, v, o, _restride(douts[gi], dil), lse, _restride(dlses[gi], dil), slopes_b, None,
                               name=f"dil{gi}_bwd", **B_KW[gi])
        parts += [_from_heads(dq, dil), _from_heads(dk, dil), _from_heads(dv, dil)]
    dh16 = _pad_cols(jnp.concatenate(parts, axis=-1), EVEN_IN // N_CHIPS, EVEN_IN_PAD).astype(BF16)
    dw_in = _matmul_tn(x16, dh16, shards=N_CHIPS, name="even_dwin")
    dx = _matmul(dh16, w_in, mode="nt", out_dtype=F32, epilogue="add", extra=du, alpha=ALPHA, name="even_dx")
    return dx, dw_in, dsink[:, 0, 0], dw_out


def _odd_fwd(x16, w_in, qn_g, kvn_g, w_uq, w_ukv, w_out):
    S = x16.shape[0]
    h = _unpad_cols(_matmul(x16, w_in, mode="nn", out_dtype=F32, name="odd_in"), ODD_IN // N_CHIPS, ODD_IN_PAD)
    qc, kc, vc = (_to_heads(h[:, i * C_W:(i + 1) * C_W].astype(BF16), C_HEADS) for i in range(3))
    cq = h[:, 3 * C_W:3 * C_W + D_Q_RANK]
    ckv = h[:, 3 * C_W + D_Q_RANK:3 * C_W + D_Q_RANK + D_KV_RANK]
    kr = h[:, 3 * C_W + D_Q_RANK + D_KV_RANK:]
    oc, tot = _stick_fwd(qc, kc, vc, name="stick_fwd", **C_KW)
    _, cqn16, _ = _norm_fwd(cq, None, qn_g, None, center=False, eps=RMS_EPS, alpha=1.0, name="q_rms")
    _, ckvn16, _ = _norm_fwd(ckv, None, kvn_g, None, center=False, eps=RMS_EPS, alpha=1.0, name="kv_rms")
    q = _matmul(cqn16, w_uq, mode="nn", out_dtype=F32, name="mla_uq").reshape(S, D_HEADS, D_NOPE + D_ROPE)
    kv = _matmul(ckvn16, w_ukv, mode="nn", out_dtype=F32, name="mla_ukv").reshape(S, D_HEADS, D_NOPE + D_V)
    half = D_ROPE // 2
    q_rope = q[..., D_NOPE:]
    x1 = jnp.concatenate([q_rope[..., :half].reshape(S, D_HEADS * half), kr[:, :half]], axis=-1)
    x2 = jnp.concatenate([q_rope[..., half:].reshape(S, D_HEADS * half), kr[:, half:]], axis=-1)
    cos, sin = _rope_tables(D_HEADS + 1)
    y1, y2 = _rope(x1[None], x2[None], cos, sin, name="rope_fwd")
    nq = D_HEADS * half
    q_rot = jnp.concatenate([y1[:, :nq].reshape(S, D_HEADS, half), y2[:, :nq].reshape(S, D_HEADS, half)], axis=-1)
    kr_rot = jnp.concatenate([y1[:, nq:], y2[:, nq:]], axis=-1)
    qd = jnp.transpose(jnp.concatenate([q[..., :D_NOPE], q_rot], axis=-1), (1, 0, 2)).astype(BF16)
    kd = jnp.transpose(jnp.concatenate([kv[..., :D_NOPE], jnp.broadcast_to(kr_rot[:, None, :], (S, D_HEADS, D_ROPE))], axis=-1),
                       (1, 0, 2)).astype(BF16)
    vd = jnp.transpose(kv[..., D_NOPE:], (1, 0, 2)).astype(BF16)
    od, lse_d = _attn_fwd(qd, kd, vd, None, None, name="mla_fwd", **D_KW)
    y16 = jnp.concatenate([_from_heads(oc), _from_heads(od)], axis=-1).astype(BF16)
    mixed = _matmul(y16, w_out, mode="nn", out_dtype=F32, name="odd_out")
    return mixed, (x16, qc, kc, vc, tot, cq, ckv, cqn16, ckvn16, qd, kd, vd, od, lse_d, y16)


def _odd_bwd(dmix16, du, saved, w_in, qn_g, kvn_g, w_uq, w_ukv, w_out):
    x16, qc, kc, vc, tot, cq, ckv, cqn16, ckvn16, qd, kd, vd, od, lse_d, y16 = saved
    S = x16.shape[0]
    half = D_ROPE // 2
    dw_out = _matmul_tn(y16, dmix16, shards=1, name="odd_dwout")
    dy = _matmul(dmix16, w_out, mode="nt", out_dtype=F32, name="odd_dy")
    doc = _to_heads(dy[:, :C_W], C_HEADS)
    dod = _to_heads(dy[:, C_W:], D_HEADS)
    dqc, dkc, dvc = _stick_bwd(qc, kc, vc, tot, doc, name="stick_bwd", **C_KW)
    dqd, dkd, dvd = _attn_bwd(qd, kd, vd, od, dod, lse_d, None, None, None, name="mla_bwd", **D_KW)
    cos, sin = _rope_tables(D_HEADS + 1)
    nq = D_HEADS * half
    gq1 = jnp.transpose(dqd[..., D_NOPE:D_NOPE + half], (1, 0, 2)).reshape(1, S, nq)
    gq2 = jnp.transpose(dqd[..., D_NOPE + half:], (1, 0, 2)).reshape(1, S, nq)
    dq1, dq2 = _rope(gq1, gq2, cos[:, :nq], -sin[:, :nq], name="rope_bwd_q")
    dk1, dk2 = _rope(dkd[..., D_NOPE:D_NOPE + half], dkd[..., D_NOPE + half:], cos[:, nq:], -sin[:, nq:], name="rope_bwd_k")
    dq_full = jnp.concatenate([jnp.transpose(dqd[..., :D_NOPE], (1, 0, 2)), dq1.reshape(S, D_HEADS, half),
                               dq2.reshape(S, D_HEADS, half)], axis=-1).reshape(S, D_HEADS * (D_NOPE + D_ROPE)).astype(BF16)
    dkv_full = jnp.concatenate([jnp.transpose(dkd[..., :D_NOPE], (1, 0, 2)), jnp.transpose(dvd, (1, 0, 2))],
                               axis=-1).reshape(S, D_HEADS * (D_NOPE + D_V)).astype(BF16)
    dw_uq = _matmul_tn(cqn16, dq_full, shards=N_CHIPS, name="mla_dwuq")
    dw_ukv = _matmul_tn(ckvn16, dkv_full, shards=N_CHIPS, name="mla_dwukv")
    dcqn = _matmul(dq_full, w_uq, mode="nt", out_dtype=F32, name="mla_dcq")
    dckvn = _matmul(dkv_full, w_ukv, mode="nt", out_dtype=F32, name="mla_dckv")
    dcq, _, dqn_g, _ = _norm_bwd(dcqn, cq, qn_g, center=False, eps=RMS_EPS, name="q_rms_bwd")
    dckv, _, dkvn_g, _ = _norm_bwd(dckvn, ckv, kvn_g, center=False, eps=RMS_EPS, name="kv_rms_bwd")
    dh = jnp.concatenate([_from_heads(dqc), _from_heads(dkc), _from_heads(dvc), dcq, dckv, dk1, dk2], axis=-1)
    dh16 = _pad_cols(dh, ODD_IN // N_CHIPS, ODD_IN_PAD).astype(BF16)
    dw_in = _matmul_tn(x16, dh16, shards=N_CHIPS, name="odd_dwin")
    dx = _matmul(dh16, w_in, mode="nt", out_dtype=F32, epilogue="add", extra=du, alpha=ALPHA, name="odd_dx")
    return dx, dw_in, dqn_g[0], dkvn_g[0], dw_uq, dw_ukv, dw_out


def _local_step(x, target, w, small):
    x16 = x.astype(BF16)
    mixed0, sv_e = _even_fwd(x16, w["even_w_in"], small["even_sinks"], w["even_w_out"])
    x1, x1_16, u01 = _norm_fwd(x, mixed0, small["ln1_g"][0], small["ln1_b"][0], center=True, eps=LN_EPS, alpha=ALPHA, name="ln1_0")
    x2, x2_16, sv_m0 = _mlp_fwd(x1, x1_16, w["mlp_w1"][0], w["mlp_w2"][0], small["ln2_g"][0], small["ln2_b"][0], 0)
    mixed1, sv_o = _odd_fwd(x2_16, w["odd_w_in"], small["odd_q_norm_g"], small["odd_kv_norm_g"], w["odd_w_uq"], w["odd_w_ukv"],
                            w["odd_w_out"])
    x3, x3_16, u11 = _norm_fwd(x2, mixed1, small["ln1_g"][1], small["ln1_b"][1], center=True, eps=LN_EPS, alpha=ALPHA, name="ln1_1")
    x4, _, sv_m1 = _mlp_fwd(x3, x3_16, w["mlp_w1"][1], w["mlp_w2"][1], small["ln2_g"][1], small["ln2_b"][1], 1)
    dy, loss = _loss_head(x4, target, name="loss_head")

    dx3, dw1_1, dw2_1, dln2g_1, dln2b_1 = _mlp_bwd(dy, sv_m1, w["mlp_w1"][1], w["mlp_w2"][1], small["ln2_g"][1], 1)
    du, du16, dln1g_1, dln1b_1 = _norm_bwd(dx3, u11, small["ln1_g"][1], center=True, eps=LN_EPS, name="ln1_1_bwd")
    dx2, dwin_o, dqn_g, dkvn_g, dwuq, dwukv, dwout_o = _odd_bwd(
        du16, du, sv_o, w["odd_w_in"], small["odd_q_norm_g"], small["odd_kv_norm_g"], w["odd_w_uq"], w["odd_w_ukv"], w["odd_w_out"])
    dx1, dw1_0, dw2_0, dln2g_0, dln2b_0 = _mlp_bwd(dx2, sv_m0, w["mlp_w1"][0], w["mlp_w2"][0], small["ln2_g"][0], 0)
    du, du16, dln1g_0, dln1b_0 = _norm_bwd(dx1, u01, small["ln1_g"][0], center=True, eps=LN_EPS, name="ln1_0_bwd")
    dx0, dwin_e, dsinks, dwout_e = _even_bwd(du16, du, sv_e, w["even_w_in"], small["even_sinks"], w["even_w_out"])

    big = dict(even_w_in=dwin_e, even_w_out=dwout_e, odd_w_in=dwin_o, odd_w_uq=dwuq, odd_w_ukv=dwukv,
               odd_w_out=dwout_o.reshape(N_CHIPS, D_MODEL // N_CHIPS, D_MODEL),
               mlp_w1_0=dw1_0, mlp_w1_1=dw1_1,
               mlp_w2_0=dw2_0.reshape(N_CHIPS, D_FF // N_CHIPS, D_MODEL), mlp_w2_1=dw2_1.reshape(N_CHIPS, D_FF // N_CHIPS, D_MODEL))
    sm = dict(even_sinks=dsinks, odd_q_norm_g=dqn_g, odd_kv_norm_g=dkvn_g,
              ln1_g=jnp.concatenate([dln1g_0, dln1g_1]), ln1_b=jnp.concatenate([dln1b_0, dln1b_1]),
              ln2_g=jnp.concatenate([dln2g_0, dln2g_1]), ln2_b=jnp.concatenate([dln2b_0, dln2b_1]))
    return loss, dx0, big, sm


SMALL_ORDER = ("ln1_g", "ln1_b", "ln2_g", "ln2_b", "even_sinks", "odd_q_norm_g", "odd_kv_norm_g")
SMALL_COLS = 2176


def _pack_small(parts):
    flat = jnp.concatenate([p.reshape(-1) for p in parts])
    return jnp.pad(flat, (0, 8 * SMALL_COLS - flat.shape[0])).reshape(8, SMALL_COLS)


def _unpack_small(packed, shapes):
    flat = packed.reshape(-1)
    out, pos = [], 0
    for s in shapes:
        n = math.prod(s)
        out.append(flat[pos:pos + n].reshape(s))
        pos += n
    return out


def kernel(x, even_w_in, even_sinks, even_w_out, odd_w_in, odd_q_norm_g, odd_kv_norm_g, odd_w_uq, odd_w_ukv, odd_w_out, ln1_g, ln1_b, mlp_w1, mlp_w2, ln2_g, ln2_b, loss_target, m_even_w_in, m_even_sinks, m_even_w_out, m_odd_w_in, m_odd_q_norm_g, m_odd_kv_norm_g, m_odd_w_uq, m_odd_w_ukv, m_odd_w_out, m_ln1_g, m_ln1_b, m_mlp_w1, m_mlp_w2, m_ln2_g, m_ln2_b, v_even_w_in, v_even_sinks, v_even_w_out, v_odd_w_in, v_odd_q_norm_g, v_odd_kv_norm_g, v_odd_w_uq, v_odd_w_ukv, v_odd_w_out, v_ln1_g, v_ln1_b, v_mlp_w1, v_mlp_w2, v_ln2_g, v_ln2_b):
    chip = 2 * lax.axis_index("x") + lax.axis_index("y")
    e_w, o_w = EVEN_IN // N_CHIPS, ODD_IN // N_CHIPS

    shards = dict(
        even_w_in=jnp.pad(even_w_in[0], ((0, 0), (0, EVEN_IN_PAD - e_w))), even_w_out=even_w_out[0],
        odd_w_in=jnp.pad(odd_w_in[0], ((0, 0), (0, ODD_IN_PAD - o_w))), odd_w_uq=odd_w_uq[0], odd_w_ukv=odd_w_ukv[0],
        odd_w_out=odd_w_out[0], mlp_w1_0=mlp_w1[0], mlp_w1_1=mlp_w1[1], mlp_w2_0=mlp_w2[0], mlp_w2_1=mlp_w2[1])
    names = list(shards)
    gathered = dict(zip(names, _gather_weights([shards[n].astype(BF16) for n in names], name="gather_weights")))
    w = dict(even_w_in=gathered["even_w_in"], even_w_out=gathered["even_w_out"], odd_w_in=gathered["odd_w_in"],
             odd_w_uq=gathered["odd_w_uq"], odd_w_ukv=gathered["odd_w_ukv"],
             odd_w_out=gathered["odd_w_out"].reshape(1, D_MODEL, D_MODEL),
             mlp_w1=[gathered["mlp_w1_0"], gathered["mlp_w1_1"]],
             mlp_w2=[gathered["mlp_w2_0"].reshape(1, D_FF, D_MODEL), gathered["mlp_w2_1"].reshape(1, D_FF, D_MODEL)])
    norm_rows = jnp.pad(jnp.concatenate([odd_q_norm_g[0], odd_kv_norm_g[0]])[None], ((0, 7), (0, 64)))
    norm_all = _all_gather8(norm_rows, name="gather_norm_gains")[0::16]
    qn_w, kvn_w = D_Q_RANK // N_CHIPS, D_KV_RANK // N_CHIPS
    small = dict(even_sinks=even_sinks[0], odd_q_norm_g=norm_all[:, :qn_w].reshape(D_Q_RANK),
                 odd_kv_norm_g=norm_all[:, qn_w:qn_w + kvn_w].reshape(D_KV_RANK), ln1_g=ln1_g, ln1_b=ln1_b, ln2_g=ln2_g, ln2_b=ln2_b)

    loss_row, grad_x, big, sm = _local_step(x[0], loss_target[0], w, small)
    loss = lax.psum(loss_row[0, 0], ("x", "y", "c"))

    reduced = dict(zip(names, _reduce_scatter([big[n] for n in names], name="grad_rs")))
    every = _all_gather8(_pack_small([sm[n] for n in SMALL_ORDER]), name="gather_small_grads")
    sm_sum = _sum_slabs(every.reshape(8, 8, SMALL_COLS), name="sum_small_grads")
    g_ln1g, g_ln1b, g_ln2g, g_ln2b, g_sinks, g_qn, g_kvn = _unpack_small(
        sm_sum, [(DEPTH, D_MODEL)] * 4 + [(1, A_Q_HEADS), (D_Q_RANK,), (D_KV_RANK,)])
    grads = dict(
        even_w_in=reduced["even_w_in"][:, :e_w][None], even_sinks=g_sinks, even_w_out=reduced["even_w_out"][None],
        odd_w_in=reduced["odd_w_in"][:, :o_w][None], odd_q_norm_g=lax.dynamic_slice_in_dim(g_qn, chip * qn_w, qn_w)[None],
        odd_kv_norm_g=lax.dynamic_slice_in_dim(g_kvn, chip * kvn_w, kvn_w)[None], odd_w_uq=reduced["odd_w_uq"][None],
        odd_w_ukv=reduced["odd_w_ukv"][None], odd_w_out=reduced["odd_w_out"][None], ln1_g=g_ln1g, ln1_b=g_ln1b,
        mlp_w1=jnp.stack([reduced["mlp_w1_0"], reduced["mlp_w1_1"]]), mlp_w2=jnp.stack([reduced["mlp_w2_0"], reduced["mlp_w2_1"]]),
        ln2_g=g_ln2g, ln2_b=g_ln2b)

    weights = dict(even_w_in=even_w_in, even_sinks=even_sinks, even_w_out=even_w_out, odd_w_in=odd_w_in, odd_q_norm_g=odd_q_norm_g,
                   odd_kv_norm_g=odd_kv_norm_g, odd_w_uq=odd_w_uq, odd_w_ukv=odd_w_ukv, odd_w_out=odd_w_out, ln1_g=ln1_g, ln1_b=ln1_b,
                   mlp_w1=mlp_w1, mlp_w2=mlp_w2, ln2_g=ln2_g, ln2_b=ln2_b)
    m_in = dict(even_w_in=m_even_w_in, even_sinks=m_even_sinks, even_w_out=m_even_w_out, odd_w_in=m_odd_w_in,
                odd_q_norm_g=m_odd_q_norm_g, odd_kv_norm_g=m_odd_kv_norm_g, odd_w_uq=m_odd_w_uq, odd_w_ukv=m_odd_w_ukv,
                odd_w_out=m_odd_w_out, ln1_g=m_ln1_g, ln1_b=m_ln1_b, mlp_w1=m_mlp_w1, mlp_w2=m_mlp_w2, ln2_g=m_ln2_g, ln2_b=m_ln2_b)
    v_in = dict(even_w_in=v_even_w_in, even_sinks=v_even_sinks, even_w_out=v_even_w_out, odd_w_in=v_odd_w_in,
                odd_q_norm_g=v_odd_q_norm_g, odd_kv_norm_g=v_odd_kv_norm_g, odd_w_uq=v_odd_w_uq, odd_w_ukv=v_odd_w_ukv,
                odd_w_out=v_odd_w_out, ln1_g=v_ln1_g, ln1_b=v_ln1_b, mlp_w1=v_mlp_w1, mlp_w2=v_mlp_w2, ln2_g=v_ln2_g, ln2_b=v_ln2_b)
    order = list(weights)
    delta, new_m, new_v = {}, {}, {}
    for n in order:
        shape = weights[n].shape
        flat = (math.prod(shape[:-1]), shape[-1])
        d, mm, vv = _adamw(weights[n].reshape(flat), grads[n].reshape(flat), m_in[n].reshape(flat), v_in[n].reshape(flat),
                           name=f"adamw_{n}")
        delta[n], new_m[n], new_v[n] = d.reshape(shape), mm.reshape(shape), vv.reshape(shape)
    return (loss, grad_x[None], *[grads[n] for n in order], *[delta[n] for n in order], *[new_m[n] for n in order],
            *[new_v[n] for n in order])
```

```python
import functools
import math

import jax
import jax.numpy as jnp
from jax import lax
from jax.experimental import pallas as pl
from jax.experimental.pallas import tpu as pltpu

F32 = jnp.float32
BF16 = jnp.bfloat16
MESH = pl.DeviceIdType.MESH

D_MODEL = 2048
SEQ = 2048
DEPTH = 2
HEAD_DIM = 64
A_Q_HEADS, A_KV_HEADS, A_WINDOW = 16, 2, 128
B_HEADS = 8
B_PATTERNS = ((128, 1), (512, 4), (2048, 16))
C_HEADS = 16
D_HEADS, D_Q_RANK, D_KV_RANK, D_NOPE, D_ROPE, D_V = 16, 512, 256, 64, 32, 64
ROPE_BASE = 10000.0
D_FF = 4 * D_MODEL
LN_EPS = 1e-5
RMS_EPS = 1e-6
ALPHA = (2 * DEPTH) ** 0.25
A_Q_W = A_Q_HEADS * HEAD_DIM
A_KV_W = A_KV_HEADS * HEAD_DIM
B_W = B_HEADS * HEAD_DIM
EVEN_IN = A_Q_W + 2 * A_KV_W + 3 * B_W * len(B_PATTERNS)
EVEN_OUT = A_Q_W + B_W
C_W = C_HEADS * HEAD_DIM
ODD_IN = 3 * C_W + D_Q_RANK + D_KV_RANK + D_ROPE
ADAM_LR, ADAM_B1, ADAM_B2, ADAM_EPS, ADAM_WD, ADAM_STEP = 0.001, 0.9, 0.999, 1e-08, 0.01, 10

N_CHIPS = 4
EVEN_IN_PAD = 1536
ODD_IN_PAD = 1024
ATT_BLK = 128
NEG = -1e30
V7X_VMEM_LIMIT = 48 * 1024 * 1024


def _params(*sem):
    return pltpu.CompilerParams(dimension_semantics=sem, vmem_limit_bytes=V7X_VMEM_LIMIT)


def _tile(n, cap):
    if n <= cap:
        return n
    t = cap - cap % 128
    while n % t:
        t -= 128
    return t


def _matmul(a, b, *, mode, out_dtype, name, epilogue=None, extra=None, alpha=1.0, tm=1024, tn=512, tk=2048):
    if mode == "nn":
        M, K = a.shape
        P, _, Np = b.shape
        tm, tn, tk = _tile(M, tm), _tile(Np, tn), _tile(K, tk)
        nj = Np // tn
        grid = (M // tm, P * nj, K // tk)
        a_spec = pl.BlockSpec((tm, tk), lambda i, j, k: (i, k))
        b_spec = pl.BlockSpec((None, tk, tn), lambda i, j, k: (j // nj, k, j % nj))
        o_spec = pl.BlockSpec((tm, tn), lambda i, j, k: (i, j))
        out_shape = (M, P * Np)
        dims = (((1,), (0,)), ((), ()))
    elif mode == "nt":
        M, N = a.shape
        P, K, Np = b.shape
        tm, tn, tk = _tile(M, tm), _tile(K, tn), _tile(Np, tk)
        nkk = Np // tk
        grid = (M // tm, K // tn, P * nkk)
        a_spec = pl.BlockSpec((tm, tk), lambda i, j, k: (i, k))
        b_spec = pl.BlockSpec((None, tn, tk), lambda i, j, k: (k // nkk, j, k % nkk))
        o_spec = pl.BlockSpec((tm, tn), lambda i, j, k: (i, j))
        out_shape = (M, K)
        dims = (((1,), (1,)), ((), ()))
    else:
        raise ValueError(mode)
    n_k = grid[2]
    n_extra = 0 if extra is None else 1
    n_out = 2 if epilogue == "relu2" else 1

    def body(*refs):
        a_ref, b_ref = refs[:2]
        e_ref = refs[2] if n_extra else None
        outs = refs[2 + n_extra:2 + n_extra + n_out]
        part = lax.dot_general(a_ref[...], b_ref[...], dims, preferred_element_type=F32)

        def finish(r):
            if epilogue == "relu2":
                outs[0][...] = r.astype(outs[0].dtype)
                rp = jnp.maximum(r, 0.0)
                outs[1][...] = (rp * rp).astype(outs[1].dtype)
            elif epilogue == "drelu2":
                outs[0][...] = (r * (2.0 * jnp.maximum(e_ref[...].astype(F32), 0.0))).astype(outs[0].dtype)
            elif epilogue == "add":
                outs[0][...] = (r + alpha * e_ref[...].astype(F32)).astype(outs[0].dtype)
            else:
                outs[0][...] = r.astype(outs[0].dtype)

        if n_k == 1:
            finish(part)
        else:
            acc = refs[-1]
            k = pl.program_id(2)

            @pl.when(k == 0)
            def _():
                acc[...] = part

            @pl.when((k > 0) & (k < n_k - 1))
            def _():
                acc[...] += part

            @pl.when(k == n_k - 1)
            def _():
                finish(acc[...] + part)

    in_specs = [a_spec, b_spec] + ([o_spec] if n_extra else [])
    shapes = [jax.ShapeDtypeStruct(out_shape, out_dtype)] * n_out
    res = pl.pallas_call(
        body, name=name, grid=grid, in_specs=in_specs,
        out_specs=[o_spec] * n_out, out_shape=shapes,
        scratch_shapes=[pltpu.VMEM((tm, tn), F32)] if n_k > 1 else [],
        compiler_params=_params("parallel", "parallel", "arbitrary"),
    )(a, b, *([extra] if n_extra else []))
    return res if n_out > 1 else res[0]


def _matmul_tn(a, g, *, shards, name, tm=1024, tn=512):
    M, K = a.shape
    P = shards
    Np = g.shape[1] // P
    tm, tn = _tile(K, tm), _tile(Np, tn)
    nj = Np // tn

    def body(a_ref, g_ref, o_ref, o16_ref):
        r = lax.dot_general(a_ref[...], g_ref[...], (((0,), (0,)), ((), ())), preferred_element_type=F32)
        o_ref[...] = r
        o16_ref[...] = r.astype(BF16)

    o_spec = pl.BlockSpec((None, tm, tn), lambda i, j: (j // nj, i, j % nj))
    return pl.pallas_call(
        body, name=name, grid=(K // tm, P * nj),
        in_specs=[pl.BlockSpec((M, tm), lambda i, j: (0, i)), pl.BlockSpec((M, tn), lambda i, j: (0, j))],
        out_specs=[o_spec, o_spec],
        out_shape=[jax.ShapeDtypeStruct((P, K, Np), F32), jax.ShapeDtypeStruct((P, K, Np), BF16)],
        compiler_params=_params("parallel", "parallel"),
    )(a, g)


def _norm_fwd(x, res, g, b, *, center, eps, alpha, name, rows=256):
    S, W = x.shape
    has_res = res is not None
    rows = _tile(S, rows)

    def body(*refs):
        x_ref = refs[0]
        r_ref = refs[1] if has_res else None
        g_ref = refs[1 + has_res]
        b_ref = refs[2 + has_res] if center else None
        y_ref, y16_ref, u_ref = refs[-3:]
        u = x_ref[...]
        if has_res:
            u = alpha * u + r_ref[...]
        if center:
            mu = jnp.mean(u, axis=1, keepdims=True)
            xc = u - mu
        else:
            xc = u
        var = jnp.mean(xc * xc, axis=1, keepdims=True)
        y = xc * lax.rsqrt(var + eps) * g_ref[...]
        if center:
            y = y + b_ref[...]
        y_ref[...] = y
        y16_ref[...] = y.astype(BF16)
        u_ref[...] = u

    row_spec = pl.BlockSpec((rows, W), lambda i: (i, 0))
    vec_spec = pl.BlockSpec((1, W), lambda i: (0, 0))
    ins = [x] + ([res] if has_res else []) + [g.reshape(1, W)] + ([b.reshape(1, W)] if center else [])
    specs = [row_spec] + ([row_spec] if has_res else []) + [vec_spec] + ([vec_spec] if center else [])
    return pl.pallas_call(
        body, name=name, grid=(S // rows,), in_specs=specs,
        out_specs=[row_spec, row_spec, row_spec],
        out_shape=[jax.ShapeDtypeStruct((S, W), F32), jax.ShapeDtypeStruct((S, W), BF16), jax.ShapeDtypeStruct((S, W), F32)],
        compiler_params=_params("parallel"),
    )(*ins)


def _norm_bwd(dy, u, g, *, center, eps, name, rows=256):
    S, W = u.shape
    rows = _tile(S, rows)

    def body(dy_ref, u_ref, g_ref, du_ref, du16_ref, dg_ref, db_ref):
        i = pl.program_id(0)
        uu = u_ref[...]
        dyv = dy_ref[...]
        if center:
            xc = uu - jnp.mean(uu, axis=1, keepdims=True)
        else:
            xc = uu
        rstd = lax.rsqrt(jnp.mean(xc * xc, axis=1, keepdims=True) + eps)
        xhat = xc * rstd
        dxh = dyv * g_ref[...]
        c2 = jnp.mean(dxh * xhat, axis=1, keepdims=True)
        du = dxh - xhat * c2
        if center:
            du = du - jnp.mean(dxh, axis=1, keepdims=True)
        du = du * rstd
        du_ref[...] = du
        du16_ref[...] = du.astype(BF16)

        @pl.when(i == 0)
        def _():
            dg_ref[...] = jnp.zeros_like(dg_ref)
            db_ref[...] = jnp.zeros_like(db_ref)

        dg_ref[...] += jnp.sum(dyv * xhat, axis=0, keepdims=True)
        db_ref[...] += jnp.sum(dyv, axis=0, keepdims=True)

    row_spec = pl.BlockSpec((rows, W), lambda i: (i, 0))
    vec_spec = pl.BlockSpec((1, W), lambda i: (0, 0))
    return pl.pallas_call(
        body, name=name, grid=(S // rows,), in_specs=[row_spec, row_spec, vec_spec],
        out_specs=[row_spec, row_spec, vec_spec, vec_spec],
        out_shape=[jax.ShapeDtypeStruct((S, W), F32), jax.ShapeDtypeStruct((S, W), BF16),
                   jax.ShapeDtypeStruct((1, W), F32), jax.ShapeDtypeStruct((1, W), F32)],
        compiler_params=_params("arbitrary"),
    )(dy, u, g.reshape(1, W))


def _loss_head(y, target, *, name, rows=256):
    S, W = y.shape
    rows = _tile(S, rows)

    def body(y_ref, t_ref, dy_ref, l_ref):
        i = pl.program_id(0)
        e = y_ref[...] - t_ref[...]
        dy_ref[...] = e * (1.0 / W)

        @pl.when(i == 0)
        def _():
            l_ref[...] = jnp.zeros_like(l_ref)

        l_ref[...] += jnp.full(l_ref.shape, 0.5 / W * jnp.sum(e * e), F32)

    row_spec = pl.BlockSpec((rows, W), lambda i: (i, 0))
    return pl.pallas_call(
        body, name=name, grid=(S // rows,), in_specs=[row_spec, row_spec],
        out_specs=[row_spec, pl.BlockSpec((1, 128), lambda i: (0, 0))],
        out_shape=[jax.ShapeDtypeStruct((S, W), F32), jax.ShapeDtypeStruct((1, 128), F32)],
        compiler_params=_params("arbitrary"),
    )(y, target)


def _adamw(w, g, m, v, *, name, rows=256):
    R, C = w.shape
    rows = _tile(R, rows) if R % 8 == 0 else R
    c1 = 1.0 / (1.0 - ADAM_B1 ** ADAM_STEP)
    c2 = 1.0 / (1.0 - ADAM_B2 ** ADAM_STEP)

    def body(w_ref, g_ref, m_ref, v_ref, d_ref, mo_ref, vo_ref):
        gg = g_ref[...]
        mn = ADAM_B1 * m_ref[...] + (1.0 - ADAM_B1) * gg
        vn = ADAM_B2 * v_ref[...] + (1.0 - ADAM_B2) * (gg * gg)
        d_ref[...] = -ADAM_LR * ((mn * c1) / (jnp.sqrt(vn * c2) + ADAM_EPS) + ADAM_WD * w_ref[...])
        mo_ref[...] = mn
        vo_ref[...] = vn

    spec = pl.BlockSpec((rows, C), lambda i: (i, 0))
    shp = jax.ShapeDtypeStruct((R, C), F32)
    return pl.pallas_call(
        body, name=name, grid=(R // rows,), in_specs=[spec] * 4, out_specs=[spec] * 3, out_shape=[shp] * 3,
        compiler_params=_params("parallel"),
    )(w, g, m, v)


def _rope(x1, x2, cos, sin, *, name, rows=512):
    H, S, W = x1.shape
    rows = _tile(S, rows)

    def body(x1_ref, x2_ref, c_ref, s_ref, y1_ref, y2_ref):
        t1 = jnp.sum(x1_ref[...], axis=0)
        t2 = jnp.sum(x2_ref[...], axis=0)
        c = c_ref[...]
        s = s_ref[...]
        y1_ref[...] = t1 * c - t2 * s
        y2_ref[...] = t1 * s + t2 * c

    xs = pl.BlockSpec((H, rows, W), lambda i: (0, i, 0))
    ts = pl.BlockSpec((rows, W), lambda i: (i, 0))
    shp = jax.ShapeDtypeStruct((S, W), F32)
    return pl.pallas_call(
        body, name=name, grid=(S // rows,), in_specs=[xs, xs, ts, ts], out_specs=[ts, ts], out_shape=[shp, shp],
        compiler_params=_params("parallel"),
    )(x1, x2, cos, sin)


def _band_geometry(blk, nb_seq, n_prev):
    def geo(i):
        seq = i // nb_seq
        n = i % nb_seq
        return seq, n, jnp.maximum(n - n_prev, 0)
    return geo


def _attn_fwd(q, k, v, slopes, sinks, *, blk, nb_seq, n_prev, n_back, scale, dist_scale, name):
    Hq, T, dqk = q.shape
    Hk, _, dv = v.shape
    group = Hq // Hk
    use_bias, use_sink = slopes is not None, sinks is not None
    geo = _band_geometry(blk, nb_seq, n_prev)

    def body(*refs):
        q_ref, k_ref, v_ref = refs[:3]
        slope_ref = refs[3] if use_bias else None
        sink_ref = refs[3 + use_bias] if use_sink else None
        o_ref, lse_ref = refs[-2:]
        h = pl.program_id(0)
        seq, n, lo = geo(pl.program_id(1))
        qb = q_ref[...]
        diff = lax.broadcasted_iota(jnp.int32, (blk, blk), 0) - lax.broadcasted_iota(jnp.int32, (blk, blk), 1)
        if use_sink:
            m0 = jnp.full((blk, 1), sink_ref[h], F32)
            l0 = jnp.ones((blk, 1), F32)
        else:
            m0 = jnp.full((blk, 1), NEG, F32)
            l0 = jnp.zeros((blk, 1), F32)

        def step(j, carry):
            m, l, acc = carry
            start = pl.multiple_of((seq * nb_seq + j) * blk, blk)
            kb = k_ref[pl.ds(start, blk), :]
            vb = v_ref[pl.ds(start, blk), :]
            s = lax.dot_general(qb, kb, (((1,), (1,)), ((), ())), preferred_element_type=F32) * scale
            rel = (n - j) * blk + diff
            valid = (rel >= 0) & (rel <= n_back)
            if use_bias:
                s = s - (slope_ref[h] * dist_scale) * rel.astype(F32)
            s = jnp.where(valid, s, NEG)
            m_new = jnp.maximum(m, jnp.max(s, axis=1, keepdims=True))
            p = jnp.where(valid, jnp.exp(s - m_new), 0.0)
            a = jnp.exp(m - m_new)
            l = a * l + jnp.sum(p, axis=1, keepdims=True)
            acc = a * acc + jnp.dot(p.astype(BF16), vb, preferred_element_type=F32)
            return m_new, l, acc

        m, l, acc = lax.fori_loop(lo, n + 1, step, (m0, l0, jnp.zeros((blk, dv), F32)))
        o_ref[...] = acc / l
        lse_ref[...] = m + jnp.log(l)

    smem = pl.BlockSpec(memory_space=pltpu.SMEM)
    in_specs = [pl.BlockSpec((None, blk, dqk), lambda h, i: (h, i, 0)),
                pl.BlockSpec((None, T, dqk), lambda h, i: (h // group, 0, 0)),
                pl.BlockSpec((None, T, dv), lambda h, i: (h // group, 0, 0))]
    ins = [q, k, v]
    if use_bias:
        in_specs.append(smem)
        ins.append(slopes)
    if use_sink:
        in_specs.append(smem)
        ins.append(sinks)
    return pl.pallas_call(
        body, name=name, grid=(Hq, T // blk), in_specs=in_specs,
        out_specs=[pl.BlockSpec((None, blk, dv), lambda h, i: (h, i, 0)), pl.BlockSpec((None, blk, 1), lambda h, i: (h, i, 0))],
        out_shape=[jax.ShapeDtypeStruct((Hq, T, dv), F32), jax.ShapeDtypeStruct((Hq, T, 1), F32)],
        compiler_params=_params("parallel", "parallel"),
    )(*ins)


def _attn_bwd(q, k, v, o, do, lse, dlse, slopes, sinks, *, blk, nb_seq, n_prev, n_back, scale, dist_scale, name):
    Hq, T, dqk = q.shape
    Hk, _, dv = v.shape
    group = Hq // Hk
    use_bias, use_sink, use_dlse = slopes is not None, sinks is not None, dlse is not None
    geo = _band_geometry(blk, nb_seq, n_prev)

    def body(*refs):
        q_ref, k_ref, v_ref, o_ref, do_ref, lse_ref = refs[:6]
        pos = 6
        dlse_ref = refs[pos] if use_dlse else None
        pos += use_dlse
        slope_ref = refs[pos] if use_bias else None
        pos += use_bias
        sink_ref = refs[pos] if use_sink else None
        pos += use_sink
        dq_ref, dk_ref, dv_ref = refs[pos:pos + 3]
        dsink_ref = refs[pos + 3] if use_sink else None
        h = pl.program_id(0)
        i = pl.program_id(1)
        seq, n, lo = geo(i)
        qb = q_ref[...]
        dob = do_ref[...]
        do16 = dob.astype(BF16)
        lse = lse_ref[...]
        c = -jnp.sum(dob * o_ref[...], axis=1, keepdims=True)
        if use_dlse:
            c = c + dlse_ref[...]
        diff = lax.broadcasted_iota(jnp.int32, (blk, blk), 0) - lax.broadcasted_iota(jnp.int32, (blk, blk), 1)

        @pl.when((h % group == 0) & (i == 0))
        def _():
            dk_ref[...] = jnp.zeros_like(dk_ref)
            dv_ref[...] = jnp.zeros_like(dv_ref)

        def step(j, dq):
            start = pl.multiple_of((seq * nb_seq + j) * blk, blk)
            kb = k_ref[pl.ds(start, blk), :]
            vb = v_ref[pl.ds(start, blk), :]
            s = lax.dot_general(qb, kb, (((1,), (1,)), ((), ())), preferred_element_type=F32) * scale
            rel = (n - j) * blk + diff
            valid = (rel >= 0) & (rel <= n_back)
            if use_bias:
                s = s - (slope_ref[h] * dist_scale) * rel.astype(F32)
            p = jnp.where(valid, jnp.exp(jnp.where(valid, s, NEG) - lse), 0.0)
            dp = lax.dot_general(do16, vb, (((1,), (1,)), ((), ())), preferred_element_type=F32)
            ds16 = (p * (dp + c) * scale).astype(BF16)
            dk_ref[pl.ds(start, blk), :] += lax.dot_general(ds16, qb, (((0,), (0,)), ((), ())), preferred_element_type=F32)
            dv_ref[pl.ds(start, blk), :] += lax.dot_general(p.astype(BF16), do16, (((0,), (0,)), ((), ())), preferred_element_type=F32)
            return dq + jnp.dot(ds16, kb, preferred_element_type=F32)

        dq_ref[...] = lax.fori_loop(lo, n + 1, step, jnp.zeros((blk, dqk), F32))
        if use_sink:
            @pl.when(i == 0)
            def _():
                dsink_ref[...] = jnp.zeros_like(dsink_ref)

            dsink_ref[...] += jnp.full(dsink_ref.shape, jnp.sum(jnp.exp(sink_ref[h] - lse) * c), F32)

    smem = pl.BlockSpec(memory_space=pltpu.SMEM)
    qs = pl.BlockSpec((None, blk, dqk), lambda h, i: (h, i, 0))
    os_ = pl.BlockSpec((None, blk, dv), lambda h, i: (h, i, 0))
    ls = pl.BlockSpec((None, blk, 1), lambda h, i: (h, i, 0))
    ks = pl.BlockSpec((None, T, dqk), lambda h, i: (h // group, 0, 0))
    vs = pl.BlockSpec((None, T, dv), lambda h, i: (h // group, 0, 0))
    in_specs = [qs, ks, vs, os_, os_, ls]
    ins = [q, k, v, o, do, lse]
    if use_dlse:
        in_specs.append(ls)
        ins.append(dlse)
    if use_bias:
        in_specs.append(smem)
        ins.append(slopes)
    if use_sink:
        in_specs.append(smem)
        ins.append(sinks)
    out_specs = [qs, ks, vs]
    out_shape = [jax.ShapeDtypeStruct((Hq, T, dqk), F32), jax.ShapeDtypeStruct((Hk, T, dqk), F32),
                 jax.ShapeDtypeStruct((Hk, T, dv), F32)]
    if use_sink:
        out_specs.append(pl.BlockSpec((None, 1, 128), lambda h, i: (h, 0, 0)))
        out_shape.append(jax.ShapeDtypeStruct((Hq, 1, 128), F32))
    return pl.pallas_call(
        body, name=name, grid=(Hq, T // blk), in_specs=in_specs, out_specs=out_specs, out_shape=out_shape,
        compiler_params=_params("arbitrary", "arbitrary"),
    )(*ins)


def _tri_sum(x, upper):
    hi = x.astype(BF16)
    r = x - hi.astype(F32)
    mid = r.astype(BF16)
    lo = (r - mid.astype(F32)).astype(BF16)
    return (jnp.dot(hi, upper, preferred_element_type=F32) + jnp.dot(mid, upper, preferred_element_type=F32)
            + jnp.dot(lo, upper, preferred_element_type=F32))


def _stick_terms(qb, kb, n, j, blk, scale, diff):
    z = lax.dot_general(qb, kb, (((1,), (1,)), ((), ())), preferred_element_type=F32) * scale
    e = jnp.exp(-jnp.abs(z))
    l1p = jnp.log(1.0 + e)
    lb = jnp.minimum(z, 0.0) - l1p
    strict = ((n - j) * blk + diff) > 0
    lk = jnp.where(strict, lb - z, 0.0)
    return z, e, lb, lk, strict


def _stick_fwd(q, k, v, *, blk, scale, name):
    H, T, dh = q.shape

    def body(q_ref, k_ref, v_ref, o_ref, tot_ref):
        n = pl.program_id(1)
        qb = q_ref[...]
        r_i = lax.broadcasted_iota(jnp.int32, (blk, blk), 0)
        c_i = lax.broadcasted_iota(jnp.int32, (blk, blk), 1)
        diff = r_i - c_i
        upper = (r_i > c_i).astype(BF16)

        def step(t, carry):
            run, acc = carry
            j = n - t
            start = pl.multiple_of(j * blk, blk)
            kb = k_ref[pl.ds(start, blk), :]
            vb = v_ref[pl.ds(start, blk), :]
            _, _, lb, lk, strict = _stick_terms(qb, kb, n, j, blk, scale, diff)
            w = jnp.where(strict, jnp.exp(lb + run + _tri_sum(lk, upper)), 0.0)
            acc = acc + jnp.dot(w.astype(BF16), vb, preferred_element_type=F32)
            return run + jnp.sum(lk, axis=1, keepdims=True), acc

        run, acc = lax.fori_loop(0, n + 1, step, (jnp.zeros((blk, 1), F32), jnp.zeros((blk, dh), F32)))
        o_ref[...] = acc
        tot_ref[...] = run

    qs = pl.BlockSpec((None, blk, dh), lambda h, i: (h, i, 0))
    ks = pl.BlockSpec((None, T, dh), lambda h, i: (h, 0, 0))
    ts = pl.BlockSpec((None, blk, 1), lambda h, i: (h, i, 0))
    return pl.pallas_call(
        body, name=name, grid=(H, T // blk), in_specs=[qs, ks, ks], out_specs=[qs, ts],
        out_shape=[jax.ShapeDtypeStruct((H, T, dh), F32), jax.ShapeDtypeStruct((H, T, 1), F32)],
        compiler_params=_params("parallel", "parallel"),
    )(q, k, v)


def _stick_bwd(q, k, v, tot, do, *, blk, scale, name):
    H, T, dh = q.shape

    def body(q_ref, k_ref, v_ref, tot_ref, do_ref, dq_ref, dk_ref, dv_ref):
        n = pl.program_id(1)
        qb = q_ref[...]
        do16 = do_ref[...].astype(BF16)
        tot = tot_ref[...]
        r_i = lax.broadcasted_iota(jnp.int32, (blk, blk), 0)
        c_i = lax.broadcasted_iota(jnp.int32, (blk, blk), 1)
        diff = r_i - c_i
        upto = (r_i <= c_i).astype(BF16)
        below = (r_i < c_i).astype(BF16)

        @pl.when(n == 0)
        def _():
            dk_ref[...] = jnp.zeros_like(dk_ref)
            dv_ref[...] = jnp.zeros_like(dv_ref)

        def step(j, carry):
            run, grun, dq = carry
            start = pl.multiple_of(j * blk, blk)
            kb = k_ref[pl.ds(start, blk), :]
            vb = v_ref[pl.ds(start, blk), :]
            z, e, lb, lk, strict = _stick_terms(qb, kb, n, j, blk, scale, diff)
            w = jnp.where(strict, jnp.exp(lb + tot - (run + _tri_sum(lk, upto))), 0.0)
            dw = lax.dot_general(do16, vb, (((1,), (1,)), ((), ())), preferred_element_type=F32)
            g = w * dw
            before = grun + _tri_sum(g, below)
            inv = 1.0 / (1.0 + e)
            sig = jnp.where(z >= 0, inv, e * inv)
            dz16 = (jnp.where(strict, g * (1.0 - sig) - sig * before, 0.0) * scale).astype(BF16)
            dk_ref[pl.ds(start, blk), :] += lax.dot_general(dz16, qb, (((0,), (0,)), ((), ())), preferred_element_type=F32)
            dv_ref[pl.ds(start, blk), :] += lax.dot_general(w.astype(BF16), do16, (((0,), (0,)), ((), ())), preferred_element_type=F32)
            dq = dq + jnp.dot(dz16, kb, preferred_element_type=F32)
            return run + jnp.sum(lk, axis=1, keepdims=True), grun + jnp.sum(g, axis=1, keepdims=True), dq

        zero = jnp.zeros((blk, 1), F32)
        _, _, dq = lax.fori_loop(0, n + 1, step, (zero, zero, jnp.zeros((blk, dh), F32)))
        dq_ref[...] = dq

    qs = pl.BlockSpec((None, blk, dh), lambda h, i: (h, i, 0))
    ks = pl.BlockSpec((None, T, dh), lambda h, i: (h, 0, 0))
    ts = pl.BlockSpec((None, blk, 1), lambda h, i: (h, i, 0))
    shp = jax.ShapeDtypeStruct((H, T, dh), F32)
    return pl.pallas_call(
        body, name=name, grid=(H, T // blk), in_specs=[qs, ks, ks, ts, qs], out_specs=[qs, ks, ks],
        out_shape=[shp, shp, shp],
        compiler_params=_params("arbitrary", "arbitrary"),
    )(q, k, v, tot, do)


def _mix_fwd(outs, lses, *, name, rows=512):
    H, T, dh = outs[0].shape
    rows = _tile(T, rows)

    def body(o0, o1, o2, l0, l1, l2, y_ref):
        ls = [l0[...], l1[...], l2[...]]
        mx = jnp.maximum(jnp.maximum(ls[0], ls[1]), ls[2])
        es = [jnp.exp(x - mx) for x in ls]
        den = es[0] + es[1] + es[2]
        y_ref[...] = (es[0] * o0[...] + es[1] * o1[...] + es[2] * o2[...]) / den

    os_ = pl.BlockSpec((None, rows, dh), lambda h, i: (h, i, 0))
    ls_ = pl.BlockSpec((None, rows, 1), lambda h, i: (h, i, 0))
    return pl.pallas_call(
        body, name=name, grid=(H, T // rows), in_specs=[os_] * 3 + [ls_] * 3, out_specs=os_,
        out_shape=jax.ShapeDtypeStruct((H, T, dh), F32),
        compiler_params=_params("parallel", "parallel"),
    )(*outs, *lses)


def _mix_bwd(outs, lses, dy, *, name, rows=512):
    H, T, dh = outs[0].shape
    rows = _tile(T, rows)

    def body(o0, o1, o2, l0, l1, l2, dy_ref, d0, d1, d2, g0, g1, g2):
        ls = [l0[...], l1[...], l2[...]]
        mx = jnp.maximum(jnp.maximum(ls[0], ls[1]), ls[2])
        es = [jnp.exp(x - mx) for x in ls]
        den = es[0] + es[1] + es[2]
        mix = [e / den for e in es]
        dyv = dy_ref[...]
        dm = [jnp.sum(dyv * o[...], axis=1, keepdims=True) for o in (o0, o1, o2)]
        avg = mix[0] * dm[0] + mix[1] * dm[1] + mix[2] * dm[2]
        for d_ref, g_ref, w, t in zip((d0, d1, d2), (g0, g1, g2), mix, dm):
            d_ref[...] = w * dyv
            g_ref[...] = w * (t - avg)

    os_ = pl.BlockSpec((None, rows, dh), lambda h, i: (h, i, 0))
    ls_ = pl.BlockSpec((None, rows, 1), lambda h, i: (h, i, 0))
    so = jax.ShapeDtypeStruct((H, T, dh), F32)
    sl = jax.ShapeDtypeStruct((H, T, 1), F32)
    res = pl.pallas_call(
        body, name=name, grid=(H, T // rows), in_specs=[os_] * 3 + [ls_] * 3 + [os_], out_specs=[os_] * 3 + [ls_] * 3,
        out_shape=[so] * 3 + [sl] * 3,
        compiler_params=_params("parallel", "parallel"),
    )(*outs, *lses, dy)
    return res[:3], res[3:]


def _position():
    return lax.axis_index("x"), lax.axis_index("y"), lax.axis_index("c")


def _other_chips(x, y):
    return [(1 - x, y), (x, 1 - y), (1 - x, 1 - y)]


def _gather_weights(shards, *, name):
    n = len(shards)

    def body(*refs):
        ins, outs = refs[:n], refs[n:2 * n]
        send_sems, recv_sems = refs[2 * n:]
        x, y, c = _position()
        chips = _other_chips(x, y)

        def remote(w, k, chip, core, to, src=None):
            half = ins[w].shape[0] // 2
            slab = outs[w].at[2 * chip[0] + chip[1], pl.ds(core * half, half), :]
            return pltpu.make_async_remote_copy(
                src_ref=slab if src is None else src, dst_ref=slab,
                send_sem=send_sems.at[w, k], recv_sem=recv_sems.at[w, k], device_id=to, device_id_type=MESH)

        sends = []
        for w in range(n):
            half = ins[w].shape[0] // 2
            for j, chip in enumerate(chips):
                sends.append(remote(w, j, (x, y), c, (chip[0], chip[1], c), src=ins[w].at[pl.ds(c * half, half), :]))
                sends[-1].start()
        for w in range(n):
            for j, chip in enumerate(chips):
                remote(w, j, chip, c, (x, y, c)).wait_recv()
                sends.append(remote(w, 3 + j, chip, c, (x, y, 1 - c)))
                sends[-1].start()
        for w in range(n):
            for j, chip in enumerate(chips):
                remote(w, 3 + j, chip, 1 - c, (x, y, c)).wait_recv()
        for cp in sends:
            cp.wait_send()

    hbm = pl.BlockSpec(memory_space=pl.ANY)
    return pl.pallas_call(
        body, name=name, in_specs=[hbm] * n, out_specs=[hbm] * n,
        out_shape=[jax.ShapeDtypeStruct((N_CHIPS,) + s.shape, s.dtype) for s in shards],
        scratch_shapes=[pltpu.SemaphoreType.DMA((n, 6)), pltpu.SemaphoreType.DMA((n, 6))],
    )(*shards)


def _swap_other_half(arrs, *, name):
    n = len(arrs)

    def body(*refs):
        ins, outs = refs[:n], refs[n:2 * n]
        send_sems, recv_sems = refs[2 * n:]
        x, y, c = _position()
        cps = []
        for w in range(n):
            half = ins[w].shape[1] // 2
            cps.append(pltpu.make_async_remote_copy(
                src_ref=ins[w].at[:, pl.ds((1 - c) * half, half), :], dst_ref=outs[w], send_sem=send_sems.at[w],
                recv_sem=recv_sems.at[w], device_id=(x, y, 1 - c), device_id_type=MESH))
            cps[-1].start()
        for cp in cps:
            cp.wait()

    hbm = pl.BlockSpec(memory_space=pl.ANY)
    return pl.pallas_call(
        body, name=name, in_specs=[hbm] * n, out_specs=[hbm] * n,
        out_shape=[jax.ShapeDtypeStruct((a.shape[0], a.shape[1] // 2, a.shape[2]), a.dtype) for a in arrs],
        scratch_shapes=[pltpu.SemaphoreType.DMA((n,)), pltpu.SemaphoreType.DMA((n,))],
    )(*arrs)


def _share_halves(bufs, *, name):
    n = len(bufs)

    def body(*refs):
        ins, outs = refs[:n], refs[n:2 * n]
        send_sems, recv_sems = refs[2 * n:]
        x, y, c = _position()
        sends = []
        for w in range(n):
            sends.append(pltpu.make_async_remote_copy(src_ref=ins[w].at[c], dst_ref=outs[w].at[c], send_sem=send_sems.at[w],
                                                      recv_sem=recv_sems.at[w], device_id=(x, y, 1 - c), device_id_type=MESH))
            sends[-1].start()
        for w in range(n):
            pltpu.make_async_remote_copy(src_ref=ins[w].at[1 - c], dst_ref=outs[w].at[1 - c], send_sem=send_sems.at[w],
                                         recv_sem=recv_sems.at[w], device_id=(x, y, 1 - c), device_id_type=MESH).wait_recv()
        for cp in sends:
            cp.wait_send()

    hbm = pl.BlockSpec(memory_space=pl.ANY)
    return pl.pallas_call(
        body, name=name, in_specs=[hbm] * n, out_specs=[hbm] * n,
        out_shape=[jax.ShapeDtypeStruct(a.shape, a.dtype) for a in bufs],
        input_output_aliases={w: w for w in range(n)},
        scratch_shapes=[pltpu.SemaphoreType.DMA((n,)), pltpu.SemaphoreType.DMA((n,))],
    )(*bufs)


def _chip_exchange(arrs, *, name):
    n = len(arrs)

    def body(*refs):
        ins, outs = refs[:n], refs[n:2 * n]
        send_sems, recv_sems = refs[2 * n:]
        x, y, c = _position()
        chips = _other_chips(x, y)
        sends = []
        for w in range(n):
            for j, chip in enumerate(chips):
                sends.append(pltpu.make_async_remote_copy(
                    src_ref=ins[w].at[2 * chip[0] + chip[1]], dst_ref=outs[w].at[j], send_sem=send_sems.at[w, j],
                    recv_sem=recv_sems.at[w, j], device_id=(chip[0], chip[1], c), device_id_type=MESH))
                sends[-1].start()
        for cp in sends:
            cp.wait()

    hbm = pl.BlockSpec(memory_space=pl.ANY)
    return pl.pallas_call(
        body, name=name, in_specs=[hbm] * n, out_specs=[hbm] * n,
        out_shape=[jax.ShapeDtypeStruct((3,) + a.shape[1:], a.dtype) for a in arrs],
        scratch_shapes=[pltpu.SemaphoreType.DMA((n, 3)), pltpu.SemaphoreType.DMA((n, 3))],
    )(*arrs)


def _all_gather8(v, *, name):
    m, n = v.shape

    def body(v_ref, out_ref, send_sems, recv_sems, local_sem):
        x, y, c = _position()
        me, sibling = (x, y, c), (x, y, 1 - c)
        chips = _other_chips(x, y)

        def rows(px, py, pc):
            return out_ref.at[pl.ds((4 * px + 2 * py + pc) * m, m), :]

        def copy(k, block, to, src=None):
            return pltpu.make_async_remote_copy(src_ref=rows(*block) if src is None else src, dst_ref=rows(*block),
                                                send_sem=send_sems.at[k], recv_sem=recv_sems.at[k], device_id=to, device_id_type=MESH)

        mine = pltpu.make_async_copy(v_ref, rows(*me), local_sem)
        mine.start()
        first = [copy(0, me, sibling, src=v_ref)]
        first += [copy(1 + j, me, (*chip, c), src=v_ref) for j, chip in enumerate(chips)]
        for cp in first:
            cp.start()
        passed = [copy(4 + j, (*chip, c), sibling) for j, chip in enumerate(chips)]
        for j, chip in enumerate(chips):
            copy(1 + j, (*chip, c), me).wait_recv()
            passed[j].start()
        copy(0, sibling, me).wait_recv()
        for j, chip in enumerate(chips):
            copy(4 + j, (*chip, 1 - c), me).wait_recv()
        for cp in first + passed:
            cp.wait_send()
        mine.wait()

    vmem = pl.BlockSpec(memory_space=pltpu.VMEM)
    return pl.pallas_call(
        body, name=name, in_specs=[vmem], out_specs=vmem, out_shape=jax.ShapeDtypeStruct((8 * m, n), v.dtype),
        scratch_shapes=[pltpu.SemaphoreType.DMA((7,)), pltpu.SemaphoreType.DMA((7,)), pltpu.SemaphoreType.DMA],
    )(v)


def _add_half(full, part, core, *, name, rows=256):
    P, R, C = full.shape
    half = R // 2
    rows = _tile(half, rows)
    nb = half // rows

    def body(core_ref, a_ref, b_ref, o_ref):
        o_ref[...] = (a_ref[...] + b_ref[...].astype(F32)).astype(BF16)

    spec = pl.BlockSpec((None, rows, C), lambda p, i, core_ref: (p, i, 0))
    grid_spec = pltpu.PrefetchScalarGridSpec(
        num_scalar_prefetch=1, grid=(P, nb),
        in_specs=[pl.BlockSpec((None, rows, C), lambda p, i, core_ref: (p, core_ref[0] * nb + i, 0)), spec], out_specs=spec)
    return pl.pallas_call(
        body, name=name, grid_spec=grid_spec, out_shape=jax.ShapeDtypeStruct((P, half, C), BF16),
        compiler_params=_params("parallel", "parallel"),
    )(core, full, part)


def _sum_parts(own, landed, where, *, name, rows=256):
    _, R, C = own.shape
    rows = _tile(R, rows)

    def body(where_ref, own_ref, land_ref, o_ref):
        acc = own_ref[...].astype(F32)
        for j in range(3):
            acc = acc + land_ref[j].astype(F32)
        o_ref[...] = acc

    grid_spec = pltpu.PrefetchScalarGridSpec(
        num_scalar_prefetch=1, grid=(R // rows,),
        in_specs=[pl.BlockSpec((None, rows, C), lambda i, where_ref: (where_ref[0], i, 0)),
                  pl.BlockSpec((3, rows, C), lambda i, where_ref: (0, i, 0))],
        out_specs=pl.BlockSpec((None, rows, C), lambda i, where_ref: (where_ref[1], i, 0)))
    return pl.pallas_call(
        body, name=name, grid_spec=grid_spec, out_shape=jax.ShapeDtypeStruct((2, R, C), F32),
        compiler_params=_params("parallel"),
    )(where, own, landed)


def _sum_slabs(a, *, name, rows=256):
    P, R, C = a.shape
    rows = _tile(R, rows)

    def body(a_ref, o_ref):
        acc = a_ref[0].astype(F32)
        for p in range(1, P):
            acc = acc + a_ref[p].astype(F32)
        o_ref[...] = acc

    return pl.pallas_call(
        body, name=name, grid=(R // rows,), in_specs=[pl.BlockSpec((P, rows, C), lambda i: (0, i, 0))],
        out_specs=pl.BlockSpec((rows, C), lambda i: (i, 0)),
        out_shape=jax.ShapeDtypeStruct((R, C), F32), compiler_params=_params("parallel"),
    )(a)


def _reduce_scatter(grads, grads16, *, name):
    core = lax.axis_index("c").astype(jnp.int32)
    chip = (2 * lax.axis_index("x") + lax.axis_index("y")).astype(jnp.int32)
    got = _swap_other_half(grads16, name=name + "_swap")
    chip_sums = [_add_half(g, r, core.reshape(1), name=f"{name}_add{w}") for w, (g, r) in enumerate(zip(grads, got))]
    landed = _chip_exchange(chip_sums, name=name + "_exchange")
    where = jnp.stack([chip, core])
    reduced = [_sum_parts(s, a, where, name=f"{name}_sum{w}") for w, (s, a) in enumerate(zip(chip_sums, landed))]
    both = _share_halves(reduced, name=name + "_share")
    return [b.reshape(2 * b.shape[1], b.shape[2]) for b in both]


def _to_heads(t, heads, dil=1):
    S = t.shape[0]
    d = t.shape[1] // heads
    return jnp.transpose(t.reshape(S // dil, dil, heads, d), (2, 1, 0, 3)).reshape(heads, S, d)


def _from_heads(t, dil=1):
    heads, S, d = t.shape
    return jnp.transpose(t.reshape(heads, dil, S // dil, d), (2, 1, 0, 3)).reshape(S, heads * d)


def _unstride(t, dil):
    heads, S, d = t.shape
    return jnp.transpose(t.reshape(heads, dil, S // dil, d), (0, 2, 1, 3)).reshape(heads, S, d)


def _restride(t, dil):
    heads, S, d = t.shape
    return jnp.transpose(t.reshape(heads, S // dil, dil, d), (0, 2, 1, 3)).reshape(heads, S, d)


def _pad_cols(t, width, padded):
    S = t.shape[0]
    return jnp.pad(t.reshape(S, N_CHIPS, width), ((0, 0), (0, 0), (0, padded - width))).reshape(S, N_CHIPS * padded)


def _unpad_cols(t, width, padded):
    S = t.shape[0]
    return t.reshape(S, N_CHIPS, padded)[:, :, :width].reshape(S, N_CHIPS * width)


def _alibi(n):
    return 2.0 ** (-8.0 * jnp.arange(1, n + 1, dtype=F32) / n)


B_KW = [dict(blk=ATT_BLK, nb_seq=SEQ // d // ATT_BLK, n_prev=1, n_back=w // d, scale=1.0 / math.sqrt(HEAD_DIM), dist_scale=d)
        for w, d in B_PATTERNS]
A_KW = dict(blk=ATT_BLK, nb_seq=SEQ // ATT_BLK, n_prev=1, n_back=A_WINDOW - 1, scale=1.0 / math.sqrt(HEAD_DIM), dist_scale=1)
D_BLK = 256
D_KW = dict(blk=D_BLK, nb_seq=SEQ // D_BLK, n_prev=SEQ // D_BLK, n_back=SEQ, scale=1.0 / math.sqrt(D_NOPE + D_ROPE), dist_scale=1)
C_KW = dict(blk=ATT_BLK, scale=1.0 / math.sqrt(HEAD_DIM))


def _rope_tables(reps):
    inv_freq = ROPE_BASE ** (-jnp.arange(0, D_ROPE, 2, dtype=F32) / D_ROPE)
    ang = jnp.arange(SEQ, dtype=F32)[:, None] * inv_freq[None, :]
    return jnp.tile(jnp.cos(ang), (1, reps)), jnp.tile(jnp.sin(ang), (1, reps))


def _mlp_fwd(x, x16, w1, w2, g, b, tag):
    hid, act = _matmul(x16, w1, mode="nn", out_dtype=BF16, epilogue="relu2", name=f"mlp{tag}_up")
    m = _matmul(act, w2, mode="nn", out_dtype=F32, name=f"mlp{tag}_down")
    y, y16, u = _norm_fwd(x, m, g, b, center=True, eps=LN_EPS, alpha=ALPHA, name=f"ln2_{tag}")
    return y, y16, (x16, hid, act, u)


def _mlp_bwd(dy, saved, w1, w2, g, tag):
    x16, hid, act, u = saved
    du, du16, dg, db = _norm_bwd(dy, u, g, center=True, eps=LN_EPS, name=f"ln2_{tag}_bwd")
    dw2 = _matmul_tn(act, du16, shards=1, name=f"mlp{tag}_dw2")
    dhid = _matmul(du16, w2, mode="nt", out_dtype=BF16, epilogue="drelu2", extra=hid, name=f"mlp{tag}_dact")
    dw1 = _matmul_tn(x16, dhid, shards=N_CHIPS, name=f"mlp{tag}_dw1")
    dx = _matmul(dhid, w1, mode="nt", out_dtype=F32, epilogue="add", extra=du, alpha=ALPHA, name=f"mlp{tag}_dx")
    return dx, dw1, dw2, dg, db


def _even_fwd(x16, w_in, sinks, w_out):
    S = x16.shape[0]
    h = _unpad_cols(_matmul(x16, w_in, mode="nn", out_dtype=BF16, name="even_in"), EVEN_IN // N_CHIPS, EVEN_IN_PAD)
    qa = _to_heads(h[:, :A_Q_W], A_Q_HEADS)
    ka = _to_heads(h[:, A_Q_W:A_Q_W + A_KV_W], A_KV_HEADS)
    va = _to_heads(h[:, A_Q_W + A_KV_W:A_Q_W + 2 * A_KV_W], A_KV_HEADS)
    slopes_a, slopes_b = _alibi(A_Q_HEADS), _alibi(B_HEADS)
    oa, lse_a = _attn_fwd(qa, ka, va, slopes_a, sinks, name="swa_fwd", **A_KW)
    base = A_Q_W + 2 * A_KV_W
    qkv_b, outs, lses = [], [], []
    for gi, (_, dil) in enumerate(B_PATTERNS):
        cols = h[:, base + gi * 3 * B_W:base + (gi + 1) * 3 * B_W]
        q, k, v = (_to_heads(cols[:, i * B_W:(i + 1) * B_W], B_HEADS, dil) for i in range(3))
        o, lse = _attn_fwd(q, k, v, slopes_b, None, name=f"dil{gi}_fwd", **B_KW[gi])
        qkv_b.append((q, k, v, o, lse))
        outs.append(_unstride(o, dil))
        lses.append(_unstride(lse, dil))
    ob = _mix_fwd(outs, lses, name="dil_mix")
    y16 = jnp.concatenate([_from_heads(oa), _from_heads(ob)], axis=-1).astype(BF16)
    mixed = _matmul(y16, w_out, mode="nn", out_dtype=F32, name="even_out")
    return mixed, (x16, qa, ka, va, oa, lse_a, qkv_b, outs, lses, y16)


def _even_bwd(dmix16, du, saved, w_in, sinks, w_out):
    x16, qa, ka, va, oa, lse_a, qkv_b, outs, lses, y16 = saved
    slopes_a, slopes_b = _alibi(A_Q_HEADS), _alibi(B_HEADS)
    dw_out = _matmul_tn(y16, dmix16, shards=N_CHIPS, name="even_dwout")
    dy = _matmul(dmix16, w_out, mode="nt", out_dtype=F32, name="even_dy")
    doa = _to_heads(dy[:, :A_Q_W], A_Q_HEADS)
    dob = _to_heads(dy[:, A_Q_W:], B_HEADS)
    dqa, dka, dva, dsink = _attn_bwd(qa, ka, va, oa, doa, lse_a, None, slopes_a, sinks, name="swa_bwd", **A_KW)
    douts, dlses = _mix_bwd(outs, lses, dob, name="dil_mix_bwd")
    parts = [_from_heads(dqa), _from_heads(dka), _from_heads(dva)]
    for gi, (_, dil) in enumerate(B_PATTERNS):
        q, k, v, o, lse = qkv_b[gi]
        dq, dk, dv = _attn_bwd(q, k, v, o, _restride(douts[gi], dil), lse, _restride(dlses[gi], dil), slopes_b, None,
                               name=f"dil{gi}_bwd", **B_KW[gi])
        parts += [_from_heads(dq, dil), _from_heads(dk, dil), _from_heads(dv, dil)]
    dh16 = _pad_cols(jnp.concatenate(parts, axis=-1), EVEN_IN // N_CHIPS, EVEN_IN_PAD).astype(BF16)
    dw_in = _matmul_tn(x16, dh16, shards=N_CHIPS, name="even_dwin")
    dx = _matmul(dh16, w_in, mode="nt", out_dtype=F32, epilogue="add", extra=du, alpha=ALPHA, name="even_dx")
    return dx, dw_in, dsink[:, 0, 0], dw_out


def _odd_fwd(x16, w_in, qn_g, kvn_g, w_uq, w_ukv, w_out):
    S = x16.shape[0]
    h = _unpad_cols(_matmul(x16, w_in, mode="nn", out_dtype=F32, name="odd_in"), ODD_IN // N_CHIPS, ODD_IN_PAD)
    qc, kc, vc = (_to_heads(h[:, i * C_W:(i + 1) * C_W].astype(BF16), C_HEADS) for i in range(3))
    cq = h[:, 3 * C_W:3 * C_W + D_Q_RANK]
    ckv = h[:, 3 * C_W + D_Q_RANK:3 * C_W + D_Q_RANK + D_KV_RANK]
    kr = h[:, 3 * C_W + D_Q_RANK + D_KV_RANK:]
    oc, tot = _stick_fwd(qc, kc, vc, name="stick_fwd", **C_KW)
    _, cqn16, _ = _norm_fwd(cq, None, qn_g, None, center=False, eps=RMS_EPS, alpha=1.0, name="q_rms")
    _, ckvn16, _ = _norm_fwd(ckv, None, kvn_g, None, center=False, eps=RMS_EPS, alpha=1.0, name="kv_rms")
    q = _matmul(cqn16, w_uq, mode="nn", out_dtype=F32, name="mla_uq").reshape(S, D_HEADS, D_NOPE + D_ROPE)
    kv = _matmul(ckvn16, w_ukv, mode="nn", out_dtype=F32, name="mla_ukv").reshape(S, D_HEADS, D_NOPE + D_V)
    half = D_ROPE // 2
    q_rope = q[..., D_NOPE:]
    x1 = jnp.concatenate([q_rope[..., :half].reshape(S, D_HEADS * half), kr[:, :half]], axis=-1)
    x2 = jnp.concatenate([q_rope[..., half:].reshape(S, D_HEADS * half), kr[:, half:]], axis=-1)
    cos, sin = _rope_tables(D_HEADS + 1)
    y1, y2 = _rope(x1[None], x2[None], cos, sin, name="rope_fwd")
    nq = D_HEADS * half
    q_rot = jnp.concatenate([y1[:, :nq].reshape(S, D_HEADS, half), y2[:, :nq].reshape(S, D_HEADS, half)], axis=-1)
    kr_rot = jnp.concatenate([y1[:, nq:], y2[:, nq:]], axis=-1)
    qd = jnp.transpose(jnp.concatenate([q[..., :D_NOPE], q_rot], axis=-1), (1, 0, 2)).astype(BF16)
    kd = jnp.transpose(jnp.concatenate([kv[..., :D_NOPE], jnp.broadcast_to(kr_rot[:, None, :], (S, D_HEADS, D_ROPE))], axis=-1),
                       (1, 0, 2)).astype(BF16)
    vd = jnp.transpose(kv[..., D_NOPE:], (1, 0, 2)).astype(BF16)
    od, lse_d = _attn_fwd(qd, kd, vd, None, None, name="mla_fwd", **D_KW)
    y16 = jnp.concatenate([_from_heads(oc), _from_heads(od)], axis=-1).astype(BF16)
    mixed = _matmul(y16, w_out, mode="nn", out_dtype=F32, name="odd_out")
    return mixed, (x16, qc, kc, vc, tot, cq, ckv, cqn16, ckvn16, qd, kd, vd, od, lse_d, y16)


def _odd_bwd(dmix16, du, saved, w_in, qn_g, kvn_g, w_uq, w_ukv, w_out):
    x16, qc, kc, vc, tot, cq, ckv, cqn16, ckvn16, qd, kd, vd, od, lse_d, y16 = saved
    S = x16.shape[0]
    half = D_ROPE // 2
    dw_out = _matmul_tn(y16, dmix16, shards=1, name="odd_dwout")
    dy = _matmul(dmix16, w_out, mode="nt", out_dtype=F32, name="odd_dy")
    doc = _to_heads(dy[:, :C_W], C_HEADS)
    dod = _to_heads(dy[:, C_W:], D_HEADS)
    dqc, dkc, dvc = _stick_bwd(qc, kc, vc, tot, doc, name="stick_bwd", **C_KW)
    dqd, dkd, dvd = _attn_bwd(qd, kd, vd, od, dod, lse_d, None, None, None, name="mla_bwd", **D_KW)
    cos, sin = _rope_tables(D_HEADS + 1)
    nq = D_HEADS * half
    gq1 = jnp.transpose(dqd[..., D_NOPE:D_NOPE + half], (1, 0, 2)).reshape(1, S, nq)
    gq2 = jnp.transpose(dqd[..., D_NOPE + half:], (1, 0, 2)).reshape(1, S, nq)
    dq1, dq2 = _rope(gq1, gq2, cos[:, :nq], -sin[:, :nq], name="rope_bwd_q")
    dk1, dk2 = _rope(dkd[..., D_NOPE:D_NOPE + half], dkd[..., D_NOPE + half:], cos[:, nq:], -sin[:, nq:], name="rope_bwd_k")
    dq_full = jnp.concatenate([jnp.transpose(dqd[..., :D_NOPE], (1, 0, 2)), dq1.reshape(S, D_HEADS, half),
                               dq2.reshape(S, D_HEADS, half)], axis=-1).reshape(S, D_HEADS * (D_NOPE + D_ROPE)).astype(BF16)
    dkv_full = jnp.concatenate([jnp.transpose(dkd[..., :D_NOPE], (1, 0, 2)), jnp.transpose(dvd, (1, 0, 2))],
                               axis=-1).reshape(S, D_HEADS * (D_NOPE + D_V)).astype(BF16)
    dw_uq = _matmul_tn(cqn16, dq_full, shards=N_CHIPS, name="mla_dwuq")
    dw_ukv = _matmul_tn(ckvn16, dkv_full, shards=N_CHIPS, name="mla_dwukv")
    dcqn = _matmul(dq_full, w_uq, mode="nt", out_dtype=F32, name="mla_dcq")
    dckvn = _matmul(dkv_full, w_ukv, mode="nt", out_dtype=F32, name="mla_dckv")
    dcq, _, dqn_g, _ = _norm_bwd(dcqn, cq, qn_g, center=False, eps=RMS_EPS, name="q_rms_bwd")
    dckv, _, dkvn_g, _ = _norm_bwd(dckvn, ckv, kvn_g, center=False, eps=RMS_EPS, name="kv_rms_bwd")
    dh = jnp.concatenate([_from_heads(dqc), _from_heads(dkc), _from_heads(dvc), dcq, dckv, dk1, dk2], axis=-1)
    dh16 = _pad_cols(dh, ODD_IN // N_CHIPS, ODD_IN_PAD).astype(BF16)
    dw_in = _matmul_tn(x16, dh16, shards=N_CHIPS, name="odd_dwin")
    dx = _matmul(dh16, w_in, mode="nt", out_dtype=F32, epilogue="add", extra=du, alpha=ALPHA, name="odd_dx")
    return dx, dw_in, dqn_g[0], dkvn_g[0], dw_uq, dw_ukv, dw_out


def _local_step(x, target, w, small):
    x16 = x.astype(BF16)
    mixed0, sv_e = _even_fwd(x16, w["even_w_in"], small["even_sinks"], w["even_w_out"])
    x1, x1_16, u01 = _norm_fwd(x, mixed0, small["ln1_g"][0], small["ln1_b"][0], center=True, eps=LN_EPS, alpha=ALPHA, name="ln1_0")
    x2, x2_16, sv_m0 = _mlp_fwd(x1, x1_16, w["mlp_w1"][0], w["mlp_w2"][0], small["ln2_g"][0], small["ln2_b"][0], 0)
    mixed1, sv_o = _odd_fwd(x2_16, w["odd_w_in"], small["odd_q_norm_g"], small["odd_kv_norm_g"], w["odd_w_uq"], w["odd_w_ukv"],
                            w["odd_w_out"])
    x3, x3_16, u11 = _norm_fwd(x2, mixed1, small["ln1_g"][1], small["ln1_b"][1], center=True, eps=LN_EPS, alpha=ALPHA, name="ln1_1")
    x4, _, sv_m1 = _mlp_fwd(x3, x3_16, w["mlp_w1"][1], w["mlp_w2"][1], small["ln2_g"][1], small["ln2_b"][1], 1)
    dy, loss = _loss_head(x4, target, name="loss_head")

    dx3, dw1_1, dw2_1, dln2g_1, dln2b_1 = _mlp_bwd(dy, sv_m1, w["mlp_w1"][1], w["mlp_w2"][1], small["ln2_g"][1], 1)
    du, du16, dln1g_1, dln1b_1 = _norm_bwd(dx3, u11, small["ln1_g"][1], center=True, eps=LN_EPS, name="ln1_1_bwd")
    dx2, dwin_o, dqn_g, dkvn_g, dwuq, dwukv, dwout_o = _odd_bwd(
        du16, du, sv_o, w["odd_w_in"], small["odd_q_norm_g"], small["odd_kv_norm_g"], w["odd_w_uq"], w["odd_w_ukv"], w["odd_w_out"])
    dx1, dw1_0, dw2_0, dln2g_0, dln2b_0 = _mlp_bwd(dx2, sv_m0, w["mlp_w1"][0], w["mlp_w2"][0], small["ln2_g"][0], 0)
    du, du16, dln1g_0, dln1b_0 = _norm_bwd(dx1, u01, small["ln1_g"][0], center=True, eps=LN_EPS, name="ln1_0_bwd")
    dx0, dwin_e, dsinks, dwout_e = _even_bwd(du16, du, sv_e, w["even_w_in"], small["even_sinks"], w["even_w_out"])

    def by_rows(pair, rows):
        return tuple(t.reshape(N_CHIPS, rows // N_CHIPS, D_MODEL) for t in pair)

    big = dict(even_w_in=dwin_e, even_w_out=dwout_e, odd_w_in=dwin_o, odd_w_uq=dwuq, odd_w_ukv=dwukv,
               odd_w_out=by_rows(dwout_o, D_MODEL), mlp_w1_0=dw1_0, mlp_w1_1=dw1_1,
               mlp_w2_0=by_rows(dw2_0, D_FF), mlp_w2_1=by_rows(dw2_1, D_FF))
    sm = dict(even_sinks=dsinks, odd_q_norm_g=dqn_g, odd_kv_norm_g=dkvn_g,
              ln1_g=jnp.concatenate([dln1g_0, dln1g_1]), ln1_b=jnp.concatenate([dln1b_0, dln1b_1]),
              ln2_g=jnp.concatenate([dln2g_0, dln2g_1]), ln2_b=jnp.concatenate([dln2b_0, dln2b_1]))
    return loss, dx0, big, sm


SMALL_ORDER = ("ln1_g", "ln1_b", "ln2_g", "ln2_b", "even_sinks", "odd_q_norm_g", "odd_kv_norm_g")
SMALL_COLS = 2176


def _pack_small(parts):
    flat = jnp.concatenate([p.reshape(-1) for p in parts])
    return jnp.pad(flat, (0, 8 * SMALL_COLS - flat.shape[0])).reshape(8, SMALL_COLS)


def _unpack_small(packed, shapes):
    flat = packed.reshape(-1)
    out, pos = [], 0
    for s in shapes:
        n = math.prod(s)
        out.append(flat[pos:pos + n].reshape(s))
        pos += n
    return out


def kernel(x, even_w_in, even_sinks, even_w_out, odd_w_in, odd_q_norm_g, odd_kv_norm_g, odd_w_uq, odd_w_ukv, odd_w_out, ln1_g, ln1_b, mlp_w1, mlp_w2, ln2_g, ln2_b, loss_target, m_even_w_in, m_even_sinks, m_even_w_out, m_odd_w_in, m_odd_q_norm_g, m_odd_kv_norm_g, m_odd_w_uq, m_odd_w_ukv, m_odd_w_out, m_ln1_g, m_ln1_b, m_mlp_w1, m_mlp_w2, m_ln2_g, m_ln2_b, v_even_w_in, v_even_sinks, v_even_w_out, v_odd_w_in, v_odd_q_norm_g, v_odd_kv_norm_g, v_odd_w_uq, v_odd_w_ukv, v_odd_w_out, v_ln1_g, v_ln1_b, v_mlp_w1, v_mlp_w2, v_ln2_g, v_ln2_b):
    chip = 2 * lax.axis_index("x") + lax.axis_index("y")
    e_w, o_w = EVEN_IN // N_CHIPS, ODD_IN // N_CHIPS

    shards = dict(
        even_w_in=jnp.pad(even_w_in[0], ((0, 0), (0, EVEN_IN_PAD - e_w))), even_w_out=even_w_out[0],
        odd_w_in=jnp.pad(odd_w_in[0], ((0, 0), (0, ODD_IN_PAD - o_w))), odd_w_uq=odd_w_uq[0], odd_w_ukv=odd_w_ukv[0],
        odd_w_out=odd_w_out[0], mlp_w1_0=mlp_w1[0], mlp_w1_1=mlp_w1[1], mlp_w2_0=mlp_w2[0], mlp_w2_1=mlp_w2[1])
    names = list(shards)
    own16 = [shards[n].astype(BF16) for n in names]
    gathered = {n: lax.dynamic_update_slice(g, o[None], (chip, 0, 0))
                for n, g, o in zip(names, _gather_weights(own16, name="gather_weights"), own16)}
    w = dict(even_w_in=gathered["even_w_in"], even_w_out=gathered["even_w_out"], odd_w_in=gathered["odd_w_in"],
             odd_w_uq=gathered["odd_w_uq"], odd_w_ukv=gathered["odd_w_ukv"],
             odd_w_out=gathered["odd_w_out"].reshape(1, D_MODEL, D_MODEL),
             mlp_w1=[gathered["mlp_w1_0"], gathered["mlp_w1_1"]],
             mlp_w2=[gathered["mlp_w2_0"].reshape(1, D_FF, D_MODEL), gathered["mlp_w2_1"].reshape(1, D_FF, D_MODEL)])
    norm_rows = jnp.pad(jnp.concatenate([odd_q_norm_g[0], odd_kv_norm_g[0]])[None], ((0, 7), (0, 64)))
    norm_all = _all_gather8(norm_rows, name="gather_norm_gains")[0::16]
    qn_w, kvn_w = D_Q_RANK // N_CHIPS, D_KV_RANK // N_CHIPS
    small = dict(even_sinks=even_sinks[0], odd_q_norm_g=norm_all[:, :qn_w].reshape(D_Q_RANK),
                 odd_kv_norm_g=norm_all[:, qn_w:qn_w + kvn_w].reshape(D_KV_RANK), ln1_g=ln1_g, ln1_b=ln1_b, ln2_g=ln2_g, ln2_b=ln2_b)

    loss_row, grad_x, big, sm = _local_step(x[0], loss_target[0], w, small)
    loss = lax.psum(loss_row[0, 0], ("x", "y", "c"))

    reduced = dict(zip(names, _reduce_scatter([big[n][0] for n in names], [big[n][1] for n in names], name="grad_rs")))
    every = _all_gather8(_pack_small([sm[n] for n in SMALL_ORDER]), name="gather_small_grads")
    sm_sum = _sum_slabs(every.reshape(8, 8, SMALL_COLS), name="sum_small_grads")
    g_ln1g, g_ln1b, g_ln2g, g_ln2b, g_sinks, g_qn, g_kvn = _unpack_small(
        sm_sum, [(DEPTH, D_MODEL)] * 4 + [(1, A_Q_HEADS), (D_Q_RANK,), (D_KV_RANK,)])
    grads = dict(
        even_w_in=reduced["even_w_in"][:, :e_w][None], even_sinks=g_sinks, even_w_out=reduced["even_w_out"][None],
        odd_w_in=reduced["odd_w_in"][:, :o_w][None], odd_q_norm_g=lax.dynamic_slice_in_dim(g_qn, chip * qn_w, qn_w)[None],
        odd_kv_norm_g=lax.dynamic_slice_in_dim(g_kvn, chip * kvn_w, kvn_w)[None], odd_w_uq=reduced["odd_w_uq"][None],
        odd_w_ukv=reduced["odd_w_ukv"][None], odd_w_out=reduced["odd_w_out"][None], ln1_g=g_ln1g, ln1_b=g_ln1b,
        mlp_w1=jnp.stack([reduced["mlp_w1_0"], reduced["mlp_w1_1"]]), mlp_w2=jnp.stack([reduced["mlp_w2_0"], reduced["mlp_w2_1"]]),
        ln2_g=g_ln2g, ln2_b=g_ln2b)

    weights = dict(even_w_in=even_w_in, even_sinks=even_sinks, even_w_out=even_w_out, odd_w_in=odd_w_in, odd_q_norm_g=odd_q_norm_g,
                   odd_kv_norm_g=odd_kv_norm_g, odd_w_uq=odd_w_uq, odd_w_ukv=odd_w_ukv, odd_w_out=odd_w_out, ln1_g=ln1_g, ln1_b=ln1_b,
                   mlp_w1=mlp_w1, mlp_w2=mlp_w2, ln2_g=ln2_g, ln2_b=ln2_b)
    m_in = dict(even_w_in=m_even_w_in, even_sinks=m_even_sinks, even_w_out=m_even_w_out, odd_w_in=m_odd_w_in,
                odd_q_norm_g=m_odd_q_norm_g, odd_kv_norm_g=m_odd_kv_norm_g, odd_w_uq=m_odd_w_uq, odd_w_ukv=m_odd_w_ukv,
                odd_w_out=m_odd_w_out, ln1_g=m_ln1_g, ln1_b=m_ln1_b, mlp_w1=m_mlp_w1, mlp_w2=m_mlp_w2, ln2_g=m_ln2_g, ln2_b=m_ln2_b)
    v_in = dict(even_w_in=v_even_w_in, even_sinks=v_even_sinks, even_w_out=v_even_w_out, odd_w_in=v_odd_w_in,
                odd_q_norm_g=v_odd_q_norm_g, odd_kv_norm_g=v_odd_kv_norm_g, odd_w_uq=v_odd_w_uq, odd_w_ukv=v_odd_w_ukv,
                odd_w_out=v_odd_w_out, ln1_g=v_ln1_g, ln1_b=v_ln1_b, mlp_w1=v_mlp_w1, mlp_w2=v_mlp_w2, ln2_g=v_ln2_g, ln2_b=v_ln2_b)
    order = list(weights)
    delta, new_m, new_v = {}, {}, {}
    for n in order:
        shape = weights[n].shape
        flat = (math.prod(shape[:-1]), shape[-1])
        d, mm, vv = _adamw(weights[n].reshape(flat), grads[n].reshape(flat), m_in[n].reshape(flat), v_in[n].reshape(flat),
                           name=f"adamw_{n}")
        delta[n], new_m[n], new_v[n] = d.reshape(shape), mm.reshape(shape), vv.reshape(shape)
    return (loss, grad_x[None], *[grads[n] for n in order], *[delta[n] for n in order], *[new_m[n] for n in order],
            *[new_v[n] for n in order])
```

```python
import functools
import math

import jax
import jax.numpy as jnp
from jax import lax
from jax.experimental import pallas as pl
from jax.experimental.pallas import tpu as pltpu

F32 = jnp.float32
BF16 = jnp.bfloat16
MESH = pl.DeviceIdType.MESH

D_MODEL = 2048
SEQ = 2048
DEPTH = 2
HEAD_DIM = 64
A_Q_HEADS, A_KV_HEADS, A_WINDOW = 16, 2, 128
B_HEADS = 8
B_PATTERNS = ((128, 1), (512, 4), (2048, 16))
C_HEADS = 16
D_HEADS, D_Q_RANK, D_KV_RANK, D_NOPE, D_ROPE, D_V = 16, 512, 256, 64, 32, 64
ROPE_BASE = 10000.0
D_FF = 4 * D_MODEL
LN_EPS = 1e-5
RMS_EPS = 1e-6
ALPHA = (2 * DEPTH) ** 0.25
A_Q_W = A_Q_HEADS * HEAD_DIM
A_KV_W = A_KV_HEADS * HEAD_DIM
B_W = B_HEADS * HEAD_DIM
EVEN_IN = A_Q_W + 2 * A_KV_W + 3 * B_W * len(B_PATTERNS)
EVEN_OUT = A_Q_W + B_W
C_W = C_HEADS * HEAD_DIM
ODD_IN = 3 * C_W + D_Q_RANK + D_KV_RANK + D_ROPE
ADAM_LR, ADAM_B1, ADAM_B2, ADAM_EPS, ADAM_WD, ADAM_STEP = 0.001, 0.9, 0.999, 1e-08, 0.01, 10

N_CHIPS = 4
EVEN_IN_PAD = 1536
ODD_IN_PAD = 1024
ATT_BLK = 128
NEG = -1e30
V7X_VMEM_LIMIT = 48 * 1024 * 1024


def _params(*sem):
    return pltpu.CompilerParams(dimension_semantics=sem, vmem_limit_bytes=V7X_VMEM_LIMIT)


def _tile(n, cap):
    if n <= cap:
        return n
    t = cap - cap % 128
    while n % t:
        t -= 128
    return t


def _matmul(a, b, *, mode, out_dtype, name, epilogue=None, extra=None, alpha=1.0, tm=1024, tn=512, tk=2048):
    if mode == "nn":
        M, K = a.shape
        P, _, Np = b.shape
        tm, tn, tk = _tile(M, tm), _tile(Np, tn), _tile(K, tk)
        nj = Np // tn
        grid = (M // tm, P * nj, K // tk)
        a_spec = pl.BlockSpec((tm, tk), lambda i, j, k: (i, k))
        b_spec = pl.BlockSpec((None, tk, tn), lambda i, j, k: (j // nj, k, j % nj))
        o_spec = pl.BlockSpec((tm, tn), lambda i, j, k: (i, j))
        out_shape = (M, P * Np)
        dims = (((1,), (0,)), ((), ()))
    elif mode == "nt":
        M, N = a.shape
        P, K, Np = b.shape
        tm, tn, tk = _tile(M, tm), _tile(K, tn), _tile(Np, tk)
        nkk = Np // tk
        grid = (M // tm, K // tn, P * nkk)
        a_spec = pl.BlockSpec((tm, tk), lambda i, j, k: (i, k))
        b_spec = pl.BlockSpec((None, tn, tk), lambda i, j, k: (k // nkk, j, k % nkk))
        o_spec = pl.BlockSpec((tm, tn), lambda i, j, k: (i, j))
        out_shape = (M, K)
        dims = (((1,), (1,)), ((), ()))
    else:
        raise ValueError(mode)
    n_k = grid[2]
    n_extra = 0 if extra is None else 1
    n_out = 2 if epilogue == "relu2" else 1

    def body(*refs):
        a_ref, b_ref = refs[:2]
        e_ref = refs[2] if n_extra else None
        outs = refs[2 + n_extra:2 + n_extra + n_out]
        part = lax.dot_general(a_ref[...], b_ref[...], dims, preferred_element_type=F32)

        def finish(r):
            if epilogue == "relu2":
                outs[0][...] = r.astype(outs[0].dtype)
                rp = jnp.maximum(r, 0.0)
                outs[1][...] = (rp * rp).astype(outs[1].dtype)
            elif epilogue == "drelu2":
                outs[0][...] = (r * (2.0 * jnp.maximum(e_ref[...].astype(F32), 0.0))).astype(outs[0].dtype)
            elif epilogue == "add":
                outs[0][...] = (r + alpha * e_ref[...].astype(F32)).astype(outs[0].dtype)
            else:
                outs[0][...] = r.astype(outs[0].dtype)

        if n_k == 1:
            finish(part)
        else:
            acc = refs[-1]
            k = pl.program_id(2)

            @pl.when(k == 0)
            def _():
                acc[...] = part

            @pl.when((k > 0) & (k < n_k - 1))
            def _():
                acc[...] += part

            @pl.when(k == n_k - 1)
            def _():
                finish(acc[...] + part)

    in_specs = [a_spec, b_spec] + ([o_spec] if n_extra else [])
    shapes = [jax.ShapeDtypeStruct(out_shape, out_dtype)] * n_out
    res = pl.pallas_call(
        body, name=name, grid=grid, in_specs=in_specs,
        out_specs=[o_spec] * n_out, out_shape=shapes,
        scratch_shapes=[pltpu.VMEM((tm, tn), F32)] if n_k > 1 else [],
        compiler_params=_params("parallel", "parallel", "arbitrary"),
    )(a, b, *([extra] if n_extra else []))
    return res if n_out > 1 else res[0]


def _matmul_tn(a, g, *, shards, name, tm=1024, tn=512):
    M, K = a.shape
    P = shards
    Np = g.shape[1] // P
    tm, tn = _tile(K, tm), _tile(Np, tn)
    nj = Np // tn

    def body(a_ref, g_ref, o_ref, o16_ref):
        r = lax.dot_general(a_ref[...], g_ref[...], (((0,), (0,)), ((), ())), preferred_element_type=F32)
        o_ref[...] = r
        o16_ref[...] = r.astype(BF16)

    o_spec = pl.BlockSpec((None, tm, tn), lambda i, j: (j // nj, i, j % nj))
    return pl.pallas_call(
        body, name=name, grid=(K // tm, P * nj),
        in_specs=[pl.BlockSpec((M, tm), lambda i, j: (0, i)), pl.BlockSpec((M, tn), lambda i, j: (0, j))],
        out_specs=[o_spec, o_spec],
        out_shape=[jax.ShapeDtypeStruct((P, K, Np), F32), jax.ShapeDtypeStruct((P, K, Np), BF16)],
        compiler_params=_params("parallel", "parallel"),
    )(a, g)


def _norm_fwd(x, res, g, b, *, center, eps, alpha, name, rows=256):
    S, W = x.shape
    has_res = res is not None
    rows = _tile(S, rows)

    def body(*refs):
        x_ref = refs[0]
        r_ref = refs[1] if has_res else None
        g_ref = refs[1 + has_res]
        b_ref = refs[2 + has_res] if center else None
        y_ref, y16_ref, u_ref = refs[-3:]
        u = x_ref[...]
        if has_res:
            u = alpha * u + r_ref[...]
        if center:
            mu = jnp.mean(u, axis=1, keepdims=True)
            xc = u - mu
        else:
            xc = u
        var = jnp.mean(xc * xc, axis=1, keepdims=True)
        y = xc * lax.rsqrt(var + eps) * g_ref[...]
        if center:
            y = y + b_ref[...]
        y_ref[...] = y
        y16_ref[...] = y.astype(BF16)
        u_ref[...] = u

    row_spec = pl.BlockSpec((rows, W), lambda i: (i, 0))
    vec_spec = pl.BlockSpec((1, W), lambda i: (0, 0))
    ins = [x] + ([res] if has_res else []) + [g.reshape(1, W)] + ([b.reshape(1, W)] if center else [])
    specs = [row_spec] + ([row_spec] if has_res else []) + [vec_spec] + ([vec_spec] if center else [])
    return pl.pallas_call(
        body, name=name, grid=(S // rows,), in_specs=specs,
        out_specs=[row_spec, row_spec, row_spec],
        out_shape=[jax.ShapeDtypeStruct((S, W), F32), jax.ShapeDtypeStruct((S, W), BF16), jax.ShapeDtypeStruct((S, W), F32)],
        compiler_params=_params("parallel"),
    )(*ins)


def _norm_bwd(dy, u, g, *, center, eps, name, rows=256):
    S, W = u.shape
    rows = _tile(S, rows)

    def body(dy_ref, u_ref, g_ref, du_ref, du16_ref, dg_ref, db_ref):
        i = pl.program_id(0)
        uu = u_ref[...]
        dyv = dy_ref[...]
        if center:
            xc = uu - jnp.mean(uu, axis=1, keepdims=True)
        else:
            xc = uu
        rstd = lax.rsqrt(jnp.mean(xc * xc, axis=1, keepdims=True) + eps)
        xhat = xc * rstd
        dxh = dyv * g_ref[...]
        c2 = jnp.mean(dxh * xhat, axis=1, keepdims=True)
        du = dxh - xhat * c2
        if center:
            du = du - jnp.mean(dxh, axis=1, keepdims=True)
        du = du * rstd
        du_ref[...] = du
        du16_ref[...] = du.astype(BF16)

        @pl.when(i == 0)
        def _():
            dg_ref[...] = jnp.zeros_like(dg_ref)
            db_ref[...] = jnp.zeros_like(db_ref)

        dg_ref[...] += jnp.sum(dyv * xhat, axis=0, keepdims=True)
        db_ref[...] += jnp.sum(dyv, axis=0, keepdims=True)

    row_spec = pl.BlockSpec((rows, W), lambda i: (i, 0))
    vec_spec = pl.BlockSpec((1, W), lambda i: (0, 0))
    return pl.pallas_call(
        body, name=name, grid=(S // rows,), in_specs=[row_spec, row_spec, vec_spec],
        out_specs=[row_spec, row_spec, vec_spec, vec_spec],
        out_shape=[jax.ShapeDtypeStruct((S, W), F32), jax.ShapeDtypeStruct((S, W), BF16),
                   jax.ShapeDtypeStruct((1, W), F32), jax.ShapeDtypeStruct((1, W), F32)],
        compiler_params=_params("arbitrary"),
    )(dy, u, g.reshape(1, W))


def _loss_head(y, target, *, name, rows=256):
    S, W = y.shape
    rows = _tile(S, rows)

    def body(y_ref, t_ref, dy_ref, l_ref):
        i = pl.program_id(0)
        e = y_ref[...] - t_ref[...]
        dy_ref[...] = e * (1.0 / W)

        @pl.when(i == 0)
        def _():
            l_ref[...] = jnp.zeros_like(l_ref)

        l_ref[...] += jnp.full(l_ref.shape, 0.5 / W * jnp.sum(e * e), F32)

    row_spec = pl.BlockSpec((rows, W), lambda i: (i, 0))
    return pl.pallas_call(
        body, name=name, grid=(S // rows,), in_specs=[row_spec, row_spec],
        out_specs=[row_spec, pl.BlockSpec((1, 128), lambda i: (0, 0))],
        out_shape=[jax.ShapeDtypeStruct((S, W), F32), jax.ShapeDtypeStruct((1, 128), F32)],
        compiler_params=_params("arbitrary"),
    )(y, target)


def _adamw(w, g, m, v, *, name, rows=256):
    R, C = w.shape
    rows = _tile(R, rows) if R % 8 == 0 else R
    c1 = 1.0 / (1.0 - ADAM_B1 ** ADAM_STEP)
    c2 = 1.0 / (1.0 - ADAM_B2 ** ADAM_STEP)

    def body(w_ref, g_ref, m_ref, v_ref, d_ref, mo_ref, vo_ref):
        gg = g_ref[...]
        mn = ADAM_B1 * m_ref[...] + (1.0 - ADAM_B1) * gg
        vn = ADAM_B2 * v_ref[...] + (1.0 - ADAM_B2) * (gg * gg)
        d_ref[...] = -ADAM_LR * ((mn * c1) / (jnp.sqrt(vn * c2) + ADAM_EPS) + ADAM_WD * w_ref[...])
        mo_ref[...] = mn
        vo_ref[...] = vn

    spec = pl.BlockSpec((rows, C), lambda i: (i, 0))
    shp = jax.ShapeDtypeStruct((R, C), F32)
    return pl.pallas_call(
        body, name=name, grid=(R // rows,), in_specs=[spec] * 4, out_specs=[spec] * 3, out_shape=[shp] * 3,
        compiler_params=_params("parallel"),
    )(w, g, m, v)


def _rope(x1, x2, cos, sin, *, name, rows=512):
    H, S, W = x1.shape
    rows = _tile(S, rows)

    def body(x1_ref, x2_ref, c_ref, s_ref, y1_ref, y2_ref):
        t1 = jnp.sum(x1_ref[...], axis=0)
        t2 = jnp.sum(x2_ref[...], axis=0)
        c = c_ref[...]
        s = s_ref[...]
        y1_ref[...] = t1 * c - t2 * s
        y2_ref[...] = t1 * s + t2 * c

    xs = pl.BlockSpec((H, rows, W), lambda i: (0, i, 0))
    ts = pl.BlockSpec((rows, W), lambda i: (i, 0))
    shp = jax.ShapeDtypeStruct((S, W), F32)
    return pl.pallas_call(
        body, name=name, grid=(S // rows,), in_specs=[xs, xs, ts, ts], out_specs=[ts, ts], out_shape=[shp, shp],
        compiler_params=_params("parallel"),
    )(x1, x2, cos, sin)


def _band_mask(blk, n, n_back):
    row = lax.broadcasted_iota(jnp.int32, (blk, 2 * blk), 0)
    col = lax.broadcasted_iota(jnp.int32, (blk, 2 * blk), 1)
    rel = blk + row - col
    return rel, (rel >= 0) & (rel <= n_back) & ((col >= blk) | (n >= 1))


def _band_fwd(q, k, v, slopes, sinks, *, hps, nb_seq, n_back, scale, dist_scale, name):
    blk = ATT_BLK
    Hq, T, d = q.shape
    Hk = k.shape[0]
    group = Hq // Hk
    kps = max(hps // group, 1)
    use_sink = sinks is not None

    def body(*refs):
        q_ref, k_ref, v_ref, slope_ref = refs[:4]
        sink_ref = refs[4] if use_sink else None
        o_ref, lse_ref = refs[-2:]
        i = pl.program_id(1)
        start = pl.multiple_of(i * blk, blk)
        rel, valid = _band_mask(blk, i % nb_seq, n_back)
        relf = rel.astype(F32)
        for hh in range(hps):
            h = pl.program_id(0) * hps + hh
            kk = k_ref[hh // group, pl.ds(start, 2 * blk), :]
            vv = v_ref[hh // group, pl.ds(start, 2 * blk), :]
            s = lax.dot_general(q_ref[hh], kk, (((1,), (1,)), ((), ())), preferred_element_type=F32) * scale
            s = jnp.where(valid, s - (slope_ref[h] * dist_scale) * relf, NEG)
            m = jnp.max(s, axis=1, keepdims=True)
            if use_sink:
                m = jnp.maximum(m, sink_ref[h])
            p = jnp.where(valid, jnp.exp(s - m), 0.0)
            l = jnp.sum(p, axis=1, keepdims=True)
            if use_sink:
                l = l + jnp.exp(sink_ref[h] - m)
            o_ref[hh] = jnp.dot(p.astype(BF16), vv, preferred_element_type=F32) / l
            lse_ref[hh] = m + jnp.log(l)

    smem = pl.BlockSpec(memory_space=pltpu.SMEM)
    qs = pl.BlockSpec((hps, blk, d), lambda g, i: (g, i, 0))
    ks = pl.BlockSpec((kps, T + blk, d), lambda g, i: (g, 0, 0))
    ins = [q, k, v, slopes] + ([sinks] if use_sink else [])
    return pl.pallas_call(
        body, name=name, grid=(Hq // hps, T // blk), in_specs=[qs, ks, ks, smem] + ([smem] if use_sink else []),
        out_specs=[qs, pl.BlockSpec((hps, blk, 1), lambda g, i: (g, i, 0))],
        out_shape=[jax.ShapeDtypeStruct((Hq, T, d), F32), jax.ShapeDtypeStruct((Hq, T, 1), F32)],
        compiler_params=_params("parallel", "parallel"),
    )(*ins)


def _band_bwd(q, k, v, o, do, lse, dlse, slopes, sinks, *, hps, nb_seq, n_back, scale, dist_scale, name):
    blk = ATT_BLK
    Hq, T, d = q.shape
    Hk = k.shape[0]
    group = Hq // Hk
    kps = max(hps // group, 1)
    use_sink, use_dlse = sinks is not None, dlse is not None

    def body(*refs):
        q_ref, k_ref, v_ref, o_ref, do_ref, lse_ref = refs[:6]
        pos = 6
        dlse_ref = refs[pos] if use_dlse else None
        pos += use_dlse
        slope_ref = refs[pos]
        pos += 1
        sink_ref = refs[pos] if use_sink else None
        pos += use_sink
        dq_ref, dk_ref, dv_ref = refs[pos:pos + 3]
        dsink_ref = refs[pos + 3] if use_sink else None
        i = pl.program_id(1)
        start = pl.multiple_of(i * blk, blk)
        rel, valid = _band_mask(blk, i % nb_seq, n_back)
        relf = rel.astype(F32)

        @pl.when(i == 0)
        def _():
            dk_ref[...] = jnp.zeros_like(dk_ref)
            dv_ref[...] = jnp.zeros_like(dv_ref)
            if use_sink:
                dsink_ref[...] = jnp.zeros_like(dsink_ref)

        for kh in range(kps):
            dk_acc = jnp.zeros((2 * blk, d), F32)
            dv_acc = jnp.zeros((2 * blk, d), F32)
            kk = k_ref[kh, pl.ds(start, 2 * blk), :]
            vv = v_ref[kh, pl.ds(start, 2 * blk), :]
            for hh in range(kh * min(group, hps), (kh + 1) * min(group, hps)):
                h = pl.program_id(0) * hps + hh
                qb = q_ref[hh]
                dob = do_ref[hh]
                do16 = dob.astype(BF16)
                lse = lse_ref[hh]
                c = -jnp.sum(dob * o_ref[hh], axis=1, keepdims=True)
                if use_dlse:
                    c = c + dlse_ref[hh]
                s = lax.dot_general(qb, kk, (((1,), (1,)), ((), ())), preferred_element_type=F32) * scale
                s = jnp.where(valid, s - (slope_ref[h] * dist_scale) * relf, NEG)
                p = jnp.where(valid, jnp.exp(s - lse), 0.0)
                dp = lax.dot_general(do16, vv, (((1,), (1,)), ((), ())), preferred_element_type=F32)
                ds16 = (p * (dp + c) * scale).astype(BF16)
                dq_ref[hh] = jnp.dot(ds16, kk, preferred_element_type=F32)
                dk_acc = dk_acc + lax.dot_general(ds16, qb, (((0,), (0,)), ((), ())), preferred_element_type=F32)
                dv_acc = dv_acc + lax.dot_general(p.astype(BF16), do16, (((0,), (0,)), ((), ())), preferred_element_type=F32)
                if use_sink:
                    dsink_ref[hh] += jnp.full((1, 128), jnp.sum(jnp.exp(sink_ref[h] - lse) * c), F32)
            dk_ref[kh, pl.ds(start, 2 * blk), :] += dk_acc
            dv_ref[kh, pl.ds(start, 2 * blk), :] += dv_acc

    smem = pl.BlockSpec(memory_space=pltpu.SMEM)
    qs = pl.BlockSpec((hps, blk, d), lambda g, i: (g, i, 0))
    ls = pl.BlockSpec((hps, blk, 1), lambda g, i: (g, i, 0))
    ks = pl.BlockSpec((kps, T + blk, d), lambda g, i: (g, 0, 0))
    in_specs = [qs, ks, ks, qs, qs, ls] + ([ls] if use_dlse else []) + [smem] + ([smem] if use_sink else [])
    ins = [q, k, v, o, do, lse] + ([dlse] if use_dlse else []) + [slopes] + ([sinks] if use_sink else [])
    out_specs = [qs, ks, ks]
    out_shape = [jax.ShapeDtypeStruct((Hq, T, d), F32), jax.ShapeDtypeStruct((Hk, T + blk, d), F32),
                 jax.ShapeDtypeStruct((Hk, T + blk, d), F32)]
    if use_sink:
        out_specs.append(pl.BlockSpec((hps, 1, 128), lambda g, i: (g, 0, 0)))
        out_shape.append(jax.ShapeDtypeStruct((Hq, 1, 128), F32))
    return pl.pallas_call(
        body, name=name, grid=(Hq // hps, T // blk), in_specs=in_specs, out_specs=out_specs, out_shape=out_shape,
        compiler_params=_params("parallel", "arbitrary"),
    )(*ins)


def _band_geometry(blk, nb_seq, n_prev):
    def geo(i):
        seq = i // nb_seq
        n = i % nb_seq
        return seq, n, jnp.maximum(n - n_prev, 0)
    return geo


def _attn_fwd(q, k, v, slopes, sinks, *, blk, nb_seq, n_prev, n_back, scale, dist_scale, name):
    Hq, T, dqk = q.shape
    Hk, _, dv = v.shape
    group = Hq // Hk
    use_bias, use_sink = slopes is not None, sinks is not None
    geo = _band_geometry(blk, nb_seq, n_prev)

    def body(*refs):
        q_ref, k_ref, v_ref = refs[:3]
        slope_ref = refs[3] if use_bias else None
        sink_ref = refs[3 + use_bias] if use_sink else None
        o_ref, lse_ref = refs[-2:]
        h = pl.program_id(0)
        seq, n, lo = geo(pl.program_id(1))
        qb = q_ref[...]
        diff = lax.broadcasted_iota(jnp.int32, (blk, blk), 0) - lax.broadcasted_iota(jnp.int32, (blk, blk), 1)
        if use_sink:
            m0 = jnp.full((blk, 1), sink_ref[h], F32)
            l0 = jnp.ones((blk, 1), F32)
        else:
            m0 = jnp.full((blk, 1), NEG, F32)
            l0 = jnp.zeros((blk, 1), F32)

        def step(j, carry):
            m, l, acc = carry
            start = pl.multiple_of((seq * nb_seq + j) * blk, blk)
            kb = k_ref[pl.ds(start, blk), :]
            vb = v_ref[pl.ds(start, blk), :]
            s = lax.dot_general(qb, kb, (((1,), (1,)), ((), ())), preferred_element_type=F32) * scale
            rel = (n - j) * blk + diff
            valid = (rel >= 0) & (rel <= n_back)
            if use_bias:
                s = s - (slope_ref[h] * dist_scale) * rel.astype(F32)
            s = jnp.where(valid, s, NEG)
            m_new = jnp.maximum(m, jnp.max(s, axis=1, keepdims=True))
            p = jnp.where(valid, jnp.exp(s - m_new), 0.0)
            a = jnp.exp(m - m_new)
            l = a * l + jnp.sum(p, axis=1, keepdims=True)
            acc = a * acc + jnp.dot(p.astype(BF16), vb, preferred_element_type=F32)
            return m_new, l, acc

        m, l, acc = lax.fori_loop(lo, n + 1, step, (m0, l0, jnp.zeros((blk, dv), F32)))
        o_ref[...] = acc / l
        lse_ref[...] = m + jnp.log(l)

    smem = pl.BlockSpec(memory_space=pltpu.SMEM)
    in_specs = [pl.BlockSpec((None, blk, dqk), lambda h, i: (h, i, 0)),
                pl.BlockSpec((None, T, dqk), lambda h, i: (h // group, 0, 0)),
                pl.BlockSpec((None, T, dv), lambda h, i: (h // group, 0, 0))]
    ins = [q, k, v]
    if use_bias:
        in_specs.append(smem)
        ins.append(slopes)
    if use_sink:
        in_specs.append(smem)
        ins.append(sinks)
    return pl.pallas_call(
        body, name=name, grid=(Hq, T // blk), in_specs=in_specs,
        out_specs=[pl.BlockSpec((None, blk, dv), lambda h, i: (h, i, 0)), pl.BlockSpec((None, blk, 1), lambda h, i: (h, i, 0))],
        out_shape=[jax.ShapeDtypeStruct((Hq, T, dv), F32), jax.ShapeDtypeStruct((Hq, T, 1), F32)],
        compiler_params=_params("parallel", "parallel"),
    )(*ins)


def _attn_bwd(q, k, v, o, do, lse, dlse, slopes, sinks, *, blk, nb_seq, n_prev, n_back, scale, dist_scale, name):
    Hq, T, dqk = q.shape
    Hk, _, dv = v.shape
    group = Hq // Hk
    use_bias, use_sink, use_dlse = slopes is not None, sinks is not None, dlse is not None
    geo = _band_geometry(blk, nb_seq, n_prev)

    def body(*refs):
        q_ref, k_ref, v_ref, o_ref, do_ref, lse_ref = refs[:6]
        pos = 6
        dlse_ref = refs[pos] if use_dlse else None
        pos += use_dlse
        slope_ref = refs[pos] if use_bias else None
        pos += use_bias
        sink_ref = refs[pos] if use_sink else None
        pos += use_sink
        dq_ref, dk_ref, dv_ref = refs[pos:pos + 3]
        dsink_ref = refs[pos + 3] if use_sink else None
        h = pl.program_id(0)
        i = pl.program_id(1)
        seq, n, lo = geo(i)
        qb = q_ref[...]
        dob = do_ref[...]
        do16 = dob.astype(BF16)
        lse = lse_ref[...]
        c = -jnp.sum(dob * o_ref[...], axis=1, keepdims=True)
        if use_dlse:
            c = c + dlse_ref[...]
        diff = lax.broadcasted_iota(jnp.int32, (blk, blk), 0) - lax.broadcasted_iota(jnp.int32, (blk, blk), 1)

        @pl.when((h % group == 0) & (i == 0))
        def _():
            dk_ref[...] = jnp.zeros_like(dk_ref)
            dv_ref[...] = jnp.zeros_like(dv_ref)

        def step(j, dq):
            start = pl.multiple_of((seq * nb_seq + j) * blk, blk)
            kb = k_ref[pl.ds(start, blk), :]
            vb = v_ref[pl.ds(start, blk), :]
            s = lax.dot_general(qb, kb, (((1,), (1,)), ((), ())), preferred_element_type=F32) * scale
            rel = (n - j) * blk + diff
            valid = (rel >= 0) & (rel <= n_back)
            if use_bias:
                s = s - (slope_ref[h] * dist_scale) * rel.astype(F32)
            p = jnp.where(valid, jnp.exp(jnp.where(valid, s, NEG) - lse), 0.0)
            dp = lax.dot_general(do16, vb, (((1,), (1,)), ((), ())), preferred_element_type=F32)
            ds16 = (p * (dp + c) * scale).astype(BF16)
            dk_ref[pl.ds(start, blk), :] += lax.dot_general(ds16, qb, (((0,), (0,)), ((), ())), preferred_element_type=F32)
            dv_ref[pl.ds(start, blk), :] += lax.dot_general(p.astype(BF16), do16, (((0,), (0,)), ((), ())), preferred_element_type=F32)
            return dq + jnp.dot(ds16, kb, preferred_element_type=F32)

        dq_ref[...] = lax.fori_loop(lo, n + 1, step, jnp.zeros((blk, dqk), F32))
        if use_sink:
            @pl.when(i == 0)
            def _():
                dsink_ref[...] = jnp.zeros_like(dsink_ref)

            dsink_ref[...] += jnp.full(dsink_ref.shape, jnp.sum(jnp.exp(sink_ref[h] - lse) * c), F32)

    smem = pl.BlockSpec(memory_space=pltpu.SMEM)
    qs = pl.BlockSpec((None, blk, dqk), lambda h, i: (h, i, 0))
    os_ = pl.BlockSpec((None, blk, dv), lambda h, i: (h, i, 0))
    ls = pl.BlockSpec((None, blk, 1), lambda h, i: (h, i, 0))
    ks = pl.BlockSpec((None, T, dqk), lambda h, i: (h // group, 0, 0))
    vs = pl.BlockSpec((None, T, dv), lambda h, i: (h // group, 0, 0))
    in_specs = [qs, ks, vs, os_, os_, ls]
    ins = [q, k, v, o, do, lse]
    if use_dlse:
        in_specs.append(ls)
        ins.append(dlse)
    if use_bias:
        in_specs.append(smem)
        ins.append(slopes)
    if use_sink:
        in_specs.append(smem)
        ins.append(sinks)
    out_specs = [qs, ks, vs]
    out_shape = [jax.ShapeDtypeStruct((Hq, T, dqk), F32), jax.ShapeDtypeStruct((Hk, T, dqk), F32),
                 jax.ShapeDtypeStruct((Hk, T, dv), F32)]
    if use_sink:
        out_specs.append(pl.BlockSpec((None, 1, 128), lambda h, i: (h, 0, 0)))
        out_shape.append(jax.ShapeDtypeStruct((Hq, 1, 128), F32))
    return pl.pallas_call(
        body, name=name, grid=(Hq, T // blk), in_specs=in_specs, out_specs=out_specs, out_shape=out_shape,
        compiler_params=_params("arbitrary", "arbitrary"),
    )(*ins)


def _tri_sum(x, upper):
    hi = x.astype(BF16)
    r = x - hi.astype(F32)
    mid = r.astype(BF16)
    lo = (r - mid.astype(F32)).astype(BF16)
    return (jnp.dot(hi, upper, preferred_element_type=F32) + jnp.dot(mid, upper, preferred_element_type=F32)
            + jnp.dot(lo, upper, preferred_element_type=F32))


def _stick_terms(qb, kb, n, j, blk, scale, diff):
    z = lax.dot_general(qb, kb, (((1,), (1,)), ((), ())), preferred_element_type=F32) * scale
    e = jnp.exp(-jnp.abs(z))
    l1p = jnp.log(1.0 + e)
    lb = jnp.minimum(z, 0.0) - l1p
    strict = ((n - j) * blk + diff) > 0
    lk = jnp.where(strict, lb - z, 0.0)
    return z, e, lb, lk, strict


def _stick_fwd(q, k, v, *, blk, scale, name):
    H, T, dh = q.shape

    def body(q_ref, k_ref, v_ref, o_ref, tot_ref):
        n = pl.program_id(1)
        qb = q_ref[...]
        r_i = lax.broadcasted_iota(jnp.int32, (blk, blk), 0)
        c_i = lax.broadcasted_iota(jnp.int32, (blk, blk), 1)
        diff = r_i - c_i
        upper = (r_i > c_i).astype(BF16)

        def step(t, carry):
            run, acc = carry
            j = n - t
            start = pl.multiple_of(j * blk, blk)
            kb = k_ref[pl.ds(start, blk), :]
            vb = v_ref[pl.ds(start, blk), :]
            _, _, lb, lk, strict = _stick_terms(qb, kb, n, j, blk, scale, diff)
            w = jnp.where(strict, jnp.exp(lb + run + _tri_sum(lk, upper)), 0.0)
            acc = acc + jnp.dot(w.astype(BF16), vb, preferred_element_type=F32)
            return run + jnp.sum(lk, axis=1, keepdims=True), acc

        run, acc = lax.fori_loop(0, n + 1, step, (jnp.zeros((blk, 1), F32), jnp.zeros((blk, dh), F32)))
        o_ref[...] = acc
        tot_ref[...] = run

    qs = pl.BlockSpec((None, blk, dh), lambda h, i: (h, i, 0))
    ks = pl.BlockSpec((None, T, dh), lambda h, i: (h, 0, 0))
    ts = pl.BlockSpec((None, blk, 1), lambda h, i: (h, i, 0))
    return pl.pallas_call(
        body, name=name, grid=(H, T // blk), in_specs=[qs, ks, ks], out_specs=[qs, ts],
        out_shape=[jax.ShapeDtypeStruct((H, T, dh), F32), jax.ShapeDtypeStruct((H, T, 1), F32)],
        compiler_params=_params("parallel", "parallel"),
    )(q, k, v)


def _stick_bwd(q, k, v, tot, do, *, blk, scale, name):
    H, T, dh = q.shape

    def body(q_ref, k_ref, v_ref, tot_ref, do_ref, dq_ref, dk_ref, dv_ref):
        n = pl.program_id(1)
        qb = q_ref[...]
        do16 = do_ref[...].astype(BF16)
        tot = tot_ref[...]
        r_i = lax.broadcasted_iota(jnp.int32, (blk, blk), 0)
        c_i = lax.broadcasted_iota(jnp.int32, (blk, blk), 1)
        diff = r_i - c_i
        upto = (r_i <= c_i).astype(BF16)
        below = (r_i < c_i).astype(BF16)

        @pl.when(n == 0)
        def _():
            dk_ref[...] = jnp.zeros_like(dk_ref)
            dv_ref[...] = jnp.zeros_like(dv_ref)

        def step(j, carry):
            run, grun, dq = carry
            start = pl.multiple_of(j * blk, blk)
            kb = k_ref[pl.ds(start, blk), :]
            vb = v_ref[pl.ds(start, blk), :]
            z, e, lb, lk, strict = _stick_terms(qb, kb, n, j, blk, scale, diff)
            w = jnp.where(strict, jnp.exp(lb + tot - (run + _tri_sum(lk, upto))), 0.0)
            dw = lax.dot_general(do16, vb, (((1,), (1,)), ((), ())), preferred_element_type=F32)
            g = w * dw
            before = grun + _tri_sum(g, below)
            inv = 1.0 / (1.0 + e)
            sig = jnp.where(z >= 0, inv, e * inv)
            dz16 = (jnp.where(strict, g * (1.0 - sig) - sig * before, 0.0) * scale).astype(BF16)
            dk_ref[pl.ds(start, blk), :] += lax.dot_general(dz16, qb, (((0,), (0,)), ((), ())), preferred_element_type=F32)
            dv_ref[pl.ds(start, blk), :] += lax.dot_general(w.astype(BF16), do16, (((0,), (0,)), ((), ())), preferred_element_type=F32)
            dq = dq + jnp.dot(dz16, kb, preferred_element_type=F32)
            return run + jnp.sum(lk, axis=1, keepdims=True), grun + jnp.sum(g, axis=1, keepdims=True), dq

        zero = jnp.zeros((blk, 1), F32)
        _, _, dq = lax.fori_loop(0, n + 1, step, (zero, zero, jnp.zeros((blk, dh), F32)))
        dq_ref[...] = dq

    qs = pl.BlockSpec((None, blk, dh), lambda h, i: (h, i, 0))
    ks = pl.BlockSpec((None, T, dh), lambda h, i: (h, 0, 0))
    ts = pl.BlockSpec((None, blk, 1), lambda h, i: (h, i, 0))
    shp = jax.ShapeDtypeStruct((H, T, dh), F32)
    return pl.pallas_call(
        body, name=name, grid=(H, T // blk), in_specs=[qs, ks, ks, ts, qs], out_specs=[qs, ks, ks],
        out_shape=[shp, shp, shp],
        compiler_params=_params("arbitrary", "arbitrary"),
    )(q, k, v, tot, do)


def _mix_fwd(outs, lses, *, name, rows=512):
    H, T, dh = outs[0].shape
    rows = _tile(T, rows)

    def body(o0, o1, o2, l0, l1, l2, y_ref):
        ls = [l0[...], l1[...], l2[...]]
        mx = jnp.maximum(jnp.maximum(ls[0], ls[1]), ls[2])
        es = [jnp.exp(x - mx) for x in ls]
        den = es[0] + es[1] + es[2]
        y_ref[...] = (es[0] * o0[...] + es[1] * o1[...] + es[2] * o2[...]) / den

    os_ = pl.BlockSpec((None, rows, dh), lambda h, i: (h, i, 0))
    ls_ = pl.BlockSpec((None, rows, 1), lambda h, i: (h, i, 0))
    return pl.pallas_call(
        body, name=name, grid=(H, T // rows), in_specs=[os_] * 3 + [ls_] * 3, out_specs=os_,
        out_shape=jax.ShapeDtypeStruct((H, T, dh), F32),
        compiler_params=_params("parallel", "parallel"),
    )(*outs, *lses)


def _mix_bwd(outs, lses, dy, *, name, rows=512):
    H, T, dh = outs[0].shape
    rows = _tile(T, rows)

    def body(o0, o1, o2, l0, l1, l2, dy_ref, d0, d1, d2, g0, g1, g2):
        ls = [l0[...], l1[...], l2[...]]
        mx = jnp.maximum(jnp.maximum(ls[0], ls[1]), ls[2])
        es = [jnp.exp(x - mx) for x in ls]
        den = es[0] + es[1] + es[2]
        mix = [e / den for e in es]
        dyv = dy_ref[...]
        dm = [jnp.sum(dyv * o[...], axis=1, keepdims=True) for o in (o0, o1, o2)]
        avg = mix[0] * dm[0] + mix[1] * dm[1] + mix[2] * dm[2]
        for d_ref, g_ref, w, t in zip((d0, d1, d2), (g0, g1, g2), mix, dm):
            d_ref[...] = w * dyv
            g_ref[...] = w * (t - avg)

    os_ = pl.BlockSpec((None, rows, dh), lambda h, i: (h, i, 0))
    ls_ = pl.BlockSpec((None, rows, 1), lambda h, i: (h, i, 0))
    so = jax.ShapeDtypeStruct((H, T, dh), F32)
    sl = jax.ShapeDtypeStruct((H, T, 1), F32)
    res = pl.pallas_call(
        body, name=name, grid=(H, T // rows), in_specs=[os_] * 3 + [ls_] * 3 + [os_], out_specs=[os_] * 3 + [ls_] * 3,
        out_shape=[so] * 3 + [sl] * 3,
        compiler_params=_params("parallel", "parallel"),
    )(*outs, *lses, dy)
    return res[:3], res[3:]


def _position():
    return lax.axis_index("x"), lax.axis_index("y"), lax.axis_index("c")


def _other_chips(x, y):
    return [(1 - x, y), (x, 1 - y), (1 - x, 1 - y)]


def _gather_weights(shards, *, name):
    n = len(shards)

    def body(*refs):
        ins, outs = refs[:n], refs[n:2 * n]
        send_sems, recv_sems = refs[2 * n:]
        x, y, c = _position()
        chips = _other_chips(x, y)

        def remote(w, k, chip, core, to, src=None):
            half = ins[w].shape[0] // 2
            slab = outs[w].at[2 * chip[0] + chip[1], pl.ds(core * half, half), :]
            return pltpu.make_async_remote_copy(
                src_ref=slab if src is None else src, dst_ref=slab,
                send_sem=send_sems.at[w, k], recv_sem=recv_sems.at[w, k], device_id=to, device_id_type=MESH)

        sends = []
        for w in range(n):
            half = ins[w].shape[0] // 2
            for j, chip in enumerate(chips):
                sends.append(remote(w, j, (x, y), c, (chip[0], chip[1], c), src=ins[w].at[pl.ds(c * half, half), :]))
                sends[-1].start()
        for w in range(n):
            for j, chip in enumerate(chips):
                remote(w, j, chip, c, (x, y, c)).wait_recv()
                sends.append(remote(w, 3 + j, chip, c, (x, y, 1 - c)))
                sends[-1].start()
        for w in range(n):
            for j, chip in enumerate(chips):
                remote(w, 3 + j, chip, 1 - c, (x, y, c)).wait_recv()
        for cp in sends:
            cp.wait_send()

    hbm = pl.BlockSpec(memory_space=pl.ANY)
    return pl.pallas_call(
        body, name=name, in_specs=[hbm] * n, out_specs=[hbm] * n,
        out_shape=[jax.ShapeDtypeStruct((N_CHIPS,) + s.shape, s.dtype) for s in shards],
        scratch_shapes=[pltpu.SemaphoreType.DMA((n, 6)), pltpu.SemaphoreType.DMA((n, 6))],
    )(*shards)


def _swap_other_half(arrs, *, name):
    n = len(arrs)

    def body(*refs):
        ins, outs = refs[:n], refs[n:2 * n]
        send_sems, recv_sems = refs[2 * n:]
        x, y, c = _position()
        cps = []
        for w in range(n):
            half = ins[w].shape[1] // 2
            cps.append(pltpu.make_async_remote_copy(
                src_ref=ins[w].at[:, pl.ds((1 - c) * half, half), :], dst_ref=outs[w], send_sem=send_sems.at[w],
                recv_sem=recv_sems.at[w], device_id=(x, y, 1 - c), device_id_type=MESH))
            cps[-1].start()
        for cp in cps:
            cp.wait()

    hbm = pl.BlockSpec(memory_space=pl.ANY)
    return pl.pallas_call(
        body, name=name, in_specs=[hbm] * n, out_specs=[hbm] * n,
        out_shape=[jax.ShapeDtypeStruct((a.shape[0], a.shape[1] // 2, a.shape[2]), a.dtype) for a in arrs],
        scratch_shapes=[pltpu.SemaphoreType.DMA((n,)), pltpu.SemaphoreType.DMA((n,))],
    )(*arrs)


def _share_halves(bufs, *, name):
    n = len(bufs)

    def body(*refs):
        ins, outs = refs[:n], refs[n:2 * n]
        send_sems, recv_sems = refs[2 * n:]
        x, y, c = _position()
        sends = []
        for w in range(n):
            sends.append(pltpu.make_async_remote_copy(src_ref=ins[w].at[c], dst_ref=outs[w].at[c], send_sem=send_sems.at[w],
                                                      recv_sem=recv_sems.at[w], device_id=(x, y, 1 - c), device_id_type=MESH))
            sends[-1].start()
        for w in range(n):
            pltpu.make_async_remote_copy(src_ref=ins[w].at[1 - c], dst_ref=outs[w].at[1 - c], send_sem=send_sems.at[w],
                                         recv_sem=recv_sems.at[w], device_id=(x, y, 1 - c), device_id_type=MESH).wait_recv()
        for cp in sends:
            cp.wait_send()

    hbm = pl.BlockSpec(memory_space=pl.ANY)
    return pl.pallas_call(
        body, name=name, in_specs=[hbm] * n, out_specs=[hbm] * n,
        out_shape=[jax.ShapeDtypeStruct(a.shape, a.dtype) for a in bufs],
        input_output_aliases={w: w for w in range(n)},
        scratch_shapes=[pltpu.SemaphoreType.DMA((n,)), pltpu.SemaphoreType.DMA((n,))],
    )(*bufs)


def _chip_exchange(arrs, *, name):
    n = len(arrs)

    def body(*refs):
        ins, outs = refs[:n], refs[n:2 * n]
        send_sems, recv_sems = refs[2 * n:]
        x, y, c = _position()
        chips = _other_chips(x, y)
        sends = []
        for w in range(n):
            for j, chip in enumerate(chips):
                sends.append(pltpu.make_async_remote_copy(
                    src_ref=ins[w].at[2 * chip[0] + chip[1]], dst_ref=outs[w].at[j], send_sem=send_sems.at[w, j],
                    recv_sem=recv_sems.at[w, j], device_id=(chip[0], chip[1], c), device_id_type=MESH))
                sends[-1].start()
        for cp in sends:
            cp.wait()

    hbm = pl.BlockSpec(memory_space=pl.ANY)
    return pl.pallas_call(
        body, name=name, in_specs=[hbm] * n, out_specs=[hbm] * n,
        out_shape=[jax.ShapeDtypeStruct((3,) + a.shape[1:], a.dtype) for a in arrs],
        scratch_shapes=[pltpu.SemaphoreType.DMA((n, 3)), pltpu.SemaphoreType.DMA((n, 3))],
    )(*arrs)


def _all_gather8(v, *, name):
    m, n = v.shape

    def body(v_ref, out_ref, send_sems, recv_sems, local_sem):
        x, y, c = _position()
        me, sibling = (x, y, c), (x, y, 1 - c)
        chips = _other_chips(x, y)

        def rows(px, py, pc):
            return out_ref.at[pl.ds((4 * px + 2 * py + pc) * m, m), :]

        def copy(k, block, to, src=None):
            return pltpu.make_async_remote_copy(src_ref=rows(*block) if src is None else src, dst_ref=rows(*block),
                                                send_sem=send_sems.at[k], recv_sem=recv_sems.at[k], device_id=to, device_id_type=MESH)

        mine = pltpu.make_async_copy(v_ref, rows(*me), local_sem)
        mine.start()
        first = [copy(0, me, sibling, src=v_ref)]
        first += [copy(1 + j, me, (*chip, c), src=v_ref) for j, chip in enumerate(chips)]
        for cp in first:
            cp.start()
        passed = [copy(4 + j, (*chip, c), sibling) for j, chip in enumerate(chips)]
        for j, chip in enumerate(chips):
            copy(1 + j, (*chip, c), me).wait_recv()
            passed[j].start()
        copy(0, sibling, me).wait_recv()
        for j, chip in enumerate(chips):
            copy(4 + j, (*chip, 1 - c), me).wait_recv()
        for cp in first + passed:
            cp.wait_send()
        mine.wait()

    vmem = pl.BlockSpec(memory_space=pltpu.VMEM)
    return pl.pallas_call(
        body, name=name, in_specs=[vmem], out_specs=vmem, out_shape=jax.ShapeDtypeStruct((8 * m, n), v.dtype),
        scratch_shapes=[pltpu.SemaphoreType.DMA((7,)), pltpu.SemaphoreType.DMA((7,)), pltpu.SemaphoreType.DMA],
    )(v)


def _add_half(full, part, core, *, name, rows=256):
    P, R, C = full.shape
    half = R // 2
    rows = _tile(half, rows)
    nb = half // rows

    def body(core_ref, a_ref, b_ref, o_ref):
        o_ref[...] = (a_ref[...] + b_ref[...].astype(F32)).astype(BF16)

    spec = pl.BlockSpec((None, rows, C), lambda p, i, core_ref: (p, i, 0))
    grid_spec = pltpu.PrefetchScalarGridSpec(
        num_scalar_prefetch=1, grid=(P, nb),
        in_specs=[pl.BlockSpec((None, rows, C), lambda p, i, core_ref: (p, core_ref[0] * nb + i, 0)), spec], out_specs=spec)
    return pl.pallas_call(
        body, name=name, grid_spec=grid_spec, out_shape=jax.ShapeDtypeStruct((P, half, C), BF16),
        compiler_params=_params("parallel", "parallel"),
    )(core, full, part)


def _sum_parts(own, landed, where, *, name, rows=256):
    _, R, C = own.shape
    rows = _tile(R, rows)

    def body(where_ref, own_ref, land_ref, o_ref):
        acc = own_ref[...].astype(F32)
        for j in range(3):
            acc = acc + land_ref[j].astype(F32)
        o_ref[...] = acc

    grid_spec = pltpu.PrefetchScalarGridSpec(
        num_scalar_prefetch=1, grid=(R // rows,),
        in_specs=[pl.BlockSpec((None, rows, C), lambda i, where_ref: (where_ref[0], i, 0)),
                  pl.BlockSpec((3, rows, C), lambda i, where_ref: (0, i, 0))],
        out_specs=pl.BlockSpec((None, rows, C), lambda i, where_ref: (where_ref[1], i, 0)))
    return pl.pallas_call(
        body, name=name, grid_spec=grid_spec, out_shape=jax.ShapeDtypeStruct((2, R, C), F32),
        compiler_params=_params("parallel"),
    )(where, own, landed)


def _sum_slabs(a, *, name, rows=256):
    P, R, C = a.shape
    rows = _tile(R, rows)

    def body(a_ref, o_ref):
        acc = a_ref[0].astype(F32)
        for p in range(1, P):
            acc = acc + a_ref[p].astype(F32)
        o_ref[...] = acc

    return pl.pallas_call(
        body, name=name, grid=(R // rows,), in_specs=[pl.BlockSpec((P, rows, C), lambda i: (0, i, 0))],
        out_specs=pl.BlockSpec((rows, C), lambda i: (i, 0)),
        out_shape=jax.ShapeDtypeStruct((R, C), F32), compiler_params=_params("parallel"),
    )(a)


def _reduce_scatter(grads, grads16, *, name):
    core = lax.axis_index("c").astype(jnp.int32)
    chip = (2 * lax.axis_index("x") + lax.axis_index("y")).astype(jnp.int32)
    got = _swap_other_half(grads16, name=name + "_swap")
    chip_sums = [_add_half(g, r, core.reshape(1), name=f"{name}_add{w}") for w, (g, r) in enumerate(zip(grads, got))]
    landed = _chip_exchange(chip_sums, name=name + "_exchange")
    where = jnp.stack([chip, core])
    reduced = [_sum_parts(s, a, where, name=f"{name}_sum{w}") for w, (s, a) in enumerate(zip(chip_sums, landed))]
    both = _share_halves(reduced, name=name + "_share")
    return [b.reshape(2 * b.shape[1], b.shape[2]) for b in both]


def _to_heads(t, heads, dil=1):
    S = t.shape[0]
    d = t.shape[1] // heads
    return jnp.transpose(t.reshape(S // dil, dil, heads, d), (2, 1, 0, 3)).reshape(heads, S, d)


def _from_heads(t, dil=1):
    heads, S, d = t.shape
    return jnp.transpose(t.reshape(heads, dil, S // dil, d), (2, 1, 0, 3)).reshape(S, heads * d)


def _unstride(t, dil):
    heads, S, d = t.shape
    return jnp.transpose(t.reshape(heads, dil, S // dil, d), (0, 2, 1, 3)).reshape(heads, S, d)


def _restride(t, dil):
    heads, S, d = t.shape
    return jnp.transpose(t.reshape(heads, S // dil, dil, d), (0, 2, 1, 3)).reshape(heads, S, d)


def _pad_cols(t, width, padded):
    S = t.shape[0]
    return jnp.pad(t.reshape(S, N_CHIPS, width), ((0, 0), (0, 0), (0, padded - width))).reshape(S, N_CHIPS * padded)


def _unpad_cols(t, width, padded):
    S = t.shape[0]
    return t.reshape(S, N_CHIPS, padded)[:, :, :width].reshape(S, N_CHIPS * width)


def _alibi(n):
    return 2.0 ** (-8.0 * jnp.arange(1, n + 1, dtype=F32) / n)


B_KW = [dict(hps=4, nb_seq=SEQ // d // ATT_BLK, n_back=w // d, scale=1.0 / math.sqrt(HEAD_DIM), dist_scale=d)
        for w, d in B_PATTERNS]
A_KW = dict(hps=A_Q_HEADS // A_KV_HEADS, nb_seq=SEQ // ATT_BLK, n_back=A_WINDOW - 1, scale=1.0 / math.sqrt(HEAD_DIM), dist_scale=1)
D_BLK = 256
D_KW = dict(blk=D_BLK, nb_seq=SEQ // D_BLK, n_prev=SEQ // D_BLK, n_back=SEQ, scale=1.0 / math.sqrt(D_NOPE + D_ROPE), dist_scale=1)
C_KW = dict(blk=256, scale=1.0 / math.sqrt(HEAD_DIM))


def _front_block(t):
    return jnp.pad(t, ((0, 0), (ATT_BLK, 0), (0, 0)))


def _rope_tables(reps):
    inv_freq = ROPE_BASE ** (-jnp.arange(0, D_ROPE, 2, dtype=F32) / D_ROPE)
    ang = jnp.arange(SEQ, dtype=F32)[:, None] * inv_freq[None, :]
    return jnp.tile(jnp.cos(ang), (1, reps)), jnp.tile(jnp.sin(ang), (1, reps))


def _mlp_fwd(x, x16, w1, w2, g, b, tag):
    hid, act = _matmul(x16, w1, mode="nn", out_dtype=BF16, epilogue="relu2", name=f"mlp{tag}_up")
    m = _matmul(act, w2, mode="nn", out_dtype=F32, name=f"mlp{tag}_down")
    y, y16, u = _norm_fwd(x, m, g, b, center=True, eps=LN_EPS, alpha=ALPHA, name=f"ln2_{tag}")
    return y, y16, (x16, hid, act, u)


def _mlp_bwd(dy, saved, w1, w2, g, tag):
    x16, hid, act, u = saved
    du, du16, dg, db = _norm_bwd(dy, u, g, center=True, eps=LN_EPS, name=f"ln2_{tag}_bwd")
    dw2 = _matmul_tn(act, du16, shards=1, name=f"mlp{tag}_dw2")
    dhid = _matmul(du16, w2, mode="nt", out_dtype=BF16, epilogue="drelu2", extra=hid, name=f"mlp{tag}_dact")
    dw1 = _matmul_tn(x16, dhid, shards=N_CHIPS, name=f"mlp{tag}_dw1")
    dx = _matmul(dhid, w1, mode="nt", out_dtype=F32, epilogue="add", extra=du, alpha=ALPHA, name=f"mlp{tag}_dx")
    return dx, dw1, dw2, dg, db


def _even_fwd(x16, w_in, sinks, w_out):
    S = x16.shape[0]
    h = _unpad_cols(_matmul(x16, w_in, mode="nn", out_dtype=BF16, name="even_in"), EVEN_IN // N_CHIPS, EVEN_IN_PAD)
    qa = _to_heads(h[:, :A_Q_W], A_Q_HEADS)
    ka = _front_block(_to_heads(h[:, A_Q_W:A_Q_W + A_KV_W], A_KV_HEADS))
    va = _front_block(_to_heads(h[:, A_Q_W + A_KV_W:A_Q_W + 2 * A_KV_W], A_KV_HEADS))
    slopes_a, slopes_b = _alibi(A_Q_HEADS), _alibi(B_HEADS)
    oa, lse_a = _band_fwd(qa, ka, va, slopes_a, sinks, name="swa_fwd", **A_KW)
    base = A_Q_W + 2 * A_KV_W
    qkv_b, outs, lses = [], [], []
    for gi, (_, dil) in enumerate(B_PATTERNS):
        cols = h[:, base + gi * 3 * B_W:base + (gi + 1) * 3 * B_W]
        q, k, v = (_to_heads(cols[:, i * B_W:(i + 1) * B_W], B_HEADS, dil) for i in range(3))
        k, v = _front_block(k), _front_block(v)
        o, lse = _band_fwd(q, k, v, slopes_b, None, name=f"dil{gi}_fwd", **B_KW[gi])
        qkv_b.append((q, k, v, o, lse))
        outs.append(_unstride(o, dil))
        lses.append(_unstride(lse, dil))
    ob = _mix_fwd(outs, lses, name="dil_mix")
    y16 = jnp.concatenate([_from_heads(oa), _from_heads(ob)], axis=-1).astype(BF16)
    mixed = _matmul(y16, w_out, mode="nn", out_dtype=F32, name="even_out")
    return mixed, (x16, qa, ka, va, oa, lse_a, qkv_b, outs, lses, y16)


def _even_bwd(dmix16, du, saved, w_in, sinks, w_out):
    x16, qa, ka, va, oa, lse_a, qkv_b, outs, lses, y16 = saved
    slopes_a, slopes_b = _alibi(A_Q_HEADS), _alibi(B_HEADS)
    dw_out = _matmul_tn(y16, dmix16, shards=N_CHIPS, name="even_dwout")
    dy = _matmul(dmix16, w_out, mode="nt", out_dtype=F32, name="even_dy")
    doa = _to_heads(dy[:, :A_Q_W], A_Q_HEADS)
    dob = _to_heads(dy[:, A_Q_W:], B_HEADS)
    dqa, dka, dva, dsink = _band_bwd(qa, ka, va, oa, doa, lse_a, None, slopes_a, sinks, name="swa_bwd", **A_KW)
    douts, dlses = _mix_bwd(outs, lses, dob, name="dil_mix_bwd")
    parts = [_from_heads(dqa), _from_heads(dka[:, ATT_BLK:]), _from_heads(dva[:, ATT_BLK:])]
    for gi, (_, dil) in enumerate(B_PATTERNS):
        q, k, v, o, lse = qkv_b[gi]
        dq, dk, dv = _band_bwd(q, k, v, o, _restride(douts[gi], dil), lse, _restride(dlses[gi], dil), slopes_b, None,
                               name=f"dil{gi}_bwd", **B_KW[gi])
        parts += [_from_heads(dq, dil), _from_heads(dk[:, ATT_BLK:], dil), _from_heads(dv[:, ATT_BLK:], dil)]
    dh16 = _pad_cols(jnp.concatenate(parts, axis=-1), EVEN_IN // N_CHIPS, EVEN_IN_PAD).astype(BF16)
    dw_in = _matmul_tn(x16, dh16, shards=N_CHIPS, name="even_dwin")
    dx = _matmul(dh16, w_in, mode="nt", out_dtype=F32, epilogue="add", extra=du, alpha=ALPHA, name="even_dx")
    return dx, dw_in, dsink[:, 0, 0], dw_out


def _odd_fwd(x16, w_in, qn_g, kvn_g, w_uq, w_ukv, w_out):
    S = x16.shape[0]
    h = _unpad_cols(_matmul(x16, w_in, mode="nn", out_dtype=F32, name="odd_in"), ODD_IN // N_CHIPS, ODD_IN_PAD)
    qc, kc, vc = (_to_heads(h[:, i * C_W:(i + 1) * C_W].astype(BF16), C_HEADS) for i in range(3))
    cq = h[:, 3 * C_W:3 * C_W + D_Q_RANK]
    ckv = h[:, 3 * C_W + D_Q_RANK:3 * C_W + D_Q_RANK + D_KV_RANK]
    kr = h[:, 3 * C_W + D_Q_RANK + D_KV_RANK:]
    oc, tot = _stick_fwd(qc, kc, vc, name="stick_fwd", **C_KW)
    _, cqn16, _ = _norm_fwd(cq, None, qn_g, None, center=False, eps=RMS_EPS, alpha=1.0, name="q_rms")
    _, ckvn16, _ = _norm_fwd(ckv, None, kvn_g, None, center=False, eps=RMS_EPS, alpha=1.0, name="kv_rms")
    q = _matmul(cqn16, w_uq, mode="nn", out_dtype=F32, name="mla_uq").reshape(S, D_HEADS, D_NOPE + D_ROPE)
    kv = _matmul(ckvn16, w_ukv, mode="nn", out_dtype=F32, name="mla_ukv").reshape(S, D_HEADS, D_NOPE + D_V)
    half = D_ROPE // 2
    q_rope = q[..., D_NOPE:]
    x1 = jnp.concatenate([q_rope[..., :half].reshape(S, D_HEADS * half), kr[:, :half]], axis=-1)
    x2 = jnp.concatenate([q_rope[..., half:].reshape(S, D_HEADS * half), kr[:, half:]], axis=-1)
    cos, sin = _rope_tables(D_HEADS + 1)
    y1, y2 = _rope(x1[None], x2[None], cos, sin, name="rope_fwd")
    nq = D_HEADS * half
    q_rot = jnp.concatenate([y1[:, :nq].reshape(S, D_HEADS, half), y2[:, :nq].reshape(S, D_HEADS, half)], axis=-1)
    kr_rot = jnp.concatenate([y1[:, nq:], y2[:, nq:]], axis=-1)
    qd = jnp.transpose(jnp.concatenate([q[..., :D_NOPE], q_rot], axis=-1), (1, 0, 2)).astype(BF16)
    kd = jnp.transpose(jnp.concatenate([kv[..., :D_NOPE], jnp.broadcast_to(kr_rot[:, None, :], (S, D_HEADS, D_ROPE))], axis=-1),
                       (1, 0, 2)).astype(BF16)
    vd = jnp.transpose(kv[..., D_NOPE:], (1, 0, 2)).astype(BF16)
    od, lse_d = _attn_fwd(qd, kd, vd, None, None, name="mla_fwd", **D_KW)
    y16 = jnp.concatenate([_from_heads(oc), _from_heads(od)], axis=-1).astype(BF16)
    mixed = _matmul(y16, w_out, mode="nn", out_dtype=F32, name="odd_out")
    return mixed, (x16, qc, kc, vc, tot, cq, ckv, cqn16, ckvn16, qd, kd, vd, od, lse_d, y16)


def _odd_bwd(dmix16, du, saved, w_in, qn_g, kvn_g, w_uq, w_ukv, w_out):
    x16, qc, kc, vc, tot, cq, ckv, cqn16, ckvn16, qd, kd, vd, od, lse_d, y16 = saved
    S = x16.shape[0]
    half = D_ROPE // 2
    dw_out = _matmul_tn(y16, dmix16, shards=1, name="odd_dwout")
    dy = _matmul(dmix16, w_out, mode="nt", out_dtype=F32, name="odd_dy")
    doc = _to_heads(dy[:, :C_W], C_HEADS)
    dod = _to_heads(dy[:, C_W:], D_HEADS)
    dqc, dkc, dvc = _stick_bwd(qc, kc, vc, tot, doc, name="stick_bwd", **C_KW)
    dqd, dkd, dvd = _attn_bwd(qd, kd, vd, od, dod, lse_d, None, None, None, name="mla_bwd", **D_KW)
    cos, sin = _rope_tables(D_HEADS + 1)
    nq = D_HEADS * half
    gq1 = jnp.transpose(dqd[..., D_NOPE:D_NOPE + half], (1, 0, 2)).reshape(1, S, nq)
    gq2 = jnp.transpose(dqd[..., D_NOPE + half:], (1, 0, 2)).reshape(1, S, nq)
    dq1, dq2 = _rope(gq1, gq2, cos[:, :nq], -sin[:, :nq], name="rope_bwd_q")
    dk1, dk2 = _rope(dkd[..., D_NOPE:D_NOPE + half], dkd[..., D_NOPE + half:], cos[:, nq:], -sin[:, nq:], name="rope_bwd_k")
    dq_full = jnp.concatenate([jnp.transpose(dqd[..., :D_NOPE], (1, 0, 2)), dq1.reshape(S, D_HEADS, half),
                               dq2.reshape(S, D_HEADS, half)], axis=-1).reshape(S, D_HEADS * (D_NOPE + D_ROPE)).astype(BF16)
    dkv_full = jnp.concatenate([jnp.transpose(dkd[..., :D_NOPE], (1, 0, 2)), jnp.transpose(dvd, (1, 0, 2))],
                               axis=-1).reshape(S, D_HEADS * (D_NOPE + D_V)).astype(BF16)
    dw_uq = _matmul_tn(cqn16, dq_full, shards=N_CHIPS, name="mla_dwuq")
    dw_ukv = _matmul_tn(ckvn16, dkv_full, shards=N_CHIPS, name="mla_dwukv")
    dcqn = _matmul(dq_full, w_uq, mode="nt", out_dtype=F32, name="mla_dcq")
    dckvn = _matmul(dkv_full, w_ukv, mode="nt", out_dtype=F32, name="mla_dckv")
    dcq, _, dqn_g, _ = _norm_bwd(dcqn, cq, qn_g, center=False, eps=RMS_EPS, name="q_rms_bwd")
    dckv, _, dkvn_g, _ = _norm_bwd(dckvn, ckv, kvn_g, center=False, eps=RMS_EPS, name="kv_rms_bwd")
    dh = jnp.concatenate([_from_heads(dqc), _from_heads(dkc), _from_heads(dvc), dcq, dckv, dk1, dk2], axis=-1)
    dh16 = _pad_cols(dh, ODD_IN // N_CHIPS, ODD_IN_PAD).astype(BF16)
    dw_in = _matmul_tn(x16, dh16, shards=N_CHIPS, name="odd_dwin")
    dx = _matmul(dh16, w_in, mode="nt", out_dtype=F32, epilogue="add", extra=du, alpha=ALPHA, name="odd_dx")
    return dx, dw_in, dqn_g[0], dkvn_g[0], dw_uq, dw_ukv, dw_out


def _local_step(x, target, w, small):
    x16 = x.astype(BF16)
    mixed0, sv_e = _even_fwd(x16, w["even_w_in"], small["even_sinks"], w["even_w_out"])
    x1, x1_16, u01 = _norm_fwd(x, mixed0, small["ln1_g"][0], small["ln1_b"][0], center=True, eps=LN_EPS, alpha=ALPHA, name="ln1_0")
    x2, x2_16, sv_m0 = _mlp_fwd(x1, x1_16, w["mlp_w1"][0], w["mlp_w2"][0], small["ln2_g"][0], small["ln2_b"][0], 0)
    mixed1, sv_o = _odd_fwd(x2_16, w["odd_w_in"], small["odd_q_norm_g"], small["odd_kv_norm_g"], w["odd_w_uq"], w["odd_w_ukv"],
                            w["odd_w_out"])
    x3, x3_16, u11 = _norm_fwd(x2, mixed1, small["ln1_g"][1], small["ln1_b"][1], center=True, eps=LN_EPS, alpha=ALPHA, name="ln1_1")
    x4, _, sv_m1 = _mlp_fwd(x3, x3_16, w["mlp_w1"][1], w["mlp_w2"][1], small["ln2_g"][1], small["ln2_b"][1], 1)
    dy, loss = _loss_head(x4, target, name="loss_head")

    dx3, dw1_1, dw2_1, dln2g_1, dln2b_1 = _mlp_bwd(dy, sv_m1, w["mlp_w1"][1], w["mlp_w2"][1], small["ln2_g"][1], 1)
    du, du16, dln1g_1, dln1b_1 = _norm_bwd(dx3, u11, small["ln1_g"][1], center=True, eps=LN_EPS, name="ln1_1_bwd")
    dx2, dwin_o, dqn_g, dkvn_g, dwuq, dwukv, dwout_o = _odd_bwd(
        du16, du, sv_o, w["odd_w_in"], small["odd_q_norm_g"], small["odd_kv_norm_g"], w["odd_w_uq"], w["odd_w_ukv"], w["odd_w_out"])
    dx1, dw1_0, dw2_0, dln2g_0, dln2b_0 = _mlp_bwd(dx2, sv_m0, w["mlp_w1"][0], w["mlp_w2"][0], small["ln2_g"][0], 0)
    du, du16, dln1g_0, dln1b_0 = _norm_bwd(dx1, u01, small["ln1_g"][0], center=True, eps=LN_EPS, name="ln1_0_bwd")
    dx0, dwin_e, dsinks, dwout_e = _even_bwd(du16, du, sv_e, w["even_w_in"], small["even_sinks"], w["even_w_out"])

    def by_rows(pair, rows):
        return tuple(t.reshape(N_CHIPS, rows // N_CHIPS, D_MODEL) for t in pair)

    big = dict(even_w_in=dwin_e, even_w_out=dwout_e, odd_w_in=dwin_o, odd_w_uq=dwuq, odd_w_ukv=dwukv,
               odd_w_out=by_rows(dwout_o, D_MODEL), mlp_w1_0=dw1_0, mlp_w1_1=dw1_1,
               mlp_w2_0=by_rows(dw2_0, D_FF), mlp_w2_1=by_rows(dw2_1, D_FF))
    sm = dict(even_sinks=dsinks, odd_q_norm_g=dqn_g, odd_kv_norm_g=dkvn_g,
              ln1_g=jnp.concatenate([dln1g_0, dln1g_1]), ln1_b=jnp.concatenate([dln1b_0, dln1b_1]),
              ln2_g=jnp.concatenate([dln2g_0, dln2g_1]), ln2_b=jnp.concatenate([dln2b_0, dln2b_1]))
    return loss, dx0, big, sm


SMALL_ORDER = ("ln1_g", "ln1_b", "ln2_g", "ln2_b", "even_sinks", "odd_q_norm_g", "odd_kv_norm_g")
SMALL_COLS = 2176


def _pack_small(parts):
    flat = jnp.concatenate([p.reshape(-1) for p in parts])
    return jnp.pad(flat, (0, 8 * SMALL_COLS - flat.shape[0])).reshape(8, SMALL_COLS)


def _unpack_small(packed, shapes):
    flat = packed.reshape(-1)
    out, pos = [], 0
    for s in shapes:
        n = math.prod(s)
        out.append(flat[pos:pos + n].reshape(s))
        pos += n
    return out


def kernel(x, even_w_in, even_sinks, even_w_out, odd_w_in, odd_q_norm_g, odd_kv_norm_g, odd_w_uq, odd_w_ukv, odd_w_out, ln1_g, ln1_b, mlp_w1, mlp_w2, ln2_g, ln2_b, loss_target, m_even_w_in, m_even_sinks, m_even_w_out, m_odd_w_in, m_odd_q_norm_g, m_odd_kv_norm_g, m_odd_w_uq, m_odd_w_ukv, m_odd_w_out, m_ln1_g, m_ln1_b, m_mlp_w1, m_mlp_w2, m_ln2_g, m_ln2_b, v_even_w_in, v_even_sinks, v_even_w_out, v_odd_w_in, v_odd_q_norm_g, v_odd_kv_norm_g, v_odd_w_uq, v_odd_w_ukv, v_odd_w_out, v_ln1_g, v_ln1_b, v_mlp_w1, v_mlp_w2, v_ln2_g, v_ln2_b):
    chip = 2 * lax.axis_index("x") + lax.axis_index("y")
    e_w, o_w = EVEN_IN // N_CHIPS, ODD_IN // N_CHIPS

    shards = dict(
        even_w_in=jnp.pad(even_w_in[0], ((0, 0), (0, EVEN_IN_PAD - e_w))), even_w_out=even_w_out[0],
        odd_w_in=jnp.pad(odd_w_in[0], ((0, 0), (0, ODD_IN_PAD - o_w))), odd_w_uq=odd_w_uq[0], odd_w_ukv=odd_w_ukv[0],
        odd_w_out=odd_w_out[0], mlp_w1_0=mlp_w1[0], mlp_w1_1=mlp_w1[1], mlp_w2_0=mlp_w2[0], mlp_w2_1=mlp_w2[1])
    names = list(shards)
    own16 = [shards[n].astype(BF16) for n in names]
    gathered = {n: lax.dynamic_update_slice(g, o[None], (chip, 0, 0))
                for n, g, o in zip(names, _gather_weights(own16, name="gather_weights"), own16)}
    w = dict(even_w_in=gathered["even_w_in"], even_w_out=gathered["even_w_out"], odd_w_in=gathered["odd_w_in"],
             odd_w_uq=gathered["odd_w_uq"], odd_w_ukv=gathered["odd_w_ukv"],
             odd_w_out=gathered["odd_w_out"].reshape(1, D_MODEL, D_MODEL),
             mlp_w1=[gathered["mlp_w1_0"], gathered["mlp_w1_1"]],
             mlp_w2=[gathered["mlp_w2_0"].reshape(1, D_FF, D_MODEL), gathered["mlp_w2_1"].reshape(1, D_FF, D_MODEL)])
    norm_rows = jnp.pad(jnp.concatenate([odd_q_norm_g[0], odd_kv_norm_g[0]])[None], ((0, 7), (0, 64)))
    norm_all = _all_gather8(norm_rows, name="gather_norm_gains")[0::16]
    qn_w, kvn_w = D_Q_RANK // N_CHIPS, D_KV_RANK // N_CHIPS
    small = dict(even_sinks=even_sinks[0], odd_q_norm_g=norm_all[:, :qn_w].reshape(D_Q_RANK),
                 odd_kv_norm_g=norm_all[:, qn_w:qn_w + kvn_w].reshape(D_KV_RANK), ln1_g=ln1_g, ln1_b=ln1_b, ln2_g=ln2_g, ln2_b=ln2_b)

    loss_row, grad_x, big, sm = _local_step(x[0], loss_target[0], w, small)
    loss = lax.psum(loss_row[0, 0], ("x", "y", "c"))

    reduced = dict(zip(names, _reduce_scatter([big[n][0] for n in names], [big[n][1] for n in names], name="grad_rs")))
    every = _all_gather8(_pack_small([sm[n] for n in SMALL_ORDER]), name="gather_small_grads")
    sm_sum = _sum_slabs(every.reshape(8, 8, SMALL_COLS), name="sum_small_grads")
    g_ln1g, g_ln1b, g_ln2g, g_ln2b, g_sinks, g_qn, g_kvn = _unpack_small(
        sm_sum, [(DEPTH, D_MODEL)] * 4 + [(1, A_Q_HEADS), (D_Q_RANK,), (D_KV_RANK,)])
    grads = dict(
        even_w_in=reduced["even_w_in"][:, :e_w][None], even_sinks=g_sinks, even_w_out=reduced["even_w_out"][None],
        odd_w_in=reduced["odd_w_in"][:, :o_w][None], odd_q_norm_g=lax.dynamic_slice_in_dim(g_qn, chip * qn_w, qn_w)[None],
        odd_kv_norm_g=lax.dynamic_slice_in_dim(g_kvn, chip * kvn_w, kvn_w)[None], odd_w_uq=reduced["odd_w_uq"][None],
        odd_w_ukv=reduced["odd_w_ukv"][None], odd_w_out=reduced["odd_w_out"][None], ln1_g=g_ln1g, ln1_b=g_ln1b,
        mlp_w1=jnp.stack([reduced["mlp_w1_0"], reduced["mlp_w1_1"]]), mlp_w2=jnp.stack([reduced["mlp_w2_0"], reduced["mlp_w2_1"]]),
        ln2_g=g_ln2g, ln2_b=g_ln2b)

    weights = dict(even_w_in=even_w_in, even_sinks=even_sinks, even_w_out=even_w_out, odd_w_in=odd_w_in, odd_q_norm_g=odd_q_norm_g,
                   odd_kv_norm_g=odd_kv_norm_g, odd_w_uq=odd_w_uq, odd_w_ukv=odd_w_ukv, odd_w_out=odd_w_out, ln1_g=ln1_g, ln1_b=ln1_b,
                   mlp_w1=mlp_w1, mlp_w2=mlp_w2, ln2_g=ln2_g, ln2_b=ln2_b)
    m_in = dict(even_w_in=m_even_w_in, even_sinks=m_even_sinks, even_w_out=m_even_w_out, odd_w_in=m_odd_w_in,
                odd_q_norm_g=m_odd_q_norm_g, odd_kv_norm_g=m_odd_kv_norm_g, odd_w_uq=m_odd_w_uq, odd_w_ukv=m_odd_w_ukv,
                odd_w_out=m_odd_w_out, ln1_g=m_ln1_g, ln1_b=m_ln1_b, mlp_w1=m_mlp_w1, mlp_w2=m_mlp_w2, ln2_g=m_ln2_g, ln2_b=m_ln2_b)
    v_in = dict(even_w_in=v_even_w_in, even_sinks=v_even_sinks, even_w_out=v_even_w_out, odd_w_in=v_odd_w_in,
                odd_q_norm_g=v_odd_q_norm_g, odd_kv_norm_g=v_odd_kv_norm_g, odd_w_uq=v_odd_w_uq, odd_w_ukv=v_odd_w_ukv,
                odd_w_out=v_odd_w_out, ln1_g=v_ln1_g, ln1_b=v_ln1_b, mlp_w1=v_mlp_w1, mlp_w2=v_mlp_w2, ln2_g=v_ln2_g, ln2_b=v_ln2_b)
    order = list(weights)
    delta, new_m, new_v = {}, {}, {}
    for n in order:
        shape = weights[n].shape
        flat = (math.prod(shape[:-1]), shape[-1])
        d, mm, vv = _adamw(weights[n].reshape(flat), grads[n].reshape(flat), m_in[n].reshape(flat), v_in[n].reshape(flat),
                           name=f"adamw_{n}")
        delta[n], new_m[n], new_v[n] = d.reshape(shape), mm.reshape(shape), vv.reshape(shape)
    return (loss, grad_x[None], *[grads[n] for n in order], *[delta[n] for n in order], *[new_m[n] for n in order],
            *[new_v[n] for n in order])
```

```python
import functools
import math

import jax
import jax.numpy as jnp
from jax import lax
from jax.experimental import pallas as pl
from jax.experimental.pallas import tpu as pltpu

F32 = jnp.float32
BF16 = jnp.bfloat16
MESH = pl.DeviceIdType.MESH

D_MODEL = 2048
SEQ = 2048
DEPTH = 2
HEAD_DIM = 64
A_Q_HEADS, A_KV_HEADS, A_WINDOW = 16, 2, 128
B_HEADS = 8
B_PATTERNS = ((128, 1), (512, 4), (2048, 16))
C_HEADS = 16
D_HEADS, D_Q_RANK, D_KV_RANK, D_NOPE, D_ROPE, D_V = 16, 512, 256, 64, 32, 64
ROPE_BASE = 10000.0
D_FF = 4 * D_MODEL
LN_EPS = 1e-5
RMS_EPS = 1e-6
ALPHA = (2 * DEPTH) ** 0.25
A_Q_W = A_Q_HEADS * HEAD_DIM
A_KV_W = A_KV_HEADS * HEAD_DIM
B_W = B_HEADS * HEAD_DIM
EVEN_IN = A_Q_W + 2 * A_KV_W + 3 * B_W * len(B_PATTERNS)
EVEN_OUT = A_Q_W + B_W
C_W = C_HEADS * HEAD_DIM
ODD_IN = 3 * C_W + D_Q_RANK + D_KV_RANK + D_ROPE
ADAM_LR, ADAM_B1, ADAM_B2, ADAM_EPS, ADAM_WD, ADAM_STEP = 0.001, 0.9, 0.999, 1e-08, 0.01, 10

N_CHIPS = 4
EVEN_IN_PAD = 1536
ODD_IN_PAD = 1024
ATT_BLK = 128
NEG = -1e30
V7X_VMEM_LIMIT = 48 * 1024 * 1024


def _params(*sem):
    return pltpu.CompilerParams(dimension_semantics=sem, vmem_limit_bytes=V7X_VMEM_LIMIT)


def _tile(n, cap):
    if n <= cap:
        return n
    t = cap - cap % 128
    while n % t:
        t -= 128
    return t


def _matmul(a, b, *, mode, out_dtype, name, epilogue=None, extra=None, alpha=1.0, tm=1024, tn=512, tk=2048):
    if mode == "nn":
        M, K = a.shape
        P, _, Np = b.shape
        tm, tn, tk = _tile(M, tm), _tile(Np, tn), _tile(K, tk)
        nj = Np // tn
        grid = (M // tm, P * nj, K // tk)
        a_spec = pl.BlockSpec((tm, tk), lambda i, j, k: (i, k))
        b_spec = pl.BlockSpec((None, tk, tn), lambda i, j, k: (j // nj, k, j % nj))
        o_spec = pl.BlockSpec((tm, tn), lambda i, j, k: (i, j))
        out_shape = (M, P * Np)
        dims = (((1,), (0,)), ((), ()))
    elif mode == "nt":
        M, N = a.shape
        P, K, Np = b.shape
        tm, tn, tk = _tile(M, tm), _tile(K, tn), _tile(Np, tk)
        nkk = Np // tk
        grid = (M // tm, K // tn, P * nkk)
        a_spec = pl.BlockSpec((tm, tk), lambda i, j, k: (i, k))
        b_spec = pl.BlockSpec((None, tn, tk), lambda i, j, k: (k // nkk, j, k % nkk))
        o_spec = pl.BlockSpec((tm, tn), lambda i, j, k: (i, j))
        out_shape = (M, K)
        dims = (((1,), (1,)), ((), ()))
    else:
        raise ValueError(mode)
    n_k = grid[2]
    n_extra = 0 if extra is None else 1
    n_out = 2 if epilogue == "relu2" else 1

    def body(*refs):
        a_ref, b_ref = refs[:2]
        e_ref = refs[2] if n_extra else None
        outs = refs[2 + n_extra:2 + n_extra + n_out]
        part = lax.dot_general(a_ref[...], b_ref[...], dims, preferred_element_type=F32)

        def finish(r):
            if epilogue == "relu2":
                outs[0][...] = r.astype(outs[0].dtype)
                rp = jnp.maximum(r, 0.0)
                outs[1][...] = (rp * rp).astype(outs[1].dtype)
            elif epilogue == "drelu2":
                outs[0][...] = (r * (2.0 * jnp.maximum(e_ref[...].astype(F32), 0.0))).astype(outs[0].dtype)
            elif epilogue == "add":
                outs[0][...] = (r + alpha * e_ref[...].astype(F32)).astype(outs[0].dtype)
            else:
                outs[0][...] = r.astype(outs[0].dtype)

        if n_k == 1:
            finish(part)
        else:
            acc = refs[-1]
            k = pl.program_id(2)

            @pl.when(k == 0)
            def _():
                acc[...] = part

            @pl.when((k > 0) & (k < n_k - 1))
            def _():
                acc[...] += part

            @pl.when(k == n_k - 1)
            def _():
                finish(acc[...] + part)

    in_specs = [a_spec, b_spec] + ([o_spec] if n_extra else [])
    shapes = [jax.ShapeDtypeStruct(out_shape, out_dtype)] * n_out
    res = pl.pallas_call(
        body, name=name, grid=grid, in_specs=in_specs,
        out_specs=[o_spec] * n_out, out_shape=shapes,
        scratch_shapes=[pltpu.VMEM((tm, tn), F32)] if n_k > 1 else [],
        compiler_params=_params("parallel", "parallel", "arbitrary"),
    )(a, b, *([extra] if n_extra else []))
    return res if n_out > 1 else res[0]


def _matmul_tn(a, g, *, shards, name, tm=1024, tn=512):
    M, K = a.shape
    P = shards
    Np = g.shape[1] // P
    tm, tn = _tile(K, tm), _tile(Np, tn)
    nj = Np // tn

    def body(a_ref, g_ref, o_ref, o16_ref):
        r = lax.dot_general(a_ref[...], g_ref[...], (((0,), (0,)), ((), ())), preferred_element_type=F32)
        o_ref[...] = r
        o16_ref[...] = r.astype(BF16)

    o_spec = pl.BlockSpec((None, tm, tn), lambda i, j: (j // nj, i, j % nj))
    return pl.pallas_call(
        body, name=name, grid=(K // tm, P * nj),
        in_specs=[pl.BlockSpec((M, tm), lambda i, j: (0, i)), pl.BlockSpec((M, tn), lambda i, j: (0, j))],
        out_specs=[o_spec, o_spec],
        out_shape=[jax.ShapeDtypeStruct((P, K, Np), F32), jax.ShapeDtypeStruct((P, K, Np), BF16)],
        compiler_params=_params("parallel", "parallel"),
    )(a, g)


def _norm_fwd(x, res, g, b, *, center, eps, alpha, name, rows=256):
    S, W = x.shape
    has_res = res is not None
    rows = _tile(S, rows)

    def body(*refs):
        x_ref = refs[0]
        r_ref = refs[1] if has_res else None
        g_ref = refs[1 + has_res]
        b_ref = refs[2 + has_res] if center else None
        y_ref, y16_ref, u_ref = refs[-3:]
        u = x_ref[...]
        if has_res:
            u = alpha * u + r_ref[...]
        if center:
            mu = jnp.mean(u, axis=1, keepdims=True)
            xc = u - mu
        else:
            xc = u
        var = jnp.mean(xc * xc, axis=1, keepdims=True)
        y = xc * lax.rsqrt(var + eps) * g_ref[...]
        if center:
            y = y + b_ref[...]
        y_ref[...] = y
        y16_ref[...] = y.astype(BF16)
        u_ref[...] = u

    row_spec = pl.BlockSpec((rows, W), lambda i: (i, 0))
    vec_spec = pl.BlockSpec((1, W), lambda i: (0, 0))
    ins = [x] + ([res] if has_res else []) + [g.reshape(1, W)] + ([b.reshape(1, W)] if center else [])
    specs = [row_spec] + ([row_spec] if has_res else []) + [vec_spec] + ([vec_spec] if center else [])
    return pl.pallas_call(
        body, name=name, grid=(S // rows,), in_specs=specs,
        out_specs=[row_spec, row_spec, row_spec],
        out_shape=[jax.ShapeDtypeStruct((S, W), F32), jax.ShapeDtypeStruct((S, W), BF16), jax.ShapeDtypeStruct((S, W), F32)],
        compiler_params=_params("parallel"),
    )(*ins)


def _norm_bwd(dy, u, g, *, center, eps, name, rows=256, after=None):
    S, W = u.shape
    rows = _tile(S, rows)

    def body(dy_ref, u_ref, g_ref, *rest):
        du_ref, du16_ref, dg_ref, db_ref = rest[-4:]
        i = pl.program_id(0)
        uu = u_ref[...]
        dyv = dy_ref[...]
        if center:
            xc = uu - jnp.mean(uu, axis=1, keepdims=True)
        else:
            xc = uu
        rstd = lax.rsqrt(jnp.mean(xc * xc, axis=1, keepdims=True) + eps)
        xhat = xc * rstd
        dxh = dyv * g_ref[...]
        c2 = jnp.mean(dxh * xhat, axis=1, keepdims=True)
        du = dxh - xhat * c2
        if center:
            du = du - jnp.mean(dxh, axis=1, keepdims=True)
        du = du * rstd
        du_ref[...] = du
        du16_ref[...] = du.astype(BF16)

        @pl.when(i == 0)
        def _():
            dg_ref[...] = jnp.zeros_like(dg_ref)
            db_ref[...] = jnp.zeros_like(db_ref)

        dg_ref[...] += jnp.sum(dyv * xhat, axis=0, keepdims=True)
        db_ref[...] += jnp.sum(dyv, axis=0, keepdims=True)

    row_spec = pl.BlockSpec((rows, W), lambda i: (i, 0))
    vec_spec = pl.BlockSpec((1, W), lambda i: (0, 0))
    return pl.pallas_call(
        body, name=name, grid=(S // rows,),
        in_specs=[row_spec, row_spec, vec_spec] + ([] if after is None else [pl.BlockSpec(memory_space=pl.ANY)]),
        out_specs=[row_spec, row_spec, vec_spec, vec_spec],
        out_shape=[jax.ShapeDtypeStruct((S, W), F32), jax.ShapeDtypeStruct((S, W), BF16),
                   jax.ShapeDtypeStruct((1, W), F32), jax.ShapeDtypeStruct((1, W), F32)],
        compiler_params=_params("arbitrary"),
    )(dy, u, g.reshape(1, W), *([] if after is None else [after]))


def _loss_head(y, target, *, name, rows=256):
    S, W = y.shape
    rows = _tile(S, rows)

    def body(y_ref, t_ref, dy_ref, l_ref):
        i = pl.program_id(0)
        e = y_ref[...] - t_ref[...]
        dy_ref[...] = e * (1.0 / W)

        @pl.when(i == 0)
        def _():
            l_ref[...] = jnp.zeros_like(l_ref)

        l_ref[...] += jnp.full(l_ref.shape, 0.5 / W * jnp.sum(e * e), F32)

    row_spec = pl.BlockSpec((rows, W), lambda i: (i, 0))
    return pl.pallas_call(
        body, name=name, grid=(S // rows,), in_specs=[row_spec, row_spec],
        out_specs=[row_spec, pl.BlockSpec((1, 128), lambda i: (0, 0))],
        out_shape=[jax.ShapeDtypeStruct((S, W), F32), jax.ShapeDtypeStruct((1, 128), F32)],
        compiler_params=_params("arbitrary"),
    )(y, target)


def _adamw(w, g, m, v, *, name, rows=256):
    R, C = w.shape
    rows = _tile(R, rows) if R % 8 == 0 else R
    c1 = 1.0 / (1.0 - ADAM_B1 ** ADAM_STEP)
    c2 = 1.0 / (1.0 - ADAM_B2 ** ADAM_STEP)

    def body(w_ref, g_ref, m_ref, v_ref, d_ref, mo_ref, vo_ref):
        gg = g_ref[...]
        mn = ADAM_B1 * m_ref[...] + (1.0 - ADAM_B1) * gg
        vn = ADAM_B2 * v_ref[...] + (1.0 - ADAM_B2) * (gg * gg)
        d_ref[...] = -ADAM_LR * ((mn * c1) / (jnp.sqrt(vn * c2) + ADAM_EPS) + ADAM_WD * w_ref[...])
        mo_ref[...] = mn
        vo_ref[...] = vn

    spec = pl.BlockSpec((rows, C), lambda i: (i, 0))
    shp = jax.ShapeDtypeStruct((R, C), F32)
    return pl.pallas_call(
        body, name=name, grid=(R // rows,), in_specs=[spec] * 4, out_specs=[spec] * 3, out_shape=[shp] * 3,
        compiler_params=_params("parallel"),
    )(w, g, m, v)


def _rope(x1, x2, cos, sin, *, name, rows=512):
    H, S, W = x1.shape
    rows = _tile(S, rows)

    def body(x1_ref, x2_ref, c_ref, s_ref, y1_ref, y2_ref):
        t1 = jnp.sum(x1_ref[...], axis=0)
        t2 = jnp.sum(x2_ref[...], axis=0)
        c = c_ref[...]
        s = s_ref[...]
        y1_ref[...] = t1 * c - t2 * s
        y2_ref[...] = t1 * s + t2 * c

    xs = pl.BlockSpec((H, rows, W), lambda i: (0, i, 0))
    ts = pl.BlockSpec((rows, W), lambda i: (i, 0))
    shp = jax.ShapeDtypeStruct((S, W), F32)
    return pl.pallas_call(
        body, name=name, grid=(S // rows,), in_specs=[xs, xs, ts, ts], out_specs=[ts, ts], out_shape=[shp, shp],
        compiler_params=_params("parallel"),
    )(x1, x2, cos, sin)


def _band_mask(blk, n, n_back):
    row = lax.broadcasted_iota(jnp.int32, (blk, 2 * blk), 0)
    col = lax.broadcasted_iota(jnp.int32, (blk, 2 * blk), 1)
    rel = blk + row - col
    return rel, (rel >= 0) & (rel <= n_back) & ((col >= blk) | (n >= 1))


def _band_fwd(q, k, v, slopes, sinks, *, hps, nb_seq, n_back, scale, dist_scale, name):
    blk = ATT_BLK
    Hq, T, d = q.shape
    Hk = k.shape[0]
    group = Hq // Hk
    kps = max(hps // group, 1)
    use_sink = sinks is not None

    def body(*refs):
        q_ref, k_ref, v_ref, slope_ref = refs[:4]
        sink_ref = refs[4] if use_sink else None
        o_ref, lse_ref = refs[-2:]
        i = pl.program_id(1)
        start = pl.multiple_of(i * blk, blk)
        rel, valid = _band_mask(blk, i % nb_seq, n_back)
        relf = rel.astype(F32)
        for hh in range(hps):
            h = pl.program_id(0) * hps + hh
            kk = k_ref[hh // group, pl.ds(start, 2 * blk), :]
            vv = v_ref[hh // group, pl.ds(start, 2 * blk), :]
            s = lax.dot_general(q_ref[hh], kk, (((1,), (1,)), ((), ())), preferred_element_type=F32) * scale
            s = jnp.where(valid, s - (slope_ref[h] * dist_scale) * relf, NEG)
            m = jnp.max(s, axis=1, keepdims=True)
            if use_sink:
                m = jnp.maximum(m, sink_ref[h])
            p = jnp.where(valid, jnp.exp(s - m), 0.0)
            l = jnp.sum(p, axis=1, keepdims=True)
            if use_sink:
                l = l + jnp.exp(sink_ref[h] - m)
            o_ref[hh] = jnp.dot(p.astype(BF16), vv, preferred_element_type=F32) / l
            lse_ref[hh] = m + jnp.log(l)

    smem = pl.BlockSpec(memory_space=pltpu.SMEM)
    qs = pl.BlockSpec((hps, blk, d), lambda g, i: (g, i, 0))
    ks = pl.BlockSpec((kps, T + blk, d), lambda g, i: (g, 0, 0))
    ins = [q, k, v, slopes] + ([sinks] if use_sink else [])
    return pl.pallas_call(
        body, name=name, grid=(Hq // hps, T // blk), in_specs=[qs, ks, ks, smem] + ([smem] if use_sink else []),
        out_specs=[qs, pl.BlockSpec((hps, blk, 1), lambda g, i: (g, i, 0))],
        out_shape=[jax.ShapeDtypeStruct((Hq, T, d), F32), jax.ShapeDtypeStruct((Hq, T, 1), F32)],
        compiler_params=_params("parallel", "parallel"),
    )(*ins)


def _band_bwd(q, k, v, o, do, lse, dlse, slopes, sinks, *, hps, nb_seq, n_back, scale, dist_scale, name):
    blk = ATT_BLK
    Hq, T, d = q.shape
    Hk = k.shape[0]
    group = Hq // Hk
    kps = max(hps // group, 1)
    use_sink, use_dlse = sinks is not None, dlse is not None

    def body(*refs):
        q_ref, k_ref, v_ref, o_ref, do_ref, lse_ref = refs[:6]
        pos = 6
        dlse_ref = refs[pos] if use_dlse else None
        pos += use_dlse
        slope_ref = refs[pos]
        pos += 1
        sink_ref = refs[pos] if use_sink else None
        pos += use_sink
        dq_ref, dk_ref, dv_ref = refs[pos:pos + 3]
        dsink_ref = refs[pos + 3] if use_sink else None
        i = pl.program_id(1)
        start = pl.multiple_of(i * blk, blk)
        rel, valid = _band_mask(blk, i % nb_seq, n_back)
        relf = rel.astype(F32)

        @pl.when(i == 0)
        def _():
            dk_ref[...] = jnp.zeros_like(dk_ref)
            dv_ref[...] = jnp.zeros_like(dv_ref)
            if use_sink:
                dsink_ref[...] = jnp.zeros_like(dsink_ref)

        for kh in range(kps):
            dk_acc = jnp.zeros((2 * blk, d), F32)
            dv_acc = jnp.zeros((2 * blk, d), F32)
            kk = k_ref[kh, pl.ds(start, 2 * blk), :]
            vv = v_ref[kh, pl.ds(start, 2 * blk), :]
            for hh in range(kh * min(group, hps), (kh + 1) * min(group, hps)):
                h = pl.program_id(0) * hps + hh
                qb = q_ref[hh]
                dob = do_ref[hh]
                do16 = dob.astype(BF16)
                lse = lse_ref[hh]
                c = -jnp.sum(dob * o_ref[hh], axis=1, keepdims=True)
                if use_dlse:
                    c = c + dlse_ref[hh]
                s = lax.dot_general(qb, kk, (((1,), (1,)), ((), ())), preferred_element_type=F32) * scale
                s = jnp.where(valid, s - (slope_ref[h] * dist_scale) * relf, NEG)
                p = jnp.where(valid, jnp.exp(s - lse), 0.0)
                dp = lax.dot_general(do16, vv, (((1,), (1,)), ((), ())), preferred_element_type=F32)
                ds16 = (p * (dp + c) * scale).astype(BF16)
                dq_ref[hh] = jnp.dot(ds16, kk, preferred_element_type=F32)
                dk_acc = dk_acc + lax.dot_general(ds16, qb, (((0,), (0,)), ((), ())), preferred_element_type=F32)
                dv_acc = dv_acc + lax.dot_general(p.astype(BF16), do16, (((0,), (0,)), ((), ())), preferred_element_type=F32)
                if use_sink:
                    dsink_ref[hh] += jnp.full((1, 128), jnp.sum(jnp.exp(sink_ref[h] - lse) * c), F32)
            dk_ref[kh, pl.ds(start, 2 * blk), :] += dk_acc
            dv_ref[kh, pl.ds(start, 2 * blk), :] += dv_acc

    smem = pl.BlockSpec(memory_space=pltpu.SMEM)
    qs = pl.BlockSpec((hps, blk, d), lambda g, i: (g, i, 0))
    ls = pl.BlockSpec((hps, blk, 1), lambda g, i: (g, i, 0))
    ks = pl.BlockSpec((kps, T + blk, d), lambda g, i: (g, 0, 0))
    in_specs = [qs, ks, ks, qs, qs, ls] + ([ls] if use_dlse else []) + [smem] + ([smem] if use_sink else [])
    ins = [q, k, v, o, do, lse] + ([dlse] if use_dlse else []) + [slopes] + ([sinks] if use_sink else [])
    out_specs = [qs, ks, ks]
    out_shape = [jax.ShapeDtypeStruct((Hq, T, d), F32), jax.ShapeDtypeStruct((Hk, T + blk, d), F32),
                 jax.ShapeDtypeStruct((Hk, T + blk, d), F32)]
    if use_sink:
        out_specs.append(pl.BlockSpec((hps, 1, 128), lambda g, i: (g, 0, 0)))
        out_shape.append(jax.ShapeDtypeStruct((Hq, 1, 128), F32))
    return pl.pallas_call(
        body, name=name, grid=(Hq // hps, T // blk), in_specs=in_specs, out_specs=out_specs, out_shape=out_shape,
        compiler_params=_params("parallel", "arbitrary"),
    )(*ins)


def _band_geometry(blk, nb_seq, n_prev):
    def geo(i):
        seq = i // nb_seq
        n = i % nb_seq
        return seq, n, jnp.maximum(n - n_prev, 0)
    return geo


def _attn_fwd(q, k, v, slopes, sinks, *, blk, nb_seq, n_prev, n_back, scale, dist_scale, name):
    Hq, T, dqk = q.shape
    Hk, _, dv = v.shape
    group = Hq // Hk
    use_bias, use_sink = slopes is not None, sinks is not None
    geo = _band_geometry(blk, nb_seq, n_prev)

    def body(*refs):
        q_ref, k_ref, v_ref = refs[:3]
        slope_ref = refs[3] if use_bias else None
        sink_ref = refs[3 + use_bias] if use_sink else None
        o_ref, lse_ref = refs[-2:]
        h = pl.program_id(0)
        seq, n, lo = geo(pl.program_id(1))
        qb = q_ref[...]
        diff = lax.broadcasted_iota(jnp.int32, (blk, blk), 0) - lax.broadcasted_iota(jnp.int32, (blk, blk), 1)
        if use_sink:
            m0 = jnp.full((blk, 1), sink_ref[h], F32)
            l0 = jnp.ones((blk, 1), F32)
        else:
            m0 = jnp.full((blk, 1), NEG, F32)
            l0 = jnp.zeros((blk, 1), F32)

        def step(j, carry):
            m, l, acc = carry
            start = pl.multiple_of((seq * nb_seq + j) * blk, blk)
            kb = k_ref[pl.ds(start, blk), :]
            vb = v_ref[pl.ds(start, blk), :]
            s = lax.dot_general(qb, kb, (((1,), (1,)), ((), ())), preferred_element_type=F32) * scale
            rel = (n - j) * blk + diff
            valid = (rel >= 0) & (rel <= n_back)
            if use_bias:
                s = s - (slope_ref[h] * dist_scale) * rel.astype(F32)
            s = jnp.where(valid, s, NEG)
            m_new = jnp.maximum(m, jnp.max(s, axis=1, keepdims=True))
            p = jnp.where(valid, jnp.exp(s - m_new), 0.0)
            a = jnp.exp(m - m_new)
            l = a * l + jnp.sum(p, axis=1, keepdims=True)
            acc = a * acc + jnp.dot(p.astype(BF16), vb, preferred_element_type=F32)
            return m_new, l, acc

        m, l, acc = lax.fori_loop(lo, n + 1, step, (m0, l0, jnp.zeros((blk, dv), F32)))
        o_ref[...] = acc / l
        lse_ref[...] = m + jnp.log(l)

    smem = pl.BlockSpec(memory_space=pltpu.SMEM)
    in_specs = [pl.BlockSpec((None, blk, dqk), lambda h, i: (h, i, 0)),
                pl.BlockSpec((None, T, dqk), lambda h, i: (h // group, 0, 0)),
                pl.BlockSpec((None, T, dv), lambda h, i: (h // group, 0, 0))]
    ins = [q, k, v]
    if use_bias:
        in_specs.append(smem)
        ins.append(slopes)
    if use_sink:
        in_specs.append(smem)
        ins.append(sinks)
    return pl.pallas_call(
        body, name=name, grid=(Hq, T // blk), in_specs=in_specs,
        out_specs=[pl.BlockSpec((None, blk, dv), lambda h, i: (h, i, 0)), pl.BlockSpec((None, blk, 1), lambda h, i: (h, i, 0))],
        out_shape=[jax.ShapeDtypeStruct((Hq, T, dv), F32), jax.ShapeDtypeStruct((Hq, T, 1), F32)],
        compiler_params=_params("parallel", "parallel"),
    )(*ins)


def _attn_bwd(q, k, v, o, do, lse, dlse, slopes, sinks, *, blk, nb_seq, n_prev, n_back, scale, dist_scale, name):
    Hq, T, dqk = q.shape
    Hk, _, dv = v.shape
    group = Hq // Hk
    use_bias, use_sink, use_dlse = slopes is not None, sinks is not None, dlse is not None
    geo = _band_geometry(blk, nb_seq, n_prev)

    def body(*refs):
        q_ref, k_ref, v_ref, o_ref, do_ref, lse_ref = refs[:6]
        pos = 6
        dlse_ref = refs[pos] if use_dlse else None
        pos += use_dlse
        slope_ref = refs[pos] if use_bias else None
        pos += use_bias
        sink_ref = refs[pos] if use_sink else None
        pos += use_sink
        dq_ref, dk_ref, dv_ref = refs[pos:pos + 3]
        dsink_ref = refs[pos + 3] if use_sink else None
        h = pl.program_id(0)
        i = pl.program_id(1)
        seq, n, lo = geo(i)
        qb = q_ref[...]
        dob = do_ref[...]
        do16 = dob.astype(BF16)
        lse = lse_ref[...]
        c = -jnp.sum(dob * o_ref[...], axis=1, keepdims=True)
        if use_dlse:
            c = c + dlse_ref[...]
        diff = lax.broadcasted_iota(jnp.int32, (blk, blk), 0) - lax.broadcasted_iota(jnp.int32, (blk, blk), 1)

        @pl.when((h % group == 0) & (i == 0))
        def _():
            dk_ref[...] = jnp.zeros_like(dk_ref)
            dv_ref[...] = jnp.zeros_like(dv_ref)

        def step(j, dq):
            start = pl.multiple_of((seq * nb_seq + j) * blk, blk)
            kb = k_ref[pl.ds(start, blk), :]
            vb = v_ref[pl.ds(start, blk), :]
            s = lax.dot_general(qb, kb, (((1,), (1,)), ((), ())), preferred_element_type=F32) * scale
            rel = (n - j) * blk + diff
            valid = (rel >= 0) & (rel <= n_back)
            if use_bias:
                s = s - (slope_ref[h] * dist_scale) * rel.astype(F32)
            p = jnp.where(valid, jnp.exp(jnp.where(valid, s, NEG) - lse), 0.0)
            dp = lax.dot_general(do16, vb, (((1,), (1,)), ((), ())), preferred_element_type=F32)
            ds16 = (p * (dp + c) * scale).astype(BF16)
            dk_ref[pl.ds(start, blk), :] += lax.dot_general(ds16, qb, (((0,), (0,)), ((), ())), preferred_element_type=F32)
            dv_ref[pl.ds(start, blk), :] += lax.dot_general(p.astype(BF16), do16, (((0,), (0,)), ((), ())), preferred_element_type=F32)
            return dq + jnp.dot(ds16, kb, preferred_element_type=F32)

        dq_ref[...] = lax.fori_loop(lo, n + 1, step, jnp.zeros((blk, dqk), F32))
        if use_sink:
            @pl.when(i == 0)
            def _():
                dsink_ref[...] = jnp.zeros_like(dsink_ref)

            dsink_ref[...] += jnp.full(dsink_ref.shape, jnp.sum(jnp.exp(sink_ref[h] - lse) * c), F32)

    smem = pl.BlockSpec(memory_space=pltpu.SMEM)
    qs = pl.BlockSpec((None, blk, dqk), lambda h, i: (h, i, 0))
    os_ = pl.BlockSpec((None, blk, dv), lambda h, i: (h, i, 0))
    ls = pl.BlockSpec((None, blk, 1), lambda h, i: (h, i, 0))
    ks = pl.BlockSpec((None, T, dqk), lambda h, i: (h // group, 0, 0))
    vs = pl.BlockSpec((None, T, dv), lambda h, i: (h // group, 0, 0))
    in_specs = [qs, ks, vs, os_, os_, ls]
    ins = [q, k, v, o, do, lse]
    if use_dlse:
        in_specs.append(ls)
        ins.append(dlse)
    if use_bias:
        in_specs.append(smem)
        ins.append(slopes)
    if use_sink:
        in_specs.append(smem)
        ins.append(sinks)
    out_specs = [qs, ks, vs]
    out_shape = [jax.ShapeDtypeStruct((Hq, T, dqk), F32), jax.ShapeDtypeStruct((Hk, T, dqk), F32),
                 jax.ShapeDtypeStruct((Hk, T, dv), F32)]
    if use_sink:
        out_specs.append(pl.BlockSpec((None, 1, 128), lambda h, i: (h, 0, 0)))
        out_shape.append(jax.ShapeDtypeStruct((Hq, 1, 128), F32))
    return pl.pallas_call(
        body, name=name, grid=(Hq, T // blk), in_specs=in_specs, out_specs=out_specs, out_shape=out_shape,
        compiler_params=_params("arbitrary", "arbitrary"),
    )(*ins)


def _tri_sum(x, upper):
    hi = x.astype(BF16)
    r = x - hi.astype(F32)
    mid = r.astype(BF16)
    lo = (r - mid.astype(F32)).astype(BF16)
    return (jnp.dot(hi, upper, preferred_element_type=F32) + jnp.dot(mid, upper, preferred_element_type=F32)
            + jnp.dot(lo, upper, preferred_element_type=F32))


def _stick_terms(qb, kb, n, j, blk, scale, diff):
    z = lax.dot_general(qb, kb, (((1,), (1,)), ((), ())), preferred_element_type=F32) * scale
    e = jnp.exp(-jnp.abs(z))
    l1p = jnp.log(1.0 + e)
    lb = jnp.minimum(z, 0.0) - l1p
    strict = ((n - j) * blk + diff) > 0
    lk = jnp.where(strict, lb - z, 0.0)
    return z, e, lb, lk, strict


def _stick_fwd(q, k, v, *, blk, scale, name):
    H, T, dh = q.shape

    def body(q_ref, k_ref, v_ref, o_ref, tot_ref):
        n = pl.program_id(1)
        qb = q_ref[...]
        r_i = lax.broadcasted_iota(jnp.int32, (blk, blk), 0)
        c_i = lax.broadcasted_iota(jnp.int32, (blk, blk), 1)
        diff = r_i - c_i
        upper = (r_i > c_i).astype(BF16)

        def step(t, carry):
            run, acc = carry
            j = n - t
            start = pl.multiple_of(j * blk, blk)
            kb = k_ref[pl.ds(start, blk), :]
            vb = v_ref[pl.ds(start, blk), :]
            _, _, lb, lk, strict = _stick_terms(qb, kb, n, j, blk, scale, diff)
            w = jnp.where(strict, jnp.exp(lb + run + _tri_sum(lk, upper)), 0.0)
            acc = acc + jnp.dot(w.astype(BF16), vb, preferred_element_type=F32)
            return run + jnp.sum(lk, axis=1, keepdims=True), acc

        run, acc = lax.fori_loop(0, n + 1, step, (jnp.zeros((blk, 1), F32), jnp.zeros((blk, dh), F32)))
        o_ref[...] = acc
        tot_ref[...] = run

    qs = pl.BlockSpec((None, blk, dh), lambda h, i: (h, i, 0))
    ks = pl.BlockSpec((None, T, dh), lambda h, i: (h, 0, 0))
    ts = pl.BlockSpec((None, blk, 1), lambda h, i: (h, i, 0))
    return pl.pallas_call(
        body, name=name, grid=(H, T // blk), in_specs=[qs, ks, ks], out_specs=[qs, ts],
        out_shape=[jax.ShapeDtypeStruct((H, T, dh), F32), jax.ShapeDtypeStruct((H, T, 1), F32)],
        compiler_params=_params("parallel", "parallel"),
    )(q, k, v)


def _stick_bwd(q, k, v, tot, do, *, blk, scale, name):
    H, T, dh = q.shape

    def body(q_ref, k_ref, v_ref, tot_ref, do_ref, dq_ref, dk_ref, dv_ref):
        n = pl.program_id(1)
        qb = q_ref[...]
        do16 = do_ref[...].astype(BF16)
        tot = tot_ref[...]
        r_i = lax.broadcasted_iota(jnp.int32, (blk, blk), 0)
        c_i = lax.broadcasted_iota(jnp.int32, (blk, blk), 1)
        diff = r_i - c_i
        upto = (r_i <= c_i).astype(BF16)
        below = (r_i < c_i).astype(BF16)

        @pl.when(n == 0)
        def _():
            dk_ref[...] = jnp.zeros_like(dk_ref)
            dv_ref[...] = jnp.zeros_like(dv_ref)

        def step(j, carry):
            run, grun, dq = carry
            start = pl.multiple_of(j * blk, blk)
            kb = k_ref[pl.ds(start, blk), :]
            vb = v_ref[pl.ds(start, blk), :]
            z, e, lb, lk, strict = _stick_terms(qb, kb, n, j, blk, scale, diff)
            w = jnp.where(strict, jnp.exp(lb + tot - (run + _tri_sum(lk, upto))), 0.0)
            dw = lax.dot_general(do16, vb, (((1,), (1,)), ((), ())), preferred_element_type=F32)
            g = w * dw
            before = grun + _tri_sum(g, below)
            inv = 1.0 / (1.0 + e)
            sig = jnp.where(z >= 0, inv, e * inv)
            dz16 = (jnp.where(strict, g * (1.0 - sig) - sig * before, 0.0) * scale).astype(BF16)
            dk_ref[pl.ds(start, blk), :] += lax.dot_general(dz16, qb, (((0,), (0,)), ((), ())), preferred_element_type=F32)
            dv_ref[pl.ds(start, blk), :] += lax.dot_general(w.astype(BF16), do16, (((0,), (0,)), ((), ())), preferred_element_type=F32)
            dq = dq + jnp.dot(dz16, kb, preferred_element_type=F32)
            return run + jnp.sum(lk, axis=1, keepdims=True), grun + jnp.sum(g, axis=1, keepdims=True), dq

        zero = jnp.zeros((blk, 1), F32)
        _, _, dq = lax.fori_loop(0, n + 1, step, (zero, zero, jnp.zeros((blk, dh), F32)))
        dq_ref[...] = dq

    qs = pl.BlockSpec((None, blk, dh), lambda h, i: (h, i, 0))
    ks = pl.BlockSpec((None, T, dh), lambda h, i: (h, 0, 0))
    ts = pl.BlockSpec((None, blk, 1), lambda h, i: (h, i, 0))
    shp = jax.ShapeDtypeStruct((H, T, dh), F32)
    return pl.pallas_call(
        body, name=name, grid=(H, T // blk), in_specs=[qs, ks, ks, ts, qs], out_specs=[qs, ks, ks],
        out_shape=[shp, shp, shp],
        compiler_params=_params("arbitrary", "arbitrary"),
    )(q, k, v, tot, do)


def _mix_fwd(outs, lses, *, name, rows=512):
    H, T, dh = outs[0].shape
    rows = _tile(T, rows)

    def body(o0, o1, o2, l0, l1, l2, y_ref):
        ls = [l0[...], l1[...], l2[...]]
        mx = jnp.maximum(jnp.maximum(ls[0], ls[1]), ls[2])
        es = [jnp.exp(x - mx) for x in ls]
        den = es[0] + es[1] + es[2]
        y_ref[...] = (es[0] * o0[...] + es[1] * o1[...] + es[2] * o2[...]) / den

    os_ = pl.BlockSpec((None, rows, dh), lambda h, i: (h, i, 0))
    ls_ = pl.BlockSpec((None, rows, 1), lambda h, i: (h, i, 0))
    return pl.pallas_call(
        body, name=name, grid=(H, T // rows), in_specs=[os_] * 3 + [ls_] * 3, out_specs=os_,
        out_shape=jax.ShapeDtypeStruct((H, T, dh), F32),
        compiler_params=_params("parallel", "parallel"),
    )(*outs, *lses)


def _mix_bwd(outs, lses, dy, *, name, rows=512):
    H, T, dh = outs[0].shape
    rows = _tile(T, rows)

    def body(o0, o1, o2, l0, l1, l2, dy_ref, d0, d1, d2, g0, g1, g2):
        ls = [l0[...], l1[...], l2[...]]
        mx = jnp.maximum(jnp.maximum(ls[0], ls[1]), ls[2])
        es = [jnp.exp(x - mx) for x in ls]
        den = es[0] + es[1] + es[2]
        mix = [e / den for e in es]
        dyv = dy_ref[...]
        dm = [jnp.sum(dyv * o[...], axis=1, keepdims=True) for o in (o0, o1, o2)]
        avg = mix[0] * dm[0] + mix[1] * dm[1] + mix[2] * dm[2]
        for d_ref, g_ref, w, t in zip((d0, d1, d2), (g0, g1, g2), mix, dm):
            d_ref[...] = w * dyv
            g_ref[...] = w * (t - avg)

    os_ = pl.BlockSpec((None, rows, dh), lambda h, i: (h, i, 0))
    ls_ = pl.BlockSpec((None, rows, 1), lambda h, i: (h, i, 0))
    so = jax.ShapeDtypeStruct((H, T, dh), F32)
    sl = jax.ShapeDtypeStruct((H, T, 1), F32)
    res = pl.pallas_call(
        body, name=name, grid=(H, T // rows), in_specs=[os_] * 3 + [ls_] * 3 + [os_], out_specs=[os_] * 3 + [ls_] * 3,
        out_shape=[so] * 3 + [sl] * 3,
        compiler_params=_params("parallel", "parallel"),
    )(*outs, *lses, dy)
    return res[:3], res[3:]


def _position():
    return lax.axis_index("x"), lax.axis_index("y"), lax.axis_index("c")


def _other_chips(x, y):
    return [(1 - x, y), (x, 1 - y), (1 - x, 1 - y)]


HBM_SPEC = pl.BlockSpec(memory_space=pltpu.HBM)
SEM_SPEC = pl.BlockSpec(memory_space=pltpu.SEMAPHORE)
ANY_SPEC = pl.BlockSpec(memory_space=pl.ANY)
SPLIT_COPY = pltpu.CompilerParams(has_side_effects=pltpu.SideEffectType.DATAFLOW_SIDE_EFFECTING)


def _in_hbm(a):
    return pltpu.with_memory_space_constraint(a, pltpu.HBM)


def _chip_copies(srcs, lands, send_sems, recv_sems, src_of, dst_of):
    x, y, c = _position()
    return [pltpu.make_async_remote_copy(
        src_ref=src_of(srcs[w], chip, c), dst_ref=dst_of(lands[w], j, chip, (x, y), c), send_sem=send_sems[3 * w + j],
        recv_sem=recv_sems[3 * w + j], device_id=(chip[0], chip[1], c), device_id_type=MESH)
        for w in range(len(srcs)) for j, chip in enumerate(_other_chips(x, y))]


def _copies_start(srcs, land_shapes, after, src_of, dst_of, *, name):
    n = len(srcs)
    m = 3 * n

    def body(*refs):
        src_refs, land_refs = refs[:n], refs[n:2 * n]
        send_sems, recv_sems = refs[2 * n + 1:2 * n + 1 + m], refs[2 * n + 1 + m:2 * n + 1 + 2 * m]
        token = refs[-1]
        for cp in _chip_copies(src_refs, land_refs, send_sems, recv_sems, src_of, dst_of):
            cp.start()
        token[...] = jnp.zeros_like(token)

    lands = [_in_hbm(lax.empty(s.shape, s.dtype)) for s in land_shapes]
    out_shape = ([pltpu.SemaphoreType.DMA(())] * (2 * m)
                 + [pltpu.HBM(a.shape, a.dtype) for a in srcs] + [pltpu.HBM(s.shape, s.dtype) for s in land_shapes]
                 + [jax.ShapeDtypeStruct((8, 128), F32)])
    res = pl.pallas_call(
        body, name=name, in_specs=[HBM_SPEC] * (2 * n) + [ANY_SPEC],
        out_specs=[SEM_SPEC] * (2 * m) + [HBM_SPEC] * (2 * n) + [pl.BlockSpec(memory_space=pltpu.VMEM)],
        out_shape=out_shape, input_output_aliases={i: 2 * m + i for i in range(2 * n)}, compiler_params=SPLIT_COPY,
    )(*[_in_hbm(a) for a in srcs], *lands, after)
    return (list(res[:m]), list(res[m:2 * m]), list(res[2 * m:2 * m + n]), list(res[2 * m + n:2 * m + 2 * n]), res[-1])


def _copies_wait(started, after, src_of, dst_of, *, name):
    send_sems, recv_sems, srcs, lands, _ = started
    n = len(srcs)
    m = 3 * n

    def body(*refs):
        src_refs, land_refs = refs[:n], refs[n:2 * n]
        send_refs, recv_refs = refs[2 * n:2 * n + m], refs[2 * n + m:2 * n + 2 * m]
        for cp in _chip_copies(src_refs, land_refs, send_refs, recv_refs, src_of, dst_of):
            cp.wait_send()
            cp.wait_recv()

    res = pl.pallas_call(
        body, name=name, in_specs=[HBM_SPEC] * (2 * n) + [SEM_SPEC] * (2 * m) + [ANY_SPEC], out_specs=[HBM_SPEC] * (2 * n),
        out_shape=[pltpu.HBM(a.shape, a.dtype) for a in srcs + lands],
        input_output_aliases={i: i for i in range(2 * n)}, compiler_params=SPLIT_COPY,
    )(*srcs, *lands, *send_sems, *recv_sems, after)
    return list(res[:n]), list(res[n:])


def _half_rows(ref, core):
    half = ref.shape[-2] // 2
    return pl.ds(core * half, half)


def _gather_src(src, chip, core):
    return src.at[_half_rows(src, core), :]


def _gather_dst(land, j, chip, me, core):
    return land.at[2 * me[0] + me[1], _half_rows(land, core), :]


def _gather_landed(land, j, chip, me, core):
    return land.at[2 * chip[0] + chip[1], _half_rows(land, core), :]


def _exchange_src(src, chip, core):
    return src.at[2 * chip[0] + chip[1]]


def _exchange_dst(land, j, chip, me, core):
    return land.at[j]


def _forward_halves(bufs, *, name):
    n = len(bufs)

    def body(*refs):
        ins, outs = refs[:n], refs[n:2 * n]
        send_sems, recv_sems = refs[2 * n:]
        x, y, c = _position()
        chips = _other_chips(x, y)

        def copy(w, j, chip, core):
            slab = _gather_landed(ins[w], j, chip, (x, y), core)
            return pltpu.make_async_remote_copy(src_ref=slab, dst_ref=_gather_landed(outs[w], j, chip, (x, y), core),
                                                send_sem=send_sems.at[w, j], recv_sem=recv_sems.at[w, j],
                                                device_id=(x, y, 1 - c), device_id_type=MESH)

        sends = [copy(w, j, chip, c) for w in range(n) for j, chip in enumerate(chips)]
        for cp in sends:
            cp.start()
        for w in range(n):
            for j, chip in enumerate(chips):
                copy(w, j, chip, 1 - c).wait_recv()
        for cp in sends:
            cp.wait_send()

    return pl.pallas_call(
        body, name=name, in_specs=[ANY_SPEC] * n, out_specs=[ANY_SPEC] * n,
        out_shape=[jax.ShapeDtypeStruct(a.shape, a.dtype) for a in bufs], input_output_aliases={w: w for w in range(n)},
        scratch_shapes=[pltpu.SemaphoreType.DMA((n, 3)), pltpu.SemaphoreType.DMA((n, 3))],
    )(*bufs)


def _swap_other_half(arrs, *, name):
    n = len(arrs)

    def body(*refs):
        ins, outs = refs[:n], refs[n:2 * n]
        send_sems, recv_sems = refs[2 * n:]
        x, y, c = _position()
        cps = []
        for w in range(n):
            half = ins[w].shape[1] // 2
            cps.append(pltpu.make_async_remote_copy(
                src_ref=ins[w].at[:, pl.ds((1 - c) * half, half), :], dst_ref=outs[w], send_sem=send_sems.at[w],
                recv_sem=recv_sems.at[w], device_id=(x, y, 1 - c), device_id_type=MESH))
            cps[-1].start()
        for cp in cps:
            cp.wait()

    hbm = pl.BlockSpec(memory_space=pl.ANY)
    return pl.pallas_call(
        body, name=name, in_specs=[hbm] * n, out_specs=[hbm] * n,
        out_shape=[jax.ShapeDtypeStruct((a.shape[0], a.shape[1] // 2, a.shape[2]), a.dtype) for a in arrs],
        scratch_shapes=[pltpu.SemaphoreType.DMA((n,)), pltpu.SemaphoreType.DMA((n,))],
    )(*arrs)


def _share_halves(bufs, *, name):
    n = len(bufs)

    def body(*refs):
        ins, outs = refs[:n], refs[n:2 * n]
        send_sems, recv_sems = refs[2 * n:]
        x, y, c = _position()
        sends = []
        for w in range(n):
            sends.append(pltpu.make_async_remote_copy(src_ref=ins[w].at[c], dst_ref=outs[w].at[c], send_sem=send_sems.at[w],
                                                      recv_sem=recv_sems.at[w], device_id=(x, y, 1 - c), device_id_type=MESH))
            sends[-1].start()
        for w in range(n):
            pltpu.make_async_remote_copy(src_ref=ins[w].at[1 - c], dst_ref=outs[w].at[1 - c], send_sem=send_sems.at[w],
                                         recv_sem=recv_sems.at[w], device_id=(x, y, 1 - c), device_id_type=MESH).wait_recv()
        for cp in sends:
            cp.wait_send()

    hbm = pl.BlockSpec(memory_space=pl.ANY)
    return pl.pallas_call(
        body, name=name, in_specs=[hbm] * n, out_specs=[hbm] * n,
        out_shape=[jax.ShapeDtypeStruct(a.shape, a.dtype) for a in bufs],
        input_output_aliases={w: w for w in range(n)},
        scratch_shapes=[pltpu.SemaphoreType.DMA((n,)), pltpu.SemaphoreType.DMA((n,))],
    )(*bufs)


def _all_gather8(v, *, name):
    m, n = v.shape

    def body(v_ref, out_ref, send_sems, recv_sems, local_sem):
        x, y, c = _position()
        me, sibling = (x, y, c), (x, y, 1 - c)
        chips = _other_chips(x, y)

        def rows(px, py, pc):
            return out_ref.at[pl.ds((4 * px + 2 * py + pc) * m, m), :]

        def copy(k, block, to, src=None):
            return pltpu.make_async_remote_copy(src_ref=rows(*block) if src is None else src, dst_ref=rows(*block),
                                                send_sem=send_sems.at[k], recv_sem=recv_sems.at[k], device_id=to, device_id_type=MESH)

        mine = pltpu.make_async_copy(v_ref, rows(*me), local_sem)
        mine.start()
        first = [copy(0, me, sibling, src=v_ref)]
        first += [copy(1 + j, me, (*chip, c), src=v_ref) for j, chip in enumerate(chips)]
        for cp in first:
            cp.start()
        passed = [copy(4 + j, (*chip, c), sibling) for j, chip in enumerate(chips)]
        for j, chip in enumerate(chips):
            copy(1 + j, (*chip, c), me).wait_recv()
            passed[j].start()
        copy(0, sibling, me).wait_recv()
        for j, chip in enumerate(chips):
            copy(4 + j, (*chip, 1 - c), me).wait_recv()
        for cp in first + passed:
            cp.wait_send()
        mine.wait()

    vmem = pl.BlockSpec(memory_space=pltpu.VMEM)
    return pl.pallas_call(
        body, name=name, in_specs=[vmem], out_specs=vmem, out_shape=jax.ShapeDtypeStruct((8 * m, n), v.dtype),
        scratch_shapes=[pltpu.SemaphoreType.DMA((7,)), pltpu.SemaphoreType.DMA((7,)), pltpu.SemaphoreType.DMA],
    )(v)


def _add_half(full, part, core, *, name, rows=256):
    P, R, C = full.shape
    half = R // 2
    rows = _tile(half, rows)
    nb = half // rows

    def body(core_ref, a_ref, b_ref, o_ref):
        o_ref[...] = (a_ref[...] + b_ref[...].astype(F32)).astype(BF16)

    spec = pl.BlockSpec((None, rows, C), lambda p, i, core_ref: (p, i, 0))
    grid_spec = pltpu.PrefetchScalarGridSpec(
        num_scalar_prefetch=1, grid=(P, nb),
        in_specs=[pl.BlockSpec((None, rows, C), lambda p, i, core_ref: (p, core_ref[0] * nb + i, 0)), spec], out_specs=spec)
    return pl.pallas_call(
        body, name=name, grid_spec=grid_spec, out_shape=jax.ShapeDtypeStruct((P, half, C), BF16),
        compiler_params=_params("parallel", "parallel"),
    )(core, full, part)


def _sum_parts(own, landed, where, *, name, rows=256):
    _, R, C = own.shape
    rows = _tile(R, rows)

    def body(where_ref, own_ref, land_ref, o_ref):
        acc = own_ref[...].astype(F32)
        for j in range(3):
            acc = acc + land_ref[j].astype(F32)
        o_ref[...] = acc

    grid_spec = pltpu.PrefetchScalarGridSpec(
        num_scalar_prefetch=1, grid=(R // rows,),
        in_specs=[pl.BlockSpec((None, rows, C), lambda i, where_ref: (where_ref[0], i, 0)),
                  pl.BlockSpec((3, rows, C), lambda i, where_ref: (0, i, 0))],
        out_specs=pl.BlockSpec((None, rows, C), lambda i, where_ref: (where_ref[1], i, 0)))
    return pl.pallas_call(
        body, name=name, grid_spec=grid_spec, out_shape=jax.ShapeDtypeStruct((2, R, C), F32),
        compiler_params=_params("parallel"),
    )(where, own, landed)


def _sum_slabs(a, *, name, rows=256):
    P, R, C = a.shape
    rows = _tile(R, rows)

    def body(a_ref, o_ref):
        acc = a_ref[0].astype(F32)
        for p in range(1, P):
            acc = acc + a_ref[p].astype(F32)
        o_ref[...] = acc

    return pl.pallas_call(
        body, name=name, grid=(R // rows,), in_specs=[pl.BlockSpec((P, rows, C), lambda i: (0, i, 0))],
        out_specs=pl.BlockSpec((rows, C), lambda i: (i, 0)),
        out_shape=jax.ShapeDtypeStruct((R, C), F32), compiler_params=_params("parallel"),
    )(a)


def _reduce_scatter_start(grads, grads16, *, name):
    core = lax.axis_index("c").astype(jnp.int32).reshape(1)
    got = _swap_other_half(grads16, name=name + "_swap")
    chip_sums = [_add_half(g, r, core, name=f"{name}_add{w}") for w, (g, r) in enumerate(zip(grads, got))]
    lands = [jax.ShapeDtypeStruct((3,) + s.shape[1:], s.dtype) for s in chip_sums]
    return _copies_start(chip_sums, lands, chip_sums[0], _exchange_src, _exchange_dst, name=name + "_start")


def _reduce_scatter_finish(started, after, *, name):
    core = lax.axis_index("c").astype(jnp.int32)
    chip = (2 * lax.axis_index("x") + lax.axis_index("y")).astype(jnp.int32)
    where = jnp.stack([chip, core])
    reduced = []
    for g, st in enumerate(started):
        sums, landed = _copies_wait(st, after, _exchange_src, _exchange_dst, name=f"{name}{g}_wait")
        reduced += [_sum_parts(s, a, where, name=f"{name}{g}_sum{w}") for w, (s, a) in enumerate(zip(sums, landed))]
    both = _share_halves(reduced, name=name + "_share")
    return [b.reshape(2 * b.shape[1], b.shape[2]) for b in both]


def _to_heads(t, heads, dil=1):
    S = t.shape[0]
    d = t.shape[1] // heads
    return jnp.transpose(t.reshape(S // dil, dil, heads, d), (2, 1, 0, 3)).reshape(heads, S, d)


def _from_heads(t, dil=1):
    heads, S, d = t.shape
    return jnp.transpose(t.reshape(heads, dil, S // dil, d), (2, 1, 0, 3)).reshape(S, heads * d)


def _unstride(t, dil):
    heads, S, d = t.shape
    return jnp.transpose(t.reshape(heads, dil, S // dil, d), (0, 2, 1, 3)).reshape(heads, S, d)


def _restride(t, dil):
    heads, S, d = t.shape
    return jnp.transpose(t.reshape(heads, S // dil, dil, d), (0, 2, 1, 3)).reshape(heads, S, d)


def _pad_cols(t, width, padded):
    S = t.shape[0]
    return jnp.pad(t.reshape(S, N_CHIPS, width), ((0, 0), (0, 0), (0, padded - width))).reshape(S, N_CHIPS * padded)


def _unpad_cols(t, width, padded):
    S = t.shape[0]
    return t.reshape(S, N_CHIPS, padded)[:, :, :width].reshape(S, N_CHIPS * width)


def _alibi(n):
    return 2.0 ** (-8.0 * jnp.arange(1, n + 1, dtype=F32) / n)


B_KW = [dict(hps=4, nb_seq=SEQ // d // ATT_BLK, n_back=w // d, scale=1.0 / math.sqrt(HEAD_DIM), dist_scale=d)
        for w, d in B_PATTERNS]
A_KW = dict(hps=A_Q_HEADS // A_KV_HEADS, nb_seq=SEQ // ATT_BLK, n_back=A_WINDOW - 1, scale=1.0 / math.sqrt(HEAD_DIM), dist_scale=1)
D_BLK = 256
D_KW = dict(blk=D_BLK, nb_seq=SEQ // D_BLK, n_prev=SEQ // D_BLK, n_back=SEQ, scale=1.0 / math.sqrt(D_NOPE + D_ROPE), dist_scale=1)
C_KW = dict(blk=256, scale=1.0 / math.sqrt(HEAD_DIM))


def _front_block(t):
    return jnp.pad(t, ((0, 0), (ATT_BLK, 0), (0, 0)))


def _rope_tables(reps):
    inv_freq = ROPE_BASE ** (-jnp.arange(0, D_ROPE, 2, dtype=F32) / D_ROPE)
    ang = jnp.arange(SEQ, dtype=F32)[:, None] * inv_freq[None, :]
    return jnp.tile(jnp.cos(ang), (1, reps)), jnp.tile(jnp.sin(ang), (1, reps))


def _mlp_fwd(x, x16, w1, w2, g, b, tag):
    hid, act = _matmul(x16, w1, mode="nn", out_dtype=BF16, epilogue="relu2", name=f"mlp{tag}_up")
    m = _matmul(act, w2, mode="nn", out_dtype=F32, name=f"mlp{tag}_down")
    y, y16, u = _norm_fwd(x, m, g, b, center=True, eps=LN_EPS, alpha=ALPHA, name=f"ln2_{tag}")
    return y, y16, (x16, hid, act, u)


def _mlp_bwd(dy, saved, w1, w2, g, tag, after=None):
    x16, hid, act, u = saved
    du, du16, dg, db = _norm_bwd(dy, u, g, center=True, eps=LN_EPS, name=f"ln2_{tag}_bwd", after=after)
    dw2 = _matmul_tn(act, du16, shards=1, name=f"mlp{tag}_dw2")
    dhid = _matmul(du16, w2, mode="nt", out_dtype=BF16, epilogue="drelu2", extra=hid, name=f"mlp{tag}_dact")
    dw1 = _matmul_tn(x16, dhid, shards=N_CHIPS, name=f"mlp{tag}_dw1")
    dx = _matmul(dhid, w1, mode="nt", out_dtype=F32, epilogue="add", extra=du, alpha=ALPHA, name=f"mlp{tag}_dx")
    return dx, dw1, dw2, dg, db


def _even_fwd(x16, w_in, sinks):
    h = _unpad_cols(_matmul(x16, w_in, mode="nn", out_dtype=BF16, name="even_in"), EVEN_IN // N_CHIPS, EVEN_IN_PAD)
    qa = _to_heads(h[:, :A_Q_W], A_Q_HEADS)
    ka = _front_block(_to_heads(h[:, A_Q_W:A_Q_W + A_KV_W], A_KV_HEADS))
    va = _front_block(_to_heads(h[:, A_Q_W + A_KV_W:A_Q_W + 2 * A_KV_W], A_KV_HEADS))
    slopes_a, slopes_b = _alibi(A_Q_HEADS), _alibi(B_HEADS)
    oa, lse_a = _band_fwd(qa, ka, va, slopes_a, sinks, name="swa_fwd", **A_KW)
    base = A_Q_W + 2 * A_KV_W
    qkv_b, outs, lses = [], [], []
    for gi, (_, dil) in enumerate(B_PATTERNS):
        cols = h[:, base + gi * 3 * B_W:base + (gi + 1) * 3 * B_W]
        q, k, v = (_to_heads(cols[:, i * B_W:(i + 1) * B_W], B_HEADS, dil) for i in range(3))
        k, v = _front_block(k), _front_block(v)
        o, lse = _band_fwd(q, k, v, slopes_b, None, name=f"dil{gi}_fwd", **B_KW[gi])
        qkv_b.append((q, k, v, o, lse))
        outs.append(_unstride(o, dil))
        lses.append(_unstride(lse, dil))
    ob = _mix_fwd(outs, lses, name="dil_mix")
    y16 = jnp.concatenate([_from_heads(oa), _from_heads(ob)], axis=-1).astype(BF16)
    return y16, (x16, qa, ka, va, oa, lse_a, qkv_b, outs, lses, y16)


def _even_bwd(dmix16, du, saved, w_in, sinks, w_out):
    x16, qa, ka, va, oa, lse_a, qkv_b, outs, lses, y16 = saved
    slopes_a, slopes_b = _alibi(A_Q_HEADS), _alibi(B_HEADS)
    dw_out = _matmul_tn(y16, dmix16, shards=N_CHIPS, name="even_dwout")
    dy = _matmul(dmix16, w_out, mode="nt", out_dtype=F32, name="even_dy")
    doa = _to_heads(dy[:, :A_Q_W], A_Q_HEADS)
    dob = _to_heads(dy[:, A_Q_W:], B_HEADS)
    dqa, dka, dva, dsink = _band_bwd(qa, ka, va, oa, doa, lse_a, None, slopes_a, sinks, name="swa_bwd", **A_KW)
    douts, dlses = _mix_bwd(outs, lses, dob, name="dil_mix_bwd")
    parts = [_from_heads(dqa), _from_heads(dka[:, ATT_BLK:]), _from_heads(dva[:, ATT_BLK:])]
    for gi, (_, dil) in enumerate(B_PATTERNS):
        q, k, v, o, lse = qkv_b[gi]
        dq, dk, dv = _band_bwd(q, k, v, o, _restride(douts[gi], dil), lse, _restride(dlses[gi], dil), slopes_b, None,
                               name=f"dil{gi}_bwd", **B_KW[gi])
        parts += [_from_heads(dq, dil), _from_heads(dk[:, ATT_BLK:], dil), _from_heads(dv[:, ATT_BLK:], dil)]
    dh16 = _pad_cols(jnp.concatenate(parts, axis=-1), EVEN_IN // N_CHIPS, EVEN_IN_PAD).astype(BF16)
    dw_in = _matmul_tn(x16, dh16, shards=N_CHIPS, name="even_dwin")
    dx = _matmul(dh16, w_in, mode="nt", out_dtype=F32, epilogue="add", extra=du, alpha=ALPHA, name="even_dx")
    return dx, dw_in, dsink[:, 0, 0], dw_out


def _odd_fwd(x16, w_in, qn_g, kvn_g, w_uq, w_ukv, w_out):
    S = x16.shape[0]
    h = _unpad_cols(_matmul(x16, w_in, mode="nn", out_dtype=F32, name="odd_in"), ODD_IN // N_CHIPS, ODD_IN_PAD)
    qc, kc, vc = (_to_heads(h[:, i * C_W:(i + 1) * C_W].astype(BF16), C_HEADS) for i in range(3))
    cq = h[:, 3 * C_W:3 * C_W + D_Q_RANK]
    ckv = h[:, 3 * C_W + D_Q_RANK:3 * C_W + D_Q_RANK + D_KV_RANK]
    kr = h[:, 3 * C_W + D_Q_RANK + D_KV_RANK:]
    oc, tot = _stick_fwd(qc, kc, vc, name="stick_fwd", **C_KW)
    _, cqn16, _ = _norm_fwd(cq, None, qn_g, None, center=False, eps=RMS_EPS, alpha=1.0, name="q_rms")
    _, ckvn16, _ = _norm_fwd(ckv, None, kvn_g, None, center=False, eps=RMS_EPS, alpha=1.0, name="kv_rms")
    q = _matmul(cqn16, w_uq, mode="nn", out_dtype=F32, name="mla_uq").reshape(S, D_HEADS, D_NOPE + D_ROPE)
    kv = _matmul(ckvn16, w_ukv, mode="nn", out_dtype=F32, name="mla_ukv").reshape(S, D_HEADS, D_NOPE + D_V)
    half = D_ROPE // 2
    q_rope = q[..., D_NOPE:]
    x1 = jnp.concatenate([q_rope[..., :half].reshape(S, D_HEADS * half), kr[:, :half]], axis=-1)
    x2 = jnp.concatenate([q_rope[..., half:].reshape(S, D_HEADS * half), kr[:, half:]], axis=-1)
    cos, sin = _rope_tables(D_HEADS + 1)
    y1, y2 = _rope(x1[None], x2[None], cos, sin, name="rope_fwd")
    nq = D_HEADS * half
    q_rot = jnp.concatenate([y1[:, :nq].reshape(S, D_HEADS, half), y2[:, :nq].reshape(S, D_HEADS, half)], axis=-1)
    kr_rot = jnp.concatenate([y1[:, nq:], y2[:, nq:]], axis=-1)
    qd = jnp.transpose(jnp.concatenate([q[..., :D_NOPE], q_rot], axis=-1), (1, 0, 2)).astype(BF16)
    kd = jnp.transpose(jnp.concatenate([kv[..., :D_NOPE], jnp.broadcast_to(kr_rot[:, None, :], (S, D_HEADS, D_ROPE))], axis=-1),
                       (1, 0, 2)).astype(BF16)
    vd = jnp.transpose(kv[..., D_NOPE:], (1, 0, 2)).astype(BF16)
    od, lse_d = _attn_fwd(qd, kd, vd, None, None, name="mla_fwd", **D_KW)
    y16 = jnp.concatenate([_from_heads(oc), _from_heads(od)], axis=-1).astype(BF16)
    mixed = _matmul(y16, w_out, mode="nn", out_dtype=F32, name="odd_out")
    return mixed, (x16, qc, kc, vc, tot, cq, ckv, cqn16, ckvn16, qd, kd, vd, od, lse_d, y16)


def _odd_bwd(dmix16, du, saved, w_in, qn_g, kvn_g, w_uq, w_ukv, w_out):
    x16, qc, kc, vc, tot, cq, ckv, cqn16, ckvn16, qd, kd, vd, od, lse_d, y16 = saved
    S = x16.shape[0]
    half = D_ROPE // 2
    dw_out = _matmul_tn(y16, dmix16, shards=1, name="odd_dwout")
    dy = _matmul(dmix16, w_out, mode="nt", out_dtype=F32, name="odd_dy")
    doc = _to_heads(dy[:, :C_W], C_HEADS)
    dod = _to_heads(dy[:, C_W:], D_HEADS)
    dqc, dkc, dvc = _stick_bwd(qc, kc, vc, tot, doc, name="stick_bwd", **C_KW)
    dqd, dkd, dvd = _attn_bwd(qd, kd, vd, od, dod, lse_d, None, None, None, name="mla_bwd", **D_KW)
    cos, sin = _rope_tables(D_HEADS + 1)
    nq = D_HEADS * half
    gq1 = jnp.transpose(dqd[..., D_NOPE:D_NOPE + half], (1, 0, 2)).reshape(1, S, nq)
    gq2 = jnp.transpose(dqd[..., D_NOPE + half:], (1, 0, 2)).reshape(1, S, nq)
    dq1, dq2 = _rope(gq1, gq2, cos[:, :nq], -sin[:, :nq], name="rope_bwd_q")
    dk1, dk2 = _rope(dkd[..., D_NOPE:D_NOPE + half], dkd[..., D_NOPE + half:], cos[:, nq:], -sin[:, nq:], name="rope_bwd_k")
    dq_full = jnp.concatenate([jnp.transpose(dqd[..., :D_NOPE], (1, 0, 2)), dq1.reshape(S, D_HEADS, half),
                               dq2.reshape(S, D_HEADS, half)], axis=-1).reshape(S, D_HEADS * (D_NOPE + D_ROPE)).astype(BF16)
    dkv_full = jnp.concatenate([jnp.transpose(dkd[..., :D_NOPE], (1, 0, 2)), jnp.transpose(dvd, (1, 0, 2))],
                               axis=-1).reshape(S, D_HEADS * (D_NOPE + D_V)).astype(BF16)
    dw_uq = _matmul_tn(cqn16, dq_full, shards=N_CHIPS, name="mla_dwuq")
    dw_ukv = _matmul_tn(ckvn16, dkv_full, shards=N_CHIPS, name="mla_dwukv")
    dcqn = _matmul(dq_full, w_uq, mode="nt", out_dtype=F32, name="mla_dcq")
    dckvn = _matmul(dkv_full, w_ukv, mode="nt", out_dtype=F32, name="mla_dckv")
    dcq, _, dqn_g, _ = _norm_bwd(dcqn, cq, qn_g, center=False, eps=RMS_EPS, name="q_rms_bwd")
    dckv, _, dkvn_g, _ = _norm_bwd(dckvn, ckv, kvn_g, center=False, eps=RMS_EPS, name="kv_rms_bwd")
    dh = jnp.concatenate([_from_heads(dqc), _from_heads(dkc), _from_heads(dvc), dcq, dckv, dk1, dk2], axis=-1)
    dh16 = _pad_cols(dh, ODD_IN // N_CHIPS, ODD_IN_PAD).astype(BF16)
    dw_in = _matmul_tn(x16, dh16, shards=N_CHIPS, name="odd_dwin")
    dx = _matmul(dh16, w_in, mode="nt", out_dtype=F32, epilogue="add", extra=du, alpha=ALPHA, name="odd_dx")
    return dx, dw_in, dqn_g[0], dkvn_g[0], dw_uq, dw_ukv, dw_out


WEIGHT_GROUPS = (("even_w_in",), ("even_w_out", "mlp_w1_0", "mlp_w2_0"), ("odd_w_in", "odd_w_uq", "odd_w_ukv", "odd_w_out"),
                 ("mlp_w1_1", "mlp_w2_1"))
GRAD_GROUPS = (("mlp_w2_1", "mlp_w1_1"), ("odd_w_out", "odd_w_uq", "odd_w_ukv", "odd_w_in"), ("mlp_w2_0", "mlp_w1_0"),
               ("even_w_out", "even_w_in"))


def _local_step(x, target, small, weights_for, send_grads):
    def by_rows(pair, rows):
        return tuple(t.reshape(N_CHIPS, rows // N_CHIPS, D_MODEL) for t in pair)

    x16 = x.astype(BF16)
    w = dict(weights_for(0, x16))
    y16, sv_e = _even_fwd(x16, w["even_w_in"], small["even_sinks"])
    w.update(weights_for(1, y16))
    mixed0 = _matmul(y16, w["even_w_out"], mode="nn", out_dtype=F32, name="even_out")
    x1, x1_16, u01 = _norm_fwd(x, mixed0, small["ln1_g"][0], small["ln1_b"][0], center=True, eps=LN_EPS, alpha=ALPHA, name="ln1_0")
    w1_0, w2_0 = w["mlp_w1_0"], w["mlp_w2_0"].reshape(1, D_FF, D_MODEL)
    x2, x2_16, sv_m0 = _mlp_fwd(x1, x1_16, w1_0, w2_0, small["ln2_g"][0], small["ln2_b"][0], 0)
    w.update(weights_for(2, x2_16))
    odd_w = (w["odd_w_in"], small["odd_q_norm_g"], small["odd_kv_norm_g"], w["odd_w_uq"], w["odd_w_ukv"],
             w["odd_w_out"].reshape(1, D_MODEL, D_MODEL))
    mixed1, sv_o = _odd_fwd(x2_16, *odd_w)
    x3, x3_16, u11 = _norm_fwd(x2, mixed1, small["ln1_g"][1], small["ln1_b"][1], center=True, eps=LN_EPS, alpha=ALPHA, name="ln1_1")
    w.update(weights_for(3, x3_16))
    w1_1, w2_1 = w["mlp_w1_1"], w["mlp_w2_1"].reshape(1, D_FF, D_MODEL)
    x4, _, sv_m1 = _mlp_fwd(x3, x3_16, w1_1, w2_1, small["ln2_g"][1], small["ln2_b"][1], 1)
    dy, loss = _loss_head(x4, target, name="loss_head")

    dx3, dw1_1, dw2_1, dln2g_1, dln2b_1 = _mlp_bwd(dy, sv_m1, w1_1, w2_1, small["ln2_g"][1], 1)
    sent = send_grads(0, dict(mlp_w2_1=by_rows(dw2_1, D_FF), mlp_w1_1=dw1_1))
    du, du16, dln1g_1, dln1b_1 = _norm_bwd(dx3, u11, small["ln1_g"][1], center=True, eps=LN_EPS, name="ln1_1_bwd", after=sent)
    dx2, dwin_o, dqn_g, dkvn_g, dwuq, dwukv, dwout_o = _odd_bwd(du16, du, sv_o, *odd_w)
    sent = send_grads(1, dict(odd_w_out=by_rows(dwout_o, D_MODEL), odd_w_uq=dwuq, odd_w_ukv=dwukv, odd_w_in=dwin_o))
    dx1, dw1_0, dw2_0, dln2g_0, dln2b_0 = _mlp_bwd(dx2, sv_m0, w1_0, w2_0, small["ln2_g"][0], 0, after=sent)
    sent = send_grads(2, dict(mlp_w2_0=by_rows(dw2_0, D_FF), mlp_w1_0=dw1_0))
    du, du16, dln1g_0, dln1b_0 = _norm_bwd(dx1, u01, small["ln1_g"][0], center=True, eps=LN_EPS, name="ln1_0_bwd", after=sent)
    dx0, dwin_e, dsinks, dwout_e = _even_bwd(du16, du, sv_e, w["even_w_in"], small["even_sinks"], w["even_w_out"])
    send_grads(3, dict(even_w_out=dwout_e, even_w_in=dwin_e))

    sm = dict(even_sinks=dsinks, odd_q_norm_g=dqn_g, odd_kv_norm_g=dkvn_g,
              ln1_g=jnp.concatenate([dln1g_0, dln1g_1]), ln1_b=jnp.concatenate([dln1b_0, dln1b_1]),
              ln2_g=jnp.concatenate([dln2g_0, dln2g_1]), ln2_b=jnp.concatenate([dln2b_0, dln2b_1]))
    return loss, dx0, sm


SMALL_ORDER = ("ln1_g", "ln1_b", "ln2_g", "ln2_b", "even_sinks", "odd_q_norm_g", "odd_kv_norm_g")
SMALL_COLS = 2176


def _pack_small(parts):
    flat = jnp.concatenate([p.reshape(-1) for p in parts])
    return jnp.pad(flat, (0, 8 * SMALL_COLS - flat.shape[0])).reshape(8, SMALL_COLS)


def _unpack_small(packed, shapes):
    flat = packed.reshape(-1)
    out, pos = [], 0
    for s in shapes:
        n = math.prod(s)
        out.append(flat[pos:pos + n].reshape(s))
        pos += n
    return out


def kernel(x, even_w_in, even_sinks, even_w_out, odd_w_in, odd_q_norm_g, odd_kv_norm_g, odd_w_uq, odd_w_ukv, odd_w_out, ln1_g, ln1_b, mlp_w1, mlp_w2, ln2_g, ln2_b, loss_target, m_even_w_in, m_even_sinks, m_even_w_out, m_odd_w_in, m_odd_q_norm_g, m_odd_kv_norm_g, m_odd_w_uq, m_odd_w_ukv, m_odd_w_out, m_ln1_g, m_ln1_b, m_mlp_w1, m_mlp_w2, m_ln2_g, m_ln2_b, v_even_w_in, v_even_sinks, v_even_w_out, v_odd_w_in, v_odd_q_norm_g, v_odd_kv_norm_g, v_odd_w_uq, v_odd_w_ukv, v_odd_w_out, v_ln1_g, v_ln1_b, v_mlp_w1, v_mlp_w2, v_ln2_g, v_ln2_b):
    chip = 2 * lax.axis_index("x") + lax.axis_index("y")
    e_w, o_w = EVEN_IN // N_CHIPS, ODD_IN // N_CHIPS

    shards = dict(
        even_w_in=jnp.pad(even_w_in[0], ((0, 0), (0, EVEN_IN_PAD - e_w))), even_w_out=even_w_out[0],
        odd_w_in=jnp.pad(odd_w_in[0], ((0, 0), (0, ODD_IN_PAD - o_w))), odd_w_uq=odd_w_uq[0], odd_w_ukv=odd_w_ukv[0],
        odd_w_out=odd_w_out[0], mlp_w1_0=mlp_w1[0], mlp_w1_1=mlp_w1[1], mlp_w2_0=mlp_w2[0], mlp_w2_1=mlp_w2[1])
    names = list(shards)
    own16 = {n: shards[n].astype(BF16) for n in names}
    gather_started, token = [], own16[WEIGHT_GROUPS[0][0]]
    for g, group in enumerate(WEIGHT_GROUPS):
        lands = [jax.ShapeDtypeStruct((N_CHIPS,) + own16[n].shape, BF16) for n in group]
        gather_started.append(_copies_start([own16[n] for n in group], lands, token, _gather_src, _gather_dst,
                                            name=f"gather{g}_start"))
        token = gather_started[-1][-1]
    all_started = token

    def weights_for(g, after):
        group = WEIGHT_GROUPS[g]
        _, landed = _copies_wait(gather_started[g], all_started if g == 0 else after, _gather_src, _gather_landed,
                                 name=f"gather{g}_wait")
        full = _forward_halves(landed, name=f"gather{g}_forward")
        return {n: lax.dynamic_update_slice(f, own16[n][None], (chip, 0, 0)) for n, f in zip(group, full)}

    grads_started = {}

    def send_grads(g, pairs):
        group = GRAD_GROUPS[g]
        grads_started[g] = _reduce_scatter_start([pairs[n][0] for n in group], [pairs[n][1] for n in group], name=f"grads{g}")
        return grads_started[g][-1]

    norm_rows = jnp.pad(jnp.concatenate([odd_q_norm_g[0], odd_kv_norm_g[0]])[None], ((0, 7), (0, 64)))
    norm_all = _all_gather8(norm_rows, name="gather_norm_gains")[0::16]
    qn_w, kvn_w = D_Q_RANK // N_CHIPS, D_KV_RANK // N_CHIPS
    small = dict(even_sinks=even_sinks[0], odd_q_norm_g=norm_all[:, :qn_w].reshape(D_Q_RANK),
                 odd_kv_norm_g=norm_all[:, qn_w:qn_w + kvn_w].reshape(D_KV_RANK), ln1_g=ln1_g, ln1_b=ln1_b, ln2_g=ln2_g, ln2_b=ln2_b)

    loss_row, grad_x, sm = _local_step(x[0], loss_target[0], small, weights_for, send_grads)
    loss = lax.psum(loss_row[0, 0], ("x", "y", "c"))

    order_sent = [n for group in GRAD_GROUPS for n in group]
    reduced = dict(zip(order_sent, _reduce_scatter_finish([grads_started[g] for g in range(len(GRAD_GROUPS))], grad_x,
                                                          name="grads")))
    every = _all_gather8(_pack_small([sm[n] for n in SMALL_ORDER]), name="gather_small_grads")
    sm_sum = _sum_slabs(every.reshape(8, 8, SMALL_COLS), name="sum_small_grads")
    g_ln1g, g_ln1b, g_ln2g, g_ln2b, g_sinks, g_qn, g_kvn = _unpack_small(
        sm_sum, [(DEPTH, D_MODEL)] * 4 + [(1, A_Q_HEADS), (D_Q_RANK,), (D_KV_RANK,)])
    grads = dict(
        even_w_in=reduced["even_w_in"][:, :e_w][None], even_sinks=g_sinks, even_w_out=reduced["even_w_out"][None],
        odd_w_in=reduced["odd_w_in"][:, :o_w][None], odd_q_norm_g=lax.dynamic_slice_in_dim(g_qn, chip * qn_w, qn_w)[None],
        odd_kv_norm_g=lax.dynamic_slice_in_dim(g_kvn, chip * kvn_w, kvn_w)[None], odd_w_uq=reduced["odd_w_uq"][None],
        odd_w_ukv=reduced["odd_w_ukv"][None], odd_w_out=reduced["odd_w_out"][None], ln1_g=g_ln1g, ln1_b=g_ln1b,
        mlp_w1=jnp.stack([reduced["mlp_w1_0"], reduced["mlp_w1_1"]]), mlp_w2=jnp.stack([reduced["mlp_w2_0"], reduced["mlp_w2_1"]]),
        ln2_g=g_ln2g, ln2_b=g_ln2b)

    weights = dict(even_w_in=even_w_in, even_sinks=even_sinks, even_w_out=even_w_out, odd_w_in=odd_w_in, odd_q_norm_g=odd_q_norm_g,
                   odd_kv_norm_g=odd_kv_norm_g, odd_w_uq=odd_w_uq, odd_w_ukv=odd_w_ukv, odd_w_out=odd_w_out, ln1_g=ln1_g, ln1_b=ln1_b,
                   mlp_w1=mlp_w1, mlp_w2=mlp_w2, ln2_g=ln2_g, ln2_b=ln2_b)
    m_in = dict(even_w_in=m_even_w_in, even_sinks=m_even_sinks, even_w_out=m_even_w_out, odd_w_in=m_odd_w_in,
                odd_q_norm_g=m_odd_q_norm_g, odd_kv_norm_g=m_odd_kv_norm_g, odd_w_uq=m_odd_w_uq, odd_w_ukv=m_odd_w_ukv,
                odd_w_out=m_odd_w_out, ln1_g=m_ln1_g, ln1_b=m_ln1_b, mlp_w1=m_mlp_w1, mlp_w2=m_mlp_w2, ln2_g=m_ln2_g, ln2_b=m_ln2_b)
    v_in = dict(even_w_in=v_even_w_in, even_sinks=v_even_sinks, even_w_out=v_even_w_out, odd_w_in=v_odd_w_in,
                odd_q_norm_g=v_odd_q_norm_g, odd_kv_norm_g=v_odd_kv_norm_g, odd_w_uq=v_odd_w_uq, odd_w_ukv=v_odd_w_ukv,
                odd_w_out=v_odd_w_out, ln1_g=v_ln1_g, ln1_b=v_ln1_b, mlp_w1=v_mlp_w1, mlp_w2=v_mlp_w2, ln2_g=v_ln2_g, ln2_b=v_ln2_b)
    order = list(weights)
    delta, new_m, new_v = {}, {}, {}
    for n in order:
        shape = weights[n].shape
        flat = (math.prod(shape[:-1]), shape[-1])
        d, mm, vv = _adamw(weights[n].reshape(flat), grads[n].reshape(flat), m_in[n].reshape(flat), v_in[n].reshape(flat),
                           name=f"adamw_{n}")
        delta[n], new_m[n], new_v[n] = d.reshape(shape), mm.reshape(shape), vv.reshape(shape)
    return (loss, grad_x[None], *[grads[n] for n in order], *[delta[n] for n in order], *[new_m[n] for n in order],
            *[new_v[n] for n in order])
```

```python
import functools
import math

import jax
import jax.numpy as jnp
from jax import lax
from jax.experimental import pallas as pl
from jax.experimental.pallas import tpu as pltpu

F32 = jnp.float32
BF16 = jnp.bfloat16
MESH = pl.DeviceIdType.MESH

D_MODEL = 2048
SEQ = 2048
DEPTH = 2
HEAD_DIM = 64
A_Q_HEADS, A_KV_HEADS, A_WINDOW = 16, 2, 128
B_HEADS = 8
B_PATTERNS = ((128, 1), (512, 4), (2048, 16))
C_HEADS = 16
D_HEADS, D_Q_RANK, D_KV_RANK, D_NOPE, D_ROPE, D_V = 16, 512, 256, 64, 32, 64
ROPE_BASE = 10000.0
D_FF = 4 * D_MODEL
LN_EPS = 1e-5
RMS_EPS = 1e-6
ALPHA = (2 * DEPTH) ** 0.25
A_Q_W = A_Q_HEADS * HEAD_DIM
A_KV_W = A_KV_HEADS * HEAD_DIM
B_W = B_HEADS * HEAD_DIM
EVEN_IN = A_Q_W + 2 * A_KV_W + 3 * B_W * len(B_PATTERNS)
EVEN_OUT = A_Q_W + B_W
C_W = C_HEADS * HEAD_DIM
ODD_IN = 3 * C_W + D_Q_RANK + D_KV_RANK + D_ROPE
ADAM_LR, ADAM_B1, ADAM_B2, ADAM_EPS, ADAM_WD, ADAM_STEP = 0.001, 0.9, 0.999, 1e-08, 0.01, 10

N_CHIPS = 4
EVEN_IN_PAD = 1536
ODD_IN_PAD = 1024
ATT_BLK = 128
NEG = -1e30
V7X_VMEM_LIMIT = 48 * 1024 * 1024


def _params(*sem):
    return pltpu.CompilerParams(dimension_semantics=sem, vmem_limit_bytes=V7X_VMEM_LIMIT)


def _tile(n, cap):
    if n <= cap:
        return n
    t = cap - cap % 128
    while n % t:
        t -= 128
    return t


def _matmul(a, b, *, mode, out_dtype, name, epilogue=None, extra=None, alpha=1.0, tm=1024, tn=512, tk=2048):
    if mode == "nn":
        M, K = a.shape
        P, _, Np = b.shape
        tm, tn, tk = _tile(M, tm), _tile(Np, tn), _tile(K, tk)
        nj = Np // tn
        grid = (M // tm, P * nj, K // tk)
        a_spec = pl.BlockSpec((tm, tk), lambda i, j, k: (i, k))
        b_spec = pl.BlockSpec((None, tk, tn), lambda i, j, k: (j // nj, k, j % nj))
        o_spec = pl.BlockSpec((tm, tn), lambda i, j, k: (i, j))
        out_shape = (M, P * Np)
        dims = (((1,), (0,)), ((), ()))
    elif mode == "nt":
        M, N = a.shape
        P, K, Np = b.shape
        tm, tn, tk = _tile(M, tm), _tile(K, tn), _tile(Np, tk)
        nkk = Np // tk
        grid = (M // tm, K // tn, P * nkk)
        a_spec = pl.BlockSpec((tm, tk), lambda i, j, k: (i, k))
        b_spec = pl.BlockSpec((None, tn, tk), lambda i, j, k: (k // nkk, j, k % nkk))
        o_spec = pl.BlockSpec((tm, tn), lambda i, j, k: (i, j))
        out_shape = (M, K)
        dims = (((1,), (1,)), ((), ()))
    else:
        raise ValueError(mode)
    n_k = grid[2]
    n_extra = 0 if extra is None else 1
    n_out = 2 if epilogue == "relu2" else 1

    def body(*refs):
        a_ref, b_ref = refs[:2]
        e_ref = refs[2] if n_extra else None
        outs = refs[2 + n_extra:2 + n_extra + n_out]
        part = lax.dot_general(a_ref[...], b_ref[...], dims, preferred_element_type=F32)

        def finish(r):
            if epilogue == "relu2":
                outs[0][...] = r.astype(outs[0].dtype)
                rp = jnp.maximum(r, 0.0)
                outs[1][...] = (rp * rp).astype(outs[1].dtype)
            elif epilogue == "drelu2":
                outs[0][...] = (r * (2.0 * jnp.maximum(e_ref[...].astype(F32), 0.0))).astype(outs[0].dtype)
            elif epilogue == "add":
                outs[0][...] = (r + alpha * e_ref[...].astype(F32)).astype(outs[0].dtype)
            else:
                outs[0][...] = r.astype(outs[0].dtype)

        if n_k == 1:
            finish(part)
        else:
            acc = refs[-1]
            k = pl.program_id(2)

            @pl.when(k == 0)
            def _():
                acc[...] = part

            @pl.when((k > 0) & (k < n_k - 1))
            def _():
                acc[...] += part

            @pl.when(k == n_k - 1)
            def _():
                finish(acc[...] + part)

    in_specs = [a_spec, b_spec] + ([o_spec] if n_extra else [])
    shapes = [jax.ShapeDtypeStruct(out_shape, out_dtype)] * n_out
    res = pl.pallas_call(
        body, name=name, grid=grid, in_specs=in_specs,
        out_specs=[o_spec] * n_out, out_shape=shapes,
        scratch_shapes=[pltpu.VMEM((tm, tn), F32)] if n_k > 1 else [],
        compiler_params=_params("parallel", "parallel", "arbitrary"),
    )(a, b, *([extra] if n_extra else []))
    return res if n_out > 1 else res[0]


def _matmul_tn(a, g, *, shards, name, tm=1024, tn=512):
    M, K = a.shape
    P = shards
    Np = g.shape[1] // P
    tm, tn = _tile(K, tm), _tile(Np, tn)
    nj = Np // tn

    def body(a_ref, g_ref, o_ref, o16_ref):
        r = lax.dot_general(a_ref[...], g_ref[...], (((0,), (0,)), ((), ())), preferred_element_type=F32)
        o_ref[...] = r
        o16_ref[...] = r.astype(BF16)

    o_spec = pl.BlockSpec((None, tm, tn), lambda i, j: (j // nj, i, j % nj))
    return pl.pallas_call(
        body, name=name, grid=(K // tm, P * nj),
        in_specs=[pl.BlockSpec((M, tm), lambda i, j: (0, i)), pl.BlockSpec((M, tn), lambda i, j: (0, j))],
        out_specs=[o_spec, o_spec],
        out_shape=[jax.ShapeDtypeStruct((P, K, Np), F32), jax.ShapeDtypeStruct((P, K, Np), BF16)],
        compiler_params=_params("parallel", "parallel"),
    )(a, g)


def _norm_fwd(x, res, g, b, *, center, eps, alpha, name, rows=256):
    S, W = x.shape
    has_res = res is not None
    rows = _tile(S, rows)

    def body(*refs):
        x_ref = refs[0]
        r_ref = refs[1] if has_res else None
        g_ref = refs[1 + has_res]
        b_ref = refs[2 + has_res] if center else None
        y_ref, y16_ref, u_ref = refs[-3:]
        u = x_ref[...]
        if has_res:
            u = alpha * u + r_ref[...]
        if center:
            mu = jnp.mean(u, axis=1, keepdims=True)
            xc = u - mu
        else:
            xc = u
        var = jnp.mean(xc * xc, axis=1, keepdims=True)
        y = xc * lax.rsqrt(var + eps) * g_ref[...]
        if center:
            y = y + b_ref[...]
        y_ref[...] = y
        y16_ref[...] = y.astype(BF16)
        u_ref[...] = u

    row_spec = pl.BlockSpec((rows, W), lambda i: (i, 0))
    vec_spec = pl.BlockSpec((1, W), lambda i: (0, 0))
    ins = [x] + ([res] if has_res else []) + [g.reshape(1, W)] + ([b.reshape(1, W)] if center else [])
    specs = [row_spec] + ([row_spec] if has_res else []) + [vec_spec] + ([vec_spec] if center else [])
    return pl.pallas_call(
        body, name=name, grid=(S // rows,), in_specs=specs,
        out_specs=[row_spec, row_spec, row_spec],
        out_shape=[jax.ShapeDtypeStruct((S, W), F32), jax.ShapeDtypeStruct((S, W), BF16), jax.ShapeDtypeStruct((S, W), F32)],
        compiler_params=_params("parallel"),
    )(*ins)


def _norm_bwd(dy, u, g, *, center, eps, name, rows=256, after=None):
    S, W = u.shape
    rows = _tile(S, rows)

    def body(dy_ref, u_ref, g_ref, *rest):
        du_ref, du16_ref, dg_ref, db_ref = rest[-4:]
        i = pl.program_id(0)
        uu = u_ref[...]
        dyv = dy_ref[...]
        if center:
            xc = uu - jnp.mean(uu, axis=1, keepdims=True)
        else:
            xc = uu
        rstd = lax.rsqrt(jnp.mean(xc * xc, axis=1, keepdims=True) + eps)
        xhat = xc * rstd
        dxh = dyv * g_ref[...]
        c2 = jnp.mean(dxh * xhat, axis=1, keepdims=True)
        du = dxh - xhat * c2
        if center:
            du = du - jnp.mean(dxh, axis=1, keepdims=True)
        du = du * rstd
        du_ref[...] = du
        du16_ref[...] = du.astype(BF16)

        @pl.when(i == 0)
        def _():
            dg_ref[...] = jnp.zeros_like(dg_ref)
            db_ref[...] = jnp.zeros_like(db_ref)

        dg_ref[...] += jnp.sum(dyv * xhat, axis=0, keepdims=True)
        db_ref[...] += jnp.sum(dyv, axis=0, keepdims=True)

    row_spec = pl.BlockSpec((rows, W), lambda i: (i, 0))
    vec_spec = pl.BlockSpec((1, W), lambda i: (0, 0))
    return pl.pallas_call(
        body, name=name, grid=(S // rows,),
        in_specs=[row_spec, row_spec, vec_spec] + ([] if after is None else [pl.BlockSpec(memory_space=pl.ANY)]),
        out_specs=[row_spec, row_spec, vec_spec, vec_spec],
        out_shape=[jax.ShapeDtypeStruct((S, W), F32), jax.ShapeDtypeStruct((S, W), BF16),
                   jax.ShapeDtypeStruct((1, W), F32), jax.ShapeDtypeStruct((1, W), F32)],
        compiler_params=_params("arbitrary"),
    )(dy, u, g.reshape(1, W), *([] if after is None else [after]))


def _loss_head(y, target, *, name, rows=256):
    S, W = y.shape
    rows = _tile(S, rows)

    def body(y_ref, t_ref, dy_ref, l_ref):
        i = pl.program_id(0)
        e = y_ref[...] - t_ref[...]
        dy_ref[...] = e * (1.0 / W)

        @pl.when(i == 0)
        def _():
            l_ref[...] = jnp.zeros_like(l_ref)

        l_ref[...] += jnp.full(l_ref.shape, 0.5 / W * jnp.sum(e * e), F32)

    row_spec = pl.BlockSpec((rows, W), lambda i: (i, 0))
    return pl.pallas_call(
        body, name=name, grid=(S // rows,), in_specs=[row_spec, row_spec],
        out_specs=[row_spec, pl.BlockSpec((1, 128), lambda i: (0, 0))],
        out_shape=[jax.ShapeDtypeStruct((S, W), F32), jax.ShapeDtypeStruct((1, 128), F32)],
        compiler_params=_params("arbitrary"),
    )(y, target)


def _adamw(w, g, m, v, *, name, layer=0, carry=None, rows=256):
    L, R, C = w.shape
    rows = _tile(R, rows) if R % 8 == 0 else R
    c1 = 1.0 / (1.0 - ADAM_B1 ** ADAM_STEP)
    c2 = 1.0 / (1.0 - ADAM_B2 ** ADAM_STEP)

    def body(w_ref, g_ref, m_ref, v_ref, *rest):
        go_ref, d_ref, mo_ref, vo_ref = rest[-4:]
        gg = g_ref[...]
        mn = ADAM_B1 * m_ref[...] + (1.0 - ADAM_B1) * gg
        vn = ADAM_B2 * v_ref[...] + (1.0 - ADAM_B2) * (gg * gg)
        go_ref[...] = gg
        d_ref[...] = -ADAM_LR * ((mn * c1) / (jnp.sqrt(vn * c2) + ADAM_EPS) + ADAM_WD * w_ref[...])
        mo_ref[...] = mn
        vo_ref[...] = vn

    slab = pl.BlockSpec((None, rows, C), lambda i: (layer, i, 0))
    shp = jax.ShapeDtypeStruct((L, R, C), F32)
    carried = [] if carry is None else list(carry)
    return pl.pallas_call(
        body, name=name, grid=(R // rows,),
        in_specs=[slab, pl.BlockSpec((rows, C), lambda i: (i, 0)), slab, slab] + [pl.BlockSpec(memory_space=pl.ANY)] * len(carried),
        out_specs=[slab] * 4, out_shape=[shp] * 4, input_output_aliases={4 + k: k for k in range(len(carried))},
        compiler_params=_params("parallel"),
    )(w, g, m, v, *carried)


def _rope(x1, x2, cos, sin, *, name, rows=512):
    H, S, W = x1.shape
    rows = _tile(S, rows)

    def body(x1_ref, x2_ref, c_ref, s_ref, y1_ref, y2_ref):
        t1 = jnp.sum(x1_ref[...], axis=0)
        t2 = jnp.sum(x2_ref[...], axis=0)
        c = c_ref[...]
        s = s_ref[...]
        y1_ref[...] = t1 * c - t2 * s
        y2_ref[...] = t1 * s + t2 * c

    xs = pl.BlockSpec((H, rows, W), lambda i: (0, i, 0))
    ts = pl.BlockSpec((rows, W), lambda i: (i, 0))
    shp = jax.ShapeDtypeStruct((S, W), F32)
    return pl.pallas_call(
        body, name=name, grid=(S // rows,), in_specs=[xs, xs, ts, ts], out_specs=[ts, ts], out_shape=[shp, shp],
        compiler_params=_params("parallel"),
    )(x1, x2, cos, sin)


def _band_mask(blk, n, n_back):
    row = lax.broadcasted_iota(jnp.int32, (blk, 2 * blk), 0)
    col = lax.broadcasted_iota(jnp.int32, (blk, 2 * blk), 1)
    rel = blk + row - col
    return rel, (rel >= 0) & (rel <= n_back) & ((col >= blk) | (n >= 1))


def _band_fwd(q, k, v, slopes, sinks, *, hps, nb_seq, n_back, scale, dist_scale, name):
    blk = ATT_BLK
    Hq, T, d = q.shape
    Hk = k.shape[0]
    group = Hq // Hk
    kps = max(hps // group, 1)
    use_sink = sinks is not None

    def body(*refs):
        q_ref, k_ref, v_ref, slope_ref = refs[:4]
        sink_ref = refs[4] if use_sink else None
        o_ref, lse_ref = refs[-2:]
        i = pl.program_id(1)
        start = pl.multiple_of(i * blk, blk)
        rel, valid = _band_mask(blk, i % nb_seq, n_back)
        relf = rel.astype(F32)
        for hh in range(hps):
            h = pl.program_id(0) * hps + hh
            kk = k_ref[hh // group, pl.ds(start, 2 * blk), :]
            vv = v_ref[hh // group, pl.ds(start, 2 * blk), :]
            s = lax.dot_general(q_ref[hh], kk, (((1,), (1,)), ((), ())), preferred_element_type=F32) * scale
            s = jnp.where(valid, s - (slope_ref[h] * dist_scale) * relf, NEG)
            m = jnp.max(s, axis=1, keepdims=True)
            if use_sink:
                m = jnp.maximum(m, sink_ref[h])
            p = jnp.where(valid, jnp.exp(s - m), 0.0)
            l = jnp.sum(p, axis=1, keepdims=True)
            if use_sink:
                l = l + jnp.exp(sink_ref[h] - m)
            o_ref[hh] = jnp.dot(p.astype(BF16), vv, preferred_element_type=F32) / l
            lse_ref[hh] = m + jnp.log(l)

    smem = pl.BlockSpec(memory_space=pltpu.SMEM)
    qs = pl.BlockSpec((hps, blk, d), lambda g, i: (g, i, 0))
    ks = pl.BlockSpec((kps, T + blk, d), lambda g, i: (g, 0, 0))
    ins = [q, k, v, slopes] + ([sinks] if use_sink else [])
    return pl.pallas_call(
        body, name=name, grid=(Hq // hps, T // blk), in_specs=[qs, ks, ks, smem] + ([smem] if use_sink else []),
        out_specs=[qs, pl.BlockSpec((hps, blk, 1), lambda g, i: (g, i, 0))],
        out_shape=[jax.ShapeDtypeStruct((Hq, T, d), F32), jax.ShapeDtypeStruct((Hq, T, 1), F32)],
        compiler_params=_params("parallel", "parallel"),
    )(*ins)


def _band_bwd(q, k, v, o, do, lse, dlse, slopes, sinks, *, hps, nb_seq, n_back, scale, dist_scale, name):
    blk = ATT_BLK
    Hq, T, d = q.shape
    Hk = k.shape[0]
    group = Hq // Hk
    kps = max(hps // group, 1)
    use_sink, use_dlse = sinks is not None, dlse is not None

    def body(*refs):
        q_ref, k_ref, v_ref, o_ref, do_ref, lse_ref = refs[:6]
        pos = 6
        dlse_ref = refs[pos] if use_dlse else None
        pos += use_dlse
        slope_ref = refs[pos]
        pos += 1
        sink_ref = refs[pos] if use_sink else None
        pos += use_sink
        dq_ref, dk_ref, dv_ref = refs[pos:pos + 3]
        dsink_ref = refs[pos + 3] if use_sink else None
        i = pl.program_id(1)
        start = pl.multiple_of(i * blk, blk)
        rel, valid = _band_mask(blk, i % nb_seq, n_back)
        relf = rel.astype(F32)

        @pl.when(i == 0)
        def _():
            dk_ref[...] = jnp.zeros_like(dk_ref)
            dv_ref[...] = jnp.zeros_like(dv_ref)
            if use_sink:
                dsink_ref[...] = jnp.zeros_like(dsink_ref)

        for kh in range(kps):
            dk_acc = jnp.zeros((2 * blk, d), F32)
            dv_acc = jnp.zeros((2 * blk, d), F32)
            kk = k_ref[kh, pl.ds(start, 2 * blk), :]
            vv = v_ref[kh, pl.ds(start, 2 * blk), :]
            for hh in range(kh * min(group, hps), (kh + 1) * min(group, hps)):
                h = pl.program_id(0) * hps + hh
                qb = q_ref[hh]
                dob = do_ref[hh]
                do16 = dob.astype(BF16)
                lse = lse_ref[hh]
                c = -jnp.sum(dob * o_ref[hh], axis=1, keepdims=True)
                if use_dlse:
                    c = c + dlse_ref[hh]
                s = lax.dot_general(qb, kk, (((1,), (1,)), ((), ())), preferred_element_type=F32) * scale
                s = jnp.where(valid, s - (slope_ref[h] * dist_scale) * relf, NEG)
                p = jnp.where(valid, jnp.exp(s - lse), 0.0)
                dp = lax.dot_general(do16, vv, (((1,), (1,)), ((), ())), preferred_element_type=F32)
                ds16 = (p * (dp + c) * scale).astype(BF16)
                dq_ref[hh] = jnp.dot(ds16, kk, preferred_element_type=F32)
                dk_acc = dk_acc + lax.dot_general(ds16, qb, (((0,), (0,)), ((), ())), preferred_element_type=F32)
                dv_acc = dv_acc + lax.dot_general(p.astype(BF16), do16, (((0,), (0,)), ((), ())), preferred_element_type=F32)
                if use_sink:
                    dsink_ref[hh] += jnp.full((1, 128), jnp.sum(jnp.exp(sink_ref[h] - lse) * c), F32)
            dk_ref[kh, pl.ds(start, 2 * blk), :] += dk_acc
            dv_ref[kh, pl.ds(start, 2 * blk), :] += dv_acc

    smem = pl.BlockSpec(memory_space=pltpu.SMEM)
    qs = pl.BlockSpec((hps, blk, d), lambda g, i: (g, i, 0))
    ls = pl.BlockSpec((hps, blk, 1), lambda g, i: (g, i, 0))
    ks = pl.BlockSpec((kps, T + blk, d), lambda g, i: (g, 0, 0))
    in_specs = [qs, ks, ks, qs, qs, ls] + ([ls] if use_dlse else []) + [smem] + ([smem] if use_sink else [])
    ins = [q, k, v, o, do, lse] + ([dlse] if use_dlse else []) + [slopes] + ([sinks] if use_sink else [])
    out_specs = [qs, ks, ks]
    out_shape = [jax.ShapeDtypeStruct((Hq, T, d), F32), jax.ShapeDtypeStruct((Hk, T + blk, d), F32),
                 jax.ShapeDtypeStruct((Hk, T + blk, d), F32)]
    if use_sink:
        out_specs.append(pl.BlockSpec((hps, 1, 128), lambda g, i: (g, 0, 0)))
        out_shape.append(jax.ShapeDtypeStruct((Hq, 1, 128), F32))
    return pl.pallas_call(
        body, name=name, grid=(Hq // hps, T // blk), in_specs=in_specs, out_specs=out_specs, out_shape=out_shape,
        compiler_params=_params("parallel", "arbitrary"),
    )(*ins)


def _diagonal_mask(blk):
    return lax.broadcasted_iota(jnp.int32, (blk, blk), 0) >= lax.broadcasted_iota(jnp.int32, (blk, blk), 1)


def _causal_fwd(q, k, v, *, blk, hps, scale, name):
    H, T, dqk = q.shape
    dv = v.shape[2]

    def body(q_ref, k_ref, v_ref, o_ref, lse_ref):
        n = pl.program_id(1)
        qs = [q_ref[hh] for hh in range(hps)]

        def block(j, carry, valid):
            start = pl.multiple_of(j * blk, blk)
            out = []
            for hh in range(hps):
                m, l, acc = carry[hh]
                kb = k_ref[hh, pl.ds(start, blk), :]
                vb = v_ref[hh, pl.ds(start, blk), :]
                s = lax.dot_general(qs[hh], kb, (((1,), (1,)), ((), ())), preferred_element_type=F32) * scale
                if valid is not None:
                    s = jnp.where(valid, s, NEG)
                m_new = jnp.maximum(m, jnp.max(s, axis=1, keepdims=True))
                p = _keep(valid, jnp.exp(s - m_new))
                a = jnp.exp(m - m_new)
                out.append((m_new, a * l + jnp.sum(p, axis=1, keepdims=True),
                            a * acc + jnp.dot(p.astype(BF16), vb, preferred_element_type=F32)))
            return tuple(out)

        init = tuple((jnp.full((blk, 1), NEG, F32), jnp.zeros((blk, 1), F32), jnp.zeros((blk, dv), F32)) for _ in range(hps))
        res = block(n, lax.fori_loop(0, n, lambda j, carry: block(j, carry, None), init), _diagonal_mask(blk))
        for hh in range(hps):
            m, l, acc = res[hh]
            o_ref[hh] = acc / l
            lse_ref[hh] = m + jnp.log(l)

    qs_ = pl.BlockSpec((hps, blk, dqk), lambda g, i: (g, i, 0))
    return pl.pallas_call(
        body, name=name, grid=(H // hps, T // blk),
        in_specs=[qs_, pl.BlockSpec((hps, T, dqk), lambda g, i: (g, 0, 0)), pl.BlockSpec((hps, T, dv), lambda g, i: (g, 0, 0))],
        out_specs=[pl.BlockSpec((hps, blk, dv), lambda g, i: (g, i, 0)), pl.BlockSpec((hps, blk, 1), lambda g, i: (g, i, 0))],
        out_shape=[jax.ShapeDtypeStruct((H, T, dv), F32), jax.ShapeDtypeStruct((H, T, 1), F32)],
        compiler_params=_params("parallel", "parallel"),
    )(q, k, v)


def _causal_bwd(q, k, v, o, do, lse, *, blk, hps, scale, name):
    H, T, dqk = q.shape
    dv = v.shape[2]

    def body(q_ref, k_ref, v_ref, o_ref, do_ref, lse_ref, dq_ref, dk_ref, dv_ref):
        n = pl.program_id(1)

        @pl.when(n == 0)
        def _():
            dk_ref[...] = jnp.zeros_like(dk_ref)
            dv_ref[...] = jnp.zeros_like(dv_ref)

        qs = [q_ref[hh] for hh in range(hps)]
        do16 = [do_ref[hh].astype(BF16) for hh in range(hps)]
        lses = [lse_ref[hh] for hh in range(hps)]
        cs = [-jnp.sum(do_ref[hh] * o_ref[hh], axis=1, keepdims=True) for hh in range(hps)]

        def block(j, dqs, valid):
            start = pl.multiple_of(j * blk, blk)
            out = []
            for hh in range(hps):
                kb = k_ref[hh, pl.ds(start, blk), :]
                vb = v_ref[hh, pl.ds(start, blk), :]
                s = lax.dot_general(qs[hh], kb, (((1,), (1,)), ((), ())), preferred_element_type=F32) * scale
                if valid is not None:
                    s = jnp.where(valid, s, NEG)
                p = _keep(valid, jnp.exp(s - lses[hh]))
                dp = lax.dot_general(do16[hh], vb, (((1,), (1,)), ((), ())), preferred_element_type=F32)
                ds16 = (p * (dp + cs[hh]) * scale).astype(BF16)
                dk_ref[hh, pl.ds(start, blk), :] += lax.dot_general(ds16, qs[hh], (((0,), (0,)), ((), ())), preferred_element_type=F32)
                dv_ref[hh, pl.ds(start, blk), :] += lax.dot_general(p.astype(BF16), do16[hh], (((0,), (0,)), ((), ())),
                                                                    preferred_element_type=F32)
                out.append(dqs[hh] + jnp.dot(ds16, kb, preferred_element_type=F32))
            return tuple(out)

        init = tuple(jnp.zeros((blk, dqk), F32) for _ in range(hps))
        res = block(n, lax.fori_loop(0, n, lambda j, dqs: block(j, dqs, None), init), _diagonal_mask(blk))
        for hh in range(hps):
            dq_ref[hh] = res[hh]

    qs_ = pl.BlockSpec((hps, blk, dqk), lambda g, i: (g, i, 0))
    os_ = pl.BlockSpec((hps, blk, dv), lambda g, i: (g, i, 0))
    ls_ = pl.BlockSpec((hps, blk, 1), lambda g, i: (g, i, 0))
    ks_ = pl.BlockSpec((hps, T, dqk), lambda g, i: (g, 0, 0))
    vs_ = pl.BlockSpec((hps, T, dv), lambda g, i: (g, 0, 0))
    return pl.pallas_call(
        body, name=name, grid=(H // hps, T // blk), in_specs=[qs_, ks_, vs_, os_, os_, ls_], out_specs=[qs_, ks_, vs_],
        out_shape=[jax.ShapeDtypeStruct((H, T, dqk), F32), jax.ShapeDtypeStruct((H, T, dqk), F32),
                   jax.ShapeDtypeStruct((H, T, dv), F32)],
        compiler_params=_params("parallel", "arbitrary"),
    )(q, k, v, o, do, lse)


def _band_geometry(blk, nb_seq, n_prev):
    def geo(i):
        seq = i // nb_seq
        n = i % nb_seq
        return seq, n, jnp.maximum(n - n_prev, 0)
    return geo


def _attn_fwd(q, k, v, slopes, sinks, *, blk, nb_seq, n_prev, n_back, scale, dist_scale, name):
    Hq, T, dqk = q.shape
    Hk, _, dv = v.shape
    group = Hq // Hk
    use_bias, use_sink = slopes is not None, sinks is not None
    geo = _band_geometry(blk, nb_seq, n_prev)

    def body(*refs):
        q_ref, k_ref, v_ref = refs[:3]
        slope_ref = refs[3] if use_bias else None
        sink_ref = refs[3 + use_bias] if use_sink else None
        o_ref, lse_ref = refs[-2:]
        h = pl.program_id(0)
        seq, n, lo = geo(pl.program_id(1))
        qb = q_ref[...]
        diff = lax.broadcasted_iota(jnp.int32, (blk, blk), 0) - lax.broadcasted_iota(jnp.int32, (blk, blk), 1)
        if use_sink:
            m0 = jnp.full((blk, 1), sink_ref[h], F32)
            l0 = jnp.ones((blk, 1), F32)
        else:
            m0 = jnp.full((blk, 1), NEG, F32)
            l0 = jnp.zeros((blk, 1), F32)

        def step(j, carry):
            m, l, acc = carry
            start = pl.multiple_of((seq * nb_seq + j) * blk, blk)
            kb = k_ref[pl.ds(start, blk), :]
            vb = v_ref[pl.ds(start, blk), :]
            s = lax.dot_general(qb, kb, (((1,), (1,)), ((), ())), preferred_element_type=F32) * scale
            rel = (n - j) * blk + diff
            valid = (rel >= 0) & (rel <= n_back)
            if use_bias:
                s = s - (slope_ref[h] * dist_scale) * rel.astype(F32)
            s = jnp.where(valid, s, NEG)
            m_new = jnp.maximum(m, jnp.max(s, axis=1, keepdims=True))
            p = jnp.where(valid, jnp.exp(s - m_new), 0.0)
            a = jnp.exp(m - m_new)
            l = a * l + jnp.sum(p, axis=1, keepdims=True)
            acc = a * acc + jnp.dot(p.astype(BF16), vb, preferred_element_type=F32)
            return m_new, l, acc

        m, l, acc = lax.fori_loop(lo, n + 1, step, (m0, l0, jnp.zeros((blk, dv), F32)))
        o_ref[...] = acc / l
        lse_ref[...] = m + jnp.log(l)

    smem = pl.BlockSpec(memory_space=pltpu.SMEM)
    in_specs = [pl.BlockSpec((None, blk, dqk), lambda h, i: (h, i, 0)),
                pl.BlockSpec((None, T, dqk), lambda h, i: (h // group, 0, 0)),
                pl.BlockSpec((None, T, dv), lambda h, i: (h // group, 0, 0))]
    ins = [q, k, v]
    if use_bias:
        in_specs.append(smem)
        ins.append(slopes)
    if use_sink:
        in_specs.append(smem)
        ins.append(sinks)
    return pl.pallas_call(
        body, name=name, grid=(Hq, T // blk), in_specs=in_specs,
        out_specs=[pl.BlockSpec((None, blk, dv), lambda h, i: (h, i, 0)), pl.BlockSpec((None, blk, 1), lambda h, i: (h, i, 0))],
        out_shape=[jax.ShapeDtypeStruct((Hq, T, dv), F32), jax.ShapeDtypeStruct((Hq, T, 1), F32)],
        compiler_params=_params("parallel", "parallel"),
    )(*ins)


def _attn_bwd(q, k, v, o, do, lse, dlse, slopes, sinks, *, blk, nb_seq, n_prev, n_back, scale, dist_scale, name):
    Hq, T, dqk = q.shape
    Hk, _, dv = v.shape
    group = Hq // Hk
    use_bias, use_sink, use_dlse = slopes is not None, sinks is not None, dlse is not None
    geo = _band_geometry(blk, nb_seq, n_prev)

    def body(*refs):
        q_ref, k_ref, v_ref, o_ref, do_ref, lse_ref = refs[:6]
        pos = 6
        dlse_ref = refs[pos] if use_dlse else None
        pos += use_dlse
        slope_ref = refs[pos] if use_bias else None
        pos += use_bias
        sink_ref = refs[pos] if use_sink else None
        pos += use_sink
        dq_ref, dk_ref, dv_ref = refs[pos:pos + 3]
        dsink_ref = refs[pos + 3] if use_sink else None
        h = pl.program_id(0)
        i = pl.program_id(1)
        seq, n, lo = geo(i)
        qb = q_ref[...]
        dob = do_ref[...]
        do16 = dob.astype(BF16)
        lse = lse_ref[...]
        c = -jnp.sum(dob * o_ref[...], axis=1, keepdims=True)
        if use_dlse:
            c = c + dlse_ref[...]
        diff = lax.broadcasted_iota(jnp.int32, (blk, blk), 0) - lax.broadcasted_iota(jnp.int32, (blk, blk), 1)

        @pl.when((h % group == 0) & (i == 0))
        def _():
            dk_ref[...] = jnp.zeros_like(dk_ref)
            dv_ref[...] = jnp.zeros_like(dv_ref)

        def step(j, dq):
            start = pl.multiple_of((seq * nb_seq + j) * blk, blk)
            kb = k_ref[pl.ds(start, blk), :]
            vb = v_ref[pl.ds(start, blk), :]
            s = lax.dot_general(qb, kb, (((1,), (1,)), ((), ())), preferred_element_type=F32) * scale
            rel = (n - j) * blk + diff
            valid = (rel >= 0) & (rel <= n_back)
            if use_bias:
                s = s - (slope_ref[h] * dist_scale) * rel.astype(F32)
            p = jnp.where(valid, jnp.exp(jnp.where(valid, s, NEG) - lse), 0.0)
            dp = lax.dot_general(do16, vb, (((1,), (1,)), ((), ())), preferred_element_type=F32)
            ds16 = (p * (dp + c) * scale).astype(BF16)
            dk_ref[pl.ds(start, blk), :] += lax.dot_general(ds16, qb, (((0,), (0,)), ((), ())), preferred_element_type=F32)
            dv_ref[pl.ds(start, blk), :] += lax.dot_general(p.astype(BF16), do16, (((0,), (0,)), ((), ())), preferred_element_type=F32)
            return dq + jnp.dot(ds16, kb, preferred_element_type=F32)

        dq_ref[...] = lax.fori_loop(lo, n + 1, step, jnp.zeros((blk, dqk), F32))
        if use_sink:
            @pl.when(i == 0)
            def _():
                dsink_ref[...] = jnp.zeros_like(dsink_ref)

            dsink_ref[...] += jnp.full(dsink_ref.shape, jnp.sum(jnp.exp(sink_ref[h] - lse) * c), F32)

    smem = pl.BlockSpec(memory_space=pltpu.SMEM)
    qs = pl.BlockSpec((None, blk, dqk), lambda h, i: (h, i, 0))
    os_ = pl.BlockSpec((None, blk, dv), lambda h, i: (h, i, 0))
    ls = pl.BlockSpec((None, blk, 1), lambda h, i: (h, i, 0))
    ks = pl.BlockSpec((None, T, dqk), lambda h, i: (h // group, 0, 0))
    vs = pl.BlockSpec((None, T, dv), lambda h, i: (h // group, 0, 0))
    in_specs = [qs, ks, vs, os_, os_, ls]
    ins = [q, k, v, o, do, lse]
    if use_dlse:
        in_specs.append(ls)
        ins.append(dlse)
    if use_bias:
        in_specs.append(smem)
        ins.append(slopes)
    if use_sink:
        in_specs.append(smem)
        ins.append(sinks)
    out_specs = [qs, ks, vs]
    out_shape = [jax.ShapeDtypeStruct((Hq, T, dqk), F32), jax.ShapeDtypeStruct((Hk, T, dqk), F32),
                 jax.ShapeDtypeStruct((Hk, T, dv), F32)]
    if use_sink:
        out_specs.append(pl.BlockSpec((None, 1, 128), lambda h, i: (h, 0, 0)))
        out_shape.append(jax.ShapeDtypeStruct((Hq, 1, 128), F32))
    return pl.pallas_call(
        body, name=name, grid=(Hq, T // blk), in_specs=in_specs, out_specs=out_specs, out_shape=out_shape,
        compiler_params=_params("arbitrary", "arbitrary"),
    )(*ins)


def _tri_sum(x, upper):
    hi = x.astype(BF16)
    lo = (x - hi.astype(F32)).astype(BF16)
    return jnp.dot(hi, upper, preferred_element_type=F32) + jnp.dot(lo, upper, preferred_element_type=F32)


def _keep(mask, x):
    return x if mask is None else jnp.where(mask, x, 0.0)


def _stick_terms(qb, kb, scale, strict):
    z = lax.dot_general(qb, kb, (((1,), (1,)), ((), ())), preferred_element_type=F32) * scale
    e = jnp.exp(-jnp.abs(z))
    lb = jnp.minimum(z, 0.0) - jnp.log(1.0 + e)
    return z, e, lb, _keep(strict, lb - z)


def _stick_fwd(q, k, v, *, blk, hps, scale, name):
    H, T, dh = q.shape

    def body(q_ref, k_ref, v_ref, o_ref, tot_ref):
        n = pl.program_id(1)
        qs = [q_ref[hh] for hh in range(hps)]
        r_i = lax.broadcasted_iota(jnp.int32, (blk, blk), 0)
        c_i = lax.broadcasted_iota(jnp.int32, (blk, blk), 1)
        upper = (r_i > c_i).astype(BF16)

        def block(j, carry, strict):
            start = pl.multiple_of(j * blk, blk)
            out = []
            for hh in range(hps):
                run, acc = carry[hh]
                kb = k_ref[hh, pl.ds(start, blk), :]
                vb = v_ref[hh, pl.ds(start, blk), :]
                _, _, lb, lk = _stick_terms(qs[hh], kb, scale, strict)
                w = _keep(strict, jnp.exp(lb + run + _tri_sum(lk, upper)))
                out.append((run + jnp.sum(lk, axis=1, keepdims=True), acc + jnp.dot(w.astype(BF16), vb, preferred_element_type=F32)))
            return tuple(out)

        init = tuple((jnp.zeros((blk, 1), F32), jnp.zeros((blk, dh), F32)) for _ in range(hps))
        res = lax.fori_loop(1, n + 1, lambda t, carry: block(n - t, carry, None), block(n, init, r_i > c_i))
        for hh in range(hps):
            tot_ref[hh], o_ref[hh] = res[hh]

    qs_ = pl.BlockSpec((hps, blk, dh), lambda g, i: (g, i, 0))
    ks_ = pl.BlockSpec((hps, T, dh), lambda g, i: (g, 0, 0))
    ts_ = pl.BlockSpec((hps, blk, 1), lambda g, i: (g, i, 0))
    return pl.pallas_call(
        body, name=name, grid=(H // hps, T // blk), in_specs=[qs_, ks_, ks_], out_specs=[qs_, ts_],
        out_shape=[jax.ShapeDtypeStruct((H, T, dh), F32), jax.ShapeDtypeStruct((H, T, 1), F32)],
        compiler_params=_params("parallel", "parallel"),
    )(q, k, v)


def _stick_bwd(q, k, v, tot, do, *, blk, hps, scale, name):
    H, T, dh = q.shape

    def body(q_ref, k_ref, v_ref, tot_ref, do_ref, dq_ref, dk_ref, dv_ref):
        n = pl.program_id(1)
        qs = [q_ref[hh] for hh in range(hps)]
        do16 = [do_ref[hh].astype(BF16) for hh in range(hps)]
        tots = [tot_ref[hh] for hh in range(hps)]
        r_i = lax.broadcasted_iota(jnp.int32, (blk, blk), 0)
        c_i = lax.broadcasted_iota(jnp.int32, (blk, blk), 1)
        upto = (r_i <= c_i).astype(BF16)
        below = (r_i < c_i).astype(BF16)

        @pl.when(n == 0)
        def _():
            dk_ref[...] = jnp.zeros_like(dk_ref)
            dv_ref[...] = jnp.zeros_like(dv_ref)

        def block(j, carry, strict):
            start = pl.multiple_of(j * blk, blk)
            out = []
            for hh in range(hps):
                run, grun, dq = carry[hh]
                kb = k_ref[hh, pl.ds(start, blk), :]
                vb = v_ref[hh, pl.ds(start, blk), :]
                z, e, lb, lk = _stick_terms(qs[hh], kb, scale, strict)
                w = _keep(strict, jnp.exp(lb + tots[hh] - (run + _tri_sum(lk, upto))))
                dw = lax.dot_general(do16[hh], vb, (((1,), (1,)), ((), ())), preferred_element_type=F32)
                g = w * dw
                before = grun + _tri_sum(g, below)
                inv = 1.0 / (1.0 + e)
                sig = jnp.where(z >= 0, inv, e * inv)
                dz16 = (_keep(strict, g * (1.0 - sig) - sig * before) * scale).astype(BF16)
                dk_ref[hh, pl.ds(start, blk), :] += lax.dot_general(dz16, qs[hh], (((0,), (0,)), ((), ())), preferred_element_type=F32)
                dv_ref[hh, pl.ds(start, blk), :] += lax.dot_general(w.astype(BF16), do16[hh], (((0,), (0,)), ((), ())),
                                                                    preferred_element_type=F32)
                out.append((run + jnp.sum(lk, axis=1, keepdims=True), grun + jnp.sum(g, axis=1, keepdims=True),
                            dq + jnp.dot(dz16, kb, preferred_element_type=F32)))
            return tuple(out)

        zero = jnp.zeros((blk, 1), F32)
        init = tuple((zero, zero, jnp.zeros((blk, dh), F32)) for _ in range(hps))
        res = block(n, lax.fori_loop(0, n, lambda j, carry: block(j, carry, None), init), r_i > c_i)
        for hh in range(hps):
            dq_ref[hh] = res[hh][2]

    qs_ = pl.BlockSpec((hps, blk, dh), lambda g, i: (g, i, 0))
    ks_ = pl.BlockSpec((hps, T, dh), lambda g, i: (g, 0, 0))
    ts_ = pl.BlockSpec((hps, blk, 1), lambda g, i: (g, i, 0))
    shp = jax.ShapeDtypeStruct((H, T, dh), F32)
    return pl.pallas_call(
        body, name=name, grid=(H // hps, T // blk), in_specs=[qs_, ks_, ks_, ts_, qs_], out_specs=[qs_, ks_, ks_],
        out_shape=[shp, shp, shp],
        compiler_params=_params("parallel", "arbitrary"),
    )(q, k, v, tot, do)


def _mix_fwd(outs, lses, *, name, rows=512):
    H, T, dh = outs[0].shape
    rows = _tile(T, rows)

    def body(o0, o1, o2, l0, l1, l2, y_ref):
        ls = [l0[...], l1[...], l2[...]]
        mx = jnp.maximum(jnp.maximum(ls[0], ls[1]), ls[2])
        es = [jnp.exp(x - mx) for x in ls]
        den = es[0] + es[1] + es[2]
        y_ref[...] = (es[0] * o0[...] + es[1] * o1[...] + es[2] * o2[...]) / den

    os_ = pl.BlockSpec((None, rows, dh), lambda h, i: (h, i, 0))
    ls_ = pl.BlockSpec((None, rows, 1), lambda h, i: (h, i, 0))
    return pl.pallas_call(
        body, name=name, grid=(H, T // rows), in_specs=[os_] * 3 + [ls_] * 3, out_specs=os_,
        out_shape=jax.ShapeDtypeStruct((H, T, dh), F32),
        compiler_params=_params("parallel", "parallel"),
    )(*outs, *lses)


def _mix_bwd(outs, lses, dy, *, name, rows=512):
    H, T, dh = outs[0].shape
    rows = _tile(T, rows)

    def body(o0, o1, o2, l0, l1, l2, dy_ref, d0, d1, d2, g0, g1, g2):
        ls = [l0[...], l1[...], l2[...]]
        mx = jnp.maximum(jnp.maximum(ls[0], ls[1]), ls[2])
        es = [jnp.exp(x - mx) for x in ls]
        den = es[0] + es[1] + es[2]
        mix = [e / den for e in es]
        dyv = dy_ref[...]
        dm = [jnp.sum(dyv * o[...], axis=1, keepdims=True) for o in (o0, o1, o2)]
        avg = mix[0] * dm[0] + mix[1] * dm[1] + mix[2] * dm[2]
        for d_ref, g_ref, w, t in zip((d0, d1, d2), (g0, g1, g2), mix, dm):
            d_ref[...] = w * dyv
            g_ref[...] = w * (t - avg)

    os_ = pl.BlockSpec((None, rows, dh), lambda h, i: (h, i, 0))
    ls_ = pl.BlockSpec((None, rows, 1), lambda h, i: (h, i, 0))
    so = jax.ShapeDtypeStruct((H, T, dh), F32)
    sl = jax.ShapeDtypeStruct((H, T, 1), F32)
    res = pl.pallas_call(
        body, name=name, grid=(H, T // rows), in_specs=[os_] * 3 + [ls_] * 3 + [os_], out_specs=[os_] * 3 + [ls_] * 3,
        out_shape=[so] * 3 + [sl] * 3,
        compiler_params=_params("parallel", "parallel"),
    )(*outs, *lses, dy)
    return res[:3], res[3:]


def _position():
    return lax.axis_index("x"), lax.axis_index("y"), lax.axis_index("c")


def _other_chips(x, y):
    return [(1 - x, y), (x, 1 - y), (1 - x, 1 - y)]


HBM_SPEC = pl.BlockSpec(memory_space=pltpu.HBM)
SEM_SPEC = pl.BlockSpec(memory_space=pltpu.SEMAPHORE)
ANY_SPEC = pl.BlockSpec(memory_space=pl.ANY)
SPLIT_COPY = pltpu.CompilerParams(has_side_effects=pltpu.SideEffectType.DATAFLOW_SIDE_EFFECTING)


def _in_hbm(a):
    return pltpu.with_memory_space_constraint(a, pltpu.HBM)


def _chip_copies(srcs, lands, send_sems, recv_sems, src_of, dst_of):
    x, y, c = _position()
    return [pltpu.make_async_remote_copy(
        src_ref=src_of(srcs[w], chip, c), dst_ref=dst_of(lands[w], j, chip, (x, y), c), send_sem=send_sems[3 * w + j],
        recv_sem=recv_sems[3 * w + j], device_id=(chip[0], chip[1], c), device_id_type=MESH)
        for w in range(len(srcs)) for j, chip in enumerate(_other_chips(x, y))]


def _copies_start(srcs, land_shapes, after, src_of, dst_of, *, name):
    n = len(srcs)
    m = 3 * n

    def body(*refs):
        src_refs, land_refs = refs[:n], refs[n:2 * n]
        send_sems, recv_sems = refs[2 * n + 1:2 * n + 1 + m], refs[2 * n + 1 + m:2 * n + 1 + 2 * m]
        token = refs[-1]
        for cp in _chip_copies(src_refs, land_refs, send_sems, recv_sems, src_of, dst_of):
            cp.start()
        token[...] = jnp.zeros_like(token)

    lands = [_in_hbm(lax.empty(s.shape, s.dtype)) for s in land_shapes]
    out_shape = ([pltpu.SemaphoreType.DMA(())] * (2 * m)
                 + [pltpu.HBM(a.shape, a.dtype) for a in srcs] + [pltpu.HBM(s.shape, s.dtype) for s in land_shapes]
                 + [jax.ShapeDtypeStruct((8, 128), F32)])
    res = pl.pallas_call(
        body, name=name, in_specs=[HBM_SPEC] * (2 * n) + [ANY_SPEC],
        out_specs=[SEM_SPEC] * (2 * m) + [HBM_SPEC] * (2 * n) + [pl.BlockSpec(memory_space=pltpu.VMEM)],
        out_shape=out_shape, input_output_aliases={i: 2 * m + i for i in range(2 * n)}, compiler_params=SPLIT_COPY,
    )(*[_in_hbm(a) for a in srcs], *lands, after)
    return (list(res[:m]), list(res[m:2 * m]), list(res[2 * m:2 * m + n]), list(res[2 * m + n:2 * m + 2 * n]), res[-1])


def _copies_wait(started, after, src_of, dst_of, *, name):
    send_sems, recv_sems, srcs, lands, _ = started
    n = len(srcs)
    m = 3 * n

    def body(*refs):
        src_refs, land_refs = refs[:n], refs[n:2 * n]
        send_refs, recv_refs = refs[2 * n:2 * n + m], refs[2 * n + m:2 * n + 2 * m]
        for cp in _chip_copies(src_refs, land_refs, send_refs, recv_refs, src_of, dst_of):
            cp.wait_send()
            cp.wait_recv()

    res = pl.pallas_call(
        body, name=name, in_specs=[HBM_SPEC] * (2 * n) + [SEM_SPEC] * (2 * m) + [ANY_SPEC], out_specs=[HBM_SPEC] * (2 * n),
        out_shape=[pltpu.HBM(a.shape, a.dtype) for a in srcs + lands],
        input_output_aliases={i: i for i in range(2 * n)}, compiler_params=SPLIT_COPY,
    )(*srcs, *lands, *send_sems, *recv_sems, after)
    return list(res[:n]), list(res[n:])


def _half_rows(ref, core):
    half = ref.shape[-2] // 2
    return pl.ds(core * half, half)


def _gather_src(src, chip, core):
    return src.at[_half_rows(src, core), :]


def _gather_dst(land, j, chip, me, core):
    return land.at[2 * me[0] + me[1], _half_rows(land, core), :]


def _gather_landed(land, j, chip, me, core):
    return land.at[2 * chip[0] + chip[1], _half_rows(land, core), :]


def _exchange_src(src, chip, core):
    return src.at[2 * chip[0] + chip[1]]


def _exchange_dst(land, j, chip, me, core):
    return land.at[j]


def _forward_halves(bufs, *, name):
    n = len(bufs)

    def body(*refs):
        ins, outs = refs[:n], refs[n:2 * n]
        send_sems, recv_sems = refs[2 * n:]
        x, y, c = _position()
        chips = _other_chips(x, y)

        def copy(w, j, chip, core):
            slab = _gather_landed(ins[w], j, chip, (x, y), core)
            return pltpu.make_async_remote_copy(src_ref=slab, dst_ref=_gather_landed(outs[w], j, chip, (x, y), core),
                                                send_sem=send_sems.at[w, j], recv_sem=recv_sems.at[w, j],
                                                device_id=(x, y, 1 - c), device_id_type=MESH)

        sends = [copy(w, j, chip, c) for w in range(n) for j, chip in enumerate(chips)]
        for cp in sends:
            cp.start()
        for w in range(n):
            for j, chip in enumerate(chips):
                copy(w, j, chip, 1 - c).wait_recv()
        for cp in sends:
            cp.wait_send()

    return pl.pallas_call(
        body, name=name, in_specs=[ANY_SPEC] * n, out_specs=[ANY_SPEC] * n,
        out_shape=[jax.ShapeDtypeStruct(a.shape, a.dtype) for a in bufs], input_output_aliases={w: w for w in range(n)},
        scratch_shapes=[pltpu.SemaphoreType.DMA((n, 3)), pltpu.SemaphoreType.DMA((n, 3))],
    )(*bufs)


def _swap_other_half(arrs, *, name):
    n = len(arrs)

    def body(*refs):
        ins, outs = refs[:n], refs[n:2 * n]
        send_sems, recv_sems = refs[2 * n:]
        x, y, c = _position()
        cps = []
        for w in range(n):
            half = ins[w].shape[1] // 2
            cps.append(pltpu.make_async_remote_copy(
                src_ref=ins[w].at[:, pl.ds((1 - c) * half, half), :], dst_ref=outs[w], send_sem=send_sems.at[w],
                recv_sem=recv_sems.at[w], device_id=(x, y, 1 - c), device_id_type=MESH))
            cps[-1].start()
        for cp in cps:
            cp.wait()

    hbm = pl.BlockSpec(memory_space=pl.ANY)
    return pl.pallas_call(
        body, name=name, in_specs=[hbm] * n, out_specs=[hbm] * n,
        out_shape=[jax.ShapeDtypeStruct((a.shape[0], a.shape[1] // 2, a.shape[2]), a.dtype) for a in arrs],
        scratch_shapes=[pltpu.SemaphoreType.DMA((n,)), pltpu.SemaphoreType.DMA((n,))],
    )(*arrs)


def _share_halves(bufs, *, name):
    n = len(bufs)

    def body(*refs):
        ins, outs = refs[:n], refs[n:2 * n]
        send_sems, recv_sems = refs[2 * n:]
        x, y, c = _position()
        sends = []
        for w in range(n):
            sends.append(pltpu.make_async_remote_copy(src_ref=ins[w].at[c], dst_ref=outs[w].at[c], send_sem=send_sems.at[w],
                                                      recv_sem=recv_sems.at[w], device_id=(x, y, 1 - c), device_id_type=MESH))
            sends[-1].start()
        for w in range(n):
            pltpu.make_async_remote_copy(src_ref=ins[w].at[1 - c], dst_ref=outs[w].at[1 - c], send_sem=send_sems.at[w],
                                         recv_sem=recv_sems.at[w], device_id=(x, y, 1 - c), device_id_type=MESH).wait_recv()
        for cp in sends:
            cp.wait_send()

    hbm = pl.BlockSpec(memory_space=pl.ANY)
    return pl.pallas_call(
        body, name=name, in_specs=[hbm] * n, out_specs=[hbm] * n,
        out_shape=[jax.ShapeDtypeStruct(a.shape, a.dtype) for a in bufs],
        input_output_aliases={w: w for w in range(n)},
        scratch_shapes=[pltpu.SemaphoreType.DMA((n,)), pltpu.SemaphoreType.DMA((n,))],
    )(*bufs)


def _all_gather8(v, *, name):
    m, n = v.shape

    def body(v_ref, out_ref, send_sems, recv_sems, local_sem):
        x, y, c = _position()
        me, sibling = (x, y, c), (x, y, 1 - c)
        chips = _other_chips(x, y)

        def rows(px, py, pc):
            return out_ref.at[pl.ds((4 * px + 2 * py + pc) * m, m), :]

        def copy(k, block, to, src=None):
            return pltpu.make_async_remote_copy(src_ref=rows(*block) if src is None else src, dst_ref=rows(*block),
                                                send_sem=send_sems.at[k], recv_sem=recv_sems.at[k], device_id=to, device_id_type=MESH)

        mine = pltpu.make_async_copy(v_ref, rows(*me), local_sem)
        mine.start()
        first = [copy(0, me, sibling, src=v_ref)]
        first += [copy(1 + j, me, (*chip, c), src=v_ref) for j, chip in enumerate(chips)]
        for cp in first:
            cp.start()
        passed = [copy(4 + j, (*chip, c), sibling) for j, chip in enumerate(chips)]
        for j, chip in enumerate(chips):
            copy(1 + j, (*chip, c), me).wait_recv()
            passed[j].start()
        copy(0, sibling, me).wait_recv()
        for j, chip in enumerate(chips):
            copy(4 + j, (*chip, 1 - c), me).wait_recv()
        for cp in first + passed:
            cp.wait_send()
        mine.wait()

    vmem = pl.BlockSpec(memory_space=pltpu.VMEM)
    return pl.pallas_call(
        body, name=name, in_specs=[vmem], out_specs=vmem, out_shape=jax.ShapeDtypeStruct((8 * m, n), v.dtype),
        scratch_shapes=[pltpu.SemaphoreType.DMA((7,)), pltpu.SemaphoreType.DMA((7,)), pltpu.SemaphoreType.DMA],
    )(v)


def _add_half(full, part, core, *, name, rows=256):
    P, R, C = full.shape
    half = R // 2
    rows = _tile(half, rows)
    nb = half // rows

    def body(core_ref, a_ref, b_ref, o_ref):
        o_ref[...] = (a_ref[...] + b_ref[...].astype(F32)).astype(BF16)

    spec = pl.BlockSpec((None, rows, C), lambda p, i, core_ref: (p, i, 0))
    grid_spec = pltpu.PrefetchScalarGridSpec(
        num_scalar_prefetch=1, grid=(P, nb),
        in_specs=[pl.BlockSpec((None, rows, C), lambda p, i, core_ref: (p, core_ref[0] * nb + i, 0)), spec], out_specs=spec)
    return pl.pallas_call(
        body, name=name, grid_spec=grid_spec, out_shape=jax.ShapeDtypeStruct((P, half, C), BF16),
        compiler_params=_params("parallel", "parallel"),
    )(core, full, part)


def _sum_parts(own, landed, where, *, name, rows=256):
    _, R, C = own.shape
    rows = _tile(R, rows)

    def body(where_ref, own_ref, land_ref, o_ref):
        acc = own_ref[...].astype(F32)
        for j in range(3):
            acc = acc + land_ref[j].astype(F32)
        o_ref[...] = acc

    grid_spec = pltpu.PrefetchScalarGridSpec(
        num_scalar_prefetch=1, grid=(R // rows,),
        in_specs=[pl.BlockSpec((None, rows, C), lambda i, where_ref: (where_ref[0], i, 0)),
                  pl.BlockSpec((3, rows, C), lambda i, where_ref: (0, i, 0))],
        out_specs=pl.BlockSpec((None, rows, C), lambda i, where_ref: (where_ref[1], i, 0)))
    return pl.pallas_call(
        body, name=name, grid_spec=grid_spec, out_shape=jax.ShapeDtypeStruct((2, R, C), F32),
        compiler_params=_params("parallel"),
    )(where, own, landed)


def _sum_slabs(a, *, name, rows=256):
    P, R, C = a.shape
    rows = _tile(R, rows)

    def body(a_ref, o_ref):
        acc = a_ref[0].astype(F32)
        for p in range(1, P):
            acc = acc + a_ref[p].astype(F32)
        o_ref[...] = acc

    return pl.pallas_call(
        body, name=name, grid=(R // rows,), in_specs=[pl.BlockSpec((P, rows, C), lambda i: (0, i, 0))],
        out_specs=pl.BlockSpec((rows, C), lambda i: (i, 0)),
        out_shape=jax.ShapeDtypeStruct((R, C), F32), compiler_params=_params("parallel"),
    )(a)


def _reduce_scatter_start(grads, grads16, *, name):
    core = lax.axis_index("c").astype(jnp.int32).reshape(1)
    got = _swap_other_half(grads16, name=name + "_swap")
    chip_sums = [_add_half(g, r, core, name=f"{name}_add{w}") for w, (g, r) in enumerate(zip(grads, got))]
    lands = [jax.ShapeDtypeStruct((3,) + s.shape[1:], s.dtype) for s in chip_sums]
    return _copies_start(chip_sums, lands, chip_sums[0], _exchange_src, _exchange_dst, name=name + "_start")


def _reduce_scatter_finish(started, after, *, name):
    core = lax.axis_index("c").astype(jnp.int32)
    chip = (2 * lax.axis_index("x") + lax.axis_index("y")).astype(jnp.int32)
    where = jnp.stack([chip, core])
    sums, landed = _copies_wait(started, after, _exchange_src, _exchange_dst, name=name + "_wait")
    reduced = [_sum_parts(s, a, where, name=f"{name}_sum{w}") for w, (s, a) in enumerate(zip(sums, landed))]
    both = _share_halves(reduced, name=name + "_share")
    return [b.reshape(2 * b.shape[1], b.shape[2]) for b in both]


def _to_heads(t, heads, dil=1):
    S = t.shape[0]
    d = t.shape[1] // heads
    return jnp.transpose(t.reshape(S // dil, dil, heads, d), (2, 1, 0, 3)).reshape(heads, S, d)


def _from_heads(t, dil=1):
    heads, S, d = t.shape
    return jnp.transpose(t.reshape(heads, dil, S // dil, d), (2, 1, 0, 3)).reshape(S, heads * d)


def _unstride(t, dil):
    heads, S, d = t.shape
    return jnp.transpose(t.reshape(heads, dil, S // dil, d), (0, 2, 1, 3)).reshape(heads, S, d)


def _restride(t, dil):
    heads, S, d = t.shape
    return jnp.transpose(t.reshape(heads, S // dil, dil, d), (0, 2, 1, 3)).reshape(heads, S, d)


def _pad_cols(t, width, padded):
    S = t.shape[0]
    return jnp.pad(t.reshape(S, N_CHIPS, width), ((0, 0), (0, 0), (0, padded - width))).reshape(S, N_CHIPS * padded)


def _unpad_cols(t, width, padded):
    S = t.shape[0]
    return t.reshape(S, N_CHIPS, padded)[:, :, :width].reshape(S, N_CHIPS * width)


def _alibi(n):
    return 2.0 ** (-8.0 * jnp.arange(1, n + 1, dtype=F32) / n)


B_KW = [dict(hps=4, nb_seq=SEQ // d // ATT_BLK, n_back=w // d, scale=1.0 / math.sqrt(HEAD_DIM), dist_scale=d)
        for w, d in B_PATTERNS]
A_KW = dict(hps=A_Q_HEADS // A_KV_HEADS, nb_seq=SEQ // ATT_BLK, n_back=A_WINDOW - 1, scale=1.0 / math.sqrt(HEAD_DIM), dist_scale=1)
D_KW = dict(blk=256, hps=2, scale=1.0 / math.sqrt(D_NOPE + D_ROPE))
C_KW = dict(blk=256, hps=2, scale=1.0 / math.sqrt(HEAD_DIM))


def _front_block(t):
    return jnp.pad(t, ((0, 0), (ATT_BLK, 0), (0, 0)))


def _rope_tables(reps):
    inv_freq = ROPE_BASE ** (-jnp.arange(0, D_ROPE, 2, dtype=F32) / D_ROPE)
    ang = jnp.arange(SEQ, dtype=F32)[:, None] * inv_freq[None, :]
    return jnp.tile(jnp.cos(ang), (1, reps)), jnp.tile(jnp.sin(ang), (1, reps))


def _mlp_fwd(x, x16, w1, w2, g, b, tag):
    hid, act = _matmul(x16, w1, mode="nn", out_dtype=BF16, epilogue="relu2", name=f"mlp{tag}_up")
    m = _matmul(act, w2, mode="nn", out_dtype=F32, name=f"mlp{tag}_down")
    y, y16, u = _norm_fwd(x, m, g, b, center=True, eps=LN_EPS, alpha=ALPHA, name=f"ln2_{tag}")
    return y, y16, (x16, hid, act, u)


def _mlp_bwd(dy, saved, w1, w2, g, tag, after=None):
    x16, hid, act, u = saved
    du, du16, dg, db = _norm_bwd(dy, u, g, center=True, eps=LN_EPS, name=f"ln2_{tag}_bwd", after=after)
    dw2 = _matmul_tn(act, du16, shards=1, name=f"mlp{tag}_dw2")
    dhid = _matmul(du16, w2, mode="nt", out_dtype=BF16, epilogue="drelu2", extra=hid, name=f"mlp{tag}_dact")
    dw1 = _matmul_tn(x16, dhid, shards=N_CHIPS, name=f"mlp{tag}_dw1")
    dx = _matmul(dhid, w1, mode="nt", out_dtype=F32, epilogue="add", extra=du, alpha=ALPHA, name=f"mlp{tag}_dx")
    return dx, dw1, dw2, dg, db


def _even_fwd(x16, w_in, sinks):
    h = _unpad_cols(_matmul(x16, w_in, mode="nn", out_dtype=BF16, name="even_in"), EVEN_IN // N_CHIPS, EVEN_IN_PAD)
    qa = _to_heads(h[:, :A_Q_W], A_Q_HEADS)
    ka = _front_block(_to_heads(h[:, A_Q_W:A_Q_W + A_KV_W], A_KV_HEADS))
    va = _front_block(_to_heads(h[:, A_Q_W + A_KV_W:A_Q_W + 2 * A_KV_W], A_KV_HEADS))
    slopes_a, slopes_b = _alibi(A_Q_HEADS), _alibi(B_HEADS)
    oa, lse_a = _band_fwd(qa, ka, va, slopes_a, sinks, name="swa_fwd", **A_KW)
    base = A_Q_W + 2 * A_KV_W
    qkv_b, outs, lses = [], [], []
    for gi, (_, dil) in enumerate(B_PATTERNS):
        cols = h[:, base + gi * 3 * B_W:base + (gi + 1) * 3 * B_W]
        q, k, v = (_to_heads(cols[:, i * B_W:(i + 1) * B_W], B_HEADS, dil) for i in range(3))
        k, v = _front_block(k), _front_block(v)
        o, lse = _band_fwd(q, k, v, slopes_b, None, name=f"dil{gi}_fwd", **B_KW[gi])
        qkv_b.append((q, k, v, o, lse))
        outs.append(_unstride(o, dil))
        lses.append(_unstride(lse, dil))
    ob = _mix_fwd(outs, lses, name="dil_mix")
    y16 = jnp.concatenate([_from_heads(oa), _from_heads(ob)], axis=-1).astype(BF16)
    return y16, (x16, qa, ka, va, oa, lse_a, qkv_b, outs, lses, y16)


def _even_bwd(dmix16, du, saved, w_in, sinks, w_out):
    x16, qa, ka, va, oa, lse_a, qkv_b, outs, lses, y16 = saved
    slopes_a, slopes_b = _alibi(A_Q_HEADS), _alibi(B_HEADS)
    dw_out = _matmul_tn(y16, dmix16, shards=N_CHIPS, name="even_dwout")
    dy = _matmul(dmix16, w_out, mode="nt", out_dtype=F32, name="even_dy")
    doa = _to_heads(dy[:, :A_Q_W], A_Q_HEADS)
    dob = _to_heads(dy[:, A_Q_W:], B_HEADS)
    dqa, dka, dva, dsink = _band_bwd(qa, ka, va, oa, doa, lse_a, None, slopes_a, sinks, name="swa_bwd", **A_KW)
    douts, dlses = _mix_bwd(outs, lses, dob, name="dil_mix_bwd")
    parts = [_from_heads(dqa), _from_heads(dka[:, ATT_BLK:]), _from_heads(dva[:, ATT_BLK:])]
    for gi, (_, dil) in enumerate(B_PATTERNS):
        q, k, v, o, lse = qkv_b[gi]
        dq, dk, dv = _band_bwd(q, k, v, o, _restride(douts[gi], dil), lse, _restride(dlses[gi], dil), slopes_b, None,
                               name=f"dil{gi}_bwd", **B_KW[gi])
        parts += [_from_heads(dq, dil), _from_heads(dk[:, ATT_BLK:], dil), _from_heads(dv[:, ATT_BLK:], dil)]
    dh16 = _pad_cols(jnp.concatenate(parts, axis=-1), EVEN_IN // N_CHIPS, EVEN_IN_PAD).astype(BF16)
    dw_in = _matmul_tn(x16, dh16, shards=N_CHIPS, name="even_dwin")
    dx = _matmul(dh16, w_in, mode="nt", out_dtype=F32, epilogue="add", extra=du, alpha=ALPHA, name="even_dx")
    return dx, dw_in, dsink[:, 0, 0], dw_out


def _odd_fwd(x16, w_in, qn_g, kvn_g, w_uq, w_ukv, w_out):
    S = x16.shape[0]
    h = _unpad_cols(_matmul(x16, w_in, mode="nn", out_dtype=F32, name="odd_in"), ODD_IN // N_CHIPS, ODD_IN_PAD)
    qc, kc, vc = (_to_heads(h[:, i * C_W:(i + 1) * C_W].astype(BF16), C_HEADS) for i in range(3))
    cq = h[:, 3 * C_W:3 * C_W + D_Q_RANK]
    ckv = h[:, 3 * C_W + D_Q_RANK:3 * C_W + D_Q_RANK + D_KV_RANK]
    kr = h[:, 3 * C_W + D_Q_RANK + D_KV_RANK:]
    oc, tot = _stick_fwd(qc, kc, vc, name="stick_fwd", **C_KW)
    _, cqn16, _ = _norm_fwd(cq, None, qn_g, None, center=False, eps=RMS_EPS, alpha=1.0, name="q_rms")
    _, ckvn16, _ = _norm_fwd(ckv, None, kvn_g, None, center=False, eps=RMS_EPS, alpha=1.0, name="kv_rms")
    q = _matmul(cqn16, w_uq, mode="nn", out_dtype=F32, name="mla_uq").reshape(S, D_HEADS, D_NOPE + D_ROPE)
    kv = _matmul(ckvn16, w_ukv, mode="nn", out_dtype=F32, name="mla_ukv").reshape(S, D_HEADS, D_NOPE + D_V)
    half = D_ROPE // 2
    q_rope = q[..., D_NOPE:]
    x1 = jnp.concatenate([q_rope[..., :half].reshape(S, D_HEADS * half), kr[:, :half]], axis=-1)
    x2 = jnp.concatenate([q_rope[..., half:].reshape(S, D_HEADS * half), kr[:, half:]], axis=-1)
    cos, sin = _rope_tables(D_HEADS + 1)
    y1, y2 = _rope(x1[None], x2[None], cos, sin, name="rope_fwd")
    nq = D_HEADS * half
    q_rot = jnp.concatenate([y1[:, :nq].reshape(S, D_HEADS, half), y2[:, :nq].reshape(S, D_HEADS, half)], axis=-1)
    kr_rot = jnp.concatenate([y1[:, nq:], y2[:, nq:]], axis=-1)
    qd = jnp.transpose(jnp.concatenate([q[..., :D_NOPE], q_rot], axis=-1), (1, 0, 2)).astype(BF16)
    kd = jnp.transpose(jnp.concatenate([kv[..., :D_NOPE], jnp.broadcast_to(kr_rot[:, None, :], (S, D_HEADS, D_ROPE))], axis=-1),
                       (1, 0, 2)).astype(BF16)
    vd = jnp.transpose(kv[..., D_NOPE:], (1, 0, 2)).astype(BF16)
    od, lse_d = _causal_fwd(qd, kd, vd, name="mla_fwd", **D_KW)
    y16 = jnp.concatenate([_from_heads(oc), _from_heads(od)], axis=-1).astype(BF16)
    mixed = _matmul(y16, w_out, mode="nn", out_dtype=F32, name="odd_out")
    return mixed, (x16, qc, kc, vc, tot, cq, ckv, cqn16, ckvn16, qd, kd, vd, od, lse_d, y16)


def _odd_bwd(dmix16, du, saved, w_in, qn_g, kvn_g, w_uq, w_ukv, w_out):
    x16, qc, kc, vc, tot, cq, ckv, cqn16, ckvn16, qd, kd, vd, od, lse_d, y16 = saved
    S = x16.shape[0]
    half = D_ROPE // 2
    dw_out = _matmul_tn(y16, dmix16, shards=1, name="odd_dwout")
    dy = _matmul(dmix16, w_out, mode="nt", out_dtype=F32, name="odd_dy")
    doc = _to_heads(dy[:, :C_W], C_HEADS)
    dod = _to_heads(dy[:, C_W:], D_HEADS)
    dqc, dkc, dvc = _stick_bwd(qc, kc, vc, tot, doc, name="stick_bwd", **C_KW)
    dqd, dkd, dvd = _causal_bwd(qd, kd, vd, od, dod, lse_d, name="mla_bwd", **D_KW)
    cos, sin = _rope_tables(D_HEADS + 1)
    nq = D_HEADS * half
    gq1 = jnp.transpose(dqd[..., D_NOPE:D_NOPE + half], (1, 0, 2)).reshape(1, S, nq)
    gq2 = jnp.transpose(dqd[..., D_NOPE + half:], (1, 0, 2)).reshape(1, S, nq)
    dq1, dq2 = _rope(gq1, gq2, cos[:, :nq], -sin[:, :nq], name="rope_bwd_q")
    dk1, dk2 = _rope(dkd[..., D_NOPE:D_NOPE + half], dkd[..., D_NOPE + half:], cos[:, nq:], -sin[:, nq:], name="rope_bwd_k")
    dq_full = jnp.concatenate([jnp.transpose(dqd[..., :D_NOPE], (1, 0, 2)), dq1.reshape(S, D_HEADS, half),
                               dq2.reshape(S, D_HEADS, half)], axis=-1).reshape(S, D_HEADS * (D_NOPE + D_ROPE)).astype(BF16)
    dkv_full = jnp.concatenate([jnp.transpose(dkd[..., :D_NOPE], (1, 0, 2)), jnp.transpose(dvd, (1, 0, 2))],
                               axis=-1).reshape(S, D_HEADS * (D_NOPE + D_V)).astype(BF16)
    dw_uq = _matmul_tn(cqn16, dq_full, shards=N_CHIPS, name="mla_dwuq")
    dw_ukv = _matmul_tn(ckvn16, dkv_full, shards=N_CHIPS, name="mla_dwukv")
    dcqn = _matmul(dq_full, w_uq, mode="nt", out_dtype=F32, name="mla_dcq")
    dckvn = _matmul(dkv_full, w_ukv, mode="nt", out_dtype=F32, name="mla_dckv")
    dcq, _, dqn_g, _ = _norm_bwd(dcqn, cq, qn_g, center=False, eps=RMS_EPS, name="q_rms_bwd")
    dckv, _, dkvn_g, _ = _norm_bwd(dckvn, ckv, kvn_g, center=False, eps=RMS_EPS, name="kv_rms_bwd")
    dh = jnp.concatenate([_from_heads(dqc), _from_heads(dkc), _from_heads(dvc), dcq, dckv, dk1, dk2], axis=-1)
    dh16 = _pad_cols(dh, ODD_IN // N_CHIPS, ODD_IN_PAD).astype(BF16)
    dw_in = _matmul_tn(x16, dh16, shards=N_CHIPS, name="odd_dwin")
    dx = _matmul(dh16, w_in, mode="nt", out_dtype=F32, epilogue="add", extra=du, alpha=ALPHA, name="odd_dx")
    return dx, dw_in, dqn_g[0], dkvn_g[0], dw_uq, dw_ukv, dw_out


WEIGHT_GROUPS = (("even_w_in",), ("even_w_out", "mlp_w1_0", "mlp_w2_0"), ("odd_w_in", "odd_w_uq", "odd_w_ukv", "odd_w_out"),
                 ("mlp_w1_1", "mlp_w2_1"))
GRAD_GROUPS = (("mlp_w2_1", "mlp_w1_1"), ("odd_w_out", "odd_w_uq", "odd_w_ukv", "odd_w_in"), ("mlp_w2_0", "mlp_w1_0"),
               ("even_w_out", "even_w_in"))


def _local_step(x, target, small, weights_for, send_grads):
    def by_rows(pair, rows):
        return tuple(t.reshape(N_CHIPS, rows // N_CHIPS, D_MODEL) for t in pair)

    x16 = x.astype(BF16)
    w = dict(weights_for(0, x16))
    y16, sv_e = _even_fwd(x16, w["even_w_in"], small["even_sinks"])
    w.update(weights_for(1, y16))
    mixed0 = _matmul(y16, w["even_w_out"], mode="nn", out_dtype=F32, name="even_out")
    x1, x1_16, u01 = _norm_fwd(x, mixed0, small["ln1_g"][0], small["ln1_b"][0], center=True, eps=LN_EPS, alpha=ALPHA, name="ln1_0")
    w1_0, w2_0 = w["mlp_w1_0"], w["mlp_w2_0"].reshape(1, D_FF, D_MODEL)
    x2, x2_16, sv_m0 = _mlp_fwd(x1, x1_16, w1_0, w2_0, small["ln2_g"][0], small["ln2_b"][0], 0)
    w.update(weights_for(2, x2_16))
    odd_w = (w["odd_w_in"], small["odd_q_norm_g"], small["odd_kv_norm_g"], w["odd_w_uq"], w["odd_w_ukv"],
             w["odd_w_out"].reshape(1, D_MODEL, D_MODEL))
    mixed1, sv_o = _odd_fwd(x2_16, *odd_w)
    x3, x3_16, u11 = _norm_fwd(x2, mixed1, small["ln1_g"][1], small["ln1_b"][1], center=True, eps=LN_EPS, alpha=ALPHA, name="ln1_1")
    w.update(weights_for(3, x3_16))
    w1_1, w2_1 = w["mlp_w1_1"], w["mlp_w2_1"].reshape(1, D_FF, D_MODEL)
    x4, _, sv_m1 = _mlp_fwd(x3, x3_16, w1_1, w2_1, small["ln2_g"][1], small["ln2_b"][1], 1)
    dy, loss = _loss_head(x4, target, name="loss_head")

    dx3, dw1_1, dw2_1, dln2g_1, dln2b_1 = _mlp_bwd(dy, sv_m1, w1_1, w2_1, small["ln2_g"][1], 1)
    sent = send_grads(0, dict(mlp_w2_1=by_rows(dw2_1, D_FF), mlp_w1_1=dw1_1))
    du, du16, dln1g_1, dln1b_1 = _norm_bwd(dx3, u11, small["ln1_g"][1], center=True, eps=LN_EPS, name="ln1_1_bwd", after=sent)
    dx2, dwin_o, dqn_g, dkvn_g, dwuq, dwukv, dwout_o = _odd_bwd(du16, du, sv_o, *odd_w)
    sent = send_grads(1, dict(odd_w_out=by_rows(dwout_o, D_MODEL), odd_w_uq=dwuq, odd_w_ukv=dwukv, odd_w_in=dwin_o))
    dx1, dw1_0, dw2_0, dln2g_0, dln2b_0 = _mlp_bwd(dx2, sv_m0, w1_0, w2_0, small["ln2_g"][0], 0, after=sent)
    sent = send_grads(2, dict(mlp_w2_0=by_rows(dw2_0, D_FF), mlp_w1_0=dw1_0))
    du, du16, dln1g_0, dln1b_0 = _norm_bwd(dx1, u01, small["ln1_g"][0], center=True, eps=LN_EPS, name="ln1_0_bwd", after=sent)
    dx0, dwin_e, dsinks, dwout_e = _even_bwd(du16, du, sv_e, w["even_w_in"], small["even_sinks"], w["even_w_out"])
    send_grads(3, dict(even_w_out=dwout_e, even_w_in=dwin_e))

    sm = dict(even_sinks=dsinks, odd_q_norm_g=dqn_g, odd_kv_norm_g=dkvn_g,
              ln1_g=jnp.concatenate([dln1g_0, dln1g_1]), ln1_b=jnp.concatenate([dln1b_0, dln1b_1]),
              ln2_g=jnp.concatenate([dln2g_0, dln2g_1]), ln2_b=jnp.concatenate([dln2b_0, dln2b_1]))
    return loss, dx0, sm


SMALL_ORDER = ("ln1_g", "ln1_b", "ln2_g", "ln2_b", "even_sinks", "odd_q_norm_g", "odd_kv_norm_g")
SMALL_COLS = 2176


def _pack_small(parts):
    flat = jnp.concatenate([p.reshape(-1) for p in parts])
    return jnp.pad(flat, (0, 8 * SMALL_COLS - flat.shape[0])).reshape(8, SMALL_COLS)


def _unpack_small(packed, shapes):
    flat = packed.reshape(-1)
    out, pos = [], 0
    for s in shapes:
        n = math.prod(s)
        out.append(flat[pos:pos + n].reshape(s))
        pos += n
    return out


def kernel(x, even_w_in, even_sinks, even_w_out, odd_w_in, odd_q_norm_g, odd_kv_norm_g, odd_w_uq, odd_w_ukv, odd_w_out, ln1_g, ln1_b, mlp_w1, mlp_w2, ln2_g, ln2_b, loss_target, m_even_w_in, m_even_sinks, m_even_w_out, m_odd_w_in, m_odd_q_norm_g, m_odd_kv_norm_g, m_odd_w_uq, m_odd_w_ukv, m_odd_w_out, m_ln1_g, m_ln1_b, m_mlp_w1, m_mlp_w2, m_ln2_g, m_ln2_b, v_even_w_in, v_even_sinks, v_even_w_out, v_odd_w_in, v_odd_q_norm_g, v_odd_kv_norm_g, v_odd_w_uq, v_odd_w_ukv, v_odd_w_out, v_ln1_g, v_ln1_b, v_mlp_w1, v_mlp_w2, v_ln2_g, v_ln2_b):
    chip = 2 * lax.axis_index("x") + lax.axis_index("y")
    e_w, o_w = EVEN_IN // N_CHIPS, ODD_IN // N_CHIPS

    shards = dict(
        even_w_in=jnp.pad(even_w_in[0], ((0, 0), (0, EVEN_IN_PAD - e_w))), even_w_out=even_w_out[0],
        odd_w_in=jnp.pad(odd_w_in[0], ((0, 0), (0, ODD_IN_PAD - o_w))), odd_w_uq=odd_w_uq[0], odd_w_ukv=odd_w_ukv[0],
        odd_w_out=odd_w_out[0], mlp_w1_0=mlp_w1[0], mlp_w1_1=mlp_w1[1], mlp_w2_0=mlp_w2[0], mlp_w2_1=mlp_w2[1])
    names = list(shards)
    own16 = {n: shards[n].astype(BF16) for n in names}
    gather_started, token = [], own16[WEIGHT_GROUPS[0][0]]
    for g, group in enumerate(WEIGHT_GROUPS):
        lands = [jax.ShapeDtypeStruct((N_CHIPS,) + own16[n].shape, BF16) for n in group]
        gather_started.append(_copies_start([own16[n] for n in group], lands, token, _gather_src, _gather_dst,
                                            name=f"gather{g}_start"))
        token = gather_started[-1][-1]
    all_started = token

    def weights_for(g, after):
        group = WEIGHT_GROUPS[g]
        _, landed = _copies_wait(gather_started[g], all_started if g == 0 else after, _gather_src, _gather_landed,
                                 name=f"gather{g}_wait")
        full = _forward_halves(landed, name=f"gather{g}_forward")
        return {n: lax.dynamic_update_slice(f, own16[n][None], (chip, 0, 0)) for n, f in zip(group, full)}

    grads_started = {}

    def send_grads(g, pairs):
        group = GRAD_GROUPS[g]
        grads_started[g] = _reduce_scatter_start([pairs[n][0] for n in group], [pairs[n][1] for n in group], name=f"grads{g}")
        return grads_started[g][-1]

    norm_rows = jnp.pad(jnp.concatenate([odd_q_norm_g[0], odd_kv_norm_g[0]])[None], ((0, 7), (0, 64)))
    norm_all = _all_gather8(norm_rows, name="gather_norm_gains")[0::16]
    qn_w, kvn_w = D_Q_RANK // N_CHIPS, D_KV_RANK // N_CHIPS
    small = dict(even_sinks=even_sinks[0], odd_q_norm_g=norm_all[:, :qn_w].reshape(D_Q_RANK),
                 odd_kv_norm_g=norm_all[:, qn_w:qn_w + kvn_w].reshape(D_KV_RANK), ln1_g=ln1_g, ln1_b=ln1_b, ln2_g=ln2_g, ln2_b=ln2_b)

    loss_row, grad_x, sm = _local_step(x[0], loss_target[0], small, weights_for, send_grads)
    loss = lax.psum(loss_row[0, 0], ("x", "y", "c"))

    weights = dict(even_w_in=even_w_in, even_sinks=even_sinks, even_w_out=even_w_out, odd_w_in=odd_w_in, odd_q_norm_g=odd_q_norm_g,
                   odd_kv_norm_g=odd_kv_norm_g, odd_w_uq=odd_w_uq, odd_w_ukv=odd_w_ukv, odd_w_out=odd_w_out, ln1_g=ln1_g, ln1_b=ln1_b,
                   mlp_w1=mlp_w1, mlp_w2=mlp_w2, ln2_g=ln2_g, ln2_b=ln2_b)
    m_in = dict(even_w_in=m_even_w_in, even_sinks=m_even_sinks, even_w_out=m_even_w_out, odd_w_in=m_odd_w_in,
                odd_q_norm_g=m_odd_q_norm_g, odd_kv_norm_g=m_odd_kv_norm_g, odd_w_uq=m_odd_w_uq, odd_w_ukv=m_odd_w_ukv,
                odd_w_out=m_odd_w_out, ln1_g=m_ln1_g, ln1_b=m_ln1_b, mlp_w1=m_mlp_w1, mlp_w2=m_mlp_w2, ln2_g=m_ln2_g, ln2_b=m_ln2_b)
    v_in = dict(even_w_in=v_even_w_in, even_sinks=v_even_sinks, even_w_out=v_even_w_out, odd_w_in=v_odd_w_in,
                odd_q_norm_g=v_odd_q_norm_g, odd_kv_norm_g=v_odd_kv_norm_g, odd_w_uq=v_odd_w_uq, odd_w_ukv=v_odd_w_ukv,
                odd_w_out=v_odd_w_out, ln1_g=v_ln1_g, ln1_b=v_ln1_b, mlp_w1=v_mlp_w1, mlp_w2=v_mlp_w2, ln2_g=v_ln2_g, ln2_b=v_ln2_b)
    order = list(weights)
    updated = {}

    def update(n, g2d, layer=0, tag=""):
        shape = weights[n].shape
        as3d = (1, math.prod(shape[:-1]), shape[-1]) if len(shape) == 2 else shape
        res = _adamw(weights[n].reshape(as3d), g2d, m_in[n].reshape(as3d), v_in[n].reshape(as3d), layer=layer,
                     carry=updated.get(n), name=f"adamw_{n}{tag}")
        updated[n] = tuple(res)
        return res[0]

    after = grad_x
    for g, group in enumerate(GRAD_GROUPS):
        for n, r in zip(group, _reduce_scatter_finish(grads_started[g], after, name=f"grads{g}")):
            if n[:-2] in ("mlp_w1", "mlp_w2"):
                after = update(n[:-2], r, layer=int(n[-1]), tag=n[-2:])
            elif n == "even_w_in":
                after = update(n, r[:, :e_w])
            elif n == "odd_w_in":
                after = update(n, r[:, :o_w])
            else:
                after = update(n, r)
        if g == len(GRAD_GROUPS) - 2:
            every = _all_gather8(_pack_small([sm[n] for n in SMALL_ORDER]), name="gather_small_grads")
            sm_sum = _sum_slabs(every.reshape(8, 8, SMALL_COLS), name="sum_small_grads")
            g_ln1g, g_ln1b, g_ln2g, g_ln2b, g_sinks, g_qn, g_kvn = _unpack_small(
                sm_sum, [(DEPTH, D_MODEL)] * 4 + [(1, A_Q_HEADS), (D_Q_RANK,), (D_KV_RANK,)])
            for n, gs in (("even_sinks", g_sinks), ("odd_q_norm_g", lax.dynamic_slice_in_dim(g_qn, chip * qn_w, qn_w)[None]),
                          ("odd_kv_norm_g", lax.dynamic_slice_in_dim(g_kvn, chip * kvn_w, kvn_w)[None]), ("ln1_g", g_ln1g),
                          ("ln1_b", g_ln1b), ("ln2_g", g_ln2g), ("ln2_b", g_ln2b)):
                after = update(n, gs)
    outs = [[updated[n][k].reshape(weights[n].shape) for n in order] for k in range(4)]
    return (loss, grad_x[None], *outs[0], *outs[1], *outs[2], *outs[3])
```

```python
import functools
import math

import jax
import jax.numpy as jnp
from jax import lax
from jax.experimental import pallas as pl
from jax.experimental.pallas import tpu as pltpu

F32 = jnp.float32
BF16 = jnp.bfloat16
MESH = pl.DeviceIdType.MESH

D_MODEL = 2048
SEQ = 2048
DEPTH = 2
HEAD_DIM = 64
A_Q_HEADS, A_KV_HEADS, A_WINDOW = 16, 2, 128
B_HEADS = 8
B_PATTERNS = ((128, 1), (512, 4), (2048, 16))
C_HEADS = 16
D_HEADS, D_Q_RANK, D_KV_RANK, D_NOPE, D_ROPE, D_V = 16, 512, 256, 64, 32, 64
ROPE_BASE = 10000.0
D_FF = 4 * D_MODEL
LN_EPS = 1e-5
RMS_EPS = 1e-6
ALPHA = (2 * DEPTH) ** 0.25
A_Q_W = A_Q_HEADS * HEAD_DIM
A_KV_W = A_KV_HEADS * HEAD_DIM
B_W = B_HEADS * HEAD_DIM
EVEN_IN = A_Q_W + 2 * A_KV_W + 3 * B_W * len(B_PATTERNS)
EVEN_OUT = A_Q_W + B_W
C_W = C_HEADS * HEAD_DIM
ODD_IN = 3 * C_W + D_Q_RANK + D_KV_RANK + D_ROPE
ADAM_LR, ADAM_B1, ADAM_B2, ADAM_EPS, ADAM_WD, ADAM_STEP = 0.001, 0.9, 0.999, 1e-08, 0.01, 10

N_CHIPS = 4
EVEN_IN_PAD = 1536
ODD_IN_PAD = 1024
ATT_BLK = 128
NEG = -1e30
V7X_VMEM_LIMIT = 48 * 1024 * 1024


def _params(*sem):
    return pltpu.CompilerParams(dimension_semantics=sem, vmem_limit_bytes=V7X_VMEM_LIMIT)


def _tile(n, cap):
    if n <= cap:
        return n
    t = cap - cap % 128
    while n % t:
        t -= 128
    return t


def _matmul(a, b, *, mode, out_dtype, name, epilogue=None, extra=None, alpha=1.0, tm=1024, tn=1024, tk=2048):
    if mode == "nn":
        M, K = a.shape
        P, _, Np = b.shape
        tm, tn, tk = _tile(M, tm), _tile(Np, tn), _tile(K, tk)
        nj = Np // tn
        grid = (M // tm, P * nj, K // tk)
        a_spec = pl.BlockSpec((tm, tk), lambda i, j, k: (i, k))
        b_spec = pl.BlockSpec((None, tk, tn), lambda i, j, k: (j // nj, k, j % nj))
        o_spec = pl.BlockSpec((tm, tn), lambda i, j, k: (i, j))
        out_shape = (M, P * Np)
        dims = (((1,), (0,)), ((), ()))
    elif mode == "nt":
        M, N = a.shape
        P, K, Np = b.shape
        tm, tn, tk = _tile(M, tm), _tile(K, tn), _tile(Np, tk)
        nkk = Np // tk
        grid = (M // tm, K // tn, P * nkk)
        a_spec = pl.BlockSpec((tm, tk), lambda i, j, k: (i, k))
        b_spec = pl.BlockSpec((None, tn, tk), lambda i, j, k: (k // nkk, j, k % nkk))
        o_spec = pl.BlockSpec((tm, tn), lambda i, j, k: (i, j))
        out_shape = (M, K)
        dims = (((1,), (1,)), ((), ()))
    else:
        raise ValueError(mode)
    n_k = grid[2]
    n_extra = 0 if extra is None else 1
    n_out = 2 if epilogue == "relu2" else 1

    def body(*refs):
        a_ref, b_ref = refs[:2]
        e_ref = refs[2] if n_extra else None
        outs = refs[2 + n_extra:2 + n_extra + n_out]
        part = lax.dot_general(a_ref[...], b_ref[...], dims, preferred_element_type=F32)

        def finish(r):
            if epilogue == "relu2":
                outs[0][...] = r.astype(outs[0].dtype)
                rp = jnp.maximum(r, 0.0)
                outs[1][...] = (rp * rp).astype(outs[1].dtype)
            elif epilogue == "drelu2":
                outs[0][...] = (r * (2.0 * jnp.maximum(e_ref[...].astype(F32), 0.0))).astype(outs[0].dtype)
            elif epilogue == "add":
                outs[0][...] = (r + alpha * e_ref[...].astype(F32)).astype(outs[0].dtype)
            else:
                outs[0][...] = r.astype(outs[0].dtype)

        if n_k == 1:
            finish(part)
        else:
            acc = refs[-1]
            k = pl.program_id(2)

            @pl.when(k == 0)
            def _():
                acc[...] = part

            @pl.when((k > 0) & (k < n_k - 1))
            def _():
                acc[...] += part

            @pl.when(k == n_k - 1)
            def _():
                finish(acc[...] + part)

    in_specs = [a_spec, b_spec] + ([o_spec] if n_extra else [])
    shapes = [jax.ShapeDtypeStruct(out_shape, out_dtype)] * n_out
    res = pl.pallas_call(
        body, name=name, grid=grid, in_specs=in_specs,
        out_specs=[o_spec] * n_out, out_shape=shapes,
        scratch_shapes=[pltpu.VMEM((tm, tn), F32)] if n_k > 1 else [],
        compiler_params=_params("parallel", "parallel", "arbitrary"),
    )(a, b, *([extra] if n_extra else []))
    return res if n_out > 1 else res[0]


def _matmul_tn(a, g, *, shards, name, tm=1024, tn=1024):
    M, K = a.shape
    P = shards
    Np = g.shape[1] // P
    tm, tn = _tile(K, tm), _tile(Np, tn)
    nj = Np // tn

    def body(a_ref, g_ref, o_ref, o16_ref):
        r = lax.dot_general(a_ref[...], g_ref[...], (((0,), (0,)), ((), ())), preferred_element_type=F32)
        o_ref[...] = r
        o16_ref[...] = r.astype(BF16)

    o_spec = pl.BlockSpec((None, tm, tn), lambda i, j: (j // nj, i, j % nj))
    return pl.pallas_call(
        body, name=name, grid=(K // tm, P * nj),
        in_specs=[pl.BlockSpec((M, tm), lambda i, j: (0, i)), pl.BlockSpec((M, tn), lambda i, j: (0, j))],
        out_specs=[o_spec, o_spec],
        out_shape=[jax.ShapeDtypeStruct((P, K, Np), F32), jax.ShapeDtypeStruct((P, K, Np), BF16)],
        compiler_params=_params("parallel", "parallel"),
    )(a, g)


def _norm_fwd(x, res, g, b, *, center, eps, alpha, name, rows=256):
    S, W = x.shape
    has_res = res is not None
    rows = _tile(S, rows)

    def body(*refs):
        x_ref = refs[0]
        r_ref = refs[1] if has_res else None
        g_ref = refs[1 + has_res]
        b_ref = refs[2 + has_res] if center else None
        y_ref, y16_ref, u_ref = refs[-3:]
        u = x_ref[...]
        if has_res:
            u = alpha * u + r_ref[...]
        if center:
            mu = jnp.mean(u, axis=1, keepdims=True)
            xc = u - mu
        else:
            xc = u
        var = jnp.mean(xc * xc, axis=1, keepdims=True)
        y = xc * lax.rsqrt(var + eps) * g_ref[...]
        if center:
            y = y + b_ref[...]
        y_ref[...] = y
        y16_ref[...] = y.astype(BF16)
        u_ref[...] = u

    row_spec = pl.BlockSpec((rows, W), lambda i: (i, 0))
    vec_spec = pl.BlockSpec((1, W), lambda i: (0, 0))
    ins = [x] + ([res] if has_res else []) + [g.reshape(1, W)] + ([b.reshape(1, W)] if center else [])
    specs = [row_spec] + ([row_spec] if has_res else []) + [vec_spec] + ([vec_spec] if center else [])
    return pl.pallas_call(
        body, name=name, grid=(S // rows,), in_specs=specs,
        out_specs=[row_spec, row_spec, row_spec],
        out_shape=[jax.ShapeDtypeStruct((S, W), F32), jax.ShapeDtypeStruct((S, W), BF16), jax.ShapeDtypeStruct((S, W), F32)],
        compiler_params=_params("parallel"),
    )(*ins)


def _norm_bwd(dy, u, g, *, center, eps, name, rows=256, after=None):
    S, W = u.shape
    rows = _tile(S, rows)

    def body(dy_ref, u_ref, g_ref, *rest):
        du_ref, du16_ref, dg_ref, db_ref = rest[-4:]
        i = pl.program_id(0)
        uu = u_ref[...]
        dyv = dy_ref[...]
        if center:
            xc = uu - jnp.mean(uu, axis=1, keepdims=True)
        else:
            xc = uu
        rstd = lax.rsqrt(jnp.mean(xc * xc, axis=1, keepdims=True) + eps)
        xhat = xc * rstd
        dxh = dyv * g_ref[...]
        c2 = jnp.mean(dxh * xhat, axis=1, keepdims=True)
        du = dxh - xhat * c2
        if center:
            du = du - jnp.mean(dxh, axis=1, keepdims=True)
        du = du * rstd
        du_ref[...] = du
        du16_ref[...] = du.astype(BF16)

        @pl.when(i == 0)
        def _():
            dg_ref[...] = jnp.zeros_like(dg_ref)
            db_ref[...] = jnp.zeros_like(db_ref)

        dg_ref[...] += jnp.sum(dyv * xhat, axis=0, keepdims=True)
        db_ref[...] += jnp.sum(dyv, axis=0, keepdims=True)

    row_spec = pl.BlockSpec((rows, W), lambda i: (i, 0))
    vec_spec = pl.BlockSpec((1, W), lambda i: (0, 0))
    return pl.pallas_call(
        body, name=name, grid=(S // rows,),
        in_specs=[row_spec, row_spec, vec_spec] + ([] if after is None else [pl.BlockSpec(memory_space=pl.ANY)]),
        out_specs=[row_spec, row_spec, vec_spec, vec_spec],
        out_shape=[jax.ShapeDtypeStruct((S, W), F32), jax.ShapeDtypeStruct((S, W), BF16),
                   jax.ShapeDtypeStruct((1, W), F32), jax.ShapeDtypeStruct((1, W), F32)],
        compiler_params=_params("arbitrary"),
    )(dy, u, g.reshape(1, W), *([] if after is None else [after]))


def _loss_head(y, target, *, name, rows=256):
    S, W = y.shape
    rows = _tile(S, rows)

    def body(y_ref, t_ref, dy_ref, l_ref):
        i = pl.program_id(0)
        e = y_ref[...] - t_ref[...]
        dy_ref[...] = e * (1.0 / W)

        @pl.when(i == 0)
        def _():
            l_ref[...] = jnp.zeros_like(l_ref)

        l_ref[...] += jnp.full(l_ref.shape, 0.5 / W * jnp.sum(e * e), F32)

    row_spec = pl.BlockSpec((rows, W), lambda i: (i, 0))
    return pl.pallas_call(
        body, name=name, grid=(S // rows,), in_specs=[row_spec, row_spec],
        out_specs=[row_spec, pl.BlockSpec((1, 128), lambda i: (0, 0))],
        out_shape=[jax.ShapeDtypeStruct((S, W), F32), jax.ShapeDtypeStruct((1, 128), F32)],
        compiler_params=_params("arbitrary"),
    )(y, target)


def _adamw(w, g, m, v, *, name, layer=0, carry=None, rows=256):
    L, R, C = w.shape
    rows = _tile(R, rows) if R % 8 == 0 else R
    c1 = 1.0 / (1.0 - ADAM_B1 ** ADAM_STEP)
    c2 = 1.0 / (1.0 - ADAM_B2 ** ADAM_STEP)

    def body(w_ref, g_ref, m_ref, v_ref, *rest):
        go_ref, d_ref, mo_ref, vo_ref = rest[-4:]
        gg = g_ref[...]
        mn = ADAM_B1 * m_ref[...] + (1.0 - ADAM_B1) * gg
        vn = ADAM_B2 * v_ref[...] + (1.0 - ADAM_B2) * (gg * gg)
        go_ref[...] = gg
        d_ref[...] = -ADAM_LR * ((mn * c1) / (jnp.sqrt(vn * c2) + ADAM_EPS) + ADAM_WD * w_ref[...])
        mo_ref[...] = mn
        vo_ref[...] = vn

    slab = pl.BlockSpec((None, rows, C), lambda i: (layer, i, 0))
    shp = jax.ShapeDtypeStruct((L, R, C), F32)
    carried = [] if carry is None else list(carry)
    return pl.pallas_call(
        body, name=name, grid=(R // rows,),
        in_specs=[slab, pl.BlockSpec((rows, C), lambda i: (i, 0)), slab, slab] + [pl.BlockSpec(memory_space=pl.ANY)] * len(carried),
        out_specs=[slab] * 4, out_shape=[shp] * 4, input_output_aliases={4 + k: k for k in range(len(carried))},
        compiler_params=_params("parallel"),
    )(w, g, m, v, *carried)


def _rope(x1, x2, cos, sin, *, name, rows=512):
    H, S, W = x1.shape
    rows = _tile(S, rows)

    def body(x1_ref, x2_ref, c_ref, s_ref, y1_ref, y2_ref):
        t1 = jnp.sum(x1_ref[...], axis=0)
        t2 = jnp.sum(x2_ref[...], axis=0)
        c = c_ref[...]
        s = s_ref[...]
        y1_ref[...] = t1 * c - t2 * s
        y2_ref[...] = t1 * s + t2 * c

    xs = pl.BlockSpec((H, rows, W), lambda i: (0, i, 0))
    ts = pl.BlockSpec((rows, W), lambda i: (i, 0))
    shp = jax.ShapeDtypeStruct((S, W), F32)
    return pl.pallas_call(
        body, name=name, grid=(S // rows,), in_specs=[xs, xs, ts, ts], out_specs=[ts, ts], out_shape=[shp, shp],
        compiler_params=_params("parallel"),
    )(x1, x2, cos, sin)


def _band_mask(blk, n, n_back):
    row = lax.broadcasted_iota(jnp.int32, (blk, 2 * blk), 0)
    col = lax.broadcasted_iota(jnp.int32, (blk, 2 * blk), 1)
    rel = blk + row - col
    return rel, (rel >= 0) & (rel <= n_back) & ((col >= blk) | (n >= 1))


def _band_fwd(q, k, v, slopes, sinks, *, hps, nb_seq, n_back, scale, dist_scale, name):
    blk = ATT_BLK
    Hq, T, d = q.shape
    Hk = k.shape[0]
    group = Hq // Hk
    kps = max(hps // group, 1)
    use_sink = sinks is not None

    def body(*refs):
        q_ref, k_ref, v_ref, slope_ref = refs[:4]
        sink_ref = refs[4] if use_sink else None
        o_ref, lse_ref = refs[-2:]
        i = pl.program_id(1)
        start = pl.multiple_of(i * blk, blk)
        rel, valid = _band_mask(blk, i % nb_seq, n_back)
        relf = rel.astype(F32)
        for hh in range(hps):
            h = pl.program_id(0) * hps + hh
            kk = k_ref[hh // group, pl.ds(start, 2 * blk), :]
            vv = v_ref[hh // group, pl.ds(start, 2 * blk), :]
            s = lax.dot_general(q_ref[hh], kk, (((1,), (1,)), ((), ())), preferred_element_type=F32) * scale
            s = jnp.where(valid, s - (slope_ref[h] * dist_scale) * relf, NEG)
            m = jnp.max(s, axis=1, keepdims=True)
            if use_sink:
                m = jnp.maximum(m, sink_ref[h])
            p = jnp.where(valid, jnp.exp(s - m), 0.0)
            l = jnp.sum(p, axis=1, keepdims=True)
            if use_sink:
                l = l + jnp.exp(sink_ref[h] - m)
            o_ref[hh] = jnp.dot(p.astype(BF16), vv, preferred_element_type=F32) / l
            lse_ref[hh] = m + jnp.log(l)

    smem = pl.BlockSpec(memory_space=pltpu.SMEM)
    qs = pl.BlockSpec((hps, blk, d), lambda g, i: (g, i, 0))
    ks = pl.BlockSpec((kps, T + blk, d), lambda g, i: (g, 0, 0))
    ins = [q, k, v, slopes] + ([sinks] if use_sink else [])
    return pl.pallas_call(
        body, name=name, grid=(Hq // hps, T // blk), in_specs=[qs, ks, ks, smem] + ([smem] if use_sink else []),
        out_specs=[qs, pl.BlockSpec((hps, blk, 1), lambda g, i: (g, i, 0))],
        out_shape=[jax.ShapeDtypeStruct((Hq, T, d), F32), jax.ShapeDtypeStruct((Hq, T, 1), F32)],
        compiler_params=_params("parallel", "parallel"),
    )(*ins)


def _band_bwd(q, k, v, o, do, lse, dlse, slopes, sinks, *, hps, nb_seq, n_back, scale, dist_scale, name):
    blk = ATT_BLK
    Hq, T, d = q.shape
    Hk = k.shape[0]
    group = Hq // Hk
    kps = max(hps // group, 1)
    use_sink, use_dlse = sinks is not None, dlse is not None

    def body(*refs):
        q_ref, k_ref, v_ref, o_ref, do_ref, lse_ref = refs[:6]
        pos = 6
        dlse_ref = refs[pos] if use_dlse else None
        pos += use_dlse
        slope_ref = refs[pos]
        pos += 1
        sink_ref = refs[pos] if use_sink else None
        pos += use_sink
        dq_ref, dk_ref, dv_ref = refs[pos:pos + 3]
        dsink_ref = refs[pos + 3] if use_sink else None
        i = pl.program_id(1)
        start = pl.multiple_of(i * blk, blk)
        rel, valid = _band_mask(blk, i % nb_seq, n_back)
        relf = rel.astype(F32)

        @pl.when(i == 0)
        def _():
            dk_ref[...] = jnp.zeros_like(dk_ref)
            dv_ref[...] = jnp.zeros_like(dv_ref)
            if use_sink:
                dsink_ref[...] = jnp.zeros_like(dsink_ref)

        for kh in range(kps):
            dk_acc = jnp.zeros((2 * blk, d), F32)
            dv_acc = jnp.zeros((2 * blk, d), F32)
            kk = k_ref[kh, pl.ds(start, 2 * blk), :]
            vv = v_ref[kh, pl.ds(start, 2 * blk), :]
            for hh in range(kh * min(group, hps), (kh + 1) * min(group, hps)):
                h = pl.program_id(0) * hps + hh
                qb = q_ref[hh]
                dob = do_ref[hh]
                do16 = dob.astype(BF16)
                lse = lse_ref[hh]
                c = -jnp.sum(dob * o_ref[hh], axis=1, keepdims=True)
                if use_dlse:
                    c = c + dlse_ref[hh]
                s = lax.dot_general(qb, kk, (((1,), (1,)), ((), ())), preferred_element_type=F32) * scale
                s = jnp.where(valid, s - (slope_ref[h] * dist_scale) * relf, NEG)
                p = jnp.where(valid, jnp.exp(s - lse), 0.0)
                dp = lax.dot_general(do16, vv, (((1,), (1,)), ((), ())), preferred_element_type=F32)
                ds16 = (p * (dp + c) * scale).astype(BF16)
                dq_ref[hh] = jnp.dot(ds16, kk, preferred_element_type=F32)
                dk_acc = dk_acc + lax.dot_general(ds16, qb, (((0,), (0,)), ((), ())), preferred_element_type=F32)
                dv_acc = dv_acc + lax.dot_general(p.astype(BF16), do16, (((0,), (0,)), ((), ())), preferred_element_type=F32)
                if use_sink:
                    dsink_ref[hh] += jnp.full((1, 128), jnp.sum(jnp.exp(sink_ref[h] - lse) * c), F32)
            dk_ref[kh, pl.ds(start, 2 * blk), :] += dk_acc
            dv_ref[kh, pl.ds(start, 2 * blk), :] += dv_acc

    smem = pl.BlockSpec(memory_space=pltpu.SMEM)
    qs = pl.BlockSpec((hps, blk, d), lambda g, i: (g, i, 0))
    ls = pl.BlockSpec((hps, blk, 1), lambda g, i: (g, i, 0))
    ks = pl.BlockSpec((kps, T + blk, d), lambda g, i: (g, 0, 0))
    in_specs = [qs, ks, ks, qs, qs, ls] + ([ls] if use_dlse else []) + [smem] + ([smem] if use_sink else [])
    ins = [q, k, v, o, do, lse] + ([dlse] if use_dlse else []) + [slopes] + ([sinks] if use_sink else [])
    out_specs = [qs, ks, ks]
    out_shape = [jax.ShapeDtypeStruct((Hq, T, d), F32), jax.ShapeDtypeStruct((Hk, T + blk, d), F32),
                 jax.ShapeDtypeStruct((Hk, T + blk, d), F32)]
    if use_sink:
        out_specs.append(pl.BlockSpec((hps, 1, 128), lambda g, i: (g, 0, 0)))
        out_shape.append(jax.ShapeDtypeStruct((Hq, 1, 128), F32))
    return pl.pallas_call(
        body, name=name, grid=(Hq // hps, T // blk), in_specs=in_specs, out_specs=out_specs, out_shape=out_shape,
        compiler_params=_params("parallel", "arbitrary"),
    )(*ins)


def _diagonal_mask(blk):
    return lax.broadcasted_iota(jnp.int32, (blk, blk), 0) >= lax.broadcasted_iota(jnp.int32, (blk, blk), 1)


def _causal_fwd(q, k, v, *, blk, hps, scale, name):
    H, T, dqk = q.shape
    dv = v.shape[2]

    def body(q_ref, k_ref, v_ref, o_ref, lse_ref):
        n = pl.program_id(1)
        qs = [q_ref[hh] for hh in range(hps)]

        def block(j, carry, valid):
            start = pl.multiple_of(j * blk, blk)
            out = []
            for hh in range(hps):
                m, l, acc = carry[hh]
                kb = k_ref[hh, pl.ds(start, blk), :]
                vb = v_ref[hh, pl.ds(start, blk), :]
                s = lax.dot_general(qs[hh], kb, (((1,), (1,)), ((), ())), preferred_element_type=F32) * scale
                if valid is not None:
                    s = jnp.where(valid, s, NEG)
                m_new = jnp.maximum(m, jnp.max(s, axis=1, keepdims=True))
                p = _keep(valid, jnp.exp(s - m_new))
                a = jnp.exp(m - m_new)
                out.append((m_new, a * l + jnp.sum(p, axis=1, keepdims=True),
                            a * acc + jnp.dot(p.astype(BF16), vb, preferred_element_type=F32)))
            return tuple(out)

        init = tuple((jnp.full((blk, 1), NEG, F32), jnp.zeros((blk, 1), F32), jnp.zeros((blk, dv), F32)) for _ in range(hps))
        res = block(n, lax.fori_loop(0, n, lambda j, carry: block(j, carry, None), init), _diagonal_mask(blk))
        for hh in range(hps):
            m, l, acc = res[hh]
            o_ref[hh] = acc / l
            lse_ref[hh] = m + jnp.log(l)

    qs_ = pl.BlockSpec((hps, blk, dqk), lambda g, i: (g, i, 0))
    return pl.pallas_call(
        body, name=name, grid=(H // hps, T // blk),
        in_specs=[qs_, pl.BlockSpec((hps, T, dqk), lambda g, i: (g, 0, 0)), pl.BlockSpec((hps, T, dv), lambda g, i: (g, 0, 0))],
        out_specs=[pl.BlockSpec((hps, blk, dv), lambda g, i: (g, i, 0)), pl.BlockSpec((hps, blk, 1), lambda g, i: (g, i, 0))],
        out_shape=[jax.ShapeDtypeStruct((H, T, dv), F32), jax.ShapeDtypeStruct((H, T, 1), F32)],
        compiler_params=_params("parallel", "parallel"),
    )(q, k, v)


def _causal_bwd(q, k, v, o, do, lse, *, blk, hps, scale, name):
    H, T, dqk = q.shape
    dv = v.shape[2]

    def body(q_ref, k_ref, v_ref, o_ref, do_ref, lse_ref, dq_ref, dk_ref, dv_ref):
        n = pl.program_id(1)

        @pl.when(n == 0)
        def _():
            dk_ref[...] = jnp.zeros_like(dk_ref)
            dv_ref[...] = jnp.zeros_like(dv_ref)

        qs = [q_ref[hh] for hh in range(hps)]
        do16 = [do_ref[hh].astype(BF16) for hh in range(hps)]
        lses = [lse_ref[hh] for hh in range(hps)]
        cs = [-jnp.sum(do_ref[hh] * o_ref[hh], axis=1, keepdims=True) for hh in range(hps)]

        def block(j, dqs, valid):
            start = pl.multiple_of(j * blk, blk)
            out = []
            for hh in range(hps):
                kb = k_ref[hh, pl.ds(start, blk), :]
                vb = v_ref[hh, pl.ds(start, blk), :]
                s = lax.dot_general(qs[hh], kb, (((1,), (1,)), ((), ())), preferred_element_type=F32) * scale
                if valid is not None:
                    s = jnp.where(valid, s, NEG)
                p = _keep(valid, jnp.exp(s - lses[hh]))
                dp = lax.dot_general(do16[hh], vb, (((1,), (1,)), ((), ())), preferred_element_type=F32)
                ds16 = (p * (dp + cs[hh]) * scale).astype(BF16)
                dk_ref[hh, pl.ds(start, blk), :] += lax.dot_general(ds16, qs[hh], (((0,), (0,)), ((), ())), preferred_element_type=F32)
                dv_ref[hh, pl.ds(start, blk), :] += lax.dot_general(p.astype(BF16), do16[hh], (((0,), (0,)), ((), ())),
                                                                    preferred_element_type=F32)
                out.append(dqs[hh] + jnp.dot(ds16, kb, preferred_element_type=F32))
            return tuple(out)

        init = tuple(jnp.zeros((blk, dqk), F32) for _ in range(hps))
        res = block(n, lax.fori_loop(0, n, lambda j, dqs: block(j, dqs, None), init), _diagonal_mask(blk))
        for hh in range(hps):
            dq_ref[hh] = res[hh]

    qs_ = pl.BlockSpec((hps, blk, dqk), lambda g, i: (g, i, 0))
    os_ = pl.BlockSpec((hps, blk, dv), lambda g, i: (g, i, 0))
    ls_ = pl.BlockSpec((hps, blk, 1), lambda g, i: (g, i, 0))
    ks_ = pl.BlockSpec((hps, T, dqk), lambda g, i: (g, 0, 0))
    vs_ = pl.BlockSpec((hps, T, dv), lambda g, i: (g, 0, 0))
    return pl.pallas_call(
        body, name=name, grid=(H // hps, T // blk), in_specs=[qs_, ks_, vs_, os_, os_, ls_], out_specs=[qs_, ks_, vs_],
        out_shape=[jax.ShapeDtypeStruct((H, T, dqk), F32), jax.ShapeDtypeStruct((H, T, dqk), F32),
                   jax.ShapeDtypeStruct((H, T, dv), F32)],
        compiler_params=_params("parallel", "arbitrary"),
    )(q, k, v, o, do, lse)


def _band_geometry(blk, nb_seq, n_prev):
    def geo(i):
        seq = i // nb_seq
        n = i % nb_seq
        return seq, n, jnp.maximum(n - n_prev, 0)
    return geo


def _attn_fwd(q, k, v, slopes, sinks, *, blk, nb_seq, n_prev, n_back, scale, dist_scale, name):
    Hq, T, dqk = q.shape
    Hk, _, dv = v.shape
    group = Hq // Hk
    use_bias, use_sink = slopes is not None, sinks is not None
    geo = _band_geometry(blk, nb_seq, n_prev)

    def body(*refs):
        q_ref, k_ref, v_ref = refs[:3]
        slope_ref = refs[3] if use_bias else None
        sink_ref = refs[3 + use_bias] if use_sink else None
        o_ref, lse_ref = refs[-2:]
        h = pl.program_id(0)
        seq, n, lo = geo(pl.program_id(1))
        qb = q_ref[...]
        diff = lax.broadcasted_iota(jnp.int32, (blk, blk), 0) - lax.broadcasted_iota(jnp.int32, (blk, blk), 1)
        if use_sink:
            m0 = jnp.full((blk, 1), sink_ref[h], F32)
            l0 = jnp.ones((blk, 1), F32)
        else:
            m0 = jnp.full((blk, 1), NEG, F32)
            l0 = jnp.zeros((blk, 1), F32)

        def step(j, carry):
            m, l, acc = carry
            start = pl.multiple_of((seq * nb_seq + j) * blk, blk)
            kb = k_ref[pl.ds(start, blk), :]
            vb = v_ref[pl.ds(start, blk), :]
            s = lax.dot_general(qb, kb, (((1,), (1,)), ((), ())), preferred_element_type=F32) * scale
            rel = (n - j) * blk + diff
            valid = (rel >= 0) & (rel <= n_back)
            if use_bias:
                s = s - (slope_ref[h] * dist_scale) * rel.astype(F32)
            s = jnp.where(valid, s, NEG)
            m_new = jnp.maximum(m, jnp.max(s, axis=1, keepdims=True))
            p = jnp.where(valid, jnp.exp(s - m_new), 0.0)
            a = jnp.exp(m - m_new)
            l = a * l + jnp.sum(p, axis=1, keepdims=True)
            acc = a * acc + jnp.dot(p.astype(BF16), vb, preferred_element_type=F32)
            return m_new, l, acc

        m, l, acc = lax.fori_loop(lo, n + 1, step, (m0, l0, jnp.zeros((blk, dv), F32)))
        o_ref[...] = acc / l
        lse_ref[...] = m + jnp.log(l)

    smem = pl.BlockSpec(memory_space=pltpu.SMEM)
    in_specs = [pl.BlockSpec((None, blk, dqk), lambda h, i: (h, i, 0)),
                pl.BlockSpec((None, T, dqk), lambda h, i: (h // group, 0, 0)),
                pl.BlockSpec((None, T, dv), lambda h, i: (h // group, 0, 0))]
    ins = [q, k, v]
    if use_bias:
        in_specs.append(smem)
        ins.append(slopes)
    if use_sink:
        in_specs.append(smem)
        ins.append(sinks)
    return pl.pallas_call(
        body, name=name, grid=(Hq, T // blk), in_specs=in_specs,
        out_specs=[pl.BlockSpec((None, blk, dv), lambda h, i: (h, i, 0)), pl.BlockSpec((None, blk, 1), lambda h, i: (h, i, 0))],
        out_shape=[jax.ShapeDtypeStruct((Hq, T, dv), F32), jax.ShapeDtypeStruct((Hq, T, 1), F32)],
        compiler_params=_params("parallel", "parallel"),
    )(*ins)


def _attn_bwd(q, k, v, o, do, lse, dlse, slopes, sinks, *, blk, nb_seq, n_prev, n_back, scale, dist_scale, name):
    Hq, T, dqk = q.shape
    Hk, _, dv = v.shape
    group = Hq // Hk
    use_bias, use_sink, use_dlse = slopes is not None, sinks is not None, dlse is not None
    geo = _band_geometry(blk, nb_seq, n_prev)

    def body(*refs):
        q_ref, k_ref, v_ref, o_ref, do_ref, lse_ref = refs[:6]
        pos = 6
        dlse_ref = refs[pos] if use_dlse else None
        pos += use_dlse
        slope_ref = refs[pos] if use_bias else None
        pos += use_bias
        sink_ref = refs[pos] if use_sink else None
        pos += use_sink
        dq_ref, dk_ref, dv_ref = refs[pos:pos + 3]
        dsink_ref = refs[pos + 3] if use_sink else None
        h = pl.program_id(0)
        i = pl.program_id(1)
        seq, n, lo = geo(i)
        qb = q_ref[...]
        dob = do_ref[...]
        do16 = dob.astype(BF16)
        lse = lse_ref[...]
        c = -jnp.sum(dob * o_ref[...], axis=1, keepdims=True)
        if use_dlse:
            c = c + dlse_ref[...]
        diff = lax.broadcasted_iota(jnp.int32, (blk, blk), 0) - lax.broadcasted_iota(jnp.int32, (blk, blk), 1)

        @pl.when((h % group == 0) & (i == 0))
        def _():
            dk_ref[...] = jnp.zeros_like(dk_ref)
            dv_ref[...] = jnp.zeros_like(dv_ref)

        def step(j, dq):
            start = pl.multiple_of((seq * nb_seq + j) * blk, blk)
            kb = k_ref[pl.ds(start, blk), :]
            vb = v_ref[pl.ds(start, blk), :]
            s = lax.dot_general(qb, kb, (((1,), (1,)), ((), ())), preferred_element_type=F32) * scale
            rel = (n - j) * blk + diff
            valid = (rel >= 0) & (rel <= n_back)
            if use_bias:
                s = s - (slope_ref[h] * dist_scale) * rel.astype(F32)
            p = jnp.where(valid, jnp.exp(jnp.where(valid, s, NEG) - lse), 0.0)
            dp = lax.dot_general(do16, vb, (((1,), (1,)), ((), ())), preferred_element_type=F32)
            ds16 = (p * (dp + c) * scale).astype(BF16)
            dk_ref[pl.ds(start, blk), :] += lax.dot_general(ds16, qb, (((0,), (0,)), ((), ())), preferred_element_type=F32)
            dv_ref[pl.ds(start, blk), :] += lax.dot_general(p.astype(BF16), do16, (((0,), (0,)), ((), ())), preferred_element_type=F32)
            return dq + jnp.dot(ds16, kb, preferred_element_type=F32)

        dq_ref[...] = lax.fori_loop(lo, n + 1, step, jnp.zeros((blk, dqk), F32))
        if use_sink:
            @pl.when(i == 0)
            def _():
                dsink_ref[...] = jnp.zeros_like(dsink_ref)

            dsink_ref[...] += jnp.full(dsink_ref.shape, jnp.sum(jnp.exp(sink_ref[h] - lse) * c), F32)

    smem = pl.BlockSpec(memory_space=pltpu.SMEM)
    qs = pl.BlockSpec((None, blk, dqk), lambda h, i: (h, i, 0))
    os_ = pl.BlockSpec((None, blk, dv), lambda h, i: (h, i, 0))
    ls = pl.BlockSpec((None, blk, 1), lambda h, i: (h, i, 0))
    ks = pl.BlockSpec((None, T, dqk), lambda h, i: (h // group, 0, 0))
    vs = pl.BlockSpec((None, T, dv), lambda h, i: (h // group, 0, 0))
    in_specs = [qs, ks, vs, os_, os_, ls]
    ins = [q, k, v, o, do, lse]
    if use_dlse:
        in_specs.append(ls)
        ins.append(dlse)
    if use_bias:
        in_specs.append(smem)
        ins.append(slopes)
    if use_sink:
        in_specs.append(smem)
        ins.append(sinks)
    out_specs = [qs, ks, vs]
    out_shape = [jax.ShapeDtypeStruct((Hq, T, dqk), F32), jax.ShapeDtypeStruct((Hk, T, dqk), F32),
                 jax.ShapeDtypeStruct((Hk, T, dv), F32)]
    if use_sink:
        out_specs.append(pl.BlockSpec((None, 1, 128), lambda h, i: (h, 0, 0)))
        out_shape.append(jax.ShapeDtypeStruct((Hq, 1, 128), F32))
    return pl.pallas_call(
        body, name=name, grid=(Hq, T // blk), in_specs=in_specs, out_specs=out_specs, out_shape=out_shape,
        compiler_params=_params("arbitrary", "arbitrary"),
    )(*ins)


def _tri_sum(x, upper):
    hi = x.astype(BF16)
    lo = (x - hi.astype(F32)).astype(BF16)
    return jnp.dot(hi, upper, preferred_element_type=F32) + jnp.dot(lo, upper, preferred_element_type=F32)


def _keep(mask, x):
    return x if mask is None else jnp.where(mask, x, 0.0)


def _stick_terms(qb, kb, scale, strict):
    z = lax.dot_general(qb, kb, (((1,), (1,)), ((), ())), preferred_element_type=F32) * scale
    e = jnp.exp(-jnp.abs(z))
    lb = jnp.minimum(z, 0.0) - jnp.log(1.0 + e)
    return z, e, lb, _keep(strict, lb - z)


def _stick_fwd(q, k, v, *, blk, hps, scale, name):
    H, T, dh = q.shape

    def body(q_ref, k_ref, v_ref, o_ref, tot_ref):
        n = pl.program_id(1)
        qs = [q_ref[hh] for hh in range(hps)]
        r_i = lax.broadcasted_iota(jnp.int32, (blk, blk), 0)
        c_i = lax.broadcasted_iota(jnp.int32, (blk, blk), 1)
        upper = (r_i > c_i).astype(BF16)

        def block(j, carry, strict):
            start = pl.multiple_of(j * blk, blk)
            out = []
            for hh in range(hps):
                run, acc = carry[hh]
                kb = k_ref[hh, pl.ds(start, blk), :]
                vb = v_ref[hh, pl.ds(start, blk), :]
                _, _, lb, lk = _stick_terms(qs[hh], kb, scale, strict)
                w = _keep(strict, jnp.exp(lb + run + _tri_sum(lk, upper)))
                out.append((run + jnp.sum(lk, axis=1, keepdims=True), acc + jnp.dot(w.astype(BF16), vb, preferred_element_type=F32)))
            return tuple(out)

        init = tuple((jnp.zeros((blk, 1), F32), jnp.zeros((blk, dh), F32)) for _ in range(hps))
        res = lax.fori_loop(1, n + 1, lambda t, carry: block(n - t, carry, None), block(n, init, r_i > c_i))
        for hh in range(hps):
            tot_ref[hh], o_ref[hh] = res[hh]

    qs_ = pl.BlockSpec((hps, blk, dh), lambda g, i: (g, i, 0))
    ks_ = pl.BlockSpec((hps, T, dh), lambda g, i: (g, 0, 0))
    ts_ = pl.BlockSpec((hps, blk, 1), lambda g, i: (g, i, 0))
    return pl.pallas_call(
        body, name=name, grid=(H // hps, T // blk), in_specs=[qs_, ks_, ks_], out_specs=[qs_, ts_],
        out_shape=[jax.ShapeDtypeStruct((H, T, dh), F32), jax.ShapeDtypeStruct((H, T, 1), F32)],
        compiler_params=_params("parallel", "parallel"),
    )(q, k, v)


def _stick_bwd(q, k, v, tot, do, *, blk, hps, scale, name):
    H, T, dh = q.shape

    def body(q_ref, k_ref, v_ref, tot_ref, do_ref, dq_ref, dk_ref, dv_ref):
        n = pl.program_id(1)
        qs = [q_ref[hh] for hh in range(hps)]
        do16 = [do_ref[hh].astype(BF16) for hh in range(hps)]
        tots = [tot_ref[hh] for hh in range(hps)]
        r_i = lax.broadcasted_iota(jnp.int32, (blk, blk), 0)
        c_i = lax.broadcasted_iota(jnp.int32, (blk, blk), 1)
        upto = (r_i <= c_i).astype(BF16)
        below = (r_i < c_i).astype(BF16)

        @pl.when(n == 0)
        def _():
            dk_ref[...] = jnp.zeros_like(dk_ref)
            dv_ref[...] = jnp.zeros_like(dv_ref)

        def block(j, carry, strict):
            start = pl.multiple_of(j * blk, blk)
            out = []
            for hh in range(hps):
                run, grun, dq = carry[hh]
                kb = k_ref[hh, pl.ds(start, blk), :]
                vb = v_ref[hh, pl.ds(start, blk), :]
                z, e, lb, lk = _stick_terms(qs[hh], kb, scale, strict)
                w = _keep(strict, jnp.exp(lb + tots[hh] - (run + _tri_sum(lk, upto))))
                dw = lax.dot_general(do16[hh], vb, (((1,), (1,)), ((), ())), preferred_element_type=F32)
                g = w * dw
                before = grun + _tri_sum(g, below)
                inv = 1.0 / (1.0 + e)
                sig = jnp.where(z >= 0, inv, e * inv)
                dz16 = (_keep(strict, g * (1.0 - sig) - sig * before) * scale).astype(BF16)
                dk_ref[hh, pl.ds(start, blk), :] += lax.dot_general(dz16, qs[hh], (((0,), (0,)), ((), ())), preferred_element_type=F32)
                dv_ref[hh, pl.ds(start, blk), :] += lax.dot_general(w.astype(BF16), do16[hh], (((0,), (0,)), ((), ())),
                                                                    preferred_element_type=F32)
                out.append((run + jnp.sum(lk, axis=1, keepdims=True), grun + jnp.sum(g, axis=1, keepdims=True),
                            dq + jnp.dot(dz16, kb, preferred_element_type=F32)))
            return tuple(out)

        zero = jnp.zeros((blk, 1), F32)
        init = tuple((zero, zero, jnp.zeros((blk, dh), F32)) for _ in range(hps))
        res = block(n, lax.fori_loop(0, n, lambda j, carry: block(j, carry, None), init), r_i > c_i)
        for hh in range(hps):
            dq_ref[hh] = res[hh][2]

    qs_ = pl.BlockSpec((hps, blk, dh), lambda g, i: (g, i, 0))
    ks_ = pl.BlockSpec((hps, T, dh), lambda g, i: (g, 0, 0))
    ts_ = pl.BlockSpec((hps, blk, 1), lambda g, i: (g, i, 0))
    shp = jax.ShapeDtypeStruct((H, T, dh), F32)
    return pl.pallas_call(
        body, name=name, grid=(H // hps, T // blk), in_specs=[qs_, ks_, ks_, ts_, qs_], out_specs=[qs_, ks_, ks_],
        out_shape=[shp, shp, shp],
        compiler_params=_params("parallel", "arbitrary"),
    )(q, k, v, tot, do)


def _mix_fwd(outs, lses, *, name, rows=512):
    H, T, dh = outs[0].shape
    rows = _tile(T, rows)

    def body(o0, o1, o2, l0, l1, l2, y_ref):
        ls = [l0[...], l1[...], l2[...]]
        mx = jnp.maximum(jnp.maximum(ls[0], ls[1]), ls[2])
        es = [jnp.exp(x - mx) for x in ls]
        den = es[0] + es[1] + es[2]
        y_ref[...] = (es[0] * o0[...] + es[1] * o1[...] + es[2] * o2[...]) / den

    os_ = pl.BlockSpec((None, rows, dh), lambda h, i: (h, i, 0))
    ls_ = pl.BlockSpec((None, rows, 1), lambda h, i: (h, i, 0))
    return pl.pallas_call(
        body, name=name, grid=(H, T // rows), in_specs=[os_] * 3 + [ls_] * 3, out_specs=os_,
        out_shape=jax.ShapeDtypeStruct((H, T, dh), F32),
        compiler_params=_params("parallel", "parallel"),
    )(*outs, *lses)


def _mix_bwd(outs, lses, dy, *, name, rows=512):
    H, T, dh = outs[0].shape
    rows = _tile(T, rows)

    def body(o0, o1, o2, l0, l1, l2, dy_ref, d0, d1, d2, g0, g1, g2):
        ls = [l0[...], l1[...], l2[...]]
        mx = jnp.maximum(jnp.maximum(ls[0], ls[1]), ls[2])
        es = [jnp.exp(x - mx) for x in ls]
        den = es[0] + es[1] + es[2]
        mix = [e / den for e in es]
        dyv = dy_ref[...]
        dm = [jnp.sum(dyv * o[...], axis=1, keepdims=True) for o in (o0, o1, o2)]
        avg = mix[0] * dm[0] + mix[1] * dm[1] + mix[2] * dm[2]
        for d_ref, g_ref, w, t in zip((d0, d1, d2), (g0, g1, g2), mix, dm):
            d_ref[...] = w * dyv
            g_ref[...] = w * (t - avg)

    os_ = pl.BlockSpec((None, rows, dh), lambda h, i: (h, i, 0))
    ls_ = pl.BlockSpec((None, rows, 1), lambda h, i: (h, i, 0))
    so = jax.ShapeDtypeStruct((H, T, dh), F32)
    sl = jax.ShapeDtypeStruct((H, T, 1), F32)
    res = pl.pallas_call(
        body, name=name, grid=(H, T // rows), in_specs=[os_] * 3 + [ls_] * 3 + [os_], out_specs=[os_] * 3 + [ls_] * 3,
        out_shape=[so] * 3 + [sl] * 3,
        compiler_params=_params("parallel", "parallel"),
    )(*outs, *lses, dy)
    return res[:3], res[3:]


def _position():
    return lax.axis_index("x"), lax.axis_index("y"), lax.axis_index("c")


def _other_chips(x, y):
    return [(1 - x, y), (x, 1 - y), (1 - x, 1 - y)]


HBM_SPEC = pl.BlockSpec(memory_space=pltpu.HBM)
SEM_SPEC = pl.BlockSpec(memory_space=pltpu.SEMAPHORE)
ANY_SPEC = pl.BlockSpec(memory_space=pl.ANY)
SPLIT_COPY = pltpu.CompilerParams(has_side_effects=pltpu.SideEffectType.DATAFLOW_SIDE_EFFECTING)


def _in_hbm(a):
    return pltpu.with_memory_space_constraint(a, pltpu.HBM)


def _chip_copies(srcs, lands, send_sems, recv_sems, src_of, dst_of):
    x, y, c = _position()
    return [pltpu.make_async_remote_copy(
        src_ref=src_of(srcs[w], chip, c), dst_ref=dst_of(lands[w], j, chip, (x, y), c), send_sem=send_sems[3 * w + j],
        recv_sem=recv_sems[3 * w + j], device_id=(chip[0], chip[1], c), device_id_type=MESH)
        for w in range(len(srcs)) for j, chip in enumerate(_other_chips(x, y))]


def _copies_start(srcs, land_shapes, after, src_of, dst_of, *, name):
    n = len(srcs)
    m = 3 * n

    def body(*refs):
        src_refs, land_refs = refs[:n], refs[n:2 * n]
        send_sems, recv_sems = refs[2 * n + 1:2 * n + 1 + m], refs[2 * n + 1 + m:2 * n + 1 + 2 * m]
        token = refs[-1]
        for cp in _chip_copies(src_refs, land_refs, send_sems, recv_sems, src_of, dst_of):
            cp.start()
        token[...] = jnp.zeros_like(token)

    lands = [_in_hbm(lax.empty(s.shape, s.dtype)) for s in land_shapes]
    out_shape = ([pltpu.SemaphoreType.DMA(())] * (2 * m)
                 + [pltpu.HBM(a.shape, a.dtype) for a in srcs] + [pltpu.HBM(s.shape, s.dtype) for s in land_shapes]
                 + [jax.ShapeDtypeStruct((8, 128), F32)])
    res = pl.pallas_call(
        body, name=name, in_specs=[HBM_SPEC] * (2 * n) + [ANY_SPEC],
        out_specs=[SEM_SPEC] * (2 * m) + [HBM_SPEC] * (2 * n) + [pl.BlockSpec(memory_space=pltpu.VMEM)],
        out_shape=out_shape, input_output_aliases={i: 2 * m + i for i in range(2 * n)}, compiler_params=SPLIT_COPY,
    )(*[_in_hbm(a) for a in srcs], *lands, after)
    return (list(res[:m]), list(res[m:2 * m]), list(res[2 * m:2 * m + n]), list(res[2 * m + n:2 * m + 2 * n]), res[-1])


def _copies_wait(started, after, src_of, dst_of, *, name):
    send_sems, recv_sems, srcs, lands, _ = started
    n = len(srcs)
    m = 3 * n

    def body(*refs):
        src_refs, land_refs = refs[:n], refs[n:2 * n]
        send_refs, recv_refs = refs[2 * n:2 * n + m], refs[2 * n + m:2 * n + 2 * m]
        for cp in _chip_copies(src_refs, land_refs, send_refs, recv_refs, src_of, dst_of):
            cp.wait_send()
            cp.wait_recv()

    res = pl.pallas_call(
        body, name=name, in_specs=[HBM_SPEC] * (2 * n) + [SEM_SPEC] * (2 * m) + [ANY_SPEC], out_specs=[HBM_SPEC] * (2 * n),
        out_shape=[pltpu.HBM(a.shape, a.dtype) for a in srcs + lands],
        input_output_aliases={i: i for i in range(2 * n)}, compiler_params=SPLIT_COPY,
    )(*srcs, *lands, *send_sems, *recv_sems, after)
    return list(res[:n]), list(res[n:])


def _half_rows(ref, core):
    half = ref.shape[-2] // 2
    return pl.ds(core * half, half)


def _gather_src(src, chip, core):
    return src.at[_half_rows(src, core), :]


def _gather_dst(land, j, chip, me, core):
    return land.at[2 * me[0] + me[1], _half_rows(land, core), :]


def _gather_landed(land, j, chip, me, core):
    return land.at[2 * chip[0] + chip[1], _half_rows(land, core), :]


def _exchange_src(src, chip, core):
    return src.at[2 * chip[0] + chip[1]]


def _exchange_dst(land, j, chip, me, core):
    return land.at[j]


def _forward_halves(bufs, *, name):
    n = len(bufs)

    def body(*refs):
        ins, outs = refs[:n], refs[n:2 * n]
        send_sems, recv_sems = refs[2 * n:]
        x, y, c = _position()
        chips = _other_chips(x, y)

        def copy(w, j, chip, core):
            slab = _gather_landed(ins[w], j, chip, (x, y), core)
            return pltpu.make_async_remote_copy(src_ref=slab, dst_ref=_gather_landed(outs[w], j, chip, (x, y), core),
                                                send_sem=send_sems.at[w, j], recv_sem=recv_sems.at[w, j],
                                                device_id=(x, y, 1 - c), device_id_type=MESH)

        sends = [copy(w, j, chip, c) for w in range(n) for j, chip in enumerate(chips)]
        for cp in sends:
            cp.start()
        for w in range(n):
            for j, chip in enumerate(chips):
                copy(w, j, chip, 1 - c).wait_recv()
        for cp in sends:
            cp.wait_send()

    return pl.pallas_call(
        body, name=name, in_specs=[ANY_SPEC] * n, out_specs=[ANY_SPEC] * n,
        out_shape=[jax.ShapeDtypeStruct(a.shape, a.dtype) for a in bufs], input_output_aliases={w: w for w in range(n)},
        scratch_shapes=[pltpu.SemaphoreType.DMA((n, 3)), pltpu.SemaphoreType.DMA((n, 3))],
    )(*bufs)


def _swap_other_half(arrs, *, name):
    n = len(arrs)

    def body(*refs):
        ins, outs = refs[:n], refs[n:2 * n]
        send_sems, recv_sems = refs[2 * n:]
        x, y, c = _position()
        cps = []
        for w in range(n):
            half = ins[w].shape[1] // 2
            cps.append(pltpu.make_async_remote_copy(
                src_ref=ins[w].at[:, pl.ds((1 - c) * half, half), :], dst_ref=outs[w], send_sem=send_sems.at[w],
                recv_sem=recv_sems.at[w], device_id=(x, y, 1 - c), device_id_type=MESH))
            cps[-1].start()
        for cp in cps:
            cp.wait()

    hbm = pl.BlockSpec(memory_space=pl.ANY)
    return pl.pallas_call(
        body, name=name, in_specs=[hbm] * n, out_specs=[hbm] * n,
        out_shape=[jax.ShapeDtypeStruct((a.shape[0], a.shape[1] // 2, a.shape[2]), a.dtype) for a in arrs],
        scratch_shapes=[pltpu.SemaphoreType.DMA((n,)), pltpu.SemaphoreType.DMA((n,))],
    )(*arrs)


def _share_halves(bufs, *, name):
    n = len(bufs)

    def body(*refs):
        ins, outs = refs[:n], refs[n:2 * n]
        send_sems, recv_sems = refs[2 * n:]
        x, y, c = _position()
        sends = []
        for w in range(n):
            sends.append(pltpu.make_async_remote_copy(src_ref=ins[w].at[c], dst_ref=outs[w].at[c], send_sem=send_sems.at[w],
                                                      recv_sem=recv_sems.at[w], device_id=(x, y, 1 - c), device_id_type=MESH))
            sends[-1].start()
        for w in range(n):
            pltpu.make_async_remote_copy(src_ref=ins[w].at[1 - c], dst_ref=outs[w].at[1 - c], send_sem=send_sems.at[w],
                                         recv_sem=recv_sems.at[w], device_id=(x, y, 1 - c), device_id_type=MESH).wait_recv()
        for cp in sends:
            cp.wait_send()

    hbm = pl.BlockSpec(memory_space=pl.ANY)
    return pl.pallas_call(
        body, name=name, in_specs=[hbm] * n, out_specs=[hbm] * n,
        out_shape=[jax.ShapeDtypeStruct(a.shape, a.dtype) for a in bufs],
        input_output_aliases={w: w for w in range(n)},
        scratch_shapes=[pltpu.SemaphoreType.DMA((n,)), pltpu.SemaphoreType.DMA((n,))],
    )(*bufs)


def _all_gather8(v, *, name):
    m, n = v.shape

    def body(v_ref, out_ref, send_sems, recv_sems, local_sem):
        x, y, c = _position()
        me, sibling = (x, y, c), (x, y, 1 - c)
        chips = _other_chips(x, y)

        def rows(px, py, pc):
            return out_ref.at[pl.ds((4 * px + 2 * py + pc) * m, m), :]

        def copy(k, block, to, src=None):
            return pltpu.make_async_remote_copy(src_ref=rows(*block) if src is None else src, dst_ref=rows(*block),
                                                send_sem=send_sems.at[k], recv_sem=recv_sems.at[k], device_id=to, device_id_type=MESH)

        mine = pltpu.make_async_copy(v_ref, rows(*me), local_sem)
        mine.start()
        first = [copy(0, me, sibling, src=v_ref)]
        first += [copy(1 + j, me, (*chip, c), src=v_ref) for j, chip in enumerate(chips)]
        for cp in first:
            cp.start()
        passed = [copy(4 + j, (*chip, c), sibling) for j, chip in enumerate(chips)]
        for j, chip in enumerate(chips):
            copy(1 + j, (*chip, c), me).wait_recv()
            passed[j].start()
        copy(0, sibling, me).wait_recv()
        for j, chip in enumerate(chips):
            copy(4 + j, (*chip, 1 - c), me).wait_recv()
        for cp in first + passed:
            cp.wait_send()
        mine.wait()

    vmem = pl.BlockSpec(memory_space=pltpu.VMEM)
    return pl.pallas_call(
        body, name=name, in_specs=[vmem], out_specs=vmem, out_shape=jax.ShapeDtypeStruct((8 * m, n), v.dtype),
        scratch_shapes=[pltpu.SemaphoreType.DMA((7,)), pltpu.SemaphoreType.DMA((7,)), pltpu.SemaphoreType.DMA],
    )(v)


def _add_half(full, part, core, *, name, rows=256):
    P, R, C = full.shape
    half = R // 2
    rows = _tile(half, rows)
    nb = half // rows

    def body(core_ref, a_ref, b_ref, o_ref):
        o_ref[...] = (a_ref[...] + b_ref[...].astype(F32)).astype(BF16)

    spec = pl.BlockSpec((None, rows, C), lambda p, i, core_ref: (p, i, 0))
    grid_spec = pltpu.PrefetchScalarGridSpec(
        num_scalar_prefetch=1, grid=(P, nb),
        in_specs=[pl.BlockSpec((None, rows, C), lambda p, i, core_ref: (p, core_ref[0] * nb + i, 0)), spec], out_specs=spec)
    return pl.pallas_call(
        body, name=name, grid_spec=grid_spec, out_shape=jax.ShapeDtypeStruct((P, half, C), BF16),
        compiler_params=_params("parallel", "parallel"),
    )(core, full, part)


def _sum_parts(own, landed, where, *, name, rows=256):
    _, R, C = own.shape
    rows = _tile(R, rows)

    def body(where_ref, own_ref, land_ref, o_ref):
        acc = own_ref[...].astype(F32)
        for j in range(3):
            acc = acc + land_ref[j].astype(F32)
        o_ref[...] = acc

    grid_spec = pltpu.PrefetchScalarGridSpec(
        num_scalar_prefetch=1, grid=(R // rows,),
        in_specs=[pl.BlockSpec((None, rows, C), lambda i, where_ref: (where_ref[0], i, 0)),
                  pl.BlockSpec((3, rows, C), lambda i, where_ref: (0, i, 0))],
        out_specs=pl.BlockSpec((None, rows, C), lambda i, where_ref: (where_ref[1], i, 0)))
    return pl.pallas_call(
        body, name=name, grid_spec=grid_spec, out_shape=jax.ShapeDtypeStruct((2, R, C), F32),
        compiler_params=_params("parallel"),
    )(where, own, landed)


def _sum_slabs(a, *, name, rows=256):
    P, R, C = a.shape
    rows = _tile(R, rows)

    def body(a_ref, o_ref):
        acc = a_ref[0].astype(F32)
        for p in range(1, P):
            acc = acc + a_ref[p].astype(F32)
        o_ref[...] = acc

    return pl.pallas_call(
        body, name=name, grid=(R // rows,), in_specs=[pl.BlockSpec((P, rows, C), lambda i: (0, i, 0))],
        out_specs=pl.BlockSpec((rows, C), lambda i: (i, 0)),
        out_shape=jax.ShapeDtypeStruct((R, C), F32), compiler_params=_params("parallel"),
    )(a)


def _reduce_scatter_start(grads, grads16, *, name):
    core = lax.axis_index("c").astype(jnp.int32).reshape(1)
    got = _swap_other_half(grads16, name=name + "_swap")
    chip_sums = [_add_half(g, r, core, name=f"{name}_add{w}") for w, (g, r) in enumerate(zip(grads, got))]
    lands = [jax.ShapeDtypeStruct((3,) + s.shape[1:], s.dtype) for s in chip_sums]
    return _copies_start(chip_sums, lands, chip_sums[0], _exchange_src, _exchange_dst, name=name + "_start")


def _reduce_scatter_finish(started, after, *, name):
    core = lax.axis_index("c").astype(jnp.int32)
    chip = (2 * lax.axis_index("x") + lax.axis_index("y")).astype(jnp.int32)
    where = jnp.stack([chip, core])
    sums, landed = _copies_wait(started, after, _exchange_src, _exchange_dst, name=name + "_wait")
    reduced = [_sum_parts(s, a, where, name=f"{name}_sum{w}") for w, (s, a) in enumerate(zip(sums, landed))]
    both = _share_halves(reduced, name=name + "_share")
    return [b.reshape(2 * b.shape[1], b.shape[2]) for b in both]


def _to_heads(t, heads, dil=1):
    S = t.shape[0]
    d = t.shape[1] // heads
    return jnp.transpose(t.reshape(S // dil, dil, heads, d), (2, 1, 0, 3)).reshape(heads, S, d)


def _from_heads(t, dil=1):
    heads, S, d = t.shape
    return jnp.transpose(t.reshape(heads, dil, S // dil, d), (2, 1, 0, 3)).reshape(S, heads * d)


def _unstride(t, dil):
    heads, S, d = t.shape
    return jnp.transpose(t.reshape(heads, dil, S // dil, d), (0, 2, 1, 3)).reshape(heads, S, d)


def _restride(t, dil):
    heads, S, d = t.shape
    return jnp.transpose(t.reshape(heads, S // dil, dil, d), (0, 2, 1, 3)).reshape(heads, S, d)


def _pad_cols(t, width, padded):
    S = t.shape[0]
    return jnp.pad(t.reshape(S, N_CHIPS, width), ((0, 0), (0, 0), (0, padded - width))).reshape(S, N_CHIPS * padded)


def _unpad_cols(t, width, padded):
    S = t.shape[0]
    return t.reshape(S, N_CHIPS, padded)[:, :, :width].reshape(S, N_CHIPS * width)


def _alibi(n):
    return 2.0 ** (-8.0 * jnp.arange(1, n + 1, dtype=F32) / n)


B_KW = [dict(hps=4, nb_seq=SEQ // d // ATT_BLK, n_back=w // d, scale=1.0 / math.sqrt(HEAD_DIM), dist_scale=d)
        for w, d in B_PATTERNS]
A_KW = dict(hps=A_Q_HEADS // A_KV_HEADS, nb_seq=SEQ // ATT_BLK, n_back=A_WINDOW - 1, scale=1.0 / math.sqrt(HEAD_DIM), dist_scale=1)
D_KW = dict(blk=256, hps=2, scale=1.0 / math.sqrt(D_NOPE + D_ROPE))
C_KW = dict(blk=256, hps=2, scale=1.0 / math.sqrt(HEAD_DIM))


def _front_block(t):
    return jnp.pad(t, ((0, 0), (ATT_BLK, 0), (0, 0)))


def _rope_tables(reps):
    inv_freq = ROPE_BASE ** (-jnp.arange(0, D_ROPE, 2, dtype=F32) / D_ROPE)
    ang = jnp.arange(SEQ, dtype=F32)[:, None] * inv_freq[None, :]
    return jnp.tile(jnp.cos(ang), (1, reps)), jnp.tile(jnp.sin(ang), (1, reps))


def _mlp_fwd(x, x16, w1, w2, g, b, tag):
    hid, act = _matmul(x16, w1, mode="nn", out_dtype=BF16, epilogue="relu2", name=f"mlp{tag}_up")
    m = _matmul(act, w2, mode="nn", out_dtype=F32, name=f"mlp{tag}_down")
    y, y16, u = _norm_fwd(x, m, g, b, center=True, eps=LN_EPS, alpha=ALPHA, name=f"ln2_{tag}")
    return y, y16, (x16, hid, act, u)


def _mlp_bwd(dy, saved, w1, w2, g, tag, after=None):
    x16, hid, act, u = saved
    du, du16, dg, db = _norm_bwd(dy, u, g, center=True, eps=LN_EPS, name=f"ln2_{tag}_bwd", after=after)
    dw2 = _matmul_tn(act, du16, shards=1, name=f"mlp{tag}_dw2")
    dhid = _matmul(du16, w2, mode="nt", out_dtype=BF16, epilogue="drelu2", extra=hid, name=f"mlp{tag}_dact")
    dw1 = _matmul_tn(x16, dhid, shards=N_CHIPS, name=f"mlp{tag}_dw1")
    dx = _matmul(dhid, w1, mode="nt", out_dtype=F32, epilogue="add", extra=du, alpha=ALPHA, name=f"mlp{tag}_dx")
    return dx, dw1, dw2, dg, db


def _even_fwd(x16, w_in, sinks):
    h = _unpad_cols(_matmul(x16, w_in, mode="nn", out_dtype=BF16, name="even_in"), EVEN_IN // N_CHIPS, EVEN_IN_PAD)
    qa = _to_heads(h[:, :A_Q_W], A_Q_HEADS)
    ka = _front_block(_to_heads(h[:, A_Q_W:A_Q_W + A_KV_W], A_KV_HEADS))
    va = _front_block(_to_heads(h[:, A_Q_W + A_KV_W:A_Q_W + 2 * A_KV_W], A_KV_HEADS))
    slopes_a, slopes_b = _alibi(A_Q_HEADS), _alibi(B_HEADS)
    oa, lse_a = _band_fwd(qa, ka, va, slopes_a, sinks, name="swa_fwd", **A_KW)
    base = A_Q_W + 2 * A_KV_W
    qkv_b, outs, lses = [], [], []
    for gi, (_, dil) in enumerate(B_PATTERNS):
        cols = h[:, base + gi * 3 * B_W:base + (gi + 1) * 3 * B_W]
        q, k, v = (_to_heads(cols[:, i * B_W:(i + 1) * B_W], B_HEADS, dil) for i in range(3))
        k, v = _front_block(k), _front_block(v)
        o, lse = _band_fwd(q, k, v, slopes_b, None, name=f"dil{gi}_fwd", **B_KW[gi])
        qkv_b.append((q, k, v, o, lse))
        outs.append(_unstride(o, dil))
        lses.append(_unstride(lse, dil))
    ob = _mix_fwd(outs, lses, name="dil_mix")
    y16 = jnp.concatenate([_from_heads(oa), _from_heads(ob)], axis=-1).astype(BF16)
    return y16, (x16, qa, ka, va, oa, lse_a, qkv_b, outs, lses, y16)


def _even_bwd(dmix16, du, saved, w_in, sinks, w_out):
    x16, qa, ka, va, oa, lse_a, qkv_b, outs, lses, y16 = saved
    slopes_a, slopes_b = _alibi(A_Q_HEADS), _alibi(B_HEADS)
    dw_out = _matmul_tn(y16, dmix16, shards=N_CHIPS, name="even_dwout")
    dy = _matmul(dmix16, w_out, mode="nt", out_dtype=F32, name="even_dy")
    doa = _to_heads(dy[:, :A_Q_W], A_Q_HEADS)
    dob = _to_heads(dy[:, A_Q_W:], B_HEADS)
    dqa, dka, dva, dsink = _band_bwd(qa, ka, va, oa, doa, lse_a, None, slopes_a, sinks, name="swa_bwd", **A_KW)
    douts, dlses = _mix_bwd(outs, lses, dob, name="dil_mix_bwd")
    parts = [_from_heads(dqa), _from_heads(dka[:, ATT_BLK:]), _from_heads(dva[:, ATT_BLK:])]
    for gi, (_, dil) in enumerate(B_PATTERNS):
        q, k, v, o, lse = qkv_b[gi]
        dq, dk, dv = _band_bwd(q, k, v, o, _restride(douts[gi], dil), lse, _restride(dlses[gi], dil), slopes_b, None,
                               name=f"dil{gi}_bwd", **B_KW[gi])
        parts += [_from_heads(dq, dil), _from_heads(dk[:, ATT_BLK:], dil), _from_heads(dv[:, ATT_BLK:], dil)]
    dh16 = _pad_cols(jnp.concatenate(parts, axis=-1), EVEN_IN // N_CHIPS, EVEN_IN_PAD).astype(BF16)
    dw_in = _matmul_tn(x16, dh16, shards=N_CHIPS, name="even_dwin")
    dx = _matmul(dh16, w_in, mode="nt", out_dtype=F32, epilogue="add", extra=du, alpha=ALPHA, name="even_dx")
    return dx, dw_in, dsink[:, 0, 0], dw_out


def _odd_fwd(x16, w_in, qn_g, kvn_g, w_uq, w_ukv, w_out):
    S = x16.shape[0]
    h = _unpad_cols(_matmul(x16, w_in, mode="nn", out_dtype=F32, name="odd_in"), ODD_IN // N_CHIPS, ODD_IN_PAD)
    qc, kc, vc = (_to_heads(h[:, i * C_W:(i + 1) * C_W].astype(BF16), C_HEADS) for i in range(3))
    cq = h[:, 3 * C_W:3 * C_W + D_Q_RANK]
    ckv = h[:, 3 * C_W + D_Q_RANK:3 * C_W + D_Q_RANK + D_KV_RANK]
    kr = h[:, 3 * C_W + D_Q_RANK + D_KV_RANK:]
    oc, tot = _stick_fwd(qc, kc, vc, name="stick_fwd", **C_KW)
    _, cqn16, _ = _norm_fwd(cq, None, qn_g, None, center=False, eps=RMS_EPS, alpha=1.0, name="q_rms")
    _, ckvn16, _ = _norm_fwd(ckv, None, kvn_g, None, center=False, eps=RMS_EPS, alpha=1.0, name="kv_rms")
    q = _matmul(cqn16, w_uq, mode="nn", out_dtype=F32, name="mla_uq").reshape(S, D_HEADS, D_NOPE + D_ROPE)
    kv = _matmul(ckvn16, w_ukv, mode="nn", out_dtype=F32, name="mla_ukv").reshape(S, D_HEADS, D_NOPE + D_V)
    half = D_ROPE // 2
    q_rope = q[..., D_NOPE:]
    x1 = jnp.concatenate([q_rope[..., :half].reshape(S, D_HEADS * half), kr[:, :half]], axis=-1)
    x2 = jnp.concatenate([q_rope[..., half:].reshape(S, D_HEADS * half), kr[:, half:]], axis=-1)
    cos, sin = _rope_tables(D_HEADS + 1)
    y1, y2 = _rope(x1[None], x2[None], cos, sin, name="rope_fwd")
    nq = D_HEADS * half
    q_rot = jnp.concatenate([y1[:, :nq].reshape(S, D_HEADS, half), y2[:, :nq].reshape(S, D_HEADS, half)], axis=-1)
    kr_rot = jnp.concatenate([y1[:, nq:], y2[:, nq:]], axis=-1)
    qd = jnp.transpose(jnp.concatenate([q[..., :D_NOPE], q_rot], axis=-1), (1, 0, 2)).astype(BF16)
    kd = jnp.transpose(jnp.concatenate([kv[..., :D_NOPE], jnp.broadcast_to(kr_rot[:, None, :], (S, D_HEADS, D_ROPE))], axis=-1),
                       (1, 0, 2)).astype(BF16)
    vd = jnp.transpose(kv[..., D_NOPE:], (1, 0, 2)).astype(BF16)
    od, lse_d = _causal_fwd(qd, kd, vd, name="mla_fwd", **D_KW)
    y16 = jnp.concatenate([_from_heads(oc), _from_heads(od)], axis=-1).astype(BF16)
    mixed = _matmul(y16, w_out, mode="nn", out_dtype=F32, name="odd_out")
    return mixed, (x16, qc, kc, vc, tot, cq, ckv, cqn16, ckvn16, qd, kd, vd, od, lse_d, y16)


def _odd_bwd(dmix16, du, saved, w_in, qn_g, kvn_g, w_uq, w_ukv, w_out):
    x16, qc, kc, vc, tot, cq, ckv, cqn16, ckvn16, qd, kd, vd, od, lse_d, y16 = saved
    S = x16.shape[0]
    half = D_ROPE // 2
    dw_out = _matmul_tn(y16, dmix16, shards=1, name="odd_dwout")
    dy = _matmul(dmix16, w_out, mode="nt", out_dtype=F32, name="odd_dy")
    doc = _to_heads(dy[:, :C_W], C_HEADS)
    dod = _to_heads(dy[:, C_W:], D_HEADS)
    dqc, dkc, dvc = _stick_bwd(qc, kc, vc, tot, doc, name="stick_bwd", **C_KW)
    dqd, dkd, dvd = _causal_bwd(qd, kd, vd, od, dod, lse_d, name="mla_bwd", **D_KW)
    cos, sin = _rope_tables(D_HEADS + 1)
    nq = D_HEADS * half
    gq1 = jnp.transpose(dqd[..., D_NOPE:D_NOPE + half], (1, 0, 2)).reshape(1, S, nq)
    gq2 = jnp.transpose(dqd[..., D_NOPE + half:], (1, 0, 2)).reshape(1, S, nq)
    dq1, dq2 = _rope(gq1, gq2, cos[:, :nq], -sin[:, :nq], name="rope_bwd_q")
    dk1, dk2 = _rope(dkd[..., D_NOPE:D_NOPE + half], dkd[..., D_NOPE + half:], cos[:, nq:], -sin[:, nq:], name="rope_bwd_k")
    dq_full = jnp.concatenate([jnp.transpose(dqd[..., :D_NOPE], (1, 0, 2)), dq1.reshape(S, D_HEADS, half),
                               dq2.reshape(S, D_HEADS, half)], axis=-1).reshape(S, D_HEADS * (D_NOPE + D_ROPE)).astype(BF16)
    dkv_full = jnp.concatenate([jnp.transpose(dkd[..., :D_NOPE], (1, 0, 2)), jnp.transpose(dvd, (1, 0, 2))],
                               axis=-1).reshape(S, D_HEADS * (D_NOPE + D_V)).astype(BF16)
    dw_uq = _matmul_tn(cqn16, dq_full, shards=N_CHIPS, name="mla_dwuq")
    dw_ukv = _matmul_tn(ckvn16, dkv_full, shards=N_CHIPS, name="mla_dwukv")
    dcqn = _matmul(dq_full, w_uq, mode="nt", out_dtype=F32, name="mla_dcq")
    dckvn = _matmul(dkv_full, w_ukv, mode="nt", out_dtype=F32, name="mla_dckv")
    dcq, _, dqn_g, _ = _norm_bwd(dcqn, cq, qn_g, center=False, eps=RMS_EPS, name="q_rms_bwd")
    dckv, _, dkvn_g, _ = _norm_bwd(dckvn, ckv, kvn_g, center=False, eps=RMS_EPS, name="kv_rms_bwd")
    dh = jnp.concatenate([_from_heads(dqc), _from_heads(dkc), _from_heads(dvc), dcq, dckv, dk1, dk2], axis=-1)
    dh16 = _pad_cols(dh, ODD_IN // N_CHIPS, ODD_IN_PAD).astype(BF16)
    dw_in = _matmul_tn(x16, dh16, shards=N_CHIPS, name="odd_dwin")
    dx = _matmul(dh16, w_in, mode="nt", out_dtype=F32, epilogue="add", extra=du, alpha=ALPHA, name="odd_dx")
    return dx, dw_in, dqn_g[0], dkvn_g[0], dw_uq, dw_ukv, dw_out


WEIGHT_GROUPS = (("even_w_in",), ("even_w_out", "mlp_w1_0", "mlp_w2_0"), ("odd_w_in", "odd_w_uq", "odd_w_ukv", "odd_w_out"),
                 ("mlp_w1_1", "mlp_w2_1"))
GRAD_GROUPS = (("mlp_w2_1", "mlp_w1_1"), ("odd_w_out", "odd_w_uq", "odd_w_ukv", "odd_w_in"), ("mlp_w2_0", "mlp_w1_0"),
               ("even_w_out", "even_w_in"))


def _local_step(x, target, small, weights_for, send_grads, total_loss):
    def by_rows(pair, rows):
        return tuple(t.reshape(N_CHIPS, rows // N_CHIPS, D_MODEL) for t in pair)

    x16 = x.astype(BF16)
    w = dict(weights_for(0, x16))
    y16, sv_e = _even_fwd(x16, w["even_w_in"], small["even_sinks"])
    w.update(weights_for(1, y16))
    mixed0 = _matmul(y16, w["even_w_out"], mode="nn", out_dtype=F32, name="even_out")
    x1, x1_16, u01 = _norm_fwd(x, mixed0, small["ln1_g"][0], small["ln1_b"][0], center=True, eps=LN_EPS, alpha=ALPHA, name="ln1_0")
    w1_0, w2_0 = w["mlp_w1_0"], w["mlp_w2_0"].reshape(1, D_FF, D_MODEL)
    x2, x2_16, sv_m0 = _mlp_fwd(x1, x1_16, w1_0, w2_0, small["ln2_g"][0], small["ln2_b"][0], 0)
    w.update(weights_for(2, x2_16))
    odd_w = (w["odd_w_in"], small["odd_q_norm_g"], small["odd_kv_norm_g"], w["odd_w_uq"], w["odd_w_ukv"],
             w["odd_w_out"].reshape(1, D_MODEL, D_MODEL))
    mixed1, sv_o = _odd_fwd(x2_16, *odd_w)
    x3, x3_16, u11 = _norm_fwd(x2, mixed1, small["ln1_g"][1], small["ln1_b"][1], center=True, eps=LN_EPS, alpha=ALPHA, name="ln1_1")
    w.update(weights_for(3, x3_16))
    w1_1, w2_1 = w["mlp_w1_1"], w["mlp_w2_1"].reshape(1, D_FF, D_MODEL)
    x4, _, sv_m1 = _mlp_fwd(x3, x3_16, w1_1, w2_1, small["ln2_g"][1], small["ln2_b"][1], 1)
    dy, loss_row = _loss_head(x4, target, name="loss_head")
    loss = total_loss(loss_row)

    dx3, dw1_1, dw2_1, dln2g_1, dln2b_1 = _mlp_bwd(dy, sv_m1, w1_1, w2_1, small["ln2_g"][1], 1, after=loss.reshape(1, 1))
    sent = send_grads(0, dict(mlp_w2_1=by_rows(dw2_1, D_FF), mlp_w1_1=dw1_1))
    du, du16, dln1g_1, dln1b_1 = _norm_bwd(dx3, u11, small["ln1_g"][1], center=True, eps=LN_EPS, name="ln1_1_bwd", after=sent)
    dx2, dwin_o, dqn_g, dkvn_g, dwuq, dwukv, dwout_o = _odd_bwd(du16, du, sv_o, *odd_w)
    sent = send_grads(1, dict(odd_w_out=by_rows(dwout_o, D_MODEL), odd_w_uq=dwuq, odd_w_ukv=dwukv, odd_w_in=dwin_o))
    dx1, dw1_0, dw2_0, dln2g_0, dln2b_0 = _mlp_bwd(dx2, sv_m0, w1_0, w2_0, small["ln2_g"][0], 0, after=sent)
    sent = send_grads(2, dict(mlp_w2_0=by_rows(dw2_0, D_FF), mlp_w1_0=dw1_0))
    du, du16, dln1g_0, dln1b_0 = _norm_bwd(dx1, u01, small["ln1_g"][0], center=True, eps=LN_EPS, name="ln1_0_bwd", after=sent)
    dx0, dwin_e, dsinks, dwout_e = _even_bwd(du16, du, sv_e, w["even_w_in"], small["even_sinks"], w["even_w_out"])
    send_grads(3, dict(even_w_out=dwout_e, even_w_in=dwin_e))

    sm = dict(even_sinks=dsinks, odd_q_norm_g=dqn_g, odd_kv_norm_g=dkvn_g,
              ln1_g=jnp.concatenate([dln1g_0, dln1g_1]), ln1_b=jnp.concatenate([dln1b_0, dln1b_1]),
              ln2_g=jnp.concatenate([dln2g_0, dln2g_1]), ln2_b=jnp.concatenate([dln2b_0, dln2b_1]))
    return loss, dx0, sm


SMALL_ORDER = ("ln1_g", "ln1_b", "ln2_g", "ln2_b", "even_sinks", "odd_q_norm_g", "odd_kv_norm_g")
SMALL_COLS = 2176


def _pack_small(parts):
    flat = jnp.concatenate([p.reshape(-1) for p in parts])
    return jnp.pad(flat, (0, 8 * SMALL_COLS - flat.shape[0])).reshape(8, SMALL_COLS)


def _unpack_small(packed, shapes):
    flat = packed.reshape(-1)
    out, pos = [], 0
    for s in shapes:
        n = math.prod(s)
        out.append(flat[pos:pos + n].reshape(s))
        pos += n
    return out


def kernel(x, even_w_in, even_sinks, even_w_out, odd_w_in, odd_q_norm_g, odd_kv_norm_g, odd_w_uq, odd_w_ukv, odd_w_out, ln1_g, ln1_b, mlp_w1, mlp_w2, ln2_g, ln2_b, loss_target, m_even_w_in, m_even_sinks, m_even_w_out, m_odd_w_in, m_odd_q_norm_g, m_odd_kv_norm_g, m_odd_w_uq, m_odd_w_ukv, m_odd_w_out, m_ln1_g, m_ln1_b, m_mlp_w1, m_mlp_w2, m_ln2_g, m_ln2_b, v_even_w_in, v_even_sinks, v_even_w_out, v_odd_w_in, v_odd_q_norm_g, v_odd_kv_norm_g, v_odd_w_uq, v_odd_w_ukv, v_odd_w_out, v_ln1_g, v_ln1_b, v_mlp_w1, v_mlp_w2, v_ln2_g, v_ln2_b):
    chip = 2 * lax.axis_index("x") + lax.axis_index("y")
    e_w, o_w = EVEN_IN // N_CHIPS, ODD_IN // N_CHIPS

    shards = dict(
        even_w_in=jnp.pad(even_w_in[0], ((0, 0), (0, EVEN_IN_PAD - e_w))), even_w_out=even_w_out[0],
        odd_w_in=jnp.pad(odd_w_in[0], ((0, 0), (0, ODD_IN_PAD - o_w))), odd_w_uq=odd_w_uq[0], odd_w_ukv=odd_w_ukv[0],
        odd_w_out=odd_w_out[0], mlp_w1_0=mlp_w1[0], mlp_w1_1=mlp_w1[1], mlp_w2_0=mlp_w2[0], mlp_w2_1=mlp_w2[1])
    names = list(shards)
    own16 = {n: shards[n].astype(BF16) for n in names}
    gather_started, token = [], own16[WEIGHT_GROUPS[0][0]]
    for g, group in enumerate(WEIGHT_GROUPS):
        lands = [jax.ShapeDtypeStruct((N_CHIPS,) + own16[n].shape, BF16) for n in group]
        gather_started.append(_copies_start([own16[n] for n in group], lands, token, _gather_src, _gather_dst,
                                            name=f"gather{g}_start"))
        token = gather_started[-1][-1]
    all_started = token

    def weights_for(g, after):
        group = WEIGHT_GROUPS[g]
        _, landed = _copies_wait(gather_started[g], all_started if g == 0 else after, _gather_src, _gather_landed,
                                 name=f"gather{g}_wait")
        full = _forward_halves(landed, name=f"gather{g}_forward")
        return {n: lax.dynamic_update_slice(f, own16[n][None], (chip, 0, 0)) for n, f in zip(group, full)}

    grads_started = {}

    def send_grads(g, pairs):
        group = GRAD_GROUPS[g]
        grads_started[g] = _reduce_scatter_start([pairs[n][0] for n in group], [pairs[n][1] for n in group], name=f"grads{g}")
        return grads_started[g][-1]

    norm_rows = jnp.pad(jnp.concatenate([odd_q_norm_g[0], odd_kv_norm_g[0]])[None], ((0, 7), (0, 64)))
    norm_all = _all_gather8(norm_rows, name="gather_norm_gains")[0::16]
    qn_w, kvn_w = D_Q_RANK // N_CHIPS, D_KV_RANK // N_CHIPS
    small = dict(even_sinks=even_sinks[0], odd_q_norm_g=norm_all[:, :qn_w].reshape(D_Q_RANK),
                 odd_kv_norm_g=norm_all[:, qn_w:qn_w + kvn_w].reshape(D_KV_RANK), ln1_g=ln1_g, ln1_b=ln1_b, ln2_g=ln2_g, ln2_b=ln2_b)

    loss, grad_x, sm = _local_step(x[0], loss_target[0], small, weights_for, send_grads,
                                   lambda row: lax.psum(row[0, 0], ("x", "y", "c")))

    weights = dict(even_w_in=even_w_in, even_sinks=even_sinks, even_w_out=even_w_out, odd_w_in=odd_w_in, odd_q_norm_g=odd_q_norm_g,
                   odd_kv_norm_g=odd_kv_norm_g, odd_w_uq=odd_w_uq, odd_w_ukv=odd_w_ukv, odd_w_out=odd_w_out, ln1_g=ln1_g, ln1_b=ln1_b,
                   mlp_w1=mlp_w1, mlp_w2=mlp_w2, ln2_g=ln2_g, ln2_b=ln2_b)
    m_in = dict(even_w_in=m_even_w_in, even_sinks=m_even_sinks, even_w_out=m_even_w_out, odd_w_in=m_odd_w_in,
                odd_q_norm_g=m_odd_q_norm_g, odd_kv_norm_g=m_odd_kv_norm_g, odd_w_uq=m_odd_w_uq, odd_w_ukv=m_odd_w_ukv,
                odd_w_out=m_odd_w_out, ln1_g=m_ln1_g, ln1_b=m_ln1_b, mlp_w1=m_mlp_w1, mlp_w2=m_mlp_w2, ln2_g=m_ln2_g, ln2_b=m_ln2_b)
    v_in = dict(even_w_in=v_even_w_in, even_sinks=v_even_sinks, even_w_out=v_even_w_out, odd_w_in=v_odd_w_in,
                odd_q_norm_g=v_odd_q_norm_g, odd_kv_norm_g=v_odd_kv_norm_g, odd_w_uq=v_odd_w_uq, odd_w_ukv=v_odd_w_ukv,
                odd_w_out=v_odd_w_out, ln1_g=v_ln1_g, ln1_b=v_ln1_b, mlp_w1=v_mlp_w1, mlp_w2=v_mlp_w2, ln2_g=v_ln2_g, ln2_b=v_ln2_b)
    order = list(weights)
    updated = {}

    def update(n, g2d, layer=0, tag=""):
        shape = weights[n].shape
        as3d = (1, math.prod(shape[:-1]), shape[-1]) if len(shape) == 2 else shape
        res = _adamw(weights[n].reshape(as3d), g2d, m_in[n].reshape(as3d), v_in[n].reshape(as3d), layer=layer,
                     carry=updated.get(n), name=f"adamw_{n}{tag}")
        updated[n] = tuple(res)
        return res[0]

    after = grad_x
    for g, group in enumerate(GRAD_GROUPS):
        for n, r in zip(group, _reduce_scatter_finish(grads_started[g], after, name=f"grads{g}")):
            if n[:-2] in ("mlp_w1", "mlp_w2"):
                after = update(n[:-2], r, layer=int(n[-1]), tag=n[-2:])
            elif n == "even_w_in":
                after = update(n, r[:, :e_w])
            elif n == "odd_w_in":
                after = update(n, r[:, :o_w])
            else:
                after = update(n, r)
        if g == len(GRAD_GROUPS) - 2:
            every = _all_gather8(_pack_small([sm[n] for n in SMALL_ORDER]), name="gather_small_grads")
            sm_sum = _sum_slabs(every.reshape(8, 8, SMALL_COLS), name="sum_small_grads")
            g_ln1g, g_ln1b, g_ln2g, g_ln2b, g_sinks, g_qn, g_kvn = _unpack_small(
                sm_sum, [(DEPTH, D_MODEL)] * 4 + [(1, A_Q_HEADS), (D_Q_RANK,), (D_KV_RANK,)])
            for n, gs in (("even_sinks", g_sinks), ("odd_q_norm_g", lax.dynamic_slice_in_dim(g_qn, chip * qn_w, qn_w)[None]),
                          ("odd_kv_norm_g", lax.dynamic_slice_in_dim(g_kvn, chip * kvn_w, kvn_w)[None]), ("ln1_g", g_ln1g),
                          ("ln1_b", g_ln1b), ("ln2_g", g_ln2g), ("ln2_b", g_ln2b)):
                update(n, gs)
    outs = [[updated[n][k].reshape(weights[n].shape) for n in order] for k in range(4)]
    return (loss, grad_x[None], *outs[0], *outs[1], *outs[2], *outs[3])
```

```python
import functools
import math

import jax
import jax.numpy as jnp
from jax import lax
from jax.experimental import pallas as pl
from jax.experimental.pallas import tpu as pltpu

F32 = jnp.float32
BF16 = jnp.bfloat16
MESH = pl.DeviceIdType.MESH

D_MODEL = 2048
SEQ = 2048
DEPTH = 2
HEAD_DIM = 64
A_Q_HEADS, A_KV_HEADS, A_WINDOW = 16, 2, 128
B_HEADS = 8
B_PATTERNS = ((128, 1), (512, 4), (2048, 16))
C_HEADS = 16
D_HEADS, D_Q_RANK, D_KV_RANK, D_NOPE, D_ROPE, D_V = 16, 512, 256, 64, 32, 64
ROPE_BASE = 10000.0
D_FF = 4 * D_MODEL
LN_EPS = 1e-5
RMS_EPS = 1e-6
ALPHA = (2 * DEPTH) ** 0.25
A_Q_W = A_Q_HEADS * HEAD_DIM
A_KV_W = A_KV_HEADS * HEAD_DIM
B_W = B_HEADS * HEAD_DIM
EVEN_IN = A_Q_W + 2 * A_KV_W + 3 * B_W * len(B_PATTERNS)
EVEN_OUT = A_Q_W + B_W
C_W = C_HEADS * HEAD_DIM
ODD_IN = 3 * C_W + D_Q_RANK + D_KV_RANK + D_ROPE
ADAM_LR, ADAM_B1, ADAM_B2, ADAM_EPS, ADAM_WD, ADAM_STEP = 0.001, 0.9, 0.999, 1e-08, 0.01, 10

N_CHIPS = 4
EVEN_IN_PAD = 1536
ODD_IN_PAD = 1024
ATT_BLK = 128
NEG = -1e30
V7X_VMEM_LIMIT = 48 * 1024 * 1024


def _params(*sem):
    return pltpu.CompilerParams(dimension_semantics=sem, vmem_limit_bytes=V7X_VMEM_LIMIT)


def _tile(n, cap):
    if n <= cap:
        return n
    t = cap - cap % 128
    while n % t:
        t -= 128
    return t


def _matmul(a, b, *, mode, out_dtype, name, epilogue=None, extra=None, alpha=1.0, tm=1024, tn=1024, tk=2048):
    if mode == "nn":
        M, K = a.shape
        P, _, Np = b.shape
        tm, tn, tk = _tile(M, tm), _tile(Np, tn), _tile(K, tk)
        nj = Np // tn
        grid = (M // tm, P * nj, K // tk)
        a_spec = pl.BlockSpec((tm, tk), lambda i, j, k: (i, k))
        b_spec = pl.BlockSpec((None, tk, tn), lambda i, j, k: (j // nj, k, j % nj))
        o_spec = pl.BlockSpec((tm, tn), lambda i, j, k: (i, j))
        out_shape = (M, P * Np)
        dims = (((1,), (0,)), ((), ()))
    elif mode == "nt":
        M, N = a.shape
        P, K, Np = b.shape
        tm, tn, tk = _tile(M, tm), _tile(K, tn), _tile(Np, tk)
        nkk = Np // tk
        grid = (M // tm, K // tn, P * nkk)
        a_spec = pl.BlockSpec((tm, tk), lambda i, j, k: (i, k))
        b_spec = pl.BlockSpec((None, tn, tk), lambda i, j, k: (k // nkk, j, k % nkk))
        o_spec = pl.BlockSpec((tm, tn), lambda i, j, k: (i, j))
        out_shape = (M, K)
        dims = (((1,), (1,)), ((), ()))
    else:
        raise ValueError(mode)
    n_k = grid[2]
    n_extra = 0 if extra is None else 1
    n_out = 2 if epilogue == "relu2" else 1

    def body(*refs):
        a_ref, b_ref = refs[:2]
        e_ref = refs[2] if n_extra else None
        outs = refs[2 + n_extra:2 + n_extra + n_out]
        part = lax.dot_general(a_ref[...], b_ref[...], dims, preferred_element_type=F32)

        def finish(r):
            if epilogue == "relu2":
                outs[0][...] = r.astype(outs[0].dtype)
                rp = jnp.maximum(r, 0.0)
                outs[1][...] = (rp * rp).astype(outs[1].dtype)
            elif epilogue == "drelu2":
                outs[0][...] = (r * (2.0 * jnp.maximum(e_ref[...].astype(F32), 0.0))).astype(outs[0].dtype)
            elif epilogue == "add":
                outs[0][...] = (r + alpha * e_ref[...].astype(F32)).astype(outs[0].dtype)
            else:
                outs[0][...] = r.astype(outs[0].dtype)

        if n_k == 1:
            finish(part)
        else:
            acc = refs[-1]
            k = pl.program_id(2)

            @pl.when(k == 0)
            def _():
                acc[...] = part

            @pl.when((k > 0) & (k < n_k - 1))
            def _():
                acc[...] += part

            @pl.when(k == n_k - 1)
            def _():
                finish(acc[...] + part)

    in_specs = [a_spec, b_spec] + ([o_spec] if n_extra else [])
    shapes = [jax.ShapeDtypeStruct(out_shape, out_dtype)] * n_out
    res = pl.pallas_call(
        body, name=name, grid=grid, in_specs=in_specs,
        out_specs=[o_spec] * n_out, out_shape=shapes,
        scratch_shapes=[pltpu.VMEM((tm, tn), F32)] if n_k > 1 else [],
        compiler_params=_params("parallel", "parallel", "arbitrary"),
    )(a, b, *([extra] if n_extra else []))
    return res if n_out > 1 else res[0]


def _matmul_tn(a, g, *, shards, name, tm=1024, tn=1024):
    M, K = a.shape
    P = shards
    Np = g.shape[1] // P
    tm, tn = _tile(K, tm), _tile(Np, tn)
    nj = Np // tn

    def body(a_ref, g_ref, o_ref):
        o_ref[...] = lax.dot_general(a_ref[...], g_ref[...], (((0,), (0,)), ((), ())), preferred_element_type=F32).astype(BF16)

    return pl.pallas_call(
        body, name=name, grid=(K // tm, P * nj),
        in_specs=[pl.BlockSpec((M, tm), lambda i, j: (0, i)), pl.BlockSpec((M, tn), lambda i, j: (0, j))],
        out_specs=pl.BlockSpec((None, tm, tn), lambda i, j: (j // nj, i, j % nj)),
        out_shape=jax.ShapeDtypeStruct((P, K, Np), BF16),
        compiler_params=_params("parallel", "parallel"),
    )(a, g)


def _norm_fwd(x, res, g, b, *, center, eps, alpha, name, rows=256):
    S, W = x.shape
    has_res = res is not None
    rows = _tile(S, rows)

    def body(*refs):
        x_ref = refs[0]
        r_ref = refs[1] if has_res else None
        g_ref = refs[1 + has_res]
        b_ref = refs[2 + has_res] if center else None
        y_ref, y16_ref, u_ref = refs[-3:]
        u = x_ref[...]
        if has_res:
            u = alpha * u + r_ref[...]
        if center:
            mu = jnp.mean(u, axis=1, keepdims=True)
            xc = u - mu
        else:
            xc = u
        var = jnp.mean(xc * xc, axis=1, keepdims=True)
        y = xc * lax.rsqrt(var + eps) * g_ref[...]
        if center:
            y = y + b_ref[...]
        y_ref[...] = y
        y16_ref[...] = y.astype(BF16)
        u_ref[...] = u

    row_spec = pl.BlockSpec((rows, W), lambda i: (i, 0))
    vec_spec = pl.BlockSpec((1, W), lambda i: (0, 0))
    ins = [x] + ([res] if has_res else []) + [g.reshape(1, W)] + ([b.reshape(1, W)] if center else [])
    specs = [row_spec] + ([row_spec] if has_res else []) + [vec_spec] + ([vec_spec] if center else [])
    return pl.pallas_call(
        body, name=name, grid=(S // rows,), in_specs=specs,
        out_specs=[row_spec, row_spec, row_spec],
        out_shape=[jax.ShapeDtypeStruct((S, W), F32), jax.ShapeDtypeStruct((S, W), BF16), jax.ShapeDtypeStruct((S, W), F32)],
        compiler_params=_params("parallel"),
    )(*ins)


def _norm_bwd(dy, u, g, *, center, eps, name, rows=256, after=None):
    S, W = u.shape
    rows = _tile(S, rows)

    def body(dy_ref, u_ref, g_ref, *rest):
        du_ref, du16_ref, dg_ref, db_ref = rest[-4:]
        i = pl.program_id(0)
        uu = u_ref[...]
        dyv = dy_ref[...]
        if center:
            xc = uu - jnp.mean(uu, axis=1, keepdims=True)
        else:
            xc = uu
        rstd = lax.rsqrt(jnp.mean(xc * xc, axis=1, keepdims=True) + eps)
        xhat = xc * rstd
        dxh = dyv * g_ref[...]
        c2 = jnp.mean(dxh * xhat, axis=1, keepdims=True)
        du = dxh - xhat * c2
        if center:
            du = du - jnp.mean(dxh, axis=1, keepdims=True)
        du = du * rstd
        du_ref[...] = du
        du16_ref[...] = du.astype(BF16)

        @pl.when(i == 0)
        def _():
            dg_ref[...] = jnp.zeros_like(dg_ref)
            db_ref[...] = jnp.zeros_like(db_ref)

        dg_ref[...] += jnp.sum(dyv * xhat, axis=0, keepdims=True)
        db_ref[...] += jnp.sum(dyv, axis=0, keepdims=True)

    row_spec = pl.BlockSpec((rows, W), lambda i: (i, 0))
    vec_spec = pl.BlockSpec((1, W), lambda i: (0, 0))
    return pl.pallas_call(
        body, name=name, grid=(S // rows,),
        in_specs=[row_spec, row_spec, vec_spec] + ([] if after is None else [pl.BlockSpec(memory_space=pl.ANY)]),
        out_specs=[row_spec, row_spec, vec_spec, vec_spec],
        out_shape=[jax.ShapeDtypeStruct((S, W), F32), jax.ShapeDtypeStruct((S, W), BF16),
                   jax.ShapeDtypeStruct((1, W), F32), jax.ShapeDtypeStruct((1, W), F32)],
        compiler_params=_params("arbitrary"),
    )(dy, u, g.reshape(1, W), *([] if after is None else [after]))


def _loss_head(y, target, *, name, rows=256):
    S, W = y.shape
    rows = _tile(S, rows)

    def body(y_ref, t_ref, dy_ref, l_ref):
        i = pl.program_id(0)
        e = y_ref[...] - t_ref[...]
        dy_ref[...] = e * (1.0 / W)

        @pl.when(i == 0)
        def _():
            l_ref[...] = jnp.zeros_like(l_ref)

        l_ref[...] += jnp.full(l_ref.shape, 0.5 / W * jnp.sum(e * e), F32)

    row_spec = pl.BlockSpec((rows, W), lambda i: (i, 0))
    return pl.pallas_call(
        body, name=name, grid=(S // rows,), in_specs=[row_spec, row_spec],
        out_specs=[row_spec, pl.BlockSpec((1, 128), lambda i: (0, 0))],
        out_shape=[jax.ShapeDtypeStruct((S, W), F32), jax.ShapeDtypeStruct((1, 128), F32)],
        compiler_params=_params("arbitrary"),
    )(y, target)


def _adamw(w, g, m, v, *, name, layer=0, carry=None, rows=256):
    L, R, C = w.shape
    rows = _tile(R, rows) if R % 8 == 0 else R
    c1 = 1.0 / (1.0 - ADAM_B1 ** ADAM_STEP)
    c2 = 1.0 / (1.0 - ADAM_B2 ** ADAM_STEP)

    def body(w_ref, g_ref, m_ref, v_ref, *rest):
        go_ref, d_ref, mo_ref, vo_ref = rest[-4:]
        gg = g_ref[...]
        mn = ADAM_B1 * m_ref[...] + (1.0 - ADAM_B1) * gg
        vn = ADAM_B2 * v_ref[...] + (1.0 - ADAM_B2) * (gg * gg)
        go_ref[...] = gg
        d_ref[...] = -ADAM_LR * ((mn * c1) / (jnp.sqrt(vn * c2) + ADAM_EPS) + ADAM_WD * w_ref[...])
        mo_ref[...] = mn
        vo_ref[...] = vn

    slab = pl.BlockSpec((None, rows, C), lambda i: (layer, i, 0))
    shp = jax.ShapeDtypeStruct((L, R, C), F32)
    carried = [] if carry is None else list(carry)
    return pl.pallas_call(
        body, name=name, grid=(R // rows,),
        in_specs=[slab, pl.BlockSpec((rows, C), lambda i: (i, 0)), slab, slab] + [pl.BlockSpec(memory_space=pl.ANY)] * len(carried),
        out_specs=[slab] * 4, out_shape=[shp] * 4, input_output_aliases={4 + k: k for k in range(len(carried))},
        compiler_params=_params("parallel"),
    )(w, g, m, v, *carried)


def _rope(x1, x2, cos, sin, *, name, rows=512):
    H, S, W = x1.shape
    rows = _tile(S, rows)

    def body(x1_ref, x2_ref, c_ref, s_ref, y1_ref, y2_ref):
        t1 = jnp.sum(x1_ref[...], axis=0)
        t2 = jnp.sum(x2_ref[...], axis=0)
        c = c_ref[...]
        s = s_ref[...]
        y1_ref[...] = t1 * c - t2 * s
        y2_ref[...] = t1 * s + t2 * c

    xs = pl.BlockSpec((H, rows, W), lambda i: (0, i, 0))
    ts = pl.BlockSpec((rows, W), lambda i: (i, 0))
    shp = jax.ShapeDtypeStruct((S, W), F32)
    return pl.pallas_call(
        body, name=name, grid=(S // rows,), in_specs=[xs, xs, ts, ts], out_specs=[ts, ts], out_shape=[shp, shp],
        compiler_params=_params("parallel"),
    )(x1, x2, cos, sin)


def _band_mask(blk, n, n_back):
    row = lax.broadcasted_iota(jnp.int32, (blk, 2 * blk), 0)
    col = lax.broadcasted_iota(jnp.int32, (blk, 2 * blk), 1)
    rel = blk + row - col
    return rel, (rel >= 0) & (rel <= n_back) & ((col >= blk) | (n >= 1))


def _band_fwd(q, k, v, slopes, sinks, *, hps, nb_seq, n_back, scale, dist_scale, name):
    blk = ATT_BLK
    Hq, T, d = q.shape
    Hk = k.shape[0]
    group = Hq // Hk
    kps = max(hps // group, 1)
    use_sink = sinks is not None

    def body(*refs):
        q_ref, k_ref, v_ref, slope_ref = refs[:4]
        sink_ref = refs[4] if use_sink else None
        o_ref, lse_ref = refs[-2:]
        i = pl.program_id(1)
        start = pl.multiple_of(i * blk, blk)
        rel, valid = _band_mask(blk, i % nb_seq, n_back)
        relf = rel.astype(F32)
        for hh in range(hps):
            h = pl.program_id(0) * hps + hh
            kk = k_ref[hh // group, pl.ds(start, 2 * blk), :]
            vv = v_ref[hh // group, pl.ds(start, 2 * blk), :]
            s = lax.dot_general(q_ref[hh], kk, (((1,), (1,)), ((), ())), preferred_element_type=F32) * scale
            s = jnp.where(valid, s - (slope_ref[h] * dist_scale) * relf, NEG)
            m = jnp.max(s, axis=1, keepdims=True)
            if use_sink:
                m = jnp.maximum(m, sink_ref[h])
            p = jnp.where(valid, jnp.exp(s - m), 0.0)
            l = jnp.sum(p, axis=1, keepdims=True)
            if use_sink:
                l = l + jnp.exp(sink_ref[h] - m)
            o_ref[hh] = jnp.dot(p.astype(BF16), vv, preferred_element_type=F32) / l
            lse_ref[hh] = m + jnp.log(l)

    smem = pl.BlockSpec(memory_space=pltpu.SMEM)
    qs = pl.BlockSpec((hps, blk, d), lambda g, i: (g, i, 0))
    ks = pl.BlockSpec((kps, T + blk, d), lambda g, i: (g, 0, 0))
    ins = [q, k, v, slopes] + ([sinks] if use_sink else [])
    return pl.pallas_call(
        body, name=name, grid=(Hq // hps, T // blk), in_specs=[qs, ks, ks, smem] + ([smem] if use_sink else []),
        out_specs=[qs, pl.BlockSpec((hps, blk, 1), lambda g, i: (g, i, 0))],
        out_shape=[jax.ShapeDtypeStruct((Hq, T, d), F32), jax.ShapeDtypeStruct((Hq, T, 1), F32)],
        compiler_params=_params("parallel", "parallel"),
    )(*ins)


def _band_bwd(q, k, v, o, do, lse, dlse, slopes, sinks, *, hps, nb_seq, n_back, scale, dist_scale, name):
    blk = ATT_BLK
    Hq, T, d = q.shape
    Hk = k.shape[0]
    group = Hq // Hk
    kps = max(hps // group, 1)
    use_sink, use_dlse = sinks is not None, dlse is not None

    def body(*refs):
        q_ref, k_ref, v_ref, o_ref, do_ref, lse_ref = refs[:6]
        pos = 6
        dlse_ref = refs[pos] if use_dlse else None
        pos += use_dlse
        slope_ref = refs[pos]
        pos += 1
        sink_ref = refs[pos] if use_sink else None
        pos += use_sink
        dq_ref, dk_ref, dv_ref = refs[pos:pos + 3]
        dsink_ref = refs[pos + 3] if use_sink else None
        i = pl.program_id(1)
        start = pl.multiple_of(i * blk, blk)
        rel, valid = _band_mask(blk, i % nb_seq, n_back)
        relf = rel.astype(F32)

        @pl.when(i == 0)
        def _():
            dk_ref[...] = jnp.zeros_like(dk_ref)
            dv_ref[...] = jnp.zeros_like(dv_ref)
            if use_sink:
                dsink_ref[...] = jnp.zeros_like(dsink_ref)

        for kh in range(kps):
            dk_acc = jnp.zeros((2 * blk, d), F32)
            dv_acc = jnp.zeros((2 * blk, d), F32)
            kk = k_ref[kh, pl.ds(start, 2 * blk), :]
            vv = v_ref[kh, pl.ds(start, 2 * blk), :]
            for hh in range(kh * min(group, hps), (kh + 1) * min(group, hps)):
                h = pl.program_id(0) * hps + hh
                qb = q_ref[hh]
                dob = do_ref[hh]
                do16 = dob.astype(BF16)
                lse = lse_ref[hh]
                c = -jnp.sum(dob * o_ref[hh], axis=1, keepdims=True)
                if use_dlse:
                    c = c + dlse_ref[hh]
                s = lax.dot_general(qb, kk, (((1,), (1,)), ((), ())), preferred_element_type=F32) * scale
                s = jnp.where(valid, s - (slope_ref[h] * dist_scale) * relf, NEG)
                p = jnp.where(valid, jnp.exp(s - lse), 0.0)
                dp = lax.dot_general(do16, vv, (((1,), (1,)), ((), ())), preferred_element_type=F32)
                ds16 = (p * (dp + c) * scale).astype(BF16)
                dq_ref[hh] = jnp.dot(ds16, kk, preferred_element_type=F32)
                dk_acc = dk_acc + lax.dot_general(ds16, qb, (((0,), (0,)), ((), ())), preferred_element_type=F32)
                dv_acc = dv_acc + lax.dot_general(p.astype(BF16), do16, (((0,), (0,)), ((), ())), preferred_element_type=F32)
                if use_sink:
                    dsink_ref[hh] += jnp.full((1, 128), jnp.sum(jnp.exp(sink_ref[h] - lse) * c), F32)
            dk_ref[kh, pl.ds(start, 2 * blk), :] += dk_acc
            dv_ref[kh, pl.ds(start, 2 * blk), :] += dv_acc

    smem = pl.BlockSpec(memory_space=pltpu.SMEM)
    qs = pl.BlockSpec((hps, blk, d), lambda g, i: (g, i, 0))
    ls = pl.BlockSpec((hps, blk, 1), lambda g, i: (g, i, 0))
    ks = pl.BlockSpec((kps, T + blk, d), lambda g, i: (g, 0, 0))
    in_specs = [qs, ks, ks, qs, qs, ls] + ([ls] if use_dlse else []) + [smem] + ([smem] if use_sink else [])
    ins = [q, k, v, o, do, lse] + ([dlse] if use_dlse else []) + [slopes] + ([sinks] if use_sink else [])
    out_specs = [qs, ks, ks]
    out_shape = [jax.ShapeDtypeStruct((Hq, T, d), F32), jax.ShapeDtypeStruct((Hk, T + blk, d), F32),
                 jax.ShapeDtypeStruct((Hk, T + blk, d), F32)]
    if use_sink:
        out_specs.append(pl.BlockSpec((hps, 1, 128), lambda g, i: (g, 0, 0)))
        out_shape.append(jax.ShapeDtypeStruct((Hq, 1, 128), F32))
    return pl.pallas_call(
        body, name=name, grid=(Hq // hps, T // blk), in_specs=in_specs, out_specs=out_specs, out_shape=out_shape,
        compiler_params=_params("parallel", "arbitrary"),
    )(*ins)


def _diagonal_mask(blk):
    return lax.broadcasted_iota(jnp.int32, (blk, blk), 0) >= lax.broadcasted_iota(jnp.int32, (blk, blk), 1)


def _causal_fwd(q, k, v, *, blk, hps, scale, name):
    H, T, dqk = q.shape
    dv = v.shape[2]

    def body(q_ref, k_ref, v_ref, o_ref, lse_ref):
        n = pl.program_id(1)
        qs = [q_ref[hh] for hh in range(hps)]

        def block(j, carry, valid):
            start = pl.multiple_of(j * blk, blk)
            out = []
            for hh in range(hps):
                m, l, acc = carry[hh]
                kb = k_ref[hh, pl.ds(start, blk), :]
                vb = v_ref[hh, pl.ds(start, blk), :]
                s = lax.dot_general(qs[hh], kb, (((1,), (1,)), ((), ())), preferred_element_type=F32) * scale
                if valid is not None:
                    s = jnp.where(valid, s, NEG)
                m_new = jnp.maximum(m, jnp.max(s, axis=1, keepdims=True))
                p = _keep(valid, jnp.exp(s - m_new))
                a = jnp.exp(m - m_new)
                out.append((m_new, a * l + jnp.sum(p, axis=1, keepdims=True),
                            a * acc + jnp.dot(p.astype(BF16), vb, preferred_element_type=F32)))
            return tuple(out)

        init = tuple((jnp.full((blk, 1), NEG, F32), jnp.zeros((blk, 1), F32), jnp.zeros((blk, dv), F32)) for _ in range(hps))
        res = block(n, lax.fori_loop(0, n, lambda j, carry: block(j, carry, None), init), _diagonal_mask(blk))
        for hh in range(hps):
            m, l, acc = res[hh]
            o_ref[hh] = acc / l
            lse_ref[hh] = m + jnp.log(l)

    qs_ = pl.BlockSpec((hps, blk, dqk), lambda g, i: (g, i, 0))
    return pl.pallas_call(
        body, name=name, grid=(H // hps, T // blk),
        in_specs=[qs_, pl.BlockSpec((hps, T, dqk), lambda g, i: (g, 0, 0)), pl.BlockSpec((hps, T, dv), lambda g, i: (g, 0, 0))],
        out_specs=[pl.BlockSpec((hps, blk, dv), lambda g, i: (g, i, 0)), pl.BlockSpec((hps, blk, 1), lambda g, i: (g, i, 0))],
        out_shape=[jax.ShapeDtypeStruct((H, T, dv), F32), jax.ShapeDtypeStruct((H, T, 1), F32)],
        compiler_params=_params("parallel", "parallel"),
    )(q, k, v)


def _causal_bwd(q, k, v, o, do, lse, *, blk, hps, scale, name):
    H, T, dqk = q.shape
    dv = v.shape[2]

    def body(q_ref, k_ref, v_ref, o_ref, do_ref, lse_ref, dq_ref, dk_ref, dv_ref):
        n = pl.program_id(1)

        @pl.when(n == 0)
        def _():
            dk_ref[...] = jnp.zeros_like(dk_ref)
            dv_ref[...] = jnp.zeros_like(dv_ref)

        qs = [q_ref[hh] for hh in range(hps)]
        do16 = [do_ref[hh].astype(BF16) for hh in range(hps)]
        lses = [lse_ref[hh] for hh in range(hps)]
        cs = [-jnp.sum(do_ref[hh] * o_ref[hh], axis=1, keepdims=True) for hh in range(hps)]

        def block(j, dqs, valid):
            start = pl.multiple_of(j * blk, blk)
            out = []
            for hh in range(hps):
                kb = k_ref[hh, pl.ds(start, blk), :]
                vb = v_ref[hh, pl.ds(start, blk), :]
                s = lax.dot_general(qs[hh], kb, (((1,), (1,)), ((), ())), preferred_element_type=F32) * scale
                if valid is not None:
                    s = jnp.where(valid, s, NEG)
                p = _keep(valid, jnp.exp(s - lses[hh]))
                dp = lax.dot_general(do16[hh], vb, (((1,), (1,)), ((), ())), preferred_element_type=F32)
                ds16 = (p * (dp + cs[hh]) * scale).astype(BF16)
                dk_ref[hh, pl.ds(start, blk), :] += lax.dot_general(ds16, qs[hh], (((0,), (0,)), ((), ())), preferred_element_type=F32)
                dv_ref[hh, pl.ds(start, blk), :] += lax.dot_general(p.astype(BF16), do16[hh], (((0,), (0,)), ((), ())),
                                                                    preferred_element_type=F32)
                out.append(dqs[hh] + jnp.dot(ds16, kb, preferred_element_type=F32))
            return tuple(out)

        init = tuple(jnp.zeros((blk, dqk), F32) for _ in range(hps))
        res = block(n, lax.fori_loop(0, n, lambda j, dqs: block(j, dqs, None), init), _diagonal_mask(blk))
        for hh in range(hps):
            dq_ref[hh] = res[hh]

    qs_ = pl.BlockSpec((hps, blk, dqk), lambda g, i: (g, i, 0))
    os_ = pl.BlockSpec((hps, blk, dv), lambda g, i: (g, i, 0))
    ls_ = pl.BlockSpec((hps, blk, 1), lambda g, i: (g, i, 0))
    ks_ = pl.BlockSpec((hps, T, dqk), lambda g, i: (g, 0, 0))
    vs_ = pl.BlockSpec((hps, T, dv), lambda g, i: (g, 0, 0))
    return pl.pallas_call(
        body, name=name, grid=(H // hps, T // blk), in_specs=[qs_, ks_, vs_, os_, os_, ls_], out_specs=[qs_, ks_, vs_],
        out_shape=[jax.ShapeDtypeStruct((H, T, dqk), F32), jax.ShapeDtypeStruct((H, T, dqk), F32),
                   jax.ShapeDtypeStruct((H, T, dv), F32)],
        compiler_params=_params("parallel", "arbitrary"),
    )(q, k, v, o, do, lse)


def _band_geometry(blk, nb_seq, n_prev):
    def geo(i):
        seq = i // nb_seq
        n = i % nb_seq
        return seq, n, jnp.maximum(n - n_prev, 0)
    return geo


def _attn_fwd(q, k, v, slopes, sinks, *, blk, nb_seq, n_prev, n_back, scale, dist_scale, name):
    Hq, T, dqk = q.shape
    Hk, _, dv = v.shape
    group = Hq // Hk
    use_bias, use_sink = slopes is not None, sinks is not None
    geo = _band_geometry(blk, nb_seq, n_prev)

    def body(*refs):
        q_ref, k_ref, v_ref = refs[:3]
        slope_ref = refs[3] if use_bias else None
        sink_ref = refs[3 + use_bias] if use_sink else None
        o_ref, lse_ref = refs[-2:]
        h = pl.program_id(0)
        seq, n, lo = geo(pl.program_id(1))
        qb = q_ref[...]
        diff = lax.broadcasted_iota(jnp.int32, (blk, blk), 0) - lax.broadcasted_iota(jnp.int32, (blk, blk), 1)
        if use_sink:
            m0 = jnp.full((blk, 1), sink_ref[h], F32)
            l0 = jnp.ones((blk, 1), F32)
        else:
            m0 = jnp.full((blk, 1), NEG, F32)
            l0 = jnp.zeros((blk, 1), F32)

        def step(j, carry):
            m, l, acc = carry
            start = pl.multiple_of((seq * nb_seq + j) * blk, blk)
            kb = k_ref[pl.ds(start, blk), :]
            vb = v_ref[pl.ds(start, blk), :]
            s = lax.dot_general(qb, kb, (((1,), (1,)), ((), ())), preferred_element_type=F32) * scale
            rel = (n - j) * blk + diff
            valid = (rel >= 0) & (rel <= n_back)
            if use_bias:
                s = s - (slope_ref[h] * dist_scale) * rel.astype(F32)
            s = jnp.where(valid, s, NEG)
            m_new = jnp.maximum(m, jnp.max(s, axis=1, keepdims=True))
            p = jnp.where(valid, jnp.exp(s - m_new), 0.0)
            a = jnp.exp(m - m_new)
            l = a * l + jnp.sum(p, axis=1, keepdims=True)
            acc = a * acc + jnp.dot(p.astype(BF16), vb, preferred_element_type=F32)
            return m_new, l, acc

        m, l, acc = lax.fori_loop(lo, n + 1, step, (m0, l0, jnp.zeros((blk, dv), F32)))
        o_ref[...] = acc / l
        lse_ref[...] = m + jnp.log(l)

    smem = pl.BlockSpec(memory_space=pltpu.SMEM)
    in_specs = [pl.BlockSpec((None, blk, dqk), lambda h, i: (h, i, 0)),
                pl.BlockSpec((None, T, dqk), lambda h, i: (h // group, 0, 0)),
                pl.BlockSpec((None, T, dv), lambda h, i: (h // group, 0, 0))]
    ins = [q, k, v]
    if use_bias:
        in_specs.append(smem)
        ins.append(slopes)
    if use_sink:
        in_specs.append(smem)
        ins.append(sinks)
    return pl.pallas_call(
        body, name=name, grid=(Hq, T // blk), in_specs=in_specs,
        out_specs=[pl.BlockSpec((None, blk, dv), lambda h, i: (h, i, 0)), pl.BlockSpec((None, blk, 1), lambda h, i: (h, i, 0))],
        out_shape=[jax.ShapeDtypeStruct((Hq, T, dv), F32), jax.ShapeDtypeStruct((Hq, T, 1), F32)],
        compiler_params=_params("parallel", "parallel"),
    )(*ins)


def _attn_bwd(q, k, v, o, do, lse, dlse, slopes, sinks, *, blk, nb_seq, n_prev, n_back, scale, dist_scale, name):
    Hq, T, dqk = q.shape
    Hk, _, dv = v.shape
    group = Hq // Hk
    use_bias, use_sink, use_dlse = slopes is not None, sinks is not None, dlse is not None
    geo = _band_geometry(blk, nb_seq, n_prev)

    def body(*refs):
        q_ref, k_ref, v_ref, o_ref, do_ref, lse_ref = refs[:6]
        pos = 6
        dlse_ref = refs[pos] if use_dlse else None
        pos += use_dlse
        slope_ref = refs[pos] if use_bias else None
        pos += use_bias
        sink_ref = refs[pos] if use_sink else None
        pos += use_sink
        dq_ref, dk_ref, dv_ref = refs[pos:pos + 3]
        dsink_ref = refs[pos + 3] if use_sink else None
        h = pl.program_id(0)
        i = pl.program_id(1)
        seq, n, lo = geo(i)
        qb = q_ref[...]
        dob = do_ref[...]
        do16 = dob.astype(BF16)
        lse = lse_ref[...]
        c = -jnp.sum(dob * o_ref[...], axis=1, keepdims=True)
        if use_dlse:
            c = c + dlse_ref[...]
        diff = lax.broadcasted_iota(jnp.int32, (blk, blk), 0) - lax.broadcasted_iota(jnp.int32, (blk, blk), 1)

        @pl.when((h % group == 0) & (i == 0))
        def _():
            dk_ref[...] = jnp.zeros_like(dk_ref)
            dv_ref[...] = jnp.zeros_like(dv_ref)

        def step(j, dq):
            start = pl.multiple_of((seq * nb_seq + j) * blk, blk)
            kb = k_ref[pl.ds(start, blk), :]
            vb = v_ref[pl.ds(start, blk), :]
            s = lax.dot_general(qb, kb, (((1,), (1,)), ((), ())), preferred_element_type=F32) * scale
            rel = (n - j) * blk + diff
            valid = (rel >= 0) & (rel <= n_back)
            if use_bias:
                s = s - (slope_ref[h] * dist_scale) * rel.astype(F32)
            p = jnp.where(valid, jnp.exp(jnp.where(valid, s, NEG) - lse), 0.0)
            dp = lax.dot_general(do16, vb, (((1,), (1,)), ((), ())), preferred_element_type=F32)
            ds16 = (p * (dp + c) * scale).astype(BF16)
            dk_ref[pl.ds(start, blk), :] += lax.dot_general(ds16, qb, (((0,), (0,)), ((), ())), preferred_element_type=F32)
            dv_ref[pl.ds(start, blk), :] += lax.dot_general(p.astype(BF16), do16, (((0,), (0,)), ((), ())), preferred_element_type=F32)
            return dq + jnp.dot(ds16, kb, preferred_element_type=F32)

        dq_ref[...] = lax.fori_loop(lo, n + 1, step, jnp.zeros((blk, dqk), F32))
        if use_sink:
            @pl.when(i == 0)
            def _():
                dsink_ref[...] = jnp.zeros_like(dsink_ref)

            dsink_ref[...] += jnp.full(dsink_ref.shape, jnp.sum(jnp.exp(sink_ref[h] - lse) * c), F32)

    smem = pl.BlockSpec(memory_space=pltpu.SMEM)
    qs = pl.BlockSpec((None, blk, dqk), lambda h, i: (h, i, 0))
    os_ = pl.BlockSpec((None, blk, dv), lambda h, i: (h, i, 0))
    ls = pl.BlockSpec((None, blk, 1), lambda h, i: (h, i, 0))
    ks = pl.BlockSpec((None, T, dqk), lambda h, i: (h // group, 0, 0))
    vs = pl.BlockSpec((None, T, dv), lambda h, i: (h // group, 0, 0))
    in_specs = [qs, ks, vs, os_, os_, ls]
    ins = [q, k, v, o, do, lse]
    if use_dlse:
        in_specs.append(ls)
        ins.append(dlse)
    if use_bias:
        in_specs.append(smem)
        ins.append(slopes)
    if use_sink:
        in_specs.append(smem)
        ins.append(sinks)
    out_specs = [qs, ks, vs]
    out_shape = [jax.ShapeDtypeStruct((Hq, T, dqk), F32), jax.ShapeDtypeStruct((Hk, T, dqk), F32),
                 jax.ShapeDtypeStruct((Hk, T, dv), F32)]
    if use_sink:
        out_specs.append(pl.BlockSpec((None, 1, 128), lambda h, i: (h, 0, 0)))
        out_shape.append(jax.ShapeDtypeStruct((Hq, 1, 128), F32))
    return pl.pallas_call(
        body, name=name, grid=(Hq, T // blk), in_specs=in_specs, out_specs=out_specs, out_shape=out_shape,
        compiler_params=_params("arbitrary", "arbitrary"),
    )(*ins)


def _tri_sum(x, upper):
    hi = x.astype(BF16)
    lo = (x - hi.astype(F32)).astype(BF16)
    return jnp.dot(hi, upper, preferred_element_type=F32) + jnp.dot(lo, upper, preferred_element_type=F32)


def _keep(mask, x):
    return x if mask is None else jnp.where(mask, x, 0.0)


def _stick_terms(qb, kb, scale, strict):
    z = lax.dot_general(qb, kb, (((1,), (1,)), ((), ())), preferred_element_type=F32) * scale
    e = jnp.exp(-jnp.abs(z))
    lb = jnp.minimum(z, 0.0) - jnp.log(1.0 + e)
    return z, e, lb, _keep(strict, lb - z)


def _stick_fwd(q, k, v, *, blk, hps, scale, name):
    H, T, dh = q.shape

    def body(q_ref, k_ref, v_ref, o_ref, tot_ref):
        n = pl.program_id(1)
        qs = [q_ref[hh] for hh in range(hps)]
        r_i = lax.broadcasted_iota(jnp.int32, (blk, blk), 0)
        c_i = lax.broadcasted_iota(jnp.int32, (blk, blk), 1)
        upper = (r_i > c_i).astype(BF16)

        def block(j, carry, strict):
            start = pl.multiple_of(j * blk, blk)
            out = []
            for hh in range(hps):
                run, acc = carry[hh]
                kb = k_ref[hh, pl.ds(start, blk), :]
                vb = v_ref[hh, pl.ds(start, blk), :]
                _, _, lb, lk = _stick_terms(qs[hh], kb, scale, strict)
                w = _keep(strict, jnp.exp(lb + run + _tri_sum(lk, upper)))
                out.append((run + jnp.sum(lk, axis=1, keepdims=True), acc + jnp.dot(w.astype(BF16), vb, preferred_element_type=F32)))
            return tuple(out)

        init = tuple((jnp.zeros((blk, 1), F32), jnp.zeros((blk, dh), F32)) for _ in range(hps))
        res = lax.fori_loop(1, n + 1, lambda t, carry: block(n - t, carry, None), block(n, init, r_i > c_i))
        for hh in range(hps):
            tot_ref[hh], o_ref[hh] = res[hh]

    qs_ = pl.BlockSpec((hps, blk, dh), lambda g, i: (g, i, 0))
    ks_ = pl.BlockSpec((hps, T, dh), lambda g, i: (g, 0, 0))
    ts_ = pl.BlockSpec((hps, blk, 1), lambda g, i: (g, i, 0))
    return pl.pallas_call(
        body, name=name, grid=(H // hps, T // blk), in_specs=[qs_, ks_, ks_], out_specs=[qs_, ts_],
        out_shape=[jax.ShapeDtypeStruct((H, T, dh), F32), jax.ShapeDtypeStruct((H, T, 1), F32)],
        compiler_params=_params("parallel", "parallel"),
    )(q, k, v)


def _stick_bwd(q, k, v, tot, do, *, blk, hps, scale, name):
    H, T, dh = q.shape

    def body(q_ref, k_ref, v_ref, tot_ref, do_ref, dq_ref, dk_ref, dv_ref):
        n = pl.program_id(1)
        qs = [q_ref[hh] for hh in range(hps)]
        do16 = [do_ref[hh].astype(BF16) for hh in range(hps)]
        tots = [tot_ref[hh] for hh in range(hps)]
        r_i = lax.broadcasted_iota(jnp.int32, (blk, blk), 0)
        c_i = lax.broadcasted_iota(jnp.int32, (blk, blk), 1)
        upto = (r_i <= c_i).astype(BF16)
        below = (r_i < c_i).astype(BF16)

        @pl.when(n == 0)
        def _():
            dk_ref[...] = jnp.zeros_like(dk_ref)
            dv_ref[...] = jnp.zeros_like(dv_ref)

        def block(j, carry, strict):
            start = pl.multiple_of(j * blk, blk)
            out = []
            for hh in range(hps):
                run, grun, dq = carry[hh]
                kb = k_ref[hh, pl.ds(start, blk), :]
                vb = v_ref[hh, pl.ds(start, blk), :]
                z, e, lb, lk = _stick_terms(qs[hh], kb, scale, strict)
                w = _keep(strict, jnp.exp(lb + tots[hh] - (run + _tri_sum(lk, upto))))
                dw = lax.dot_general(do16[hh], vb, (((1,), (1,)), ((), ())), preferred_element_type=F32)
                g = w * dw
                before = grun + _tri_sum(g, below)
                inv = 1.0 / (1.0 + e)
                sig = jnp.where(z >= 0, inv, e * inv)
                dz16 = (_keep(strict, g * (1.0 - sig) - sig * before) * scale).astype(BF16)
                dk_ref[hh, pl.ds(start, blk), :] += lax.dot_general(dz16, qs[hh], (((0,), (0,)), ((), ())), preferred_element_type=F32)
                dv_ref[hh, pl.ds(start, blk), :] += lax.dot_general(w.astype(BF16), do16[hh], (((0,), (0,)), ((), ())),
                                                                    preferred_element_type=F32)
                out.append((run + jnp.sum(lk, axis=1, keepdims=True), grun + jnp.sum(g, axis=1, keepdims=True),
                            dq + jnp.dot(dz16, kb, preferred_element_type=F32)))
            return tuple(out)

        zero = jnp.zeros((blk, 1), F32)
        init = tuple((zero, zero, jnp.zeros((blk, dh), F32)) for _ in range(hps))
        res = block(n, lax.fori_loop(0, n, lambda j, carry: block(j, carry, None), init), r_i > c_i)
        for hh in range(hps):
            dq_ref[hh] = res[hh][2]

    qs_ = pl.BlockSpec((hps, blk, dh), lambda g, i: (g, i, 0))
    ks_ = pl.BlockSpec((hps, T, dh), lambda g, i: (g, 0, 0))
    ts_ = pl.BlockSpec((hps, blk, 1), lambda g, i: (g, i, 0))
    shp = jax.ShapeDtypeStruct((H, T, dh), F32)
    return pl.pallas_call(
        body, name=name, grid=(H // hps, T // blk), in_specs=[qs_, ks_, ks_, ts_, qs_], out_specs=[qs_, ks_, ks_],
        out_shape=[shp, shp, shp],
        compiler_params=_params("parallel", "arbitrary"),
    )(q, k, v, tot, do)


def _mix_fwd(outs, lses, *, name, rows=512):
    H, T, dh = outs[0].shape
    rows = _tile(T, rows)

    def body(o0, o1, o2, l0, l1, l2, y_ref):
        ls = [l0[...], l1[...], l2[...]]
        mx = jnp.maximum(jnp.maximum(ls[0], ls[1]), ls[2])
        es = [jnp.exp(x - mx) for x in ls]
        den = es[0] + es[1] + es[2]
        y_ref[...] = (es[0] * o0[...] + es[1] * o1[...] + es[2] * o2[...]) / den

    os_ = pl.BlockSpec((None, rows, dh), lambda h, i: (h, i, 0))
    ls_ = pl.BlockSpec((None, rows, 1), lambda h, i: (h, i, 0))
    return pl.pallas_call(
        body, name=name, grid=(H, T // rows), in_specs=[os_] * 3 + [ls_] * 3, out_specs=os_,
        out_shape=jax.ShapeDtypeStruct((H, T, dh), F32),
        compiler_params=_params("parallel", "parallel"),
    )(*outs, *lses)


def _mix_bwd(outs, lses, dy, *, name, rows=512):
    H, T, dh = outs[0].shape
    rows = _tile(T, rows)

    def body(o0, o1, o2, l0, l1, l2, dy_ref, d0, d1, d2, g0, g1, g2):
        ls = [l0[...], l1[...], l2[...]]
        mx = jnp.maximum(jnp.maximum(ls[0], ls[1]), ls[2])
        es = [jnp.exp(x - mx) for x in ls]
        den = es[0] + es[1] + es[2]
        mix = [e / den for e in es]
        dyv = dy_ref[...]
        dm = [jnp.sum(dyv * o[...], axis=1, keepdims=True) for o in (o0, o1, o2)]
        avg = mix[0] * dm[0] + mix[1] * dm[1] + mix[2] * dm[2]
        for d_ref, g_ref, w, t in zip((d0, d1, d2), (g0, g1, g2), mix, dm):
            d_ref[...] = w * dyv
            g_ref[...] = w * (t - avg)

    os_ = pl.BlockSpec((None, rows, dh), lambda h, i: (h, i, 0))
    ls_ = pl.BlockSpec((None, rows, 1), lambda h, i: (h, i, 0))
    so = jax.ShapeDtypeStruct((H, T, dh), F32)
    sl = jax.ShapeDtypeStruct((H, T, 1), F32)
    res = pl.pallas_call(
        body, name=name, grid=(H, T // rows), in_specs=[os_] * 3 + [ls_] * 3 + [os_], out_specs=[os_] * 3 + [ls_] * 3,
        out_shape=[so] * 3 + [sl] * 3,
        compiler_params=_params("parallel", "parallel"),
    )(*outs, *lses, dy)
    return res[:3], res[3:]


def _position():
    return lax.axis_index("x"), lax.axis_index("y"), lax.axis_index("c")


def _other_chips(x, y):
    return [(1 - x, y), (x, 1 - y), (1 - x, 1 - y)]


HBM_SPEC = pl.BlockSpec(memory_space=pltpu.HBM)
SEM_SPEC = pl.BlockSpec(memory_space=pltpu.SEMAPHORE)
ANY_SPEC = pl.BlockSpec(memory_space=pl.ANY)
SPLIT_COPY = pltpu.CompilerParams(has_side_effects=pltpu.SideEffectType.DATAFLOW_SIDE_EFFECTING)


def _in_hbm(a):
    return pltpu.with_memory_space_constraint(a, pltpu.HBM)


SAME_CORE_OF_OTHER_CHIPS = ((1, 0, 0), (0, 1, 0), (1, 1, 0))
ALL_OTHER_DEVICES = ((0, 0, 1), (1, 0, 0), (0, 1, 0), (1, 1, 0), (1, 0, 1), (0, 1, 1), (1, 1, 1))


def _peer_copies(srcs, lands, send_sems, recv_sems, relations, src_of, dst_of):
    me = _position()
    copies = []
    for w in range(len(srcs)):
        for k, flips in enumerate(relations):
            peer = tuple(1 - p if f else p for p, f in zip(me, flips))
            i = len(relations) * w + k
            copies.append(pltpu.make_async_remote_copy(
                src_ref=src_of(srcs[w], peer), dst_ref=dst_of(lands[w], k, peer, me), send_sem=send_sems[i],
                recv_sem=recv_sems[i], device_id=peer, device_id_type=MESH))
    return copies


def _copies_start(srcs, land_shapes, after, relations, src_of, dst_of, *, name):
    n = len(srcs)
    m = len(relations) * n

    def body(*refs):
        src_refs, land_refs = refs[:n], refs[n:2 * n]
        send_sems, recv_sems = refs[2 * n + 1:2 * n + 1 + m], refs[2 * n + 1 + m:2 * n + 1 + 2 * m]
        token = refs[-1]
        for cp in _peer_copies(src_refs, land_refs, send_sems, recv_sems, relations, src_of, dst_of):
            cp.start()
        token[...] = jnp.zeros_like(token)

    lands = [_in_hbm(lax.empty(s.shape, s.dtype)) for s in land_shapes]
    out_shape = ([pltpu.SemaphoreType.DMA(())] * (2 * m)
                 + [pltpu.HBM(a.shape, a.dtype) for a in srcs] + [pltpu.HBM(s.shape, s.dtype) for s in land_shapes]
                 + [jax.ShapeDtypeStruct((8, 128), F32)])
    res = pl.pallas_call(
        body, name=name, in_specs=[HBM_SPEC] * (2 * n) + [ANY_SPEC],
        out_specs=[SEM_SPEC] * (2 * m) + [HBM_SPEC] * (2 * n) + [pl.BlockSpec(memory_space=pltpu.VMEM)],
        out_shape=out_shape, input_output_aliases={i: 2 * m + i for i in range(2 * n)}, compiler_params=SPLIT_COPY,
    )(*[_in_hbm(a) for a in srcs], *lands, after)
    return (list(res[:m]), list(res[m:2 * m]), list(res[2 * m:2 * m + n]), list(res[2 * m + n:2 * m + 2 * n]), res[-1])


def _copies_wait(started, after, relations, src_of, dst_of, *, name):
    send_sems, recv_sems, srcs, lands, _ = started
    n = len(srcs)
    m = len(relations) * n

    def body(*refs):
        src_refs, land_refs = refs[:n], refs[n:2 * n]
        send_refs, recv_refs = refs[2 * n:2 * n + m], refs[2 * n + m:2 * n + 2 * m]
        for cp in _peer_copies(src_refs, land_refs, send_refs, recv_refs, relations, src_of, dst_of):
            cp.wait_send()
            cp.wait_recv()

    res = pl.pallas_call(
        body, name=name, in_specs=[HBM_SPEC] * (2 * n) + [SEM_SPEC] * (2 * m) + [ANY_SPEC], out_specs=[HBM_SPEC] * (2 * n),
        out_shape=[pltpu.HBM(a.shape, a.dtype) for a in srcs + lands],
        input_output_aliases={i: i for i in range(2 * n)}, compiler_params=SPLIT_COPY,
    )(*srcs, *lands, *send_sems, *recv_sems, after)
    return list(res[:n]), list(res[n:])


def _half_rows(ref, core):
    half = ref.shape[-2] // 2
    return pl.ds(core * half, half)


def _gather_src(src, peer):
    return src.at[_half_rows(src, peer[2]), :]


def _gather_dst(land, k, peer, me):
    return land.at[2 * me[0] + me[1], _half_rows(land, me[2]), :]


def _gather_landed(land, k, peer, me):
    return land.at[2 * peer[0] + peer[1], _half_rows(land, me[2]), :]


def _exchange_src(src, peer):
    return src.at[2 * peer[0] + peer[1], _half_rows(src, peer[2]), :]


def _exchange_dst(land, k, peer, me):
    return land.at[k]


def _forward_halves(bufs, *, name):
    n = len(bufs)

    def body(*refs):
        ins, outs = refs[:n], refs[n:2 * n]
        send_sems, recv_sems = refs[2 * n:]
        x, y, c = _position()
        chips = _other_chips(x, y)

        def copy(w, j, chip, core):
            holder = (x, y, core)
            return pltpu.make_async_remote_copy(src_ref=_gather_landed(ins[w], j, chip, holder),
                                                dst_ref=_gather_landed(outs[w], j, chip, holder),
                                                send_sem=send_sems.at[w, j], recv_sem=recv_sems.at[w, j],
                                                device_id=(x, y, 1 - c), device_id_type=MESH)

        sends = [copy(w, j, chip, c) for w in range(n) for j, chip in enumerate(chips)]
        for cp in sends:
            cp.start()
        for w in range(n):
            for j, chip in enumerate(chips):
                copy(w, j, chip, 1 - c).wait_recv()
        for cp in sends:
            cp.wait_send()

    return pl.pallas_call(
        body, name=name, in_specs=[ANY_SPEC] * n, out_specs=[ANY_SPEC] * n,
        out_shape=[jax.ShapeDtypeStruct(a.shape, a.dtype) for a in bufs], input_output_aliases={w: w for w in range(n)},
        scratch_shapes=[pltpu.SemaphoreType.DMA((n, 3)), pltpu.SemaphoreType.DMA((n, 3))],
    )(*bufs)


def _share_halves(bufs, *, name):
    n = len(bufs)

    def body(*refs):
        ins, outs = refs[:n], refs[n:2 * n]
        send_sems, recv_sems = refs[2 * n:]
        x, y, c = _position()
        sends = []
        for w in range(n):
            sends.append(pltpu.make_async_remote_copy(src_ref=ins[w].at[c], dst_ref=outs[w].at[c], send_sem=send_sems.at[w],
                                                      recv_sem=recv_sems.at[w], device_id=(x, y, 1 - c), device_id_type=MESH))
            sends[-1].start()
        for w in range(n):
            pltpu.make_async_remote_copy(src_ref=ins[w].at[1 - c], dst_ref=outs[w].at[1 - c], send_sem=send_sems.at[w],
                                         recv_sem=recv_sems.at[w], device_id=(x, y, 1 - c), device_id_type=MESH).wait_recv()
        for cp in sends:
            cp.wait_send()

    hbm = pl.BlockSpec(memory_space=pl.ANY)
    return pl.pallas_call(
        body, name=name, in_specs=[hbm] * n, out_specs=[hbm] * n,
        out_shape=[jax.ShapeDtypeStruct(a.shape, a.dtype) for a in bufs],
        input_output_aliases={w: w for w in range(n)},
        scratch_shapes=[pltpu.SemaphoreType.DMA((n,)), pltpu.SemaphoreType.DMA((n,))],
    )(*bufs)


def _all_gather8(v, *, name):
    m, n = v.shape

    def body(v_ref, out_ref, send_sems, recv_sems, local_sem):
        x, y, c = _position()
        me, sibling = (x, y, c), (x, y, 1 - c)
        chips = _other_chips(x, y)

        def rows(px, py, pc):
            return out_ref.at[pl.ds((4 * px + 2 * py + pc) * m, m), :]

        def copy(k, block, to, src=None):
            return pltpu.make_async_remote_copy(src_ref=rows(*block) if src is None else src, dst_ref=rows(*block),
                                                send_sem=send_sems.at[k], recv_sem=recv_sems.at[k], device_id=to, device_id_type=MESH)

        mine = pltpu.make_async_copy(v_ref, rows(*me), local_sem)
        mine.start()
        first = [copy(0, me, sibling, src=v_ref)]
        first += [copy(1 + j, me, (*chip, c), src=v_ref) for j, chip in enumerate(chips)]
        for cp in first:
            cp.start()
        passed = [copy(4 + j, (*chip, c), sibling) for j, chip in enumerate(chips)]
        for j, chip in enumerate(chips):
            copy(1 + j, (*chip, c), me).wait_recv()
            passed[j].start()
        copy(0, sibling, me).wait_recv()
        for j, chip in enumerate(chips):
            copy(4 + j, (*chip, 1 - c), me).wait_recv()
        for cp in first + passed:
            cp.wait_send()
        mine.wait()

    vmem = pl.BlockSpec(memory_space=pltpu.VMEM)
    return pl.pallas_call(
        body, name=name, in_specs=[vmem], out_specs=vmem, out_shape=jax.ShapeDtypeStruct((8 * m, n), v.dtype),
        scratch_shapes=[pltpu.SemaphoreType.DMA((7,)), pltpu.SemaphoreType.DMA((7,)), pltpu.SemaphoreType.DMA],
    )(v)


def _sum_parts(own, landed, where, *, name, rows=256):
    n_peers, half, C = landed.shape
    rows = _tile(half, rows)
    nb = half // rows

    def body(where_ref, own_ref, land_ref, o_ref):
        acc = own_ref[...].astype(F32)
        for k in range(n_peers):
            acc = acc + land_ref[k].astype(F32)
        o_ref[...] = acc

    grid_spec = pltpu.PrefetchScalarGridSpec(
        num_scalar_prefetch=1, grid=(nb,),
        in_specs=[pl.BlockSpec((None, rows, C), lambda i, where_ref: (where_ref[0], where_ref[1] * nb + i, 0)),
                  pl.BlockSpec((n_peers, rows, C), lambda i, where_ref: (0, i, 0))],
        out_specs=pl.BlockSpec((None, rows, C), lambda i, where_ref: (where_ref[1], i, 0)))
    return pl.pallas_call(
        body, name=name, grid_spec=grid_spec, out_shape=jax.ShapeDtypeStruct((2, half, C), F32),
        compiler_params=_params("parallel"),
    )(where, own, landed)


def _sum_slabs(a, *, name, rows=256):
    P, R, C = a.shape
    rows = _tile(R, rows)

    def body(a_ref, o_ref):
        acc = a_ref[0].astype(F32)
        for p in range(1, P):
            acc = acc + a_ref[p].astype(F32)
        o_ref[...] = acc

    return pl.pallas_call(
        body, name=name, grid=(R // rows,), in_specs=[pl.BlockSpec((P, rows, C), lambda i: (0, i, 0))],
        out_specs=pl.BlockSpec((rows, C), lambda i: (i, 0)),
        out_shape=jax.ShapeDtypeStruct((R, C), F32), compiler_params=_params("parallel"),
    )(a)


def _reduce_scatter_start(grads, *, name):
    lands = [jax.ShapeDtypeStruct((len(ALL_OTHER_DEVICES), g.shape[1] // 2, g.shape[2]), g.dtype) for g in grads]
    return _copies_start(grads, lands, grads[0], ALL_OTHER_DEVICES, _exchange_src, _exchange_dst, name=name + "_start")


def _reduce_scatter_finish(started, after, *, name):
    core = lax.axis_index("c").astype(jnp.int32)
    chip = (2 * lax.axis_index("x") + lax.axis_index("y")).astype(jnp.int32)
    where = jnp.stack([chip, core])
    sums, landed = _copies_wait(started, after, ALL_OTHER_DEVICES, _exchange_src, _exchange_dst, name=name + "_wait")
    reduced = [_sum_parts(s, a, where, name=f"{name}_sum{w}") for w, (s, a) in enumerate(zip(sums, landed))]
    both = _share_halves(reduced, name=name + "_share")
    return [b.reshape(2 * b.shape[1], b.shape[2]) for b in both]


def _to_heads(t, heads, dil=1):
    S = t.shape[0]
    d = t.shape[1] // heads
    return jnp.transpose(t.reshape(S // dil, dil, heads, d), (2, 1, 0, 3)).reshape(heads, S, d)


def _from_heads(t, dil=1):
    heads, S, d = t.shape
    return jnp.transpose(t.reshape(heads, dil, S // dil, d), (2, 1, 0, 3)).reshape(S, heads * d)


def _unstride(t, dil):
    heads, S, d = t.shape
    return jnp.transpose(t.reshape(heads, dil, S // dil, d), (0, 2, 1, 3)).reshape(heads, S, d)


def _restride(t, dil):
    heads, S, d = t.shape
    return jnp.transpose(t.reshape(heads, S // dil, dil, d), (0, 2, 1, 3)).reshape(heads, S, d)


def _pad_cols(t, width, padded):
    S = t.shape[0]
    return jnp.pad(t.reshape(S, N_CHIPS, width), ((0, 0), (0, 0), (0, padded - width))).reshape(S, N_CHIPS * padded)


def _unpad_cols(t, width, padded):
    S = t.shape[0]
    return t.reshape(S, N_CHIPS, padded)[:, :, :width].reshape(S, N_CHIPS * width)


def _alibi(n):
    return 2.0 ** (-8.0 * jnp.arange(1, n + 1, dtype=F32) / n)


B_KW = [dict(hps=4, nb_seq=SEQ // d // ATT_BLK, n_back=w // d, scale=1.0 / math.sqrt(HEAD_DIM), dist_scale=d)
        for w, d in B_PATTERNS]
A_KW = dict(hps=A_Q_HEADS // A_KV_HEADS, nb_seq=SEQ // ATT_BLK, n_back=A_WINDOW - 1, scale=1.0 / math.sqrt(HEAD_DIM), dist_scale=1)
D_KW = dict(blk=256, hps=2, scale=1.0 / math.sqrt(D_NOPE + D_ROPE))
C_KW = dict(blk=256, hps=2, scale=1.0 / math.sqrt(HEAD_DIM))


def _front_block(t):
    return jnp.pad(t, ((0, 0), (ATT_BLK, 0), (0, 0)))


def _rope_tables(reps):
    inv_freq = ROPE_BASE ** (-jnp.arange(0, D_ROPE, 2, dtype=F32) / D_ROPE)
    ang = jnp.arange(SEQ, dtype=F32)[:, None] * inv_freq[None, :]
    return jnp.tile(jnp.cos(ang), (1, reps)), jnp.tile(jnp.sin(ang), (1, reps))


def _mlp_fwd(x, x16, w1, w2, g, b, tag):
    hid, act = _matmul(x16, w1, mode="nn", out_dtype=BF16, epilogue="relu2", name=f"mlp{tag}_up")
    m = _matmul(act, w2, mode="nn", out_dtype=F32, name=f"mlp{tag}_down")
    y, y16, u = _norm_fwd(x, m, g, b, center=True, eps=LN_EPS, alpha=ALPHA, name=f"ln2_{tag}")
    return y, y16, (x16, hid, act, u)


def _mlp_bwd(dy, saved, w1, w2, g, tag, after=None):
    x16, hid, act, u = saved
    du, du16, dg, db = _norm_bwd(dy, u, g, center=True, eps=LN_EPS, name=f"ln2_{tag}_bwd", after=after)
    dw2 = _matmul_tn(act, du16, shards=1, name=f"mlp{tag}_dw2")
    dhid = _matmul(du16, w2, mode="nt", out_dtype=BF16, epilogue="drelu2", extra=hid, name=f"mlp{tag}_dact")
    dw1 = _matmul_tn(x16, dhid, shards=N_CHIPS, name=f"mlp{tag}_dw1")
    dx = _matmul(dhid, w1, mode="nt", out_dtype=F32, epilogue="add", extra=du, alpha=ALPHA, name=f"mlp{tag}_dx")
    return dx, dw1, dw2, dg, db


def _even_fwd(x16, w_in, sinks):
    h = _unpad_cols(_matmul(x16, w_in, mode="nn", out_dtype=BF16, name="even_in"), EVEN_IN // N_CHIPS, EVEN_IN_PAD)
    qa = _to_heads(h[:, :A_Q_W], A_Q_HEADS)
    ka = _front_block(_to_heads(h[:, A_Q_W:A_Q_W + A_KV_W], A_KV_HEADS))
    va = _front_block(_to_heads(h[:, A_Q_W + A_KV_W:A_Q_W + 2 * A_KV_W], A_KV_HEADS))
    slopes_a, slopes_b = _alibi(A_Q_HEADS), _alibi(B_HEADS)
    oa, lse_a = _band_fwd(qa, ka, va, slopes_a, sinks, name="swa_fwd", **A_KW)
    base = A_Q_W + 2 * A_KV_W
    qkv_b, outs, lses = [], [], []
    for gi, (_, dil) in enumerate(B_PATTERNS):
        cols = h[:, base + gi * 3 * B_W:base + (gi + 1) * 3 * B_W]
        q, k, v = (_to_heads(cols[:, i * B_W:(i + 1) * B_W], B_HEADS, dil) for i in range(3))
        k, v = _front_block(k), _front_block(v)
        o, lse = _band_fwd(q, k, v, slopes_b, None, name=f"dil{gi}_fwd", **B_KW[gi])
        qkv_b.append((q, k, v, o, lse))
        outs.append(_unstride(o, dil))
        lses.append(_unstride(lse, dil))
    ob = _mix_fwd(outs, lses, name="dil_mix")
    y16 = jnp.concatenate([_from_heads(oa), _from_heads(ob)], axis=-1).astype(BF16)
    return y16, (x16, qa, ka, va, oa, lse_a, qkv_b, outs, lses, y16)


def _even_bwd(dmix16, du, saved, w_in, sinks, w_out):
    x16, qa, ka, va, oa, lse_a, qkv_b, outs, lses, y16 = saved
    slopes_a, slopes_b = _alibi(A_Q_HEADS), _alibi(B_HEADS)
    dw_out = _matmul_tn(y16, dmix16, shards=N_CHIPS, name="even_dwout")
    dy = _matmul(dmix16, w_out, mode="nt", out_dtype=F32, name="even_dy")
    doa = _to_heads(dy[:, :A_Q_W], A_Q_HEADS)
    dob = _to_heads(dy[:, A_Q_W:], B_HEADS)
    dqa, dka, dva, dsink = _band_bwd(qa, ka, va, oa, doa, lse_a, None, slopes_a, sinks, name="swa_bwd", **A_KW)
    douts, dlses = _mix_bwd(outs, lses, dob, name="dil_mix_bwd")
    parts = [_from_heads(dqa), _from_heads(dka[:, ATT_BLK:]), _from_heads(dva[:, ATT_BLK:])]
    for gi, (_, dil) in enumerate(B_PATTERNS):
        q, k, v, o, lse = qkv_b[gi]
        dq, dk, dv = _band_bwd(q, k, v, o, _restride(douts[gi], dil), lse, _restride(dlses[gi], dil), slopes_b, None,
                               name=f"dil{gi}_bwd", **B_KW[gi])
        parts += [_from_heads(dq, dil), _from_heads(dk[:, ATT_BLK:], dil), _from_heads(dv[:, ATT_BLK:], dil)]
    dh16 = _pad_cols(jnp.concatenate(parts, axis=-1), EVEN_IN // N_CHIPS, EVEN_IN_PAD).astype(BF16)
    dw_in = _matmul_tn(x16, dh16, shards=N_CHIPS, name="even_dwin")
    dx = _matmul(dh16, w_in, mode="nt", out_dtype=F32, epilogue="add", extra=du, alpha=ALPHA, name="even_dx")
    return dx, dw_in, dsink[:, 0, 0], dw_out


def _odd_fwd(x16, w_in, qn_g, kvn_g, w_uq, w_ukv, w_out):
    S = x16.shape[0]
    h = _unpad_cols(_matmul(x16, w_in, mode="nn", out_dtype=F32, name="odd_in"), ODD_IN // N_CHIPS, ODD_IN_PAD)
    qc, kc, vc = (_to_heads(h[:, i * C_W:(i + 1) * C_W].astype(BF16), C_HEADS) for i in range(3))
    cq = h[:, 3 * C_W:3 * C_W + D_Q_RANK]
    ckv = h[:, 3 * C_W + D_Q_RANK:3 * C_W + D_Q_RANK + D_KV_RANK]
    kr = h[:, 3 * C_W + D_Q_RANK + D_KV_RANK:]
    oc, tot = _stick_fwd(qc, kc, vc, name="stick_fwd", **C_KW)
    _, cqn16, _ = _norm_fwd(cq, None, qn_g, None, center=False, eps=RMS_EPS, alpha=1.0, name="q_rms")
    _, ckvn16, _ = _norm_fwd(ckv, None, kvn_g, None, center=False, eps=RMS_EPS, alpha=1.0, name="kv_rms")
    q = _matmul(cqn16, w_uq, mode="nn", out_dtype=F32, name="mla_uq").reshape(S, D_HEADS, D_NOPE + D_ROPE)
    kv = _matmul(ckvn16, w_ukv, mode="nn", out_dtype=F32, name="mla_ukv").reshape(S, D_HEADS, D_NOPE + D_V)
    half = D_ROPE // 2
    q_rope = q[..., D_NOPE:]
    x1 = jnp.concatenate([q_rope[..., :half].reshape(S, D_HEADS * half), kr[:, :half]], axis=-1)
    x2 = jnp.concatenate([q_rope[..., half:].reshape(S, D_HEADS * half), kr[:, half:]], axis=-1)
    cos, sin = _rope_tables(D_HEADS + 1)
    y1, y2 = _rope(x1[None], x2[None], cos, sin, name="rope_fwd")
    nq = D_HEADS * half
    q_rot = jnp.concatenate([y1[:, :nq].reshape(S, D_HEADS, half), y2[:, :nq].reshape(S, D_HEADS, half)], axis=-1)
    kr_rot = jnp.concatenate([y1[:, nq:], y2[:, nq:]], axis=-1)
    qd = jnp.transpose(jnp.concatenate([q[..., :D_NOPE], q_rot], axis=-1), (1, 0, 2)).astype(BF16)
    kd = jnp.transpose(jnp.concatenate([kv[..., :D_NOPE], jnp.broadcast_to(kr_rot[:, None, :], (S, D_HEADS, D_ROPE))], axis=-1),
                       (1, 0, 2)).astype(BF16)
    vd = jnp.transpose(kv[..., D_NOPE:], (1, 0, 2)).astype(BF16)
    od, lse_d = _causal_fwd(qd, kd, vd, name="mla_fwd", **D_KW)
    y16 = jnp.concatenate([_from_heads(oc), _from_heads(od)], axis=-1).astype(BF16)
    mixed = _matmul(y16, w_out, mode="nn", out_dtype=F32, name="odd_out")
    return mixed, (x16, qc, kc, vc, tot, cq, ckv, cqn16, ckvn16, qd, kd, vd, od, lse_d, y16)


def _odd_bwd(dmix16, du, saved, w_in, qn_g, kvn_g, w_uq, w_ukv, w_out):
    x16, qc, kc, vc, tot, cq, ckv, cqn16, ckvn16, qd, kd, vd, od, lse_d, y16 = saved
    S = x16.shape[0]
    half = D_ROPE // 2
    dw_out = _matmul_tn(y16, dmix16, shards=1, name="odd_dwout")
    dy = _matmul(dmix16, w_out, mode="nt", out_dtype=F32, name="odd_dy")
    doc = _to_heads(dy[:, :C_W], C_HEADS)
    dod = _to_heads(dy[:, C_W:], D_HEADS)
    dqc, dkc, dvc = _stick_bwd(qc, kc, vc, tot, doc, name="stick_bwd", **C_KW)
    dqd, dkd, dvd = _causal_bwd(qd, kd, vd, od, dod, lse_d, name="mla_bwd", **D_KW)
    cos, sin = _rope_tables(D_HEADS + 1)
    nq = D_HEADS * half
    gq1 = jnp.transpose(dqd[..., D_NOPE:D_NOPE + half], (1, 0, 2)).reshape(1, S, nq)
    gq2 = jnp.transpose(dqd[..., D_NOPE + half:], (1, 0, 2)).reshape(1, S, nq)
    dq1, dq2 = _rope(gq1, gq2, cos[:, :nq], -sin[:, :nq], name="rope_bwd_q")
    dk1, dk2 = _rope(dkd[..., D_NOPE:D_NOPE + half], dkd[..., D_NOPE + half:], cos[:, nq:], -sin[:, nq:], name="rope_bwd_k")
    dq_full = jnp.concatenate([jnp.transpose(dqd[..., :D_NOPE], (1, 0, 2)), dq1.reshape(S, D_HEADS, half),
                               dq2.reshape(S, D_HEADS, half)], axis=-1).reshape(S, D_HEADS * (D_NOPE + D_ROPE)).astype(BF16)
    dkv_full = jnp.concatenate([jnp.transpose(dkd[..., :D_NOPE], (1, 0, 2)), jnp.transpose(dvd, (1, 0, 2))],
                               axis=-1).reshape(S, D_HEADS * (D_NOPE + D_V)).astype(BF16)
    dw_uq = _matmul_tn(cqn16, dq_full, shards=N_CHIPS, name="mla_dwuq")
    dw_ukv = _matmul_tn(ckvn16, dkv_full, shards=N_CHIPS, name="mla_dwukv")
    dcqn = _matmul(dq_full, w_uq, mode="nt", out_dtype=F32, name="mla_dcq")
    dckvn = _matmul(dkv_full, w_ukv, mode="nt", out_dtype=F32, name="mla_dckv")
    dcq, _, dqn_g, _ = _norm_bwd(dcqn, cq, qn_g, center=False, eps=RMS_EPS, name="q_rms_bwd")
    dckv, _, dkvn_g, _ = _norm_bwd(dckvn, ckv, kvn_g, center=False, eps=RMS_EPS, name="kv_rms_bwd")
    dh = jnp.concatenate([_from_heads(dqc), _from_heads(dkc), _from_heads(dvc), dcq, dckv, dk1, dk2], axis=-1)
    dh16 = _pad_cols(dh, ODD_IN // N_CHIPS, ODD_IN_PAD).astype(BF16)
    dw_in = _matmul_tn(x16, dh16, shards=N_CHIPS, name="odd_dwin")
    dx = _matmul(dh16, w_in, mode="nt", out_dtype=F32, epilogue="add", extra=du, alpha=ALPHA, name="odd_dx")
    return dx, dw_in, dqn_g[0], dkvn_g[0], dw_uq, dw_ukv, dw_out


WEIGHT_GROUPS = (("even_w_in",), ("even_w_out", "mlp_w1_0", "mlp_w2_0"), ("odd_w_in", "odd_w_uq", "odd_w_ukv", "odd_w_out"),
                 ("mlp_w1_1", "mlp_w2_1"))
GRAD_GROUPS = (("mlp_w2_1", "mlp_w1_1"), ("odd_w_out", "odd_w_uq", "odd_w_ukv", "odd_w_in"), ("mlp_w2_0", "mlp_w1_0"),
               ("even_w_out", "even_w_in"))


def _local_step(x, target, small, weights_for, send_grads, total_loss):
    def by_rows(t, rows):
        return t.reshape(N_CHIPS, rows // N_CHIPS, D_MODEL)

    x16 = x.astype(BF16)
    w = dict(weights_for(0, x16))
    y16, sv_e = _even_fwd(x16, w["even_w_in"], small["even_sinks"])
    w.update(weights_for(1, y16))
    mixed0 = _matmul(y16, w["even_w_out"], mode="nn", out_dtype=F32, name="even_out")
    x1, x1_16, u01 = _norm_fwd(x, mixed0, small["ln1_g"][0], small["ln1_b"][0], center=True, eps=LN_EPS, alpha=ALPHA, name="ln1_0")
    w1_0, w2_0 = w["mlp_w1_0"], w["mlp_w2_0"].reshape(1, D_FF, D_MODEL)
    x2, x2_16, sv_m0 = _mlp_fwd(x1, x1_16, w1_0, w2_0, small["ln2_g"][0], small["ln2_b"][0], 0)
    w.update(weights_for(2, x2_16))
    odd_w = (w["odd_w_in"], small["odd_q_norm_g"], small["odd_kv_norm_g"], w["odd_w_uq"], w["odd_w_ukv"],
             w["odd_w_out"].reshape(1, D_MODEL, D_MODEL))
    mixed1, sv_o = _odd_fwd(x2_16, *odd_w)
    x3, x3_16, u11 = _norm_fwd(x2, mixed1, small["ln1_g"][1], small["ln1_b"][1], center=True, eps=LN_EPS, alpha=ALPHA, name="ln1_1")
    w.update(weights_for(3, x3_16))
    w1_1, w2_1 = w["mlp_w1_1"], w["mlp_w2_1"].reshape(1, D_FF, D_MODEL)
    x4, _, sv_m1 = _mlp_fwd(x3, x3_16, w1_1, w2_1, small["ln2_g"][1], small["ln2_b"][1], 1)
    dy, loss_row = _loss_head(x4, target, name="loss_head")
    loss = total_loss(loss_row)

    dx3, dw1_1, dw2_1, dln2g_1, dln2b_1 = _mlp_bwd(dy, sv_m1, w1_1, w2_1, small["ln2_g"][1], 1, after=loss.reshape(1, 1))
    sent = send_grads(0, dict(mlp_w2_1=by_rows(dw2_1, D_FF), mlp_w1_1=dw1_1))
    du, du16, dln1g_1, dln1b_1 = _norm_bwd(dx3, u11, small["ln1_g"][1], center=True, eps=LN_EPS, name="ln1_1_bwd", after=sent)
    dx2, dwin_o, dqn_g, dkvn_g, dwuq, dwukv, dwout_o = _odd_bwd(du16, du, sv_o, *odd_w)
    sent = send_grads(1, dict(odd_w_out=by_rows(dwout_o, D_MODEL), odd_w_uq=dwuq, odd_w_ukv=dwukv, odd_w_in=dwin_o))
    dx1, dw1_0, dw2_0, dln2g_0, dln2b_0 = _mlp_bwd(dx2, sv_m0, w1_0, w2_0, small["ln2_g"][0], 0, after=sent)
    sent = send_grads(2, dict(mlp_w2_0=by_rows(dw2_0, D_FF), mlp_w1_0=dw1_0))
    du, du16, dln1g_0, dln1b_0 = _norm_bwd(dx1, u01, small["ln1_g"][0], center=True, eps=LN_EPS, name="ln1_0_bwd", after=sent)
    dx0, dwin_e, dsinks, dwout_e = _even_bwd(du16, du, sv_e, w["even_w_in"], small["even_sinks"], w["even_w_out"])
    send_grads(3, dict(even_w_out=dwout_e, even_w_in=dwin_e))

    sm = dict(even_sinks=dsinks, odd_q_norm_g=dqn_g, odd_kv_norm_g=dkvn_g,
              ln1_g=jnp.concatenate([dln1g_0, dln1g_1]), ln1_b=jnp.concatenate([dln1b_0, dln1b_1]),
              ln2_g=jnp.concatenate([dln2g_0, dln2g_1]), ln2_b=jnp.concatenate([dln2b_0, dln2b_1]))
    return loss, dx0, sm


SMALL_ORDER = ("ln1_g", "ln1_b", "ln2_g", "ln2_b", "even_sinks", "odd_q_norm_g", "odd_kv_norm_g")
SMALL_COLS = 2176


def _pack_small(parts):
    flat = jnp.concatenate([p.reshape(-1) for p in parts])
    return jnp.pad(flat, (0, 8 * SMALL_COLS - flat.shape[0])).reshape(8, SMALL_COLS)


def _unpack_small(packed, shapes):
    flat = packed.reshape(-1)
    out, pos = [], 0
    for s in shapes:
        n = math.prod(s)
        out.append(flat[pos:pos + n].reshape(s))
        pos += n
    return out


def kernel(x, even_w_in, even_sinks, even_w_out, odd_w_in, odd_q_norm_g, odd_kv_norm_g, odd_w_uq, odd_w_ukv, odd_w_out, ln1_g, ln1_b, mlp_w1, mlp_w2, ln2_g, ln2_b, loss_target, m_even_w_in, m_even_sinks, m_even_w_out, m_odd_w_in, m_odd_q_norm_g, m_odd_kv_norm_g, m_odd_w_uq, m_odd_w_ukv, m_odd_w_out, m_ln1_g, m_ln1_b, m_mlp_w1, m_mlp_w2, m_ln2_g, m_ln2_b, v_even_w_in, v_even_sinks, v_even_w_out, v_odd_w_in, v_odd_q_norm_g, v_odd_kv_norm_g, v_odd_w_uq, v_odd_w_ukv, v_odd_w_out, v_ln1_g, v_ln1_b, v_mlp_w1, v_mlp_w2, v_ln2_g, v_ln2_b):
    chip = 2 * lax.axis_index("x") + lax.axis_index("y")
    e_w, o_w = EVEN_IN // N_CHIPS, ODD_IN // N_CHIPS

    shards = dict(
        even_w_in=jnp.pad(even_w_in[0], ((0, 0), (0, EVEN_IN_PAD - e_w))), even_w_out=even_w_out[0],
        odd_w_in=jnp.pad(odd_w_in[0], ((0, 0), (0, ODD_IN_PAD - o_w))), odd_w_uq=odd_w_uq[0], odd_w_ukv=odd_w_ukv[0],
        odd_w_out=odd_w_out[0], mlp_w1_0=mlp_w1[0], mlp_w1_1=mlp_w1[1], mlp_w2_0=mlp_w2[0], mlp_w2_1=mlp_w2[1])
    names = list(shards)
    own16 = {n: shards[n].astype(BF16) for n in names}
    gather_started, token = [], own16[WEIGHT_GROUPS[0][0]]
    for g, group in enumerate(WEIGHT_GROUPS):
        lands = [jax.ShapeDtypeStruct((N_CHIPS,) + own16[n].shape, BF16) for n in group]
        gather_started.append(_copies_start([own16[n] for n in group], lands, token, SAME_CORE_OF_OTHER_CHIPS, _gather_src,
                                            _gather_dst, name=f"gather{g}_start"))
        token = gather_started[-1][-1]
    all_started = token

    def weights_for(g, after):
        group = WEIGHT_GROUPS[g]
        _, landed = _copies_wait(gather_started[g], all_started if g == 0 else after, SAME_CORE_OF_OTHER_CHIPS, _gather_src,
                                 _gather_landed, name=f"gather{g}_wait")
        full = _forward_halves(landed, name=f"gather{g}_forward")
        return {n: lax.dynamic_update_slice(f, own16[n][None], (chip, 0, 0)) for n, f in zip(group, full)}

    grads_started = {}

    def send_grads(g, shares):
        grads_started[g] = _reduce_scatter_start([shares[n] for n in GRAD_GROUPS[g]], name=f"grads{g}")
        return grads_started[g][-1]

    norm_rows = jnp.pad(jnp.concatenate([odd_q_norm_g[0], odd_kv_norm_g[0]])[None], ((0, 7), (0, 64)))
    norm_all = _all_gather8(norm_rows, name="gather_norm_gains")[0::16]
    qn_w, kvn_w = D_Q_RANK // N_CHIPS, D_KV_RANK // N_CHIPS
    small = dict(even_sinks=even_sinks[0], odd_q_norm_g=norm_all[:, :qn_w].reshape(D_Q_RANK),
                 odd_kv_norm_g=norm_all[:, qn_w:qn_w + kvn_w].reshape(D_KV_RANK), ln1_g=ln1_g, ln1_b=ln1_b, ln2_g=ln2_g, ln2_b=ln2_b)

    loss, grad_x, sm = _local_step(x[0], loss_target[0], small, weights_for, send_grads,
                                   lambda row: lax.psum(row[0, 0], ("x", "y", "c")))

    weights = dict(even_w_in=even_w_in, even_sinks=even_sinks, even_w_out=even_w_out, odd_w_in=odd_w_in, odd_q_norm_g=odd_q_norm_g,
                   odd_kv_norm_g=odd_kv_norm_g, odd_w_uq=odd_w_uq, odd_w_ukv=odd_w_ukv, odd_w_out=odd_w_out, ln1_g=ln1_g, ln1_b=ln1_b,
                   mlp_w1=mlp_w1, mlp_w2=mlp_w2, ln2_g=ln2_g, ln2_b=ln2_b)
    m_in = dict(even_w_in=m_even_w_in, even_sinks=m_even_sinks, even_w_out=m_even_w_out, odd_w_in=m_odd_w_in,
                odd_q_norm_g=m_odd_q_norm_g, odd_kv_norm_g=m_odd_kv_norm_g, odd_w_uq=m_odd_w_uq, odd_w_ukv=m_odd_w_ukv,
                odd_w_out=m_odd_w_out, ln1_g=m_ln1_g, ln1_b=m_ln1_b, mlp_w1=m_mlp_w1, mlp_w2=m_mlp_w2, ln2_g=m_ln2_g, ln2_b=m_ln2_b)
    v_in = dict(even_w_in=v_even_w_in, even_sinks=v_even_sinks, even_w_out=v_even_w_out, odd_w_in=v_odd_w_in,
                odd_q_norm_g=v_odd_q_norm_g, odd_kv_norm_g=v_odd_kv_norm_g, odd_w_uq=v_odd_w_uq, odd_w_ukv=v_odd_w_ukv,
                odd_w_out=v_odd_w_out, ln1_g=v_ln1_g, ln1_b=v_ln1_b, mlp_w1=v_mlp_w1, mlp_w2=v_mlp_w2, ln2_g=v_ln2_g, ln2_b=v_ln2_b)
    order = list(weights)
    updated = {}

    def update(n, g2d, layer=0, tag=""):
        shape = weights[n].shape
        as3d = (1, math.prod(shape[:-1]), shape[-1]) if len(shape) == 2 else shape
        res = _adamw(weights[n].reshape(as3d), g2d, m_in[n].reshape(as3d), v_in[n].reshape(as3d), layer=layer,
                     carry=updated.get(n), name=f"adamw_{n}{tag}")
        updated[n] = tuple(res)
        return res[0]

    after = grad_x
    for g, group in enumerate(GRAD_GROUPS):
        for n, r in zip(group, _reduce_scatter_finish(grads_started[g], after, name=f"grads{g}")):
            if n[:-2] in ("mlp_w1", "mlp_w2"):
                after = update(n[:-2], r, layer=int(n[-1]), tag=n[-2:])
            elif n == "even_w_in":
                after = update(n, r[:, :e_w])
            elif n == "odd_w_in":
                after = update(n, r[:, :o_w])
            else:
                after = update(n, r)
        if g == len(GRAD_GROUPS) - 2:
            every = _all_gather8(_pack_small([sm[n] for n in SMALL_ORDER]), name="gather_small_grads")
            sm_sum = _sum_slabs(every.reshape(8, 8, SMALL_COLS), name="sum_small_grads")
            g_ln1g, g_ln1b, g_ln2g, g_ln2b, g_sinks, g_qn, g_kvn = _unpack_small(
                sm_sum, [(DEPTH, D_MODEL)] * 4 + [(1, A_Q_HEADS), (D_Q_RANK,), (D_KV_RANK,)])
            for n, gs in (("even_sinks", g_sinks), ("odd_q_norm_g", lax.dynamic_slice_in_dim(g_qn, chip * qn_w, qn_w)[None]),
                          ("odd_kv_norm_g", lax.dynamic_slice_in_dim(g_kvn, chip * kvn_w, kvn_w)[None]), ("ln1_g", g_ln1g),
                          ("ln1_b", g_ln1b), ("ln2_g", g_ln2g), ("ln2_b", g_ln2b)):
                update(n, gs)
    outs = [[updated[n][k].reshape(weights[n].shape) for n in order] for k in range(4)]
    return (loss, grad_x[None], *outs[0], *outs[1], *outs[2], *outs[3])
```

```python
import functools
import math

import jax
import jax.numpy as jnp
from jax import lax
from jax.experimental import pallas as pl
from jax.experimental.pallas import tpu as pltpu

F32 = jnp.float32
BF16 = jnp.bfloat16
MESH = pl.DeviceIdType.MESH

D_MODEL = 2048
SEQ = 2048
DEPTH = 2
HEAD_DIM = 64
A_Q_HEADS, A_KV_HEADS, A_WINDOW = 16, 2, 128
B_HEADS = 8
B_PATTERNS = ((128, 1), (512, 4), (2048, 16))
C_HEADS = 16
D_HEADS, D_Q_RANK, D_KV_RANK, D_NOPE, D_ROPE, D_V = 16, 512, 256, 64, 32, 64
ROPE_BASE = 10000.0
D_FF = 4 * D_MODEL
LN_EPS = 1e-5
RMS_EPS = 1e-6
ALPHA = (2 * DEPTH) ** 0.25
A_Q_W = A_Q_HEADS * HEAD_DIM
A_KV_W = A_KV_HEADS * HEAD_DIM
B_W = B_HEADS * HEAD_DIM
EVEN_IN = A_Q_W + 2 * A_KV_W + 3 * B_W * len(B_PATTERNS)
EVEN_OUT = A_Q_W + B_W
C_W = C_HEADS * HEAD_DIM
ODD_IN = 3 * C_W + D_Q_RANK + D_KV_RANK + D_ROPE
ADAM_LR, ADAM_B1, ADAM_B2, ADAM_EPS, ADAM_WD, ADAM_STEP = 0.001, 0.9, 0.999, 1e-08, 0.01, 10

N_CHIPS = 4
EVEN_IN_PAD = 1536
ODD_IN_PAD = 1024
ATT_BLK = 128
NEG = -1e30
V7X_VMEM_LIMIT = 48 * 1024 * 1024


def _params(*sem):
    return pltpu.CompilerParams(dimension_semantics=sem, vmem_limit_bytes=V7X_VMEM_LIMIT)


def _tile(n, cap):
    if n <= cap:
        return n
    t = cap - cap % 128
    while n % t:
        t -= 128
    return t


def _matmul(a, b, *, mode, out_dtype, name, epilogue=None, extra=None, alpha=1.0, after=None, tm=1024, tn=1024, tk=2048):
    if mode == "nn":
        M, K = a.shape
        P, _, Np = b.shape
        tm, tn, tk = _tile(M, tm), _tile(Np, tn), _tile(K, tk)
        nj = Np // tn
        grid = (M // tm, P * nj, K // tk)
        a_spec = pl.BlockSpec((tm, tk), lambda i, j, k: (i, k))
        b_spec = pl.BlockSpec((None, tk, tn), lambda i, j, k: (j // nj, k, j % nj))
        o_spec = pl.BlockSpec((tm, tn), lambda i, j, k: (i, j))
        out_shape = (M, P * Np)
        dims = (((1,), (0,)), ((), ()))
    elif mode == "nt":
        M, N = a.shape
        P, K, Np = b.shape
        tm, tn, tk = _tile(M, tm), _tile(K, tn), _tile(Np, tk)
        nkk = Np // tk
        grid = (M // tm, K // tn, P * nkk)
        a_spec = pl.BlockSpec((tm, tk), lambda i, j, k: (i, k))
        b_spec = pl.BlockSpec((None, tn, tk), lambda i, j, k: (k // nkk, j, k % nkk))
        o_spec = pl.BlockSpec((tm, tn), lambda i, j, k: (i, j))
        out_shape = (M, K)
        dims = (((1,), (1,)), ((), ()))
    else:
        raise ValueError(mode)
    n_k = grid[2]
    n_extra = 0 if extra is None else 1
    n_after = 0 if after is None else 1
    n_out = 2 if epilogue == "relu2" else 1

    def body(*refs):
        a_ref, b_ref = refs[:2]
        e_ref = refs[2] if n_extra else None
        outs = refs[2 + n_extra + n_after:2 + n_extra + n_after + n_out]
        part = lax.dot_general(a_ref[...], b_ref[...], dims, preferred_element_type=F32)

        def finish(r):
            if epilogue == "relu2":
                outs[0][...] = r.astype(outs[0].dtype)
                rp = jnp.maximum(r, 0.0)
                outs[1][...] = (rp * rp).astype(outs[1].dtype)
            elif epilogue == "drelu2":
                outs[0][...] = (r * (2.0 * jnp.maximum(e_ref[...].astype(F32), 0.0))).astype(outs[0].dtype)
            elif epilogue == "add":
                outs[0][...] = (r + alpha * e_ref[...].astype(F32)).astype(outs[0].dtype)
            else:
                outs[0][...] = r.astype(outs[0].dtype)

        if n_k == 1:
            finish(part)
        else:
            acc = refs[-1]
            k = pl.program_id(2)

            @pl.when(k == 0)
            def _():
                acc[...] = part

            @pl.when((k > 0) & (k < n_k - 1))
            def _():
                acc[...] += part

            @pl.when(k == n_k - 1)
            def _():
                finish(acc[...] + part)

    in_specs = [a_spec, b_spec] + ([o_spec] if n_extra else []) + ([pl.BlockSpec(memory_space=pl.ANY)] if n_after else [])
    shapes = [jax.ShapeDtypeStruct(out_shape, out_dtype)] * n_out
    res = pl.pallas_call(
        body, name=name, grid=grid, in_specs=in_specs,
        out_specs=[o_spec] * n_out, out_shape=shapes,
        scratch_shapes=[pltpu.VMEM((tm, tn), F32)] if n_k > 1 else [],
        compiler_params=_params("parallel", "parallel", "arbitrary"),
    )(a, b, *([extra] if n_extra else []), *([after] if n_after else []))
    return res if n_out > 1 else res[0]


def _matmul_tn(a, g, *, shards, name, tm=1024, tn=1024):
    M, K = a.shape
    P = shards
    Np = g.shape[1] // P
    tm, tn = _tile(K, tm), _tile(Np, tn)
    nj = Np // tn

    def body(a_ref, g_ref, o_ref):
        o_ref[...] = lax.dot_general(a_ref[...], g_ref[...], (((0,), (0,)), ((), ())), preferred_element_type=F32).astype(BF16)

    return pl.pallas_call(
        body, name=name, grid=(K // tm, P * nj),
        in_specs=[pl.BlockSpec((M, tm), lambda i, j: (0, i)), pl.BlockSpec((M, tn), lambda i, j: (0, j))],
        out_specs=pl.BlockSpec((None, tm, tn), lambda i, j: (j // nj, i, j % nj)),
        out_shape=jax.ShapeDtypeStruct((P, K, Np), BF16),
        compiler_params=_params("parallel", "parallel"),
    )(a, g)


def _norm_fwd(x, res, g, b, *, center, eps, alpha, name, rows=256):
    S, W = x.shape
    has_res = res is not None
    rows = _tile(S, rows)

    def body(*refs):
        x_ref = refs[0]
        r_ref = refs[1] if has_res else None
        g_ref = refs[1 + has_res]
        b_ref = refs[2 + has_res] if center else None
        y_ref, y16_ref, u_ref = refs[-3:]
        u = x_ref[...]
        if has_res:
            u = alpha * u + r_ref[...]
        if center:
            mu = jnp.mean(u, axis=1, keepdims=True)
            xc = u - mu
        else:
            xc = u
        var = jnp.mean(xc * xc, axis=1, keepdims=True)
        y = xc * lax.rsqrt(var + eps) * g_ref[...]
        if center:
            y = y + b_ref[...]
        y_ref[...] = y
        y16_ref[...] = y.astype(BF16)
        u_ref[...] = u

    row_spec = pl.BlockSpec((rows, W), lambda i: (i, 0))
    vec_spec = pl.BlockSpec((1, W), lambda i: (0, 0))
    ins = [x] + ([res] if has_res else []) + [g.reshape(1, W)] + ([b.reshape(1, W)] if center else [])
    specs = [row_spec] + ([row_spec] if has_res else []) + [vec_spec] + ([vec_spec] if center else [])
    return pl.pallas_call(
        body, name=name, grid=(S // rows,), in_specs=specs,
        out_specs=[row_spec, row_spec, row_spec],
        out_shape=[jax.ShapeDtypeStruct((S, W), F32), jax.ShapeDtypeStruct((S, W), BF16), jax.ShapeDtypeStruct((S, W), F32)],
        compiler_params=_params("parallel"),
    )(*ins)


def _norm_bwd(dy, u, g, *, center, eps, name, rows=256, after=None):
    S, W = u.shape
    rows = _tile(S, rows)

    def body(dy_ref, u_ref, g_ref, *rest):
        du_ref, du16_ref, dg_ref, db_ref = rest[-4:]
        i = pl.program_id(0)
        uu = u_ref[...]
        dyv = dy_ref[...]
        if center:
            xc = uu - jnp.mean(uu, axis=1, keepdims=True)
        else:
            xc = uu
        rstd = lax.rsqrt(jnp.mean(xc * xc, axis=1, keepdims=True) + eps)
        xhat = xc * rstd
        dxh = dyv * g_ref[...]
        c2 = jnp.mean(dxh * xhat, axis=1, keepdims=True)
        du = dxh - xhat * c2
        if center:
            du = du - jnp.mean(dxh, axis=1, keepdims=True)
        du = du * rstd
        du_ref[...] = du
        du16_ref[...] = du.astype(BF16)

        @pl.when(i == 0)
        def _():
            dg_ref[...] = jnp.zeros_like(dg_ref)
            db_ref[...] = jnp.zeros_like(db_ref)

        dg_ref[...] += jnp.sum(dyv * xhat, axis=0, keepdims=True)
        db_ref[...] += jnp.sum(dyv, axis=0, keepdims=True)

    row_spec = pl.BlockSpec((rows, W), lambda i: (i, 0))
    vec_spec = pl.BlockSpec((1, W), lambda i: (0, 0))
    return pl.pallas_call(
        body, name=name, grid=(S // rows,),
        in_specs=[row_spec, row_spec, vec_spec] + ([] if after is None else [pl.BlockSpec(memory_space=pl.ANY)]),
        out_specs=[row_spec, row_spec, vec_spec, vec_spec],
        out_shape=[jax.ShapeDtypeStruct((S, W), F32), jax.ShapeDtypeStruct((S, W), BF16),
                   jax.ShapeDtypeStruct((1, W), F32), jax.ShapeDtypeStruct((1, W), F32)],
        compiler_params=_params("arbitrary"),
    )(dy, u, g.reshape(1, W), *([] if after is None else [after]))


def _loss_head(y, target, *, name, rows=256):
    S, W = y.shape
    rows = _tile(S, rows)

    def body(y_ref, t_ref, dy_ref, l_ref):
        i = pl.program_id(0)
        e = y_ref[...] - t_ref[...]
        dy_ref[...] = e * (1.0 / W)

        @pl.when(i == 0)
        def _():
            l_ref[...] = jnp.zeros_like(l_ref)

        l_ref[...] += jnp.full(l_ref.shape, 0.5 / W * jnp.sum(e * e), F32)

    row_spec = pl.BlockSpec((rows, W), lambda i: (i, 0))
    return pl.pallas_call(
        body, name=name, grid=(S // rows,), in_specs=[row_spec, row_spec],
        out_specs=[row_spec, pl.BlockSpec((1, 128), lambda i: (0, 0))],
        out_shape=[jax.ShapeDtypeStruct((S, W), F32), jax.ShapeDtypeStruct((1, 128), F32)],
        compiler_params=_params("arbitrary"),
    )(y, target)


def _adamw(w, g, m, v, *, name, layer=0, carry=None, rows=256):
    L, R, C = w.shape
    rows = _tile(R, rows) if R % 8 == 0 else R
    c1 = 1.0 / (1.0 - ADAM_B1 ** ADAM_STEP)
    c2 = 1.0 / (1.0 - ADAM_B2 ** ADAM_STEP)

    def body(w_ref, g_ref, m_ref, v_ref, *rest):
        go_ref, d_ref, mo_ref, vo_ref = rest[-4:]
        gg = g_ref[...]
        mn = ADAM_B1 * m_ref[...] + (1.0 - ADAM_B1) * gg
        vn = ADAM_B2 * v_ref[...] + (1.0 - ADAM_B2) * (gg * gg)
        go_ref[...] = gg
        d_ref[...] = -ADAM_LR * ((mn * c1) / (jnp.sqrt(vn * c2) + ADAM_EPS) + ADAM_WD * w_ref[...])
        mo_ref[...] = mn
        vo_ref[...] = vn

    slab = pl.BlockSpec((None, rows, C), lambda i: (layer, i, 0))
    shp = jax.ShapeDtypeStruct((L, R, C), F32)
    carried = [] if carry is None else list(carry)
    return pl.pallas_call(
        body, name=name, grid=(R // rows,),
        in_specs=[slab, pl.BlockSpec((rows, C), lambda i: (i, 0)), slab, slab] + [pl.BlockSpec(memory_space=pl.ANY)] * len(carried),
        out_specs=[slab] * 4, out_shape=[shp] * 4, input_output_aliases={4 + k: k for k in range(len(carried))},
        compiler_params=_params("parallel"),
    )(w, g, m, v, *carried)


def _rope(x1, x2, cos, sin, *, name, rows=512):
    H, S, W = x1.shape
    rows = _tile(S, rows)

    def body(x1_ref, x2_ref, c_ref, s_ref, y1_ref, y2_ref):
        t1 = jnp.sum(x1_ref[...], axis=0)
        t2 = jnp.sum(x2_ref[...], axis=0)
        c = c_ref[...]
        s = s_ref[...]
        y1_ref[...] = t1 * c - t2 * s
        y2_ref[...] = t1 * s + t2 * c

    xs = pl.BlockSpec((H, rows, W), lambda i: (0, i, 0))
    ts = pl.BlockSpec((rows, W), lambda i: (i, 0))
    shp = jax.ShapeDtypeStruct((S, W), F32)
    return pl.pallas_call(
        body, name=name, grid=(S // rows,), in_specs=[xs, xs, ts, ts], out_specs=[ts, ts], out_shape=[shp, shp],
        compiler_params=_params("parallel"),
    )(x1, x2, cos, sin)


def _band_mask(blk, n, n_back):
    row = lax.broadcasted_iota(jnp.int32, (blk, 2 * blk), 0)
    col = lax.broadcasted_iota(jnp.int32, (blk, 2 * blk), 1)
    rel = blk + row - col
    return rel, (rel >= 0) & (rel <= n_back) & ((col >= blk) | (n >= 1))


def _band_fwd(q, k, v, slopes, sinks, *, hps, nb_seq, n_back, scale, dist_scale, name):
    blk = ATT_BLK
    Hq, T, d = q.shape
    Hk = k.shape[0]
    group = Hq // Hk
    kps = max(hps // group, 1)
    use_sink = sinks is not None

    def body(*refs):
        q_ref, k_ref, v_ref, slope_ref = refs[:4]
        sink_ref = refs[4] if use_sink else None
        o_ref, lse_ref = refs[-2:]
        i = pl.program_id(1)
        start = pl.multiple_of(i * blk, blk)
        rel, valid = _band_mask(blk, i % nb_seq, n_back)
        relf = rel.astype(F32)
        for hh in range(hps):
            h = pl.program_id(0) * hps + hh
            kk = k_ref[hh // group, pl.ds(start, 2 * blk), :]
            vv = v_ref[hh // group, pl.ds(start, 2 * blk), :]
            s = lax.dot_general(q_ref[hh], kk, (((1,), (1,)), ((), ())), preferred_element_type=F32) * scale
            s = jnp.where(valid, s - (slope_ref[h] * dist_scale) * relf, NEG)
            m = jnp.max(s, axis=1, keepdims=True)
            if use_sink:
                m = jnp.maximum(m, sink_ref[h])
            p = jnp.where(valid, jnp.exp(s - m), 0.0)
            l = jnp.sum(p, axis=1, keepdims=True)
            if use_sink:
                l = l + jnp.exp(sink_ref[h] - m)
            o_ref[hh] = jnp.dot(p.astype(BF16), vv, preferred_element_type=F32) / l
            lse_ref[hh] = m + jnp.log(l)

    smem = pl.BlockSpec(memory_space=pltpu.SMEM)
    qs = pl.BlockSpec((hps, blk, d), lambda g, i: (g, i, 0))
    ks = pl.BlockSpec((kps, T + blk, d), lambda g, i: (g, 0, 0))
    ins = [q, k, v, slopes] + ([sinks] if use_sink else [])
    return pl.pallas_call(
        body, name=name, grid=(Hq // hps, T // blk), in_specs=[qs, ks, ks, smem] + ([smem] if use_sink else []),
        out_specs=[qs, pl.BlockSpec((hps, blk, 1), lambda g, i: (g, i, 0))],
        out_shape=[jax.ShapeDtypeStruct((Hq, T, d), F32), jax.ShapeDtypeStruct((Hq, T, 1), F32)],
        compiler_params=_params("parallel", "parallel"),
    )(*ins)


def _band_bwd(q, k, v, o, do, lse, dlse, slopes, sinks, *, hps, nb_seq, n_back, scale, dist_scale, name):
    blk = ATT_BLK
    Hq, T, d = q.shape
    Hk = k.shape[0]
    group = Hq // Hk
    kps = max(hps // group, 1)
    use_sink, use_dlse = sinks is not None, dlse is not None

    def body(*refs):
        q_ref, k_ref, v_ref, o_ref, do_ref, lse_ref = refs[:6]
        pos = 6
        dlse_ref = refs[pos] if use_dlse else None
        pos += use_dlse
        slope_ref = refs[pos]
        pos += 1
        sink_ref = refs[pos] if use_sink else None
        pos += use_sink
        dq_ref, dk_ref, dv_ref = refs[pos:pos + 3]
        dsink_ref = refs[pos + 3] if use_sink else None
        i = pl.program_id(1)
        start = pl.multiple_of(i * blk, blk)
        rel, valid = _band_mask(blk, i % nb_seq, n_back)
        relf = rel.astype(F32)

        @pl.when(i == 0)
        def _():
            dk_ref[...] = jnp.zeros_like(dk_ref)
            dv_ref[...] = jnp.zeros_like(dv_ref)
            if use_sink:
                dsink_ref[...] = jnp.zeros_like(dsink_ref)

        for kh in range(kps):
            dk_acc = jnp.zeros((2 * blk, d), F32)
            dv_acc = jnp.zeros((2 * blk, d), F32)
            kk = k_ref[kh, pl.ds(start, 2 * blk), :]
            vv = v_ref[kh, pl.ds(start, 2 * blk), :]
            for hh in range(kh * min(group, hps), (kh + 1) * min(group, hps)):
                h = pl.program_id(0) * hps + hh
                qb = q_ref[hh]
                dob = do_ref[hh]
                do16 = dob.astype(BF16)
                lse = lse_ref[hh]
                c = -jnp.sum(dob * o_ref[hh], axis=1, keepdims=True)
                if use_dlse:
                    c = c + dlse_ref[hh]
                s = lax.dot_general(qb, kk, (((1,), (1,)), ((), ())), preferred_element_type=F32) * scale
                s = jnp.where(valid, s - (slope_ref[h] * dist_scale) * relf, NEG)
                p = jnp.where(valid, jnp.exp(s - lse), 0.0)
                dp = lax.dot_general(do16, vv, (((1,), (1,)), ((), ())), preferred_element_type=F32)
                ds16 = (p * (dp + c) * scale).astype(BF16)
                dq_ref[hh] = jnp.dot(ds16, kk, preferred_element_type=F32)
                dk_acc = dk_acc + lax.dot_general(ds16, qb, (((0,), (0,)), ((), ())), preferred_element_type=F32)
                dv_acc = dv_acc + lax.dot_general(p.astype(BF16), do16, (((0,), (0,)), ((), ())), preferred_element_type=F32)
                if use_sink:
                    dsink_ref[hh] += jnp.full((1, 128), jnp.sum(jnp.exp(sink_ref[h] - lse) * c), F32)
            dk_ref[kh, pl.ds(start, 2 * blk), :] += dk_acc
            dv_ref[kh, pl.ds(start, 2 * blk), :] += dv_acc

    smem = pl.BlockSpec(memory_space=pltpu.SMEM)
    qs = pl.BlockSpec((hps, blk, d), lambda g, i: (g, i, 0))
    ls = pl.BlockSpec((hps, blk, 1), lambda g, i: (g, i, 0))
    ks = pl.BlockSpec((kps, T + blk, d), lambda g, i: (g, 0, 0))
    in_specs = [qs, ks, ks, qs, qs, ls] + ([ls] if use_dlse else []) + [smem] + ([smem] if use_sink else [])
    ins = [q, k, v, o, do, lse] + ([dlse] if use_dlse else []) + [slopes] + ([sinks] if use_sink else [])
    out_specs = [qs, ks, ks]
    out_shape = [jax.ShapeDtypeStruct((Hq, T, d), F32), jax.ShapeDtypeStruct((Hk, T + blk, d), F32),
                 jax.ShapeDtypeStruct((Hk, T + blk, d), F32)]
    if use_sink:
        out_specs.append(pl.BlockSpec((hps, 1, 128), lambda g, i: (g, 0, 0)))
        out_shape.append(jax.ShapeDtypeStruct((Hq, 1, 128), F32))
    return pl.pallas_call(
        body, name=name, grid=(Hq // hps, T // blk), in_specs=in_specs, out_specs=out_specs, out_shape=out_shape,
        compiler_params=_params("parallel", "arbitrary"),
    )(*ins)


def _diagonal_mask(blk):
    return lax.broadcasted_iota(jnp.int32, (blk, blk), 0) >= lax.broadcasted_iota(jnp.int32, (blk, blk), 1)


def _causal_fwd(q, k, v, *, blk, hps, scale, name):
    H, T, dqk = q.shape
    dv = v.shape[2]

    def body(q_ref, k_ref, v_ref, o_ref, lse_ref):
        n = pl.program_id(1)
        qs = [q_ref[hh] for hh in range(hps)]

        def block(j, carry, valid):
            start = pl.multiple_of(j * blk, blk)
            out = []
            for hh in range(hps):
                m, l, acc = carry[hh]
                kb = k_ref[hh, pl.ds(start, blk), :]
                vb = v_ref[hh, pl.ds(start, blk), :]
                s = lax.dot_general(qs[hh], kb, (((1,), (1,)), ((), ())), preferred_element_type=F32) * scale
                if valid is not None:
                    s = jnp.where(valid, s, NEG)
                m_new = jnp.maximum(m, jnp.max(s, axis=1, keepdims=True))
                p = _keep(valid, jnp.exp(s - m_new))
                a = jnp.exp(m - m_new)
                out.append((m_new, a * l + jnp.sum(p, axis=1, keepdims=True),
                            a * acc + jnp.dot(p.astype(BF16), vb, preferred_element_type=F32)))
            return tuple(out)

        init = tuple((jnp.full((blk, 1), NEG, F32), jnp.zeros((blk, 1), F32), jnp.zeros((blk, dv), F32)) for _ in range(hps))
        res = block(n, lax.fori_loop(0, n, lambda j, carry: block(j, carry, None), init), _diagonal_mask(blk))
        for hh in range(hps):
            m, l, acc = res[hh]
            o_ref[hh] = acc / l
            lse_ref[hh] = m + jnp.log(l)

    qs_ = pl.BlockSpec((hps, blk, dqk), lambda g, i: (g, i, 0))
    return pl.pallas_call(
        body, name=name, grid=(H // hps, T // blk),
        in_specs=[qs_, pl.BlockSpec((hps, T, dqk), lambda g, i: (g, 0, 0)), pl.BlockSpec((hps, T, dv), lambda g, i: (g, 0, 0))],
        out_specs=[pl.BlockSpec((hps, blk, dv), lambda g, i: (g, i, 0)), pl.BlockSpec((hps, blk, 1), lambda g, i: (g, i, 0))],
        out_shape=[jax.ShapeDtypeStruct((H, T, dv), F32), jax.ShapeDtypeStruct((H, T, 1), F32)],
        compiler_params=_params("parallel", "parallel"),
    )(q, k, v)


def _causal_bwd(q, k, v, o, do, lse, *, blk, hps, scale, name):
    H, T, dqk = q.shape
    dv = v.shape[2]

    def body(q_ref, k_ref, v_ref, o_ref, do_ref, lse_ref, dq_ref, dk_ref, dv_ref):
        n = pl.program_id(1)

        @pl.when(n == 0)
        def _():
            dk_ref[...] = jnp.zeros_like(dk_ref)
            dv_ref[...] = jnp.zeros_like(dv_ref)

        qs = [q_ref[hh] for hh in range(hps)]
        do16 = [do_ref[hh].astype(BF16) for hh in range(hps)]
        lses = [lse_ref[hh] for hh in range(hps)]
        cs = [-jnp.sum(do_ref[hh] * o_ref[hh], axis=1, keepdims=True) for hh in range(hps)]

        def block(j, dqs, valid):
            start = pl.multiple_of(j * blk, blk)
            out = []
            for hh in range(hps):
                kb = k_ref[hh, pl.ds(start, blk), :]
                vb = v_ref[hh, pl.ds(start, blk), :]
                s = lax.dot_general(qs[hh], kb, (((1,), (1,)), ((), ())), preferred_element_type=F32) * scale
                if valid is not None:
                    s = jnp.where(valid, s, NEG)
                p = _keep(valid, jnp.exp(s - lses[hh]))
                dp = lax.dot_general(do16[hh], vb, (((1,), (1,)), ((), ())), preferred_element_type=F32)
                ds16 = (p * (dp + cs[hh]) * scale).astype(BF16)
                dk_ref[hh, pl.ds(start, blk), :] += lax.dot_general(ds16, qs[hh], (((0,), (0,)), ((), ())), preferred_element_type=F32)
                dv_ref[hh, pl.ds(start, blk), :] += lax.dot_general(p.astype(BF16), do16[hh], (((0,), (0,)), ((), ())),
                                                                    preferred_element_type=F32)
                out.append(dqs[hh] + jnp.dot(ds16, kb, preferred_element_type=F32))
            return tuple(out)

        init = tuple(jnp.zeros((blk, dqk), F32) for _ in range(hps))
        res = block(n, lax.fori_loop(0, n, lambda j, dqs: block(j, dqs, None), init), _diagonal_mask(blk))
        for hh in range(hps):
            dq_ref[hh] = res[hh]

    qs_ = pl.BlockSpec((hps, blk, dqk), lambda g, i: (g, i, 0))
    os_ = pl.BlockSpec((hps, blk, dv), lambda g, i: (g, i, 0))
    ls_ = pl.BlockSpec((hps, blk, 1), lambda g, i: (g, i, 0))
    ks_ = pl.BlockSpec((hps, T, dqk), lambda g, i: (g, 0, 0))
    vs_ = pl.BlockSpec((hps, T, dv), lambda g, i: (g, 0, 0))
    return pl.pallas_call(
        body, name=name, grid=(H // hps, T // blk), in_specs=[qs_, ks_, vs_, os_, os_, ls_], out_specs=[qs_, ks_, vs_],
        out_shape=[jax.ShapeDtypeStruct((H, T, dqk), F32), jax.ShapeDtypeStruct((H, T, dqk), F32),
                   jax.ShapeDtypeStruct((H, T, dv), F32)],
        compiler_params=_params("parallel", "arbitrary"),
    )(q, k, v, o, do, lse)


def _band_geometry(blk, nb_seq, n_prev):
    def geo(i):
        seq = i // nb_seq
        n = i % nb_seq
        return seq, n, jnp.maximum(n - n_prev, 0)
    return geo


def _attn_fwd(q, k, v, slopes, sinks, *, blk, nb_seq, n_prev, n_back, scale, dist_scale, name):
    Hq, T, dqk = q.shape
    Hk, _, dv = v.shape
    group = Hq // Hk
    use_bias, use_sink = slopes is not None, sinks is not None
    geo = _band_geometry(blk, nb_seq, n_prev)

    def body(*refs):
        q_ref, k_ref, v_ref = refs[:3]
        slope_ref = refs[3] if use_bias else None
        sink_ref = refs[3 + use_bias] if use_sink else None
        o_ref, lse_ref = refs[-2:]
        h = pl.program_id(0)
        seq, n, lo = geo(pl.program_id(1))
        qb = q_ref[...]
        diff = lax.broadcasted_iota(jnp.int32, (blk, blk), 0) - lax.broadcasted_iota(jnp.int32, (blk, blk), 1)
        if use_sink:
            m0 = jnp.full((blk, 1), sink_ref[h], F32)
            l0 = jnp.ones((blk, 1), F32)
        else:
            m0 = jnp.full((blk, 1), NEG, F32)
            l0 = jnp.zeros((blk, 1), F32)

        def step(j, carry):
            m, l, acc = carry
            start = pl.multiple_of((seq * nb_seq + j) * blk, blk)
            kb = k_ref[pl.ds(start, blk), :]
            vb = v_ref[pl.ds(start, blk), :]
            s = lax.dot_general(qb, kb, (((1,), (1,)), ((), ())), preferred_element_type=F32) * scale
            rel = (n - j) * blk + diff
            valid = (rel >= 0) & (rel <= n_back)
            if use_bias:
                s = s - (slope_ref[h] * dist_scale) * rel.astype(F32)
            s = jnp.where(valid, s, NEG)
            m_new = jnp.maximum(m, jnp.max(s, axis=1, keepdims=True))
            p = jnp.where(valid, jnp.exp(s - m_new), 0.0)
            a = jnp.exp(m - m_new)
            l = a * l + jnp.sum(p, axis=1, keepdims=True)
            acc = a * acc + jnp.dot(p.astype(BF16), vb, preferred_element_type=F32)
            return m_new, l, acc

        m, l, acc = lax.fori_loop(lo, n + 1, step, (m0, l0, jnp.zeros((blk, dv), F32)))
        o_ref[...] = acc / l
        lse_ref[...] = m + jnp.log(l)

    smem = pl.BlockSpec(memory_space=pltpu.SMEM)
    in_specs = [pl.BlockSpec((None, blk, dqk), lambda h, i: (h, i, 0)),
                pl.BlockSpec((None, T, dqk), lambda h, i: (h // group, 0, 0)),
                pl.BlockSpec((None, T, dv), lambda h, i: (h // group, 0, 0))]
    ins = [q, k, v]
    if use_bias:
        in_specs.append(smem)
        ins.append(slopes)
    if use_sink:
        in_specs.append(smem)
        ins.append(sinks)
    return pl.pallas_call(
        body, name=name, grid=(Hq, T // blk), in_specs=in_specs,
        out_specs=[pl.BlockSpec((None, blk, dv), lambda h, i: (h, i, 0)), pl.BlockSpec((None, blk, 1), lambda h, i: (h, i, 0))],
        out_shape=[jax.ShapeDtypeStruct((Hq, T, dv), F32), jax.ShapeDtypeStruct((Hq, T, 1), F32)],
        compiler_params=_params("parallel", "parallel"),
    )(*ins)


def _attn_bwd(q, k, v, o, do, lse, dlse, slopes, sinks, *, blk, nb_seq, n_prev, n_back, scale, dist_scale, name):
    Hq, T, dqk = q.shape
    Hk, _, dv = v.shape
    group = Hq // Hk
    use_bias, use_sink, use_dlse = slopes is not None, sinks is not None, dlse is not None
    geo = _band_geometry(blk, nb_seq, n_prev)

    def body(*refs):
        q_ref, k_ref, v_ref, o_ref, do_ref, lse_ref = refs[:6]
        pos = 6
        dlse_ref = refs[pos] if use_dlse else None
        pos += use_dlse
        slope_ref = refs[pos] if use_bias else None
        pos += use_bias
        sink_ref = refs[pos] if use_sink else None
        pos += use_sink
        dq_ref, dk_ref, dv_ref = refs[pos:pos + 3]
        dsink_ref = refs[pos + 3] if use_sink else None
        h = pl.program_id(0)
        i = pl.program_id(1)
        seq, n, lo = geo(i)
        qb = q_ref[...]
        dob = do_ref[...]
        do16 = dob.astype(BF16)
        lse = lse_ref[...]
        c = -jnp.sum(dob * o_ref[...], axis=1, keepdims=True)
        if use_dlse:
            c = c + dlse_ref[...]
        diff = lax.broadcasted_iota(jnp.int32, (blk, blk), 0) - lax.broadcasted_iota(jnp.int32, (blk, blk), 1)

        @pl.when((h % group == 0) & (i == 0))
        def _():
            dk_ref[...] = jnp.zeros_like(dk_ref)
            dv_ref[...] = jnp.zeros_like(dv_ref)

        def step(j, dq):
            start = pl.multiple_of((seq * nb_seq + j) * blk, blk)
            kb = k_ref[pl.ds(start, blk), :]
            vb = v_ref[pl.ds(start, blk), :]
            s = lax.dot_general(qb, kb, (((1,), (1,)), ((), ())), preferred_element_type=F32) * scale
            rel = (n - j) * blk + diff
            valid = (rel >= 0) & (rel <= n_back)
            if use_bias:
                s = s - (slope_ref[h] * dist_scale) * rel.astype(F32)
            p = jnp.where(valid, jnp.exp(jnp.where(valid, s, NEG) - lse), 0.0)
            dp = lax.dot_general(do16, vb, (((1,), (1,)), ((), ())), preferred_element_type=F32)
            ds16 = (p * (dp + c) * scale).astype(BF16)
            dk_ref[pl.ds(start, blk), :] += lax.dot_general(ds16, qb, (((0,), (0,)), ((), ())), preferred_element_type=F32)
            dv_ref[pl.ds(start, blk), :] += lax.dot_general(p.astype(BF16), do16, (((0,), (0,)), ((), ())), preferred_element_type=F32)
            return dq + jnp.dot(ds16, kb, preferred_element_type=F32)

        dq_ref[...] = lax.fori_loop(lo, n + 1, step, jnp.zeros((blk, dqk), F32))
        if use_sink:
            @pl.when(i == 0)
            def _():
                dsink_ref[...] = jnp.zeros_like(dsink_ref)

            dsink_ref[...] += jnp.full(dsink_ref.shape, jnp.sum(jnp.exp(sink_ref[h] - lse) * c), F32)

    smem = pl.BlockSpec(memory_space=pltpu.SMEM)
    qs = pl.BlockSpec((None, blk, dqk), lambda h, i: (h, i, 0))
    os_ = pl.BlockSpec((None, blk, dv), lambda h, i: (h, i, 0))
    ls = pl.BlockSpec((None, blk, 1), lambda h, i: (h, i, 0))
    ks = pl.BlockSpec((None, T, dqk), lambda h, i: (h // group, 0, 0))
    vs = pl.BlockSpec((None, T, dv), lambda h, i: (h // group, 0, 0))
    in_specs = [qs, ks, vs, os_, os_, ls]
    ins = [q, k, v, o, do, lse]
    if use_dlse:
        in_specs.append(ls)
        ins.append(dlse)
    if use_bias:
        in_specs.append(smem)
        ins.append(slopes)
    if use_sink:
        in_specs.append(smem)
        ins.append(sinks)
    out_specs = [qs, ks, vs]
    out_shape = [jax.ShapeDtypeStruct((Hq, T, dqk), F32), jax.ShapeDtypeStruct((Hk, T, dqk), F32),
                 jax.ShapeDtypeStruct((Hk, T, dv), F32)]
    if use_sink:
        out_specs.append(pl.BlockSpec((None, 1, 128), lambda h, i: (h, 0, 0)))
        out_shape.append(jax.ShapeDtypeStruct((Hq, 1, 128), F32))
    return pl.pallas_call(
        body, name=name, grid=(Hq, T // blk), in_specs=in_specs, out_specs=out_specs, out_shape=out_shape,
        compiler_params=_params("arbitrary", "arbitrary"),
    )(*ins)


def _tri_sum(x, upper):
    hi = x.astype(BF16)
    lo = (x - hi.astype(F32)).astype(BF16)
    return jnp.dot(hi, upper, preferred_element_type=F32) + jnp.dot(lo, upper, preferred_element_type=F32)


def _keep(mask, x):
    return x if mask is None else jnp.where(mask, x, 0.0)


def _stick_terms(qb, kb, scale, strict):
    z = lax.dot_general(qb, kb, (((1,), (1,)), ((), ())), preferred_element_type=F32) * scale
    e = jnp.exp(-jnp.abs(z))
    lb = jnp.minimum(z, 0.0) - jnp.log(1.0 + e)
    return z, e, lb, _keep(strict, lb - z)


def _stick_fwd(q, k, v, *, blk, hps, scale, name):
    H, T, dh = q.shape

    def body(q_ref, k_ref, v_ref, o_ref, tot_ref):
        n = pl.program_id(1)
        qs = [q_ref[hh] for hh in range(hps)]
        r_i = lax.broadcasted_iota(jnp.int32, (blk, blk), 0)
        c_i = lax.broadcasted_iota(jnp.int32, (blk, blk), 1)
        upper = (r_i > c_i).astype(BF16)

        def block(j, carry, strict):
            start = pl.multiple_of(j * blk, blk)
            out = []
            for hh in range(hps):
                run, acc = carry[hh]
                kb = k_ref[hh, pl.ds(start, blk), :]
                vb = v_ref[hh, pl.ds(start, blk), :]
                _, _, lb, lk = _stick_terms(qs[hh], kb, scale, strict)
                w = _keep(strict, jnp.exp(lb + run + _tri_sum(lk, upper)))
                out.append((run + jnp.sum(lk, axis=1, keepdims=True), acc + jnp.dot(w.astype(BF16), vb, preferred_element_type=F32)))
            return tuple(out)

        init = tuple((jnp.zeros((blk, 1), F32), jnp.zeros((blk, dh), F32)) for _ in range(hps))
        res = lax.fori_loop(1, n + 1, lambda t, carry: block(n - t, carry, None), block(n, init, r_i > c_i))
        for hh in range(hps):
            tot_ref[hh], o_ref[hh] = res[hh]

    qs_ = pl.BlockSpec((hps, blk, dh), lambda g, i: (g, i, 0))
    ks_ = pl.BlockSpec((hps, T, dh), lambda g, i: (g, 0, 0))
    ts_ = pl.BlockSpec((hps, blk, 1), lambda g, i: (g, i, 0))
    return pl.pallas_call(
        body, name=name, grid=(H // hps, T // blk), in_specs=[qs_, ks_, ks_], out_specs=[qs_, ts_],
        out_shape=[jax.ShapeDtypeStruct((H, T, dh), F32), jax.ShapeDtypeStruct((H, T, 1), F32)],
        compiler_params=_params("parallel", "parallel"),
    )(q, k, v)


def _stick_bwd(q, k, v, tot, do, *, blk, hps, scale, name):
    H, T, dh = q.shape

    def body(q_ref, k_ref, v_ref, tot_ref, do_ref, dq_ref, dk_ref, dv_ref):
        n = pl.program_id(1)
        qs = [q_ref[hh] for hh in range(hps)]
        do16 = [do_ref[hh].astype(BF16) for hh in range(hps)]
        tots = [tot_ref[hh] for hh in range(hps)]
        r_i = lax.broadcasted_iota(jnp.int32, (blk, blk), 0)
        c_i = lax.broadcasted_iota(jnp.int32, (blk, blk), 1)
        upto = (r_i <= c_i).astype(BF16)
        below = (r_i < c_i).astype(BF16)

        @pl.when(n == 0)
        def _():
            dk_ref[...] = jnp.zeros_like(dk_ref)
            dv_ref[...] = jnp.zeros_like(dv_ref)

        def block(j, carry, strict):
            start = pl.multiple_of(j * blk, blk)
            out = []
            for hh in range(hps):
                run, grun, dq = carry[hh]
                kb = k_ref[hh, pl.ds(start, blk), :]
                vb = v_ref[hh, pl.ds(start, blk), :]
                z, e, lb, lk = _stick_terms(qs[hh], kb, scale, strict)
                w = _keep(strict, jnp.exp(lb + tots[hh] - (run + _tri_sum(lk, upto))))
                dw = lax.dot_general(do16[hh], vb, (((1,), (1,)), ((), ())), preferred_element_type=F32)
                g = w * dw
                before = grun + _tri_sum(g, below)
                inv = 1.0 / (1.0 + e)
                sig = jnp.where(z >= 0, inv, e * inv)
                dz16 = (_keep(strict, g * (1.0 - sig) - sig * before) * scale).astype(BF16)
                dk_ref[hh, pl.ds(start, blk), :] += lax.dot_general(dz16, qs[hh], (((0,), (0,)), ((), ())), preferred_element_type=F32)
                dv_ref[hh, pl.ds(start, blk), :] += lax.dot_general(w.astype(BF16), do16[hh], (((0,), (0,)), ((), ())),
                                                                    preferred_element_type=F32)
                out.append((run + jnp.sum(lk, axis=1, keepdims=True), grun + jnp.sum(g, axis=1, keepdims=True),
                            dq + jnp.dot(dz16, kb, preferred_element_type=F32)))
            return tuple(out)

        zero = jnp.zeros((blk, 1), F32)
        init = tuple((zero, zero, jnp.zeros((blk, dh), F32)) for _ in range(hps))
        res = block(n, lax.fori_loop(0, n, lambda j, carry: block(j, carry, None), init), r_i > c_i)
        for hh in range(hps):
            dq_ref[hh] = res[hh][2]

    qs_ = pl.BlockSpec((hps, blk, dh), lambda g, i: (g, i, 0))
    ks_ = pl.BlockSpec((hps, T, dh), lambda g, i: (g, 0, 0))
    ts_ = pl.BlockSpec((hps, blk, 1), lambda g, i: (g, i, 0))
    shp = jax.ShapeDtypeStruct((H, T, dh), F32)
    return pl.pallas_call(
        body, name=name, grid=(H // hps, T // blk), in_specs=[qs_, ks_, ks_, ts_, qs_], out_specs=[qs_, ks_, ks_],
        out_shape=[shp, shp, shp],
        compiler_params=_params("parallel", "arbitrary"),
    )(q, k, v, tot, do)


def _mix_fwd(outs, lses, *, name, rows=512):
    H, T, dh = outs[0].shape
    rows = _tile(T, rows)

    def body(o0, o1, o2, l0, l1, l2, y_ref):
        ls = [l0[...], l1[...], l2[...]]
        mx = jnp.maximum(jnp.maximum(ls[0], ls[1]), ls[2])
        es = [jnp.exp(x - mx) for x in ls]
        den = es[0] + es[1] + es[2]
        y_ref[...] = (es[0] * o0[...] + es[1] * o1[...] + es[2] * o2[...]) / den

    os_ = pl.BlockSpec((None, rows, dh), lambda h, i: (h, i, 0))
    ls_ = pl.BlockSpec((None, rows, 1), lambda h, i: (h, i, 0))
    return pl.pallas_call(
        body, name=name, grid=(H, T // rows), in_specs=[os_] * 3 + [ls_] * 3, out_specs=os_,
        out_shape=jax.ShapeDtypeStruct((H, T, dh), F32),
        compiler_params=_params("parallel", "parallel"),
    )(*outs, *lses)


def _mix_bwd(outs, lses, dy, *, name, rows=512):
    H, T, dh = outs[0].shape
    rows = _tile(T, rows)

    def body(o0, o1, o2, l0, l1, l2, dy_ref, d0, d1, d2, g0, g1, g2):
        ls = [l0[...], l1[...], l2[...]]
        mx = jnp.maximum(jnp.maximum(ls[0], ls[1]), ls[2])
        es = [jnp.exp(x - mx) for x in ls]
        den = es[0] + es[1] + es[2]
        mix = [e / den for e in es]
        dyv = dy_ref[...]
        dm = [jnp.sum(dyv * o[...], axis=1, keepdims=True) for o in (o0, o1, o2)]
        avg = mix[0] * dm[0] + mix[1] * dm[1] + mix[2] * dm[2]
        for d_ref, g_ref, w, t in zip((d0, d1, d2), (g0, g1, g2), mix, dm):
            d_ref[...] = w * dyv
            g_ref[...] = w * (t - avg)

    os_ = pl.BlockSpec((None, rows, dh), lambda h, i: (h, i, 0))
    ls_ = pl.BlockSpec((None, rows, 1), lambda h, i: (h, i, 0))
    so = jax.ShapeDtypeStruct((H, T, dh), F32)
    sl = jax.ShapeDtypeStruct((H, T, 1), F32)
    res = pl.pallas_call(
        body, name=name, grid=(H, T // rows), in_specs=[os_] * 3 + [ls_] * 3 + [os_], out_specs=[os_] * 3 + [ls_] * 3,
        out_shape=[so] * 3 + [sl] * 3,
        compiler_params=_params("parallel", "parallel"),
    )(*outs, *lses, dy)
    return res[:3], res[3:]


def _position():
    return lax.axis_index("x"), lax.axis_index("y"), lax.axis_index("c")


def _other_chips(x, y):
    return [(1 - x, y), (x, 1 - y), (1 - x, 1 - y)]


HBM_SPEC = pl.BlockSpec(memory_space=pltpu.HBM)
SEM_SPEC = pl.BlockSpec(memory_space=pltpu.SEMAPHORE)
ANY_SPEC = pl.BlockSpec(memory_space=pl.ANY)
SPLIT_COPY = pltpu.CompilerParams(has_side_effects=pltpu.SideEffectType.DATAFLOW_SIDE_EFFECTING)


def _in_hbm(a):
    return pltpu.with_memory_space_constraint(a, pltpu.HBM)


SAME_CORE_OF_OTHER_CHIPS = ((1, 0, 0), (0, 1, 0), (1, 1, 0))
ALL_OTHER_DEVICES = ((0, 0, 1), (1, 0, 0), (0, 1, 0), (1, 1, 0), (1, 0, 1), (0, 1, 1), (1, 1, 1))


def _peer_copies(srcs, lands, send_sems, recv_sems, relations, src_of, dst_of):
    me = _position()
    copies = []
    for w in range(len(srcs)):
        for k, flips in enumerate(relations):
            peer = tuple(1 - p if f else p for p, f in zip(me, flips))
            i = len(relations) * w + k
            copies.append(pltpu.make_async_remote_copy(
                src_ref=src_of(srcs[w], peer), dst_ref=dst_of(lands[w], k, peer, me), send_sem=send_sems[i],
                recv_sem=recv_sems[i], device_id=peer, device_id_type=MESH))
    return copies


def _copies_start(srcs, land_shapes, after, relations, src_of, dst_of, *, name):
    n = len(srcs)
    m = len(relations) * n

    def body(*refs):
        src_refs, land_refs = refs[:n], refs[n:2 * n]
        send_sems, recv_sems = refs[2 * n + 1:2 * n + 1 + m], refs[2 * n + 1 + m:2 * n + 1 + 2 * m]
        token = refs[-1]
        for cp in _peer_copies(src_refs, land_refs, send_sems, recv_sems, relations, src_of, dst_of):
            cp.start()
        token[...] = jnp.zeros_like(token)

    lands = [_in_hbm(lax.empty(s.shape, s.dtype)) for s in land_shapes]
    out_shape = ([pltpu.SemaphoreType.DMA(())] * (2 * m)
                 + [pltpu.HBM(a.shape, a.dtype) for a in srcs] + [pltpu.HBM(s.shape, s.dtype) for s in land_shapes]
                 + [jax.ShapeDtypeStruct((8, 128), F32)])
    res = pl.pallas_call(
        body, name=name, in_specs=[HBM_SPEC] * (2 * n) + [ANY_SPEC],
        out_specs=[SEM_SPEC] * (2 * m) + [HBM_SPEC] * (2 * n) + [pl.BlockSpec(memory_space=pltpu.VMEM)],
        out_shape=out_shape, input_output_aliases={i: 2 * m + i for i in range(2 * n)}, compiler_params=SPLIT_COPY,
    )(*[_in_hbm(a) for a in srcs], *lands, after)
    return (list(res[:m]), list(res[m:2 * m]), list(res[2 * m:2 * m + n]), list(res[2 * m + n:2 * m + 2 * n]), res[-1])


def _copies_wait(started, after, relations, src_of, dst_of, *, name):
    send_sems, recv_sems, srcs, lands, _ = started
    n = len(srcs)
    m = len(relations) * n

    def body(*refs):
        src_refs, land_refs = refs[:n], refs[n:2 * n]
        send_refs, recv_refs = refs[2 * n:2 * n + m], refs[2 * n + m:2 * n + 2 * m]
        for cp in _peer_copies(src_refs, land_refs, send_refs, recv_refs, relations, src_of, dst_of):
            cp.wait_send()
            cp.wait_recv()

    res = pl.pallas_call(
        body, name=name, in_specs=[HBM_SPEC] * (2 * n) + [SEM_SPEC] * (2 * m) + [ANY_SPEC], out_specs=[HBM_SPEC] * (2 * n),
        out_shape=[pltpu.HBM(a.shape, a.dtype) for a in srcs + lands],
        input_output_aliases={i: i for i in range(2 * n)}, compiler_params=SPLIT_COPY,
    )(*srcs, *lands, *send_sems, *recv_sems, after)
    return list(res[:n]), list(res[n:])


def _half_rows(ref, core):
    half = ref.shape[-2] // 2
    return pl.ds(core * half, half)


def _gather_src(src, peer):
    return src.at[_half_rows(src, peer[2]), :]


def _gather_dst(land, k, peer, me):
    return land.at[2 * me[0] + me[1], _half_rows(land, me[2]), :]


def _gather_landed(land, k, peer, me):
    return land.at[2 * peer[0] + peer[1], _half_rows(land, me[2]), :]


def _exchange_src(src, peer):
    return src.at[2 * peer[0] + peer[1], _half_rows(src, peer[2]), :]


def _exchange_dst(land, k, peer, me):
    return land.at[k]


def _forward_halves(bufs, *, name):
    n = len(bufs)

    def body(*refs):
        ins, outs = refs[:n], refs[n:2 * n]
        send_sems, recv_sems = refs[2 * n:]
        x, y, c = _position()
        chips = _other_chips(x, y)

        def copy(w, j, chip, core):
            holder = (x, y, core)
            return pltpu.make_async_remote_copy(src_ref=_gather_landed(ins[w], j, chip, holder),
                                                dst_ref=_gather_landed(outs[w], j, chip, holder),
                                                send_sem=send_sems.at[w, j], recv_sem=recv_sems.at[w, j],
                                                device_id=(x, y, 1 - c), device_id_type=MESH)

        sends = [copy(w, j, chip, c) for w in range(n) for j, chip in enumerate(chips)]
        for cp in sends:
            cp.start()
        for w in range(n):
            for j, chip in enumerate(chips):
                copy(w, j, chip, 1 - c).wait_recv()
        for cp in sends:
            cp.wait_send()

    return pl.pallas_call(
        body, name=name, in_specs=[ANY_SPEC] * n, out_specs=[ANY_SPEC] * n,
        out_shape=[jax.ShapeDtypeStruct(a.shape, a.dtype) for a in bufs], input_output_aliases={w: w for w in range(n)},
        scratch_shapes=[pltpu.SemaphoreType.DMA((n, 3)), pltpu.SemaphoreType.DMA((n, 3))],
    )(*bufs)


def _share_halves(bufs, *, name):
    n = len(bufs)

    def body(*refs):
        ins, outs = refs[:n], refs[n:2 * n]
        send_sems, recv_sems = refs[2 * n:]
        x, y, c = _position()
        sends = []
        for w in range(n):
            sends.append(pltpu.make_async_remote_copy(src_ref=ins[w].at[c], dst_ref=outs[w].at[c], send_sem=send_sems.at[w],
                                                      recv_sem=recv_sems.at[w], device_id=(x, y, 1 - c), device_id_type=MESH))
            sends[-1].start()
        for w in range(n):
            pltpu.make_async_remote_copy(src_ref=ins[w].at[1 - c], dst_ref=outs[w].at[1 - c], send_sem=send_sems.at[w],
                                         recv_sem=recv_sems.at[w], device_id=(x, y, 1 - c), device_id_type=MESH).wait_recv()
        for cp in sends:
            cp.wait_send()

    hbm = pl.BlockSpec(memory_space=pl.ANY)
    return pl.pallas_call(
        body, name=name, in_specs=[hbm] * n, out_specs=[hbm] * n,
        out_shape=[jax.ShapeDtypeStruct(a.shape, a.dtype) for a in bufs],
        input_output_aliases={w: w for w in range(n)},
        scratch_shapes=[pltpu.SemaphoreType.DMA((n,)), pltpu.SemaphoreType.DMA((n,))],
    )(*bufs)


def _all_gather8(v, *, name):
    m, n = v.shape

    def body(v_ref, out_ref, send_sems, recv_sems, local_sem):
        x, y, c = _position()
        me, sibling = (x, y, c), (x, y, 1 - c)
        chips = _other_chips(x, y)

        def rows(px, py, pc):
            return out_ref.at[pl.ds((4 * px + 2 * py + pc) * m, m), :]

        def copy(k, block, to, src=None):
            return pltpu.make_async_remote_copy(src_ref=rows(*block) if src is None else src, dst_ref=rows(*block),
                                                send_sem=send_sems.at[k], recv_sem=recv_sems.at[k], device_id=to, device_id_type=MESH)

        mine = pltpu.make_async_copy(v_ref, rows(*me), local_sem)
        mine.start()
        first = [copy(0, me, sibling, src=v_ref)]
        first += [copy(1 + j, me, (*chip, c), src=v_ref) for j, chip in enumerate(chips)]
        for cp in first:
            cp.start()
        passed = [copy(4 + j, (*chip, c), sibling) for j, chip in enumerate(chips)]
        for j, chip in enumerate(chips):
            copy(1 + j, (*chip, c), me).wait_recv()
            passed[j].start()
        copy(0, sibling, me).wait_recv()
        for j, chip in enumerate(chips):
            copy(4 + j, (*chip, 1 - c), me).wait_recv()
        for cp in first + passed:
            cp.wait_send()
        mine.wait()

    vmem = pl.BlockSpec(memory_space=pltpu.VMEM)
    return pl.pallas_call(
        body, name=name, in_specs=[vmem], out_specs=vmem, out_shape=jax.ShapeDtypeStruct((8 * m, n), v.dtype),
        scratch_shapes=[pltpu.SemaphoreType.DMA((7,)), pltpu.SemaphoreType.DMA((7,)), pltpu.SemaphoreType.DMA],
    )(v)


def _sum_parts(own, landed, where, *, name, rows=256):
    n_peers, half, C = landed.shape
    rows = _tile(half, rows)
    nb = half // rows

    def body(where_ref, own_ref, land_ref, o_ref):
        acc = own_ref[...].astype(F32)
        for k in range(n_peers):
            acc = acc + land_ref[k].astype(F32)
        o_ref[...] = acc

    grid_spec = pltpu.PrefetchScalarGridSpec(
        num_scalar_prefetch=1, grid=(nb,),
        in_specs=[pl.BlockSpec((None, rows, C), lambda i, where_ref: (where_ref[0], where_ref[1] * nb + i, 0)),
                  pl.BlockSpec((n_peers, rows, C), lambda i, where_ref: (0, i, 0))],
        out_specs=pl.BlockSpec((None, rows, C), lambda i, where_ref: (where_ref[1], i, 0)))
    return pl.pallas_call(
        body, name=name, grid_spec=grid_spec, out_shape=jax.ShapeDtypeStruct((2, half, C), F32),
        compiler_params=_params("parallel"),
    )(where, own, landed)


def _sum_slabs(a, *, name, rows=256):
    P, R, C = a.shape
    rows = _tile(R, rows)

    def body(a_ref, o_ref):
        acc = a_ref[0].astype(F32)
        for p in range(1, P):
            acc = acc + a_ref[p].astype(F32)
        o_ref[...] = acc

    return pl.pallas_call(
        body, name=name, grid=(R // rows,), in_specs=[pl.BlockSpec((P, rows, C), lambda i: (0, i, 0))],
        out_specs=pl.BlockSpec((rows, C), lambda i: (i, 0)),
        out_shape=jax.ShapeDtypeStruct((R, C), F32), compiler_params=_params("parallel"),
    )(a)


def _reduce_scatter_start(grads, after=None, *, name):
    lands = [jax.ShapeDtypeStruct((len(ALL_OTHER_DEVICES), g.shape[1] // 2, g.shape[2]), g.dtype) for g in grads]
    return _copies_start(grads, lands, grads[0] if after is None else after, ALL_OTHER_DEVICES, _exchange_src, _exchange_dst,
                         name=name + "_start")


def _reduce_scatter_finish(started, after, *, name):
    core = lax.axis_index("c").astype(jnp.int32)
    chip = (2 * lax.axis_index("x") + lax.axis_index("y")).astype(jnp.int32)
    where = jnp.stack([chip, core])
    sums, landed = _copies_wait(started, after, ALL_OTHER_DEVICES, _exchange_src, _exchange_dst, name=name + "_wait")
    reduced = [_sum_parts(s, a, where, name=f"{name}_sum{w}") for w, (s, a) in enumerate(zip(sums, landed))]
    both = _share_halves(reduced, name=name + "_share")
    return [b.reshape(2 * b.shape[1], b.shape[2]) for b in both]


def _to_heads(t, heads, dil=1):
    S = t.shape[0]
    d = t.shape[1] // heads
    return jnp.transpose(t.reshape(S // dil, dil, heads, d), (2, 1, 0, 3)).reshape(heads, S, d)


def _from_heads(t, dil=1):
    heads, S, d = t.shape
    return jnp.transpose(t.reshape(heads, dil, S // dil, d), (2, 1, 0, 3)).reshape(S, heads * d)


def _unstride(t, dil):
    heads, S, d = t.shape
    return jnp.transpose(t.reshape(heads, dil, S // dil, d), (0, 2, 1, 3)).reshape(heads, S, d)


def _restride(t, dil):
    heads, S, d = t.shape
    return jnp.transpose(t.reshape(heads, S // dil, dil, d), (0, 2, 1, 3)).reshape(heads, S, d)


def _pad_cols(t, width, padded):
    S = t.shape[0]
    return jnp.pad(t.reshape(S, N_CHIPS, width), ((0, 0), (0, 0), (0, padded - width))).reshape(S, N_CHIPS * padded)


def _unpad_cols(t, width, padded):
    S = t.shape[0]
    return t.reshape(S, N_CHIPS, padded)[:, :, :width].reshape(S, N_CHIPS * width)


def _alibi(n):
    return 2.0 ** (-8.0 * jnp.arange(1, n + 1, dtype=F32) / n)


B_KW = [dict(hps=4, nb_seq=SEQ // d // ATT_BLK, n_back=w // d, scale=1.0 / math.sqrt(HEAD_DIM), dist_scale=d)
        for w, d in B_PATTERNS]
A_KW = dict(hps=A_Q_HEADS // A_KV_HEADS, nb_seq=SEQ // ATT_BLK, n_back=A_WINDOW - 1, scale=1.0 / math.sqrt(HEAD_DIM), dist_scale=1)
D_KW = dict(blk=256, hps=2, scale=1.0 / math.sqrt(D_NOPE + D_ROPE))
C_KW = dict(blk=256, hps=2, scale=1.0 / math.sqrt(HEAD_DIM))


def _front_block(t):
    return jnp.pad(t, ((0, 0), (ATT_BLK, 0), (0, 0)))


def _rope_tables(reps):
    inv_freq = ROPE_BASE ** (-jnp.arange(0, D_ROPE, 2, dtype=F32) / D_ROPE)
    ang = jnp.arange(SEQ, dtype=F32)[:, None] * inv_freq[None, :]
    return jnp.tile(jnp.cos(ang), (1, reps)), jnp.tile(jnp.sin(ang), (1, reps))


def _mlp_fwd(x, x16, w1, w2, g, b, tag):
    hid, act = _matmul(x16, w1, mode="nn", out_dtype=BF16, epilogue="relu2", name=f"mlp{tag}_up")
    m = _matmul(act, w2, mode="nn", out_dtype=F32, name=f"mlp{tag}_down")
    y, y16, u = _norm_fwd(x, m, g, b, center=True, eps=LN_EPS, alpha=ALPHA, name=f"ln2_{tag}")
    return y, y16, (x16, hid, act, u)


def _mlp_bwd(dy, saved, w1, w2, g, tag, after=None):
    x16, hid, act, u = saved
    du, du16, dg, db = _norm_bwd(dy, u, g, center=True, eps=LN_EPS, name=f"ln2_{tag}_bwd", after=after)
    dw2 = _matmul_tn(act, du16, shards=1, name=f"mlp{tag}_dw2")
    dhid = _matmul(du16, w2, mode="nt", out_dtype=BF16, epilogue="drelu2", extra=hid, name=f"mlp{tag}_dact")
    dw1 = _matmul_tn(x16, dhid, shards=N_CHIPS, name=f"mlp{tag}_dw1")
    dx = _matmul(dhid, w1, mode="nt", out_dtype=F32, epilogue="add", extra=du, alpha=ALPHA, name=f"mlp{tag}_dx")
    return dx, dw1, dw2, dg, db


def _even_fwd(x16, w_in, sinks):
    h = _unpad_cols(_matmul(x16, w_in, mode="nn", out_dtype=BF16, name="even_in"), EVEN_IN // N_CHIPS, EVEN_IN_PAD)
    qa = _to_heads(h[:, :A_Q_W], A_Q_HEADS)
    ka = _front_block(_to_heads(h[:, A_Q_W:A_Q_W + A_KV_W], A_KV_HEADS))
    va = _front_block(_to_heads(h[:, A_Q_W + A_KV_W:A_Q_W + 2 * A_KV_W], A_KV_HEADS))
    slopes_a, slopes_b = _alibi(A_Q_HEADS), _alibi(B_HEADS)
    oa, lse_a = _band_fwd(qa, ka, va, slopes_a, sinks, name="swa_fwd", **A_KW)
    base = A_Q_W + 2 * A_KV_W
    qkv_b, outs, lses = [], [], []
    for gi, (_, dil) in enumerate(B_PATTERNS):
        cols = h[:, base + gi * 3 * B_W:base + (gi + 1) * 3 * B_W]
        q, k, v = (_to_heads(cols[:, i * B_W:(i + 1) * B_W], B_HEADS, dil) for i in range(3))
        k, v = _front_block(k), _front_block(v)
        o, lse = _band_fwd(q, k, v, slopes_b, None, name=f"dil{gi}_fwd", **B_KW[gi])
        qkv_b.append((q, k, v, o, lse))
        outs.append(_unstride(o, dil))
        lses.append(_unstride(lse, dil))
    ob = _mix_fwd(outs, lses, name="dil_mix")
    y16 = jnp.concatenate([_from_heads(oa), _from_heads(ob)], axis=-1).astype(BF16)
    return y16, (x16, qa, ka, va, oa, lse_a, qkv_b, outs, lses, y16)


def _even_bwd(dmix16, du, saved, w_in, sinks, w_out, send_w_out, send_w_in):
    x16, qa, ka, va, oa, lse_a, qkv_b, outs, lses, y16 = saved
    slopes_a, slopes_b = _alibi(A_Q_HEADS), _alibi(B_HEADS)
    sent = send_w_out(_matmul_tn(y16, dmix16, shards=N_CHIPS, name="even_dwout"))
    dy = _matmul(dmix16, w_out, mode="nt", out_dtype=F32, name="even_dy", after=sent)
    doa = _to_heads(dy[:, :A_Q_W], A_Q_HEADS)
    dob = _to_heads(dy[:, A_Q_W:], B_HEADS)
    dqa, dka, dva, dsink = _band_bwd(qa, ka, va, oa, doa, lse_a, None, slopes_a, sinks, name="swa_bwd", **A_KW)
    douts, dlses = _mix_bwd(outs, lses, dob, name="dil_mix_bwd")
    parts = [_from_heads(dqa), _from_heads(dka[:, ATT_BLK:]), _from_heads(dva[:, ATT_BLK:])]
    for gi, (_, dil) in enumerate(B_PATTERNS):
        q, k, v, o, lse = qkv_b[gi]
        dq, dk, dv = _band_bwd(q, k, v, o, _restride(douts[gi], dil), lse, _restride(dlses[gi], dil), slopes_b, None,
                               name=f"dil{gi}_bwd", **B_KW[gi])
        parts += [_from_heads(dq, dil), _from_heads(dk[:, ATT_BLK:], dil), _from_heads(dv[:, ATT_BLK:], dil)]
    dh16 = _pad_cols(jnp.concatenate(parts, axis=-1), EVEN_IN // N_CHIPS, EVEN_IN_PAD).astype(BF16)
    sent = send_w_in(_matmul_tn(x16, dh16, shards=N_CHIPS, name="even_dwin"), dsink[:, 0, 0])
    return _matmul(dh16, w_in, mode="nt", out_dtype=F32, epilogue="add", extra=du, alpha=ALPHA, name="even_dx", after=sent)


def _odd_fwd(x16, w_in, qn_g, kvn_g, w_uq, w_ukv, w_out):
    S = x16.shape[0]
    h = _unpad_cols(_matmul(x16, w_in, mode="nn", out_dtype=F32, name="odd_in"), ODD_IN // N_CHIPS, ODD_IN_PAD)
    qc, kc, vc = (_to_heads(h[:, i * C_W:(i + 1) * C_W].astype(BF16), C_HEADS) for i in range(3))
    cq = h[:, 3 * C_W:3 * C_W + D_Q_RANK]
    ckv = h[:, 3 * C_W + D_Q_RANK:3 * C_W + D_Q_RANK + D_KV_RANK]
    kr = h[:, 3 * C_W + D_Q_RANK + D_KV_RANK:]
    oc, tot = _stick_fwd(qc, kc, vc, name="stick_fwd", **C_KW)
    _, cqn16, _ = _norm_fwd(cq, None, qn_g, None, center=False, eps=RMS_EPS, alpha=1.0, name="q_rms")
    _, ckvn16, _ = _norm_fwd(ckv, None, kvn_g, None, center=False, eps=RMS_EPS, alpha=1.0, name="kv_rms")
    q = _matmul(cqn16, w_uq, mode="nn", out_dtype=F32, name="mla_uq").reshape(S, D_HEADS, D_NOPE + D_ROPE)
    kv = _matmul(ckvn16, w_ukv, mode="nn", out_dtype=F32, name="mla_ukv").reshape(S, D_HEADS, D_NOPE + D_V)
    half = D_ROPE // 2
    q_rope = q[..., D_NOPE:]
    x1 = jnp.concatenate([q_rope[..., :half].reshape(S, D_HEADS * half), kr[:, :half]], axis=-1)
    x2 = jnp.concatenate([q_rope[..., half:].reshape(S, D_HEADS * half), kr[:, half:]], axis=-1)
    cos, sin = _rope_tables(D_HEADS + 1)
    y1, y2 = _rope(x1[None], x2[None], cos, sin, name="rope_fwd")
    nq = D_HEADS * half
    q_rot = jnp.concatenate([y1[:, :nq].reshape(S, D_HEADS, half), y2[:, :nq].reshape(S, D_HEADS, half)], axis=-1)
    kr_rot = jnp.concatenate([y1[:, nq:], y2[:, nq:]], axis=-1)
    qd = jnp.transpose(jnp.concatenate([q[..., :D_NOPE], q_rot], axis=-1), (1, 0, 2)).astype(BF16)
    kd = jnp.transpose(jnp.concatenate([kv[..., :D_NOPE], jnp.broadcast_to(kr_rot[:, None, :], (S, D_HEADS, D_ROPE))], axis=-1),
                       (1, 0, 2)).astype(BF16)
    vd = jnp.transpose(kv[..., D_NOPE:], (1, 0, 2)).astype(BF16)
    od, lse_d = _causal_fwd(qd, kd, vd, name="mla_fwd", **D_KW)
    y16 = jnp.concatenate([_from_heads(oc), _from_heads(od)], axis=-1).astype(BF16)
    mixed = _matmul(y16, w_out, mode="nn", out_dtype=F32, name="odd_out")
    return mixed, (x16, qc, kc, vc, tot, cq, ckv, cqn16, ckvn16, qd, kd, vd, od, lse_d, y16)


def _odd_bwd(dmix16, du, saved, w_in, qn_g, kvn_g, w_uq, w_ukv, w_out):
    x16, qc, kc, vc, tot, cq, ckv, cqn16, ckvn16, qd, kd, vd, od, lse_d, y16 = saved
    S = x16.shape[0]
    half = D_ROPE // 2
    dw_out = _matmul_tn(y16, dmix16, shards=1, name="odd_dwout")
    dy = _matmul(dmix16, w_out, mode="nt", out_dtype=F32, name="odd_dy")
    doc = _to_heads(dy[:, :C_W], C_HEADS)
    dod = _to_heads(dy[:, C_W:], D_HEADS)
    dqc, dkc, dvc = _stick_bwd(qc, kc, vc, tot, doc, name="stick_bwd", **C_KW)
    dqd, dkd, dvd = _causal_bwd(qd, kd, vd, od, dod, lse_d, name="mla_bwd", **D_KW)
    cos, sin = _rope_tables(D_HEADS + 1)
    nq = D_HEADS * half
    gq1 = jnp.transpose(dqd[..., D_NOPE:D_NOPE + half], (1, 0, 2)).reshape(1, S, nq)
    gq2 = jnp.transpose(dqd[..., D_NOPE + half:], (1, 0, 2)).reshape(1, S, nq)
    dq1, dq2 = _rope(gq1, gq2, cos[:, :nq], -sin[:, :nq], name="rope_bwd_q")
    dk1, dk2 = _rope(dkd[..., D_NOPE:D_NOPE + half], dkd[..., D_NOPE + half:], cos[:, nq:], -sin[:, nq:], name="rope_bwd_k")
    dq_full = jnp.concatenate([jnp.transpose(dqd[..., :D_NOPE], (1, 0, 2)), dq1.reshape(S, D_HEADS, half),
                               dq2.reshape(S, D_HEADS, half)], axis=-1).reshape(S, D_HEADS * (D_NOPE + D_ROPE)).astype(BF16)
    dkv_full = jnp.concatenate([jnp.transpose(dkd[..., :D_NOPE], (1, 0, 2)), jnp.transpose(dvd, (1, 0, 2))],
                               axis=-1).reshape(S, D_HEADS * (D_NOPE + D_V)).astype(BF16)
    dw_uq = _matmul_tn(cqn16, dq_full, shards=N_CHIPS, name="mla_dwuq")
    dw_ukv = _matmul_tn(ckvn16, dkv_full, shards=N_CHIPS, name="mla_dwukv")
    dcqn = _matmul(dq_full, w_uq, mode="nt", out_dtype=F32, name="mla_dcq")
    dckvn = _matmul(dkv_full, w_ukv, mode="nt", out_dtype=F32, name="mla_dckv")
    dcq, _, dqn_g, _ = _norm_bwd(dcqn, cq, qn_g, center=False, eps=RMS_EPS, name="q_rms_bwd")
    dckv, _, dkvn_g, _ = _norm_bwd(dckvn, ckv, kvn_g, center=False, eps=RMS_EPS, name="kv_rms_bwd")
    dh = jnp.concatenate([_from_heads(dqc), _from_heads(dkc), _from_heads(dvc), dcq, dckv, dk1, dk2], axis=-1)
    dh16 = _pad_cols(dh, ODD_IN // N_CHIPS, ODD_IN_PAD).astype(BF16)
    dw_in = _matmul_tn(x16, dh16, shards=N_CHIPS, name="odd_dwin")
    dx = _matmul(dh16, w_in, mode="nt", out_dtype=F32, epilogue="add", extra=du, alpha=ALPHA, name="odd_dx")
    return dx, dw_in, dqn_g[0], dkvn_g[0], dw_uq, dw_ukv, dw_out


WEIGHT_GROUPS = (("even_w_in",), ("even_w_out", "mlp_w1_0", "mlp_w2_0"), ("odd_w_in", "odd_w_uq", "odd_w_ukv", "odd_w_out"),
                 ("mlp_w1_1", "mlp_w2_1"))
GRAD_GROUPS = (("mlp_w2_1", "mlp_w1_1"), ("odd_w_out", "odd_w_uq", "odd_w_ukv", "odd_w_in"), ("mlp_w2_0", "mlp_w1_0"),
               ("even_w_out",), ("even_w_in",))


def _local_step(x, target, small, weights_for, send_grads, send_small, total_loss):
    def by_rows(t, rows):
        return t.reshape(N_CHIPS, rows // N_CHIPS, D_MODEL)

    x16 = x.astype(BF16)
    w = dict(weights_for(0, x16))
    y16, sv_e = _even_fwd(x16, w["even_w_in"], small["even_sinks"])
    w.update(weights_for(1, y16))
    mixed0 = _matmul(y16, w["even_w_out"], mode="nn", out_dtype=F32, name="even_out")
    x1, x1_16, u01 = _norm_fwd(x, mixed0, small["ln1_g"][0], small["ln1_b"][0], center=True, eps=LN_EPS, alpha=ALPHA, name="ln1_0")
    w1_0, w2_0 = w["mlp_w1_0"], w["mlp_w2_0"].reshape(1, D_FF, D_MODEL)
    x2, x2_16, sv_m0 = _mlp_fwd(x1, x1_16, w1_0, w2_0, small["ln2_g"][0], small["ln2_b"][0], 0)
    w.update(weights_for(2, x2_16))
    odd_w = (w["odd_w_in"], small["odd_q_norm_g"], small["odd_kv_norm_g"], w["odd_w_uq"], w["odd_w_ukv"],
             w["odd_w_out"].reshape(1, D_MODEL, D_MODEL))
    mixed1, sv_o = _odd_fwd(x2_16, *odd_w)
    x3, x3_16, u11 = _norm_fwd(x2, mixed1, small["ln1_g"][1], small["ln1_b"][1], center=True, eps=LN_EPS, alpha=ALPHA, name="ln1_1")
    w.update(weights_for(3, x3_16))
    w1_1, w2_1 = w["mlp_w1_1"], w["mlp_w2_1"].reshape(1, D_FF, D_MODEL)
    x4, _, sv_m1 = _mlp_fwd(x3, x3_16, w1_1, w2_1, small["ln2_g"][1], small["ln2_b"][1], 1)
    dy, loss_row = _loss_head(x4, target, name="loss_head")
    loss = total_loss(loss_row)

    dx3, dw1_1, dw2_1, dln2g_1, dln2b_1 = _mlp_bwd(dy, sv_m1, w1_1, w2_1, small["ln2_g"][1], 1, after=loss.reshape(1, 1))
    sent = send_grads(0, dict(mlp_w2_1=by_rows(dw2_1, D_FF), mlp_w1_1=dw1_1))
    du, du16, dln1g_1, dln1b_1 = _norm_bwd(dx3, u11, small["ln1_g"][1], center=True, eps=LN_EPS, name="ln1_1_bwd", after=sent)
    dx2, dwin_o, dqn_g, dkvn_g, dwuq, dwukv, dwout_o = _odd_bwd(du16, du, sv_o, *odd_w)
    sent = send_grads(1, dict(odd_w_out=by_rows(dwout_o, D_MODEL), odd_w_uq=dwuq, odd_w_ukv=dwukv, odd_w_in=dwin_o))
    dx1, dw1_0, dw2_0, dln2g_0, dln2b_0 = _mlp_bwd(dx2, sv_m0, w1_0, w2_0, small["ln2_g"][0], 0, after=sent)
    sent = send_grads(2, dict(mlp_w2_0=by_rows(dw2_0, D_FF), mlp_w1_0=dw1_0))
    du, du16, dln1g_0, dln1b_0 = _norm_bwd(dx1, u01, small["ln1_g"][0], center=True, eps=LN_EPS, name="ln1_0_bwd", after=sent)
    def send_last(dwin_e, dsinks):
        went = send_small(dict(even_sinks=dsinks, odd_q_norm_g=dqn_g, odd_kv_norm_g=dkvn_g,
                               ln1_g=jnp.concatenate([dln1g_0, dln1g_1]), ln1_b=jnp.concatenate([dln1b_0, dln1b_1]),
                               ln2_g=jnp.concatenate([dln2g_0, dln2g_1]), ln2_b=jnp.concatenate([dln2b_0, dln2b_1])))
        return send_grads(4, dict(even_w_in=dwin_e), went)

    dx0 = _even_bwd(du16, du, sv_e, w["even_w_in"], small["even_sinks"], w["even_w_out"],
                    lambda dwout_e: send_grads(3, dict(even_w_out=dwout_e)), send_last)
    return loss, dx0


SMALL_ORDER = ("ln1_g", "ln1_b", "ln2_g", "ln2_b", "even_sinks", "odd_q_norm_g", "odd_kv_norm_g")
SMALL_COLS = 2176


def _pack_small(parts):
    flat = jnp.concatenate([p.reshape(-1) for p in parts])
    return jnp.pad(flat, (0, 8 * SMALL_COLS - flat.shape[0])).reshape(8, SMALL_COLS)


def _unpack_small(packed, shapes):
    flat = packed.reshape(-1)
    out, pos = [], 0
    for s in shapes:
        n = math.prod(s)
        out.append(flat[pos:pos + n].reshape(s))
        pos += n
    return out


def kernel(x, even_w_in, even_sinks, even_w_out, odd_w_in, odd_q_norm_g, odd_kv_norm_g, odd_w_uq, odd_w_ukv, odd_w_out, ln1_g, ln1_b, mlp_w1, mlp_w2, ln2_g, ln2_b, loss_target, m_even_w_in, m_even_sinks, m_even_w_out, m_odd_w_in, m_odd_q_norm_g, m_odd_kv_norm_g, m_odd_w_uq, m_odd_w_ukv, m_odd_w_out, m_ln1_g, m_ln1_b, m_mlp_w1, m_mlp_w2, m_ln2_g, m_ln2_b, v_even_w_in, v_even_sinks, v_even_w_out, v_odd_w_in, v_odd_q_norm_g, v_odd_kv_norm_g, v_odd_w_uq, v_odd_w_ukv, v_odd_w_out, v_ln1_g, v_ln1_b, v_mlp_w1, v_mlp_w2, v_ln2_g, v_ln2_b):
    chip = 2 * lax.axis_index("x") + lax.axis_index("y")
    e_w, o_w = EVEN_IN // N_CHIPS, ODD_IN // N_CHIPS

    shards = dict(
        even_w_in=jnp.pad(even_w_in[0], ((0, 0), (0, EVEN_IN_PAD - e_w))), even_w_out=even_w_out[0],
        odd_w_in=jnp.pad(odd_w_in[0], ((0, 0), (0, ODD_IN_PAD - o_w))), odd_w_uq=odd_w_uq[0], odd_w_ukv=odd_w_ukv[0],
        odd_w_out=odd_w_out[0], mlp_w1_0=mlp_w1[0], mlp_w1_1=mlp_w1[1], mlp_w2_0=mlp_w2[0], mlp_w2_1=mlp_w2[1])
    names = list(shards)
    own16 = {n: shards[n].astype(BF16) for n in names}
    gather_started, token = [], own16[WEIGHT_GROUPS[0][0]]
    for g, group in enumerate(WEIGHT_GROUPS):
        lands = [jax.ShapeDtypeStruct((N_CHIPS,) + own16[n].shape, BF16) for n in group]
        gather_started.append(_copies_start([own16[n] for n in group], lands, token, SAME_CORE_OF_OTHER_CHIPS, _gather_src,
                                            _gather_dst, name=f"gather{g}_start"))
        token = gather_started[-1][-1]
    all_started = token

    def weights_for(g, after):
        group = WEIGHT_GROUPS[g]
        _, landed = _copies_wait(gather_started[g], all_started if g == 0 else after, SAME_CORE_OF_OTHER_CHIPS, _gather_src,
                                 _gather_landed, name=f"gather{g}_wait")
        full = _forward_halves(landed, name=f"gather{g}_forward")
        return {n: lax.dynamic_update_slice(f, own16[n][None], (chip, 0, 0)) for n, f in zip(group, full)}

    grads_started = {}

    def send_grads(g, shares, after=None):
        grads_started[g] = _reduce_scatter_start([shares[n] for n in GRAD_GROUPS[g]], after, name=f"grads{g}")
        return grads_started[g][-1]

    norm_rows = jnp.pad(jnp.concatenate([odd_q_norm_g[0], odd_kv_norm_g[0]])[None], ((0, 7), (0, 64)))
    norm_all = _all_gather8(norm_rows, name="gather_norm_gains")[0::16]
    qn_w, kvn_w = D_Q_RANK // N_CHIPS, D_KV_RANK // N_CHIPS
    small = dict(even_sinks=even_sinks[0], odd_q_norm_g=norm_all[:, :qn_w].reshape(D_Q_RANK),
                 odd_kv_norm_g=norm_all[:, qn_w:qn_w + kvn_w].reshape(D_KV_RANK), ln1_g=ln1_g, ln1_b=ln1_b, ln2_g=ln2_g, ln2_b=ln2_b)

    small_sums = []

    def send_small(sm):
        every = _all_gather8(_pack_small([sm[n] for n in SMALL_ORDER]), name="gather_small_grads")
        small_sums.append(_sum_slabs(every.reshape(8, 8, SMALL_COLS), name="sum_small_grads"))
        return small_sums[0]

    loss, grad_x = _local_step(x[0], loss_target[0], small, weights_for, send_grads, send_small,
                               lambda row: lax.psum(row[0, 0], ("x", "y", "c")))

    weights = dict(even_w_in=even_w_in, even_sinks=even_sinks, even_w_out=even_w_out, odd_w_in=odd_w_in, odd_q_norm_g=odd_q_norm_g,
                   odd_kv_norm_g=odd_kv_norm_g, odd_w_uq=odd_w_uq, odd_w_ukv=odd_w_ukv, odd_w_out=odd_w_out, ln1_g=ln1_g, ln1_b=ln1_b,
                   mlp_w1=mlp_w1, mlp_w2=mlp_w2, ln2_g=ln2_g, ln2_b=ln2_b)
    m_in = dict(even_w_in=m_even_w_in, even_sinks=m_even_sinks, even_w_out=m_even_w_out, odd_w_in=m_odd_w_in,
                odd_q_norm_g=m_odd_q_norm_g, odd_kv_norm_g=m_odd_kv_norm_g, odd_w_uq=m_odd_w_uq, odd_w_ukv=m_odd_w_ukv,
                odd_w_out=m_odd_w_out, ln1_g=m_ln1_g, ln1_b=m_ln1_b, mlp_w1=m_mlp_w1, mlp_w2=m_mlp_w2, ln2_g=m_ln2_g, ln2_b=m_ln2_b)
    v_in = dict(even_w_in=v_even_w_in, even_sinks=v_even_sinks, even_w_out=v_even_w_out, odd_w_in=v_odd_w_in,
                odd_q_norm_g=v_odd_q_norm_g, odd_kv_norm_g=v_odd_kv_norm_g, odd_w_uq=v_odd_w_uq, odd_w_ukv=v_odd_w_ukv,
                odd_w_out=v_odd_w_out, ln1_g=v_ln1_g, ln1_b=v_ln1_b, mlp_w1=v_mlp_w1, mlp_w2=v_mlp_w2, ln2_g=v_ln2_g, ln2_b=v_ln2_b)
    order = list(weights)
    updated = {}

    def update(n, g2d, layer=0, tag=""):
        shape = weights[n].shape
        as3d = (1, math.prod(shape[:-1]), shape[-1]) if len(shape) == 2 else shape
        res = _adamw(weights[n].reshape(as3d), g2d, m_in[n].reshape(as3d), v_in[n].reshape(as3d), layer=layer,
                     carry=updated.get(n), name=f"adamw_{n}{tag}")
        updated[n] = tuple(res)
        return res[0]

    after = grad_x
    for g, group in enumerate(GRAD_GROUPS):
        for n, r in zip(group, _reduce_scatter_finish(grads_started[g], after, name=f"grads{g}")):
            if n[:-2] in ("mlp_w1", "mlp_w2"):
                after = update(n[:-2], r, layer=int(n[-1]), tag=n[-2:])
            elif n == "even_w_in":
                after = update(n, r[:, :e_w])
            elif n == "odd_w_in":
                after = update(n, r[:, :o_w])
            else:
                after = update(n, r)
        if g == len(GRAD_GROUPS) - 2:
            g_ln1g, g_ln1b, g_ln2g, g_ln2b, g_sinks, g_qn, g_kvn = _unpack_small(
                small_sums[0], [(DEPTH, D_MODEL)] * 4 + [(1, A_Q_HEADS), (D_Q_RANK,), (D_KV_RANK,)])
            for n, gs in (("even_sinks", g_sinks), ("odd_q_norm_g", lax.dynamic_slice_in_dim(g_qn, chip * qn_w, qn_w)[None]),
                          ("odd_kv_norm_g", lax.dynamic_slice_in_dim(g_kvn, chip * kvn_w, kvn_w)[None]), ("ln1_g", g_ln1g),
                          ("ln1_b", g_ln1b), ("ln2_g", g_ln2g), ("ln2_b", g_ln2b)):
                update(n, gs)
    outs = [[updated[n][k].reshape(weights[n].shape) for n in order] for k in range(4)]
    return (loss, grad_x[None], *outs[0], *outs[1], *outs[2], *outs[3])
```

```python
import functools
import math

import jax
import jax.numpy as jnp
from jax import lax
from jax.experimental import pallas as pl
from jax.experimental.pallas import tpu as pltpu

F32 = jnp.float32
BF16 = jnp.bfloat16
MESH = pl.DeviceIdType.MESH

D_MODEL = 2048
SEQ = 2048
DEPTH = 2
HEAD_DIM = 64
A_Q_HEADS, A_KV_HEADS, A_WINDOW = 16, 2, 128
B_HEADS = 8
B_PATTERNS = ((128, 1), (512, 4), (2048, 16))
C_HEADS = 16
D_HEADS, D_Q_RANK, D_KV_RANK, D_NOPE, D_ROPE, D_V = 16, 512, 256, 64, 32, 64
ROPE_BASE = 10000.0
D_FF = 4 * D_MODEL
LN_EPS = 1e-5
RMS_EPS = 1e-6
ALPHA = (2 * DEPTH) ** 0.25
A_Q_W = A_Q_HEADS * HEAD_DIM
A_KV_W = A_KV_HEADS * HEAD_DIM
B_W = B_HEADS * HEAD_DIM
EVEN_IN = A_Q_W + 2 * A_KV_W + 3 * B_W * len(B_PATTERNS)
EVEN_OUT = A_Q_W + B_W
C_W = C_HEADS * HEAD_DIM
ODD_IN = 3 * C_W + D_Q_RANK + D_KV_RANK + D_ROPE
ADAM_LR, ADAM_B1, ADAM_B2, ADAM_EPS, ADAM_WD, ADAM_STEP = 0.001, 0.9, 0.999, 1e-08, 0.01, 10

N_CHIPS = 4
EVEN_IN_PAD = 1536
ODD_IN_PAD = 1024
ATT_BLK = 128
NEG = -1e30
V7X_VMEM_LIMIT = 48 * 1024 * 1024


def _params(*sem):
    return pltpu.CompilerParams(dimension_semantics=sem, vmem_limit_bytes=V7X_VMEM_LIMIT)


def _tile(n, cap):
    if n <= cap:
        return n
    t = cap - cap % 128
    while n % t:
        t -= 128
    return t


def _matmul(a, b, *, mode, out_dtype, name, epilogue=None, extra=None, alpha=1.0, after=None, tm=1024, tn=1024, tk=2048):
    if mode == "nn":
        M, K = a.shape
        P, _, Np = b.shape
        tm, tn, tk = _tile(M, tm), _tile(Np, tn), _tile(K, tk)
        nj = Np // tn
        grid = (M // tm, P * nj, K // tk)
        a_spec = pl.BlockSpec((tm, tk), lambda i, j, k: (i, k))
        b_spec = pl.BlockSpec((None, tk, tn), lambda i, j, k: (j // nj, k, j % nj))
        o_spec = pl.BlockSpec((tm, tn), lambda i, j, k: (i, j))
        out_shape = (M, P * Np)
        dims = (((1,), (0,)), ((), ()))
    elif mode == "nt":
        M, N = a.shape
        P, K, Np = b.shape
        tm, tn, tk = _tile(M, tm), _tile(K, tn), _tile(Np, tk)
        nkk = Np // tk
        grid = (M // tm, K // tn, P * nkk)
        a_spec = pl.BlockSpec((tm, tk), lambda i, j, k: (i, k))
        b_spec = pl.BlockSpec((None, tn, tk), lambda i, j, k: (k // nkk, j, k % nkk))
        o_spec = pl.BlockSpec((tm, tn), lambda i, j, k: (i, j))
        out_shape = (M, K)
        dims = (((1,), (1,)), ((), ()))
    else:
        raise ValueError(mode)
    n_k = grid[2]
    n_extra = 0 if extra is None else 1
    n_after = 0 if after is None else 1
    n_out = 2 if epilogue == "relu2" else 1

    def body(*refs):
        a_ref, b_ref = refs[:2]
        e_ref = refs[2] if n_extra else None
        outs = refs[2 + n_extra + n_after:2 + n_extra + n_after + n_out]
        part = lax.dot_general(a_ref[...], b_ref[...], dims, preferred_element_type=F32)

        def finish(r):
            if epilogue == "relu2":
                outs[0][...] = r.astype(outs[0].dtype)
                rp = jnp.maximum(r, 0.0)
                outs[1][...] = (rp * rp).astype(outs[1].dtype)
            elif epilogue == "drelu2":
                outs[0][...] = (r * (2.0 * jnp.maximum(e_ref[...].astype(F32), 0.0))).astype(outs[0].dtype)
            elif epilogue == "add":
                outs[0][...] = (r + alpha * e_ref[...].astype(F32)).astype(outs[0].dtype)
            else:
                outs[0][...] = r.astype(outs[0].dtype)

        if n_k == 1:
            finish(part)
        else:
            acc = refs[-1]
            k = pl.program_id(2)

            @pl.when(k == 0)
            def _():
                acc[...] = part

            @pl.when((k > 0) & (k < n_k - 1))
            def _():
                acc[...] += part

            @pl.when(k == n_k - 1)
            def _():
                finish(acc[...] + part)

    in_specs = [a_spec, b_spec] + ([o_spec] if n_extra else []) + ([pl.BlockSpec(memory_space=pl.ANY)] if n_after else [])
    shapes = [jax.ShapeDtypeStruct(out_shape, out_dtype)] * n_out
    res = pl.pallas_call(
        body, name=name, grid=grid, in_specs=in_specs,
        out_specs=[o_spec] * n_out, out_shape=shapes,
        scratch_shapes=[pltpu.VMEM((tm, tn), F32)] if n_k > 1 else [],
        compiler_params=_params("parallel", "parallel", "arbitrary"),
    )(a, b, *([extra] if n_extra else []), *([after] if n_after else []))
    return res if n_out > 1 else res[0]


def _matmul_tn(a, g, *, shards, name, tm=1024, tn=1024):
    M, K = a.shape
    P = shards
    Np = g.shape[1] // P
    tm, tn = _tile(K, tm), _tile(Np, tn)
    nj = Np // tn

    def body(a_ref, g_ref, o_ref):
        o_ref[...] = lax.dot_general(a_ref[...], g_ref[...], (((0,), (0,)), ((), ())), preferred_element_type=F32).astype(BF16)

    return pl.pallas_call(
        body, name=name, grid=(K // tm, P * nj),
        in_specs=[pl.BlockSpec((M, tm), lambda i, j: (0, i)), pl.BlockSpec((M, tn), lambda i, j: (0, j))],
        out_specs=pl.BlockSpec((None, tm, tn), lambda i, j: (j // nj, i, j % nj)),
        out_shape=jax.ShapeDtypeStruct((P, K, Np), BF16),
        compiler_params=_params("parallel", "parallel"),
    )(a, g)


def _norm_fwd(x, res, g, b, *, center, eps, alpha, name, rows=256):
    S, W = x.shape
    has_res = res is not None
    rows = _tile(S, rows)

    def body(*refs):
        x_ref = refs[0]
        r_ref = refs[1] if has_res else None
        g_ref = refs[1 + has_res]
        b_ref = refs[2 + has_res] if center else None
        y_ref, y16_ref, u_ref = refs[-3:]
        u = x_ref[...]
        if has_res:
            u = alpha * u + r_ref[...]
        if center:
            mu = jnp.mean(u, axis=1, keepdims=True)
            xc = u - mu
        else:
            xc = u
        var = jnp.mean(xc * xc, axis=1, keepdims=True)
        y = xc * lax.rsqrt(var + eps) * g_ref[...]
        if center:
            y = y + b_ref[...]
        y_ref[...] = y
        y16_ref[...] = y.astype(BF16)
        u_ref[...] = u

    row_spec = pl.BlockSpec((rows, W), lambda i: (i, 0))
    vec_spec = pl.BlockSpec((1, W), lambda i: (0, 0))
    ins = [x] + ([res] if has_res else []) + [g.reshape(1, W)] + ([b.reshape(1, W)] if center else [])
    specs = [row_spec] + ([row_spec] if has_res else []) + [vec_spec] + ([vec_spec] if center else [])
    return pl.pallas_call(
        body, name=name, grid=(S // rows,), in_specs=specs,
        out_specs=[row_spec, row_spec, row_spec],
        out_shape=[jax.ShapeDtypeStruct((S, W), F32), jax.ShapeDtypeStruct((S, W), BF16), jax.ShapeDtypeStruct((S, W), F32)],
        compiler_params=_params("parallel"),
    )(*ins)


def _norm_bwd(dy, u, g, *, center, eps, name, rows=256, after=None):
    S, W = u.shape
    rows = _tile(S, rows)

    def body(dy_ref, u_ref, g_ref, *rest):
        du_ref, du16_ref, dg_ref, db_ref = rest[-4:]
        i = pl.program_id(0)
        uu = u_ref[...]
        dyv = dy_ref[...]
        if center:
            xc = uu - jnp.mean(uu, axis=1, keepdims=True)
        else:
            xc = uu
        rstd = lax.rsqrt(jnp.mean(xc * xc, axis=1, keepdims=True) + eps)
        xhat = xc * rstd
        dxh = dyv * g_ref[...]
        c2 = jnp.mean(dxh * xhat, axis=1, keepdims=True)
        du = dxh - xhat * c2
        if center:
            du = du - jnp.mean(dxh, axis=1, keepdims=True)
        du = du * rstd
        du_ref[...] = du
        du16_ref[...] = du.astype(BF16)

        @pl.when(i == 0)
        def _():
            dg_ref[...] = jnp.zeros_like(dg_ref)
            db_ref[...] = jnp.zeros_like(db_ref)

        dg_ref[...] += jnp.sum(dyv * xhat, axis=0, keepdims=True)
        db_ref[...] += jnp.sum(dyv, axis=0, keepdims=True)

    row_spec = pl.BlockSpec((rows, W), lambda i: (i, 0))
    vec_spec = pl.BlockSpec((1, W), lambda i: (0, 0))
    return pl.pallas_call(
        body, name=name, grid=(S // rows,),
        in_specs=[row_spec, row_spec, vec_spec] + ([] if after is None else [pl.BlockSpec(memory_space=pl.ANY)]),
        out_specs=[row_spec, row_spec, vec_spec, vec_spec],
        out_shape=[jax.ShapeDtypeStruct((S, W), F32), jax.ShapeDtypeStruct((S, W), BF16),
                   jax.ShapeDtypeStruct((1, W), F32), jax.ShapeDtypeStruct((1, W), F32)],
        compiler_params=_params("arbitrary"),
    )(dy, u, g.reshape(1, W), *([] if after is None else [after]))


def _loss_head(y, target, *, name, rows=256):
    S, W = y.shape
    rows = _tile(S, rows)

    def body(y_ref, t_ref, dy_ref, l_ref):
        i = pl.program_id(0)
        e = y_ref[...] - t_ref[...]
        dy_ref[...] = e * (1.0 / W)

        @pl.when(i == 0)
        def _():
            l_ref[...] = jnp.zeros_like(l_ref)

        l_ref[...] += jnp.full(l_ref.shape, 0.5 / W * jnp.sum(e * e), F32)

    row_spec = pl.BlockSpec((rows, W), lambda i: (i, 0))
    return pl.pallas_call(
        body, name=name, grid=(S // rows,), in_specs=[row_spec, row_spec],
        out_specs=[row_spec, pl.BlockSpec((1, 128), lambda i: (0, 0))],
        out_shape=[jax.ShapeDtypeStruct((S, W), F32), jax.ShapeDtypeStruct((1, 128), F32)],
        compiler_params=_params("arbitrary"),
    )(y, target)


def _adamw(w, g, m, v, *, name, layer=0, carry=None, rows=256):
    L, R, C = w.shape
    rows = _tile(R, rows) if R % 8 == 0 else R
    c1 = 1.0 / (1.0 - ADAM_B1 ** ADAM_STEP)
    c2 = 1.0 / (1.0 - ADAM_B2 ** ADAM_STEP)

    def body(w_ref, g_ref, m_ref, v_ref, *rest):
        go_ref, d_ref, mo_ref, vo_ref = rest[-4:]
        gg = g_ref[...]
        mn = ADAM_B1 * m_ref[...] + (1.0 - ADAM_B1) * gg
        vn = ADAM_B2 * v_ref[...] + (1.0 - ADAM_B2) * (gg * gg)
        go_ref[...] = gg
        d_ref[...] = -ADAM_LR * ((mn * c1) / (jnp.sqrt(vn * c2) + ADAM_EPS) + ADAM_WD * w_ref[...])
        mo_ref[...] = mn
        vo_ref[...] = vn

    slab = pl.BlockSpec((None, rows, C), lambda i: (layer, i, 0))
    shp = jax.ShapeDtypeStruct((L, R, C), F32)
    carried = [] if carry is None else list(carry)
    return pl.pallas_call(
        body, name=name, grid=(R // rows,),
        in_specs=[slab, pl.BlockSpec((rows, C), lambda i: (i, 0)), slab, slab] + [pl.BlockSpec(memory_space=pl.ANY)] * len(carried),
        out_specs=[slab] * 4, out_shape=[shp] * 4, input_output_aliases={4 + k: k for k in range(len(carried))},
        compiler_params=_params("parallel"),
    )(w, g, m, v, *carried)


def _rope(x1, x2, cos, sin, *, name, rows=512):
    H, S, W = x1.shape
    rows = _tile(S, rows)

    def body(x1_ref, x2_ref, c_ref, s_ref, y1_ref, y2_ref):
        t1 = jnp.sum(x1_ref[...], axis=0)
        t2 = jnp.sum(x2_ref[...], axis=0)
        c = c_ref[...]
        s = s_ref[...]
        y1_ref[...] = t1 * c - t2 * s
        y2_ref[...] = t1 * s + t2 * c

    xs = pl.BlockSpec((H, rows, W), lambda i: (0, i, 0))
    ts = pl.BlockSpec((rows, W), lambda i: (i, 0))
    shp = jax.ShapeDtypeStruct((S, W), F32)
    return pl.pallas_call(
        body, name=name, grid=(S // rows,), in_specs=[xs, xs, ts, ts], out_specs=[ts, ts], out_shape=[shp, shp],
        compiler_params=_params("parallel"),
    )(x1, x2, cos, sin)


def _band_mask(blk, n, n_back):
    row = lax.broadcasted_iota(jnp.int32, (blk, 2 * blk), 0)
    col = lax.broadcasted_iota(jnp.int32, (blk, 2 * blk), 1)
    rel = blk + row - col
    return rel, (rel >= 0) & (rel <= n_back) & ((col >= blk) | (n >= 1))


def _band_fwd(q, k, v, slopes, sinks, *, hps, nb_seq, n_back, scale, dist_scale, name):
    blk = ATT_BLK
    Hq, T, d = q.shape
    Hk = k.shape[0]
    group = Hq // Hk
    kps = max(hps // group, 1)
    use_sink = sinks is not None

    def body(*refs):
        q_ref, k_ref, v_ref, slope_ref = refs[:4]
        sink_ref = refs[4] if use_sink else None
        o_ref, lse_ref = refs[-2:]
        i = pl.program_id(1)
        start = pl.multiple_of(i * blk, blk)
        rel, valid = _band_mask(blk, i % nb_seq, n_back)
        relf = rel.astype(F32)
        for hh in range(hps):
            h = pl.program_id(0) * hps + hh
            kk = k_ref[hh // group, pl.ds(start, 2 * blk), :]
            vv = v_ref[hh // group, pl.ds(start, 2 * blk), :]
            s = lax.dot_general(q_ref[hh], kk, (((1,), (1,)), ((), ())), preferred_element_type=F32) * scale
            s = jnp.where(valid, s - (slope_ref[h] * dist_scale) * relf, NEG)
            m = jnp.max(s, axis=1, keepdims=True)
            if use_sink:
                m = jnp.maximum(m, sink_ref[h])
            p = jnp.where(valid, jnp.exp(s - m), 0.0)
            l = jnp.sum(p, axis=1, keepdims=True)
            if use_sink:
                l = l + jnp.exp(sink_ref[h] - m)
            o_ref[hh] = jnp.dot(p.astype(BF16), vv, preferred_element_type=F32) / l
            lse_ref[hh] = m + jnp.log(l)

    smem = pl.BlockSpec(memory_space=pltpu.SMEM)
    qs = pl.BlockSpec((hps, blk, d), lambda g, i: (g, i, 0))
    ks = pl.BlockSpec((kps, T + blk, d), lambda g, i: (g, 0, 0))
    ins = [q, k, v, slopes] + ([sinks] if use_sink else [])
    return pl.pallas_call(
        body, name=name, grid=(Hq // hps, T // blk), in_specs=[qs, ks, ks, smem] + ([smem] if use_sink else []),
        out_specs=[qs, pl.BlockSpec((hps, blk, 1), lambda g, i: (g, i, 0))],
        out_shape=[jax.ShapeDtypeStruct((Hq, T, d), F32), jax.ShapeDtypeStruct((Hq, T, 1), F32)],
        compiler_params=_params("parallel", "parallel"),
    )(*ins)


def _band_bwd(q, k, v, o, do, lse, dlse, slopes, sinks, *, hps, nb_seq, n_back, scale, dist_scale, name):
    blk = ATT_BLK
    Hq, T, d = q.shape
    Hk = k.shape[0]
    group = Hq // Hk
    kps = max(hps // group, 1)
    use_sink, use_dlse = sinks is not None, dlse is not None

    def body(*refs):
        q_ref, k_ref, v_ref, o_ref, do_ref, lse_ref = refs[:6]
        pos = 6
        dlse_ref = refs[pos] if use_dlse else None
        pos += use_dlse
        slope_ref = refs[pos]
        pos += 1
        sink_ref = refs[pos] if use_sink else None
        pos += use_sink
        dq_ref, dk_ref, dv_ref = refs[pos:pos + 3]
        dsink_ref = refs[pos + 3] if use_sink else None
        i = pl.program_id(1)
        start = pl.multiple_of(i * blk, blk)
        rel, valid = _band_mask(blk, i % nb_seq, n_back)
        relf = rel.astype(F32)

        @pl.when(i == 0)
        def _():
            dk_ref[...] = jnp.zeros_like(dk_ref)
            dv_ref[...] = jnp.zeros_like(dv_ref)
            if use_sink:
                dsink_ref[...] = jnp.zeros_like(dsink_ref)

        for kh in range(kps):
            dk_acc = jnp.zeros((2 * blk, d), F32)
            dv_acc = jnp.zeros((2 * blk, d), F32)
            kk = k_ref[kh, pl.ds(start, 2 * blk), :]
            vv = v_ref[kh, pl.ds(start, 2 * blk), :]
            for hh in range(kh * min(group, hps), (kh + 1) * min(group, hps)):
                h = pl.program_id(0) * hps + hh
                qb = q_ref[hh]
                dob = do_ref[hh]
                do16 = dob.astype(BF16)
                lse = lse_ref[hh]
                c = -jnp.sum(dob * o_ref[hh], axis=1, keepdims=True)
                if use_dlse:
                    c = c + dlse_ref[hh]
                s = lax.dot_general(qb, kk, (((1,), (1,)), ((), ())), preferred_element_type=F32) * scale
                s = jnp.where(valid, s - (slope_ref[h] * dist_scale) * relf, NEG)
                p = jnp.where(valid, jnp.exp(s - lse), 0.0)
                dp = lax.dot_general(do16, vv, (((1,), (1,)), ((), ())), preferred_element_type=F32)
                ds16 = (p * (dp + c) * scale).astype(BF16)
                dq_ref[hh] = jnp.dot(ds16, kk, preferred_element_type=F32)
                dk_acc = dk_acc + lax.dot_general(ds16, qb, (((0,), (0,)), ((), ())), preferred_element_type=F32)
                dv_acc = dv_acc + lax.dot_general(p.astype(BF16), do16, (((0,), (0,)), ((), ())), preferred_element_type=F32)
                if use_sink:
                    dsink_ref[hh] += jnp.full((1, 128), jnp.sum(jnp.exp(sink_ref[h] - lse) * c), F32)
            dk_ref[kh, pl.ds(start, 2 * blk), :] += dk_acc
            dv_ref[kh, pl.ds(start, 2 * blk), :] += dv_acc

    smem = pl.BlockSpec(memory_space=pltpu.SMEM)
    qs = pl.BlockSpec((hps, blk, d), lambda g, i: (g, i, 0))
    ls = pl.BlockSpec((hps, blk, 1), lambda g, i: (g, i, 0))
    ks = pl.BlockSpec((kps, T + blk, d), lambda g, i: (g, 0, 0))
    in_specs = [qs, ks, ks, qs, qs, ls] + ([ls] if use_dlse else []) + [smem] + ([smem] if use_sink else [])
    ins = [q, k, v, o, do, lse] + ([dlse] if use_dlse else []) + [slopes] + ([sinks] if use_sink else [])
    out_specs = [qs, ks, ks]
    out_shape = [jax.ShapeDtypeStruct((Hq, T, d), F32), jax.ShapeDtypeStruct((Hk, T + blk, d), F32),
                 jax.ShapeDtypeStruct((Hk, T + blk, d), F32)]
    if use_sink:
        out_specs.append(pl.BlockSpec((hps, 1, 128), lambda g, i: (g, 0, 0)))
        out_shape.append(jax.ShapeDtypeStruct((Hq, 1, 128), F32))
    return pl.pallas_call(
        body, name=name, grid=(Hq // hps, T // blk), in_specs=in_specs, out_specs=out_specs, out_shape=out_shape,
        compiler_params=_params("parallel", "arbitrary"),
    )(*ins)


def _diagonal_mask(blk):
    return lax.broadcasted_iota(jnp.int32, (blk, blk), 0) >= lax.broadcasted_iota(jnp.int32, (blk, blk), 1)


def _causal_fwd(q, k, v, *, blk, hps, scale, name):
    H, T, dqk = q.shape
    dv = v.shape[2]

    def body(q_ref, k_ref, v_ref, o_ref, lse_ref):
        n = pl.program_id(1)
        qs = [q_ref[hh] for hh in range(hps)]

        def block(j, carry, valid):
            start = pl.multiple_of(j * blk, blk)
            out = []
            for hh in range(hps):
                m, l, acc = carry[hh]
                kb = k_ref[hh, pl.ds(start, blk), :]
                vb = v_ref[hh, pl.ds(start, blk), :]
                s = lax.dot_general(qs[hh], kb, (((1,), (1,)), ((), ())), preferred_element_type=F32) * scale
                if valid is not None:
                    s = jnp.where(valid, s, NEG)
                m_new = jnp.maximum(m, jnp.max(s, axis=1, keepdims=True))
                p = _keep(valid, jnp.exp(s - m_new))
                a = jnp.exp(m - m_new)
                out.append((m_new, a * l + jnp.sum(p, axis=1, keepdims=True),
                            a * acc + jnp.dot(p.astype(BF16), vb, preferred_element_type=F32)))
            return tuple(out)

        init = tuple((jnp.full((blk, 1), NEG, F32), jnp.zeros((blk, 1), F32), jnp.zeros((blk, dv), F32)) for _ in range(hps))
        res = block(n, lax.fori_loop(0, n, lambda j, carry: block(j, carry, None), init), _diagonal_mask(blk))
        for hh in range(hps):
            m, l, acc = res[hh]
            o_ref[hh] = acc / l
            lse_ref[hh] = m + jnp.log(l)

    qs_ = pl.BlockSpec((hps, blk, dqk), lambda g, i: (g, i, 0))
    return pl.pallas_call(
        body, name=name, grid=(H // hps, T // blk),
        in_specs=[qs_, pl.BlockSpec((hps, T, dqk), lambda g, i: (g, 0, 0)), pl.BlockSpec((hps, T, dv), lambda g, i: (g, 0, 0))],
        out_specs=[pl.BlockSpec((hps, blk, dv), lambda g, i: (g, i, 0)), pl.BlockSpec((hps, blk, 1), lambda g, i: (g, i, 0))],
        out_shape=[jax.ShapeDtypeStruct((H, T, dv), F32), jax.ShapeDtypeStruct((H, T, 1), F32)],
        compiler_params=_params("parallel", "parallel"),
    )(q, k, v)


def _causal_bwd(q, k, v, o, do, lse, *, blk, hps, scale, name):
    H, T, dqk = q.shape
    dv = v.shape[2]

    def body(q_ref, k_ref, v_ref, o_ref, do_ref, lse_ref, dq_ref, dk_ref, dv_ref):
        n = pl.program_id(1)

        @pl.when(n == 0)
        def _():
            dk_ref[...] = jnp.zeros_like(dk_ref)
            dv_ref[...] = jnp.zeros_like(dv_ref)

        qs = [q_ref[hh] for hh in range(hps)]
        do16 = [do_ref[hh].astype(BF16) for hh in range(hps)]
        lses = [lse_ref[hh] for hh in range(hps)]
        cs = [-jnp.sum(do_ref[hh] * o_ref[hh], axis=1, keepdims=True) for hh in range(hps)]

        def block(j, dqs, valid):
            start = pl.multiple_of(j * blk, blk)
            out = []
            for hh in range(hps):
                kb = k_ref[hh, pl.ds(start, blk), :]
                vb = v_ref[hh, pl.ds(start, blk), :]
                s = lax.dot_general(qs[hh], kb, (((1,), (1,)), ((), ())), preferred_element_type=F32) * scale
                if valid is not None:
                    s = jnp.where(valid, s, NEG)
                p = _keep(valid, jnp.exp(s - lses[hh]))
                dp = lax.dot_general(do16[hh], vb, (((1,), (1,)), ((), ())), preferred_element_type=F32)
                ds16 = (p * (dp + cs[hh]) * scale).astype(BF16)
                dk_ref[hh, pl.ds(start, blk), :] += lax.dot_general(ds16, qs[hh], (((0,), (0,)), ((), ())), preferred_element_type=F32)
                dv_ref[hh, pl.ds(start, blk), :] += lax.dot_general(p.astype(BF16), do16[hh], (((0,), (0,)), ((), ())),
                                                                    preferred_element_type=F32)
                out.append(dqs[hh] + jnp.dot(ds16, kb, preferred_element_type=F32))
            return tuple(out)

        init = tuple(jnp.zeros((blk, dqk), F32) for _ in range(hps))
        res = block(n, lax.fori_loop(0, n, lambda j, dqs: block(j, dqs, None), init), _diagonal_mask(blk))
        for hh in range(hps):
            dq_ref[hh] = res[hh]

    qs_ = pl.BlockSpec((hps, blk, dqk), lambda g, i: (g, i, 0))
    os_ = pl.BlockSpec((hps, blk, dv), lambda g, i: (g, i, 0))
    ls_ = pl.BlockSpec((hps, blk, 1), lambda g, i: (g, i, 0))
    ks_ = pl.BlockSpec((hps, T, dqk), lambda g, i: (g, 0, 0))
    vs_ = pl.BlockSpec((hps, T, dv), lambda g, i: (g, 0, 0))
    return pl.pallas_call(
        body, name=name, grid=(H // hps, T // blk), in_specs=[qs_, ks_, vs_, os_, os_, ls_], out_specs=[qs_, ks_, vs_],
        out_shape=[jax.ShapeDtypeStruct((H, T, dqk), F32), jax.ShapeDtypeStruct((H, T, dqk), F32),
                   jax.ShapeDtypeStruct((H, T, dv), F32)],
        compiler_params=_params("parallel", "arbitrary"),
    )(q, k, v, o, do, lse)


def _band_geometry(blk, nb_seq, n_prev):
    def geo(i):
        seq = i // nb_seq
        n = i % nb_seq
        return seq, n, jnp.maximum(n - n_prev, 0)
    return geo


def _attn_fwd(q, k, v, slopes, sinks, *, blk, nb_seq, n_prev, n_back, scale, dist_scale, name):
    Hq, T, dqk = q.shape
    Hk, _, dv = v.shape
    group = Hq // Hk
    use_bias, use_sink = slopes is not None, sinks is not None
    geo = _band_geometry(blk, nb_seq, n_prev)

    def body(*refs):
        q_ref, k_ref, v_ref = refs[:3]
        slope_ref = refs[3] if use_bias else None
        sink_ref = refs[3 + use_bias] if use_sink else None
        o_ref, lse_ref = refs[-2:]
        h = pl.program_id(0)
        seq, n, lo = geo(pl.program_id(1))
        qb = q_ref[...]
        diff = lax.broadcasted_iota(jnp.int32, (blk, blk), 0) - lax.broadcasted_iota(jnp.int32, (blk, blk), 1)
        if use_sink:
            m0 = jnp.full((blk, 1), sink_ref[h], F32)
            l0 = jnp.ones((blk, 1), F32)
        else:
            m0 = jnp.full((blk, 1), NEG, F32)
            l0 = jnp.zeros((blk, 1), F32)

        def step(j, carry):
            m, l, acc = carry
            start = pl.multiple_of((seq * nb_seq + j) * blk, blk)
            kb = k_ref[pl.ds(start, blk), :]
            vb = v_ref[pl.ds(start, blk), :]
            s = lax.dot_general(qb, kb, (((1,), (1,)), ((), ())), preferred_element_type=F32) * scale
            rel = (n - j) * blk + diff
            valid = (rel >= 0) & (rel <= n_back)
            if use_bias:
                s = s - (slope_ref[h] * dist_scale) * rel.astype(F32)
            s = jnp.where(valid, s, NEG)
            m_new = jnp.maximum(m, jnp.max(s, axis=1, keepdims=True))
            p = jnp.where(valid, jnp.exp(s - m_new), 0.0)
            a = jnp.exp(m - m_new)
            l = a * l + jnp.sum(p, axis=1, keepdims=True)
            acc = a * acc + jnp.dot(p.astype(BF16), vb, preferred_element_type=F32)
            return m_new, l, acc

        m, l, acc = lax.fori_loop(lo, n + 1, step, (m0, l0, jnp.zeros((blk, dv), F32)))
        o_ref[...] = acc / l
        lse_ref[...] = m + jnp.log(l)

    smem = pl.BlockSpec(memory_space=pltpu.SMEM)
    in_specs = [pl.BlockSpec((None, blk, dqk), lambda h, i: (h, i, 0)),
                pl.BlockSpec((None, T, dqk), lambda h, i: (h // group, 0, 0)),
                pl.BlockSpec((None, T, dv), lambda h, i: (h // group, 0, 0))]
    ins = [q, k, v]
    if use_bias:
        in_specs.append(smem)
        ins.append(slopes)
    if use_sink:
        in_specs.append(smem)
        ins.append(sinks)
    return pl.pallas_call(
        body, name=name, grid=(Hq, T // blk), in_specs=in_specs,
        out_specs=[pl.BlockSpec((None, blk, dv), lambda h, i: (h, i, 0)), pl.BlockSpec((None, blk, 1), lambda h, i: (h, i, 0))],
        out_shape=[jax.ShapeDtypeStruct((Hq, T, dv), F32), jax.ShapeDtypeStruct((Hq, T, 1), F32)],
        compiler_params=_params("parallel", "parallel"),
    )(*ins)


def _attn_bwd(q, k, v, o, do, lse, dlse, slopes, sinks, *, blk, nb_seq, n_prev, n_back, scale, dist_scale, name):
    Hq, T, dqk = q.shape
    Hk, _, dv = v.shape
    group = Hq // Hk
    use_bias, use_sink, use_dlse = slopes is not None, sinks is not None, dlse is not None
    geo = _band_geometry(blk, nb_seq, n_prev)

    def body(*refs):
        q_ref, k_ref, v_ref, o_ref, do_ref, lse_ref = refs[:6]
        pos = 6
        dlse_ref = refs[pos] if use_dlse else None
        pos += use_dlse
        slope_ref = refs[pos] if use_bias else None
        pos += use_bias
        sink_ref = refs[pos] if use_sink else None
        pos += use_sink
        dq_ref, dk_ref, dv_ref = refs[pos:pos + 3]
        dsink_ref = refs[pos + 3] if use_sink else None
        h = pl.program_id(0)
        i = pl.program_id(1)
        seq, n, lo = geo(i)
        qb = q_ref[...]
        dob = do_ref[...]
        do16 = dob.astype(BF16)
        lse = lse_ref[...]
        c = -jnp.sum(dob * o_ref[...], axis=1, keepdims=True)
        if use_dlse:
            c = c + dlse_ref[...]
        diff = lax.broadcasted_iota(jnp.int32, (blk, blk), 0) - lax.broadcasted_iota(jnp.int32, (blk, blk), 1)

        @pl.when((h % group == 0) & (i == 0))
        def _():
            dk_ref[...] = jnp.zeros_like(dk_ref)
            dv_ref[...] = jnp.zeros_like(dv_ref)

        def step(j, dq):
            start = pl.multiple_of((seq * nb_seq + j) * blk, blk)
            kb = k_ref[pl.ds(start, blk), :]
            vb = v_ref[pl.ds(start, blk), :]
            s = lax.dot_general(qb, kb, (((1,), (1,)), ((), ())), preferred_element_type=F32) * scale
            rel = (n - j) * blk + diff
            valid = (rel >= 0) & (rel <= n_back)
            if use_bias:
                s = s - (slope_ref[h] * dist_scale) * rel.astype(F32)
            p = jnp.where(valid, jnp.exp(jnp.where(valid, s, NEG) - lse), 0.0)
            dp = lax.dot_general(do16, vb, (((1,), (1,)), ((), ())), preferred_element_type=F32)
            ds16 = (p * (dp + c) * scale).astype(BF16)
            dk_ref[pl.ds(start, blk), :] += lax.dot_general(ds16, qb, (((0,), (0,)), ((), ())), preferred_element_type=F32)
            dv_ref[pl.ds(start, blk), :] += lax.dot_general(p.astype(BF16), do16, (((0,), (0,)), ((), ())), preferred_element_type=F32)
            return dq + jnp.dot(ds16, kb, preferred_element_type=F32)

        dq_ref[...] = lax.fori_loop(lo, n + 1, step, jnp.zeros((blk, dqk), F32))
        if use_sink:
            @pl.when(i == 0)
            def _():
                dsink_ref[...] = jnp.zeros_like(dsink_ref)

            dsink_ref[...] += jnp.full(dsink_ref.shape, jnp.sum(jnp.exp(sink_ref[h] - lse) * c), F32)

    smem = pl.BlockSpec(memory_space=pltpu.SMEM)
    qs = pl.BlockSpec((None, blk, dqk), lambda h, i: (h, i, 0))
    os_ = pl.BlockSpec((None, blk, dv), lambda h, i: (h, i, 0))
    ls = pl.BlockSpec((None, blk, 1), lambda h, i: (h, i, 0))
    ks = pl.BlockSpec((None, T, dqk), lambda h, i: (h // group, 0, 0))
    vs = pl.BlockSpec((None, T, dv), lambda h, i: (h // group, 0, 0))
    in_specs = [qs, ks, vs, os_, os_, ls]
    ins = [q, k, v, o, do, lse]
    if use_dlse:
        in_specs.append(ls)
        ins.append(dlse)
    if use_bias:
        in_specs.append(smem)
        ins.append(slopes)
    if use_sink:
        in_specs.append(smem)
        ins.append(sinks)
    out_specs = [qs, ks, vs]
    out_shape = [jax.ShapeDtypeStruct((Hq, T, dqk), F32), jax.ShapeDtypeStruct((Hk, T, dqk), F32),
                 jax.ShapeDtypeStruct((Hk, T, dv), F32)]
    if use_sink:
        out_specs.append(pl.BlockSpec((None, 1, 128), lambda h, i: (h, 0, 0)))
        out_shape.append(jax.ShapeDtypeStruct((Hq, 1, 128), F32))
    return pl.pallas_call(
        body, name=name, grid=(Hq, T // blk), in_specs=in_specs, out_specs=out_specs, out_shape=out_shape,
        compiler_params=_params("arbitrary", "arbitrary"),
    )(*ins)


def _tri_sum(x, upper):
    hi = x.astype(BF16)
    lo = (x - hi.astype(F32)).astype(BF16)
    return jnp.dot(hi, upper, preferred_element_type=F32) + jnp.dot(lo, upper, preferred_element_type=F32)


def _keep(mask, x):
    return x if mask is None else jnp.where(mask, x, 0.0)


def _stick_terms(qb, kb, scale, strict):
    z = lax.dot_general(qb, kb, (((1,), (1,)), ((), ())), preferred_element_type=F32) * scale
    e = jnp.exp(-jnp.abs(z))
    lb = jnp.minimum(z, 0.0) - jnp.log(1.0 + e)
    return z, e, lb, _keep(strict, lb - z)


def _stick_fwd(q, k, v, *, blk, hps, scale, name):
    H, T, dh = q.shape

    def body(q_ref, k_ref, v_ref, o_ref, tot_ref):
        n = pl.program_id(1)
        qs = [q_ref[hh] for hh in range(hps)]
        r_i = lax.broadcasted_iota(jnp.int32, (blk, blk), 0)
        c_i = lax.broadcasted_iota(jnp.int32, (blk, blk), 1)
        upper = (r_i > c_i).astype(BF16)

        def block(j, carry, strict):
            start = pl.multiple_of(j * blk, blk)
            out = []
            for hh in range(hps):
                run, acc = carry[hh]
                kb = k_ref[hh, pl.ds(start, blk), :]
                vb = v_ref[hh, pl.ds(start, blk), :]
                _, _, lb, lk = _stick_terms(qs[hh], kb, scale, strict)
                w = _keep(strict, jnp.exp(lb + run + _tri_sum(lk, upper)))
                out.append((run + jnp.sum(lk, axis=1, keepdims=True), acc + jnp.dot(w.astype(BF16), vb, preferred_element_type=F32)))
            return tuple(out)

        init = tuple((jnp.zeros((blk, 1), F32), jnp.zeros((blk, dh), F32)) for _ in range(hps))
        res = lax.fori_loop(1, n + 1, lambda t, carry: block(n - t, carry, None), block(n, init, r_i > c_i))
        for hh in range(hps):
            tot_ref[hh], o_ref[hh] = res[hh]

    qs_ = pl.BlockSpec((hps, blk, dh), lambda g, i: (g, i, 0))
    ks_ = pl.BlockSpec((hps, T, dh), lambda g, i: (g, 0, 0))
    ts_ = pl.BlockSpec((hps, blk, 1), lambda g, i: (g, i, 0))
    return pl.pallas_call(
        body, name=name, grid=(H // hps, T // blk), in_specs=[qs_, ks_, ks_], out_specs=[qs_, ts_],
        out_shape=[jax.ShapeDtypeStruct((H, T, dh), F32), jax.ShapeDtypeStruct((H, T, 1), F32)],
        compiler_params=_params("parallel", "parallel"),
    )(q, k, v)


def _stick_bwd(q, k, v, tot, do, *, blk, hps, scale, name):
    H, T, dh = q.shape

    def body(q_ref, k_ref, v_ref, tot_ref, do_ref, dq_ref, dk_ref, dv_ref):
        n = pl.program_id(1)
        qs = [q_ref[hh] for hh in range(hps)]
        do16 = [do_ref[hh].astype(BF16) for hh in range(hps)]
        tots = [tot_ref[hh] for hh in range(hps)]
        r_i = lax.broadcasted_iota(jnp.int32, (blk, blk), 0)
        c_i = lax.broadcasted_iota(jnp.int32, (blk, blk), 1)
        upto = (r_i <= c_i).astype(BF16)
        below = (r_i < c_i).astype(BF16)

        @pl.when(n == 0)
        def _():
            dk_ref[...] = jnp.zeros_like(dk_ref)
            dv_ref[...] = jnp.zeros_like(dv_ref)

        def block(j, carry, strict):
            start = pl.multiple_of(j * blk, blk)
            out = []
            for hh in range(hps):
                run, grun, dq = carry[hh]
                kb = k_ref[hh, pl.ds(start, blk), :]
                vb = v_ref[hh, pl.ds(start, blk), :]
                z, e, lb, lk = _stick_terms(qs[hh], kb, scale, strict)
                w = _keep(strict, jnp.exp(lb + tots[hh] - (run + _tri_sum(lk, upto))))
                dw = lax.dot_general(do16[hh], vb, (((1,), (1,)), ((), ())), preferred_element_type=F32)
                g = w * dw
                before = grun + _tri_sum(g, below)
                inv = 1.0 / (1.0 + e)
                sig = jnp.where(z >= 0, inv, e * inv)
                dz16 = (_keep(strict, g * (1.0 - sig) - sig * before) * scale).astype(BF16)
                dk_ref[hh, pl.ds(start, blk), :] += lax.dot_general(dz16, qs[hh], (((0,), (0,)), ((), ())), preferred_element_type=F32)
                dv_ref[hh, pl.ds(start, blk), :] += lax.dot_general(w.astype(BF16), do16[hh], (((0,), (0,)), ((), ())),
                                                                    preferred_element_type=F32)
                out.append((run + jnp.sum(lk, axis=1, keepdims=True), grun + jnp.sum(g, axis=1, keepdims=True),
                            dq + jnp.dot(dz16, kb, preferred_element_type=F32)))
            return tuple(out)

        zero = jnp.zeros((blk, 1), F32)
        init = tuple((zero, zero, jnp.zeros((blk, dh), F32)) for _ in range(hps))
        res = block(n, lax.fori_loop(0, n, lambda j, carry: block(j, carry, None), init), r_i > c_i)
        for hh in range(hps):
            dq_ref[hh] = res[hh][2]

    qs_ = pl.BlockSpec((hps, blk, dh), lambda g, i: (g, i, 0))
    ks_ = pl.BlockSpec((hps, T, dh), lambda g, i: (g, 0, 0))
    ts_ = pl.BlockSpec((hps, blk, 1), lambda g, i: (g, i, 0))
    shp = jax.ShapeDtypeStruct((H, T, dh), F32)
    return pl.pallas_call(
        body, name=name, grid=(H // hps, T // blk), in_specs=[qs_, ks_, ks_, ts_, qs_], out_specs=[qs_, ks_, ks_],
        out_shape=[shp, shp, shp],
        compiler_params=_params("parallel", "arbitrary"),
    )(q, k, v, tot, do)


def _mix_fwd(outs, lses, *, name, rows=512):
    H, T, dh = outs[0].shape
    rows = _tile(T, rows)

    def body(o0, o1, o2, l0, l1, l2, y_ref):
        ls = [l0[...], l1[...], l2[...]]
        mx = jnp.maximum(jnp.maximum(ls[0], ls[1]), ls[2])
        es = [jnp.exp(x - mx) for x in ls]
        den = es[0] + es[1] + es[2]
        y_ref[...] = (es[0] * o0[...] + es[1] * o1[...] + es[2] * o2[...]) / den

    os_ = pl.BlockSpec((None, rows, dh), lambda h, i: (h, i, 0))
    ls_ = pl.BlockSpec((None, rows, 1), lambda h, i: (h, i, 0))
    return pl.pallas_call(
        body, name=name, grid=(H, T // rows), in_specs=[os_] * 3 + [ls_] * 3, out_specs=os_,
        out_shape=jax.ShapeDtypeStruct((H, T, dh), F32),
        compiler_params=_params("parallel", "parallel"),
    )(*outs, *lses)


def _mix_bwd(outs, lses, dy, *, name, rows=512):
    H, T, dh = outs[0].shape
    rows = _tile(T, rows)

    def body(o0, o1, o2, l0, l1, l2, dy_ref, d0, d1, d2, g0, g1, g2):
        ls = [l0[...], l1[...], l2[...]]
        mx = jnp.maximum(jnp.maximum(ls[0], ls[1]), ls[2])
        es = [jnp.exp(x - mx) for x in ls]
        den = es[0] + es[1] + es[2]
        mix = [e / den for e in es]
        dyv = dy_ref[...]
        dm = [jnp.sum(dyv * o[...], axis=1, keepdims=True) for o in (o0, o1, o2)]
        avg = mix[0] * dm[0] + mix[1] * dm[1] + mix[2] * dm[2]
        for d_ref, g_ref, w, t in zip((d0, d1, d2), (g0, g1, g2), mix, dm):
            d_ref[...] = w * dyv
            g_ref[...] = w * (t - avg)

    os_ = pl.BlockSpec((None, rows, dh), lambda h, i: (h, i, 0))
    ls_ = pl.BlockSpec((None, rows, 1), lambda h, i: (h, i, 0))
    so = jax.ShapeDtypeStruct((H, T, dh), F32)
    sl = jax.ShapeDtypeStruct((H, T, 1), F32)
    res = pl.pallas_call(
        body, name=name, grid=(H, T // rows), in_specs=[os_] * 3 + [ls_] * 3 + [os_], out_specs=[os_] * 3 + [ls_] * 3,
        out_shape=[so] * 3 + [sl] * 3,
        compiler_params=_params("parallel", "parallel"),
    )(*outs, *lses, dy)
    return res[:3], res[3:]


def _position():
    return lax.axis_index("x"), lax.axis_index("y"), lax.axis_index("c")


def _other_chips(x, y):
    return [(1 - x, y), (x, 1 - y), (1 - x, 1 - y)]


HBM_SPEC = pl.BlockSpec(memory_space=pltpu.HBM)
SEM_SPEC = pl.BlockSpec(memory_space=pltpu.SEMAPHORE)
ANY_SPEC = pl.BlockSpec(memory_space=pl.ANY)
SPLIT_COPY = pltpu.CompilerParams(has_side_effects=pltpu.SideEffectType.DATAFLOW_SIDE_EFFECTING)


def _in_hbm(a):
    return pltpu.with_memory_space_constraint(a, pltpu.HBM)


SAME_CORE_OF_OTHER_CHIPS = ((1, 0, 0), (0, 1, 0), (1, 1, 0))
ALL_OTHER_DEVICES = ((0, 0, 1), (1, 0, 0), (0, 1, 0), (1, 1, 0), (1, 0, 1), (0, 1, 1), (1, 1, 1))


def _peer_copies(srcs, lands, send_sems, recv_sems, relations, src_of, dst_of):
    me = _position()
    copies = []
    for w in range(len(srcs)):
        for k, flips in enumerate(relations):
            peer = tuple(1 - p if f else p for p, f in zip(me, flips))
            i = len(relations) * w + k
            copies.append(pltpu.make_async_remote_copy(
                src_ref=src_of(srcs[w], peer), dst_ref=dst_of(lands[w], k, peer, me), send_sem=send_sems[i],
                recv_sem=recv_sems[i], device_id=peer, device_id_type=MESH))
    return copies


def _copies_start(srcs, land_shapes, after, relations, src_of, dst_of, *, name):
    n = len(srcs)
    m = len(relations) * n

    def body(*refs):
        src_refs, land_refs = refs[:n], refs[n:2 * n]
        send_sems, recv_sems = refs[2 * n + 1:2 * n + 1 + m], refs[2 * n + 1 + m:2 * n + 1 + 2 * m]
        token = refs[-1]
        for cp in _peer_copies(src_refs, land_refs, send_sems, recv_sems, relations, src_of, dst_of):
            cp.start()
        token[...] = jnp.zeros_like(token)

    lands = [_in_hbm(lax.empty(s.shape, s.dtype)) for s in land_shapes]
    out_shape = ([pltpu.SemaphoreType.DMA(())] * (2 * m)
                 + [pltpu.HBM(a.shape, a.dtype) for a in srcs] + [pltpu.HBM(s.shape, s.dtype) for s in land_shapes]
                 + [jax.ShapeDtypeStruct((8, 128), F32)])
    res = pl.pallas_call(
        body, name=name, in_specs=[HBM_SPEC] * (2 * n) + [ANY_SPEC],
        out_specs=[SEM_SPEC] * (2 * m) + [HBM_SPEC] * (2 * n) + [pl.BlockSpec(memory_space=pltpu.VMEM)],
        out_shape=out_shape, input_output_aliases={i: 2 * m + i for i in range(2 * n)}, compiler_params=SPLIT_COPY,
    )(*[_in_hbm(a) for a in srcs], *lands, after)
    return (list(res[:m]), list(res[m:2 * m]), list(res[2 * m:2 * m + n]), list(res[2 * m + n:2 * m + 2 * n]), res[-1])


def _copies_wait(started, after, relations, src_of, dst_of, *, name):
    send_sems, recv_sems, srcs, lands, _ = started
    n = len(srcs)
    m = len(relations) * n

    def body(*refs):
        src_refs, land_refs = refs[:n], refs[n:2 * n]
        send_refs, recv_refs = refs[2 * n:2 * n + m], refs[2 * n + m:2 * n + 2 * m]
        for cp in _peer_copies(src_refs, land_refs, send_refs, recv_refs, relations, src_of, dst_of):
            cp.wait_send()
            cp.wait_recv()

    res = pl.pallas_call(
        body, name=name, in_specs=[HBM_SPEC] * (2 * n) + [SEM_SPEC] * (2 * m) + [ANY_SPEC], out_specs=[HBM_SPEC] * (2 * n),
        out_shape=[pltpu.HBM(a.shape, a.dtype) for a in srcs + lands],
        input_output_aliases={i: i for i in range(2 * n)}, compiler_params=SPLIT_COPY,
    )(*srcs, *lands, *send_sems, *recv_sems, after)
    return list(res[:n]), list(res[n:])


def _half_rows(ref, core):
    half = ref.shape[-2] // 2
    return pl.ds(core * half, half)


def _gather_src(src, peer):
    return src.at[_half_rows(src, peer[2]), :]


def _gather_dst(land, k, peer, me):
    return land.at[2 * me[0] + me[1], _half_rows(land, me[2]), :]


def _gather_landed(land, k, peer, me):
    return land.at[2 * peer[0] + peer[1], _half_rows(land, me[2]), :]


def _exchange_src(src, peer):
    return src.at[2 * peer[0] + peer[1], _half_rows(src, peer[2]), :]


def _exchange_dst(land, k, peer, me):
    return land.at[k]


def _small_src(src, peer):
    return src


def _small_dst(land, k, peer, me):
    return land.at[4 * me[0] + 2 * me[1] + me[2]]


def _small_landed(land, k, peer, me):
    return land.at[4 * peer[0] + 2 * peer[1] + peer[2]]


def _forward_halves(bufs, *, name):
    n = len(bufs)

    def body(*refs):
        ins, outs = refs[:n], refs[n:2 * n]
        send_sems, recv_sems = refs[2 * n:]
        x, y, c = _position()
        chips = _other_chips(x, y)

        def copy(w, j, chip, core):
            holder = (x, y, core)
            return pltpu.make_async_remote_copy(src_ref=_gather_landed(ins[w], j, chip, holder),
                                                dst_ref=_gather_landed(outs[w], j, chip, holder),
                                                send_sem=send_sems.at[w, j], recv_sem=recv_sems.at[w, j],
                                                device_id=(x, y, 1 - c), device_id_type=MESH)

        sends = [copy(w, j, chip, c) for w in range(n) for j, chip in enumerate(chips)]
        for cp in sends:
            cp.start()
        for w in range(n):
            for j, chip in enumerate(chips):
                copy(w, j, chip, 1 - c).wait_recv()
        for cp in sends:
            cp.wait_send()

    return pl.pallas_call(
        body, name=name, in_specs=[ANY_SPEC] * n, out_specs=[ANY_SPEC] * n,
        out_shape=[jax.ShapeDtypeStruct(a.shape, a.dtype) for a in bufs], input_output_aliases={w: w for w in range(n)},
        scratch_shapes=[pltpu.SemaphoreType.DMA((n, 3)), pltpu.SemaphoreType.DMA((n, 3))],
    )(*bufs)


def _share_halves(bufs, *, name):
    n = len(bufs)

    def body(*refs):
        ins, outs = refs[:n], refs[n:2 * n]
        send_sems, recv_sems = refs[2 * n:]
        x, y, c = _position()
        sends = []
        for w in range(n):
            sends.append(pltpu.make_async_remote_copy(src_ref=ins[w].at[c], dst_ref=outs[w].at[c], send_sem=send_sems.at[w],
                                                      recv_sem=recv_sems.at[w], device_id=(x, y, 1 - c), device_id_type=MESH))
            sends[-1].start()
        for w in range(n):
            pltpu.make_async_remote_copy(src_ref=ins[w].at[1 - c], dst_ref=outs[w].at[1 - c], send_sem=send_sems.at[w],
                                         recv_sem=recv_sems.at[w], device_id=(x, y, 1 - c), device_id_type=MESH).wait_recv()
        for cp in sends:
            cp.wait_send()

    hbm = pl.BlockSpec(memory_space=pl.ANY)
    return pl.pallas_call(
        body, name=name, in_specs=[hbm] * n, out_specs=[hbm] * n,
        out_shape=[jax.ShapeDtypeStruct(a.shape, a.dtype) for a in bufs],
        input_output_aliases={w: w for w in range(n)},
        scratch_shapes=[pltpu.SemaphoreType.DMA((n,)), pltpu.SemaphoreType.DMA((n,))],
    )(*bufs)


def _all_gather8(v, *, name):
    m, n = v.shape

    def body(v_ref, out_ref, send_sems, recv_sems, local_sem):
        x, y, c = _position()
        me, sibling = (x, y, c), (x, y, 1 - c)
        chips = _other_chips(x, y)

        def rows(px, py, pc):
            return out_ref.at[pl.ds((4 * px + 2 * py + pc) * m, m), :]

        def copy(k, block, to, src=None):
            return pltpu.make_async_remote_copy(src_ref=rows(*block) if src is None else src, dst_ref=rows(*block),
                                                send_sem=send_sems.at[k], recv_sem=recv_sems.at[k], device_id=to, device_id_type=MESH)

        mine = pltpu.make_async_copy(v_ref, rows(*me), local_sem)
        mine.start()
        first = [copy(0, me, sibling, src=v_ref)]
        first += [copy(1 + j, me, (*chip, c), src=v_ref) for j, chip in enumerate(chips)]
        for cp in first:
            cp.start()
        passed = [copy(4 + j, (*chip, c), sibling) for j, chip in enumerate(chips)]
        for j, chip in enumerate(chips):
            copy(1 + j, (*chip, c), me).wait_recv()
            passed[j].start()
        copy(0, sibling, me).wait_recv()
        for j, chip in enumerate(chips):
            copy(4 + j, (*chip, 1 - c), me).wait_recv()
        for cp in first + passed:
            cp.wait_send()
        mine.wait()

    vmem = pl.BlockSpec(memory_space=pltpu.VMEM)
    return pl.pallas_call(
        body, name=name, in_specs=[vmem], out_specs=vmem, out_shape=jax.ShapeDtypeStruct((8 * m, n), v.dtype),
        scratch_shapes=[pltpu.SemaphoreType.DMA((7,)), pltpu.SemaphoreType.DMA((7,)), pltpu.SemaphoreType.DMA],
    )(v)


def _sum_parts(own, landed, where, *, name, rows=256):
    n_peers, half, C = landed.shape
    rows = _tile(half, rows)
    nb = half // rows

    def body(where_ref, own_ref, land_ref, o_ref):
        acc = own_ref[...].astype(F32)
        for k in range(n_peers):
            acc = acc + land_ref[k].astype(F32)
        o_ref[...] = acc

    grid_spec = pltpu.PrefetchScalarGridSpec(
        num_scalar_prefetch=1, grid=(nb,),
        in_specs=[pl.BlockSpec((None, rows, C), lambda i, where_ref: (where_ref[0], where_ref[1] * nb + i, 0)),
                  pl.BlockSpec((n_peers, rows, C), lambda i, where_ref: (0, i, 0))],
        out_specs=pl.BlockSpec((None, rows, C), lambda i, where_ref: (where_ref[1], i, 0)))
    return pl.pallas_call(
        body, name=name, grid_spec=grid_spec, out_shape=jax.ShapeDtypeStruct((2, half, C), F32),
        compiler_params=_params("parallel"),
    )(where, own, landed)


def _sum_slabs(a, *, name, rows=256):
    P, R, C = a.shape
    rows = _tile(R, rows)

    def body(a_ref, o_ref):
        acc = a_ref[0].astype(F32)
        for p in range(1, P):
            acc = acc + a_ref[p].astype(F32)
        o_ref[...] = acc

    return pl.pallas_call(
        body, name=name, grid=(R // rows,), in_specs=[pl.BlockSpec((P, rows, C), lambda i: (0, i, 0))],
        out_specs=pl.BlockSpec((rows, C), lambda i: (i, 0)),
        out_shape=jax.ShapeDtypeStruct((R, C), F32), compiler_params=_params("parallel"),
    )(a)


def _reduce_scatter_start(grads, after=None, *, name):
    lands = [jax.ShapeDtypeStruct((len(ALL_OTHER_DEVICES), g.shape[1] // 2, g.shape[2]), g.dtype) for g in grads]
    return _copies_start(grads, lands, grads[0] if after is None else after, ALL_OTHER_DEVICES, _exchange_src, _exchange_dst,
                         name=name + "_start")


def _reduce_scatter_finish(started, after, *, name):
    core = lax.axis_index("c").astype(jnp.int32)
    chip = (2 * lax.axis_index("x") + lax.axis_index("y")).astype(jnp.int32)
    where = jnp.stack([chip, core])
    sums, landed = _copies_wait(started, after, ALL_OTHER_DEVICES, _exchange_src, _exchange_dst, name=name + "_wait")
    reduced = [_sum_parts(s, a, where, name=f"{name}_sum{w}") for w, (s, a) in enumerate(zip(sums, landed))]
    both = _share_halves(reduced, name=name + "_share")
    return [b.reshape(2 * b.shape[1], b.shape[2]) for b in both]


def _to_heads(t, heads, dil=1):
    S = t.shape[0]
    d = t.shape[1] // heads
    return jnp.transpose(t.reshape(S // dil, dil, heads, d), (2, 1, 0, 3)).reshape(heads, S, d)


def _from_heads(t, dil=1):
    heads, S, d = t.shape
    return jnp.transpose(t.reshape(heads, dil, S // dil, d), (2, 1, 0, 3)).reshape(S, heads * d)


def _unstride(t, dil):
    heads, S, d = t.shape
    return jnp.transpose(t.reshape(heads, dil, S // dil, d), (0, 2, 1, 3)).reshape(heads, S, d)


def _restride(t, dil):
    heads, S, d = t.shape
    return jnp.transpose(t.reshape(heads, S // dil, dil, d), (0, 2, 1, 3)).reshape(heads, S, d)


def _pad_cols(t, width, padded):
    S = t.shape[0]
    return jnp.pad(t.reshape(S, N_CHIPS, width), ((0, 0), (0, 0), (0, padded - width))).reshape(S, N_CHIPS * padded)


def _column_reader(t, width, padded):
    def take(lo, hi):
        pieces = []
        for p in range(N_CHIPS):
            a, b = max(lo, p * width), min(hi, (p + 1) * width)
            if a < b:
                pieces.append(t[:, p * padded + a - p * width:p * padded + b - p * width])
        return pieces[0] if len(pieces) == 1 else jnp.concatenate(pieces, axis=-1)
    return take


def _alibi(n):
    return 2.0 ** (-8.0 * jnp.arange(1, n + 1, dtype=F32) / n)


B_KW = [dict(hps=4, nb_seq=SEQ // d // ATT_BLK, n_back=w // d, scale=1.0 / math.sqrt(HEAD_DIM), dist_scale=d)
        for w, d in B_PATTERNS]
A_KW = dict(hps=A_Q_HEADS // A_KV_HEADS, nb_seq=SEQ // ATT_BLK, n_back=A_WINDOW - 1, scale=1.0 / math.sqrt(HEAD_DIM), dist_scale=1)
D_KW = dict(blk=256, hps=2, scale=1.0 / math.sqrt(D_NOPE + D_ROPE))
C_KW = dict(blk=256, hps=2, scale=1.0 / math.sqrt(HEAD_DIM))


def _front_block(t):
    return jnp.pad(t, ((0, 0), (ATT_BLK, 0), (0, 0)))


def _rope_tables(reps):
    inv_freq = ROPE_BASE ** (-jnp.arange(0, D_ROPE, 2, dtype=F32) / D_ROPE)
    ang = jnp.arange(SEQ, dtype=F32)[:, None] * inv_freq[None, :]
    return jnp.tile(jnp.cos(ang), (1, reps)), jnp.tile(jnp.sin(ang), (1, reps))


def _mlp_fwd(x, x16, w1, w2, g, b, tag):
    hid, act = _matmul(x16, w1, mode="nn", out_dtype=BF16, epilogue="relu2", name=f"mlp{tag}_up")
    m = _matmul(act, w2, mode="nn", out_dtype=F32, name=f"mlp{tag}_down")
    y, y16, u = _norm_fwd(x, m, g, b, center=True, eps=LN_EPS, alpha=ALPHA, name=f"ln2_{tag}")
    return y, y16, (x16, hid, act, u)


def _mlp_bwd(dy, saved, w1, w2, g, tag, after=None):
    x16, hid, act, u = saved
    du, du16, dg, db = _norm_bwd(dy, u, g, center=True, eps=LN_EPS, name=f"ln2_{tag}_bwd", after=after)
    dw2 = _matmul_tn(act, du16, shards=1, name=f"mlp{tag}_dw2")
    dhid = _matmul(du16, w2, mode="nt", out_dtype=BF16, epilogue="drelu2", extra=hid, name=f"mlp{tag}_dact")
    dw1 = _matmul_tn(x16, dhid, shards=N_CHIPS, name=f"mlp{tag}_dw1")
    dx = _matmul(dhid, w1, mode="nt", out_dtype=F32, epilogue="add", extra=du, alpha=ALPHA, name=f"mlp{tag}_dx")
    return dx, dw1, dw2, dg, db


def _even_fwd(x16, w_in, sinks):
    h = _column_reader(_matmul(x16, w_in, mode="nn", out_dtype=BF16, name="even_in"), EVEN_IN // N_CHIPS, EVEN_IN_PAD)
    qa = _to_heads(h(0, A_Q_W), A_Q_HEADS)
    ka = _front_block(_to_heads(h(A_Q_W, A_Q_W + A_KV_W), A_KV_HEADS))
    va = _front_block(_to_heads(h(A_Q_W + A_KV_W, A_Q_W + 2 * A_KV_W), A_KV_HEADS))
    slopes_a, slopes_b = _alibi(A_Q_HEADS), _alibi(B_HEADS)
    oa, lse_a = _band_fwd(qa, ka, va, slopes_a, sinks, name="swa_fwd", **A_KW)
    qkv_b, outs, lses = [], [], []
    for gi, (_, dil) in enumerate(B_PATTERNS):
        base = A_Q_W + 2 * A_KV_W + gi * 3 * B_W
        q, k, v = (_to_heads(h(base + i * B_W, base + (i + 1) * B_W), B_HEADS, dil) for i in range(3))
        k, v = _front_block(k), _front_block(v)
        o, lse = _band_fwd(q, k, v, slopes_b, None, name=f"dil{gi}_fwd", **B_KW[gi])
        qkv_b.append((q, k, v, o, lse))
        outs.append(_unstride(o, dil))
        lses.append(_unstride(lse, dil))
    ob = _mix_fwd(outs, lses, name="dil_mix")
    y16 = jnp.concatenate([_from_heads(oa), _from_heads(ob)], axis=-1).astype(BF16)
    return y16, (x16, qa, ka, va, oa, lse_a, qkv_b, outs, lses, y16)


def _even_bwd(dmix16, du, saved, w_in, sinks, w_out, send_w_out, send_w_in):
    x16, qa, ka, va, oa, lse_a, qkv_b, outs, lses, y16 = saved
    slopes_a, slopes_b = _alibi(A_Q_HEADS), _alibi(B_HEADS)
    sent = send_w_out(_matmul_tn(y16, dmix16, shards=N_CHIPS, name="even_dwout"))
    dy = _matmul(dmix16, w_out, mode="nt", out_dtype=F32, name="even_dy", after=sent)
    doa = _to_heads(dy[:, :A_Q_W], A_Q_HEADS)
    dob = _to_heads(dy[:, A_Q_W:], B_HEADS)
    dqa, dka, dva, dsink = _band_bwd(qa, ka, va, oa, doa, lse_a, None, slopes_a, sinks, name="swa_bwd", **A_KW)
    douts, dlses = _mix_bwd(outs, lses, dob, name="dil_mix_bwd")
    parts = [_from_heads(dqa), _from_heads(dka[:, ATT_BLK:]), _from_heads(dva[:, ATT_BLK:])]
    for gi, (_, dil) in enumerate(B_PATTERNS):
        q, k, v, o, lse = qkv_b[gi]
        dq, dk, dv = _band_bwd(q, k, v, o, _restride(douts[gi], dil), lse, _restride(dlses[gi], dil), slopes_b, None,
                               name=f"dil{gi}_bwd", **B_KW[gi])
        parts += [_from_heads(dq, dil), _from_heads(dk[:, ATT_BLK:], dil), _from_heads(dv[:, ATT_BLK:], dil)]
    dh16 = _pad_cols(jnp.concatenate(parts, axis=-1), EVEN_IN // N_CHIPS, EVEN_IN_PAD).astype(BF16)
    sent = send_w_in(_matmul_tn(x16, dh16, shards=N_CHIPS, name="even_dwin"), dsink[:, 0, 0])
    return _matmul(dh16, w_in, mode="nt", out_dtype=F32, epilogue="add", extra=du, alpha=ALPHA, name="even_dx", after=sent)


def _odd_fwd(x16, w_in, qn_g, kvn_g, w_uq, w_ukv, w_out):
    S = x16.shape[0]
    h = _column_reader(_matmul(x16, w_in, mode="nn", out_dtype=F32, name="odd_in"), ODD_IN // N_CHIPS, ODD_IN_PAD)
    qc, kc, vc = (_to_heads(h(i * C_W, (i + 1) * C_W).astype(BF16), C_HEADS) for i in range(3))
    cq = h(3 * C_W, 3 * C_W + D_Q_RANK)
    ckv = h(3 * C_W + D_Q_RANK, 3 * C_W + D_Q_RANK + D_KV_RANK)
    kr = h(3 * C_W + D_Q_RANK + D_KV_RANK, ODD_IN)
    oc, tot = _stick_fwd(qc, kc, vc, name="stick_fwd", **C_KW)
    _, cqn16, _ = _norm_fwd(cq, None, qn_g, None, center=False, eps=RMS_EPS, alpha=1.0, name="q_rms")
    _, ckvn16, _ = _norm_fwd(ckv, None, kvn_g, None, center=False, eps=RMS_EPS, alpha=1.0, name="kv_rms")
    q = _matmul(cqn16, w_uq, mode="nn", out_dtype=F32, name="mla_uq").reshape(S, D_HEADS, D_NOPE + D_ROPE)
    kv = _matmul(ckvn16, w_ukv, mode="nn", out_dtype=F32, name="mla_ukv").reshape(S, D_HEADS, D_NOPE + D_V)
    half = D_ROPE // 2
    q_rope = q[..., D_NOPE:]
    x1 = jnp.concatenate([q_rope[..., :half].reshape(S, D_HEADS * half), kr[:, :half]], axis=-1)
    x2 = jnp.concatenate([q_rope[..., half:].reshape(S, D_HEADS * half), kr[:, half:]], axis=-1)
    cos, sin = _rope_tables(D_HEADS + 1)
    y1, y2 = _rope(x1[None], x2[None], cos, sin, name="rope_fwd")
    nq = D_HEADS * half
    q_rot = jnp.concatenate([y1[:, :nq].reshape(S, D_HEADS, half), y2[:, :nq].reshape(S, D_HEADS, half)], axis=-1)
    kr_rot = jnp.concatenate([y1[:, nq:], y2[:, nq:]], axis=-1)
    qd = jnp.transpose(jnp.concatenate([q[..., :D_NOPE], q_rot], axis=-1), (1, 0, 2)).astype(BF16)
    kd = jnp.transpose(jnp.concatenate([kv[..., :D_NOPE], jnp.broadcast_to(kr_rot[:, None, :], (S, D_HEADS, D_ROPE))], axis=-1),
                       (1, 0, 2)).astype(BF16)
    vd = jnp.transpose(kv[..., D_NOPE:], (1, 0, 2)).astype(BF16)
    od, lse_d = _causal_fwd(qd, kd, vd, name="mla_fwd", **D_KW)
    y16 = jnp.concatenate([_from_heads(oc), _from_heads(od)], axis=-1).astype(BF16)
    mixed = _matmul(y16, w_out, mode="nn", out_dtype=F32, name="odd_out")
    return mixed, (x16, qc, kc, vc, tot, cq, ckv, cqn16, ckvn16, qd, kd, vd, od, lse_d, y16)


def _odd_bwd(dmix16, du, saved, w_in, qn_g, kvn_g, w_uq, w_ukv, w_out):
    x16, qc, kc, vc, tot, cq, ckv, cqn16, ckvn16, qd, kd, vd, od, lse_d, y16 = saved
    S = x16.shape[0]
    half = D_ROPE // 2
    dw_out = _matmul_tn(y16, dmix16, shards=1, name="odd_dwout")
    dy = _matmul(dmix16, w_out, mode="nt", out_dtype=F32, name="odd_dy")
    doc = _to_heads(dy[:, :C_W], C_HEADS)
    dod = _to_heads(dy[:, C_W:], D_HEADS)
    dqc, dkc, dvc = _stick_bwd(qc, kc, vc, tot, doc, name="stick_bwd", **C_KW)
    dqd, dkd, dvd = _causal_bwd(qd, kd, vd, od, dod, lse_d, name="mla_bwd", **D_KW)
    cos, sin = _rope_tables(D_HEADS + 1)
    nq = D_HEADS * half
    gq1 = jnp.transpose(dqd[..., D_NOPE:D_NOPE + half], (1, 0, 2)).reshape(1, S, nq)
    gq2 = jnp.transpose(dqd[..., D_NOPE + half:], (1, 0, 2)).reshape(1, S, nq)
    dq1, dq2 = _rope(gq1, gq2, cos[:, :nq], -sin[:, :nq], name="rope_bwd_q")
    dk1, dk2 = _rope(dkd[..., D_NOPE:D_NOPE + half], dkd[..., D_NOPE + half:], cos[:, nq:], -sin[:, nq:], name="rope_bwd_k")
    dq_full = jnp.concatenate([jnp.transpose(dqd[..., :D_NOPE], (1, 0, 2)), dq1.reshape(S, D_HEADS, half),
                               dq2.reshape(S, D_HEADS, half)], axis=-1).reshape(S, D_HEADS * (D_NOPE + D_ROPE)).astype(BF16)
    dkv_full = jnp.concatenate([jnp.transpose(dkd[..., :D_NOPE], (1, 0, 2)), jnp.transpose(dvd, (1, 0, 2))],
                               axis=-1).reshape(S, D_HEADS * (D_NOPE + D_V)).astype(BF16)
    dw_uq = _matmul_tn(cqn16, dq_full, shards=N_CHIPS, name="mla_dwuq")
    dw_ukv = _matmul_tn(ckvn16, dkv_full, shards=N_CHIPS, name="mla_dwukv")
    dcqn = _matmul(dq_full, w_uq, mode="nt", out_dtype=F32, name="mla_dcq")
    dckvn = _matmul(dkv_full, w_ukv, mode="nt", out_dtype=F32, name="mla_dckv")
    dcq, _, dqn_g, _ = _norm_bwd(dcqn, cq, qn_g, center=False, eps=RMS_EPS, name="q_rms_bwd")
    dckv, _, dkvn_g, _ = _norm_bwd(dckvn, ckv, kvn_g, center=False, eps=RMS_EPS, name="kv_rms_bwd")
    dh = jnp.concatenate([_from_heads(dqc), _from_heads(dkc), _from_heads(dvc), dcq, dckv, dk1, dk2], axis=-1)
    dh16 = _pad_cols(dh, ODD_IN // N_CHIPS, ODD_IN_PAD).astype(BF16)
    dw_in = _matmul_tn(x16, dh16, shards=N_CHIPS, name="odd_dwin")
    dx = _matmul(dh16, w_in, mode="nt", out_dtype=F32, epilogue="add", extra=du, alpha=ALPHA, name="odd_dx")
    return dx, dw_in, dqn_g[0], dkvn_g[0], dw_uq, dw_ukv, dw_out


WEIGHT_GROUPS = (("even_w_in",), ("even_w_out", "mlp_w1_0", "mlp_w2_0"), ("odd_w_in", "odd_w_uq", "odd_w_ukv", "odd_w_out"),
                 ("mlp_w1_1", "mlp_w2_1"))
GRAD_GROUPS = (("mlp_w2_1", "mlp_w1_1"), ("odd_w_out", "odd_w_uq", "odd_w_ukv", "odd_w_in"), ("mlp_w2_0", "mlp_w1_0"),
               ("even_w_out",), ("even_w_in",))


def _local_step(x, target, small, weights_for, send_grads, send_small, total_loss):
    def by_rows(t, rows):
        return t.reshape(N_CHIPS, rows // N_CHIPS, D_MODEL)

    x16 = x.astype(BF16)
    w = dict(weights_for(0, x16))
    y16, sv_e = _even_fwd(x16, w["even_w_in"], small["even_sinks"])
    w.update(weights_for(1, y16))
    mixed0 = _matmul(y16, w["even_w_out"], mode="nn", out_dtype=F32, name="even_out")
    x1, x1_16, u01 = _norm_fwd(x, mixed0, small["ln1_g"][0], small["ln1_b"][0], center=True, eps=LN_EPS, alpha=ALPHA, name="ln1_0")
    w1_0, w2_0 = w["mlp_w1_0"], w["mlp_w2_0"].reshape(1, D_FF, D_MODEL)
    x2, x2_16, sv_m0 = _mlp_fwd(x1, x1_16, w1_0, w2_0, small["ln2_g"][0], small["ln2_b"][0], 0)
    w.update(weights_for(2, x2_16))
    odd_w = (w["odd_w_in"], small["odd_q_norm_g"], small["odd_kv_norm_g"], w["odd_w_uq"], w["odd_w_ukv"],
             w["odd_w_out"].reshape(1, D_MODEL, D_MODEL))
    mixed1, sv_o = _odd_fwd(x2_16, *odd_w)
    x3, x3_16, u11 = _norm_fwd(x2, mixed1, small["ln1_g"][1], small["ln1_b"][1], center=True, eps=LN_EPS, alpha=ALPHA, name="ln1_1")
    w.update(weights_for(3, x3_16))
    w1_1, w2_1 = w["mlp_w1_1"], w["mlp_w2_1"].reshape(1, D_FF, D_MODEL)
    x4, _, sv_m1 = _mlp_fwd(x3, x3_16, w1_1, w2_1, small["ln2_g"][1], small["ln2_b"][1], 1)
    dy, loss_row = _loss_head(x4, target, name="loss_head")
    loss = total_loss(loss_row)

    dx3, dw1_1, dw2_1, dln2g_1, dln2b_1 = _mlp_bwd(dy, sv_m1, w1_1, w2_1, small["ln2_g"][1], 1, after=loss.reshape(1, 1))
    sent = send_grads(0, dict(mlp_w2_1=by_rows(dw2_1, D_FF), mlp_w1_1=dw1_1))
    du, du16, dln1g_1, dln1b_1 = _norm_bwd(dx3, u11, small["ln1_g"][1], center=True, eps=LN_EPS, name="ln1_1_bwd", after=sent)
    dx2, dwin_o, dqn_g, dkvn_g, dwuq, dwukv, dwout_o = _odd_bwd(du16, du, sv_o, *odd_w)
    sent = send_grads(1, dict(odd_w_out=by_rows(dwout_o, D_MODEL), odd_w_uq=dwuq, odd_w_ukv=dwukv, odd_w_in=dwin_o))
    dx1, dw1_0, dw2_0, dln2g_0, dln2b_0 = _mlp_bwd(dx2, sv_m0, w1_0, w2_0, small["ln2_g"][0], 0, after=sent)
    sent = send_grads(2, dict(mlp_w2_0=by_rows(dw2_0, D_FF), mlp_w1_0=dw1_0))
    du, du16, dln1g_0, dln1b_0 = _norm_bwd(dx1, u01, small["ln1_g"][0], center=True, eps=LN_EPS, name="ln1_0_bwd", after=sent)
    def send_last(dwin_e, dsinks):
        went = send_small(dict(even_sinks=dsinks, odd_q_norm_g=dqn_g, odd_kv_norm_g=dkvn_g,
                               ln1_g=jnp.concatenate([dln1g_0, dln1g_1]), ln1_b=jnp.concatenate([dln1b_0, dln1b_1]),
                               ln2_g=jnp.concatenate([dln2g_0, dln2g_1]), ln2_b=jnp.concatenate([dln2b_0, dln2b_1])))
        return send_grads(4, dict(even_w_in=dwin_e), went)

    dx0 = _even_bwd(du16, du, sv_e, w["even_w_in"], small["even_sinks"], w["even_w_out"],
                    lambda dwout_e: send_grads(3, dict(even_w_out=dwout_e)), send_last)
    return loss, dx0


SMALL_ORDER = ("ln1_g", "ln1_b", "ln2_g", "ln2_b", "even_sinks", "odd_q_norm_g", "odd_kv_norm_g")
SMALL_COLS = 2176


def _pack_small(parts):
    flat = jnp.concatenate([p.reshape(-1) for p in parts])
    return jnp.pad(flat, (0, 8 * SMALL_COLS - flat.shape[0])).reshape(8, SMALL_COLS)


def _unpack_small(packed, shapes):
    flat = packed.reshape(-1)
    out, pos = [], 0
    for s in shapes:
        n = math.prod(s)
        out.append(flat[pos:pos + n].reshape(s))
        pos += n
    return out


def kernel(x, even_w_in, even_sinks, even_w_out, odd_w_in, odd_q_norm_g, odd_kv_norm_g, odd_w_uq, odd_w_ukv, odd_w_out, ln1_g, ln1_b, mlp_w1, mlp_w2, ln2_g, ln2_b, loss_target, m_even_w_in, m_even_sinks, m_even_w_out, m_odd_w_in, m_odd_q_norm_g, m_odd_kv_norm_g, m_odd_w_uq, m_odd_w_ukv, m_odd_w_out, m_ln1_g, m_ln1_b, m_mlp_w1, m_mlp_w2, m_ln2_g, m_ln2_b, v_even_w_in, v_even_sinks, v_even_w_out, v_odd_w_in, v_odd_q_norm_g, v_odd_kv_norm_g, v_odd_w_uq, v_odd_w_ukv, v_odd_w_out, v_ln1_g, v_ln1_b, v_mlp_w1, v_mlp_w2, v_ln2_g, v_ln2_b):
    chip = 2 * lax.axis_index("x") + lax.axis_index("y")
    e_w, o_w = EVEN_IN // N_CHIPS, ODD_IN // N_CHIPS

    shards = dict(
        even_w_in=jnp.pad(even_w_in[0], ((0, 0), (0, EVEN_IN_PAD - e_w))), even_w_out=even_w_out[0],
        odd_w_in=jnp.pad(odd_w_in[0], ((0, 0), (0, ODD_IN_PAD - o_w))), odd_w_uq=odd_w_uq[0], odd_w_ukv=odd_w_ukv[0],
        odd_w_out=odd_w_out[0], mlp_w1_0=mlp_w1[0], mlp_w1_1=mlp_w1[1], mlp_w2_0=mlp_w2[0], mlp_w2_1=mlp_w2[1])
    names = list(shards)
    own16 = {n: shards[n].astype(BF16) for n in names}
    gather_started, token = [], own16[WEIGHT_GROUPS[0][0]]
    for g, group in enumerate(WEIGHT_GROUPS):
        lands = [jax.ShapeDtypeStruct((N_CHIPS,) + own16[n].shape, BF16) for n in group]
        gather_started.append(_copies_start([own16[n] for n in group], lands, token, SAME_CORE_OF_OTHER_CHIPS, _gather_src,
                                            _gather_dst, name=f"gather{g}_start"))
        token = gather_started[-1][-1]
    all_started = token

    def weights_for(g, after):
        group = WEIGHT_GROUPS[g]
        _, landed = _copies_wait(gather_started[g], all_started if g == 0 else after, SAME_CORE_OF_OTHER_CHIPS, _gather_src,
                                 _gather_landed, name=f"gather{g}_wait")
        full = _forward_halves(landed, name=f"gather{g}_forward")
        return {n: lax.dynamic_update_slice(f, own16[n][None], (chip, 0, 0)) for n, f in zip(group, full)}

    grads_started = {}

    def send_grads(g, shares, after=None):
        grads_started[g] = _reduce_scatter_start([shares[n] for n in GRAD_GROUPS[g]], after, name=f"grads{g}")
        return grads_started[g][-1]

    norm_rows = jnp.pad(jnp.concatenate([odd_q_norm_g[0], odd_kv_norm_g[0]])[None], ((0, 7), (0, 64)))
    norm_all = _all_gather8(norm_rows, name="gather_norm_gains")[0::16]
    qn_w, kvn_w = D_Q_RANK // N_CHIPS, D_KV_RANK // N_CHIPS
    small = dict(even_sinks=even_sinks[0], odd_q_norm_g=norm_all[:, :qn_w].reshape(D_Q_RANK),
                 odd_kv_norm_g=norm_all[:, qn_w:qn_w + kvn_w].reshape(D_KV_RANK), ln1_g=ln1_g, ln1_b=ln1_b, ln2_g=ln2_g, ln2_b=ln2_b)

    small_started = []

    def send_small(sm):
        pack = _pack_small([sm[n] for n in SMALL_ORDER])
        small_started.append(_copies_start([pack], [jax.ShapeDtypeStruct((8,) + pack.shape, F32)], pack, ALL_OTHER_DEVICES,
                                           _small_src, _small_dst, name="small_grads_start"))
        return small_started[0][-1]

    def small_sum(after):
        (pack,), (landed,) = _copies_wait(small_started[0], after, ALL_OTHER_DEVICES, _small_src, _small_landed,
                                          name="small_grads_wait")
        device = 4 * lax.axis_index("x") + 2 * lax.axis_index("y") + lax.axis_index("c")
        return _sum_slabs(lax.dynamic_update_slice(landed, pack[None], (device, 0, 0)), name="sum_small_grads")

    loss, grad_x = _local_step(x[0], loss_target[0], small, weights_for, send_grads, send_small,
                               lambda row: lax.psum(row[0, 0], ("x", "y", "c")))

    weights = dict(even_w_in=even_w_in, even_sinks=even_sinks, even_w_out=even_w_out, odd_w_in=odd_w_in, odd_q_norm_g=odd_q_norm_g,
                   odd_kv_norm_g=odd_kv_norm_g, odd_w_uq=odd_w_uq, odd_w_ukv=odd_w_ukv, odd_w_out=odd_w_out, ln1_g=ln1_g, ln1_b=ln1_b,
                   mlp_w1=mlp_w1, mlp_w2=mlp_w2, ln2_g=ln2_g, ln2_b=ln2_b)
    m_in = dict(even_w_in=m_even_w_in, even_sinks=m_even_sinks, even_w_out=m_even_w_out, odd_w_in=m_odd_w_in,
                odd_q_norm_g=m_odd_q_norm_g, odd_kv_norm_g=m_odd_kv_norm_g, odd_w_uq=m_odd_w_uq, odd_w_ukv=m_odd_w_ukv,
                odd_w_out=m_odd_w_out, ln1_g=m_ln1_g, ln1_b=m_ln1_b, mlp_w1=m_mlp_w1, mlp_w2=m_mlp_w2, ln2_g=m_ln2_g, ln2_b=m_ln2_b)
    v_in = dict(even_w_in=v_even_w_in, even_sinks=v_even_sinks, even_w_out=v_even_w_out, odd_w_in=v_odd_w_in,
                odd_q_norm_g=v_odd_q_norm_g, odd_kv_norm_g=v_odd_kv_norm_g, odd_w_uq=v_odd_w_uq, odd_w_ukv=v_odd_w_ukv,
                odd_w_out=v_odd_w_out, ln1_g=v_ln1_g, ln1_b=v_ln1_b, mlp_w1=v_mlp_w1, mlp_w2=v_mlp_w2, ln2_g=v_ln2_g, ln2_b=v_ln2_b)
    order = list(weights)
    updated = {}

    def update(n, g2d, layer=0, tag=""):
        shape = weights[n].shape
        as3d = (1, math.prod(shape[:-1]), shape[-1]) if len(shape) == 2 else shape
        res = _adamw(weights[n].reshape(as3d), g2d, m_in[n].reshape(as3d), v_in[n].reshape(as3d), layer=layer,
                     carry=updated.get(n), name=f"adamw_{n}{tag}")
        updated[n] = tuple(res)
        return res[0]

    after = grad_x
    for g, group in enumerate(GRAD_GROUPS):
        for n, r in zip(group, _reduce_scatter_finish(grads_started[g], after, name=f"grads{g}")):
            if n[:-2] in ("mlp_w1", "mlp_w2"):
                after = update(n[:-2], r, layer=int(n[-1]), tag=n[-2:])
            elif n == "even_w_in":
                after = update(n, r[:, :e_w])
            elif n == "odd_w_in":
                after = update(n, r[:, :o_w])
            else:
                after = update(n, r)
        if g == len(GRAD_GROUPS) - 2:
            g_ln1g, g_ln1b, g_ln2g, g_ln2b, g_sinks, g_qn, g_kvn = _unpack_small(
                small_sum(after), [(DEPTH, D_MODEL)] * 4 + [(1, A_Q_HEADS), (D_Q_RANK,), (D_KV_RANK,)])
            for n, gs in (("even_sinks", g_sinks), ("odd_q_norm_g", lax.dynamic_slice_in_dim(g_qn, chip * qn_w, qn_w)[None]),
                          ("odd_kv_norm_g", lax.dynamic_slice_in_dim(g_kvn, chip * kvn_w, kvn_w)[None]), ("ln1_g", g_ln1g),
                          ("ln1_b", g_ln1b), ("ln2_g", g_ln2g), ("ln2_b", g_ln2b)):
                update(n, gs)
    outs = [[updated[n][k].reshape(weights[n].shape) for n in order] for k in range(4)]
    return (loss, grad_x[None], *outs[0], *outs[1], *outs[2], *outs[3])
```

```python
import functools
import math

import jax
import jax.numpy as jnp
from jax import lax
from jax.experimental import pallas as pl
from jax.experimental.pallas import tpu as pltpu

F32 = jnp.float32
BF16 = jnp.bfloat16
MESH = pl.DeviceIdType.MESH

D_MODEL = 2048
SEQ = 2048
DEPTH = 2
HEAD_DIM = 64
A_Q_HEADS, A_KV_HEADS, A_WINDOW = 16, 2, 128
B_HEADS = 8
B_PATTERNS = ((128, 1), (512, 4), (2048, 16))
C_HEADS = 16
D_HEADS, D_Q_RANK, D_KV_RANK, D_NOPE, D_ROPE, D_V = 16, 512, 256, 64, 32, 64
ROPE_BASE = 10000.0
D_FF = 4 * D_MODEL
LN_EPS = 1e-5
RMS_EPS = 1e-6
ALPHA = (2 * DEPTH) ** 0.25
A_Q_W = A_Q_HEADS * HEAD_DIM
A_KV_W = A_KV_HEADS * HEAD_DIM
B_W = B_HEADS * HEAD_DIM
EVEN_IN = A_Q_W + 2 * A_KV_W + 3 * B_W * len(B_PATTERNS)
EVEN_OUT = A_Q_W + B_W
C_W = C_HEADS * HEAD_DIM
ODD_IN = 3 * C_W + D_Q_RANK + D_KV_RANK + D_ROPE
ADAM_LR, ADAM_B1, ADAM_B2, ADAM_EPS, ADAM_WD, ADAM_STEP = 0.001, 0.9, 0.999, 1e-08, 0.01, 10

N_CHIPS = 4
EVEN_IN_PAD = 1536
ODD_IN_PAD = 1024
ATT_BLK = 128
NEG = -1e30
V7X_VMEM_LIMIT = 48 * 1024 * 1024


def _params(*sem):
    return pltpu.CompilerParams(dimension_semantics=sem, vmem_limit_bytes=V7X_VMEM_LIMIT)


def _tile(n, cap):
    if n <= cap:
        return n
    t = cap - cap % 128
    while n % t:
        t -= 128
    return t


def _matmul(a, b, *, mode, out_dtype, name, epilogue=None, extra=None, alpha=1.0, after=None, tm=1024, tn=1024, tk=2048):
    if mode == "nn":
        M, K = a.shape
        P, _, Np = b.shape
        tm, tn, tk = _tile(M, tm), _tile(Np, tn), _tile(K, tk)
        nj = Np // tn
        grid = (M // tm, P * nj, K // tk)
        a_spec = pl.BlockSpec((tm, tk), lambda i, j, k: (i, k))
        b_spec = pl.BlockSpec((None, tk, tn), lambda i, j, k: (j // nj, k, j % nj))
        o_spec = pl.BlockSpec((tm, tn), lambda i, j, k: (i, j))
        out_shape = (M, P * Np)
        dims = (((1,), (0,)), ((), ()))
    elif mode == "nt":
        M, N = a.shape
        P, K, Np = b.shape
        tm, tn, tk = _tile(M, tm), _tile(K, tn), _tile(Np, tk)
        nkk = Np // tk
        grid = (M // tm, K // tn, P * nkk)
        a_spec = pl.BlockSpec((tm, tk), lambda i, j, k: (i, k))
        b_spec = pl.BlockSpec((None, tn, tk), lambda i, j, k: (k // nkk, j, k % nkk))
        o_spec = pl.BlockSpec((tm, tn), lambda i, j, k: (i, j))
        out_shape = (M, K)
        dims = (((1,), (1,)), ((), ()))
    else:
        raise ValueError(mode)
    n_k = grid[2]
    n_extra = 0 if extra is None else 1
    n_after = 0 if after is None else 1
    n_out = 2 if epilogue == "relu2" else 1

    def body(*refs):
        a_ref, b_ref = refs[:2]
        e_ref = refs[2] if n_extra else None
        outs = refs[2 + n_extra + n_after:2 + n_extra + n_after + n_out]
        part = lax.dot_general(a_ref[...], b_ref[...], dims, preferred_element_type=F32)

        def finish(r):
            if epilogue == "relu2":
                outs[0][...] = r.astype(outs[0].dtype)
                rp = jnp.maximum(r, 0.0)
                outs[1][...] = (rp * rp).astype(outs[1].dtype)
            elif epilogue == "drelu2":
                outs[0][...] = (r * (2.0 * jnp.maximum(e_ref[...].astype(F32), 0.0))).astype(outs[0].dtype)
            elif epilogue == "add":
                outs[0][...] = (r + alpha * e_ref[...].astype(F32)).astype(outs[0].dtype)
            else:
                outs[0][...] = r.astype(outs[0].dtype)

        if n_k == 1:
            finish(part)
        else:
            acc = refs[-1]
            k = pl.program_id(2)

            @pl.when(k == 0)
            def _():
                acc[...] = part

            @pl.when((k > 0) & (k < n_k - 1))
            def _():
                acc[...] += part

            @pl.when(k == n_k - 1)
            def _():
                finish(acc[...] + part)

    in_specs = [a_spec, b_spec] + ([o_spec] if n_extra else []) + ([pl.BlockSpec(memory_space=pl.ANY)] if n_after else [])
    shapes = [jax.ShapeDtypeStruct(out_shape, out_dtype)] * n_out
    res = pl.pallas_call(
        body, name=name, grid=grid, in_specs=in_specs,
        out_specs=[o_spec] * n_out, out_shape=shapes,
        scratch_shapes=[pltpu.VMEM((tm, tn), F32)] if n_k > 1 else [],
        compiler_params=_params("parallel", "parallel", "arbitrary"),
    )(a, b, *([extra] if n_extra else []), *([after] if n_after else []))
    return res if n_out > 1 else res[0]


def _matmul_tn(a, g, *, shards, name, tm=1024, tn=1024):
    M, K = a.shape
    P = shards
    Np = g.shape[1] // P
    tm, tn = _tile(K, tm), _tile(Np, tn)
    nj = Np // tn

    def body(a_ref, g_ref, o_ref):
        o_ref[...] = lax.dot_general(a_ref[...], g_ref[...], (((0,), (0,)), ((), ())), preferred_element_type=F32).astype(BF16)

    return pl.pallas_call(
        body, name=name, grid=(K // tm, P * nj),
        in_specs=[pl.BlockSpec((M, tm), lambda i, j: (0, i)), pl.BlockSpec((M, tn), lambda i, j: (0, j))],
        out_specs=pl.BlockSpec((None, tm, tn), lambda i, j: (j // nj, i, j % nj)),
        out_shape=jax.ShapeDtypeStruct((P, K, Np), BF16),
        compiler_params=_params("parallel", "parallel"),
    )(a, g)


def _norm_fwd(x, res, g, b, *, center, eps, alpha, name, rows=256):
    S, W = x.shape
    has_res = res is not None
    rows = _tile(S, rows)

    def body(*refs):
        x_ref = refs[0]
        r_ref = refs[1] if has_res else None
        g_ref = refs[1 + has_res]
        b_ref = refs[2 + has_res] if center else None
        y_ref, y16_ref, u_ref = refs[-3:]
        u = x_ref[...]
        if has_res:
            u = alpha * u + r_ref[...]
        if center:
            mu = jnp.mean(u, axis=1, keepdims=True)
            xc = u - mu
        else:
            xc = u
        var = jnp.mean(xc * xc, axis=1, keepdims=True)
        y = xc * lax.rsqrt(var + eps) * g_ref[...]
        if center:
            y = y + b_ref[...]
        y_ref[...] = y
        y16_ref[...] = y.astype(BF16)
        u_ref[...] = u

    row_spec = pl.BlockSpec((rows, W), lambda i: (i, 0))
    vec_spec = pl.BlockSpec((1, W), lambda i: (0, 0))
    ins = [x] + ([res] if has_res else []) + [g.reshape(1, W)] + ([b.reshape(1, W)] if center else [])
    specs = [row_spec] + ([row_spec] if has_res else []) + [vec_spec] + ([vec_spec] if center else [])
    return pl.pallas_call(
        body, name=name, grid=(S // rows,), in_specs=specs,
        out_specs=[row_spec, row_spec, row_spec],
        out_shape=[jax.ShapeDtypeStruct((S, W), F32), jax.ShapeDtypeStruct((S, W), BF16), jax.ShapeDtypeStruct((S, W), F32)],
        compiler_params=_params("parallel"),
    )(*ins)


def _norm_bwd(dy, u, g, *, center, eps, name, rows=256, after=None):
    S, W = u.shape
    rows = _tile(S, rows)

    def body(dy_ref, u_ref, g_ref, *rest):
        du_ref, du16_ref, dg_ref, db_ref = rest[-4:]
        i = pl.program_id(0)
        uu = u_ref[...]
        dyv = dy_ref[...]
        if center:
            xc = uu - jnp.mean(uu, axis=1, keepdims=True)
        else:
            xc = uu
        rstd = lax.rsqrt(jnp.mean(xc * xc, axis=1, keepdims=True) + eps)
        xhat = xc * rstd
        dxh = dyv * g_ref[...]
        c2 = jnp.mean(dxh * xhat, axis=1, keepdims=True)
        du = dxh - xhat * c2
        if center:
            du = du - jnp.mean(dxh, axis=1, keepdims=True)
        du = du * rstd
        du_ref[...] = du
        du16_ref[...] = du.astype(BF16)

        @pl.when(i == 0)
        def _():
            dg_ref[...] = jnp.zeros_like(dg_ref)
            db_ref[...] = jnp.zeros_like(db_ref)

        dg_ref[...] += jnp.sum(dyv * xhat, axis=0, keepdims=True)
        db_ref[...] += jnp.sum(dyv, axis=0, keepdims=True)

    row_spec = pl.BlockSpec((rows, W), lambda i: (i, 0))
    vec_spec = pl.BlockSpec((1, W), lambda i: (0, 0))
    return pl.pallas_call(
        body, name=name, grid=(S // rows,),
        in_specs=[row_spec, row_spec, vec_spec] + ([] if after is None else [pl.BlockSpec(memory_space=pl.ANY)]),
        out_specs=[row_spec, row_spec, vec_spec, vec_spec],
        out_shape=[jax.ShapeDtypeStruct((S, W), F32), jax.ShapeDtypeStruct((S, W), BF16),
                   jax.ShapeDtypeStruct((1, W), F32), jax.ShapeDtypeStruct((1, W), F32)],
        compiler_params=_params("arbitrary"),
    )(dy, u, g.reshape(1, W), *([] if after is None else [after]))


def _loss_head(y, target, *, name, rows=256):
    S, W = y.shape
    rows = _tile(S, rows)

    def body(y_ref, t_ref, dy_ref, l_ref):
        i = pl.program_id(0)
        e = y_ref[...] - t_ref[...]
        dy_ref[...] = e * (1.0 / W)

        @pl.when(i == 0)
        def _():
            l_ref[...] = jnp.zeros_like(l_ref)

        l_ref[...] += jnp.full(l_ref.shape, 0.5 / W * jnp.sum(e * e), F32)

    row_spec = pl.BlockSpec((rows, W), lambda i: (i, 0))
    return pl.pallas_call(
        body, name=name, grid=(S // rows,), in_specs=[row_spec, row_spec],
        out_specs=[row_spec, pl.BlockSpec((1, 128), lambda i: (0, 0))],
        out_shape=[jax.ShapeDtypeStruct((S, W), F32), jax.ShapeDtypeStruct((1, 128), F32)],
        compiler_params=_params("arbitrary"),
    )(y, target)


def _adamw(w, g, m, v, *, name, layer=0, carry=None, rows=256):
    L, R, C = w.shape
    rows = max(t for t in range(8, rows + 1, 8) if R % t == 0) if R % 8 == 0 else R
    c1 = 1.0 / (1.0 - ADAM_B1 ** ADAM_STEP)
    c2 = 1.0 / (1.0 - ADAM_B2 ** ADAM_STEP)

    def body(w_ref, g_ref, m_ref, v_ref, *rest):
        go_ref, d_ref, mo_ref, vo_ref = rest[-4:]
        gg = g_ref[...]
        mn = ADAM_B1 * m_ref[...] + (1.0 - ADAM_B1) * gg
        vn = ADAM_B2 * v_ref[...] + (1.0 - ADAM_B2) * (gg * gg)
        go_ref[...] = gg
        d_ref[...] = -ADAM_LR * ((mn * c1) / (jnp.sqrt(vn * c2) + ADAM_EPS) + ADAM_WD * w_ref[...])
        mo_ref[...] = mn
        vo_ref[...] = vn

    slab = pl.BlockSpec((None, rows, C), lambda i: (layer, i, 0))
    shp = jax.ShapeDtypeStruct((L, R, C), F32)
    carried = [] if carry is None else list(carry)
    return pl.pallas_call(
        body, name=name, grid=(R // rows,),
        in_specs=[slab, pl.BlockSpec((rows, C), lambda i: (i, 0)), slab, slab] + [pl.BlockSpec(memory_space=pl.ANY)] * len(carried),
        out_specs=[slab] * 4, out_shape=[shp] * 4, input_output_aliases={4 + k: k for k in range(len(carried))},
        compiler_params=_params("parallel"),
    )(w, g, m, v, *carried)


def _rope(x1, x2, cos, sin, *, name, rows=512):
    H, S, W = x1.shape
    rows = _tile(S, rows)

    def body(x1_ref, x2_ref, c_ref, s_ref, y1_ref, y2_ref):
        t1 = jnp.sum(x1_ref[...], axis=0)
        t2 = jnp.sum(x2_ref[...], axis=0)
        c = c_ref[...]
        s = s_ref[...]
        y1_ref[...] = t1 * c - t2 * s
        y2_ref[...] = t1 * s + t2 * c

    xs = pl.BlockSpec((H, rows, W), lambda i: (0, i, 0))
    ts = pl.BlockSpec((rows, W), lambda i: (i, 0))
    shp = jax.ShapeDtypeStruct((S, W), F32)
    return pl.pallas_call(
        body, name=name, grid=(S // rows,), in_specs=[xs, xs, ts, ts], out_specs=[ts, ts], out_shape=[shp, shp],
        compiler_params=_params("parallel"),
    )(x1, x2, cos, sin)


def _band_mask(blk, n, n_back):
    row = lax.broadcasted_iota(jnp.int32, (blk, 2 * blk), 0)
    col = lax.broadcasted_iota(jnp.int32, (blk, 2 * blk), 1)
    rel = blk + row - col
    return rel, (rel >= 0) & (rel <= n_back) & ((col >= blk) | (n >= 1))


def _band_fwd(q, k, v, slopes, sinks, *, hps, nb_seq, n_back, scale, dist_scale, name):
    blk = ATT_BLK
    Hq, T, d = q.shape
    Hk = k.shape[0]
    group = Hq // Hk
    kps = max(hps // group, 1)
    use_sink = sinks is not None

    def body(*refs):
        q_ref, k_ref, v_ref, slope_ref = refs[:4]
        sink_ref = refs[4] if use_sink else None
        o_ref, lse_ref = refs[-2:]
        i = pl.program_id(1)
        start = pl.multiple_of(i * blk, blk)
        rel, valid = _band_mask(blk, i % nb_seq, n_back)
        relf = rel.astype(F32)
        for hh in range(hps):
            h = pl.program_id(0) * hps + hh
            kk = k_ref[hh // group, pl.ds(start, 2 * blk), :]
            vv = v_ref[hh // group, pl.ds(start, 2 * blk), :]
            s = lax.dot_general(q_ref[hh], kk, (((1,), (1,)), ((), ())), preferred_element_type=F32) * scale
            s = jnp.where(valid, s - (slope_ref[h] * dist_scale) * relf, NEG)
            m = jnp.max(s, axis=1, keepdims=True)
            if use_sink:
                m = jnp.maximum(m, sink_ref[h])
            p = jnp.where(valid, jnp.exp(s - m), 0.0)
            l = jnp.sum(p, axis=1, keepdims=True)
            if use_sink:
                l = l + jnp.exp(sink_ref[h] - m)
            o_ref[hh] = jnp.dot(p.astype(BF16), vv, preferred_element_type=F32) / l
            lse_ref[hh] = m + jnp.log(l)

    smem = pl.BlockSpec(memory_space=pltpu.SMEM)
    qs = pl.BlockSpec((hps, blk, d), lambda g, i: (g, i, 0))
    ks = pl.BlockSpec((kps, T + blk, d), lambda g, i: (g, 0, 0))
    ins = [q, k, v, slopes] + ([sinks] if use_sink else [])
    return pl.pallas_call(
        body, name=name, grid=(Hq // hps, T // blk), in_specs=[qs, ks, ks, smem] + ([smem] if use_sink else []),
        out_specs=[qs, pl.BlockSpec((hps, blk, 1), lambda g, i: (g, i, 0))],
        out_shape=[jax.ShapeDtypeStruct((Hq, T, d), F32), jax.ShapeDtypeStruct((Hq, T, 1), F32)],
        compiler_params=_params("parallel", "parallel"),
    )(*ins)


def _band_bwd(q, k, v, o, do, lse, dlse, slopes, sinks, *, hps, nb_seq, n_back, scale, dist_scale, name):
    blk = ATT_BLK
    Hq, T, d = q.shape
    Hk = k.shape[0]
    group = Hq // Hk
    kps = max(hps // group, 1)
    use_sink, use_dlse = sinks is not None, dlse is not None

    def body(*refs):
        q_ref, k_ref, v_ref, o_ref, do_ref, lse_ref = refs[:6]
        pos = 6
        dlse_ref = refs[pos] if use_dlse else None
        pos += use_dlse
        slope_ref = refs[pos]
        pos += 1
        sink_ref = refs[pos] if use_sink else None
        pos += use_sink
        dq_ref, dk_ref, dv_ref = refs[pos:pos + 3]
        dsink_ref = refs[pos + 3] if use_sink else None
        i = pl.program_id(1)
        start = pl.multiple_of(i * blk, blk)
        rel, valid = _band_mask(blk, i % nb_seq, n_back)
        relf = rel.astype(F32)

        @pl.when(i == 0)
        def _():
            dk_ref[...] = jnp.zeros_like(dk_ref)
            dv_ref[...] = jnp.zeros_like(dv_ref)
            if use_sink:
                dsink_ref[...] = jnp.zeros_like(dsink_ref)

        for kh in range(kps):
            dk_acc = jnp.zeros((2 * blk, d), F32)
            dv_acc = jnp.zeros((2 * blk, d), F32)
            kk = k_ref[kh, pl.ds(start, 2 * blk), :]
            vv = v_ref[kh, pl.ds(start, 2 * blk), :]
            for hh in range(kh * min(group, hps), (kh + 1) * min(group, hps)):
                h = pl.program_id(0) * hps + hh
                qb = q_ref[hh]
                dob = do_ref[hh]
                do16 = dob.astype(BF16)
                lse = lse_ref[hh]
                c = -jnp.sum(dob * o_ref[hh], axis=1, keepdims=True)
                if use_dlse:
                    c = c + dlse_ref[hh]
                s = lax.dot_general(qb, kk, (((1,), (1,)), ((), ())), preferred_element_type=F32) * scale
                s = jnp.where(valid, s - (slope_ref[h] * dist_scale) * relf, NEG)
                p = jnp.where(valid, jnp.exp(s - lse), 0.0)
                dp = lax.dot_general(do16, vv, (((1,), (1,)), ((), ())), preferred_element_type=F32)
                ds16 = (p * (dp + c) * scale).astype(BF16)
                dq_ref[hh] = jnp.dot(ds16, kk, preferred_element_type=F32)
                dk_acc = dk_acc + lax.dot_general(ds16, qb, (((0,), (0,)), ((), ())), preferred_element_type=F32)
                dv_acc = dv_acc + lax.dot_general(p.astype(BF16), do16, (((0,), (0,)), ((), ())), preferred_element_type=F32)
                if use_sink:
                    dsink_ref[hh] += jnp.full((1, 128), jnp.sum(jnp.exp(sink_ref[h] - lse) * c), F32)
            dk_ref[kh, pl.ds(start, 2 * blk), :] += dk_acc
            dv_ref[kh, pl.ds(start, 2 * blk), :] += dv_acc

    smem = pl.BlockSpec(memory_space=pltpu.SMEM)
    qs = pl.BlockSpec((hps, blk, d), lambda g, i: (g, i, 0))
    ls = pl.BlockSpec((hps, blk, 1), lambda g, i: (g, i, 0))
    ks = pl.BlockSpec((kps, T + blk, d), lambda g, i: (g, 0, 0))
    in_specs = [qs, ks, ks, qs, qs, ls] + ([ls] if use_dlse else []) + [smem] + ([smem] if use_sink else [])
    ins = [q, k, v, o, do, lse] + ([dlse] if use_dlse else []) + [slopes] + ([sinks] if use_sink else [])
    out_specs = [qs, ks, ks]
    out_shape = [jax.ShapeDtypeStruct((Hq, T, d), F32), jax.ShapeDtypeStruct((Hk, T + blk, d), F32),
                 jax.ShapeDtypeStruct((Hk, T + blk, d), F32)]
    if use_sink:
        out_specs.append(pl.BlockSpec((hps, 1, 128), lambda g, i: (g, 0, 0)))
        out_shape.append(jax.ShapeDtypeStruct((Hq, 1, 128), F32))
    return pl.pallas_call(
        body, name=name, grid=(Hq // hps, T // blk), in_specs=in_specs, out_specs=out_specs, out_shape=out_shape,
        compiler_params=_params("parallel", "arbitrary"),
    )(*ins)


def _diagonal_mask(blk):
    return lax.broadcasted_iota(jnp.int32, (blk, blk), 0) >= lax.broadcasted_iota(jnp.int32, (blk, blk), 1)


def _causal_fwd(q, k, v, *, blk, hps, scale, name):
    H, T, dqk = q.shape
    dv = v.shape[2]

    def body(q_ref, k_ref, v_ref, o_ref, lse_ref):
        n = pl.program_id(1)
        qs = [q_ref[hh] for hh in range(hps)]

        def block(j, carry, valid):
            start = pl.multiple_of(j * blk, blk)
            out = []
            for hh in range(hps):
                m, l, acc = carry[hh]
                kb = k_ref[hh, pl.ds(start, blk), :]
                vb = v_ref[hh, pl.ds(start, blk), :]
                s = lax.dot_general(qs[hh], kb, (((1,), (1,)), ((), ())), preferred_element_type=F32) * scale
                if valid is not None:
                    s = jnp.where(valid, s, NEG)
                m_new = jnp.maximum(m, jnp.max(s, axis=1, keepdims=True))
                p = _keep(valid, jnp.exp(s - m_new))
                a = jnp.exp(m - m_new)
                out.append((m_new, a * l + jnp.sum(p, axis=1, keepdims=True),
                            a * acc + jnp.dot(p.astype(BF16), vb, preferred_element_type=F32)))
            return tuple(out)

        init = tuple((jnp.full((blk, 1), NEG, F32), jnp.zeros((blk, 1), F32), jnp.zeros((blk, dv), F32)) for _ in range(hps))
        res = block(n, lax.fori_loop(0, n, lambda j, carry: block(j, carry, None), init), _diagonal_mask(blk))
        for hh in range(hps):
            m, l, acc = res[hh]
            o_ref[hh] = acc / l
            lse_ref[hh] = m + jnp.log(l)

    qs_ = pl.BlockSpec((hps, blk, dqk), lambda g, i: (g, i, 0))
    return pl.pallas_call(
        body, name=name, grid=(H // hps, T // blk),
        in_specs=[qs_, pl.BlockSpec((hps, T, dqk), lambda g, i: (g, 0, 0)), pl.BlockSpec((hps, T, dv), lambda g, i: (g, 0, 0))],
        out_specs=[pl.BlockSpec((hps, blk, dv), lambda g, i: (g, i, 0)), pl.BlockSpec((hps, blk, 1), lambda g, i: (g, i, 0))],
        out_shape=[jax.ShapeDtypeStruct((H, T, dv), F32), jax.ShapeDtypeStruct((H, T, 1), F32)],
        compiler_params=_params("parallel", "parallel"),
    )(q, k, v)


def _causal_bwd(q, k, v, o, do, lse, *, blk, hps, scale, name):
    H, T, dqk = q.shape
    dv = v.shape[2]

    def body(q_ref, k_ref, v_ref, o_ref, do_ref, lse_ref, dq_ref, dk_ref, dv_ref):
        n = pl.program_id(1)

        @pl.when(n == 0)
        def _():
            dk_ref[...] = jnp.zeros_like(dk_ref)
            dv_ref[...] = jnp.zeros_like(dv_ref)

        qs = [q_ref[hh] for hh in range(hps)]
        do16 = [do_ref[hh].astype(BF16) for hh in range(hps)]
        lses = [lse_ref[hh] for hh in range(hps)]
        cs = [-jnp.sum(do_ref[hh] * o_ref[hh], axis=1, keepdims=True) for hh in range(hps)]

        def block(j, dqs, valid):
            start = pl.multiple_of(j * blk, blk)
            out = []
            for hh in range(hps):
                kb = k_ref[hh, pl.ds(start, blk), :]
                vb = v_ref[hh, pl.ds(start, blk), :]
                s = lax.dot_general(qs[hh], kb, (((1,), (1,)), ((), ())), preferred_element_type=F32) * scale
                if valid is not None:
                    s = jnp.where(valid, s, NEG)
                p = _keep(valid, jnp.exp(s - lses[hh]))
                dp = lax.dot_general(do16[hh], vb, (((1,), (1,)), ((), ())), preferred_element_type=F32)
                ds16 = (p * (dp + cs[hh]) * scale).astype(BF16)
                dk_ref[hh, pl.ds(start, blk), :] += lax.dot_general(ds16, qs[hh], (((0,), (0,)), ((), ())), preferred_element_type=F32)
                dv_ref[hh, pl.ds(start, blk), :] += lax.dot_general(p.astype(BF16), do16[hh], (((0,), (0,)), ((), ())),
                                                                    preferred_element_type=F32)
                out.append(dqs[hh] + jnp.dot(ds16, kb, preferred_element_type=F32))
            return tuple(out)

        init = tuple(jnp.zeros((blk, dqk), F32) for _ in range(hps))
        res = block(n, lax.fori_loop(0, n, lambda j, dqs: block(j, dqs, None), init), _diagonal_mask(blk))
        for hh in range(hps):
            dq_ref[hh] = res[hh]

    qs_ = pl.BlockSpec((hps, blk, dqk), lambda g, i: (g, i, 0))
    os_ = pl.BlockSpec((hps, blk, dv), lambda g, i: (g, i, 0))
    ls_ = pl.BlockSpec((hps, blk, 1), lambda g, i: (g, i, 0))
    ks_ = pl.BlockSpec((hps, T, dqk), lambda g, i: (g, 0, 0))
    vs_ = pl.BlockSpec((hps, T, dv), lambda g, i: (g, 0, 0))
    return pl.pallas_call(
        body, name=name, grid=(H // hps, T // blk), in_specs=[qs_, ks_, vs_, os_, os_, ls_], out_specs=[qs_, ks_, vs_],
        out_shape=[jax.ShapeDtypeStruct((H, T, dqk), F32), jax.ShapeDtypeStruct((H, T, dqk), F32),
                   jax.ShapeDtypeStruct((H, T, dv), F32)],
        compiler_params=_params("parallel", "arbitrary"),
    )(q, k, v, o, do, lse)


def _band_geometry(blk, nb_seq, n_prev):
    def geo(i):
        seq = i // nb_seq
        n = i % nb_seq
        return seq, n, jnp.maximum(n - n_prev, 0)
    return geo


def _attn_fwd(q, k, v, slopes, sinks, *, blk, nb_seq, n_prev, n_back, scale, dist_scale, name):
    Hq, T, dqk = q.shape
    Hk, _, dv = v.shape
    group = Hq // Hk
    use_bias, use_sink = slopes is not None, sinks is not None
    geo = _band_geometry(blk, nb_seq, n_prev)

    def body(*refs):
        q_ref, k_ref, v_ref = refs[:3]
        slope_ref = refs[3] if use_bias else None
        sink_ref = refs[3 + use_bias] if use_sink else None
        o_ref, lse_ref = refs[-2:]
        h = pl.program_id(0)
        seq, n, lo = geo(pl.program_id(1))
        qb = q_ref[...]
        diff = lax.broadcasted_iota(jnp.int32, (blk, blk), 0) - lax.broadcasted_iota(jnp.int32, (blk, blk), 1)
        if use_sink:
            m0 = jnp.full((blk, 1), sink_ref[h], F32)
            l0 = jnp.ones((blk, 1), F32)
        else:
            m0 = jnp.full((blk, 1), NEG, F32)
            l0 = jnp.zeros((blk, 1), F32)

        def step(j, carry):
            m, l, acc = carry
            start = pl.multiple_of((seq * nb_seq + j) * blk, blk)
            kb = k_ref[pl.ds(start, blk), :]
            vb = v_ref[pl.ds(start, blk), :]
            s = lax.dot_general(qb, kb, (((1,), (1,)), ((), ())), preferred_element_type=F32) * scale
            rel = (n - j) * blk + diff
            valid = (rel >= 0) & (rel <= n_back)
            if use_bias:
                s = s - (slope_ref[h] * dist_scale) * rel.astype(F32)
            s = jnp.where(valid, s, NEG)
            m_new = jnp.maximum(m, jnp.max(s, axis=1, keepdims=True))
            p = jnp.where(valid, jnp.exp(s - m_new), 0.0)
            a = jnp.exp(m - m_new)
            l = a * l + jnp.sum(p, axis=1, keepdims=True)
            acc = a * acc + jnp.dot(p.astype(BF16), vb, preferred_element_type=F32)
            return m_new, l, acc

        m, l, acc = lax.fori_loop(lo, n + 1, step, (m0, l0, jnp.zeros((blk, dv), F32)))
        o_ref[...] = acc / l
        lse_ref[...] = m + jnp.log(l)

    smem = pl.BlockSpec(memory_space=pltpu.SMEM)
    in_specs = [pl.BlockSpec((None, blk, dqk), lambda h, i: (h, i, 0)),
                pl.BlockSpec((None, T, dqk), lambda h, i: (h // group, 0, 0)),
                pl.BlockSpec((None, T, dv), lambda h, i: (h // group, 0, 0))]
    ins = [q, k, v]
    if use_bias:
        in_specs.append(smem)
        ins.append(slopes)
    if use_sink:
        in_specs.append(smem)
        ins.append(sinks)
    return pl.pallas_call(
        body, name=name, grid=(Hq, T // blk), in_specs=in_specs,
        out_specs=[pl.BlockSpec((None, blk, dv), lambda h, i: (h, i, 0)), pl.BlockSpec((None, blk, 1), lambda h, i: (h, i, 0))],
        out_shape=[jax.ShapeDtypeStruct((Hq, T, dv), F32), jax.ShapeDtypeStruct((Hq, T, 1), F32)],
        compiler_params=_params("parallel", "parallel"),
    )(*ins)


def _attn_bwd(q, k, v, o, do, lse, dlse, slopes, sinks, *, blk, nb_seq, n_prev, n_back, scale, dist_scale, name):
    Hq, T, dqk = q.shape
    Hk, _, dv = v.shape
    group = Hq // Hk
    use_bias, use_sink, use_dlse = slopes is not None, sinks is not None, dlse is not None
    geo = _band_geometry(blk, nb_seq, n_prev)

    def body(*refs):
        q_ref, k_ref, v_ref, o_ref, do_ref, lse_ref = refs[:6]
        pos = 6
        dlse_ref = refs[pos] if use_dlse else None
        pos += use_dlse
        slope_ref = refs[pos] if use_bias else None
        pos += use_bias
        sink_ref = refs[pos] if use_sink else None
        pos += use_sink
        dq_ref, dk_ref, dv_ref = refs[pos:pos + 3]
        dsink_ref = refs[pos + 3] if use_sink else None
        h = pl.program_id(0)
        i = pl.program_id(1)
        seq, n, lo = geo(i)
        qb = q_ref[...]
        dob = do_ref[...]
        do16 = dob.astype(BF16)
        lse = lse_ref[...]
        c = -jnp.sum(dob * o_ref[...], axis=1, keepdims=True)
        if use_dlse:
            c = c + dlse_ref[...]
        diff = lax.broadcasted_iota(jnp.int32, (blk, blk), 0) - lax.broadcasted_iota(jnp.int32, (blk, blk), 1)

        @pl.when((h % group == 0) & (i == 0))
        def _():
            dk_ref[...] = jnp.zeros_like(dk_ref)
            dv_ref[...] = jnp.zeros_like(dv_ref)

        def step(j, dq):
            start = pl.multiple_of((seq * nb_seq + j) * blk, blk)
            kb = k_ref[pl.ds(start, blk), :]
            vb = v_ref[pl.ds(start, blk), :]
            s = lax.dot_general(qb, kb, (((1,), (1,)), ((), ())), preferred_element_type=F32) * scale
            rel = (n - j) * blk + diff
            valid = (rel >= 0) & (rel <= n_back)
            if use_bias:
                s = s - (slope_ref[h] * dist_scale) * rel.astype(F32)
            p = jnp.where(valid, jnp.exp(jnp.where(valid, s, NEG) - lse), 0.0)
            dp = lax.dot_general(do16, vb, (((1,), (1,)), ((), ())), preferred_element_type=F32)
            ds16 = (p * (dp + c) * scale).astype(BF16)
            dk_ref[pl.ds(start, blk), :] += lax.dot_general(ds16, qb, (((0,), (0,)), ((), ())), preferred_element_type=F32)
            dv_ref[pl.ds(start, blk), :] += lax.dot_general(p.astype(BF16), do16, (((0,), (0,)), ((), ())), preferred_element_type=F32)
            return dq + jnp.dot(ds16, kb, preferred_element_type=F32)

        dq_ref[...] = lax.fori_loop(lo, n + 1, step, jnp.zeros((blk, dqk), F32))
        if use_sink:
            @pl.when(i == 0)
            def _():
                dsink_ref[...] = jnp.zeros_like(dsink_ref)

            dsink_ref[...] += jnp.full(dsink_ref.shape, jnp.sum(jnp.exp(sink_ref[h] - lse) * c), F32)

    smem = pl.BlockSpec(memory_space=pltpu.SMEM)
    qs = pl.BlockSpec((None, blk, dqk), lambda h, i: (h, i, 0))
    os_ = pl.BlockSpec((None, blk, dv), lambda h, i: (h, i, 0))
    ls = pl.BlockSpec((None, blk, 1), lambda h, i: (h, i, 0))
    ks = pl.BlockSpec((None, T, dqk), lambda h, i: (h // group, 0, 0))
    vs = pl.BlockSpec((None, T, dv), lambda h, i: (h // group, 0, 0))
    in_specs = [qs, ks, vs, os_, os_, ls]
    ins = [q, k, v, o, do, lse]
    if use_dlse:
        in_specs.append(ls)
        ins.append(dlse)
    if use_bias:
        in_specs.append(smem)
        ins.append(slopes)
    if use_sink:
        in_specs.append(smem)
        ins.append(sinks)
    out_specs = [qs, ks, vs]
    out_shape = [jax.ShapeDtypeStruct((Hq, T, dqk), F32), jax.ShapeDtypeStruct((Hk, T, dqk), F32),
                 jax.ShapeDtypeStruct((Hk, T, dv), F32)]
    if use_sink:
        out_specs.append(pl.BlockSpec((None, 1, 128), lambda h, i: (h, 0, 0)))
        out_shape.append(jax.ShapeDtypeStruct((Hq, 1, 128), F32))
    return pl.pallas_call(
        body, name=name, grid=(Hq, T // blk), in_specs=in_specs, out_specs=out_specs, out_shape=out_shape,
        compiler_params=_params("arbitrary", "arbitrary"),
    )(*ins)


def _tri_sum(x, upper):
    hi = x.astype(BF16)
    lo = (x - hi.astype(F32)).astype(BF16)
    return jnp.dot(hi, upper, preferred_element_type=F32) + jnp.dot(lo, upper, preferred_element_type=F32)


def _keep(mask, x):
    return x if mask is None else jnp.where(mask, x, 0.0)


def _stick_terms(qb, kb, scale, strict):
    z = lax.dot_general(qb, kb, (((1,), (1,)), ((), ())), preferred_element_type=F32) * scale
    e = jnp.exp(-jnp.abs(z))
    lb = jnp.minimum(z, 0.0) - jnp.log(1.0 + e)
    return z, e, lb, _keep(strict, lb - z)


def _stick_fwd(q, k, v, *, blk, hps, scale, name):
    H, T, dh = q.shape

    def body(q_ref, k_ref, v_ref, o_ref, tot_ref):
        n = pl.program_id(1)
        qs = [q_ref[hh] for hh in range(hps)]
        r_i = lax.broadcasted_iota(jnp.int32, (blk, blk), 0)
        c_i = lax.broadcasted_iota(jnp.int32, (blk, blk), 1)
        upper = (r_i > c_i).astype(BF16)

        def block(j, carry, strict):
            start = pl.multiple_of(j * blk, blk)
            out = []
            for hh in range(hps):
                run, acc = carry[hh]
                kb = k_ref[hh, pl.ds(start, blk), :]
                vb = v_ref[hh, pl.ds(start, blk), :]
                _, _, lb, lk = _stick_terms(qs[hh], kb, scale, strict)
                w = _keep(strict, jnp.exp(lb + run + _tri_sum(lk, upper)))
                out.append((run + jnp.sum(lk, axis=1, keepdims=True), acc + jnp.dot(w.astype(BF16), vb, preferred_element_type=F32)))
            return tuple(out)

        init = tuple((jnp.zeros((blk, 1), F32), jnp.zeros((blk, dh), F32)) for _ in range(hps))
        res = lax.fori_loop(1, n + 1, lambda t, carry: block(n - t, carry, None), block(n, init, r_i > c_i))
        for hh in range(hps):
            tot_ref[hh], o_ref[hh] = res[hh]

    qs_ = pl.BlockSpec((hps, blk, dh), lambda g, i: (g, i, 0))
    ks_ = pl.BlockSpec((hps, T, dh), lambda g, i: (g, 0, 0))
    ts_ = pl.BlockSpec((hps, blk, 1), lambda g, i: (g, i, 0))
    return pl.pallas_call(
        body, name=name, grid=(H // hps, T // blk), in_specs=[qs_, ks_, ks_], out_specs=[qs_, ts_],
        out_shape=[jax.ShapeDtypeStruct((H, T, dh), F32), jax.ShapeDtypeStruct((H, T, 1), F32)],
        compiler_params=_params("parallel", "parallel"),
    )(q, k, v)


def _stick_bwd(q, k, v, tot, do, *, blk, hps, scale, name):
    H, T, dh = q.shape

    def body(q_ref, k_ref, v_ref, tot_ref, do_ref, dq_ref, dk_ref, dv_ref):
        n = pl.program_id(1)
        qs = [q_ref[hh] for hh in range(hps)]
        do16 = [do_ref[hh].astype(BF16) for hh in range(hps)]
        tots = [tot_ref[hh] for hh in range(hps)]
        r_i = lax.broadcasted_iota(jnp.int32, (blk, blk), 0)
        c_i = lax.broadcasted_iota(jnp.int32, (blk, blk), 1)
        upto = (r_i <= c_i).astype(BF16)
        below = (r_i < c_i).astype(BF16)

        @pl.when(n == 0)
        def _():
            dk_ref[...] = jnp.zeros_like(dk_ref)
            dv_ref[...] = jnp.zeros_like(dv_ref)

        def block(j, carry, strict):
            start = pl.multiple_of(j * blk, blk)
            out = []
            for hh in range(hps):
                run, grun, dq = carry[hh]
                kb = k_ref[hh, pl.ds(start, blk), :]
                vb = v_ref[hh, pl.ds(start, blk), :]
                z, e, lb, lk = _stick_terms(qs[hh], kb, scale, strict)
                w = _keep(strict, jnp.exp(lb + tots[hh] - (run + _tri_sum(lk, upto))))
                dw = lax.dot_general(do16[hh], vb, (((1,), (1,)), ((), ())), preferred_element_type=F32)
                g = w * dw
                before = grun + _tri_sum(g, below)
                inv = 1.0 / (1.0 + e)
                sig = jnp.where(z >= 0, inv, e * inv)
                dz16 = (_keep(strict, g * (1.0 - sig) - sig * before) * scale).astype(BF16)
                dk_ref[hh, pl.ds(start, blk), :] += lax.dot_general(dz16, qs[hh], (((0,), (0,)), ((), ())), preferred_element_type=F32)
                dv_ref[hh, pl.ds(start, blk), :] += lax.dot_general(w.astype(BF16), do16[hh], (((0,), (0,)), ((), ())),
                                                                    preferred_element_type=F32)
                out.append((run + jnp.sum(lk, axis=1, keepdims=True), grun + jnp.sum(g, axis=1, keepdims=True),
                            dq + jnp.dot(dz16, kb, preferred_element_type=F32)))
            return tuple(out)

        zero = jnp.zeros((blk, 1), F32)
        init = tuple((zero, zero, jnp.zeros((blk, dh), F32)) for _ in range(hps))
        res = block(n, lax.fori_loop(0, n, lambda j, carry: block(j, carry, None), init), r_i > c_i)
        for hh in range(hps):
            dq_ref[hh] = res[hh][2]

    qs_ = pl.BlockSpec((hps, blk, dh), lambda g, i: (g, i, 0))
    ks_ = pl.BlockSpec((hps, T, dh), lambda g, i: (g, 0, 0))
    ts_ = pl.BlockSpec((hps, blk, 1), lambda g, i: (g, i, 0))
    shp = jax.ShapeDtypeStruct((H, T, dh), F32)
    return pl.pallas_call(
        body, name=name, grid=(H // hps, T // blk), in_specs=[qs_, ks_, ks_, ts_, qs_], out_specs=[qs_, ks_, ks_],
        out_shape=[shp, shp, shp],
        compiler_params=_params("parallel", "arbitrary"),
    )(q, k, v, tot, do)


def _mix_fwd(outs, lses, *, name, rows=512):
    H, T, dh = outs[0].shape
    rows = _tile(T, rows)

    def body(o0, o1, o2, l0, l1, l2, y_ref):
        ls = [l0[...], l1[...], l2[...]]
        mx = jnp.maximum(jnp.maximum(ls[0], ls[1]), ls[2])
        es = [jnp.exp(x - mx) for x in ls]
        den = es[0] + es[1] + es[2]
        y_ref[...] = (es[0] * o0[...] + es[1] * o1[...] + es[2] * o2[...]) / den

    os_ = pl.BlockSpec((None, rows, dh), lambda h, i: (h, i, 0))
    ls_ = pl.BlockSpec((None, rows, 1), lambda h, i: (h, i, 0))
    return pl.pallas_call(
        body, name=name, grid=(H, T // rows), in_specs=[os_] * 3 + [ls_] * 3, out_specs=os_,
        out_shape=jax.ShapeDtypeStruct((H, T, dh), F32),
        compiler_params=_params("parallel", "parallel"),
    )(*outs, *lses)


def _mix_bwd(outs, lses, dy, *, name, rows=512):
    H, T, dh = outs[0].shape
    rows = _tile(T, rows)

    def body(o0, o1, o2, l0, l1, l2, dy_ref, d0, d1, d2, g0, g1, g2):
        ls = [l0[...], l1[...], l2[...]]
        mx = jnp.maximum(jnp.maximum(ls[0], ls[1]), ls[2])
        es = [jnp.exp(x - mx) for x in ls]
        den = es[0] + es[1] + es[2]
        mix = [e / den for e in es]
        dyv = dy_ref[...]
        dm = [jnp.sum(dyv * o[...], axis=1, keepdims=True) for o in (o0, o1, o2)]
        avg = mix[0] * dm[0] + mix[1] * dm[1] + mix[2] * dm[2]
        for d_ref, g_ref, w, t in zip((d0, d1, d2), (g0, g1, g2), mix, dm):
            d_ref[...] = w * dyv
            g_ref[...] = w * (t - avg)

    os_ = pl.BlockSpec((None, rows, dh), lambda h, i: (h, i, 0))
    ls_ = pl.BlockSpec((None, rows, 1), lambda h, i: (h, i, 0))
    so = jax.ShapeDtypeStruct((H, T, dh), F32)
    sl = jax.ShapeDtypeStruct((H, T, 1), F32)
    res = pl.pallas_call(
        body, name=name, grid=(H, T // rows), in_specs=[os_] * 3 + [ls_] * 3 + [os_], out_specs=[os_] * 3 + [ls_] * 3,
        out_shape=[so] * 3 + [sl] * 3,
        compiler_params=_params("parallel", "parallel"),
    )(*outs, *lses, dy)
    return res[:3], res[3:]


def _position():
    return lax.axis_index("x"), lax.axis_index("y"), lax.axis_index("c")


def _other_chips(x, y):
    return [(1 - x, y), (x, 1 - y), (1 - x, 1 - y)]


HBM_SPEC = pl.BlockSpec(memory_space=pltpu.HBM)
SEM_SPEC = pl.BlockSpec(memory_space=pltpu.SEMAPHORE)
ANY_SPEC = pl.BlockSpec(memory_space=pl.ANY)
SPLIT_COPY = pltpu.CompilerParams(has_side_effects=pltpu.SideEffectType.DATAFLOW_SIDE_EFFECTING)


def _in_hbm(a):
    return pltpu.with_memory_space_constraint(a, pltpu.HBM)


SAME_CORE_OF_OTHER_CHIPS = ((1, 0, 0), (0, 1, 0), (1, 1, 0))
ALL_OTHER_DEVICES = ((0, 0, 1), (1, 0, 0), (0, 1, 0), (1, 1, 0), (1, 0, 1), (0, 1, 1), (1, 1, 1))


def _peer_copies(srcs, lands, send_sems, recv_sems, relations, src_of, dst_of):
    me = _position()
    copies = []
    for w in range(len(srcs)):
        for k, flips in enumerate(relations):
            peer = tuple(1 - p if f else p for p, f in zip(me, flips))
            i = len(relations) * w + k
            copies.append(pltpu.make_async_remote_copy(
                src_ref=src_of(srcs[w], peer), dst_ref=dst_of(lands[w], k, peer, me), send_sem=send_sems[i],
                recv_sem=recv_sems[i], device_id=peer, device_id_type=MESH))
    return copies


def _copies_start(srcs, land_shapes, after, relations, src_of, dst_of, *, name):
    n = len(srcs)
    m = len(relations) * n

    def body(*refs):
        src_refs, land_refs = refs[:n], refs[n:2 * n]
        send_sems, recv_sems = refs[2 * n + 1:2 * n + 1 + m], refs[2 * n + 1 + m:2 * n + 1 + 2 * m]
        token = refs[-1]
        for cp in _peer_copies(src_refs, land_refs, send_sems, recv_sems, relations, src_of, dst_of):
            cp.start()
        token[...] = jnp.zeros_like(token)

    lands = [_in_hbm(lax.empty(s.shape, s.dtype)) for s in land_shapes]
    out_shape = ([pltpu.SemaphoreType.DMA(())] * (2 * m)
                 + [pltpu.HBM(a.shape, a.dtype) for a in srcs] + [pltpu.HBM(s.shape, s.dtype) for s in land_shapes]
                 + [jax.ShapeDtypeStruct((8, 128), F32)])
    res = pl.pallas_call(
        body, name=name, in_specs=[HBM_SPEC] * (2 * n) + [ANY_SPEC],
        out_specs=[SEM_SPEC] * (2 * m) + [HBM_SPEC] * (2 * n) + [pl.BlockSpec(memory_space=pltpu.VMEM)],
        out_shape=out_shape, input_output_aliases={i: 2 * m + i for i in range(2 * n)}, compiler_params=SPLIT_COPY,
    )(*[_in_hbm(a) for a in srcs], *lands, after)
    return (list(res[:m]), list(res[m:2 * m]), list(res[2 * m:2 * m + n]), list(res[2 * m + n:2 * m + 2 * n]), res[-1])


def _copies_wait(started, after, relations, src_of, dst_of, *, name):
    send_sems, recv_sems, srcs, lands, _ = started
    n = len(srcs)
    m = len(relations) * n

    def body(*refs):
        src_refs, land_refs = refs[:n], refs[n:2 * n]
        send_refs, recv_refs = refs[2 * n:2 * n + m], refs[2 * n + m:2 * n + 2 * m]
        for cp in _peer_copies(src_refs, land_refs, send_refs, recv_refs, relations, src_of, dst_of):
            cp.wait_send()
            cp.wait_recv()

    res = pl.pallas_call(
        body, name=name, in_specs=[HBM_SPEC] * (2 * n) + [SEM_SPEC] * (2 * m) + [ANY_SPEC], out_specs=[HBM_SPEC] * (2 * n),
        out_shape=[pltpu.HBM(a.shape, a.dtype) for a in srcs + lands],
        input_output_aliases={i: i for i in range(2 * n)}, compiler_params=SPLIT_COPY,
    )(*srcs, *lands, *send_sems, *recv_sems, after)
    return list(res[:n]), list(res[n:])


def _half_rows(ref, core):
    half = ref.shape[-2] // 2
    return pl.ds(core * half, half)


def _gather_src(src, peer):
    return src.at[_half_rows(src, peer[2]), :]


def _gather_dst(land, k, peer, me):
    return land.at[2 * me[0] + me[1], _half_rows(land, me[2]), :]


def _gather_landed(land, k, peer, me):
    return land.at[2 * peer[0] + peer[1], _half_rows(land, me[2]), :]


def _exchange_src(src, peer):
    return src.at[2 * peer[0] + peer[1], _half_rows(src, peer[2]), :]


def _exchange_dst(land, k, peer, me):
    return land.at[k]


def _small_src(src, peer):
    return src


def _small_dst(land, k, peer, me):
    return land.at[4 * me[0] + 2 * me[1] + me[2]]


def _small_landed(land, k, peer, me):
    return land.at[4 * peer[0] + 2 * peer[1] + peer[2]]


def _forward_halves(bufs, *, name):
    n = len(bufs)

    def body(*refs):
        ins, outs = refs[:n], refs[n:2 * n]
        send_sems, recv_sems = refs[2 * n:]
        x, y, c = _position()
        chips = _other_chips(x, y)

        def copy(w, j, chip, core):
            holder = (x, y, core)
            return pltpu.make_async_remote_copy(src_ref=_gather_landed(ins[w], j, chip, holder),
                                                dst_ref=_gather_landed(outs[w], j, chip, holder),
                                                send_sem=send_sems.at[w, j], recv_sem=recv_sems.at[w, j],
                                                device_id=(x, y, 1 - c), device_id_type=MESH)

        sends = [copy(w, j, chip, c) for w in range(n) for j, chip in enumerate(chips)]
        for cp in sends:
            cp.start()
        for w in range(n):
            for j, chip in enumerate(chips):
                copy(w, j, chip, 1 - c).wait_recv()
        for cp in sends:
            cp.wait_send()

    return pl.pallas_call(
        body, name=name, in_specs=[ANY_SPEC] * n, out_specs=[ANY_SPEC] * n,
        out_shape=[jax.ShapeDtypeStruct(a.shape, a.dtype) for a in bufs], input_output_aliases={w: w for w in range(n)},
        scratch_shapes=[pltpu.SemaphoreType.DMA((n, 3)), pltpu.SemaphoreType.DMA((n, 3))],
    )(*bufs)


def _share_halves(bufs, *, name):
    n = len(bufs)

    def body(*refs):
        ins, outs = refs[:n], refs[n:2 * n]
        send_sems, recv_sems = refs[2 * n:]
        x, y, c = _position()
        sends = []
        for w in range(n):
            sends.append(pltpu.make_async_remote_copy(src_ref=ins[w].at[c], dst_ref=outs[w].at[c], send_sem=send_sems.at[w],
                                                      recv_sem=recv_sems.at[w], device_id=(x, y, 1 - c), device_id_type=MESH))
            sends[-1].start()
        for w in range(n):
            pltpu.make_async_remote_copy(src_ref=ins[w].at[1 - c], dst_ref=outs[w].at[1 - c], send_sem=send_sems.at[w],
                                         recv_sem=recv_sems.at[w], device_id=(x, y, 1 - c), device_id_type=MESH).wait_recv()
        for cp in sends:
            cp.wait_send()

    hbm = pl.BlockSpec(memory_space=pl.ANY)
    return pl.pallas_call(
        body, name=name, in_specs=[hbm] * n, out_specs=[hbm] * n,
        out_shape=[jax.ShapeDtypeStruct(a.shape, a.dtype) for a in bufs],
        input_output_aliases={w: w for w in range(n)},
        scratch_shapes=[pltpu.SemaphoreType.DMA((n,)), pltpu.SemaphoreType.DMA((n,))],
    )(*bufs)


def _all_gather8(v, *, name):
    m, n = v.shape

    def body(v_ref, out_ref, send_sems, recv_sems, local_sem):
        x, y, c = _position()
        me, sibling = (x, y, c), (x, y, 1 - c)
        chips = _other_chips(x, y)

        def rows(px, py, pc):
            return out_ref.at[pl.ds((4 * px + 2 * py + pc) * m, m), :]

        def copy(k, block, to, src=None):
            return pltpu.make_async_remote_copy(src_ref=rows(*block) if src is None else src, dst_ref=rows(*block),
                                                send_sem=send_sems.at[k], recv_sem=recv_sems.at[k], device_id=to, device_id_type=MESH)

        mine = pltpu.make_async_copy(v_ref, rows(*me), local_sem)
        mine.start()
        first = [copy(0, me, sibling, src=v_ref)]
        first += [copy(1 + j, me, (*chip, c), src=v_ref) for j, chip in enumerate(chips)]
        for cp in first:
            cp.start()
        passed = [copy(4 + j, (*chip, c), sibling) for j, chip in enumerate(chips)]
        for j, chip in enumerate(chips):
            copy(1 + j, (*chip, c), me).wait_recv()
            passed[j].start()
        copy(0, sibling, me).wait_recv()
        for j, chip in enumerate(chips):
            copy(4 + j, (*chip, 1 - c), me).wait_recv()
        for cp in first + passed:
            cp.wait_send()
        mine.wait()

    vmem = pl.BlockSpec(memory_space=pltpu.VMEM)
    return pl.pallas_call(
        body, name=name, in_specs=[vmem], out_specs=vmem, out_shape=jax.ShapeDtypeStruct((8 * m, n), v.dtype),
        scratch_shapes=[pltpu.SemaphoreType.DMA((7,)), pltpu.SemaphoreType.DMA((7,)), pltpu.SemaphoreType.DMA],
    )(v)


def _sum_parts(own, landed, where, *, name, rows=256):
    n_peers, half, C = landed.shape
    rows = _tile(half, rows)
    nb = half // rows

    def body(where_ref, own_ref, land_ref, o_ref):
        acc = own_ref[...].astype(F32)
        for k in range(n_peers):
            acc = acc + land_ref[k].astype(F32)
        o_ref[...] = acc

    grid_spec = pltpu.PrefetchScalarGridSpec(
        num_scalar_prefetch=1, grid=(nb,),
        in_specs=[pl.BlockSpec((None, rows, C), lambda i, where_ref: (where_ref[0], where_ref[1] * nb + i, 0)),
                  pl.BlockSpec((n_peers, rows, C), lambda i, where_ref: (0, i, 0))],
        out_specs=pl.BlockSpec((None, rows, C), lambda i, where_ref: (where_ref[1], i, 0)))
    return pl.pallas_call(
        body, name=name, grid_spec=grid_spec, out_shape=jax.ShapeDtypeStruct((2, half, C), F32),
        compiler_params=_params("parallel"),
    )(where, own, landed)


def _sum_slabs(a, *, name, rows=256):
    P, R, C = a.shape
    rows = _tile(R, rows)

    def body(a_ref, o_ref):
        acc = a_ref[0].astype(F32)
        for p in range(1, P):
            acc = acc + a_ref[p].astype(F32)
        o_ref[...] = acc

    return pl.pallas_call(
        body, name=name, grid=(R // rows,), in_specs=[pl.BlockSpec((P, rows, C), lambda i: (0, i, 0))],
        out_specs=pl.BlockSpec((rows, C), lambda i: (i, 0)),
        out_shape=jax.ShapeDtypeStruct((R, C), F32), compiler_params=_params("parallel"),
    )(a)


def _reduce_scatter_start(grads, after=None, *, name):
    lands = [jax.ShapeDtypeStruct((len(ALL_OTHER_DEVICES), g.shape[1] // 2, g.shape[2]), g.dtype) for g in grads]
    return _copies_start(grads, lands, grads[0] if after is None else after, ALL_OTHER_DEVICES, _exchange_src, _exchange_dst,
                         name=name + "_start")


def _reduce_scatter_finish(started, after, *, name):
    core = lax.axis_index("c").astype(jnp.int32)
    chip = (2 * lax.axis_index("x") + lax.axis_index("y")).astype(jnp.int32)
    where = jnp.stack([chip, core])
    sums, landed = _copies_wait(started, after, ALL_OTHER_DEVICES, _exchange_src, _exchange_dst, name=name + "_wait")
    reduced = [_sum_parts(s, a, where, name=f"{name}_sum{w}") for w, (s, a) in enumerate(zip(sums, landed))]
    both = _share_halves(reduced, name=name + "_share")
    return [b.reshape(2 * b.shape[1], b.shape[2]) for b in both]


def _to_heads(t, heads, dil=1):
    S = t.shape[0]
    d = t.shape[1] // heads
    return jnp.transpose(t.reshape(S // dil, dil, heads, d), (2, 1, 0, 3)).reshape(heads, S, d)


def _from_heads(t, dil=1):
    heads, S, d = t.shape
    return jnp.transpose(t.reshape(heads, dil, S // dil, d), (2, 1, 0, 3)).reshape(S, heads * d)


def _unstride(t, dil):
    heads, S, d = t.shape
    return jnp.transpose(t.reshape(heads, dil, S // dil, d), (0, 2, 1, 3)).reshape(heads, S, d)


def _restride(t, dil):
    heads, S, d = t.shape
    return jnp.transpose(t.reshape(heads, S // dil, dil, d), (0, 2, 1, 3)).reshape(heads, S, d)


def _pad_cols(t, width, padded):
    S = t.shape[0]
    return jnp.pad(t.reshape(S, N_CHIPS, width), ((0, 0), (0, 0), (0, padded - width))).reshape(S, N_CHIPS * padded)


def _column_reader(t, width, padded):
    def take(lo, hi):
        pieces = []
        for p in range(N_CHIPS):
            a, b = max(lo, p * width), min(hi, (p + 1) * width)
            if a < b:
                pieces.append(t[:, p * padded + a - p * width:p * padded + b - p * width])
        return pieces[0] if len(pieces) == 1 else jnp.concatenate(pieces, axis=-1)
    return take


def _alibi(n):
    return 2.0 ** (-8.0 * jnp.arange(1, n + 1, dtype=F32) / n)


B_KW = [dict(hps=4, nb_seq=SEQ // d // ATT_BLK, n_back=w // d, scale=1.0 / math.sqrt(HEAD_DIM), dist_scale=d)
        for w, d in B_PATTERNS]
A_KW = dict(hps=A_Q_HEADS // A_KV_HEADS, nb_seq=SEQ // ATT_BLK, n_back=A_WINDOW - 1, scale=1.0 / math.sqrt(HEAD_DIM), dist_scale=1)
D_KW = dict(blk=256, hps=2, scale=1.0 / math.sqrt(D_NOPE + D_ROPE))
C_KW = dict(blk=256, hps=2, scale=1.0 / math.sqrt(HEAD_DIM))


def _front_block(t):
    return jnp.pad(t, ((0, 0), (ATT_BLK, 0), (0, 0)))


def _rope_tables(reps):
    inv_freq = ROPE_BASE ** (-jnp.arange(0, D_ROPE, 2, dtype=F32) / D_ROPE)
    ang = jnp.arange(SEQ, dtype=F32)[:, None] * inv_freq[None, :]
    return jnp.tile(jnp.cos(ang), (1, reps)), jnp.tile(jnp.sin(ang), (1, reps))


def _mlp_fwd(x, x16, w1, w2, g, b, tag):
    hid, act = _matmul(x16, w1, mode="nn", out_dtype=BF16, epilogue="relu2", name=f"mlp{tag}_up")
    m = _matmul(act, w2, mode="nn", out_dtype=F32, name=f"mlp{tag}_down")
    y, y16, u = _norm_fwd(x, m, g, b, center=True, eps=LN_EPS, alpha=ALPHA, name=f"ln2_{tag}")
    return y, y16, (x16, hid, act, u)


def _mlp_bwd(dy, saved, w1, w2, g, tag, after=None):
    x16, hid, act, u = saved
    du, du16, dg, db = _norm_bwd(dy, u, g, center=True, eps=LN_EPS, name=f"ln2_{tag}_bwd", after=after)
    dw2 = _matmul_tn(act, du16, shards=1, name=f"mlp{tag}_dw2")
    dhid = _matmul(du16, w2, mode="nt", out_dtype=BF16, epilogue="drelu2", extra=hid, name=f"mlp{tag}_dact")
    dw1 = _matmul_tn(x16, dhid, shards=N_CHIPS, name=f"mlp{tag}_dw1")
    dx = _matmul(dhid, w1, mode="nt", out_dtype=F32, epilogue="add", extra=du, alpha=ALPHA, name=f"mlp{tag}_dx")
    return dx, dw1, dw2, dg, db


def _even_fwd(x16, w_in, sinks):
    h = _column_reader(_matmul(x16, w_in, mode="nn", out_dtype=BF16, name="even_in"), EVEN_IN // N_CHIPS, EVEN_IN_PAD)
    qa = _to_heads(h(0, A_Q_W), A_Q_HEADS)
    ka = _front_block(_to_heads(h(A_Q_W, A_Q_W + A_KV_W), A_KV_HEADS))
    va = _front_block(_to_heads(h(A_Q_W + A_KV_W, A_Q_W + 2 * A_KV_W), A_KV_HEADS))
    slopes_a, slopes_b = _alibi(A_Q_HEADS), _alibi(B_HEADS)
    oa, lse_a = _band_fwd(qa, ka, va, slopes_a, sinks, name="swa_fwd", **A_KW)
    qkv_b, outs, lses = [], [], []
    for gi, (_, dil) in enumerate(B_PATTERNS):
        base = A_Q_W + 2 * A_KV_W + gi * 3 * B_W
        q, k, v = (_to_heads(h(base + i * B_W, base + (i + 1) * B_W), B_HEADS, dil) for i in range(3))
        k, v = _front_block(k), _front_block(v)
        o, lse = _band_fwd(q, k, v, slopes_b, None, name=f"dil{gi}_fwd", **B_KW[gi])
        qkv_b.append((q, k, v, o, lse))
        outs.append(_unstride(o, dil))
        lses.append(_unstride(lse, dil))
    ob = _mix_fwd(outs, lses, name="dil_mix")
    y16 = jnp.concatenate([_from_heads(oa), _from_heads(ob)], axis=-1).astype(BF16)
    return y16, (x16, qa, ka, va, oa, lse_a, qkv_b, outs, lses, y16)


def _even_bwd(dmix16, du, saved, w_in, sinks, w_out, send_w_out, send_w_in):
    x16, qa, ka, va, oa, lse_a, qkv_b, outs, lses, y16 = saved
    slopes_a, slopes_b = _alibi(A_Q_HEADS), _alibi(B_HEADS)
    sent = send_w_out(_matmul_tn(y16, dmix16, shards=N_CHIPS, name="even_dwout"))
    dy = _matmul(dmix16, w_out, mode="nt", out_dtype=F32, name="even_dy", after=sent)
    doa = _to_heads(dy[:, :A_Q_W], A_Q_HEADS)
    dob = _to_heads(dy[:, A_Q_W:], B_HEADS)
    dqa, dka, dva, dsink = _band_bwd(qa, ka, va, oa, doa, lse_a, None, slopes_a, sinks, name="swa_bwd", **A_KW)
    douts, dlses = _mix_bwd(outs, lses, dob, name="dil_mix_bwd")
    parts = [_from_heads(dqa), _from_heads(dka[:, ATT_BLK:]), _from_heads(dva[:, ATT_BLK:])]
    for gi, (_, dil) in enumerate(B_PATTERNS):
        q, k, v, o, lse = qkv_b[gi]
        dq, dk, dv = _band_bwd(q, k, v, o, _restride(douts[gi], dil), lse, _restride(dlses[gi], dil), slopes_b, None,
                               name=f"dil{gi}_bwd", **B_KW[gi])
        parts += [_from_heads(dq, dil), _from_heads(dk[:, ATT_BLK:], dil), _from_heads(dv[:, ATT_BLK:], dil)]
    dh16 = _pad_cols(jnp.concatenate(parts, axis=-1), EVEN_IN // N_CHIPS, EVEN_IN_PAD).astype(BF16)
    sent = send_w_in(_matmul_tn(x16, dh16, shards=N_CHIPS, name="even_dwin"), dsink[:, 0, 0])
    return _matmul(dh16, w_in, mode="nt", out_dtype=F32, epilogue="add", extra=du, alpha=ALPHA, name="even_dx", after=sent)


def _odd_fwd(x16, w_in, qn_g, kvn_g, w_uq, w_ukv, w_out):
    S = x16.shape[0]
    h = _column_reader(_matmul(x16, w_in, mode="nn", out_dtype=F32, name="odd_in"), ODD_IN // N_CHIPS, ODD_IN_PAD)
    qc, kc, vc = (_to_heads(h(i * C_W, (i + 1) * C_W).astype(BF16), C_HEADS) for i in range(3))
    cq = h(3 * C_W, 3 * C_W + D_Q_RANK)
    ckv = h(3 * C_W + D_Q_RANK, 3 * C_W + D_Q_RANK + D_KV_RANK)
    kr = h(3 * C_W + D_Q_RANK + D_KV_RANK, ODD_IN)
    oc, tot = _stick_fwd(qc, kc, vc, name="stick_fwd", **C_KW)
    _, cqn16, _ = _norm_fwd(cq, None, qn_g, None, center=False, eps=RMS_EPS, alpha=1.0, name="q_rms")
    _, ckvn16, _ = _norm_fwd(ckv, None, kvn_g, None, center=False, eps=RMS_EPS, alpha=1.0, name="kv_rms")
    q = _matmul(cqn16, w_uq, mode="nn", out_dtype=F32, name="mla_uq").reshape(S, D_HEADS, D_NOPE + D_ROPE)
    kv = _matmul(ckvn16, w_ukv, mode="nn", out_dtype=F32, name="mla_ukv").reshape(S, D_HEADS, D_NOPE + D_V)
    half = D_ROPE // 2
    q_rope = q[..., D_NOPE:]
    x1 = jnp.concatenate([q_rope[..., :half].reshape(S, D_HEADS * half), kr[:, :half]], axis=-1)
    x2 = jnp.concatenate([q_rope[..., half:].reshape(S, D_HEADS * half), kr[:, half:]], axis=-1)
    cos, sin = _rope_tables(D_HEADS + 1)
    y1, y2 = _rope(x1[None], x2[None], cos, sin, name="rope_fwd")
    nq = D_HEADS * half
    q_rot = jnp.concatenate([y1[:, :nq].reshape(S, D_HEADS, half), y2[:, :nq].reshape(S, D_HEADS, half)], axis=-1)
    kr_rot = jnp.concatenate([y1[:, nq:], y2[:, nq:]], axis=-1)
    qd = jnp.transpose(jnp.concatenate([q[..., :D_NOPE], q_rot], axis=-1), (1, 0, 2)).astype(BF16)
    kd = jnp.transpose(jnp.concatenate([kv[..., :D_NOPE], jnp.broadcast_to(kr_rot[:, None, :], (S, D_HEADS, D_ROPE))], axis=-1),
                       (1, 0, 2)).astype(BF16)
    vd = jnp.transpose(kv[..., D_NOPE:], (1, 0, 2)).astype(BF16)
    od, lse_d = _causal_fwd(qd, kd, vd, name="mla_fwd", **D_KW)
    y16 = jnp.concatenate([_from_heads(oc), _from_heads(od)], axis=-1).astype(BF16)
    mixed = _matmul(y16, w_out, mode="nn", out_dtype=F32, name="odd_out")
    return mixed, (x16, qc, kc, vc, tot, cq, ckv, cqn16, ckvn16, qd, kd, vd, od, lse_d, y16)


def _odd_bwd(dmix16, du, saved, w_in, qn_g, kvn_g, w_uq, w_ukv, w_out):
    x16, qc, kc, vc, tot, cq, ckv, cqn16, ckvn16, qd, kd, vd, od, lse_d, y16 = saved
    S = x16.shape[0]
    half = D_ROPE // 2
    dw_out = _matmul_tn(y16, dmix16, shards=1, name="odd_dwout")
    dy = _matmul(dmix16, w_out, mode="nt", out_dtype=F32, name="odd_dy")
    doc = _to_heads(dy[:, :C_W], C_HEADS)
    dod = _to_heads(dy[:, C_W:], D_HEADS)
    dqc, dkc, dvc = _stick_bwd(qc, kc, vc, tot, doc, name="stick_bwd", **C_KW)
    dqd, dkd, dvd = _causal_bwd(qd, kd, vd, od, dod, lse_d, name="mla_bwd", **D_KW)
    cos, sin = _rope_tables(D_HEADS + 1)
    nq = D_HEADS * half
    gq1 = jnp.transpose(dqd[..., D_NOPE:D_NOPE + half], (1, 0, 2)).reshape(1, S, nq)
    gq2 = jnp.transpose(dqd[..., D_NOPE + half:], (1, 0, 2)).reshape(1, S, nq)
    dq1, dq2 = _rope(gq1, gq2, cos[:, :nq], -sin[:, :nq], name="rope_bwd_q")
    dk1, dk2 = _rope(dkd[..., D_NOPE:D_NOPE + half], dkd[..., D_NOPE + half:], cos[:, nq:], -sin[:, nq:], name="rope_bwd_k")
    dq_full = jnp.concatenate([jnp.transpose(dqd[..., :D_NOPE], (1, 0, 2)), dq1.reshape(S, D_HEADS, half),
                               dq2.reshape(S, D_HEADS, half)], axis=-1).reshape(S, D_HEADS * (D_NOPE + D_ROPE)).astype(BF16)
    dkv_full = jnp.concatenate([jnp.transpose(dkd[..., :D_NOPE], (1, 0, 2)), jnp.transpose(dvd, (1, 0, 2))],
                               axis=-1).reshape(S, D_HEADS * (D_NOPE + D_V)).astype(BF16)
    dw_uq = _matmul_tn(cqn16, dq_full, shards=N_CHIPS, name="mla_dwuq")
    dw_ukv = _matmul_tn(ckvn16, dkv_full, shards=N_CHIPS, name="mla_dwukv")
    dcqn = _matmul(dq_full, w_uq, mode="nt", out_dtype=F32, name="mla_dcq")
    dckvn = _matmul(dkv_full, w_ukv, mode="nt", out_dtype=F32, name="mla_dckv")
    dcq, _, dqn_g, _ = _norm_bwd(dcqn, cq, qn_g, center=False, eps=RMS_EPS, name="q_rms_bwd")
    dckv, _, dkvn_g, _ = _norm_bwd(dckvn, ckv, kvn_g, center=False, eps=RMS_EPS, name="kv_rms_bwd")
    dh = jnp.concatenate([_from_heads(dqc), _from_heads(dkc), _from_heads(dvc), dcq, dckv, dk1, dk2], axis=-1)
    dh16 = _pad_cols(dh, ODD_IN // N_CHIPS, ODD_IN_PAD).astype(BF16)
    dw_in = _matmul_tn(x16, dh16, shards=N_CHIPS, name="odd_dwin")
    dx = _matmul(dh16, w_in, mode="nt", out_dtype=F32, epilogue="add", extra=du, alpha=ALPHA, name="odd_dx")
    return dx, dw_in, dqn_g[0], dkvn_g[0], dw_uq, dw_ukv, dw_out


WEIGHT_GROUPS = (("even_w_in",), ("even_w_out", "mlp_w1_0", "mlp_w2_0"), ("odd_w_in", "odd_w_uq", "odd_w_ukv", "odd_w_out"),
                 ("mlp_w1_1", "mlp_w2_1"))
GRAD_GROUPS = (("mlp_w2_1", "mlp_w1_1"), ("odd_w_out", "odd_w_uq", "odd_w_ukv", "odd_w_in"), ("mlp_w2_0", "mlp_w1_0"),
               ("even_w_out",), ("even_w_in",))


def _local_step(x, target, small, weights_for, send_grads, send_small, total_loss):
    def by_rows(t, rows):
        return t.reshape(N_CHIPS, rows // N_CHIPS, D_MODEL)

    x16 = x.astype(BF16)
    w = dict(weights_for(0, x16))
    y16, sv_e = _even_fwd(x16, w["even_w_in"], small["even_sinks"])
    w.update(weights_for(1, y16))
    mixed0 = _matmul(y16, w["even_w_out"], mode="nn", out_dtype=F32, name="even_out")
    x1, x1_16, u01 = _norm_fwd(x, mixed0, small["ln1_g"][0], small["ln1_b"][0], center=True, eps=LN_EPS, alpha=ALPHA, name="ln1_0")
    w1_0, w2_0 = w["mlp_w1_0"], w["mlp_w2_0"].reshape(1, D_FF, D_MODEL)
    x2, x2_16, sv_m0 = _mlp_fwd(x1, x1_16, w1_0, w2_0, small["ln2_g"][0], small["ln2_b"][0], 0)
    w.update(weights_for(2, x2_16))
    odd_w = (w["odd_w_in"], small["odd_q_norm_g"], small["odd_kv_norm_g"], w["odd_w_uq"], w["odd_w_ukv"],
             w["odd_w_out"].reshape(1, D_MODEL, D_MODEL))
    mixed1, sv_o = _odd_fwd(x2_16, *odd_w)
    x3, x3_16, u11 = _norm_fwd(x2, mixed1, small["ln1_g"][1], small["ln1_b"][1], center=True, eps=LN_EPS, alpha=ALPHA, name="ln1_1")
    w.update(weights_for(3, x3_16))
    w1_1, w2_1 = w["mlp_w1_1"], w["mlp_w2_1"].reshape(1, D_FF, D_MODEL)
    x4, _, sv_m1 = _mlp_fwd(x3, x3_16, w1_1, w2_1, small["ln2_g"][1], small["ln2_b"][1], 1)
    dy, loss_row = _loss_head(x4, target, name="loss_head")
    loss = total_loss(loss_row)

    dx3, dw1_1, dw2_1, dln2g_1, dln2b_1 = _mlp_bwd(dy, sv_m1, w1_1, w2_1, small["ln2_g"][1], 1, after=loss.reshape(1, 1))
    sent = send_grads(0, dict(mlp_w2_1=by_rows(dw2_1, D_FF), mlp_w1_1=dw1_1))
    du, du16, dln1g_1, dln1b_1 = _norm_bwd(dx3, u11, small["ln1_g"][1], center=True, eps=LN_EPS, name="ln1_1_bwd", after=sent)
    dx2, dwin_o, dqn_g, dkvn_g, dwuq, dwukv, dwout_o = _odd_bwd(du16, du, sv_o, *odd_w)
    sent = send_grads(1, dict(odd_w_out=by_rows(dwout_o, D_MODEL), odd_w_uq=dwuq, odd_w_ukv=dwukv, odd_w_in=dwin_o))
    dx1, dw1_0, dw2_0, dln2g_0, dln2b_0 = _mlp_bwd(dx2, sv_m0, w1_0, w2_0, small["ln2_g"][0], 0, after=sent)
    sent = send_grads(2, dict(mlp_w2_0=by_rows(dw2_0, D_FF), mlp_w1_0=dw1_0))
    du, du16, dln1g_0, dln1b_0 = _norm_bwd(dx1, u01, small["ln1_g"][0], center=True, eps=LN_EPS, name="ln1_0_bwd", after=sent)
    def send_last(dwin_e, dsinks):
        went = send_small(dict(even_sinks=dsinks, odd_q_norm_g=dqn_g, odd_kv_norm_g=dkvn_g,
                               ln1_g=jnp.concatenate([dln1g_0, dln1g_1]), ln1_b=jnp.concatenate([dln1b_0, dln1b_1]),
                               ln2_g=jnp.concatenate([dln2g_0, dln2g_1]), ln2_b=jnp.concatenate([dln2b_0, dln2b_1])))
        return send_grads(4, dict(even_w_in=dwin_e), went)

    dx0 = _even_bwd(du16, du, sv_e, w["even_w_in"], small["even_sinks"], w["even_w_out"],
                    lambda dwout_e: send_grads(3, dict(even_w_out=dwout_e)), send_last)
    return loss, dx0


SMALL_ORDER = ("ln1_g", "ln1_b", "ln2_g", "ln2_b", "even_sinks", "odd_q_norm_g", "odd_kv_norm_g")
SMALL_COLS = 2176


def _pack_small(parts):
    flat = jnp.concatenate([p.reshape(-1) for p in parts])
    return jnp.pad(flat, (0, 8 * SMALL_COLS - flat.shape[0])).reshape(8, SMALL_COLS)


def _unpack_small(packed, shapes):
    flat = packed.reshape(-1)
    out, pos = [], 0
    for s in shapes:
        n = math.prod(s)
        out.append(flat[pos:pos + n].reshape(s))
        pos += n
    return out


def kernel(x, even_w_in, even_sinks, even_w_out, odd_w_in, odd_q_norm_g, odd_kv_norm_g, odd_w_uq, odd_w_ukv, odd_w_out, ln1_g, ln1_b, mlp_w1, mlp_w2, ln2_g, ln2_b, loss_target, m_even_w_in, m_even_sinks, m_even_w_out, m_odd_w_in, m_odd_q_norm_g, m_odd_kv_norm_g, m_odd_w_uq, m_odd_w_ukv, m_odd_w_out, m_ln1_g, m_ln1_b, m_mlp_w1, m_mlp_w2, m_ln2_g, m_ln2_b, v_even_w_in, v_even_sinks, v_even_w_out, v_odd_w_in, v_odd_q_norm_g, v_odd_kv_norm_g, v_odd_w_uq, v_odd_w_ukv, v_odd_w_out, v_ln1_g, v_ln1_b, v_mlp_w1, v_mlp_w2, v_ln2_g, v_ln2_b):
    chip = 2 * lax.axis_index("x") + lax.axis_index("y")
    e_w, o_w = EVEN_IN // N_CHIPS, ODD_IN // N_CHIPS

    shards = dict(
        even_w_in=jnp.pad(even_w_in[0], ((0, 0), (0, EVEN_IN_PAD - e_w))), even_w_out=even_w_out[0],
        odd_w_in=jnp.pad(odd_w_in[0], ((0, 0), (0, ODD_IN_PAD - o_w))), odd_w_uq=odd_w_uq[0], odd_w_ukv=odd_w_ukv[0],
        odd_w_out=odd_w_out[0], mlp_w1_0=mlp_w1[0], mlp_w1_1=mlp_w1[1], mlp_w2_0=mlp_w2[0], mlp_w2_1=mlp_w2[1])
    names = list(shards)
    own16 = {n: shards[n].astype(BF16) for n in names}
    gather_started, token = [], own16[WEIGHT_GROUPS[0][0]]
    for g, group in enumerate(WEIGHT_GROUPS):
        lands = [jax.ShapeDtypeStruct((N_CHIPS,) + own16[n].shape, BF16) for n in group]
        gather_started.append(_copies_start([own16[n] for n in group], lands, token, SAME_CORE_OF_OTHER_CHIPS, _gather_src,
                                            _gather_dst, name=f"gather{g}_start"))
        token = gather_started[-1][-1]
    all_started = token

    def weights_for(g, after):
        group = WEIGHT_GROUPS[g]
        _, landed = _copies_wait(gather_started[g], all_started if g == 0 else after, SAME_CORE_OF_OTHER_CHIPS, _gather_src,
                                 _gather_landed, name=f"gather{g}_wait")
        full = _forward_halves(landed, name=f"gather{g}_forward")
        return {n: lax.dynamic_update_slice(f, own16[n][None], (chip, 0, 0)) for n, f in zip(group, full)}

    grads_started = {}

    def send_grads(g, shares, after=None):
        grads_started[g] = _reduce_scatter_start([shares[n] for n in GRAD_GROUPS[g]], after, name=f"grads{g}")
        return grads_started[g][-1]

    norm_rows = jnp.pad(jnp.concatenate([odd_q_norm_g[0], odd_kv_norm_g[0]])[None], ((0, 7), (0, 64)))
    norm_all = _all_gather8(norm_rows, name="gather_norm_gains")[0::16]
    qn_w, kvn_w = D_Q_RANK // N_CHIPS, D_KV_RANK // N_CHIPS
    small = dict(even_sinks=even_sinks[0], odd_q_norm_g=norm_all[:, :qn_w].reshape(D_Q_RANK),
                 odd_kv_norm_g=norm_all[:, qn_w:qn_w + kvn_w].reshape(D_KV_RANK), ln1_g=ln1_g, ln1_b=ln1_b, ln2_g=ln2_g, ln2_b=ln2_b)

    small_started = []

    def send_small(sm):
        pack = _pack_small([sm[n] for n in SMALL_ORDER])
        small_started.append(_copies_start([pack], [jax.ShapeDtypeStruct((8,) + pack.shape, F32)], pack, ALL_OTHER_DEVICES,
                                           _small_src, _small_dst, name="small_grads_start"))
        return small_started[0][-1]

    def small_sum(after):
        (pack,), (landed,) = _copies_wait(small_started[0], after, ALL_OTHER_DEVICES, _small_src, _small_landed,
                                          name="small_grads_wait")
        device = 4 * lax.axis_index("x") + 2 * lax.axis_index("y") + lax.axis_index("c")
        return _sum_slabs(lax.dynamic_update_slice(landed, pack[None], (device, 0, 0)), name="sum_small_grads")

    loss, grad_x = _local_step(x[0], loss_target[0], small, weights_for, send_grads, send_small,
                               lambda row: lax.psum(row[0, 0], ("x", "y", "c")))

    weights = dict(even_w_in=even_w_in, even_sinks=even_sinks, even_w_out=even_w_out, odd_w_in=odd_w_in, odd_q_norm_g=odd_q_norm_g,
                   odd_kv_norm_g=odd_kv_norm_g, odd_w_uq=odd_w_uq, odd_w_ukv=odd_w_ukv, odd_w_out=odd_w_out, ln1_g=ln1_g, ln1_b=ln1_b,
                   mlp_w1=mlp_w1, mlp_w2=mlp_w2, ln2_g=ln2_g, ln2_b=ln2_b)
    m_in = dict(even_w_in=m_even_w_in, even_sinks=m_even_sinks, even_w_out=m_even_w_out, odd_w_in=m_odd_w_in,
                odd_q_norm_g=m_odd_q_norm_g, odd_kv_norm_g=m_odd_kv_norm_g, odd_w_uq=m_odd_w_uq, odd_w_ukv=m_odd_w_ukv,
                odd_w_out=m_odd_w_out, ln1_g=m_ln1_g, ln1_b=m_ln1_b, mlp_w1=m_mlp_w1, mlp_w2=m_mlp_w2, ln2_g=m_ln2_g, ln2_b=m_ln2_b)
    v_in = dict(even_w_in=v_even_w_in, even_sinks=v_even_sinks, even_w_out=v_even_w_out, odd_w_in=v_odd_w_in,
                odd_q_norm_g=v_odd_q_norm_g, odd_kv_norm_g=v_odd_kv_norm_g, odd_w_uq=v_odd_w_uq, odd_w_ukv=v_odd_w_ukv,
                odd_w_out=v_odd_w_out, ln1_g=v_ln1_g, ln1_b=v_ln1_b, mlp_w1=v_mlp_w1, mlp_w2=v_mlp_w2, ln2_g=v_ln2_g, ln2_b=v_ln2_b)
    order = list(weights)
    updated = {}

    def update(n, g2d, layer=0, tag="", by_column=False):
        shape = weights[n].shape
        as3d = (1, math.prod(shape[:-1]), shape[-1]) if len(shape) == 2 else shape
        view = (lambda t: jnp.swapaxes(t.reshape(as3d), 1, 2)) if by_column else (lambda t: t.reshape(as3d))
        res = _adamw(view(weights[n]), g2d.T if by_column else g2d, view(m_in[n]), view(v_in[n]), layer=layer,
                     carry=updated.get(n), name=f"adamw_{n}{tag}")
        updated[n] = tuple(jnp.swapaxes(t, 1, 2) for t in res) if by_column else tuple(res)
        return res[0]

    after = grad_x
    for g, group in enumerate(GRAD_GROUPS):
        for n, r in zip(group, _reduce_scatter_finish(grads_started[g], after, name=f"grads{g}")):
            if n[:-2] in ("mlp_w1", "mlp_w2"):
                after = update(n[:-2], r, layer=int(n[-1]), tag=n[-2:])
            elif n == "even_w_in":
                after = update(n, r[:, :e_w], by_column=True)
            elif n == "odd_w_in":
                after = update(n, r[:, :o_w], by_column=True)
            else:
                after = update(n, r)
        if g == len(GRAD_GROUPS) - 2:
            g_ln1g, g_ln1b, g_ln2g, g_ln2b, g_sinks, g_qn, g_kvn = _unpack_small(
                small_sum(after), [(DEPTH, D_MODEL)] * 4 + [(1, A_Q_HEADS), (D_Q_RANK,), (D_KV_RANK,)])
            for n, gs in (("even_sinks", g_sinks), ("odd_q_norm_g", lax.dynamic_slice_in_dim(g_qn, chip * qn_w, qn_w)[None]),
                          ("odd_kv_norm_g", lax.dynamic_slice_in_dim(g_kvn, chip * kvn_w, kvn_w)[None]), ("ln1_g", g_ln1g),
                          ("ln1_b", g_ln1b), ("ln2_g", g_ln2g), ("ln2_b", g_ln2b)):
                update(n, gs)
    outs = [[updated[n][k].reshape(weights[n].shape) for n in order] for k in range(4)]
    return (loss, grad_x[None], *outs[0], *outs[1], *outs[2], *outs[3])
```

```python
import math

import jax
import jax.numpy as jnp
from jax import lax
from jax.experimental import pallas as pl
from jax.experimental.pallas import tpu as pltpu

F32 = jnp.float32
BF16 = jnp.bfloat16
MESH = pl.DeviceIdType.MESH

D_MODEL = 2048
SEQ = 2048
DEPTH = 2
HEAD_DIM = 64
A_Q_HEADS, A_KV_HEADS, A_WINDOW = 16, 2, 128
B_HEADS = 8
B_PATTERNS = ((128, 1), (512, 4), (2048, 16))
C_HEADS = 16
D_HEADS, D_Q_RANK, D_KV_RANK, D_NOPE, D_ROPE, D_V = 16, 512, 256, 64, 32, 64
ROPE_BASE = 10000.0
D_FF = 4 * D_MODEL
LN_EPS = 1e-5
RMS_EPS = 1e-6
ALPHA = (2 * DEPTH) ** 0.25
A_Q_W = A_Q_HEADS * HEAD_DIM
A_KV_W = A_KV_HEADS * HEAD_DIM
B_W = B_HEADS * HEAD_DIM
EVEN_IN = A_Q_W + 2 * A_KV_W + 3 * B_W * len(B_PATTERNS)
EVEN_OUT = A_Q_W + B_W
C_W = C_HEADS * HEAD_DIM
ODD_IN = 3 * C_W + D_Q_RANK + D_KV_RANK + D_ROPE
ADAM_LR, ADAM_B1, ADAM_B2, ADAM_EPS, ADAM_WD, ADAM_STEP = 0.001, 0.9, 0.999, 1e-08, 0.01, 10

N_CHIPS = 4
EVEN_IN_PAD = 1536
ODD_IN_PAD = 1024
ATT_BLK = 128
NEG = -1e30
V7X_VMEM_LIMIT = 48 * 1024 * 1024


def _params(*sem):
    return pltpu.CompilerParams(dimension_semantics=sem, vmem_limit_bytes=V7X_VMEM_LIMIT)


def _tile(n, cap):
    if n <= cap:
        return n
    t = cap - cap % 128
    while n % t:
        t -= 128
    return t


def _matmul(a, b, *, mode, out_dtype, name, epilogue=None, extra=None, alpha=1.0, after=None, tm=1024, tn=1024, tk=2048):
    if mode == "nn":
        M, K = a.shape
        P, _, Np = b.shape
        tm, tn, tk = _tile(M, tm), _tile(Np, tn), _tile(K, tk)
        nj = Np // tn
        grid = (M // tm, P * nj, K // tk)
        a_spec = pl.BlockSpec((tm, tk), lambda i, j, k: (i, k))
        b_spec = pl.BlockSpec((None, tk, tn), lambda i, j, k: (j // nj, k, j % nj))
        o_spec = pl.BlockSpec((tm, tn), lambda i, j, k: (i, j))
        out_shape = (M, P * Np)
        dims = (((1,), (0,)), ((), ()))
    elif mode == "nt":
        M, N = a.shape
        P, K, Np = b.shape
        tm, tn, tk = _tile(M, tm), _tile(K, tn), _tile(Np, tk)
        nkk = Np // tk
        grid = (M // tm, K // tn, P * nkk)
        a_spec = pl.BlockSpec((tm, tk), lambda i, j, k: (i, k))
        b_spec = pl.BlockSpec((None, tn, tk), lambda i, j, k: (k // nkk, j, k % nkk))
        o_spec = pl.BlockSpec((tm, tn), lambda i, j, k: (i, j))
        out_shape = (M, K)
        dims = (((1,), (1,)), ((), ()))
    else:
        raise ValueError(mode)
    n_k = grid[2]
    n_extra = 0 if extra is None else 1
    n_after = 0 if after is None else 1
    n_out = 2 if epilogue == "relu2" else 1

    def body(*refs):
        a_ref, b_ref = refs[:2]
        e_ref = refs[2] if n_extra else None
        outs = refs[2 + n_extra + n_after:2 + n_extra + n_after + n_out]
        part = lax.dot_general(a_ref[...], b_ref[...], dims, preferred_element_type=F32)

        def finish(r):
            if epilogue == "relu2":
                outs[0][...] = r.astype(outs[0].dtype)
                rp = jnp.maximum(r, 0.0)
                outs[1][...] = (rp * rp).astype(outs[1].dtype)
            elif epilogue == "drelu2":
                outs[0][...] = (r * (2.0 * jnp.maximum(e_ref[...].astype(F32), 0.0))).astype(outs[0].dtype)
            elif epilogue == "add":
                outs[0][...] = (r + alpha * e_ref[...].astype(F32)).astype(outs[0].dtype)
            else:
                outs[0][...] = r.astype(outs[0].dtype)

        if n_k == 1:
            finish(part)
        else:
            acc = refs[-1]
            k = pl.program_id(2)

            @pl.when(k == 0)
            def _():
                acc[...] = part

            @pl.when((k > 0) & (k < n_k - 1))
            def _():
                acc[...] += part

            @pl.when(k == n_k - 1)
            def _():
                finish(acc[...] + part)

    in_specs = [a_spec, b_spec] + ([o_spec] if n_extra else []) + ([pl.BlockSpec(memory_space=pl.ANY)] if n_after else [])
    shapes = [jax.ShapeDtypeStruct(out_shape, out_dtype)] * n_out
    res = pl.pallas_call(
        body, name=name, grid=grid, in_specs=in_specs,
        out_specs=[o_spec] * n_out, out_shape=shapes,
        scratch_shapes=[pltpu.VMEM((tm, tn), F32)] if n_k > 1 else [],
        compiler_params=_params("parallel", "parallel", "arbitrary"),
    )(a, b, *([extra] if n_extra else []), *([after] if n_after else []))
    return res if n_out > 1 else res[0]


def _matmul_tn(a, g, *, shards, name, tm=1024, tn=1024):
    M, K = a.shape
    P = shards
    Np = g.shape[1] // P
    tm, tn = _tile(K, tm), _tile(Np, tn)
    nj = Np // tn

    def body(a_ref, g_ref, o_ref):
        o_ref[...] = lax.dot_general(a_ref[...], g_ref[...], (((0,), (0,)), ((), ())), preferred_element_type=F32).astype(BF16)

    return pl.pallas_call(
        body, name=name, grid=(K // tm, P * nj),
        in_specs=[pl.BlockSpec((M, tm), lambda i, j: (0, i)), pl.BlockSpec((M, tn), lambda i, j: (0, j))],
        out_specs=pl.BlockSpec((None, tm, tn), lambda i, j: (j // nj, i, j % nj)),
        out_shape=jax.ShapeDtypeStruct((P, K, Np), BF16),
        compiler_params=_params("parallel", "parallel"),
    )(a, g)


def _norm_fwd(x, res, g, b, *, center, eps, alpha, name, rows=256):
    S, W = x.shape
    has_res = res is not None
    rows = _tile(S, rows)

    def body(*refs):
        x_ref = refs[0]
        r_ref = refs[1] if has_res else None
        g_ref = refs[1 + has_res]
        b_ref = refs[2 + has_res] if center else None
        y_ref, y16_ref, u_ref = refs[-3:]
        u = x_ref[...]
        if has_res:
            u = alpha * u + r_ref[...]
        if center:
            mu = jnp.mean(u, axis=1, keepdims=True)
            xc = u - mu
        else:
            xc = u
        var = jnp.mean(xc * xc, axis=1, keepdims=True)
        y = xc * lax.rsqrt(var + eps) * g_ref[...]
        if center:
            y = y + b_ref[...]
        y_ref[...] = y
        y16_ref[...] = y.astype(BF16)
        u_ref[...] = u

    row_spec = pl.BlockSpec((rows, W), lambda i: (i, 0))
    vec_spec = pl.BlockSpec((1, W), lambda i: (0, 0))
    ins = [x] + ([res] if has_res else []) + [g.reshape(1, W)] + ([b.reshape(1, W)] if center else [])
    specs = [row_spec] + ([row_spec] if has_res else []) + [vec_spec] + ([vec_spec] if center else [])
    return pl.pallas_call(
        body, name=name, grid=(S // rows,), in_specs=specs,
        out_specs=[row_spec, row_spec, row_spec],
        out_shape=[jax.ShapeDtypeStruct((S, W), F32), jax.ShapeDtypeStruct((S, W), BF16), jax.ShapeDtypeStruct((S, W), F32)],
        compiler_params=_params("parallel"),
    )(*ins)


def _norm_bwd(dy, u, g, *, center, eps, name, rows=256, after=None):
    S, W = u.shape
    rows = _tile(S, rows)

    def body(dy_ref, u_ref, g_ref, *rest):
        du_ref, du16_ref, dg_ref, db_ref = rest[-4:]
        i = pl.program_id(0)
        uu = u_ref[...]
        dyv = dy_ref[...]
        if center:
            xc = uu - jnp.mean(uu, axis=1, keepdims=True)
        else:
            xc = uu
        rstd = lax.rsqrt(jnp.mean(xc * xc, axis=1, keepdims=True) + eps)
        xhat = xc * rstd
        dxh = dyv * g_ref[...]
        c2 = jnp.mean(dxh * xhat, axis=1, keepdims=True)
        du = dxh - xhat * c2
        if center:
            du = du - jnp.mean(dxh, axis=1, keepdims=True)
        du = du * rstd
        du_ref[...] = du
        du16_ref[...] = du.astype(BF16)

        @pl.when(i == 0)
        def _():
            dg_ref[...] = jnp.zeros_like(dg_ref)
            db_ref[...] = jnp.zeros_like(db_ref)

        dg_ref[...] += jnp.sum(dyv * xhat, axis=0, keepdims=True)
        db_ref[...] += jnp.sum(dyv, axis=0, keepdims=True)

    row_spec = pl.BlockSpec((rows, W), lambda i: (i, 0))
    vec_spec = pl.BlockSpec((1, W), lambda i: (0, 0))
    return pl.pallas_call(
        body, name=name, grid=(S // rows,),
        in_specs=[row_spec, row_spec, vec_spec] + ([] if after is None else [pl.BlockSpec(memory_space=pl.ANY)]),
        out_specs=[row_spec, row_spec, vec_spec, vec_spec],
        out_shape=[jax.ShapeDtypeStruct((S, W), F32), jax.ShapeDtypeStruct((S, W), BF16),
                   jax.ShapeDtypeStruct((1, W), F32), jax.ShapeDtypeStruct((1, W), F32)],
        compiler_params=_params("arbitrary"),
    )(dy, u, g.reshape(1, W), *([] if after is None else [after]))


def _loss_head(y, target, *, name, rows=256):
    S, W = y.shape
    rows = _tile(S, rows)

    def body(y_ref, t_ref, dy_ref, l_ref):
        i = pl.program_id(0)
        e = y_ref[...] - t_ref[...]
        dy_ref[...] = e * (1.0 / W)

        @pl.when(i == 0)
        def _():
            l_ref[...] = jnp.zeros_like(l_ref)

        l_ref[...] += jnp.full(l_ref.shape, 0.5 / W * jnp.sum(e * e), F32)

    row_spec = pl.BlockSpec((rows, W), lambda i: (i, 0))
    return pl.pallas_call(
        body, name=name, grid=(S // rows,), in_specs=[row_spec, row_spec],
        out_specs=[row_spec, pl.BlockSpec((1, 128), lambda i: (0, 0))],
        out_shape=[jax.ShapeDtypeStruct((S, W), F32), jax.ShapeDtypeStruct((1, 128), F32)],
        compiler_params=_params("arbitrary"),
    )(y, target)


def _adamw(w, g, m, v, *, name, layer=0, carry=None, rows=256):
    L, R, C = w.shape
    rows = max(t for t in range(8, rows + 1, 8) if R % t == 0) if R % 8 == 0 else R
    c1 = 1.0 / (1.0 - ADAM_B1 ** ADAM_STEP)
    c2 = 1.0 / (1.0 - ADAM_B2 ** ADAM_STEP)

    def body(w_ref, g_ref, m_ref, v_ref, *rest):
        go_ref, d_ref, mo_ref, vo_ref = rest[-4:]
        gg = g_ref[...]
        mn = ADAM_B1 * m_ref[...] + (1.0 - ADAM_B1) * gg
        vn = ADAM_B2 * v_ref[...] + (1.0 - ADAM_B2) * (gg * gg)
        go_ref[...] = gg
        d_ref[...] = -ADAM_LR * ((mn * c1) / (jnp.sqrt(vn * c2) + ADAM_EPS) + ADAM_WD * w_ref[...])
        mo_ref[...] = mn
        vo_ref[...] = vn

    slab = pl.BlockSpec((None, rows, C), lambda i: (layer, i, 0))
    shp = jax.ShapeDtypeStruct((L, R, C), F32)
    carried = [] if carry is None else list(carry)
    return pl.pallas_call(
        body, name=name, grid=(R // rows,),
        in_specs=[slab, pl.BlockSpec((rows, C), lambda i: (i, 0)), slab, slab] + [pl.BlockSpec(memory_space=pl.ANY)] * len(carried),
        out_specs=[slab] * 4, out_shape=[shp] * 4, input_output_aliases={4 + k: k for k in range(len(carried))},
        compiler_params=_params("parallel"),
    )(w, g, m, v, *carried)


def _rope(x1, x2, cos, sin, *, name, rows=512):
    H, S, W = x1.shape
    rows = _tile(S, rows)

    def body(x1_ref, x2_ref, c_ref, s_ref, y1_ref, y2_ref):
        t1 = jnp.sum(x1_ref[...], axis=0)
        t2 = jnp.sum(x2_ref[...], axis=0)
        c = c_ref[...]
        s = s_ref[...]
        y1_ref[...] = t1 * c - t2 * s
        y2_ref[...] = t1 * s + t2 * c

    xs = pl.BlockSpec((H, rows, W), lambda i: (0, i, 0))
    ts = pl.BlockSpec((rows, W), lambda i: (i, 0))
    shp = jax.ShapeDtypeStruct((S, W), F32)
    return pl.pallas_call(
        body, name=name, grid=(S // rows,), in_specs=[xs, xs, ts, ts], out_specs=[ts, ts], out_shape=[shp, shp],
        compiler_params=_params("parallel"),
    )(x1, x2, cos, sin)


def _band_mask(blk, n, n_back):
    row = lax.broadcasted_iota(jnp.int32, (blk, 2 * blk), 0)
    col = lax.broadcasted_iota(jnp.int32, (blk, 2 * blk), 1)
    rel = blk + row - col
    return rel, (rel >= 0) & (rel <= n_back) & ((col >= blk) | (n >= 1))


def _window(ref, head, i, blk):
    before = pl.multiple_of(jnp.maximum(i - 1, 0) * blk, blk)
    return jnp.concatenate([ref[head, pl.ds(before, blk), :], ref[head, pl.ds(pl.multiple_of(i * blk, blk), blk), :]], axis=0)


def _window_add(ref, head, i, blk, update):
    before = pl.multiple_of(jnp.maximum(i - 1, 0) * blk, blk)
    ref[head, pl.ds(before, blk), :] += update[:blk]
    ref[head, pl.ds(pl.multiple_of(i * blk, blk), blk), :] += update[blk:]


def _band_fwd(q, k, v, slopes, sinks, *, hps, nb_seq, n_back, scale, dist_scale, name):
    blk = ATT_BLK
    Hq, T, d = q.shape
    Hk = k.shape[0]
    group = Hq // Hk
    kps = max(hps // group, 1)
    use_sink = sinks is not None

    def body(*refs):
        q_ref, k_ref, v_ref, slope_ref = refs[:4]
        sink_ref = refs[4] if use_sink else None
        o_ref, lse_ref = refs[-2:]
        i = pl.program_id(1)
        rel, valid = _band_mask(blk, i % nb_seq, n_back)
        relf = rel.astype(F32)
        for hh in range(hps):
            h = pl.program_id(0) * hps + hh
            kk = _window(k_ref, hh // group, i, blk)
            vv = _window(v_ref, hh // group, i, blk)
            s = lax.dot_general(q_ref[hh], kk, (((1,), (1,)), ((), ())), preferred_element_type=F32) * scale
            s = jnp.where(valid, s - (slope_ref[h] * dist_scale) * relf, NEG)
            m = jnp.max(s, axis=1, keepdims=True)
            if use_sink:
                m = jnp.maximum(m, sink_ref[h])
            p = jnp.where(valid, jnp.exp(s - m), 0.0)
            l = jnp.sum(p, axis=1, keepdims=True)
            if use_sink:
                l = l + jnp.exp(sink_ref[h] - m)
            o_ref[hh] = jnp.dot(p.astype(BF16), vv, preferred_element_type=F32) / l
            lse_ref[hh] = m + jnp.log(l)

    smem = pl.BlockSpec(memory_space=pltpu.SMEM)
    qs = pl.BlockSpec((hps, blk, d), lambda g, i: (g, i, 0))
    ks = pl.BlockSpec((kps, T, d), lambda g, i: (g, 0, 0))
    ins = [q, k, v, slopes] + ([sinks] if use_sink else [])
    return pl.pallas_call(
        body, name=name, grid=(Hq // hps, T // blk), in_specs=[qs, ks, ks, smem] + ([smem] if use_sink else []),
        out_specs=[qs, pl.BlockSpec((hps, blk, 1), lambda g, i: (g, i, 0))],
        out_shape=[jax.ShapeDtypeStruct((Hq, T, d), F32), jax.ShapeDtypeStruct((Hq, T, 1), F32)],
        compiler_params=_params("parallel", "parallel"),
    )(*ins)


def _band_bwd(q, k, v, o, do, lse, dlse, slopes, sinks, *, hps, nb_seq, n_back, scale, dist_scale, name):
    blk = ATT_BLK
    Hq, T, d = q.shape
    Hk = k.shape[0]
    group = Hq // Hk
    kps = max(hps // group, 1)
    use_sink, use_dlse = sinks is not None, dlse is not None

    def body(*refs):
        q_ref, k_ref, v_ref, o_ref, do_ref, lse_ref = refs[:6]
        pos = 6
        dlse_ref = refs[pos] if use_dlse else None
        pos += use_dlse
        slope_ref = refs[pos]
        pos += 1
        sink_ref = refs[pos] if use_sink else None
        pos += use_sink
        dq_ref, dk_ref, dv_ref = refs[pos:pos + 3]
        dsink_ref = refs[pos + 3] if use_sink else None
        i = pl.program_id(1)
        rel, valid = _band_mask(blk, i % nb_seq, n_back)
        relf = rel.astype(F32)

        @pl.when(i == 0)
        def _():
            dk_ref[...] = jnp.zeros_like(dk_ref)
            dv_ref[...] = jnp.zeros_like(dv_ref)
            if use_sink:
                dsink_ref[...] = jnp.zeros_like(dsink_ref)

        for kh in range(kps):
            dk_acc = jnp.zeros((2 * blk, d), F32)
            dv_acc = jnp.zeros((2 * blk, d), F32)
            kk = _window(k_ref, kh, i, blk)
            vv = _window(v_ref, kh, i, blk)
            for hh in range(kh * min(group, hps), (kh + 1) * min(group, hps)):
                h = pl.program_id(0) * hps + hh
                qb = q_ref[hh]
                dob = do_ref[hh]
                do16 = dob.astype(BF16)
                lse = lse_ref[hh]
                c = -jnp.sum(dob * o_ref[hh], axis=1, keepdims=True)
                if use_dlse:
                    c = c + dlse_ref[hh]
                s = lax.dot_general(qb, kk, (((1,), (1,)), ((), ())), preferred_element_type=F32) * scale
                s = jnp.where(valid, s - (slope_ref[h] * dist_scale) * relf, NEG)
                p = jnp.where(valid, jnp.exp(s - lse), 0.0)
                dp = lax.dot_general(do16, vv, (((1,), (1,)), ((), ())), preferred_element_type=F32)
                ds16 = (p * (dp + c) * scale).astype(BF16)
                dq_ref[hh] = jnp.dot(ds16, kk, preferred_element_type=F32)
                dk_acc = dk_acc + lax.dot_general(ds16, qb, (((0,), (0,)), ((), ())), preferred_element_type=F32)
                dv_acc = dv_acc + lax.dot_general(p.astype(BF16), do16, (((0,), (0,)), ((), ())), preferred_element_type=F32)
                if use_sink:
                    dsink_ref[hh] += jnp.full((1, 128), jnp.sum(jnp.exp(sink_ref[h] - lse) * c), F32)
            _window_add(dk_ref, kh, i, blk, dk_acc)
            _window_add(dv_ref, kh, i, blk, dv_acc)

    smem = pl.BlockSpec(memory_space=pltpu.SMEM)
    qs = pl.BlockSpec((hps, blk, d), lambda g, i: (g, i, 0))
    ls = pl.BlockSpec((hps, blk, 1), lambda g, i: (g, i, 0))
    ks = pl.BlockSpec((kps, T, d), lambda g, i: (g, 0, 0))
    in_specs = [qs, ks, ks, qs, qs, ls] + ([ls] if use_dlse else []) + [smem] + ([smem] if use_sink else [])
    ins = [q, k, v, o, do, lse] + ([dlse] if use_dlse else []) + [slopes] + ([sinks] if use_sink else [])
    out_specs = [qs, ks, ks]
    out_shape = [jax.ShapeDtypeStruct((Hq, T, d), F32), jax.ShapeDtypeStruct((Hk, T, d), F32),
                 jax.ShapeDtypeStruct((Hk, T, d), F32)]
    if use_sink:
        out_specs.append(pl.BlockSpec((hps, 1, 128), lambda g, i: (g, 0, 0)))
        out_shape.append(jax.ShapeDtypeStruct((Hq, 1, 128), F32))
    return pl.pallas_call(
        body, name=name, grid=(Hq // hps, T // blk), in_specs=in_specs, out_specs=out_specs, out_shape=out_shape,
        compiler_params=_params("parallel", "arbitrary"),
    )(*ins)


def _diagonal_mask(blk):
    return lax.broadcasted_iota(jnp.int32, (blk, blk), 0) >= lax.broadcasted_iota(jnp.int32, (blk, blk), 1)


def _causal_fwd(q, k, v, *, blk, hps, scale, name):
    H, T, dqk = q.shape
    dv = v.shape[2]

    def body(q_ref, k_ref, v_ref, o_ref, lse_ref):
        n = pl.program_id(1)
        qs = [q_ref[hh] for hh in range(hps)]

        def block(j, carry, valid):
            start = pl.multiple_of(j * blk, blk)
            out = []
            for hh in range(hps):
                m, l, acc = carry[hh]
                kb = k_ref[hh, pl.ds(start, blk), :]
                vb = v_ref[hh, pl.ds(start, blk), :]
                s = lax.dot_general(qs[hh], kb, (((1,), (1,)), ((), ())), preferred_element_type=F32) * scale
                if valid is not None:
                    s = jnp.where(valid, s, NEG)
                m_new = jnp.maximum(m, jnp.max(s, axis=1, keepdims=True))
                p = _keep(valid, jnp.exp(s - m_new))
                a = jnp.exp(m - m_new)
                out.append((m_new, a * l + jnp.sum(p, axis=1, keepdims=True),
                            a * acc + jnp.dot(p.astype(BF16), vb, preferred_element_type=F32)))
            return tuple(out)

        init = tuple((jnp.full((blk, 1), NEG, F32), jnp.zeros((blk, 1), F32), jnp.zeros((blk, dv), F32)) for _ in range(hps))
        res = block(n, lax.fori_loop(0, n, lambda j, carry: block(j, carry, None), init), _diagonal_mask(blk))
        for hh in range(hps):
            m, l, acc = res[hh]
            o_ref[hh] = acc / l
            lse_ref[hh] = m + jnp.log(l)

    qs_ = pl.BlockSpec((hps, blk, dqk), lambda g, i: (g, i, 0))
    return pl.pallas_call(
        body, name=name, grid=(H // hps, T // blk),
        in_specs=[qs_, pl.BlockSpec((hps, T, dqk), lambda g, i: (g, 0, 0)), pl.BlockSpec((hps, T, dv), lambda g, i: (g, 0, 0))],
        out_specs=[pl.BlockSpec((hps, blk, dv), lambda g, i: (g, i, 0)), pl.BlockSpec((hps, blk, 1), lambda g, i: (g, i, 0))],
        out_shape=[jax.ShapeDtypeStruct((H, T, dv), F32), jax.ShapeDtypeStruct((H, T, 1), F32)],
        compiler_params=_params("parallel", "parallel"),
    )(q, k, v)


def _causal_bwd(q, k, v, o, do, lse, *, blk, hps, scale, name):
    H, T, dqk = q.shape
    dv = v.shape[2]

    def body(q_ref, k_ref, v_ref, o_ref, do_ref, lse_ref, dq_ref, dk_ref, dv_ref):
        n = pl.program_id(1)

        @pl.when(n == 0)
        def _():
            dk_ref[...] = jnp.zeros_like(dk_ref)
            dv_ref[...] = jnp.zeros_like(dv_ref)

        qs = [q_ref[hh] for hh in range(hps)]
        do16 = [do_ref[hh].astype(BF16) for hh in range(hps)]
        lses = [lse_ref[hh] for hh in range(hps)]
        cs = [-jnp.sum(do_ref[hh] * o_ref[hh], axis=1, keepdims=True) for hh in range(hps)]

        def block(j, dqs, valid):
            start = pl.multiple_of(j * blk, blk)
            out = []
            for hh in range(hps):
                kb = k_ref[hh, pl.ds(start, blk), :]
                vb = v_ref[hh, pl.ds(start, blk), :]
                s = lax.dot_general(qs[hh], kb, (((1,), (1,)), ((), ())), preferred_element_type=F32) * scale
                if valid is not None:
                    s = jnp.where(valid, s, NEG)
                p = _keep(valid, jnp.exp(s - lses[hh]))
                dp = lax.dot_general(do16[hh], vb, (((1,), (1,)), ((), ())), preferred_element_type=F32)
                ds16 = (p * (dp + cs[hh]) * scale).astype(BF16)
                dk_ref[hh, pl.ds(start, blk), :] += lax.dot_general(ds16, qs[hh], (((0,), (0,)), ((), ())), preferred_element_type=F32)
                dv_ref[hh, pl.ds(start, blk), :] += lax.dot_general(p.astype(BF16), do16[hh], (((0,), (0,)), ((), ())),
                                                                    preferred_element_type=F32)
                out.append(dqs[hh] + jnp.dot(ds16, kb, preferred_element_type=F32))
            return tuple(out)

        init = tuple(jnp.zeros((blk, dqk), F32) for _ in range(hps))
        res = block(n, lax.fori_loop(0, n, lambda j, dqs: block(j, dqs, None), init), _diagonal_mask(blk))
        for hh in range(hps):
            dq_ref[hh] = res[hh]

    qs_ = pl.BlockSpec((hps, blk, dqk), lambda g, i: (g, i, 0))
    os_ = pl.BlockSpec((hps, blk, dv), lambda g, i: (g, i, 0))
    ls_ = pl.BlockSpec((hps, blk, 1), lambda g, i: (g, i, 0))
    ks_ = pl.BlockSpec((hps, T, dqk), lambda g, i: (g, 0, 0))
    vs_ = pl.BlockSpec((hps, T, dv), lambda g, i: (g, 0, 0))
    return pl.pallas_call(
        body, name=name, grid=(H // hps, T // blk), in_specs=[qs_, ks_, vs_, os_, os_, ls_], out_specs=[qs_, ks_, vs_],
        out_shape=[jax.ShapeDtypeStruct((H, T, dqk), F32), jax.ShapeDtypeStruct((H, T, dqk), F32),
                   jax.ShapeDtypeStruct((H, T, dv), F32)],
        compiler_params=_params("parallel", "arbitrary"),
    )(q, k, v, o, do, lse)


def _tri_sum(x, upper):
    hi = x.astype(BF16)
    lo = (x - hi.astype(F32)).astype(BF16)
    return jnp.dot(hi, upper, preferred_element_type=F32) + jnp.dot(lo, upper, preferred_element_type=F32)


def _keep(mask, x):
    return x if mask is None else jnp.where(mask, x, 0.0)


def _stick_terms(qb, kb, scale, strict):
    z = lax.dot_general(qb, kb, (((1,), (1,)), ((), ())), preferred_element_type=F32) * scale
    e = jnp.exp(-jnp.abs(z))
    lb = jnp.minimum(z, 0.0) - jnp.log(1.0 + e)
    return z, e, lb, _keep(strict, lb - z)


def _stick_fwd(q, k, v, *, blk, hps, scale, name):
    H, T, dh = q.shape

    def body(q_ref, k_ref, v_ref, o_ref, tot_ref):
        n = pl.program_id(1)
        qs = [q_ref[hh] for hh in range(hps)]
        r_i = lax.broadcasted_iota(jnp.int32, (blk, blk), 0)
        c_i = lax.broadcasted_iota(jnp.int32, (blk, blk), 1)
        upper = (r_i > c_i).astype(BF16)

        def block(j, carry, strict):
            start = pl.multiple_of(j * blk, blk)
            out = []
            for hh in range(hps):
                run, acc = carry[hh]
                kb = k_ref[hh, pl.ds(start, blk), :]
                vb = v_ref[hh, pl.ds(start, blk), :]
                _, _, lb, lk = _stick_terms(qs[hh], kb, scale, strict)
                w = _keep(strict, jnp.exp(lb + run + _tri_sum(lk, upper)))
                out.append((run + jnp.sum(lk, axis=1, keepdims=True), acc + jnp.dot(w.astype(BF16), vb, preferred_element_type=F32)))
            return tuple(out)

        init = tuple((jnp.zeros((blk, 1), F32), jnp.zeros((blk, dh), F32)) for _ in range(hps))
        res = lax.fori_loop(1, n + 1, lambda t, carry: block(n - t, carry, None), block(n, init, r_i > c_i))
        for hh in range(hps):
            tot_ref[hh], o_ref[hh] = res[hh]

    qs_ = pl.BlockSpec((hps, blk, dh), lambda g, i: (g, i, 0))
    ks_ = pl.BlockSpec((hps, T, dh), lambda g, i: (g, 0, 0))
    ts_ = pl.BlockSpec((hps, blk, 1), lambda g, i: (g, i, 0))
    return pl.pallas_call(
        body, name=name, grid=(H // hps, T // blk), in_specs=[qs_, ks_, ks_], out_specs=[qs_, ts_],
        out_shape=[jax.ShapeDtypeStruct((H, T, dh), F32), jax.ShapeDtypeStruct((H, T, 1), F32)],
        compiler_params=_params("parallel", "parallel"),
    )(q, k, v)


def _stick_bwd(q, k, v, tot, do, *, blk, hps, scale, name):
    H, T, dh = q.shape

    def body(q_ref, k_ref, v_ref, tot_ref, do_ref, dq_ref, dk_ref, dv_ref):
        n = pl.program_id(1)
        qs = [q_ref[hh] for hh in range(hps)]
        do16 = [do_ref[hh].astype(BF16) for hh in range(hps)]
        tots = [tot_ref[hh] for hh in range(hps)]
        r_i = lax.broadcasted_iota(jnp.int32, (blk, blk), 0)
        c_i = lax.broadcasted_iota(jnp.int32, (blk, blk), 1)
        upto = (r_i <= c_i).astype(BF16)
        below = (r_i < c_i).astype(BF16)

        @pl.when(n == 0)
        def _():
            dk_ref[...] = jnp.zeros_like(dk_ref)
            dv_ref[...] = jnp.zeros_like(dv_ref)

        def block(j, carry, strict):
            start = pl.multiple_of(j * blk, blk)
            out = []
            for hh in range(hps):
                run, grun, dq = carry[hh]
                kb = k_ref[hh, pl.ds(start, blk), :]
                vb = v_ref[hh, pl.ds(start, blk), :]
                z, e, lb, lk = _stick_terms(qs[hh], kb, scale, strict)
                w = _keep(strict, jnp.exp(lb + tots[hh] - (run + _tri_sum(lk, upto))))
                dw = lax.dot_general(do16[hh], vb, (((1,), (1,)), ((), ())), preferred_element_type=F32)
                g = w * dw
                before = grun + _tri_sum(g, below)
                inv = 1.0 / (1.0 + e)
                sig = jnp.where(z >= 0, inv, e * inv)
                dz16 = (_keep(strict, g * (1.0 - sig) - sig * before) * scale).astype(BF16)
                dk_ref[hh, pl.ds(start, blk), :] += lax.dot_general(dz16, qs[hh], (((0,), (0,)), ((), ())), preferred_element_type=F32)
                dv_ref[hh, pl.ds(start, blk), :] += lax.dot_general(w.astype(BF16), do16[hh], (((0,), (0,)), ((), ())),
                                                                    preferred_element_type=F32)
                out.append((run + jnp.sum(lk, axis=1, keepdims=True), grun + jnp.sum(g, axis=1, keepdims=True),
                            dq + jnp.dot(dz16, kb, preferred_element_type=F32)))
            return tuple(out)

        zero = jnp.zeros((blk, 1), F32)
        init = tuple((zero, zero, jnp.zeros((blk, dh), F32)) for _ in range(hps))
        res = block(n, lax.fori_loop(0, n, lambda j, carry: block(j, carry, None), init), r_i > c_i)
        for hh in range(hps):
            dq_ref[hh] = res[hh][2]

    qs_ = pl.BlockSpec((hps, blk, dh), lambda g, i: (g, i, 0))
    ks_ = pl.BlockSpec((hps, T, dh), lambda g, i: (g, 0, 0))
    ts_ = pl.BlockSpec((hps, blk, 1), lambda g, i: (g, i, 0))
    shp = jax.ShapeDtypeStruct((H, T, dh), F32)
    return pl.pallas_call(
        body, name=name, grid=(H // hps, T // blk), in_specs=[qs_, ks_, ks_, ts_, qs_], out_specs=[qs_, ks_, ks_],
        out_shape=[shp, shp, shp],
        compiler_params=_params("parallel", "arbitrary"),
    )(q, k, v, tot, do)


def _mix_fwd(outs, lses, *, name, rows=512):
    H, T, dh = outs[0].shape
    rows = _tile(T, rows)

    def body(o0, o1, o2, l0, l1, l2, y_ref):
        ls = [l0[...], l1[...], l2[...]]
        mx = jnp.maximum(jnp.maximum(ls[0], ls[1]), ls[2])
        es = [jnp.exp(x - mx) for x in ls]
        den = es[0] + es[1] + es[2]
        y_ref[...] = (es[0] * o0[...] + es[1] * o1[...] + es[2] * o2[...]) / den

    os_ = pl.BlockSpec((None, rows, dh), lambda h, i: (h, i, 0))
    ls_ = pl.BlockSpec((None, rows, 1), lambda h, i: (h, i, 0))
    return pl.pallas_call(
        body, name=name, grid=(H, T // rows), in_specs=[os_] * 3 + [ls_] * 3, out_specs=os_,
        out_shape=jax.ShapeDtypeStruct((H, T, dh), F32),
        compiler_params=_params("parallel", "parallel"),
    )(*outs, *lses)


def _mix_bwd(outs, lses, dy, *, name, rows=512):
    H, T, dh = outs[0].shape
    rows = _tile(T, rows)

    def body(o0, o1, o2, l0, l1, l2, dy_ref, d0, d1, d2, g0, g1, g2):
        ls = [l0[...], l1[...], l2[...]]
        mx = jnp.maximum(jnp.maximum(ls[0], ls[1]), ls[2])
        es = [jnp.exp(x - mx) for x in ls]
        den = es[0] + es[1] + es[2]
        mix = [e / den for e in es]
        dyv = dy_ref[...]
        dm = [jnp.sum(dyv * o[...], axis=1, keepdims=True) for o in (o0, o1, o2)]
        avg = mix[0] * dm[0] + mix[1] * dm[1] + mix[2] * dm[2]
        for d_ref, g_ref, w, t in zip((d0, d1, d2), (g0, g1, g2), mix, dm):
            d_ref[...] = w * dyv
            g_ref[...] = w * (t - avg)

    os_ = pl.BlockSpec((None, rows, dh), lambda h, i: (h, i, 0))
    ls_ = pl.BlockSpec((None, rows, 1), lambda h, i: (h, i, 0))
    so = jax.ShapeDtypeStruct((H, T, dh), F32)
    sl = jax.ShapeDtypeStruct((H, T, 1), F32)
    res = pl.pallas_call(
        body, name=name, grid=(H, T // rows), in_specs=[os_] * 3 + [ls_] * 3 + [os_], out_specs=[os_] * 3 + [ls_] * 3,
        out_shape=[so] * 3 + [sl] * 3,
        compiler_params=_params("parallel", "parallel"),
    )(*outs, *lses, dy)
    return res[:3], res[3:]


def _position():
    return lax.axis_index("x"), lax.axis_index("y"), lax.axis_index("c")


def _other_chips(x, y):
    return [(1 - x, y), (x, 1 - y), (1 - x, 1 - y)]


HBM_SPEC = pl.BlockSpec(memory_space=pltpu.HBM)
SEM_SPEC = pl.BlockSpec(memory_space=pltpu.SEMAPHORE)
ANY_SPEC = pl.BlockSpec(memory_space=pl.ANY)
SPLIT_COPY = pltpu.CompilerParams(has_side_effects=pltpu.SideEffectType.DATAFLOW_SIDE_EFFECTING)


def _in_hbm(a):
    return pltpu.with_memory_space_constraint(a, pltpu.HBM)


SAME_CORE_OF_OTHER_CHIPS = ((1, 0, 0), (0, 1, 0), (1, 1, 0))
ALL_OTHER_DEVICES = ((0, 0, 1), (1, 0, 0), (0, 1, 0), (1, 1, 0), (1, 0, 1), (0, 1, 1), (1, 1, 1))


def _peer_copies(srcs, lands, send_sems, recv_sems, relations, src_of, dst_of):
    me = _position()
    copies = []
    for w in range(len(srcs)):
        for k, flips in enumerate(relations):
            peer = tuple(1 - p if f else p for p, f in zip(me, flips))
            i = len(relations) * w + k
            copies.append(pltpu.make_async_remote_copy(
                src_ref=src_of(srcs[w], peer), dst_ref=dst_of(lands[w], k, peer, me), send_sem=send_sems[i],
                recv_sem=recv_sems[i], device_id=peer, device_id_type=MESH))
    return copies


def _copies_start(srcs, land_shapes, after, relations, src_of, dst_of, *, name):
    n = len(srcs)
    m = len(relations) * n

    def body(*refs):
        src_refs, land_refs = refs[:n], refs[n:2 * n]
        send_sems, recv_sems = refs[2 * n + 1:2 * n + 1 + m], refs[2 * n + 1 + m:2 * n + 1 + 2 * m]
        token = refs[-1]
        for cp in _peer_copies(src_refs, land_refs, send_sems, recv_sems, relations, src_of, dst_of):
            cp.start()
        token[...] = jnp.zeros_like(token)

    lands = [_in_hbm(lax.empty(s.shape, s.dtype)) for s in land_shapes]
    out_shape = ([pltpu.SemaphoreType.DMA(())] * (2 * m)
                 + [pltpu.HBM(a.shape, a.dtype) for a in srcs] + [pltpu.HBM(s.shape, s.dtype) for s in land_shapes]
                 + [jax.ShapeDtypeStruct((8, 128), F32)])
    res = pl.pallas_call(
        body, name=name, in_specs=[HBM_SPEC] * (2 * n) + [ANY_SPEC],
        out_specs=[SEM_SPEC] * (2 * m) + [HBM_SPEC] * (2 * n) + [pl.BlockSpec(memory_space=pltpu.VMEM)],
        out_shape=out_shape, input_output_aliases={i: 2 * m + i for i in range(2 * n)}, compiler_params=SPLIT_COPY,
    )(*[_in_hbm(a) for a in srcs], *lands, after)
    return (list(res[:m]), list(res[m:2 * m]), list(res[2 * m:2 * m + n]), list(res[2 * m + n:2 * m + 2 * n]), res[-1])


def _copies_wait(started, after, relations, src_of, dst_of, *, name):
    send_sems, recv_sems, srcs, lands, _ = started
    n = len(srcs)
    m = len(relations) * n

    def body(*refs):
        src_refs, land_refs = refs[:n], refs[n:2 * n]
        send_refs, recv_refs = refs[2 * n:2 * n + m], refs[2 * n + m:2 * n + 2 * m]
        for cp in _peer_copies(src_refs, land_refs, send_refs, recv_refs, relations, src_of, dst_of):
            cp.wait_send()
            cp.wait_recv()

    res = pl.pallas_call(
        body, name=name, in_specs=[HBM_SPEC] * (2 * n) + [SEM_SPEC] * (2 * m) + [ANY_SPEC], out_specs=[HBM_SPEC] * (2 * n),
        out_shape=[pltpu.HBM(a.shape, a.dtype) for a in srcs + lands],
        input_output_aliases={i: i for i in range(2 * n)}, compiler_params=SPLIT_COPY,
    )(*srcs, *lands, *send_sems, *recv_sems, after)
    return list(res[:n]), list(res[n:])


def _half_rows(ref, core):
    half = ref.shape[-2] // 2
    return pl.ds(core * half, half)


def _gather_src(src, peer):
    return src.at[_half_rows(src, peer[2]), :]


def _gather_dst(land, k, peer, me):
    return land.at[2 * me[0] + me[1], _half_rows(land, me[2]), :]


def _gather_landed(land, k, peer, me):
    return land.at[2 * peer[0] + peer[1], _half_rows(land, me[2]), :]


def _exchange_src(src, peer):
    return src.at[2 * peer[0] + peer[1], _half_rows(src, peer[2]), :]


def _exchange_dst(land, k, peer, me):
    return land.at[k]


def _small_src(src, peer):
    return src


def _small_dst(land, k, peer, me):
    return land.at[4 * me[0] + 2 * me[1] + me[2]]


def _small_landed(land, k, peer, me):
    return land.at[4 * peer[0] + 2 * peer[1] + peer[2]]


def _forward_halves(bufs, *, name):
    n = len(bufs)

    def body(*refs):
        ins, outs = refs[:n], refs[n:2 * n]
        send_sems, recv_sems = refs[2 * n:]
        x, y, c = _position()
        chips = _other_chips(x, y)

        def copy(w, j, chip, core):
            holder = (x, y, core)
            return pltpu.make_async_remote_copy(src_ref=_gather_landed(ins[w], j, chip, holder),
                                                dst_ref=_gather_landed(outs[w], j, chip, holder),
                                                send_sem=send_sems.at[w, j], recv_sem=recv_sems.at[w, j],
                                                device_id=(x, y, 1 - c), device_id_type=MESH)

        sends = [copy(w, j, chip, c) for w in range(n) for j, chip in enumerate(chips)]
        for cp in sends:
            cp.start()
        for w in range(n):
            for j, chip in enumerate(chips):
                copy(w, j, chip, 1 - c).wait_recv()
        for cp in sends:
            cp.wait_send()

    return pl.pallas_call(
        body, name=name, in_specs=[ANY_SPEC] * n, out_specs=[ANY_SPEC] * n,
        out_shape=[jax.ShapeDtypeStruct(a.shape, a.dtype) for a in bufs], input_output_aliases={w: w for w in range(n)},
        scratch_shapes=[pltpu.SemaphoreType.DMA((n, 3)), pltpu.SemaphoreType.DMA((n, 3))],
    )(*bufs)


def _share_halves(bufs, *, name):
    n = len(bufs)

    def body(*refs):
        ins, outs = refs[:n], refs[n:2 * n]
        send_sems, recv_sems = refs[2 * n:]
        x, y, c = _position()
        sends = []
        for w in range(n):
            sends.append(pltpu.make_async_remote_copy(src_ref=ins[w].at[c], dst_ref=outs[w].at[c], send_sem=send_sems.at[w],
                                                      recv_sem=recv_sems.at[w], device_id=(x, y, 1 - c), device_id_type=MESH))
            sends[-1].start()
        for w in range(n):
            pltpu.make_async_remote_copy(src_ref=ins[w].at[1 - c], dst_ref=outs[w].at[1 - c], send_sem=send_sems.at[w],
                                         recv_sem=recv_sems.at[w], device_id=(x, y, 1 - c), device_id_type=MESH).wait_recv()
        for cp in sends:
            cp.wait_send()

    hbm = pl.BlockSpec(memory_space=pl.ANY)
    return pl.pallas_call(
        body, name=name, in_specs=[hbm] * n, out_specs=[hbm] * n,
        out_shape=[jax.ShapeDtypeStruct(a.shape, a.dtype) for a in bufs],
        input_output_aliases={w: w for w in range(n)},
        scratch_shapes=[pltpu.SemaphoreType.DMA((n,)), pltpu.SemaphoreType.DMA((n,))],
    )(*bufs)


def _all_gather8(v, *, name):
    m, n = v.shape

    def body(v_ref, out_ref, send_sems, recv_sems, local_sem):
        x, y, c = _position()
        me, sibling = (x, y, c), (x, y, 1 - c)
        chips = _other_chips(x, y)

        def rows(px, py, pc):
            return out_ref.at[pl.ds((4 * px + 2 * py + pc) * m, m), :]

        def copy(k, block, to, src=None):
            return pltpu.make_async_remote_copy(src_ref=rows(*block) if src is None else src, dst_ref=rows(*block),
                                                send_sem=send_sems.at[k], recv_sem=recv_sems.at[k], device_id=to, device_id_type=MESH)

        mine = pltpu.make_async_copy(v_ref, rows(*me), local_sem)
        mine.start()
        first = [copy(0, me, sibling, src=v_ref)]
        first += [copy(1 + j, me, (*chip, c), src=v_ref) for j, chip in enumerate(chips)]
        for cp in first:
            cp.start()
        passed = [copy(4 + j, (*chip, c), sibling) for j, chip in enumerate(chips)]
        for j, chip in enumerate(chips):
            copy(1 + j, (*chip, c), me).wait_recv()
            passed[j].start()
        copy(0, sibling, me).wait_recv()
        for j, chip in enumerate(chips):
            copy(4 + j, (*chip, 1 - c), me).wait_recv()
        for cp in first + passed:
            cp.wait_send()
        mine.wait()

    vmem = pl.BlockSpec(memory_space=pltpu.VMEM)
    return pl.pallas_call(
        body, name=name, in_specs=[vmem], out_specs=vmem, out_shape=jax.ShapeDtypeStruct((8 * m, n), v.dtype),
        scratch_shapes=[pltpu.SemaphoreType.DMA((7,)), pltpu.SemaphoreType.DMA((7,)), pltpu.SemaphoreType.DMA],
    )(v)


def _sum_parts(own, landed, where, *, name, rows=256):
    n_peers, half, C = landed.shape
    rows = _tile(half, rows)
    nb = half // rows

    def body(where_ref, own_ref, land_ref, o_ref):
        acc = own_ref[...].astype(F32)
        for k in range(n_peers):
            acc = acc + land_ref[k].astype(F32)
        o_ref[...] = acc

    grid_spec = pltpu.PrefetchScalarGridSpec(
        num_scalar_prefetch=1, grid=(nb,),
        in_specs=[pl.BlockSpec((None, rows, C), lambda i, where_ref: (where_ref[0], where_ref[1] * nb + i, 0)),
                  pl.BlockSpec((n_peers, rows, C), lambda i, where_ref: (0, i, 0))],
        out_specs=pl.BlockSpec((None, rows, C), lambda i, where_ref: (where_ref[1], i, 0)))
    return pl.pallas_call(
        body, name=name, grid_spec=grid_spec, out_shape=jax.ShapeDtypeStruct((2, half, C), F32),
        compiler_params=_params("parallel"),
    )(where, own, landed)


def _sum_slabs(a, *, name, rows=256):
    P, R, C = a.shape
    rows = _tile(R, rows)

    def body(a_ref, o_ref):
        acc = a_ref[0].astype(F32)
        for p in range(1, P):
            acc = acc + a_ref[p].astype(F32)
        o_ref[...] = acc

    return pl.pallas_call(
        body, name=name, grid=(R // rows,), in_specs=[pl.BlockSpec((P, rows, C), lambda i: (0, i, 0))],
        out_specs=pl.BlockSpec((rows, C), lambda i: (i, 0)),
        out_shape=jax.ShapeDtypeStruct((R, C), F32), compiler_params=_params("parallel"),
    )(a)


def _reduce_scatter_start(grads, after=None, *, name):
    lands = [jax.ShapeDtypeStruct((len(ALL_OTHER_DEVICES), g.shape[1] // 2, g.shape[2]), g.dtype) for g in grads]
    return _copies_start(grads, lands, grads[0] if after is None else after, ALL_OTHER_DEVICES, _exchange_src, _exchange_dst,
                         name=name + "_start")


def _reduce_scatter_finish(started, after, *, name):
    core = lax.axis_index("c").astype(jnp.int32)
    chip = (2 * lax.axis_index("x") + lax.axis_index("y")).astype(jnp.int32)
    where = jnp.stack([chip, core])
    sums, landed = _copies_wait(started, after, ALL_OTHER_DEVICES, _exchange_src, _exchange_dst, name=name + "_wait")
    reduced = [_sum_parts(s, a, where, name=f"{name}_sum{w}") for w, (s, a) in enumerate(zip(sums, landed))]
    both = _share_halves(reduced, name=name + "_share")
    return [b.reshape(2 * b.shape[1], b.shape[2]) for b in both]


def _to_heads(t, heads, dil=1):
    S = t.shape[0]
    d = t.shape[1] // heads
    return jnp.transpose(t.reshape(S // dil, dil, heads, d), (2, 1, 0, 3)).reshape(heads, S, d)


def _from_heads(t, dil=1):
    heads, S, d = t.shape
    return jnp.transpose(t.reshape(heads, dil, S // dil, d), (2, 1, 0, 3)).reshape(S, heads * d)


def _unstride(t, dil):
    heads, S, d = t.shape
    return jnp.transpose(t.reshape(heads, dil, S // dil, d), (0, 2, 1, 3)).reshape(heads, S, d)


def _restride(t, dil):
    heads, S, d = t.shape
    return jnp.transpose(t.reshape(heads, S // dil, dil, d), (0, 2, 1, 3)).reshape(heads, S, d)


def _pad_cols(t, width, padded):
    S = t.shape[0]
    return jnp.pad(t.reshape(S, N_CHIPS, width), ((0, 0), (0, 0), (0, padded - width))).reshape(S, N_CHIPS * padded)


def _column_reader(t, width, padded):
    def take(lo, hi):
        pieces = []
        for p in range(N_CHIPS):
            a, b = max(lo, p * width), min(hi, (p + 1) * width)
            if a < b:
                pieces.append(t[:, p * padded + a - p * width:p * padded + b - p * width])
        return pieces[0] if len(pieces) == 1 else jnp.concatenate(pieces, axis=-1)
    return take


def _alibi(n):
    return 2.0 ** (-8.0 * jnp.arange(1, n + 1, dtype=F32) / n)


B_KW = [dict(hps=4, nb_seq=SEQ // d // ATT_BLK, n_back=w // d, scale=1.0 / math.sqrt(HEAD_DIM), dist_scale=d)
        for w, d in B_PATTERNS]
A_KW = dict(hps=A_Q_HEADS // A_KV_HEADS, nb_seq=SEQ // ATT_BLK, n_back=A_WINDOW - 1, scale=1.0 / math.sqrt(HEAD_DIM), dist_scale=1)
D_KW = dict(blk=256, hps=2, scale=1.0 / math.sqrt(D_NOPE + D_ROPE))
C_KW = dict(blk=256, hps=2, scale=1.0 / math.sqrt(HEAD_DIM))


def _rope_tables(reps):
    inv_freq = ROPE_BASE ** (-jnp.arange(0, D_ROPE, 2, dtype=F32) / D_ROPE)
    ang = jnp.arange(SEQ, dtype=F32)[:, None] * inv_freq[None, :]
    return jnp.tile(jnp.cos(ang), (1, reps)), jnp.tile(jnp.sin(ang), (1, reps))


def _mlp_fwd(x, x16, w1, w2, g, b, tag):
    hid, act = _matmul(x16, w1, mode="nn", out_dtype=BF16, epilogue="relu2", name=f"mlp{tag}_up")
    m = _matmul(act, w2, mode="nn", out_dtype=F32, name=f"mlp{tag}_down")
    y, y16, u = _norm_fwd(x, m, g, b, center=True, eps=LN_EPS, alpha=ALPHA, name=f"ln2_{tag}")
    return y, y16, (x16, hid, act, u)


def _mlp_bwd(dy, saved, w1, w2, g, tag, after=None):
    x16, hid, act, u = saved
    du, du16, dg, db = _norm_bwd(dy, u, g, center=True, eps=LN_EPS, name=f"ln2_{tag}_bwd", after=after)
    dw2 = _matmul_tn(act, du16, shards=1, name=f"mlp{tag}_dw2")
    dhid = _matmul(du16, w2, mode="nt", out_dtype=BF16, epilogue="drelu2", extra=hid, name=f"mlp{tag}_dact")
    dw1 = _matmul_tn(x16, dhid, shards=N_CHIPS, name=f"mlp{tag}_dw1")
    dx = _matmul(dhid, w1, mode="nt", out_dtype=F32, epilogue="add", extra=du, alpha=ALPHA, name=f"mlp{tag}_dx")
    return dx, dw1, dw2, dg, db


def _even_fwd(x16, w_in, sinks):
    h = _column_reader(_matmul(x16, w_in, mode="nn", out_dtype=BF16, name="even_in"), EVEN_IN // N_CHIPS, EVEN_IN_PAD)
    qa = _to_heads(h(0, A_Q_W), A_Q_HEADS)
    ka = _to_heads(h(A_Q_W, A_Q_W + A_KV_W), A_KV_HEADS)
    va = _to_heads(h(A_Q_W + A_KV_W, A_Q_W + 2 * A_KV_W), A_KV_HEADS)
    slopes_a, slopes_b = _alibi(A_Q_HEADS), _alibi(B_HEADS)
    oa, lse_a = _band_fwd(qa, ka, va, slopes_a, sinks, name="swa_fwd", **A_KW)
    qkv_b, outs, lses = [], [], []
    for gi, (_, dil) in enumerate(B_PATTERNS):
        base = A_Q_W + 2 * A_KV_W + gi * 3 * B_W
        q, k, v = (_to_heads(h(base + i * B_W, base + (i + 1) * B_W), B_HEADS, dil) for i in range(3))
        o, lse = _band_fwd(q, k, v, slopes_b, None, name=f"dil{gi}_fwd", **B_KW[gi])
        qkv_b.append((q, k, v, o, lse))
        outs.append(_unstride(o, dil))
        lses.append(_unstride(lse, dil))
    ob = _mix_fwd(outs, lses, name="dil_mix")
    y16 = jnp.concatenate([_from_heads(oa), _from_heads(ob)], axis=-1).astype(BF16)
    return y16, (x16, qa, ka, va, oa, lse_a, qkv_b, outs, lses, y16)


def _even_bwd(dmix16, du, saved, w_in, sinks, w_out, send_w_out, send_w_in):
    x16, qa, ka, va, oa, lse_a, qkv_b, outs, lses, y16 = saved
    slopes_a, slopes_b = _alibi(A_Q_HEADS), _alibi(B_HEADS)
    sent = send_w_out(_matmul_tn(y16, dmix16, shards=N_CHIPS, name="even_dwout"))
    dy = _matmul(dmix16, w_out, mode="nt", out_dtype=F32, name="even_dy", after=sent)
    doa = _to_heads(dy[:, :A_Q_W], A_Q_HEADS)
    dob = _to_heads(dy[:, A_Q_W:], B_HEADS)
    dqa, dka, dva, dsink = _band_bwd(qa, ka, va, oa, doa, lse_a, None, slopes_a, sinks, name="swa_bwd", **A_KW)
    douts, dlses = _mix_bwd(outs, lses, dob, name="dil_mix_bwd")
    parts = [_from_heads(dqa), _from_heads(dka), _from_heads(dva)]
    for gi, (_, dil) in enumerate(B_PATTERNS):
        q, k, v, o, lse = qkv_b[gi]
        dq, dk, dv = _band_bwd(q, k, v, o, _restride(douts[gi], dil), lse, _restride(dlses[gi], dil), slopes_b, None,
                               name=f"dil{gi}_bwd", **B_KW[gi])
        parts += [_from_heads(dq, dil), _from_heads(dk, dil), _from_heads(dv, dil)]
    dh16 = _pad_cols(jnp.concatenate(parts, axis=-1), EVEN_IN // N_CHIPS, EVEN_IN_PAD).astype(BF16)
    sent = send_w_in(_matmul_tn(x16, dh16, shards=N_CHIPS, name="even_dwin"), dsink[:, 0, 0])
    return _matmul(dh16, w_in, mode="nt", out_dtype=F32, epilogue="add", extra=du, alpha=ALPHA, name="even_dx", after=sent)


def _odd_fwd(x16, w_in, qn_g, kvn_g, w_uq, w_ukv, w_out):
    S = x16.shape[0]
    h = _column_reader(_matmul(x16, w_in, mode="nn", out_dtype=F32, name="odd_in"), ODD_IN // N_CHIPS, ODD_IN_PAD)
    qc, kc, vc = (_to_heads(h(i * C_W, (i + 1) * C_W).astype(BF16), C_HEADS) for i in range(3))
    cq = h(3 * C_W, 3 * C_W + D_Q_RANK)
    ckv = h(3 * C_W + D_Q_RANK, 3 * C_W + D_Q_RANK + D_KV_RANK)
    kr = h(3 * C_W + D_Q_RANK + D_KV_RANK, ODD_IN)
    oc, tot = _stick_fwd(qc, kc, vc, name="stick_fwd", **C_KW)
    _, cqn16, _ = _norm_fwd(cq, None, qn_g, None, center=False, eps=RMS_EPS, alpha=1.0, name="q_rms")
    _, ckvn16, _ = _norm_fwd(ckv, None, kvn_g, None, center=False, eps=RMS_EPS, alpha=1.0, name="kv_rms")
    q = _matmul(cqn16, w_uq, mode="nn", out_dtype=F32, name="mla_uq").reshape(S, D_HEADS, D_NOPE + D_ROPE)
    kv = _matmul(ckvn16, w_ukv, mode="nn", out_dtype=F32, name="mla_ukv").reshape(S, D_HEADS, D_NOPE + D_V)
    half = D_ROPE // 2
    q_rope = q[..., D_NOPE:]
    x1 = jnp.concatenate([q_rope[..., :half].reshape(S, D_HEADS * half), kr[:, :half]], axis=-1)
    x2 = jnp.concatenate([q_rope[..., half:].reshape(S, D_HEADS * half), kr[:, half:]], axis=-1)
    cos, sin = _rope_tables(D_HEADS + 1)
    y1, y2 = _rope(x1[None], x2[None], cos, sin, name="rope_fwd")
    nq = D_HEADS * half
    q_rot = jnp.concatenate([y1[:, :nq].reshape(S, D_HEADS, half), y2[:, :nq].reshape(S, D_HEADS, half)], axis=-1)
    kr_rot = jnp.concatenate([y1[:, nq:], y2[:, nq:]], axis=-1)
    qd = jnp.transpose(jnp.concatenate([q[..., :D_NOPE], q_rot], axis=-1), (1, 0, 2)).astype(BF16)
    kd = jnp.transpose(jnp.concatenate([kv[..., :D_NOPE], jnp.broadcast_to(kr_rot[:, None, :], (S, D_HEADS, D_ROPE))], axis=-1),
                       (1, 0, 2)).astype(BF16)
    vd = jnp.transpose(kv[..., D_NOPE:], (1, 0, 2)).astype(BF16)
    od, lse_d = _causal_fwd(qd, kd, vd, name="mla_fwd", **D_KW)
    y16 = jnp.concatenate([_from_heads(oc), _from_heads(od)], axis=-1).astype(BF16)
    mixed = _matmul(y16, w_out, mode="nn", out_dtype=F32, name="odd_out")
    return mixed, (x16, qc, kc, vc, tot, cq, ckv, cqn16, ckvn16, qd, kd, vd, od, lse_d, y16)


def _odd_bwd(dmix16, du, saved, w_in, qn_g, kvn_g, w_uq, w_ukv, w_out):
    x16, qc, kc, vc, tot, cq, ckv, cqn16, ckvn16, qd, kd, vd, od, lse_d, y16 = saved
    S = x16.shape[0]
    half = D_ROPE // 2
    dw_out = _matmul_tn(y16, dmix16, shards=1, name="odd_dwout")
    dy = _matmul(dmix16, w_out, mode="nt", out_dtype=F32, name="odd_dy")
    doc = _to_heads(dy[:, :C_W], C_HEADS)
    dod = _to_heads(dy[:, C_W:], D_HEADS)
    dqc, dkc, dvc = _stick_bwd(qc, kc, vc, tot, doc, name="stick_bwd", **C_KW)
    dqd, dkd, dvd = _causal_bwd(qd, kd, vd, od, dod, lse_d, name="mla_bwd", **D_KW)
    cos, sin = _rope_tables(D_HEADS + 1)
    nq = D_HEADS * half
    gq1 = jnp.transpose(dqd[..., D_NOPE:D_NOPE + half], (1, 0, 2)).reshape(1, S, nq)
    gq2 = jnp.transpose(dqd[..., D_NOPE + half:], (1, 0, 2)).reshape(1, S, nq)
    dq1, dq2 = _rope(gq1, gq2, cos[:, :nq], -sin[:, :nq], name="rope_bwd_q")
    dk1, dk2 = _rope(dkd[..., D_NOPE:D_NOPE + half], dkd[..., D_NOPE + half:], cos[:, nq:], -sin[:, nq:], name="rope_bwd_k")
    dq_full = jnp.concatenate([jnp.transpose(dqd[..., :D_NOPE], (1, 0, 2)), dq1.reshape(S, D_HEADS, half),
                               dq2.reshape(S, D_HEADS, half)], axis=-1).reshape(S, D_HEADS * (D_NOPE + D_ROPE)).astype(BF16)
    dkv_full = jnp.concatenate([jnp.transpose(dkd[..., :D_NOPE], (1, 0, 2)), jnp.transpose(dvd, (1, 0, 2))],
                               axis=-1).reshape(S, D_HEADS * (D_NOPE + D_V)).astype(BF16)
    dw_uq = _matmul_tn(cqn16, dq_full, shards=N_CHIPS, name="mla_dwuq")
    dw_ukv = _matmul_tn(ckvn16, dkv_full, shards=N_CHIPS, name="mla_dwukv")
    dcqn = _matmul(dq_full, w_uq, mode="nt", out_dtype=F32, name="mla_dcq")
    dckvn = _matmul(dkv_full, w_ukv, mode="nt", out_dtype=F32, name="mla_dckv")
    dcq, _, dqn_g, _ = _norm_bwd(dcqn, cq, qn_g, center=False, eps=RMS_EPS, name="q_rms_bwd")
    dckv, _, dkvn_g, _ = _norm_bwd(dckvn, ckv, kvn_g, center=False, eps=RMS_EPS, name="kv_rms_bwd")
    dh = jnp.concatenate([_from_heads(dqc), _from_heads(dkc), _from_heads(dvc), dcq, dckv, dk1, dk2], axis=-1)
    dh16 = _pad_cols(dh, ODD_IN // N_CHIPS, ODD_IN_PAD).astype(BF16)
    dw_in = _matmul_tn(x16, dh16, shards=N_CHIPS, name="odd_dwin")
    dx = _matmul(dh16, w_in, mode="nt", out_dtype=F32, epilogue="add", extra=du, alpha=ALPHA, name="odd_dx")
    return dx, dw_in, dqn_g[0], dkvn_g[0], dw_uq, dw_ukv, dw_out


WEIGHT_GROUPS = (("even_w_in",), ("even_w_out", "mlp_w1_0", "mlp_w2_0"), ("odd_w_in", "odd_w_uq", "odd_w_ukv", "odd_w_out"),
                 ("mlp_w1_1", "mlp_w2_1"))
GRAD_GROUPS = (("mlp_w2_1", "mlp_w1_1"), ("odd_w_out", "odd_w_uq", "odd_w_ukv", "odd_w_in"), ("mlp_w2_0", "mlp_w1_0"),
               ("even_w_out",), ("even_w_in",))


def _local_step(x, target, small, weights_for, send_grads, send_small, total_loss):
    def by_rows(t, rows):
        return t.reshape(N_CHIPS, rows // N_CHIPS, D_MODEL)

    x16 = x.astype(BF16)
    w = dict(weights_for(0, x16))
    y16, sv_e = _even_fwd(x16, w["even_w_in"], small["even_sinks"])
    w.update(weights_for(1, y16))
    mixed0 = _matmul(y16, w["even_w_out"], mode="nn", out_dtype=F32, name="even_out")
    x1, x1_16, u01 = _norm_fwd(x, mixed0, small["ln1_g"][0], small["ln1_b"][0], center=True, eps=LN_EPS, alpha=ALPHA, name="ln1_0")
    w1_0, w2_0 = w["mlp_w1_0"], w["mlp_w2_0"].reshape(1, D_FF, D_MODEL)
    x2, x2_16, sv_m0 = _mlp_fwd(x1, x1_16, w1_0, w2_0, small["ln2_g"][0], small["ln2_b"][0], 0)
    w.update(weights_for(2, x2_16))
    odd_w = (w["odd_w_in"], small["odd_q_norm_g"], small["odd_kv_norm_g"], w["odd_w_uq"], w["odd_w_ukv"],
             w["odd_w_out"].reshape(1, D_MODEL, D_MODEL))
    mixed1, sv_o = _odd_fwd(x2_16, *odd_w)
    x3, x3_16, u11 = _norm_fwd(x2, mixed1, small["ln1_g"][1], small["ln1_b"][1], center=True, eps=LN_EPS, alpha=ALPHA, name="ln1_1")
    w.update(weights_for(3, x3_16))
    w1_1, w2_1 = w["mlp_w1_1"], w["mlp_w2_1"].reshape(1, D_FF, D_MODEL)
    x4, _, sv_m1 = _mlp_fwd(x3, x3_16, w1_1, w2_1, small["ln2_g"][1], small["ln2_b"][1], 1)
    dy, loss_row = _loss_head(x4, target, name="loss_head")
    loss = total_loss(loss_row)

    dx3, dw1_1, dw2_1, dln2g_1, dln2b_1 = _mlp_bwd(dy, sv_m1, w1_1, w2_1, small["ln2_g"][1], 1, after=loss.reshape(1, 1))
    sent = send_grads(0, dict(mlp_w2_1=by_rows(dw2_1, D_FF), mlp_w1_1=dw1_1))
    du, du16, dln1g_1, dln1b_1 = _norm_bwd(dx3, u11, small["ln1_g"][1], center=True, eps=LN_EPS, name="ln1_1_bwd", after=sent)
    dx2, dwin_o, dqn_g, dkvn_g, dwuq, dwukv, dwout_o = _odd_bwd(du16, du, sv_o, *odd_w)
    sent = send_grads(1, dict(odd_w_out=by_rows(dwout_o, D_MODEL), odd_w_uq=dwuq, odd_w_ukv=dwukv, odd_w_in=dwin_o))
    dx1, dw1_0, dw2_0, dln2g_0, dln2b_0 = _mlp_bwd(dx2, sv_m0, w1_0, w2_0, small["ln2_g"][0], 0, after=sent)
    sent = send_grads(2, dict(mlp_w2_0=by_rows(dw2_0, D_FF), mlp_w1_0=dw1_0))
    du, du16, dln1g_0, dln1b_0 = _norm_bwd(dx1, u01, small["ln1_g"][0], center=True, eps=LN_EPS, name="ln1_0_bwd", after=sent)
    def send_last(dwin_e, dsinks):
        went = send_small(dict(even_sinks=dsinks, odd_q_norm_g=dqn_g, odd_kv_norm_g=dkvn_g,
                               ln1_g=jnp.concatenate([dln1g_0, dln1g_1]), ln1_b=jnp.concatenate([dln1b_0, dln1b_1]),
                               ln2_g=jnp.concatenate([dln2g_0, dln2g_1]), ln2_b=jnp.concatenate([dln2b_0, dln2b_1])))
        return send_grads(4, dict(even_w_in=dwin_e), went)

    dx0 = _even_bwd(du16, du, sv_e, w["even_w_in"], small["even_sinks"], w["even_w_out"],
                    lambda dwout_e: send_grads(3, dict(even_w_out=dwout_e)), send_last)
    return loss, dx0


SMALL_ORDER = ("ln1_g", "ln1_b", "ln2_g", "ln2_b", "even_sinks", "odd_q_norm_g", "odd_kv_norm_g")
SMALL_COLS = 2176


def _pack_small(parts):
    flat = jnp.concatenate([p.reshape(-1) for p in parts])
    return jnp.pad(flat, (0, 8 * SMALL_COLS - flat.shape[0])).reshape(8, SMALL_COLS)


def _unpack_small(packed, shapes):
    flat = packed.reshape(-1)
    out, pos = [], 0
    for s in shapes:
        n = math.prod(s)
        out.append(flat[pos:pos + n].reshape(s))
        pos += n
    return out


def kernel(x, even_w_in, even_sinks, even_w_out, odd_w_in, odd_q_norm_g, odd_kv_norm_g, odd_w_uq, odd_w_ukv, odd_w_out, ln1_g, ln1_b, mlp_w1, mlp_w2, ln2_g, ln2_b, loss_target, m_even_w_in, m_even_sinks, m_even_w_out, m_odd_w_in, m_odd_q_norm_g, m_odd_kv_norm_g, m_odd_w_uq, m_odd_w_ukv, m_odd_w_out, m_ln1_g, m_ln1_b, m_mlp_w1, m_mlp_w2, m_ln2_g, m_ln2_b, v_even_w_in, v_even_sinks, v_even_w_out, v_odd_w_in, v_odd_q_norm_g, v_odd_kv_norm_g, v_odd_w_uq, v_odd_w_ukv, v_odd_w_out, v_ln1_g, v_ln1_b, v_mlp_w1, v_mlp_w2, v_ln2_g, v_ln2_b):
    chip = 2 * lax.axis_index("x") + lax.axis_index("y")
    e_w, o_w = EVEN_IN // N_CHIPS, ODD_IN // N_CHIPS

    shards = dict(
        even_w_in=jnp.pad(even_w_in[0], ((0, 0), (0, EVEN_IN_PAD - e_w))), even_w_out=even_w_out[0],
        odd_w_in=jnp.pad(odd_w_in[0], ((0, 0), (0, ODD_IN_PAD - o_w))), odd_w_uq=odd_w_uq[0], odd_w_ukv=odd_w_ukv[0],
        odd_w_out=odd_w_out[0], mlp_w1_0=mlp_w1[0], mlp_w1_1=mlp_w1[1], mlp_w2_0=mlp_w2[0], mlp_w2_1=mlp_w2[1])
    names = list(shards)
    own16 = {n: shards[n].astype(BF16) for n in names}
    gather_started, token = [], own16[WEIGHT_GROUPS[0][0]]
    for g, group in enumerate(WEIGHT_GROUPS):
        lands = [jax.ShapeDtypeStruct((N_CHIPS,) + own16[n].shape, BF16) for n in group]
        gather_started.append(_copies_start([own16[n] for n in group], lands, token, SAME_CORE_OF_OTHER_CHIPS, _gather_src,
                                            _gather_dst, name=f"gather{g}_start"))
        token = gather_started[-1][-1]
    all_started = token

    def weights_for(g, after):
        group = WEIGHT_GROUPS[g]
        _, landed = _copies_wait(gather_started[g], all_started if g == 0 else after, SAME_CORE_OF_OTHER_CHIPS, _gather_src,
                                 _gather_landed, name=f"gather{g}_wait")
        full = _forward_halves(landed, name=f"gather{g}_forward")
        return {n: lax.dynamic_update_slice(f, own16[n][None], (chip, 0, 0)) for n, f in zip(group, full)}

    grads_started = {}

    def send_grads(g, shares, after=None):
        grads_started[g] = _reduce_scatter_start([shares[n] for n in GRAD_GROUPS[g]], after, name=f"grads{g}")
        return grads_started[g][-1]

    norm_rows = jnp.pad(jnp.concatenate([odd_q_norm_g[0], odd_kv_norm_g[0]])[None], ((0, 7), (0, 64)))
    norm_all = _all_gather8(norm_rows, name="gather_norm_gains")[0::16]
    qn_w, kvn_w = D_Q_RANK // N_CHIPS, D_KV_RANK // N_CHIPS
    small = dict(even_sinks=even_sinks[0], odd_q_norm_g=norm_all[:, :qn_w].reshape(D_Q_RANK),
                 odd_kv_norm_g=norm_all[:, qn_w:qn_w + kvn_w].reshape(D_KV_RANK), ln1_g=ln1_g, ln1_b=ln1_b, ln2_g=ln2_g, ln2_b=ln2_b)

    small_started = []

    def send_small(sm):
        pack = _pack_small([sm[n] for n in SMALL_ORDER])
        small_started.append(_copies_start([pack], [jax.ShapeDtypeStruct((8,) + pack.shape, F32)], pack, ALL_OTHER_DEVICES,
                                           _small_src, _small_dst, name="small_grads_start"))
        return small_started[0][-1]

    def small_sum(after):
        (pack,), (landed,) = _copies_wait(small_started[0], after, ALL_OTHER_DEVICES, _small_src, _small_landed,
                                          name="small_grads_wait")
        device = 4 * lax.axis_index("x") + 2 * lax.axis_index("y") + lax.axis_index("c")
        return _sum_slabs(lax.dynamic_update_slice(landed, pack[None], (device, 0, 0)), name="sum_small_grads")

    loss, grad_x = _local_step(x[0], loss_target[0], small, weights_for, send_grads, send_small,
                               lambda row: lax.psum(row[0, 0], ("x", "y", "c")))

    weights = dict(even_w_in=even_w_in, even_sinks=even_sinks, even_w_out=even_w_out, odd_w_in=odd_w_in, odd_q_norm_g=odd_q_norm_g,
                   odd_kv_norm_g=odd_kv_norm_g, odd_w_uq=odd_w_uq, odd_w_ukv=odd_w_ukv, odd_w_out=odd_w_out, ln1_g=ln1_g, ln1_b=ln1_b,
                   mlp_w1=mlp_w1, mlp_w2=mlp_w2, ln2_g=ln2_g, ln2_b=ln2_b)
    m_in = dict(even_w_in=m_even_w_in, even_sinks=m_even_sinks, even_w_out=m_even_w_out, odd_w_in=m_odd_w_in,
                odd_q_norm_g=m_odd_q_norm_g, odd_kv_norm_g=m_odd_kv_norm_g, odd_w_uq=m_odd_w_uq, odd_w_ukv=m_odd_w_ukv,
                odd_w_out=m_odd_w_out, ln1_g=m_ln1_g, ln1_b=m_ln1_b, mlp_w1=m_mlp_w1, mlp_w2=m_mlp_w2, ln2_g=m_ln2_g, ln2_b=m_ln2_b)
    v_in = dict(even_w_in=v_even_w_in, even_sinks=v_even_sinks, even_w_out=v_even_w_out, odd_w_in=v_odd_w_in,
                odd_q_norm_g=v_odd_q_norm_g, odd_kv_norm_g=v_odd_kv_norm_g, odd_w_uq=v_odd_w_uq, odd_w_ukv=v_odd_w_ukv,
                odd_w_out=v_odd_w_out, ln1_g=v_ln1_g, ln1_b=v_ln1_b, mlp_w1=v_mlp_w1, mlp_w2=v_mlp_w2, ln2_g=v_ln2_g, ln2_b=v_ln2_b)
    order = list(weights)
    updated = {}

    def update(n, g2d, layer=0, tag="", by_column=False):
        shape = weights[n].shape
        as3d = (1, math.prod(shape[:-1]), shape[-1]) if len(shape) == 2 else shape
        view = (lambda t: jnp.swapaxes(t.reshape(as3d), 1, 2)) if by_column else (lambda t: t.reshape(as3d))
        res = _adamw(view(weights[n]), g2d.T if by_column else g2d, view(m_in[n]), view(v_in[n]), layer=layer,
                     carry=updated.get(n), name=f"adamw_{n}{tag}")
        updated[n] = tuple(jnp.swapaxes(t, 1, 2) for t in res) if by_column else tuple(res)
        return res[0]

    after = grad_x
    for g, group in enumerate(GRAD_GROUPS):
        for n, r in zip(group, _reduce_scatter_finish(grads_started[g], after, name=f"grads{g}")):
            if n[:-2] in ("mlp_w1", "mlp_w2"):
                after = update(n[:-2], r, layer=int(n[-1]), tag=n[-2:])
            elif n == "even_w_in":
                after = update(n, r[:, :e_w], by_column=True)
            elif n == "odd_w_in":
                after = update(n, r[:, :o_w], by_column=True)
            else:
                after = update(n, r)
        if g == len(GRAD_GROUPS) - 2:
            g_ln1g, g_ln1b, g_ln2g, g_ln2b, g_sinks, g_qn, g_kvn = _unpack_small(
                small_sum(after), [(DEPTH, D_MODEL)] * 4 + [(1, A_Q_HEADS), (D_Q_RANK,), (D_KV_RANK,)])
            for n, gs in (("even_sinks", g_sinks), ("odd_q_norm_g", lax.dynamic_slice_in_dim(g_qn, chip * qn_w, qn_w)[None]),
                          ("odd_kv_norm_g", lax.dynamic_slice_in_dim(g_kvn, chip * kvn_w, kvn_w)[None]), ("ln1_g", g_ln1g),
                          ("ln1_b", g_ln1b), ("ln2_g", g_ln2g), ("ln2_b", g_ln2b)):
                update(n, gs)
    outs = [[updated[n][k].reshape(weights[n].shape) for n in order] for k in range(4)]
    return (loss, grad_x[None], *outs[0], *outs[1], *outs[2], *outs[3])
```

```python
import math

import jax
import jax.numpy as jnp
from jax import lax
from jax.experimental import pallas as pl
from jax.experimental.pallas import tpu as pltpu

F32 = jnp.float32
BF16 = jnp.bfloat16
MESH = pl.DeviceIdType.MESH

D_MODEL = 2048
SEQ = 2048
DEPTH = 2
HEAD_DIM = 64
A_Q_HEADS, A_KV_HEADS, A_WINDOW = 16, 2, 128
B_HEADS = 8
B_PATTERNS = ((128, 1), (512, 4), (2048, 16))
C_HEADS = 16
D_HEADS, D_Q_RANK, D_KV_RANK, D_NOPE, D_ROPE, D_V = 16, 512, 256, 64, 32, 64
ROPE_BASE = 10000.0
D_FF = 4 * D_MODEL
LN_EPS = 1e-5
RMS_EPS = 1e-6
ALPHA = (2 * DEPTH) ** 0.25
A_Q_W = A_Q_HEADS * HEAD_DIM
A_KV_W = A_KV_HEADS * HEAD_DIM
B_W = B_HEADS * HEAD_DIM
EVEN_IN = A_Q_W + 2 * A_KV_W + 3 * B_W * len(B_PATTERNS)
EVEN_OUT = A_Q_W + B_W
C_W = C_HEADS * HEAD_DIM
ODD_IN = 3 * C_W + D_Q_RANK + D_KV_RANK + D_ROPE
ADAM_LR, ADAM_B1, ADAM_B2, ADAM_EPS, ADAM_WD, ADAM_STEP = 0.001, 0.9, 0.999, 1e-08, 0.01, 10

N_CHIPS = 4
EVEN_IN_PAD = 1536
ODD_IN_PAD = 1024
ATT_BLK = 128
NEG = -1e30
V7X_VMEM_LIMIT = 48 * 1024 * 1024


def _params(*sem):
    return pltpu.CompilerParams(dimension_semantics=sem, vmem_limit_bytes=V7X_VMEM_LIMIT)


def _tile(n, cap):
    if n <= cap:
        return n
    t = cap - cap % 128
    while n % t:
        t -= 128
    return t


def _matmul(a, b, *, mode, out_dtype, name, epilogue=None, extra=None, alpha=1.0, after=None, tm=1024, tn=1024, tk=2048):
    if mode == "nn":
        M, K = a.shape
        P, _, Np = b.shape
        tm, tn, tk = _tile(M, tm), _tile(Np, tn), _tile(K, tk)
        nj = Np // tn
        grid = (M // tm, P * nj, K // tk)
        a_spec = pl.BlockSpec((tm, tk), lambda i, j, k: (i, k))
        b_spec = pl.BlockSpec((None, tk, tn), lambda i, j, k: (j // nj, k, j % nj))
        o_spec = pl.BlockSpec((tm, tn), lambda i, j, k: (i, j))
        out_shape = (M, P * Np)
        dims = (((1,), (0,)), ((), ()))
    elif mode == "nt":
        M, N = a.shape
        P, K, Np = b.shape
        tm, tn, tk = _tile(M, tm), _tile(K, tn), _tile(Np, tk)
        nkk = Np // tk
        grid = (M // tm, K // tn, P * nkk)
        a_spec = pl.BlockSpec((tm, tk), lambda i, j, k: (i, k))
        b_spec = pl.BlockSpec((None, tn, tk), lambda i, j, k: (k // nkk, j, k % nkk))
        o_spec = pl.BlockSpec((tm, tn), lambda i, j, k: (i, j))
        out_shape = (M, K)
        dims = (((1,), (1,)), ((), ()))
    else:
        raise ValueError(mode)
    n_k = grid[2]
    n_extra = 0 if extra is None else 1
    n_after = 0 if after is None else 1
    n_out = 2 if epilogue == "relu2" else 1

    def body(*refs):
        a_ref, b_ref = refs[:2]
        e_ref = refs[2] if n_extra else None
        outs = refs[2 + n_extra + n_after:2 + n_extra + n_after + n_out]
        part = lax.dot_general(a_ref[...], b_ref[...], dims, preferred_element_type=F32)

        def finish(r):
            if epilogue == "relu2":
                outs[0][...] = r.astype(outs[0].dtype)
                rp = jnp.maximum(r, 0.0)
                outs[1][...] = (rp * rp).astype(outs[1].dtype)
            elif epilogue == "drelu2":
                outs[0][...] = (r * (2.0 * jnp.maximum(e_ref[...].astype(F32), 0.0))).astype(outs[0].dtype)
            elif epilogue == "add":
                outs[0][...] = (r + alpha * e_ref[...].astype(F32)).astype(outs[0].dtype)
            else:
                outs[0][...] = r.astype(outs[0].dtype)

        if n_k == 1:
            finish(part)
        else:
            acc = refs[-1]
            k = pl.program_id(2)

            @pl.when(k == 0)
            def _():
                acc[...] = part

            @pl.when((k > 0) & (k < n_k - 1))
            def _():
                acc[...] += part

            @pl.when(k == n_k - 1)
            def _():
                finish(acc[...] + part)

    in_specs = [a_spec, b_spec] + ([o_spec] if n_extra else []) + ([pl.BlockSpec(memory_space=pl.ANY)] if n_after else [])
    shapes = [jax.ShapeDtypeStruct(out_shape, out_dtype)] * n_out
    res = pl.pallas_call(
        body, name=name, grid=grid, in_specs=in_specs,
        out_specs=[o_spec] * n_out, out_shape=shapes,
        scratch_shapes=[pltpu.VMEM((tm, tn), F32)] if n_k > 1 else [],
        compiler_params=_params("parallel", "parallel", "arbitrary"),
    )(a, b, *([extra] if n_extra else []), *([after] if n_after else []))
    return res if n_out > 1 else res[0]


def _matmul_tn(a, g, *, shards, name, tm=1024, tn=1024):
    M, K = a.shape
    P = shards
    Np = g.shape[1] // P
    tm, tn = _tile(K, tm), _tile(Np, tn)
    nj = Np // tn

    def body(a_ref, g_ref, o_ref):
        o_ref[...] = lax.dot_general(a_ref[...], g_ref[...], (((0,), (0,)), ((), ())), preferred_element_type=F32).astype(BF16)

    return pl.pallas_call(
        body, name=name, grid=(K // tm, P * nj),
        in_specs=[pl.BlockSpec((M, tm), lambda i, j: (0, i)), pl.BlockSpec((M, tn), lambda i, j: (0, j))],
        out_specs=pl.BlockSpec((None, tm, tn), lambda i, j: (j // nj, i, j % nj)),
        out_shape=jax.ShapeDtypeStruct((P, K, Np), BF16),
        compiler_params=_params("parallel", "parallel"),
    )(a, g)


def _norm_fwd(x, res, g, b, *, center, eps, alpha, name, rows=256):
    S, W = x.shape
    has_res = res is not None
    rows = _tile(S, rows)

    def body(*refs):
        x_ref = refs[0]
        r_ref = refs[1] if has_res else None
        g_ref = refs[1 + has_res]
        b_ref = refs[2 + has_res] if center else None
        y_ref, y16_ref, u_ref = refs[-3:]
        u = x_ref[...]
        if has_res:
            u = alpha * u + r_ref[...]
        if center:
            mu = jnp.mean(u, axis=1, keepdims=True)
            xc = u - mu
        else:
            xc = u
        var = jnp.mean(xc * xc, axis=1, keepdims=True)
        y = xc * lax.rsqrt(var + eps) * g_ref[...]
        if center:
            y = y + b_ref[...]
        y_ref[...] = y
        y16_ref[...] = y.astype(BF16)
        u_ref[...] = u

    row_spec = pl.BlockSpec((rows, W), lambda i: (i, 0))
    vec_spec = pl.BlockSpec((1, W), lambda i: (0, 0))
    ins = [x] + ([res] if has_res else []) + [g.reshape(1, W)] + ([b.reshape(1, W)] if center else [])
    specs = [row_spec] + ([row_spec] if has_res else []) + [vec_spec] + ([vec_spec] if center else [])
    return pl.pallas_call(
        body, name=name, grid=(S // rows,), in_specs=specs,
        out_specs=[row_spec, row_spec, row_spec],
        out_shape=[jax.ShapeDtypeStruct((S, W), F32), jax.ShapeDtypeStruct((S, W), BF16), jax.ShapeDtypeStruct((S, W), F32)],
        compiler_params=_params("parallel"),
    )(*ins)


def _norm_bwd(dy, u, g, *, center, eps, name, rows=256, after=None):
    S, W = u.shape
    rows = _tile(S, rows)

    def body(dy_ref, u_ref, g_ref, *rest):
        du_ref, du16_ref, dg_ref, db_ref = rest[-4:]
        i = pl.program_id(0)
        uu = u_ref[...]
        dyv = dy_ref[...]
        if center:
            xc = uu - jnp.mean(uu, axis=1, keepdims=True)
        else:
            xc = uu
        rstd = lax.rsqrt(jnp.mean(xc * xc, axis=1, keepdims=True) + eps)
        xhat = xc * rstd
        dxh = dyv * g_ref[...]
        c2 = jnp.mean(dxh * xhat, axis=1, keepdims=True)
        du = dxh - xhat * c2
        if center:
            du = du - jnp.mean(dxh, axis=1, keepdims=True)
        du = du * rstd
        du_ref[...] = du
        du16_ref[...] = du.astype(BF16)

        @pl.when(i == 0)
        def _():
            dg_ref[...] = jnp.zeros_like(dg_ref)
            db_ref[...] = jnp.zeros_like(db_ref)

        dg_ref[...] += jnp.sum(dyv * xhat, axis=0, keepdims=True)
        db_ref[...] += jnp.sum(dyv, axis=0, keepdims=True)

    row_spec = pl.BlockSpec((rows, W), lambda i: (i, 0))
    vec_spec = pl.BlockSpec((1, W), lambda i: (0, 0))
    return pl.pallas_call(
        body, name=name, grid=(S // rows,),
        in_specs=[row_spec, row_spec, vec_spec] + ([] if after is None else [pl.BlockSpec(memory_space=pl.ANY)]),
        out_specs=[row_spec, row_spec, vec_spec, vec_spec],
        out_shape=[jax.ShapeDtypeStruct((S, W), F32), jax.ShapeDtypeStruct((S, W), BF16),
                   jax.ShapeDtypeStruct((1, W), F32), jax.ShapeDtypeStruct((1, W), F32)],
        compiler_params=_params("arbitrary"),
    )(dy, u, g.reshape(1, W), *([] if after is None else [after]))


def _loss_head(y, target, *, name, rows=256):
    S, W = y.shape
    rows = _tile(S, rows)

    def body(y_ref, t_ref, dy_ref, l_ref):
        i = pl.program_id(0)
        e = y_ref[...] - t_ref[...]
        dy_ref[...] = e * (1.0 / W)

        @pl.when(i == 0)
        def _():
            l_ref[...] = jnp.zeros_like(l_ref)

        l_ref[...] += jnp.full(l_ref.shape, 0.5 / W * jnp.sum(e * e), F32)

    row_spec = pl.BlockSpec((rows, W), lambda i: (i, 0))
    return pl.pallas_call(
        body, name=name, grid=(S // rows,), in_specs=[row_spec, row_spec],
        out_specs=[row_spec, pl.BlockSpec((1, 128), lambda i: (0, 0))],
        out_shape=[jax.ShapeDtypeStruct((S, W), F32), jax.ShapeDtypeStruct((1, 128), F32)],
        compiler_params=_params("arbitrary"),
    )(y, target)


def _adamw(w, g, m, v, *, name, layer=0, carry=None, rows=256):
    L, R, C = w.shape
    rows = max(t for t in range(8, rows + 1, 8) if R % t == 0) if R % 8 == 0 else R
    c1 = 1.0 / (1.0 - ADAM_B1 ** ADAM_STEP)
    c2 = 1.0 / (1.0 - ADAM_B2 ** ADAM_STEP)

    def body(w_ref, g_ref, m_ref, v_ref, *rest):
        go_ref, d_ref, mo_ref, vo_ref = rest[-4:]
        gg = g_ref[...]
        mn = ADAM_B1 * m_ref[...] + (1.0 - ADAM_B1) * gg
        vn = ADAM_B2 * v_ref[...] + (1.0 - ADAM_B2) * (gg * gg)
        go_ref[...] = gg
        d_ref[...] = -ADAM_LR * ((mn * c1) / (jnp.sqrt(vn * c2) + ADAM_EPS) + ADAM_WD * w_ref[...])
        mo_ref[...] = mn
        vo_ref[...] = vn

    slab = pl.BlockSpec((None, rows, C), lambda i: (layer, i, 0))
    shp = jax.ShapeDtypeStruct((L, R, C), F32)
    carried = [] if carry is None else list(carry)
    return pl.pallas_call(
        body, name=name, grid=(R // rows,),
        in_specs=[slab, pl.BlockSpec((rows, C), lambda i: (i, 0)), slab, slab] + [pl.BlockSpec(memory_space=pl.ANY)] * len(carried),
        out_specs=[slab] * 4, out_shape=[shp] * 4, input_output_aliases={4 + k: k for k in range(len(carried))},
        compiler_params=_params("parallel"),
    )(w, g, m, v, *carried)


def _rope(x1, x2, cos, sin, *, name, rows=512):
    H, S, W = x1.shape
    rows = _tile(S, rows)

    def body(x1_ref, x2_ref, c_ref, s_ref, y1_ref, y2_ref):
        t1 = jnp.sum(x1_ref[...], axis=0)
        t2 = jnp.sum(x2_ref[...], axis=0)
        c = c_ref[...]
        s = s_ref[...]
        y1_ref[...] = t1 * c - t2 * s
        y2_ref[...] = t1 * s + t2 * c

    xs = pl.BlockSpec((H, rows, W), lambda i: (0, i, 0))
    ts = pl.BlockSpec((rows, W), lambda i: (i, 0))
    shp = jax.ShapeDtypeStruct((S, W), F32)
    return pl.pallas_call(
        body, name=name, grid=(S // rows,), in_specs=[xs, xs, ts, ts], out_specs=[ts, ts], out_shape=[shp, shp],
        compiler_params=_params("parallel"),
    )(x1, x2, cos, sin)


def _band_mask(blk, n, n_back):
    row = lax.broadcasted_iota(jnp.int32, (blk, 2 * blk), 0)
    col = lax.broadcasted_iota(jnp.int32, (blk, 2 * blk), 1)
    rel = blk + row - col
    return rel, (rel >= 0) & (rel <= n_back) & ((col >= blk) | (n >= 1))


def _window(ref, head, i, blk):
    before = pl.multiple_of(jnp.maximum(i - 1, 0) * blk, blk)
    return jnp.concatenate([ref[head, pl.ds(before, blk), :], ref[head, pl.ds(pl.multiple_of(i * blk, blk), blk), :]], axis=0)


def _window_add(ref, head, i, blk, update):
    before = pl.multiple_of(jnp.maximum(i - 1, 0) * blk, blk)
    ref[head, pl.ds(before, blk), :] += update[:blk]
    ref[head, pl.ds(pl.multiple_of(i * blk, blk), blk), :] += update[blk:]


def _band_fwd(q, k, v, slopes, sinks, *, hps, nb_seq, n_back, scale, dist_scale, name):
    blk = ATT_BLK
    Hq, T, d = q.shape
    Hk = k.shape[0]
    group = Hq // Hk
    kps = max(hps // group, 1)
    use_sink = sinks is not None

    def body(*refs):
        q_ref, k_ref, v_ref, slope_ref = refs[:4]
        sink_ref = refs[4] if use_sink else None
        o_ref, lse_ref = refs[-2:]
        i = pl.program_id(1)
        rel, valid = _band_mask(blk, i % nb_seq, n_back)
        relf = rel.astype(F32)
        for hh in range(hps):
            h = pl.program_id(0) * hps + hh
            kk = _window(k_ref, hh // group, i, blk)
            vv = _window(v_ref, hh // group, i, blk)
            s = lax.dot_general(q_ref[hh], kk, (((1,), (1,)), ((), ())), preferred_element_type=F32) * scale
            s = jnp.where(valid, s - (slope_ref[h] * dist_scale) * relf, NEG)
            m = jnp.max(s, axis=1, keepdims=True)
            if use_sink:
                m = jnp.maximum(m, sink_ref[h])
            p = jnp.where(valid, jnp.exp(s - m), 0.0)
            l = jnp.sum(p, axis=1, keepdims=True)
            if use_sink:
                l = l + jnp.exp(sink_ref[h] - m)
            o_ref[hh] = jnp.dot(p.astype(BF16), vv, preferred_element_type=F32) / l
            lse_ref[hh] = m + jnp.log(l)

    smem = pl.BlockSpec(memory_space=pltpu.SMEM)
    qs = pl.BlockSpec((hps, blk, d), lambda g, i: (g, i, 0))
    ks = pl.BlockSpec((kps, T, d), lambda g, i: (g, 0, 0))
    ins = [q, k, v, slopes] + ([sinks] if use_sink else [])
    return pl.pallas_call(
        body, name=name, grid=(Hq // hps, T // blk), in_specs=[qs, ks, ks, smem] + ([smem] if use_sink else []),
        out_specs=[qs, pl.BlockSpec((hps, blk, 1), lambda g, i: (g, i, 0))],
        out_shape=[jax.ShapeDtypeStruct((Hq, T, d), F32), jax.ShapeDtypeStruct((Hq, T, 1), F32)],
        compiler_params=_params("parallel", "parallel"),
    )(*ins)


def _band_bwd(q, k, v, o, do, lse, dlse, slopes, sinks, *, hps, nb_seq, n_back, scale, dist_scale, name):
    blk = ATT_BLK
    Hq, T, d = q.shape
    Hk = k.shape[0]
    group = Hq // Hk
    kps = max(hps // group, 1)
    use_sink, use_dlse = sinks is not None, dlse is not None

    def body(*refs):
        q_ref, k_ref, v_ref, o_ref, do_ref, lse_ref = refs[:6]
        pos = 6
        dlse_ref = refs[pos] if use_dlse else None
        pos += use_dlse
        slope_ref = refs[pos]
        pos += 1
        sink_ref = refs[pos] if use_sink else None
        pos += use_sink
        dq_ref, dk_ref, dv_ref = refs[pos:pos + 3]
        dsink_ref = refs[pos + 3] if use_sink else None
        i = pl.program_id(1)
        rel, valid = _band_mask(blk, i % nb_seq, n_back)
        relf = rel.astype(F32)

        @pl.when(i == 0)
        def _():
            dk_ref[...] = jnp.zeros_like(dk_ref)
            dv_ref[...] = jnp.zeros_like(dv_ref)
            if use_sink:
                dsink_ref[...] = jnp.zeros_like(dsink_ref)

        for kh in range(kps):
            dk_acc = jnp.zeros((2 * blk, d), F32)
            dv_acc = jnp.zeros((2 * blk, d), F32)
            kk = _window(k_ref, kh, i, blk)
            vv = _window(v_ref, kh, i, blk)
            for hh in range(kh * min(group, hps), (kh + 1) * min(group, hps)):
                h = pl.program_id(0) * hps + hh
                qb = q_ref[hh]
                dob = do_ref[hh]
                do16 = dob.astype(BF16)
                lse = lse_ref[hh]
                c = -jnp.sum(dob * o_ref[hh], axis=1, keepdims=True)
                if use_dlse:
                    c = c + dlse_ref[hh]
                s = lax.dot_general(qb, kk, (((1,), (1,)), ((), ())), preferred_element_type=F32) * scale
                s = jnp.where(valid, s - (slope_ref[h] * dist_scale) * relf, NEG)
                p = jnp.where(valid, jnp.exp(s - lse), 0.0)
                dp = lax.dot_general(do16, vv, (((1,), (1,)), ((), ())), preferred_element_type=F32)
                ds16 = (p * (dp + c) * scale).astype(BF16)
                dq_ref[hh] = jnp.dot(ds16, kk, preferred_element_type=F32)
                dk_acc = dk_acc + lax.dot_general(ds16, qb, (((0,), (0,)), ((), ())), preferred_element_type=F32)
                dv_acc = dv_acc + lax.dot_general(p.astype(BF16), do16, (((0,), (0,)), ((), ())), preferred_element_type=F32)
                if use_sink:
                    dsink_ref[hh] += jnp.full((1, 128), jnp.sum(jnp.exp(sink_ref[h] - lse) * c), F32)
            _window_add(dk_ref, kh, i, blk, dk_acc)
            _window_add(dv_ref, kh, i, blk, dv_acc)

    smem = pl.BlockSpec(memory_space=pltpu.SMEM)
    qs = pl.BlockSpec((hps, blk, d), lambda g, i: (g, i, 0))
    ls = pl.BlockSpec((hps, blk, 1), lambda g, i: (g, i, 0))
    ks = pl.BlockSpec((kps, T, d), lambda g, i: (g, 0, 0))
    in_specs = [qs, ks, ks, qs, qs, ls] + ([ls] if use_dlse else []) + [smem] + ([smem] if use_sink else [])
    ins = [q, k, v, o, do, lse] + ([dlse] if use_dlse else []) + [slopes] + ([sinks] if use_sink else [])
    out_specs = [qs, ks, ks]
    out_shape = [jax.ShapeDtypeStruct((Hq, T, d), F32), jax.ShapeDtypeStruct((Hk, T, d), F32),
                 jax.ShapeDtypeStruct((Hk, T, d), F32)]
    if use_sink:
        out_specs.append(pl.BlockSpec((hps, 1, 128), lambda g, i: (g, 0, 0)))
        out_shape.append(jax.ShapeDtypeStruct((Hq, 1, 128), F32))
    return pl.pallas_call(
        body, name=name, grid=(Hq // hps, T // blk), in_specs=in_specs, out_specs=out_specs, out_shape=out_shape,
        compiler_params=_params("parallel", "arbitrary"),
    )(*ins)


def _diagonal_mask(blk):
    return lax.broadcasted_iota(jnp.int32, (blk, blk), 0) >= lax.broadcasted_iota(jnp.int32, (blk, blk), 1)


def _causal_fwd(q, k, v, *, blk, hps, scale, name):
    H, T, dqk = q.shape
    dv = v.shape[2]

    def body(q_ref, k_ref, v_ref, o_ref, lse_ref):
        n = pl.program_id(1)
        qs = [q_ref[hh] for hh in range(hps)]

        def block(j, carry, valid):
            start = pl.multiple_of(j * blk, blk)
            out = []
            for hh in range(hps):
                m, l, acc = carry[hh]
                kb = k_ref[hh, pl.ds(start, blk), :]
                vb = v_ref[hh, pl.ds(start, blk), :]
                s = lax.dot_general(qs[hh], kb, (((1,), (1,)), ((), ())), preferred_element_type=F32) * scale
                if valid is not None:
                    s = jnp.where(valid, s, NEG)
                m_new = jnp.maximum(m, jnp.max(s, axis=1, keepdims=True))
                p = _keep(valid, jnp.exp(s - m_new))
                a = jnp.exp(m - m_new)
                out.append((m_new, a * l + jnp.sum(p, axis=1, keepdims=True),
                            a * acc + jnp.dot(p.astype(BF16), vb, preferred_element_type=F32)))
            return tuple(out)

        init = tuple((jnp.full((blk, 1), NEG, F32), jnp.zeros((blk, 1), F32), jnp.zeros((blk, dv), F32)) for _ in range(hps))
        res = block(n, lax.fori_loop(0, n, lambda j, carry: block(j, carry, None), init), _diagonal_mask(blk))
        for hh in range(hps):
            m, l, acc = res[hh]
            o_ref[hh] = acc / l
            lse_ref[hh] = m + jnp.log(l)

    qs_ = pl.BlockSpec((hps, blk, dqk), lambda g, i: (g, i, 0))
    return pl.pallas_call(
        body, name=name, grid=(H // hps, T // blk),
        in_specs=[qs_, pl.BlockSpec((hps, T, dqk), lambda g, i: (g, 0, 0)), pl.BlockSpec((hps, T, dv), lambda g, i: (g, 0, 0))],
        out_specs=[pl.BlockSpec((hps, blk, dv), lambda g, i: (g, i, 0)), pl.BlockSpec((hps, blk, 1), lambda g, i: (g, i, 0))],
        out_shape=[jax.ShapeDtypeStruct((H, T, dv), F32), jax.ShapeDtypeStruct((H, T, 1), F32)],
        compiler_params=_params("parallel", "parallel"),
    )(q, k, v)


def _causal_bwd(q, k, v, o, do, lse, *, blk, hps, scale, name):
    H, T, dqk = q.shape
    dv = v.shape[2]

    def body(q_ref, k_ref, v_ref, o_ref, do_ref, lse_ref, dq_ref, dk_ref, dv_ref):
        n = pl.program_id(1)

        @pl.when(n == 0)
        def _():
            dk_ref[...] = jnp.zeros_like(dk_ref)
            dv_ref[...] = jnp.zeros_like(dv_ref)

        qs = [q_ref[hh] for hh in range(hps)]
        do16 = [do_ref[hh].astype(BF16) for hh in range(hps)]
        lses = [lse_ref[hh] for hh in range(hps)]
        cs = [-jnp.sum(do_ref[hh] * o_ref[hh], axis=1, keepdims=True) for hh in range(hps)]

        def block(j, dqs, valid):
            start = pl.multiple_of(j * blk, blk)
            out = []
            for hh in range(hps):
                kb = k_ref[hh, pl.ds(start, blk), :]
                vb = v_ref[hh, pl.ds(start, blk), :]
                s = lax.dot_general(qs[hh], kb, (((1,), (1,)), ((), ())), preferred_element_type=F32) * scale
                if valid is not None:
                    s = jnp.where(valid, s, NEG)
                p = _keep(valid, jnp.exp(s - lses[hh]))
                dp = lax.dot_general(do16[hh], vb, (((1,), (1,)), ((), ())), preferred_element_type=F32)
                ds16 = (p * (dp + cs[hh]) * scale).astype(BF16)
                dk_ref[hh, pl.ds(start, blk), :] += lax.dot_general(ds16, qs[hh], (((0,), (0,)), ((), ())), preferred_element_type=F32)
                dv_ref[hh, pl.ds(start, blk), :] += lax.dot_general(p.astype(BF16), do16[hh], (((0,), (0,)), ((), ())),
                                                                    preferred_element_type=F32)
                out.append(dqs[hh] + jnp.dot(ds16, kb, preferred_element_type=F32))
            return tuple(out)

        init = tuple(jnp.zeros((blk, dqk), F32) for _ in range(hps))
        res = block(n, lax.fori_loop(0, n, lambda j, dqs: block(j, dqs, None), init), _diagonal_mask(blk))
        for hh in range(hps):
            dq_ref[hh] = res[hh]

    qs_ = pl.BlockSpec((hps, blk, dqk), lambda g, i: (g, i, 0))
    os_ = pl.BlockSpec((hps, blk, dv), lambda g, i: (g, i, 0))
    ls_ = pl.BlockSpec((hps, blk, 1), lambda g, i: (g, i, 0))
    ks_ = pl.BlockSpec((hps, T, dqk), lambda g, i: (g, 0, 0))
    vs_ = pl.BlockSpec((hps, T, dv), lambda g, i: (g, 0, 0))
    return pl.pallas_call(
        body, name=name, grid=(H // hps, T // blk), in_specs=[qs_, ks_, vs_, os_, os_, ls_], out_specs=[qs_, ks_, vs_],
        out_shape=[jax.ShapeDtypeStruct((H, T, dqk), F32), jax.ShapeDtypeStruct((H, T, dqk), F32),
                   jax.ShapeDtypeStruct((H, T, dv), F32)],
        compiler_params=_params("parallel", "arbitrary"),
    )(q, k, v, o, do, lse)


def _tri_sum(x, upper):
    hi = x.astype(BF16)
    lo = (x - hi.astype(F32)).astype(BF16)
    return jnp.dot(hi, upper, preferred_element_type=F32) + jnp.dot(lo, upper, preferred_element_type=F32)


def _keep(mask, x):
    return x if mask is None else jnp.where(mask, x, 0.0)


def _stick_terms(qb, kb, scale, strict):
    z = lax.dot_general(qb, kb, (((1,), (1,)), ((), ())), preferred_element_type=F32) * scale
    e = jnp.exp(-jnp.abs(z))
    lb = jnp.minimum(z, 0.0) - jnp.log(1.0 + e)
    return z, e, lb, _keep(strict, lb - z)


def _stick_fwd(q, k, v, *, blk, hps, scale, name):
    H, T, dh = q.shape

    def body(q_ref, k_ref, v_ref, o_ref, tot_ref):
        n = pl.program_id(1)
        qs = [q_ref[hh] for hh in range(hps)]
        r_i = lax.broadcasted_iota(jnp.int32, (blk, blk), 0)
        c_i = lax.broadcasted_iota(jnp.int32, (blk, blk), 1)
        upper = (r_i > c_i).astype(BF16)

        def block(j, carry, strict):
            start = pl.multiple_of(j * blk, blk)
            out = []
            for hh in range(hps):
                run, acc = carry[hh]
                kb = k_ref[hh, pl.ds(start, blk), :]
                vb = v_ref[hh, pl.ds(start, blk), :]
                _, _, lb, lk = _stick_terms(qs[hh], kb, scale, strict)
                w = _keep(strict, jnp.exp(lb + run + _tri_sum(lk, upper)))
                out.append((run + jnp.sum(lk, axis=1, keepdims=True), acc + jnp.dot(w.astype(BF16), vb, preferred_element_type=F32)))
            return tuple(out)

        init = tuple((jnp.zeros((blk, 1), F32), jnp.zeros((blk, dh), F32)) for _ in range(hps))
        res = lax.fori_loop(1, n + 1, lambda t, carry: block(n - t, carry, None), block(n, init, r_i > c_i))
        for hh in range(hps):
            tot_ref[hh], o_ref[hh] = res[hh]

    qs_ = pl.BlockSpec((hps, blk, dh), lambda g, i: (g, i, 0))
    ks_ = pl.BlockSpec((hps, T, dh), lambda g, i: (g, 0, 0))
    ts_ = pl.BlockSpec((hps, blk, 1), lambda g, i: (g, i, 0))
    return pl.pallas_call(
        body, name=name, grid=(H // hps, T // blk), in_specs=[qs_, ks_, ks_], out_specs=[qs_, ts_],
        out_shape=[jax.ShapeDtypeStruct((H, T, dh), F32), jax.ShapeDtypeStruct((H, T, 1), F32)],
        compiler_params=_params("parallel", "parallel"),
    )(q, k, v)


def _stick_bwd(q, k, v, tot, do, *, blk, hps, scale, name):
    H, T, dh = q.shape

    def body(q_ref, k_ref, v_ref, tot_ref, do_ref, dq_ref, dk_ref, dv_ref):
        n = pl.program_id(1)
        qs = [q_ref[hh] for hh in range(hps)]
        do16 = [do_ref[hh].astype(BF16) for hh in range(hps)]
        tots = [tot_ref[hh] for hh in range(hps)]
        r_i = lax.broadcasted_iota(jnp.int32, (blk, blk), 0)
        c_i = lax.broadcasted_iota(jnp.int32, (blk, blk), 1)
        upto = (r_i <= c_i).astype(BF16)
        below = (r_i < c_i).astype(BF16)

        @pl.when(n == 0)
        def _():
            dk_ref[...] = jnp.zeros_like(dk_ref)
            dv_ref[...] = jnp.zeros_like(dv_ref)

        def block(j, carry, strict):
            start = pl.multiple_of(j * blk, blk)
            out = []
            for hh in range(hps):
                run, grun, dq = carry[hh]
                kb = k_ref[hh, pl.ds(start, blk), :]
                vb = v_ref[hh, pl.ds(start, blk), :]
                z, e, lb, lk = _stick_terms(qs[hh], kb, scale, strict)
                w = _keep(strict, jnp.exp(lb + tots[hh] - (run + _tri_sum(lk, upto))))
                dw = lax.dot_general(do16[hh], vb, (((1,), (1,)), ((), ())), preferred_element_type=F32)
                g = w * dw
                before = grun + _tri_sum(g, below)
                inv = 1.0 / (1.0 + e)
                sig = jnp.where(z >= 0, inv, e * inv)
                dz16 = (_keep(strict, g * (1.0 - sig) - sig * before) * scale).astype(BF16)
                dk_ref[hh, pl.ds(start, blk), :] += lax.dot_general(dz16, qs[hh], (((0,), (0,)), ((), ())), preferred_element_type=F32)
                dv_ref[hh, pl.ds(start, blk), :] += lax.dot_general(w.astype(BF16), do16[hh], (((0,), (0,)), ((), ())),
                                                                    preferred_element_type=F32)
                out.append((run + jnp.sum(lk, axis=1, keepdims=True), grun + jnp.sum(g, axis=1, keepdims=True),
                            dq + jnp.dot(dz16, kb, preferred_element_type=F32)))
            return tuple(out)

        zero = jnp.zeros((blk, 1), F32)
        init = tuple((zero, zero, jnp.zeros((blk, dh), F32)) for _ in range(hps))
        res = block(n, lax.fori_loop(0, n, lambda j, carry: block(j, carry, None), init), r_i > c_i)
        for hh in range(hps):
            dq_ref[hh] = res[hh][2]

    qs_ = pl.BlockSpec((hps, blk, dh), lambda g, i: (g, i, 0))
    ks_ = pl.BlockSpec((hps, T, dh), lambda g, i: (g, 0, 0))
    ts_ = pl.BlockSpec((hps, blk, 1), lambda g, i: (g, i, 0))
    shp = jax.ShapeDtypeStruct((H, T, dh), F32)
    return pl.pallas_call(
        body, name=name, grid=(H // hps, T // blk), in_specs=[qs_, ks_, ks_, ts_, qs_], out_specs=[qs_, ks_, ks_],
        out_shape=[shp, shp, shp],
        compiler_params=_params("parallel", "arbitrary"),
    )(q, k, v, tot, do)


def _mix_fwd(outs, lses, *, name, rows=512):
    H, T, dh = outs[0].shape
    rows = _tile(T, rows)

    def body(o0, o1, o2, l0, l1, l2, y_ref):
        ls = [l0[...], l1[...], l2[...]]
        mx = jnp.maximum(jnp.maximum(ls[0], ls[1]), ls[2])
        es = [jnp.exp(x - mx) for x in ls]
        den = es[0] + es[1] + es[2]
        y_ref[...] = (es[0] * o0[...] + es[1] * o1[...] + es[2] * o2[...]) / den

    os_ = pl.BlockSpec((None, rows, dh), lambda h, i: (h, i, 0))
    ls_ = pl.BlockSpec((None, rows, 1), lambda h, i: (h, i, 0))
    return pl.pallas_call(
        body, name=name, grid=(H, T // rows), in_specs=[os_] * 3 + [ls_] * 3, out_specs=os_,
        out_shape=jax.ShapeDtypeStruct((H, T, dh), F32),
        compiler_params=_params("parallel", "parallel"),
    )(*outs, *lses)


def _mix_bwd(outs, lses, dy, *, name, rows=512):
    H, T, dh = outs[0].shape
    rows = _tile(T, rows)

    def body(o0, o1, o2, l0, l1, l2, dy_ref, d0, d1, d2, g0, g1, g2):
        ls = [l0[...], l1[...], l2[...]]
        mx = jnp.maximum(jnp.maximum(ls[0], ls[1]), ls[2])
        es = [jnp.exp(x - mx) for x in ls]
        den = es[0] + es[1] + es[2]
        mix = [e / den for e in es]
        dyv = dy_ref[...]
        dm = [jnp.sum(dyv * o[...], axis=1, keepdims=True) for o in (o0, o1, o2)]
        avg = mix[0] * dm[0] + mix[1] * dm[1] + mix[2] * dm[2]
        for d_ref, g_ref, w, t in zip((d0, d1, d2), (g0, g1, g2), mix, dm):
            d_ref[...] = w * dyv
            g_ref[...] = w * (t - avg)

    os_ = pl.BlockSpec((None, rows, dh), lambda h, i: (h, i, 0))
    ls_ = pl.BlockSpec((None, rows, 1), lambda h, i: (h, i, 0))
    so = jax.ShapeDtypeStruct((H, T, dh), F32)
    sl = jax.ShapeDtypeStruct((H, T, 1), F32)
    res = pl.pallas_call(
        body, name=name, grid=(H, T // rows), in_specs=[os_] * 3 + [ls_] * 3 + [os_], out_specs=[os_] * 3 + [ls_] * 3,
        out_shape=[so] * 3 + [sl] * 3,
        compiler_params=_params("parallel", "parallel"),
    )(*outs, *lses, dy)
    return res[:3], res[3:]


def _position():
    return lax.axis_index("x"), lax.axis_index("y"), lax.axis_index("c")


def _other_chips(x, y):
    return [(1 - x, y), (x, 1 - y), (1 - x, 1 - y)]


HBM_SPEC = pl.BlockSpec(memory_space=pltpu.HBM)
SEM_SPEC = pl.BlockSpec(memory_space=pltpu.SEMAPHORE)
ANY_SPEC = pl.BlockSpec(memory_space=pl.ANY)
SPLIT_COPY = pltpu.CompilerParams(has_side_effects=pltpu.SideEffectType.DATAFLOW_SIDE_EFFECTING)


def _in_hbm(a):
    return pltpu.with_memory_space_constraint(a, pltpu.HBM)


SAME_CORE_OF_OTHER_CHIPS = ((1, 0, 0), (0, 1, 0), (1, 1, 0))
ALL_OTHER_DEVICES = ((0, 0, 1), (1, 0, 0), (0, 1, 0), (1, 1, 0), (1, 0, 1), (0, 1, 1), (1, 1, 1))


def _peer_copies(srcs, lands, send_sems, recv_sems, relations, src_of, dst_of):
    me = _position()
    copies = []
    for w in range(len(srcs)):
        for k, flips in enumerate(relations):
            peer = tuple(1 - p if f else p for p, f in zip(me, flips))
            i = len(relations) * w + k
            copies.append(pltpu.make_async_remote_copy(
                src_ref=src_of(srcs[w], peer), dst_ref=dst_of(lands[w], k, peer, me), send_sem=send_sems[i],
                recv_sem=recv_sems[i], device_id=peer, device_id_type=MESH))
    return copies


def _copies_start(srcs, land_shapes, after, relations, src_of, dst_of, *, name):
    n = len(srcs)
    m = len(relations) * n

    def body(*refs):
        src_refs, land_refs = refs[:n], refs[n:2 * n]
        send_sems, recv_sems = refs[2 * n + 1:2 * n + 1 + m], refs[2 * n + 1 + m:2 * n + 1 + 2 * m]
        token = refs[-1]
        for cp in _peer_copies(src_refs, land_refs, send_sems, recv_sems, relations, src_of, dst_of):
            cp.start()
        token[...] = jnp.zeros_like(token)

    lands = [_in_hbm(lax.empty(s.shape, s.dtype)) for s in land_shapes]
    out_shape = ([pltpu.SemaphoreType.DMA(())] * (2 * m)
                 + [pltpu.HBM(a.shape, a.dtype) for a in srcs] + [pltpu.HBM(s.shape, s.dtype) for s in land_shapes]
                 + [jax.ShapeDtypeStruct((8, 128), F32)])
    res = pl.pallas_call(
        body, name=name, in_specs=[HBM_SPEC] * (2 * n) + [ANY_SPEC],
        out_specs=[SEM_SPEC] * (2 * m) + [HBM_SPEC] * (2 * n) + [pl.BlockSpec(memory_space=pltpu.VMEM)],
        out_shape=out_shape, input_output_aliases={i: 2 * m + i for i in range(2 * n)}, compiler_params=SPLIT_COPY,
    )(*[_in_hbm(a) for a in srcs], *lands, after)
    return (list(res[:m]), list(res[m:2 * m]), list(res[2 * m:2 * m + n]), list(res[2 * m + n:2 * m + 2 * n]), res[-1])


def _copies_wait(started, after, relations, src_of, dst_of, *, name):
    send_sems, recv_sems, srcs, lands, _ = started
    n = len(srcs)
    m = len(relations) * n

    def body(*refs):
        src_refs, land_refs = refs[:n], refs[n:2 * n]
        send_refs, recv_refs = refs[2 * n:2 * n + m], refs[2 * n + m:2 * n + 2 * m]
        for cp in _peer_copies(src_refs, land_refs, send_refs, recv_refs, relations, src_of, dst_of):
            cp.wait_send()
            cp.wait_recv()

    res = pl.pallas_call(
        body, name=name, in_specs=[HBM_SPEC] * (2 * n) + [SEM_SPEC] * (2 * m) + [ANY_SPEC], out_specs=[HBM_SPEC] * (2 * n),
        out_shape=[pltpu.HBM(a.shape, a.dtype) for a in srcs + lands],
        input_output_aliases={i: i for i in range(2 * n)}, compiler_params=SPLIT_COPY,
    )(*srcs, *lands, *send_sems, *recv_sems, after)
    return list(res[:n]), list(res[n:])


def _half_rows(ref, core):
    half = ref.shape[-2] // 2
    return pl.ds(core * half, half)


def _gather_src(src, peer):
    return src.at[_half_rows(src, peer[2]), :]


def _gather_dst(land, k, peer, me):
    return land.at[2 * me[0] + me[1], _half_rows(land, me[2]), :]


def _gather_landed(land, k, peer, me):
    return land.at[2 * peer[0] + peer[1], _half_rows(land, me[2]), :]


def _exchange_src(src, peer):
    return src.at[2 * peer[0] + peer[1], _half_rows(src, peer[2]), :]


def _exchange_dst(land, k, peer, me):
    return land.at[k]


def _small_src(src, peer):
    return src


def _small_dst(land, k, peer, me):
    return land.at[4 * me[0] + 2 * me[1] + me[2]]


def _small_landed(land, k, peer, me):
    return land.at[4 * peer[0] + 2 * peer[1] + peer[2]]


def _forward_copies(bufs, send_sems, recv_sems, core):
    x, y, c = _position()
    copies = []
    for w in range(len(bufs)):
        for j, chip in enumerate(_other_chips(x, y)):
            rows = _gather_landed(bufs[w], j, chip, (x, y, core))
            copies.append(pltpu.make_async_remote_copy(src_ref=rows, dst_ref=rows, send_sem=send_sems[3 * w + j],
                                                       recv_sem=recv_sems[3 * w + j], device_id=(x, y, 1 - c), device_id_type=MESH))
    return copies


def _forward_start(bufs, *, name):
    n = len(bufs)
    m = 3 * n

    def body(*refs):
        send_sems, recv_sems = refs[n:n + m], refs[n + m:n + 2 * m]
        for cp in _forward_copies(refs[:n], send_sems, recv_sems, lax.axis_index("c")):
            cp.start()

    res = pl.pallas_call(
        body, name=name, in_specs=[HBM_SPEC] * n, out_specs=[SEM_SPEC] * (2 * m) + [HBM_SPEC] * n,
        out_shape=[pltpu.SemaphoreType.DMA(())] * (2 * m) + [pltpu.HBM(a.shape, a.dtype) for a in bufs],
        input_output_aliases={i: 2 * m + i for i in range(n)}, compiler_params=SPLIT_COPY,
    )(*bufs)
    return list(res[:m]), list(res[m:2 * m]), list(res[2 * m:])


def _forward_wait(started, after, *, name):
    send_sems, recv_sems, bufs = started
    n = len(bufs)
    m = 3 * n

    def body(*refs):
        send_refs, recv_refs = refs[n:n + m], refs[n + m:n + 2 * m]
        c = lax.axis_index("c")
        for sent, got in zip(_forward_copies(refs[:n], send_refs, recv_refs, c),
                             _forward_copies(refs[:n], send_refs, recv_refs, 1 - c)):
            sent.wait_send()
            got.wait_recv()

    return list(pl.pallas_call(
        body, name=name, in_specs=[HBM_SPEC] * n + [SEM_SPEC] * (2 * m) + [ANY_SPEC], out_specs=[HBM_SPEC] * n,
        out_shape=[pltpu.HBM(a.shape, a.dtype) for a in bufs], input_output_aliases={i: i for i in range(n)},
        compiler_params=SPLIT_COPY,
    )(*bufs, *send_sems, *recv_sems, after))


def _share_halves(bufs, *, name):
    n = len(bufs)

    def body(*refs):
        ins, outs = refs[:n], refs[n:2 * n]
        send_sems, recv_sems = refs[2 * n:]
        x, y, c = _position()
        sends = []
        for w in range(n):
            sends.append(pltpu.make_async_remote_copy(src_ref=ins[w].at[c], dst_ref=outs[w].at[c], send_sem=send_sems.at[w],
                                                      recv_sem=recv_sems.at[w], device_id=(x, y, 1 - c), device_id_type=MESH))
            sends[-1].start()
        for w in range(n):
            pltpu.make_async_remote_copy(src_ref=ins[w].at[1 - c], dst_ref=outs[w].at[1 - c], send_sem=send_sems.at[w],
                                         recv_sem=recv_sems.at[w], device_id=(x, y, 1 - c), device_id_type=MESH).wait_recv()
        for cp in sends:
            cp.wait_send()

    hbm = pl.BlockSpec(memory_space=pl.ANY)
    return pl.pallas_call(
        body, name=name, in_specs=[hbm] * n, out_specs=[hbm] * n,
        out_shape=[jax.ShapeDtypeStruct(a.shape, a.dtype) for a in bufs],
        input_output_aliases={w: w for w in range(n)},
        scratch_shapes=[pltpu.SemaphoreType.DMA((n,)), pltpu.SemaphoreType.DMA((n,))],
    )(*bufs)


def _all_gather8(v, *, name):
    m, n = v.shape

    def body(v_ref, out_ref, send_sems, recv_sems, local_sem):
        x, y, c = _position()
        me, sibling = (x, y, c), (x, y, 1 - c)
        chips = _other_chips(x, y)

        def rows(px, py, pc):
            return out_ref.at[pl.ds((4 * px + 2 * py + pc) * m, m), :]

        def copy(k, block, to, src=None):
            return pltpu.make_async_remote_copy(src_ref=rows(*block) if src is None else src, dst_ref=rows(*block),
                                                send_sem=send_sems.at[k], recv_sem=recv_sems.at[k], device_id=to, device_id_type=MESH)

        mine = pltpu.make_async_copy(v_ref, rows(*me), local_sem)
        mine.start()
        first = [copy(0, me, sibling, src=v_ref)]
        first += [copy(1 + j, me, (*chip, c), src=v_ref) for j, chip in enumerate(chips)]
        for cp in first:
            cp.start()
        passed = [copy(4 + j, (*chip, c), sibling) for j, chip in enumerate(chips)]
        for j, chip in enumerate(chips):
            copy(1 + j, (*chip, c), me).wait_recv()
            passed[j].start()
        copy(0, sibling, me).wait_recv()
        for j, chip in enumerate(chips):
            copy(4 + j, (*chip, 1 - c), me).wait_recv()
        for cp in first + passed:
            cp.wait_send()
        mine.wait()

    vmem = pl.BlockSpec(memory_space=pltpu.VMEM)
    return pl.pallas_call(
        body, name=name, in_specs=[vmem], out_specs=vmem, out_shape=jax.ShapeDtypeStruct((8 * m, n), v.dtype),
        scratch_shapes=[pltpu.SemaphoreType.DMA((7,)), pltpu.SemaphoreType.DMA((7,)), pltpu.SemaphoreType.DMA],
    )(v)


def _sum_parts(own, landed, where, *, name, rows=256):
    n_peers, half, C = landed.shape
    rows = _tile(half, rows)
    nb = half // rows

    def body(where_ref, own_ref, land_ref, o_ref):
        acc = own_ref[...].astype(F32)
        for k in range(n_peers):
            acc = acc + land_ref[k].astype(F32)
        o_ref[...] = acc

    grid_spec = pltpu.PrefetchScalarGridSpec(
        num_scalar_prefetch=1, grid=(nb,),
        in_specs=[pl.BlockSpec((None, rows, C), lambda i, where_ref: (where_ref[0], where_ref[1] * nb + i, 0)),
                  pl.BlockSpec((n_peers, rows, C), lambda i, where_ref: (0, i, 0))],
        out_specs=pl.BlockSpec((None, rows, C), lambda i, where_ref: (where_ref[1], i, 0)))
    return pl.pallas_call(
        body, name=name, grid_spec=grid_spec, out_shape=jax.ShapeDtypeStruct((2, half, C), F32),
        compiler_params=_params("parallel"),
    )(where, own, landed)


def _sum_slabs(a, *, name, rows=256):
    P, R, C = a.shape
    rows = _tile(R, rows)

    def body(a_ref, o_ref):
        acc = a_ref[0].astype(F32)
        for p in range(1, P):
            acc = acc + a_ref[p].astype(F32)
        o_ref[...] = acc

    return pl.pallas_call(
        body, name=name, grid=(R // rows,), in_specs=[pl.BlockSpec((P, rows, C), lambda i: (0, i, 0))],
        out_specs=pl.BlockSpec((rows, C), lambda i: (i, 0)),
        out_shape=jax.ShapeDtypeStruct((R, C), F32), compiler_params=_params("parallel"),
    )(a)


def _reduce_scatter_start(grads, after=None, *, name):
    lands = [jax.ShapeDtypeStruct((len(ALL_OTHER_DEVICES), g.shape[1] // 2, g.shape[2]), g.dtype) for g in grads]
    return _copies_start(grads, lands, grads[0] if after is None else after, ALL_OTHER_DEVICES, _exchange_src, _exchange_dst,
                         name=name + "_start")


def _reduce_scatter_finish(started, after, *, name):
    core = lax.axis_index("c").astype(jnp.int32)
    chip = (2 * lax.axis_index("x") + lax.axis_index("y")).astype(jnp.int32)
    where = jnp.stack([chip, core])
    sums, landed = _copies_wait(started, after, ALL_OTHER_DEVICES, _exchange_src, _exchange_dst, name=name + "_wait")
    reduced = [_sum_parts(s, a, where, name=f"{name}_sum{w}") for w, (s, a) in enumerate(zip(sums, landed))]
    both = _share_halves(reduced, name=name + "_share")
    return [b.reshape(2 * b.shape[1], b.shape[2]) for b in both]


def _to_heads(t, heads, dil=1):
    S = t.shape[0]
    d = t.shape[1] // heads
    return jnp.transpose(t.reshape(S // dil, dil, heads, d), (2, 1, 0, 3)).reshape(heads, S, d)


def _from_heads(t, dil=1):
    heads, S, d = t.shape
    return jnp.transpose(t.reshape(heads, dil, S // dil, d), (2, 1, 0, 3)).reshape(S, heads * d)


def _unstride(t, dil):
    heads, S, d = t.shape
    return jnp.transpose(t.reshape(heads, dil, S // dil, d), (0, 2, 1, 3)).reshape(heads, S, d)


def _restride(t, dil):
    heads, S, d = t.shape
    return jnp.transpose(t.reshape(heads, S // dil, dil, d), (0, 2, 1, 3)).reshape(heads, S, d)


def _pad_cols(t, width, padded):
    S = t.shape[0]
    return jnp.pad(t.reshape(S, N_CHIPS, width), ((0, 0), (0, 0), (0, padded - width))).reshape(S, N_CHIPS * padded)


def _column_reader(t, width, padded):
    def take(lo, hi):
        pieces = []
        for p in range(N_CHIPS):
            a, b = max(lo, p * width), min(hi, (p + 1) * width)
            if a < b:
                pieces.append(t[:, p * padded + a - p * width:p * padded + b - p * width])
        return pieces[0] if len(pieces) == 1 else jnp.concatenate(pieces, axis=-1)
    return take


def _alibi(n):
    return 2.0 ** (-8.0 * jnp.arange(1, n + 1, dtype=F32) / n)


B_KW = [dict(hps=4, nb_seq=SEQ // d // ATT_BLK, n_back=w // d, scale=1.0 / math.sqrt(HEAD_DIM), dist_scale=d)
        for w, d in B_PATTERNS]
A_KW = dict(hps=A_Q_HEADS // A_KV_HEADS, nb_seq=SEQ // ATT_BLK, n_back=A_WINDOW - 1, scale=1.0 / math.sqrt(HEAD_DIM), dist_scale=1)
D_KW = dict(blk=256, hps=2, scale=1.0 / math.sqrt(D_NOPE + D_ROPE))
C_KW = dict(blk=256, hps=2, scale=1.0 / math.sqrt(HEAD_DIM))


def _rope_tables(reps):
    inv_freq = ROPE_BASE ** (-jnp.arange(0, D_ROPE, 2, dtype=F32) / D_ROPE)
    ang = jnp.arange(SEQ, dtype=F32)[:, None] * inv_freq[None, :]
    return jnp.tile(jnp.cos(ang), (1, reps)), jnp.tile(jnp.sin(ang), (1, reps))


def _mlp_fwd(x, x16, w1, w2, g, b, tag):
    hid, act = _matmul(x16, w1, mode="nn", out_dtype=BF16, epilogue="relu2", name=f"mlp{tag}_up")
    m = _matmul(act, w2, mode="nn", out_dtype=F32, name=f"mlp{tag}_down")
    y, y16, u = _norm_fwd(x, m, g, b, center=True, eps=LN_EPS, alpha=ALPHA, name=f"ln2_{tag}")
    return y, y16, (x16, hid, act, u)


def _mlp_bwd(dy, saved, w1, w2, g, tag, after=None):
    x16, hid, act, u = saved
    du, du16, dg, db = _norm_bwd(dy, u, g, center=True, eps=LN_EPS, name=f"ln2_{tag}_bwd", after=after)
    dw2 = _matmul_tn(act, du16, shards=1, name=f"mlp{tag}_dw2")
    dhid = _matmul(du16, w2, mode="nt", out_dtype=BF16, epilogue="drelu2", extra=hid, name=f"mlp{tag}_dact")
    dw1 = _matmul_tn(x16, dhid, shards=N_CHIPS, name=f"mlp{tag}_dw1")
    dx = _matmul(dhid, w1, mode="nt", out_dtype=F32, epilogue="add", extra=du, alpha=ALPHA, name=f"mlp{tag}_dx")
    return dx, dw1, dw2, dg, db


def _even_fwd(x16, w_in, sinks):
    h = _column_reader(_matmul(x16, w_in, mode="nn", out_dtype=BF16, name="even_in"), EVEN_IN // N_CHIPS, EVEN_IN_PAD)
    qa = _to_heads(h(0, A_Q_W), A_Q_HEADS)
    ka = _to_heads(h(A_Q_W, A_Q_W + A_KV_W), A_KV_HEADS)
    va = _to_heads(h(A_Q_W + A_KV_W, A_Q_W + 2 * A_KV_W), A_KV_HEADS)
    slopes_a, slopes_b = _alibi(A_Q_HEADS), _alibi(B_HEADS)
    oa, lse_a = _band_fwd(qa, ka, va, slopes_a, sinks, name="swa_fwd", **A_KW)
    qkv_b, outs, lses = [], [], []
    for gi, (_, dil) in enumerate(B_PATTERNS):
        base = A_Q_W + 2 * A_KV_W + gi * 3 * B_W
        q, k, v = (_to_heads(h(base + i * B_W, base + (i + 1) * B_W), B_HEADS, dil) for i in range(3))
        o, lse = _band_fwd(q, k, v, slopes_b, None, name=f"dil{gi}_fwd", **B_KW[gi])
        qkv_b.append((q, k, v, o, lse))
        outs.append(_unstride(o, dil))
        lses.append(_unstride(lse, dil))
    ob = _mix_fwd(outs, lses, name="dil_mix")
    y16 = jnp.concatenate([_from_heads(oa), _from_heads(ob)], axis=-1).astype(BF16)
    return y16, (x16, qa, ka, va, oa, lse_a, qkv_b, outs, lses, y16)


def _even_bwd(dmix16, du, saved, w_in, sinks, w_out, send_w_out, send_w_in):
    x16, qa, ka, va, oa, lse_a, qkv_b, outs, lses, y16 = saved
    slopes_a, slopes_b = _alibi(A_Q_HEADS), _alibi(B_HEADS)
    sent = send_w_out(_matmul_tn(y16, dmix16, shards=N_CHIPS, name="even_dwout"))
    dy = _matmul(dmix16, w_out, mode="nt", out_dtype=F32, name="even_dy", after=sent)
    doa = _to_heads(dy[:, :A_Q_W], A_Q_HEADS)
    dob = _to_heads(dy[:, A_Q_W:], B_HEADS)
    dqa, dka, dva, dsink = _band_bwd(qa, ka, va, oa, doa, lse_a, None, slopes_a, sinks, name="swa_bwd", **A_KW)
    douts, dlses = _mix_bwd(outs, lses, dob, name="dil_mix_bwd")
    parts = [_from_heads(dqa), _from_heads(dka), _from_heads(dva)]
    for gi, (_, dil) in enumerate(B_PATTERNS):
        q, k, v, o, lse = qkv_b[gi]
        dq, dk, dv = _band_bwd(q, k, v, o, _restride(douts[gi], dil), lse, _restride(dlses[gi], dil), slopes_b, None,
                               name=f"dil{gi}_bwd", **B_KW[gi])
        parts += [_from_heads(dq, dil), _from_heads(dk, dil), _from_heads(dv, dil)]
    dh16 = _pad_cols(jnp.concatenate(parts, axis=-1), EVEN_IN // N_CHIPS, EVEN_IN_PAD).astype(BF16)
    sent = send_w_in(_matmul_tn(x16, dh16, shards=N_CHIPS, name="even_dwin"), dsink[:, 0, 0])
    return _matmul(dh16, w_in, mode="nt", out_dtype=F32, epilogue="add", extra=du, alpha=ALPHA, name="even_dx", after=sent)


def _odd_fwd(x16, w_in, qn_g, kvn_g, w_uq, w_ukv, w_out):
    S = x16.shape[0]
    h = _column_reader(_matmul(x16, w_in, mode="nn", out_dtype=F32, name="odd_in"), ODD_IN // N_CHIPS, ODD_IN_PAD)
    qc, kc, vc = (_to_heads(h(i * C_W, (i + 1) * C_W).astype(BF16), C_HEADS) for i in range(3))
    cq = h(3 * C_W, 3 * C_W + D_Q_RANK)
    ckv = h(3 * C_W + D_Q_RANK, 3 * C_W + D_Q_RANK + D_KV_RANK)
    kr = h(3 * C_W + D_Q_RANK + D_KV_RANK, ODD_IN)
    oc, tot = _stick_fwd(qc, kc, vc, name="stick_fwd", **C_KW)
    _, cqn16, _ = _norm_fwd(cq, None, qn_g, None, center=False, eps=RMS_EPS, alpha=1.0, name="q_rms")
    _, ckvn16, _ = _norm_fwd(ckv, None, kvn_g, None, center=False, eps=RMS_EPS, alpha=1.0, name="kv_rms")
    q = _matmul(cqn16, w_uq, mode="nn", out_dtype=F32, name="mla_uq").reshape(S, D_HEADS, D_NOPE + D_ROPE)
    kv = _matmul(ckvn16, w_ukv, mode="nn", out_dtype=F32, name="mla_ukv").reshape(S, D_HEADS, D_NOPE + D_V)
    half = D_ROPE // 2
    q_rope = q[..., D_NOPE:]
    x1 = jnp.concatenate([q_rope[..., :half].reshape(S, D_HEADS * half), kr[:, :half]], axis=-1)
    x2 = jnp.concatenate([q_rope[..., half:].reshape(S, D_HEADS * half), kr[:, half:]], axis=-1)
    cos, sin = _rope_tables(D_HEADS + 1)
    y1, y2 = _rope(x1[None], x2[None], cos, sin, name="rope_fwd")
    nq = D_HEADS * half
    q_rot = jnp.concatenate([y1[:, :nq].reshape(S, D_HEADS, half), y2[:, :nq].reshape(S, D_HEADS, half)], axis=-1)
    kr_rot = jnp.concatenate([y1[:, nq:], y2[:, nq:]], axis=-1)
    qd = jnp.transpose(jnp.concatenate([q[..., :D_NOPE], q_rot], axis=-1), (1, 0, 2)).astype(BF16)
    kd = jnp.transpose(jnp.concatenate([kv[..., :D_NOPE], jnp.broadcast_to(kr_rot[:, None, :], (S, D_HEADS, D_ROPE))], axis=-1),
                       (1, 0, 2)).astype(BF16)
    vd = jnp.transpose(kv[..., D_NOPE:], (1, 0, 2)).astype(BF16)
    od, lse_d = _causal_fwd(qd, kd, vd, name="mla_fwd", **D_KW)
    y16 = jnp.concatenate([_from_heads(oc), _from_heads(od)], axis=-1).astype(BF16)
    mixed = _matmul(y16, w_out, mode="nn", out_dtype=F32, name="odd_out")
    return mixed, (x16, qc, kc, vc, tot, cq, ckv, cqn16, ckvn16, qd, kd, vd, od, lse_d, y16)


def _odd_bwd(dmix16, du, saved, w_in, qn_g, kvn_g, w_uq, w_ukv, w_out):
    x16, qc, kc, vc, tot, cq, ckv, cqn16, ckvn16, qd, kd, vd, od, lse_d, y16 = saved
    S = x16.shape[0]
    half = D_ROPE // 2
    dw_out = _matmul_tn(y16, dmix16, shards=1, name="odd_dwout")
    dy = _matmul(dmix16, w_out, mode="nt", out_dtype=F32, name="odd_dy")
    doc = _to_heads(dy[:, :C_W], C_HEADS)
    dod = _to_heads(dy[:, C_W:], D_HEADS)
    dqc, dkc, dvc = _stick_bwd(qc, kc, vc, tot, doc, name="stick_bwd", **C_KW)
    dqd, dkd, dvd = _causal_bwd(qd, kd, vd, od, dod, lse_d, name="mla_bwd", **D_KW)
    cos, sin = _rope_tables(D_HEADS + 1)
    nq = D_HEADS * half
    gq1 = jnp.transpose(dqd[..., D_NOPE:D_NOPE + half], (1, 0, 2)).reshape(1, S, nq)
    gq2 = jnp.transpose(dqd[..., D_NOPE + half:], (1, 0, 2)).reshape(1, S, nq)
    dq1, dq2 = _rope(gq1, gq2, cos[:, :nq], -sin[:, :nq], name="rope_bwd_q")
    dk1, dk2 = _rope(dkd[..., D_NOPE:D_NOPE + half], dkd[..., D_NOPE + half:], cos[:, nq:], -sin[:, nq:], name="rope_bwd_k")
    dq_full = jnp.concatenate([jnp.transpose(dqd[..., :D_NOPE], (1, 0, 2)), dq1.reshape(S, D_HEADS, half),
                               dq2.reshape(S, D_HEADS, half)], axis=-1).reshape(S, D_HEADS * (D_NOPE + D_ROPE)).astype(BF16)
    dkv_full = jnp.concatenate([jnp.transpose(dkd[..., :D_NOPE], (1, 0, 2)), jnp.transpose(dvd, (1, 0, 2))],
                               axis=-1).reshape(S, D_HEADS * (D_NOPE + D_V)).astype(BF16)
    dw_uq = _matmul_tn(cqn16, dq_full, shards=N_CHIPS, name="mla_dwuq")
    dw_ukv = _matmul_tn(ckvn16, dkv_full, shards=N_CHIPS, name="mla_dwukv")
    dcqn = _matmul(dq_full, w_uq, mode="nt", out_dtype=F32, name="mla_dcq")
    dckvn = _matmul(dkv_full, w_ukv, mode="nt", out_dtype=F32, name="mla_dckv")
    dcq, _, dqn_g, _ = _norm_bwd(dcqn, cq, qn_g, center=False, eps=RMS_EPS, name="q_rms_bwd")
    dckv, _, dkvn_g, _ = _norm_bwd(dckvn, ckv, kvn_g, center=False, eps=RMS_EPS, name="kv_rms_bwd")
    dh = jnp.concatenate([_from_heads(dqc), _from_heads(dkc), _from_heads(dvc), dcq, dckv, dk1, dk2], axis=-1)
    dh16 = _pad_cols(dh, ODD_IN // N_CHIPS, ODD_IN_PAD).astype(BF16)
    dw_in = _matmul_tn(x16, dh16, shards=N_CHIPS, name="odd_dwin")
    dx = _matmul(dh16, w_in, mode="nt", out_dtype=F32, epilogue="add", extra=du, alpha=ALPHA, name="odd_dx")
    return dx, dw_in, dqn_g[0], dkvn_g[0], dw_uq, dw_ukv, dw_out


WEIGHT_GROUPS = (("even_w_in",), ("even_w_out", "mlp_w1_0", "mlp_w2_0"), ("odd_w_in", "odd_w_uq", "odd_w_ukv", "odd_w_out"),
                 ("mlp_w1_1", "mlp_w2_1"))
GRAD_GROUPS = (("mlp_w2_1", "mlp_w1_1"), ("odd_w_out", "odd_w_uq", "odd_w_ukv", "odd_w_in"), ("mlp_w2_0", "mlp_w1_0"),
               ("even_w_out",), ("even_w_in",))


def _local_step(x, target, small, weights_for, send_grads, send_small, total_loss):
    def by_rows(t, rows):
        return t.reshape(N_CHIPS, rows // N_CHIPS, D_MODEL)

    x16 = x.astype(BF16)
    w = dict(weights_for(0, x16))
    y16, sv_e = _even_fwd(x16, w["even_w_in"], small["even_sinks"])
    w.update(weights_for(1, y16))
    mixed0 = _matmul(y16, w["even_w_out"], mode="nn", out_dtype=F32, name="even_out")
    x1, x1_16, u01 = _norm_fwd(x, mixed0, small["ln1_g"][0], small["ln1_b"][0], center=True, eps=LN_EPS, alpha=ALPHA, name="ln1_0")
    w1_0, w2_0 = w["mlp_w1_0"], w["mlp_w2_0"].reshape(1, D_FF, D_MODEL)
    x2, x2_16, sv_m0 = _mlp_fwd(x1, x1_16, w1_0, w2_0, small["ln2_g"][0], small["ln2_b"][0], 0)
    w.update(weights_for(2, x2_16))
    odd_w = (w["odd_w_in"], small["odd_q_norm_g"], small["odd_kv_norm_g"], w["odd_w_uq"], w["odd_w_ukv"],
             w["odd_w_out"].reshape(1, D_MODEL, D_MODEL))
    mixed1, sv_o = _odd_fwd(x2_16, *odd_w)
    x3, x3_16, u11 = _norm_fwd(x2, mixed1, small["ln1_g"][1], small["ln1_b"][1], center=True, eps=LN_EPS, alpha=ALPHA, name="ln1_1")
    w.update(weights_for(3, x3_16))
    w1_1, w2_1 = w["mlp_w1_1"], w["mlp_w2_1"].reshape(1, D_FF, D_MODEL)
    x4, _, sv_m1 = _mlp_fwd(x3, x3_16, w1_1, w2_1, small["ln2_g"][1], small["ln2_b"][1], 1)
    dy, loss_row = _loss_head(x4, target, name="loss_head")
    loss = total_loss(loss_row)

    dx3, dw1_1, dw2_1, dln2g_1, dln2b_1 = _mlp_bwd(dy, sv_m1, w1_1, w2_1, small["ln2_g"][1], 1, after=loss.reshape(1, 1))
    sent = send_grads(0, dict(mlp_w2_1=by_rows(dw2_1, D_FF), mlp_w1_1=dw1_1))
    du, du16, dln1g_1, dln1b_1 = _norm_bwd(dx3, u11, small["ln1_g"][1], center=True, eps=LN_EPS, name="ln1_1_bwd", after=sent)
    dx2, dwin_o, dqn_g, dkvn_g, dwuq, dwukv, dwout_o = _odd_bwd(du16, du, sv_o, *odd_w)
    sent = send_grads(1, dict(odd_w_out=by_rows(dwout_o, D_MODEL), odd_w_uq=dwuq, odd_w_ukv=dwukv, odd_w_in=dwin_o))
    dx1, dw1_0, dw2_0, dln2g_0, dln2b_0 = _mlp_bwd(dx2, sv_m0, w1_0, w2_0, small["ln2_g"][0], 0, after=sent)
    sent = send_grads(2, dict(mlp_w2_0=by_rows(dw2_0, D_FF), mlp_w1_0=dw1_0))
    du, du16, dln1g_0, dln1b_0 = _norm_bwd(dx1, u01, small["ln1_g"][0], center=True, eps=LN_EPS, name="ln1_0_bwd", after=sent)
    def send_last(dwin_e, dsinks):
        went = send_small(dict(even_sinks=dsinks, odd_q_norm_g=dqn_g, odd_kv_norm_g=dkvn_g,
                               ln1_g=jnp.concatenate([dln1g_0, dln1g_1]), ln1_b=jnp.concatenate([dln1b_0, dln1b_1]),
                               ln2_g=jnp.concatenate([dln2g_0, dln2g_1]), ln2_b=jnp.concatenate([dln2b_0, dln2b_1])))
        return send_grads(4, dict(even_w_in=dwin_e), went)

    dx0 = _even_bwd(du16, du, sv_e, w["even_w_in"], small["even_sinks"], w["even_w_out"],
                    lambda dwout_e: send_grads(3, dict(even_w_out=dwout_e)), send_last)
    return loss, dx0


SMALL_ORDER = ("ln1_g", "ln1_b", "ln2_g", "ln2_b", "even_sinks", "odd_q_norm_g", "odd_kv_norm_g")
SMALL_COLS = 2176


def _pack_small(parts):
    flat = jnp.concatenate([p.reshape(-1) for p in parts])
    return jnp.pad(flat, (0, 8 * SMALL_COLS - flat.shape[0])).reshape(8, SMALL_COLS)


def _unpack_small(packed, shapes):
    flat = packed.reshape(-1)
    out, pos = [], 0
    for s in shapes:
        n = math.prod(s)
        out.append(flat[pos:pos + n].reshape(s))
        pos += n
    return out


def kernel(x, even_w_in, even_sinks, even_w_out, odd_w_in, odd_q_norm_g, odd_kv_norm_g, odd_w_uq, odd_w_ukv, odd_w_out, ln1_g, ln1_b, mlp_w1, mlp_w2, ln2_g, ln2_b, loss_target, m_even_w_in, m_even_sinks, m_even_w_out, m_odd_w_in, m_odd_q_norm_g, m_odd_kv_norm_g, m_odd_w_uq, m_odd_w_ukv, m_odd_w_out, m_ln1_g, m_ln1_b, m_mlp_w1, m_mlp_w2, m_ln2_g, m_ln2_b, v_even_w_in, v_even_sinks, v_even_w_out, v_odd_w_in, v_odd_q_norm_g, v_odd_kv_norm_g, v_odd_w_uq, v_odd_w_ukv, v_odd_w_out, v_ln1_g, v_ln1_b, v_mlp_w1, v_mlp_w2, v_ln2_g, v_ln2_b):
    chip = 2 * lax.axis_index("x") + lax.axis_index("y")
    e_w, o_w = EVEN_IN // N_CHIPS, ODD_IN // N_CHIPS

    shards = dict(
        even_w_in=jnp.pad(even_w_in[0], ((0, 0), (0, EVEN_IN_PAD - e_w))), even_w_out=even_w_out[0],
        odd_w_in=jnp.pad(odd_w_in[0], ((0, 0), (0, ODD_IN_PAD - o_w))), odd_w_uq=odd_w_uq[0], odd_w_ukv=odd_w_ukv[0],
        odd_w_out=odd_w_out[0], mlp_w1_0=mlp_w1[0], mlp_w1_1=mlp_w1[1], mlp_w2_0=mlp_w2[0], mlp_w2_1=mlp_w2[1])
    names = list(shards)
    own16 = {n: shards[n].astype(BF16) for n in names}
    gather_started, token = [], own16[WEIGHT_GROUPS[0][0]]
    for g, group in enumerate(WEIGHT_GROUPS):
        lands = [jax.ShapeDtypeStruct((N_CHIPS,) + own16[n].shape, BF16) for n in group]
        gather_started.append(_copies_start([own16[n] for n in group], lands, token, SAME_CORE_OF_OTHER_CHIPS, _gather_src,
                                            _gather_dst, name=f"gather{g}_start"))
        token = gather_started[-1][-1]
    all_started = token

    forwarding = {}

    def forward(g, after):
        _, landed = _copies_wait(gather_started[g], after, SAME_CORE_OF_OTHER_CHIPS, _gather_src, _gather_landed,
                                 name=f"gather{g}_wait")
        forwarding[g] = _forward_start(landed, name=f"gather{g}_forward")

    def weights_for(g, after):
        if g not in forwarding:
            forward(g, all_started if g == 0 else after)
        full = _forward_wait(forwarding[g], after, name=f"gather{g}_forwarded")
        if 1 <= g < len(WEIGHT_GROUPS) - 1:
            forward(g + 1, after)
        return {n: lax.dynamic_update_slice(f, own16[n][None], (chip, 0, 0)) for n, f in zip(WEIGHT_GROUPS[g], full)}

    grads_started = {}

    def send_grads(g, shares, after=None):
        grads_started[g] = _reduce_scatter_start([shares[n] for n in GRAD_GROUPS[g]], after, name=f"grads{g}")
        return grads_started[g][-1]

    norm_rows = jnp.pad(jnp.concatenate([odd_q_norm_g[0], odd_kv_norm_g[0]])[None], ((0, 7), (0, 64)))
    norm_all = _all_gather8(norm_rows, name="gather_norm_gains")[0::16]
    qn_w, kvn_w = D_Q_RANK // N_CHIPS, D_KV_RANK // N_CHIPS
    small = dict(even_sinks=even_sinks[0], odd_q_norm_g=norm_all[:, :qn_w].reshape(D_Q_RANK),
                 odd_kv_norm_g=norm_all[:, qn_w:qn_w + kvn_w].reshape(D_KV_RANK), ln1_g=ln1_g, ln1_b=ln1_b, ln2_g=ln2_g, ln2_b=ln2_b)

    small_started = []

    def send_small(sm):
        pack = _pack_small([sm[n] for n in SMALL_ORDER])
        small_started.append(_copies_start([pack], [jax.ShapeDtypeStruct((8,) + pack.shape, F32)], pack, ALL_OTHER_DEVICES,
                                           _small_src, _small_dst, name="small_grads_start"))
        return small_started[0][-1]

    def small_sum(after):
        (pack,), (landed,) = _copies_wait(small_started[0], after, ALL_OTHER_DEVICES, _small_src, _small_landed,
                                          name="small_grads_wait")
        device = 4 * lax.axis_index("x") + 2 * lax.axis_index("y") + lax.axis_index("c")
        return _sum_slabs(lax.dynamic_update_slice(landed, pack[None], (device, 0, 0)), name="sum_small_grads")

    loss, grad_x = _local_step(x[0], loss_target[0], small, weights_for, send_grads, send_small,
                               lambda row: lax.psum(row[0, 0], ("x", "y", "c")))

    weights = dict(even_w_in=even_w_in, even_sinks=even_sinks, even_w_out=even_w_out, odd_w_in=odd_w_in, odd_q_norm_g=odd_q_norm_g,
                   odd_kv_norm_g=odd_kv_norm_g, odd_w_uq=odd_w_uq, odd_w_ukv=odd_w_ukv, odd_w_out=odd_w_out, ln1_g=ln1_g, ln1_b=ln1_b,
                   mlp_w1=mlp_w1, mlp_w2=mlp_w2, ln2_g=ln2_g, ln2_b=ln2_b)
    m_in = dict(even_w_in=m_even_w_in, even_sinks=m_even_sinks, even_w_out=m_even_w_out, odd_w_in=m_odd_w_in,
                odd_q_norm_g=m_odd_q_norm_g, odd_kv_norm_g=m_odd_kv_norm_g, odd_w_uq=m_odd_w_uq, odd_w_ukv=m_odd_w_ukv,
                odd_w_out=m_odd_w_out, ln1_g=m_ln1_g, ln1_b=m_ln1_b, mlp_w1=m_mlp_w1, mlp_w2=m_mlp_w2, ln2_g=m_ln2_g, ln2_b=m_ln2_b)
    v_in = dict(even_w_in=v_even_w_in, even_sinks=v_even_sinks, even_w_out=v_even_w_out, odd_w_in=v_odd_w_in,
                odd_q_norm_g=v_odd_q_norm_g, odd_kv_norm_g=v_odd_kv_norm_g, odd_w_uq=v_odd_w_uq, odd_w_ukv=v_odd_w_ukv,
                odd_w_out=v_odd_w_out, ln1_g=v_ln1_g, ln1_b=v_ln1_b, mlp_w1=v_mlp_w1, mlp_w2=v_mlp_w2, ln2_g=v_ln2_g, ln2_b=v_ln2_b)
    order = list(weights)
    updated = {}

    def update(n, g2d, layer=0, tag="", by_column=False):
        shape = weights[n].shape
        as3d = (1, math.prod(shape[:-1]), shape[-1]) if len(shape) == 2 else shape
        view = (lambda t: jnp.swapaxes(t.reshape(as3d), 1, 2)) if by_column else (lambda t: t.reshape(as3d))
        res = _adamw(view(weights[n]), g2d.T if by_column else g2d, view(m_in[n]), view(v_in[n]), layer=layer,
                     carry=updated.get(n), name=f"adamw_{n}{tag}")
        updated[n] = tuple(jnp.swapaxes(t, 1, 2) for t in res) if by_column else tuple(res)
        return res[0]

    after = grad_x
    for g, group in enumerate(GRAD_GROUPS):
        for n, r in zip(group, _reduce_scatter_finish(grads_started[g], after, name=f"grads{g}")):
            if n[:-2] in ("mlp_w1", "mlp_w2"):
                after = update(n[:-2], r, layer=int(n[-1]), tag=n[-2:])
            elif n == "even_w_in":
                after = update(n, r[:, :e_w], by_column=True)
            elif n == "odd_w_in":
                after = update(n, r[:, :o_w], by_column=True)
            else:
                after = update(n, r)
        if g == len(GRAD_GROUPS) - 2:
            g_ln1g, g_ln1b, g_ln2g, g_ln2b, g_sinks, g_qn, g_kvn = _unpack_small(
                small_sum(after), [(DEPTH, D_MODEL)] * 4 + [(1, A_Q_HEADS), (D_Q_RANK,), (D_KV_RANK,)])
            for n, gs in (("even_sinks", g_sinks), ("odd_q_norm_g", lax.dynamic_slice_in_dim(g_qn, chip * qn_w, qn_w)[None]),
                          ("odd_kv_norm_g", lax.dynamic_slice_in_dim(g_kvn, chip * kvn_w, kvn_w)[None]), ("ln1_g", g_ln1g),
                          ("ln1_b", g_ln1b), ("ln2_g", g_ln2g), ("ln2_b", g_ln2b)):
                update(n, gs)
    outs = [[updated[n][k].reshape(weights[n].shape) for n in order] for k in range(4)]
    return (loss, grad_x[None], *outs[0], *outs[1], *outs[2], *outs[3])
```

```python
import math

import jax
import jax.numpy as jnp
from jax import lax
from jax.experimental import pallas as pl
from jax.experimental.pallas import tpu as pltpu

F32 = jnp.float32
BF16 = jnp.bfloat16
MESH = pl.DeviceIdType.MESH

D_MODEL = 2048
SEQ = 2048
DEPTH = 2
HEAD_DIM = 64
A_Q_HEADS, A_KV_HEADS, A_WINDOW = 16, 2, 128
B_HEADS = 8
B_PATTERNS = ((128, 1), (512, 4), (2048, 16))
C_HEADS = 16
D_HEADS, D_Q_RANK, D_KV_RANK, D_NOPE, D_ROPE, D_V = 16, 512, 256, 64, 32, 64
ROPE_BASE = 10000.0
D_FF = 4 * D_MODEL
LN_EPS = 1e-5
RMS_EPS = 1e-6
ALPHA = (2 * DEPTH) ** 0.25
A_Q_W = A_Q_HEADS * HEAD_DIM
A_KV_W = A_KV_HEADS * HEAD_DIM
B_W = B_HEADS * HEAD_DIM
EVEN_IN = A_Q_W + 2 * A_KV_W + 3 * B_W * len(B_PATTERNS)
EVEN_OUT = A_Q_W + B_W
C_W = C_HEADS * HEAD_DIM
ODD_IN = 3 * C_W + D_Q_RANK + D_KV_RANK + D_ROPE
ADAM_LR, ADAM_B1, ADAM_B2, ADAM_EPS, ADAM_WD, ADAM_STEP = 0.001, 0.9, 0.999, 1e-08, 0.01, 10

N_CHIPS = 4
EVEN_IN_PAD = 1536
ODD_IN_PAD = 1024
ATT_BLK = 128
NEG = -1e30
V7X_VMEM_LIMIT = 48 * 1024 * 1024


def _params(*sem):
    return pltpu.CompilerParams(dimension_semantics=sem, vmem_limit_bytes=V7X_VMEM_LIMIT)


def _tile(n, cap):
    if n <= cap:
        return n
    t = cap - cap % 128
    while n % t:
        t -= 128
    return t


def _matmul(a, b, *, mode, out_dtype, name, epilogue=None, extra=None, alpha=1.0, after=None, tm=1024, tn=1024, tk=2048):
    if mode == "nn":
        M, K = a.shape
        P, _, Np = b.shape
        tm, tn, tk = _tile(M, tm), _tile(Np, tn), _tile(K, tk)
        nj = Np // tn
        grid = (M // tm, P * nj, K // tk)
        a_spec = pl.BlockSpec((tm, tk), lambda i, j, k: (i, k))
        b_spec = pl.BlockSpec((None, tk, tn), lambda i, j, k: (j // nj, k, j % nj))
        o_spec = pl.BlockSpec((tm, tn), lambda i, j, k: (i, j))
        out_shape = (M, P * Np)
        dims = (((1,), (0,)), ((), ()))
    elif mode == "nt":
        M, N = a.shape
        P, K, Np = b.shape
        tm, tn, tk = _tile(M, tm), _tile(K, tn), _tile(Np, tk)
        nkk = Np // tk
        grid = (M // tm, K // tn, P * nkk)
        a_spec = pl.BlockSpec((tm, tk), lambda i, j, k: (i, k))
        b_spec = pl.BlockSpec((None, tn, tk), lambda i, j, k: (k // nkk, j, k % nkk))
        o_spec = pl.BlockSpec((tm, tn), lambda i, j, k: (i, j))
        out_shape = (M, K)
        dims = (((1,), (1,)), ((), ()))
    else:
        raise ValueError(mode)
    n_k = grid[2]
    n_extra = 0 if extra is None else 1
    n_after = 0 if after is None else 1
    n_out = 2 if epilogue == "relu2" else 1

    def body(*refs):
        a_ref, b_ref = refs[:2]
        e_ref = refs[2] if n_extra else None
        outs = refs[2 + n_extra + n_after:2 + n_extra + n_after + n_out]
        part = lax.dot_general(a_ref[...], b_ref[...], dims, preferred_element_type=F32)

        def finish(r):
            if epilogue == "relu2":
                outs[0][...] = r.astype(outs[0].dtype)
                rp = jnp.maximum(r, 0.0)
                outs[1][...] = (rp * rp).astype(outs[1].dtype)
            elif epilogue == "drelu2":
                outs[0][...] = (r * (2.0 * jnp.maximum(e_ref[...].astype(F32), 0.0))).astype(outs[0].dtype)
            elif epilogue == "add":
                outs[0][...] = (r + alpha * e_ref[...].astype(F32)).astype(outs[0].dtype)
            else:
                outs[0][...] = r.astype(outs[0].dtype)

        if n_k == 1:
            finish(part)
        else:
            acc = refs[-1]
            k = pl.program_id(2)

            @pl.when(k == 0)
            def _():
                acc[...] = part

            @pl.when((k > 0) & (k < n_k - 1))
            def _():
                acc[...] += part

            @pl.when(k == n_k - 1)
            def _():
                finish(acc[...] + part)

    in_specs = [a_spec, b_spec] + ([o_spec] if n_extra else []) + ([pl.BlockSpec(memory_space=pl.ANY)] if n_after else [])
    shapes = [jax.ShapeDtypeStruct(out_shape, out_dtype)] * n_out
    res = pl.pallas_call(
        body, name=name, grid=grid, in_specs=in_specs,
        out_specs=[o_spec] * n_out, out_shape=shapes,
        scratch_shapes=[pltpu.VMEM((tm, tn), F32)] if n_k > 1 else [],
        compiler_params=_params("parallel", "parallel", "arbitrary"),
    )(a, b, *([extra] if n_extra else []), *([after] if n_after else []))
    return res if n_out > 1 else res[0]


def _matmul_tn(a, g, *, shards, name, tm=1024, tn=1024):
    M, K = a.shape
    P = shards
    Np = g.shape[1] // P
    tm, tn = _tile(K, tm), _tile(Np, tn)
    nj = Np // tn

    def body(a_ref, g_ref, o_ref):
        o_ref[...] = lax.dot_general(a_ref[...], g_ref[...], (((0,), (0,)), ((), ())), preferred_element_type=F32).astype(BF16)

    return pl.pallas_call(
        body, name=name, grid=(K // tm, P * nj),
        in_specs=[pl.BlockSpec((M, tm), lambda i, j: (0, i)), pl.BlockSpec((M, tn), lambda i, j: (0, j))],
        out_specs=pl.BlockSpec((None, tm, tn), lambda i, j: (j // nj, i, j % nj)),
        out_shape=jax.ShapeDtypeStruct((P, K, Np), BF16),
        compiler_params=_params("parallel", "parallel"),
    )(a, g)


def _norm_fwd(x, res, g, b, *, center, eps, alpha, name, rows=256):
    S, W = x.shape
    has_res = res is not None
    rows = _tile(S, rows)

    def body(*refs):
        x_ref = refs[0]
        r_ref = refs[1] if has_res else None
        g_ref = refs[1 + has_res]
        b_ref = refs[2 + has_res] if center else None
        y_ref, y16_ref, u_ref = refs[-3:]
        u = x_ref[...]
        if has_res:
            u = alpha * u + r_ref[...]
        if center:
            mu = jnp.mean(u, axis=1, keepdims=True)
            xc = u - mu
        else:
            xc = u
        var = jnp.mean(xc * xc, axis=1, keepdims=True)
        y = xc * lax.rsqrt(var + eps) * g_ref[...]
        if center:
            y = y + b_ref[...]
        y_ref[...] = y
        y16_ref[...] = y.astype(BF16)
        u_ref[...] = u

    row_spec = pl.BlockSpec((rows, W), lambda i: (i, 0))
    vec_spec = pl.BlockSpec((1, W), lambda i: (0, 0))
    ins = [x] + ([res] if has_res else []) + [g.reshape(1, W)] + ([b.reshape(1, W)] if center else [])
    specs = [row_spec] + ([row_spec] if has_res else []) + [vec_spec] + ([vec_spec] if center else [])
    return pl.pallas_call(
        body, name=name, grid=(S // rows,), in_specs=specs,
        out_specs=[row_spec, row_spec, row_spec],
        out_shape=[jax.ShapeDtypeStruct((S, W), F32), jax.ShapeDtypeStruct((S, W), BF16), jax.ShapeDtypeStruct((S, W), F32)],
        compiler_params=_params("parallel"),
    )(*ins)


def _norm_bwd(dy, u, g, *, center, eps, name, rows=256, after=None):
    S, W = u.shape
    rows = _tile(S, rows)

    def body(dy_ref, u_ref, g_ref, *rest):
        du_ref, du16_ref, dg_ref, db_ref = rest[-4:]
        i = pl.program_id(0)
        uu = u_ref[...]
        dyv = dy_ref[...]
        if center:
            xc = uu - jnp.mean(uu, axis=1, keepdims=True)
        else:
            xc = uu
        rstd = lax.rsqrt(jnp.mean(xc * xc, axis=1, keepdims=True) + eps)
        xhat = xc * rstd
        dxh = dyv * g_ref[...]
        c2 = jnp.mean(dxh * xhat, axis=1, keepdims=True)
        du = dxh - xhat * c2
        if center:
            du = du - jnp.mean(dxh, axis=1, keepdims=True)
        du = du * rstd
        du_ref[...] = du
        du16_ref[...] = du.astype(BF16)

        @pl.when(i == 0)
        def _():
            dg_ref[...] = jnp.zeros_like(dg_ref)
            db_ref[...] = jnp.zeros_like(db_ref)

        dg_ref[...] += jnp.sum(dyv * xhat, axis=0, keepdims=True)
        db_ref[...] += jnp.sum(dyv, axis=0, keepdims=True)

    row_spec = pl.BlockSpec((rows, W), lambda i: (i, 0))
    vec_spec = pl.BlockSpec((1, W), lambda i: (0, 0))
    return pl.pallas_call(
        body, name=name, grid=(S // rows,),
        in_specs=[row_spec, row_spec, vec_spec] + ([] if after is None else [pl.BlockSpec(memory_space=pl.ANY)]),
        out_specs=[row_spec, row_spec, vec_spec, vec_spec],
        out_shape=[jax.ShapeDtypeStruct((S, W), F32), jax.ShapeDtypeStruct((S, W), BF16),
                   jax.ShapeDtypeStruct((1, W), F32), jax.ShapeDtypeStruct((1, W), F32)],
        compiler_params=_params("arbitrary"),
    )(dy, u, g.reshape(1, W), *([] if after is None else [after]))


def _loss_head(y, target, *, name, rows=256):
    S, W = y.shape
    rows = _tile(S, rows)

    def body(y_ref, t_ref, dy_ref, l_ref):
        i = pl.program_id(0)
        e = y_ref[...] - t_ref[...]
        dy_ref[...] = e * (1.0 / W)

        @pl.when(i == 0)
        def _():
            l_ref[...] = jnp.zeros_like(l_ref)

        l_ref[...] += jnp.full(l_ref.shape, 0.5 / W * jnp.sum(e * e), F32)

    row_spec = pl.BlockSpec((rows, W), lambda i: (i, 0))
    return pl.pallas_call(
        body, name=name, grid=(S // rows,), in_specs=[row_spec, row_spec],
        out_specs=[row_spec, pl.BlockSpec((1, 128), lambda i: (0, 0))],
        out_shape=[jax.ShapeDtypeStruct((S, W), F32), jax.ShapeDtypeStruct((1, 128), F32)],
        compiler_params=_params("arbitrary"),
    )(y, target)


def _adamw(w, g, m, v, *, name, layer=0, carry=None, rows=256):
    L, R, C = w.shape
    rows = max(t for t in range(8, rows + 1, 8) if R % t == 0) if R % 8 == 0 else R
    c1 = 1.0 / (1.0 - ADAM_B1 ** ADAM_STEP)
    c2 = 1.0 / (1.0 - ADAM_B2 ** ADAM_STEP)

    def body(w_ref, g_ref, m_ref, v_ref, *rest):
        go_ref, d_ref, mo_ref, vo_ref = rest[-4:]
        gg = g_ref[...]
        mn = ADAM_B1 * m_ref[...] + (1.0 - ADAM_B1) * gg
        vn = ADAM_B2 * v_ref[...] + (1.0 - ADAM_B2) * (gg * gg)
        go_ref[...] = gg
        d_ref[...] = -ADAM_LR * ((mn * c1) / (jnp.sqrt(vn * c2) + ADAM_EPS) + ADAM_WD * w_ref[...])
        mo_ref[...] = mn
        vo_ref[...] = vn

    slab = pl.BlockSpec((None, rows, C), lambda i: (layer, i, 0))
    shp = jax.ShapeDtypeStruct((L, R, C), F32)
    carried = [] if carry is None else list(carry)
    return pl.pallas_call(
        body, name=name, grid=(R // rows,),
        in_specs=[slab, pl.BlockSpec((rows, C), lambda i: (i, 0)), slab, slab] + [pl.BlockSpec(memory_space=pl.ANY)] * len(carried),
        out_specs=[slab] * 4, out_shape=[shp] * 4, input_output_aliases={4 + k: k for k in range(len(carried))},
        compiler_params=_params("parallel"),
    )(w, g, m, v, *carried)


def _rope(x1, x2, cos, sin, *, name, rows=512):
    H, S, W = x1.shape
    rows = _tile(S, rows)

    def body(x1_ref, x2_ref, c_ref, s_ref, y1_ref, y2_ref):
        t1 = jnp.sum(x1_ref[...], axis=0)
        t2 = jnp.sum(x2_ref[...], axis=0)
        c = c_ref[...]
        s = s_ref[...]
        y1_ref[...] = t1 * c - t2 * s
        y2_ref[...] = t1 * s + t2 * c

    xs = pl.BlockSpec((H, rows, W), lambda i: (0, i, 0))
    ts = pl.BlockSpec((rows, W), lambda i: (i, 0))
    shp = jax.ShapeDtypeStruct((S, W), F32)
    return pl.pallas_call(
        body, name=name, grid=(S // rows,), in_specs=[xs, xs, ts, ts], out_specs=[ts, ts], out_shape=[shp, shp],
        compiler_params=_params("parallel"),
    )(x1, x2, cos, sin)


def _band_mask(blk, n, n_back):
    row = lax.broadcasted_iota(jnp.int32, (blk, 2 * blk), 0)
    col = lax.broadcasted_iota(jnp.int32, (blk, 2 * blk), 1)
    rel = blk + row - col
    return rel, (rel >= 0) & (rel <= n_back) & ((col >= blk) | (n >= 1))


def _window(ref, head, i, blk):
    before = pl.multiple_of(jnp.maximum(i - 1, 0) * blk, blk)
    return jnp.concatenate([ref[head, pl.ds(before, blk), :], ref[head, pl.ds(pl.multiple_of(i * blk, blk), blk), :]], axis=0)


def _window_add(ref, head, i, blk, update):
    before = pl.multiple_of(jnp.maximum(i - 1, 0) * blk, blk)
    ref[head, pl.ds(before, blk), :] += update[:blk]
    ref[head, pl.ds(pl.multiple_of(i * blk, blk), blk), :] += update[blk:]


def _band_fwd(q, k, v, slopes, sinks, *, hps, nb_seq, n_back, scale, dist_scale, name):
    blk = ATT_BLK
    Hq, T, d = q.shape
    Hk = k.shape[0]
    group = Hq // Hk
    kps = max(hps // group, 1)
    use_sink = sinks is not None

    def body(*refs):
        q_ref, k_ref, v_ref, slope_ref = refs[:4]
        sink_ref = refs[4] if use_sink else None
        o_ref, lse_ref = refs[-2:]
        i = pl.program_id(1)
        rel, valid = _band_mask(blk, i % nb_seq, n_back)
        relf = rel.astype(F32)
        for hh in range(hps):
            h = pl.program_id(0) * hps + hh
            kk = _window(k_ref, hh // group, i, blk)
            vv = _window(v_ref, hh // group, i, blk)
            s = lax.dot_general(q_ref[hh], kk, (((1,), (1,)), ((), ())), preferred_element_type=F32) * scale
            s = jnp.where(valid, s - (slope_ref[h] * dist_scale) * relf, NEG)
            m = jnp.max(s, axis=1, keepdims=True)
            if use_sink:
                m = jnp.maximum(m, sink_ref[h])
            p = jnp.where(valid, jnp.exp(s - m), 0.0)
            l = jnp.sum(p, axis=1, keepdims=True)
            if use_sink:
                l = l + jnp.exp(sink_ref[h] - m)
            o_ref[hh] = jnp.dot(p.astype(BF16), vv, preferred_element_type=F32) / l
            lse_ref[hh] = m + jnp.log(l)

    smem = pl.BlockSpec(memory_space=pltpu.SMEM)
    qs = pl.BlockSpec((hps, blk, d), lambda g, i: (g, i, 0))
    ks = pl.BlockSpec((kps, T, d), lambda g, i: (g, 0, 0))
    ins = [q, k, v, slopes] + ([sinks] if use_sink else [])
    return pl.pallas_call(
        body, name=name, grid=(Hq // hps, T // blk), in_specs=[qs, ks, ks, smem] + ([smem] if use_sink else []),
        out_specs=[qs, pl.BlockSpec((hps, blk, 1), lambda g, i: (g, i, 0))],
        out_shape=[jax.ShapeDtypeStruct((Hq, T, d), F32), jax.ShapeDtypeStruct((Hq, T, 1), F32)],
        compiler_params=_params("parallel", "parallel"),
    )(*ins)


def _band_bwd(q, k, v, o, do, lse, dlse, slopes, sinks, *, hps, nb_seq, n_back, scale, dist_scale, name):
    blk = ATT_BLK
    Hq, T, d = q.shape
    Hk = k.shape[0]
    group = Hq // Hk
    kps = max(hps // group, 1)
    use_sink, use_dlse = sinks is not None, dlse is not None

    def body(*refs):
        q_ref, k_ref, v_ref, o_ref, do_ref, lse_ref = refs[:6]
        pos = 6
        dlse_ref = refs[pos] if use_dlse else None
        pos += use_dlse
        slope_ref = refs[pos]
        pos += 1
        sink_ref = refs[pos] if use_sink else None
        pos += use_sink
        dq_ref, dk_ref, dv_ref = refs[pos:pos + 3]
        dsink_ref = refs[pos + 3] if use_sink else None
        i = pl.program_id(1)
        rel, valid = _band_mask(blk, i % nb_seq, n_back)
        relf = rel.astype(F32)

        @pl.when(i == 0)
        def _():
            dk_ref[...] = jnp.zeros_like(dk_ref)
            dv_ref[...] = jnp.zeros_like(dv_ref)
            if use_sink:
                dsink_ref[...] = jnp.zeros_like(dsink_ref)

        for kh in range(kps):
            dk_acc = jnp.zeros((2 * blk, d), F32)
            dv_acc = jnp.zeros((2 * blk, d), F32)
            kk = _window(k_ref, kh, i, blk)
            vv = _window(v_ref, kh, i, blk)
            for hh in range(kh * min(group, hps), (kh + 1) * min(group, hps)):
                h = pl.program_id(0) * hps + hh
                qb = q_ref[hh]
                dob = do_ref[hh]
                do16 = dob.astype(BF16)
                lse = lse_ref[hh]
                c = -jnp.sum(dob * o_ref[hh], axis=1, keepdims=True)
                if use_dlse:
                    c = c + dlse_ref[hh]
                s = lax.dot_general(qb, kk, (((1,), (1,)), ((), ())), preferred_element_type=F32) * scale
                s = jnp.where(valid, s - (slope_ref[h] * dist_scale) * relf, NEG)
                p = jnp.where(valid, jnp.exp(s - lse), 0.0)
                dp = lax.dot_general(do16, vv, (((1,), (1,)), ((), ())), preferred_element_type=F32)
                ds16 = (p * (dp + c) * scale).astype(BF16)
                dq_ref[hh] = jnp.dot(ds16, kk, preferred_element_type=F32)
                dk_acc = dk_acc + lax.dot_general(ds16, qb, (((0,), (0,)), ((), ())), preferred_element_type=F32)
                dv_acc = dv_acc + lax.dot_general(p.astype(BF16), do16, (((0,), (0,)), ((), ())), preferred_element_type=F32)
                if use_sink:
                    dsink_ref[hh] += jnp.full((1, 128), jnp.sum(jnp.exp(sink_ref[h] - lse) * c), F32)
            _window_add(dk_ref, kh, i, blk, dk_acc)
            _window_add(dv_ref, kh, i, blk, dv_acc)

    smem = pl.BlockSpec(memory_space=pltpu.SMEM)
    qs = pl.BlockSpec((hps, blk, d), lambda g, i: (g, i, 0))
    ls = pl.BlockSpec((hps, blk, 1), lambda g, i: (g, i, 0))
    ks = pl.BlockSpec((kps, T, d), lambda g, i: (g, 0, 0))
    in_specs = [qs, ks, ks, qs, qs, ls] + ([ls] if use_dlse else []) + [smem] + ([smem] if use_sink else [])
    ins = [q, k, v, o, do, lse] + ([dlse] if use_dlse else []) + [slopes] + ([sinks] if use_sink else [])
    out_specs = [qs, ks, ks]
    out_shape = [jax.ShapeDtypeStruct((Hq, T, d), F32), jax.ShapeDtypeStruct((Hk, T, d), F32),
                 jax.ShapeDtypeStruct((Hk, T, d), F32)]
    if use_sink:
        out_specs.append(pl.BlockSpec((hps, 1, 128), lambda g, i: (g, 0, 0)))
        out_shape.append(jax.ShapeDtypeStruct((Hq, 1, 128), F32))
    return pl.pallas_call(
        body, name=name, grid=(Hq // hps, T // blk), in_specs=in_specs, out_specs=out_specs, out_shape=out_shape,
        compiler_params=_params("parallel", "arbitrary"),
    )(*ins)


def _diagonal_mask(blk):
    return lax.broadcasted_iota(jnp.int32, (blk, blk), 0) >= lax.broadcasted_iota(jnp.int32, (blk, blk), 1)


def _causal_fwd(q, k, v, *, blk, hps, scale, name):
    H, T, dqk = q.shape
    dv = v.shape[2]

    def body(q_ref, k_ref, v_ref, o_ref, lse_ref):
        n = pl.program_id(1)
        qs = [q_ref[hh] for hh in range(hps)]

        def block(j, carry, valid):
            start = pl.multiple_of(j * blk, blk)
            out = []
            for hh in range(hps):
                m, l, acc = carry[hh]
                kb = k_ref[hh, pl.ds(start, blk), :]
                vb = v_ref[hh, pl.ds(start, blk), :]
                s = lax.dot_general(qs[hh], kb, (((1,), (1,)), ((), ())), preferred_element_type=F32) * scale
                if valid is not None:
                    s = jnp.where(valid, s, NEG)
                m_new = jnp.maximum(m, jnp.max(s, axis=1, keepdims=True))
                p = _keep(valid, jnp.exp(s - m_new))
                a = jnp.exp(m - m_new)
                out.append((m_new, a * l + jnp.sum(p, axis=1, keepdims=True),
                            a * acc + jnp.dot(p.astype(BF16), vb, preferred_element_type=F32)))
            return tuple(out)

        init = tuple((jnp.full((blk, 1), NEG, F32), jnp.zeros((blk, 1), F32), jnp.zeros((blk, dv), F32)) for _ in range(hps))
        res = block(n, lax.fori_loop(0, n, lambda j, carry: block(j, carry, None), init), _diagonal_mask(blk))
        for hh in range(hps):
            m, l, acc = res[hh]
            o_ref[hh] = acc / l
            lse_ref[hh] = m + jnp.log(l)

    qs_ = pl.BlockSpec((hps, blk, dqk), lambda g, i: (g, i, 0))
    return pl.pallas_call(
        body, name=name, grid=(H // hps, T // blk),
        in_specs=[qs_, pl.BlockSpec((hps, T, dqk), lambda g, i: (g, 0, 0)), pl.BlockSpec((hps, T, dv), lambda g, i: (g, 0, 0))],
        out_specs=[pl.BlockSpec((hps, blk, dv), lambda g, i: (g, i, 0)), pl.BlockSpec((hps, blk, 1), lambda g, i: (g, i, 0))],
        out_shape=[jax.ShapeDtypeStruct((H, T, dv), F32), jax.ShapeDtypeStruct((H, T, 1), F32)],
        compiler_params=_params("parallel", "parallel"),
    )(q, k, v)


def _causal_bwd(q, k, v, o, do, lse, *, blk, hps, scale, name):
    H, T, dqk = q.shape
    dv = v.shape[2]

    def body(q_ref, k_ref, v_ref, o_ref, do_ref, lse_ref, dq_ref, dk_ref, dv_ref):
        n = pl.program_id(1)

        @pl.when(n == 0)
        def _():
            dk_ref[...] = jnp.zeros_like(dk_ref)
            dv_ref[...] = jnp.zeros_like(dv_ref)

        qs = [q_ref[hh] for hh in range(hps)]
        do16 = [do_ref[hh].astype(BF16) for hh in range(hps)]
        lses = [lse_ref[hh] for hh in range(hps)]
        cs = [-jnp.sum(do_ref[hh] * o_ref[hh], axis=1, keepdims=True) for hh in range(hps)]

        def block(j, dqs, valid):
            start = pl.multiple_of(j * blk, blk)
            out = []
            for hh in range(hps):
                kb = k_ref[hh, pl.ds(start, blk), :]
                vb = v_ref[hh, pl.ds(start, blk), :]
                s = lax.dot_general(qs[hh], kb, (((1,), (1,)), ((), ())), preferred_element_type=F32) * scale
                if valid is not None:
                    s = jnp.where(valid, s, NEG)
                p = _keep(valid, jnp.exp(s - lses[hh]))
                dp = lax.dot_general(do16[hh], vb, (((1,), (1,)), ((), ())), preferred_element_type=F32)
                ds16 = (p * (dp + cs[hh]) * scale).astype(BF16)
                dk_ref[hh, pl.ds(start, blk), :] += lax.dot_general(ds16, qs[hh], (((0,), (0,)), ((), ())), preferred_element_type=F32)
                dv_ref[hh, pl.ds(start, blk), :] += lax.dot_general(p.astype(BF16), do16[hh], (((0,), (0,)), ((), ())),
                                                                    preferred_element_type=F32)
                out.append(dqs[hh] + jnp.dot(ds16, kb, preferred_element_type=F32))
            return tuple(out)

        init = tuple(jnp.zeros((blk, dqk), F32) for _ in range(hps))
        res = block(n, lax.fori_loop(0, n, lambda j, dqs: block(j, dqs, None), init), _diagonal_mask(blk))
        for hh in range(hps):
            dq_ref[hh] = res[hh]

    qs_ = pl.BlockSpec((hps, blk, dqk), lambda g, i: (g, i, 0))
    os_ = pl.BlockSpec((hps, blk, dv), lambda g, i: (g, i, 0))
    ls_ = pl.BlockSpec((hps, blk, 1), lambda g, i: (g, i, 0))
    ks_ = pl.BlockSpec((hps, T, dqk), lambda g, i: (g, 0, 0))
    vs_ = pl.BlockSpec((hps, T, dv), lambda g, i: (g, 0, 0))
    return pl.pallas_call(
        body, name=name, grid=(H // hps, T // blk), in_specs=[qs_, ks_, vs_, os_, os_, ls_], out_specs=[qs_, ks_, vs_],
        out_shape=[jax.ShapeDtypeStruct((H, T, dqk), F32), jax.ShapeDtypeStruct((H, T, dqk), F32),
                   jax.ShapeDtypeStruct((H, T, dv), F32)],
        compiler_params=_params("parallel", "arbitrary"),
    )(q, k, v, o, do, lse)


def _tri_sum(x, upper):
    hi = x.astype(BF16)
    lo = (x - hi.astype(F32)).astype(BF16)
    return jnp.dot(hi, upper, preferred_element_type=F32) + jnp.dot(lo, upper, preferred_element_type=F32)


def _keep(mask, x):
    return x if mask is None else jnp.where(mask, x, 0.0)


def _stick_terms(qb, kb, scale, strict):
    z = lax.dot_general(qb, kb, (((1,), (1,)), ((), ())), preferred_element_type=F32) * scale
    e = jnp.exp(-jnp.abs(z))
    lb = jnp.minimum(z, 0.0) - jnp.log(1.0 + e)
    return z, e, lb, _keep(strict, lb - z)


def _stick_fwd(q, k, v, *, blk, hps, scale, name):
    H, T, dh = q.shape

    def body(q_ref, k_ref, v_ref, o_ref, tot_ref):
        n = pl.program_id(1)
        qs = [q_ref[hh] for hh in range(hps)]
        r_i = lax.broadcasted_iota(jnp.int32, (blk, blk), 0)
        c_i = lax.broadcasted_iota(jnp.int32, (blk, blk), 1)
        upper = (r_i > c_i).astype(BF16)

        def block(j, carry, strict):
            start = pl.multiple_of(j * blk, blk)
            out = []
            for hh in range(hps):
                run, acc = carry[hh]
                kb = k_ref[hh, pl.ds(start, blk), :]
                vb = v_ref[hh, pl.ds(start, blk), :]
                _, _, lb, lk = _stick_terms(qs[hh], kb, scale, strict)
                w = _keep(strict, jnp.exp(lb + run + _tri_sum(lk, upper)))
                out.append((run + jnp.sum(lk, axis=1, keepdims=True), acc + jnp.dot(w.astype(BF16), vb, preferred_element_type=F32)))
            return tuple(out)

        init = tuple((jnp.zeros((blk, 1), F32), jnp.zeros((blk, dh), F32)) for _ in range(hps))
        res = lax.fori_loop(1, n + 1, lambda t, carry: block(n - t, carry, None), block(n, init, r_i > c_i))
        for hh in range(hps):
            tot_ref[hh], o_ref[hh] = res[hh]

    qs_ = pl.BlockSpec((hps, blk, dh), lambda g, i: (g, i, 0))
    ks_ = pl.BlockSpec((hps, T, dh), lambda g, i: (g, 0, 0))
    ts_ = pl.BlockSpec((hps, blk, 1), lambda g, i: (g, i, 0))
    return pl.pallas_call(
        body, name=name, grid=(H // hps, T // blk), in_specs=[qs_, ks_, ks_], out_specs=[qs_, ts_],
        out_shape=[jax.ShapeDtypeStruct((H, T, dh), F32), jax.ShapeDtypeStruct((H, T, 1), F32)],
        compiler_params=_params("parallel", "parallel"),
    )(q, k, v)


def _stick_bwd(q, k, v, tot, do, *, blk, hps, scale, name):
    H, T, dh = q.shape

    def body(q_ref, k_ref, v_ref, tot_ref, do_ref, dq_ref, dk_ref, dv_ref):
        n = pl.program_id(1)
        qs = [q_ref[hh] for hh in range(hps)]
        do16 = [do_ref[hh].astype(BF16) for hh in range(hps)]
        tots = [tot_ref[hh] for hh in range(hps)]
        r_i = lax.broadcasted_iota(jnp.int32, (blk, blk), 0)
        c_i = lax.broadcasted_iota(jnp.int32, (blk, blk), 1)
        upto = (r_i <= c_i).astype(BF16)
        below = (r_i < c_i).astype(BF16)

        @pl.when(n == 0)
        def _():
            dk_ref[...] = jnp.zeros_like(dk_ref)
            dv_ref[...] = jnp.zeros_like(dv_ref)

        def block(j, carry, strict):
            start = pl.multiple_of(j * blk, blk)
            out = []
            for hh in range(hps):
                run, grun, dq = carry[hh]
                kb = k_ref[hh, pl.ds(start, blk), :]
                vb = v_ref[hh, pl.ds(start, blk), :]
                z, e, lb, lk = _stick_terms(qs[hh], kb, scale, strict)
                w = _keep(strict, jnp.exp(lb + tots[hh] - (run + _tri_sum(lk, upto))))
                dw = lax.dot_general(do16[hh], vb, (((1,), (1,)), ((), ())), preferred_element_type=F32)
                g = w * dw
                before = grun + _tri_sum(g, below)
                inv = 1.0 / (1.0 + e)
                sig = jnp.where(z >= 0, inv, e * inv)
                dz16 = (_keep(strict, g * (1.0 - sig) - sig * before) * scale).astype(BF16)
                dk_ref[hh, pl.ds(start, blk), :] += lax.dot_general(dz16, qs[hh], (((0,), (0,)), ((), ())), preferred_element_type=F32)
                dv_ref[hh, pl.ds(start, blk), :] += lax.dot_general(w.astype(BF16), do16[hh], (((0,), (0,)), ((), ())),
                                                                    preferred_element_type=F32)
                out.append((run + jnp.sum(lk, axis=1, keepdims=True), grun + jnp.sum(g, axis=1, keepdims=True),
                            dq + jnp.dot(dz16, kb, preferred_element_type=F32)))
            return tuple(out)

        zero = jnp.zeros((blk, 1), F32)
        init = tuple((zero, zero, jnp.zeros((blk, dh), F32)) for _ in range(hps))
        res = block(n, lax.fori_loop(0, n, lambda j, carry: block(j, carry, None), init), r_i > c_i)
        for hh in range(hps):
            dq_ref[hh] = res[hh][2]

    qs_ = pl.BlockSpec((hps, blk, dh), lambda g, i: (g, i, 0))
    ks_ = pl.BlockSpec((hps, T, dh), lambda g, i: (g, 0, 0))
    ts_ = pl.BlockSpec((hps, blk, 1), lambda g, i: (g, i, 0))
    shp = jax.ShapeDtypeStruct((H, T, dh), F32)
    return pl.pallas_call(
        body, name=name, grid=(H // hps, T // blk), in_specs=[qs_, ks_, ks_, ts_, qs_], out_specs=[qs_, ks_, ks_],
        out_shape=[shp, shp, shp],
        compiler_params=_params("parallel", "arbitrary"),
    )(q, k, v, tot, do)


def _mix_fwd(outs, lses, *, name, rows=512):
    H, T, dh = outs[0].shape
    rows = _tile(T, rows)

    def body(o0, o1, o2, l0, l1, l2, y_ref):
        ls = [l0[...], l1[...], l2[...]]
        mx = jnp.maximum(jnp.maximum(ls[0], ls[1]), ls[2])
        es = [jnp.exp(x - mx) for x in ls]
        den = es[0] + es[1] + es[2]
        y_ref[...] = (es[0] * o0[...] + es[1] * o1[...] + es[2] * o2[...]) / den

    os_ = pl.BlockSpec((None, rows, dh), lambda h, i: (h, i, 0))
    ls_ = pl.BlockSpec((None, rows, 1), lambda h, i: (h, i, 0))
    return pl.pallas_call(
        body, name=name, grid=(H, T // rows), in_specs=[os_] * 3 + [ls_] * 3, out_specs=os_,
        out_shape=jax.ShapeDtypeStruct((H, T, dh), F32),
        compiler_params=_params("parallel", "parallel"),
    )(*outs, *lses)


def _mix_bwd(outs, lses, dy, *, name, rows=512):
    H, T, dh = outs[0].shape
    rows = _tile(T, rows)

    def body(o0, o1, o2, l0, l1, l2, dy_ref, d0, d1, d2, g0, g1, g2):
        ls = [l0[...], l1[...], l2[...]]
        mx = jnp.maximum(jnp.maximum(ls[0], ls[1]), ls[2])
        es = [jnp.exp(x - mx) for x in ls]
        den = es[0] + es[1] + es[2]
        mix = [e / den for e in es]
        dyv = dy_ref[...]
        dm = [jnp.sum(dyv * o[...], axis=1, keepdims=True) for o in (o0, o1, o2)]
        avg = mix[0] * dm[0] + mix[1] * dm[1] + mix[2] * dm[2]
        for d_ref, g_ref, w, t in zip((d0, d1, d2), (g0, g1, g2), mix, dm):
            d_ref[...] = w * dyv
            g_ref[...] = w * (t - avg)

    os_ = pl.BlockSpec((None, rows, dh), lambda h, i: (h, i, 0))
    ls_ = pl.BlockSpec((None, rows, 1), lambda h, i: (h, i, 0))
    so = jax.ShapeDtypeStruct((H, T, dh), F32)
    sl = jax.ShapeDtypeStruct((H, T, 1), F32)
    res = pl.pallas_call(
        body, name=name, grid=(H, T // rows), in_specs=[os_] * 3 + [ls_] * 3 + [os_], out_specs=[os_] * 3 + [ls_] * 3,
        out_shape=[so] * 3 + [sl] * 3,
        compiler_params=_params("parallel", "parallel"),
    )(*outs, *lses, dy)
    return res[:3], res[3:]


def _position():
    return lax.axis_index("x"), lax.axis_index("y"), lax.axis_index("c")


def _other_chips(x, y):
    return [(1 - x, y), (x, 1 - y), (1 - x, 1 - y)]


HBM_SPEC = pl.BlockSpec(memory_space=pltpu.HBM)
SEM_SPEC = pl.BlockSpec(memory_space=pltpu.SEMAPHORE)
ANY_SPEC = pl.BlockSpec(memory_space=pl.ANY)
SPLIT_COPY = pltpu.CompilerParams(has_side_effects=pltpu.SideEffectType.DATAFLOW_SIDE_EFFECTING)


def _in_hbm(a):
    return pltpu.with_memory_space_constraint(a, pltpu.HBM)


SAME_CORE_OF_OTHER_CHIPS = ((1, 0, 0), (0, 1, 0), (1, 1, 0))
ALL_OTHER_DEVICES = ((0, 0, 1), (1, 0, 0), (0, 1, 0), (1, 1, 0), (1, 0, 1), (0, 1, 1), (1, 1, 1))


def _peer_copies(srcs, lands, send_sems, recv_sems, relations, src_of, dst_of):
    me = _position()
    copies = []
    for w in range(len(srcs)):
        for k, flips in enumerate(relations):
            peer = tuple(1 - p if f else p for p, f in zip(me, flips))
            i = len(relations) * w + k
            copies.append(pltpu.make_async_remote_copy(
                src_ref=src_of(srcs[w], peer), dst_ref=dst_of(lands[w], k, peer, me), send_sem=send_sems[i],
                recv_sem=recv_sems[i], device_id=peer, device_id_type=MESH))
    return copies


def _copies_start(srcs, land_shapes, after, relations, src_of, dst_of, *, name):
    n = len(srcs)
    m = len(relations) * n

    def body(*refs):
        src_refs, land_refs = refs[:n], refs[n:2 * n]
        send_sems, recv_sems = refs[2 * n + 1:2 * n + 1 + m], refs[2 * n + 1 + m:2 * n + 1 + 2 * m]
        token = refs[-1]
        for cp in _peer_copies(src_refs, land_refs, send_sems, recv_sems, relations, src_of, dst_of):
            cp.start()
        token[...] = jnp.zeros_like(token)

    lands = [_in_hbm(lax.empty(s.shape, s.dtype)) for s in land_shapes]
    out_shape = ([pltpu.SemaphoreType.DMA(())] * (2 * m)
                 + [pltpu.HBM(a.shape, a.dtype) for a in srcs] + [pltpu.HBM(s.shape, s.dtype) for s in land_shapes]
                 + [jax.ShapeDtypeStruct((8, 128), F32)])
    res = pl.pallas_call(
        body, name=name, in_specs=[HBM_SPEC] * (2 * n) + [ANY_SPEC],
        out_specs=[SEM_SPEC] * (2 * m) + [HBM_SPEC] * (2 * n) + [pl.BlockSpec(memory_space=pltpu.VMEM)],
        out_shape=out_shape, input_output_aliases={i: 2 * m + i for i in range(2 * n)}, compiler_params=SPLIT_COPY,
    )(*[_in_hbm(a) for a in srcs], *lands, after)
    return (list(res[:m]), list(res[m:2 * m]), list(res[2 * m:2 * m + n]), list(res[2 * m + n:2 * m + 2 * n]), res[-1])


def _copies_wait(started, after, relations, src_of, dst_of, *, name):
    send_sems, recv_sems, srcs, lands, _ = started
    n = len(srcs)
    m = len(relations) * n

    def body(*refs):
        src_refs, land_refs = refs[:n], refs[n:2 * n]
        send_refs, recv_refs = refs[2 * n:2 * n + m], refs[2 * n + m:2 * n + 2 * m]
        for cp in _peer_copies(src_refs, land_refs, send_refs, recv_refs, relations, src_of, dst_of):
            cp.wait_send()
            cp.wait_recv()

    res = pl.pallas_call(
        body, name=name, in_specs=[HBM_SPEC] * (2 * n) + [SEM_SPEC] * (2 * m) + [ANY_SPEC], out_specs=[HBM_SPEC] * (2 * n),
        out_shape=[pltpu.HBM(a.shape, a.dtype) for a in srcs + lands],
        input_output_aliases={i: i for i in range(2 * n)}, compiler_params=SPLIT_COPY,
    )(*srcs, *lands, *send_sems, *recv_sems, after)
    return list(res[:n]), list(res[n:])


def _half_rows(ref, core):
    half = ref.shape[-2] // 2
    return pl.ds(core * half, half)


def _gather_src(src, peer):
    return src.at[_half_rows(src, peer[2]), :]


def _gather_dst(land, k, peer, me):
    return land.at[2 * me[0] + me[1], _half_rows(land, me[2]), :]


def _gather_landed(land, k, peer, me):
    return land.at[2 * peer[0] + peer[1], _half_rows(land, me[2]), :]


def _exchange_src(src, peer):
    return src.at[2 * peer[0] + peer[1], _half_rows(src, peer[2]), :]


def _exchange_dst(land, k, peer, me):
    return land.at[k]


def _small_src(src, peer):
    return src


def _small_dst(land, k, peer, me):
    return land.at[4 * me[0] + 2 * me[1] + me[2]]


def _small_landed(land, k, peer, me):
    return land.at[4 * peer[0] + 2 * peer[1] + peer[2]]


def _forward_copies(bufs, send_sems, recv_sems, core):
    x, y, c = _position()
    copies = []
    for w in range(len(bufs)):
        for j, chip in enumerate(_other_chips(x, y)):
            rows = _gather_landed(bufs[w], j, chip, (x, y, core))
            copies.append(pltpu.make_async_remote_copy(src_ref=rows, dst_ref=rows, send_sem=send_sems[3 * w + j],
                                                       recv_sem=recv_sems[3 * w + j], device_id=(x, y, 1 - c), device_id_type=MESH))
    return copies


def _forward_start(bufs, *, name):
    n = len(bufs)
    m = 3 * n

    def body(*refs):
        send_sems, recv_sems = refs[n:n + m], refs[n + m:n + 2 * m]
        for cp in _forward_copies(refs[:n], send_sems, recv_sems, lax.axis_index("c")):
            cp.start()

    res = pl.pallas_call(
        body, name=name, in_specs=[HBM_SPEC] * n, out_specs=[SEM_SPEC] * (2 * m) + [HBM_SPEC] * n,
        out_shape=[pltpu.SemaphoreType.DMA(())] * (2 * m) + [pltpu.HBM(a.shape, a.dtype) for a in bufs],
        input_output_aliases={i: 2 * m + i for i in range(n)}, compiler_params=SPLIT_COPY,
    )(*bufs)
    return list(res[:m]), list(res[m:2 * m]), list(res[2 * m:])


def _forward_wait(started, after, *, name):
    send_sems, recv_sems, bufs = started
    n = len(bufs)
    m = 3 * n

    def body(*refs):
        send_refs, recv_refs = refs[n:n + m], refs[n + m:n + 2 * m]
        c = lax.axis_index("c")
        for sent, got in zip(_forward_copies(refs[:n], send_refs, recv_refs, c),
                             _forward_copies(refs[:n], send_refs, recv_refs, 1 - c)):
            sent.wait_send()
            got.wait_recv()

    return list(pl.pallas_call(
        body, name=name, in_specs=[HBM_SPEC] * n + [SEM_SPEC] * (2 * m) + [ANY_SPEC], out_specs=[HBM_SPEC] * n,
        out_shape=[pltpu.HBM(a.shape, a.dtype) for a in bufs], input_output_aliases={i: i for i in range(n)},
        compiler_params=SPLIT_COPY,
    )(*bufs, *send_sems, *recv_sems, after))


def _share_halves(bufs, *, name):
    n = len(bufs)

    def body(*refs):
        ins, outs = refs[:n], refs[n:2 * n]
        send_sems, recv_sems = refs[2 * n:]
        x, y, c = _position()
        sends = []
        for w in range(n):
            sends.append(pltpu.make_async_remote_copy(src_ref=ins[w].at[c], dst_ref=outs[w].at[c], send_sem=send_sems.at[w],
                                                      recv_sem=recv_sems.at[w], device_id=(x, y, 1 - c), device_id_type=MESH))
            sends[-1].start()
        for w in range(n):
            pltpu.make_async_remote_copy(src_ref=ins[w].at[1 - c], dst_ref=outs[w].at[1 - c], send_sem=send_sems.at[w],
                                         recv_sem=recv_sems.at[w], device_id=(x, y, 1 - c), device_id_type=MESH).wait_recv()
        for cp in sends:
            cp.wait_send()

    hbm = pl.BlockSpec(memory_space=pl.ANY)
    return pl.pallas_call(
        body, name=name, in_specs=[hbm] * n, out_specs=[hbm] * n,
        out_shape=[jax.ShapeDtypeStruct(a.shape, a.dtype) for a in bufs],
        input_output_aliases={w: w for w in range(n)},
        scratch_shapes=[pltpu.SemaphoreType.DMA((n,)), pltpu.SemaphoreType.DMA((n,))],
    )(*bufs)


def _all_gather8(v, *, name):
    m, n = v.shape

    def body(v_ref, out_ref, send_sems, recv_sems, local_sem):
        x, y, c = _position()
        me, sibling = (x, y, c), (x, y, 1 - c)
        chips = _other_chips(x, y)

        def rows(px, py, pc):
            return out_ref.at[pl.ds((4 * px + 2 * py + pc) * m, m), :]

        def copy(k, block, to, src=None):
            return pltpu.make_async_remote_copy(src_ref=rows(*block) if src is None else src, dst_ref=rows(*block),
                                                send_sem=send_sems.at[k], recv_sem=recv_sems.at[k], device_id=to, device_id_type=MESH)

        mine = pltpu.make_async_copy(v_ref, rows(*me), local_sem)
        mine.start()
        first = [copy(0, me, sibling, src=v_ref)]
        first += [copy(1 + j, me, (*chip, c), src=v_ref) for j, chip in enumerate(chips)]
        for cp in first:
            cp.start()
        passed = [copy(4 + j, (*chip, c), sibling) for j, chip in enumerate(chips)]
        for j, chip in enumerate(chips):
            copy(1 + j, (*chip, c), me).wait_recv()
            passed[j].start()
        copy(0, sibling, me).wait_recv()
        for j, chip in enumerate(chips):
            copy(4 + j, (*chip, 1 - c), me).wait_recv()
        for cp in first + passed:
            cp.wait_send()
        mine.wait()

    vmem = pl.BlockSpec(memory_space=pltpu.VMEM)
    return pl.pallas_call(
        body, name=name, in_specs=[vmem], out_specs=vmem, out_shape=jax.ShapeDtypeStruct((8 * m, n), v.dtype),
        scratch_shapes=[pltpu.SemaphoreType.DMA((7,)), pltpu.SemaphoreType.DMA((7,)), pltpu.SemaphoreType.DMA],
    )(v)


def _sum_parts(own, landed, where, *, name, rows=256):
    n_peers, half, C = landed.shape
    rows = _tile(half, rows)
    nb = half // rows

    def body(where_ref, own_ref, land_ref, o_ref):
        acc = own_ref[...].astype(F32)
        for k in range(n_peers):
            acc = acc + land_ref[k].astype(F32)
        o_ref[...] = acc

    grid_spec = pltpu.PrefetchScalarGridSpec(
        num_scalar_prefetch=1, grid=(nb,),
        in_specs=[pl.BlockSpec((None, rows, C), lambda i, where_ref: (where_ref[0], where_ref[1] * nb + i, 0)),
                  pl.BlockSpec((n_peers, rows, C), lambda i, where_ref: (0, i, 0))],
        out_specs=pl.BlockSpec((None, rows, C), lambda i, where_ref: (where_ref[1], i, 0)))
    return pl.pallas_call(
        body, name=name, grid_spec=grid_spec, out_shape=jax.ShapeDtypeStruct((2, half, C), F32),
        compiler_params=_params("parallel"),
    )(where, own, landed)


def _sum_slabs(a, *, name, rows=256):
    P, R, C = a.shape
    rows = _tile(R, rows)

    def body(a_ref, o_ref):
        acc = a_ref[0].astype(F32)
        for p in range(1, P):
            acc = acc + a_ref[p].astype(F32)
        o_ref[...] = acc

    return pl.pallas_call(
        body, name=name, grid=(R // rows,), in_specs=[pl.BlockSpec((P, rows, C), lambda i: (0, i, 0))],
        out_specs=pl.BlockSpec((rows, C), lambda i: (i, 0)),
        out_shape=jax.ShapeDtypeStruct((R, C), F32), compiler_params=_params("parallel"),
    )(a)


def _reduce_scatter_start(grads, after=None, *, name):
    lands = [jax.ShapeDtypeStruct((len(ALL_OTHER_DEVICES), g.shape[1] // 2, g.shape[2]), g.dtype) for g in grads]
    return _copies_start(grads, lands, grads[0] if after is None else after, ALL_OTHER_DEVICES, _exchange_src, _exchange_dst,
                         name=name + "_start")


def _reduce_scatter_finish(started, after, *, name):
    core = lax.axis_index("c").astype(jnp.int32)
    chip = (2 * lax.axis_index("x") + lax.axis_index("y")).astype(jnp.int32)
    where = jnp.stack([chip, core])
    sums, landed = _copies_wait(started, after, ALL_OTHER_DEVICES, _exchange_src, _exchange_dst, name=name + "_wait")
    reduced = [_sum_parts(s, a, where, name=f"{name}_sum{w}") for w, (s, a) in enumerate(zip(sums, landed))]
    both = _share_halves(reduced, name=name + "_share")
    return [b.reshape(2 * b.shape[1], b.shape[2]) for b in both]


def _to_heads(t, heads, dil=1):
    S = t.shape[0]
    d = t.shape[1] // heads
    return jnp.transpose(t.reshape(S // dil, dil, heads, d), (2, 1, 0, 3)).reshape(heads, S, d)


def _from_heads(t, dil=1):
    heads, S, d = t.shape
    return jnp.transpose(t.reshape(heads, dil, S // dil, d), (2, 1, 0, 3)).reshape(S, heads * d)


def _unstride(t, dil):
    heads, S, d = t.shape
    return jnp.transpose(t.reshape(heads, dil, S // dil, d), (0, 2, 1, 3)).reshape(heads, S, d)


def _restride(t, dil):
    heads, S, d = t.shape
    return jnp.transpose(t.reshape(heads, S // dil, dil, d), (0, 2, 1, 3)).reshape(heads, S, d)


def _pad_cols(t, width, padded):
    S = t.shape[0]
    return jnp.pad(t.reshape(S, N_CHIPS, width), ((0, 0), (0, 0), (0, padded - width))).reshape(S, N_CHIPS * padded)


def _column_reader(t, width, padded):
    def take(lo, hi):
        pieces = []
        for p in range(N_CHIPS):
            a, b = max(lo, p * width), min(hi, (p + 1) * width)
            if a < b:
                pieces.append(t[:, p * padded + a - p * width:p * padded + b - p * width])
        return pieces[0] if len(pieces) == 1 else jnp.concatenate(pieces, axis=-1)
    return take


def _alibi(n):
    return 2.0 ** (-8.0 * jnp.arange(1, n + 1, dtype=F32) / n)


B_KW = [dict(hps=4, nb_seq=SEQ // d // ATT_BLK, n_back=w // d, scale=1.0 / math.sqrt(HEAD_DIM), dist_scale=d)
        for w, d in B_PATTERNS]
A_KW = dict(hps=A_Q_HEADS // A_KV_HEADS, nb_seq=SEQ // ATT_BLK, n_back=A_WINDOW - 1, scale=1.0 / math.sqrt(HEAD_DIM), dist_scale=1)
D_KW = dict(blk=512, hps=1, scale=1.0 / math.sqrt(D_NOPE + D_ROPE))
C_KW = dict(blk=512, hps=1, scale=1.0 / math.sqrt(HEAD_DIM))


def _rope_tables(reps):
    inv_freq = ROPE_BASE ** (-jnp.arange(0, D_ROPE, 2, dtype=F32) / D_ROPE)
    ang = jnp.arange(SEQ, dtype=F32)[:, None] * inv_freq[None, :]
    return jnp.tile(jnp.cos(ang), (1, reps)), jnp.tile(jnp.sin(ang), (1, reps))


def _mlp_fwd(x, x16, w1, w2, g, b, tag):
    hid, act = _matmul(x16, w1, mode="nn", out_dtype=BF16, epilogue="relu2", name=f"mlp{tag}_up")
    m = _matmul(act, w2, mode="nn", out_dtype=F32, name=f"mlp{tag}_down")
    y, y16, u = _norm_fwd(x, m, g, b, center=True, eps=LN_EPS, alpha=ALPHA, name=f"ln2_{tag}")
    return y, y16, (x16, hid, act, u)


def _mlp_bwd(dy, saved, w1, w2, g, tag, after=None):
    x16, hid, act, u = saved
    du, du16, dg, db = _norm_bwd(dy, u, g, center=True, eps=LN_EPS, name=f"ln2_{tag}_bwd", after=after)
    dw2 = _matmul_tn(act, du16, shards=1, name=f"mlp{tag}_dw2")
    dhid = _matmul(du16, w2, mode="nt", out_dtype=BF16, epilogue="drelu2", extra=hid, name=f"mlp{tag}_dact")
    dw1 = _matmul_tn(x16, dhid, shards=N_CHIPS, name=f"mlp{tag}_dw1")
    dx = _matmul(dhid, w1, mode="nt", out_dtype=F32, epilogue="add", extra=du, alpha=ALPHA, name=f"mlp{tag}_dx")
    return dx, dw1, dw2, dg, db


def _even_fwd(x16, w_in, sinks):
    h = _column_reader(_matmul(x16, w_in, mode="nn", out_dtype=BF16, name="even_in"), EVEN_IN // N_CHIPS, EVEN_IN_PAD)
    qa = _to_heads(h(0, A_Q_W), A_Q_HEADS)
    ka = _to_heads(h(A_Q_W, A_Q_W + A_KV_W), A_KV_HEADS)
    va = _to_heads(h(A_Q_W + A_KV_W, A_Q_W + 2 * A_KV_W), A_KV_HEADS)
    slopes_a, slopes_b = _alibi(A_Q_HEADS), _alibi(B_HEADS)
    oa, lse_a = _band_fwd(qa, ka, va, slopes_a, sinks, name="swa_fwd", **A_KW)
    qkv_b, outs, lses = [], [], []
    for gi, (_, dil) in enumerate(B_PATTERNS):
        base = A_Q_W + 2 * A_KV_W + gi * 3 * B_W
        q, k, v = (_to_heads(h(base + i * B_W, base + (i + 1) * B_W), B_HEADS, dil) for i in range(3))
        o, lse = _band_fwd(q, k, v, slopes_b, None, name=f"dil{gi}_fwd", **B_KW[gi])
        qkv_b.append((q, k, v, o, lse))
        outs.append(_unstride(o, dil))
        lses.append(_unstride(lse, dil))
    ob = _mix_fwd(outs, lses, name="dil_mix")
    y16 = jnp.concatenate([_from_heads(oa), _from_heads(ob)], axis=-1).astype(BF16)
    return y16, (x16, qa, ka, va, oa, lse_a, qkv_b, outs, lses, y16)


def _even_bwd(dmix16, du, saved, w_in, sinks, w_out, send_w_out, send_w_in):
    x16, qa, ka, va, oa, lse_a, qkv_b, outs, lses, y16 = saved
    slopes_a, slopes_b = _alibi(A_Q_HEADS), _alibi(B_HEADS)
    sent = send_w_out(_matmul_tn(y16, dmix16, shards=N_CHIPS, name="even_dwout"))
    dy = _matmul(dmix16, w_out, mode="nt", out_dtype=F32, name="even_dy", after=sent)
    doa = _to_heads(dy[:, :A_Q_W], A_Q_HEADS)
    dob = _to_heads(dy[:, A_Q_W:], B_HEADS)
    dqa, dka, dva, dsink = _band_bwd(qa, ka, va, oa, doa, lse_a, None, slopes_a, sinks, name="swa_bwd", **A_KW)
    douts, dlses = _mix_bwd(outs, lses, dob, name="dil_mix_bwd")
    parts = [_from_heads(dqa), _from_heads(dka), _from_heads(dva)]
    for gi, (_, dil) in enumerate(B_PATTERNS):
        q, k, v, o, lse = qkv_b[gi]
        dq, dk, dv = _band_bwd(q, k, v, o, _restride(douts[gi], dil), lse, _restride(dlses[gi], dil), slopes_b, None,
                               name=f"dil{gi}_bwd", **B_KW[gi])
        parts += [_from_heads(dq, dil), _from_heads(dk, dil), _from_heads(dv, dil)]
    dh16 = _pad_cols(jnp.concatenate(parts, axis=-1), EVEN_IN // N_CHIPS, EVEN_IN_PAD).astype(BF16)
    sent = send_w_in(_matmul_tn(x16, dh16, shards=N_CHIPS, name="even_dwin"), dsink[:, 0, 0])
    return _matmul(dh16, w_in, mode="nt", out_dtype=F32, epilogue="add", extra=du, alpha=ALPHA, name="even_dx", after=sent)


def _odd_fwd(x16, w_in, qn_g, kvn_g, w_uq, w_ukv, w_out):
    S = x16.shape[0]
    h = _column_reader(_matmul(x16, w_in, mode="nn", out_dtype=F32, name="odd_in"), ODD_IN // N_CHIPS, ODD_IN_PAD)
    qc, kc, vc = (_to_heads(h(i * C_W, (i + 1) * C_W).astype(BF16), C_HEADS) for i in range(3))
    cq = h(3 * C_W, 3 * C_W + D_Q_RANK)
    ckv = h(3 * C_W + D_Q_RANK, 3 * C_W + D_Q_RANK + D_KV_RANK)
    kr = h(3 * C_W + D_Q_RANK + D_KV_RANK, ODD_IN)
    oc, tot = _stick_fwd(qc, kc, vc, name="stick_fwd", **C_KW)
    _, cqn16, _ = _norm_fwd(cq, None, qn_g, None, center=False, eps=RMS_EPS, alpha=1.0, name="q_rms")
    _, ckvn16, _ = _norm_fwd(ckv, None, kvn_g, None, center=False, eps=RMS_EPS, alpha=1.0, name="kv_rms")
    q = _matmul(cqn16, w_uq, mode="nn", out_dtype=F32, name="mla_uq").reshape(S, D_HEADS, D_NOPE + D_ROPE)
    kv = _matmul(ckvn16, w_ukv, mode="nn", out_dtype=F32, name="mla_ukv").reshape(S, D_HEADS, D_NOPE + D_V)
    half = D_ROPE // 2
    q_rope = q[..., D_NOPE:]
    x1 = jnp.concatenate([q_rope[..., :half].reshape(S, D_HEADS * half), kr[:, :half]], axis=-1)
    x2 = jnp.concatenate([q_rope[..., half:].reshape(S, D_HEADS * half), kr[:, half:]], axis=-1)
    cos, sin = _rope_tables(D_HEADS + 1)
    y1, y2 = _rope(x1[None], x2[None], cos, sin, name="rope_fwd")
    nq = D_HEADS * half
    q_rot = jnp.concatenate([y1[:, :nq].reshape(S, D_HEADS, half), y2[:, :nq].reshape(S, D_HEADS, half)], axis=-1)
    kr_rot = jnp.concatenate([y1[:, nq:], y2[:, nq:]], axis=-1)
    qd = jnp.transpose(jnp.concatenate([q[..., :D_NOPE], q_rot], axis=-1), (1, 0, 2)).astype(BF16)
    kd = jnp.transpose(jnp.concatenate([kv[..., :D_NOPE], jnp.broadcast_to(kr_rot[:, None, :], (S, D_HEADS, D_ROPE))], axis=-1),
                       (1, 0, 2)).astype(BF16)
    vd = jnp.transpose(kv[..., D_NOPE:], (1, 0, 2)).astype(BF16)
    od, lse_d = _causal_fwd(qd, kd, vd, name="mla_fwd", **D_KW)
    y16 = jnp.concatenate([_from_heads(oc), _from_heads(od)], axis=-1).astype(BF16)
    mixed = _matmul(y16, w_out, mode="nn", out_dtype=F32, name="odd_out")
    return mixed, (x16, qc, kc, vc, tot, cq, ckv, cqn16, ckvn16, qd, kd, vd, od, lse_d, y16)


def _odd_bwd(dmix16, du, saved, w_in, qn_g, kvn_g, w_uq, w_ukv, w_out):
    x16, qc, kc, vc, tot, cq, ckv, cqn16, ckvn16, qd, kd, vd, od, lse_d, y16 = saved
    S = x16.shape[0]
    half = D_ROPE // 2
    dw_out = _matmul_tn(y16, dmix16, shards=1, name="odd_dwout")
    dy = _matmul(dmix16, w_out, mode="nt", out_dtype=F32, name="odd_dy")
    doc = _to_heads(dy[:, :C_W], C_HEADS)
    dod = _to_heads(dy[:, C_W:], D_HEADS)
    dqc, dkc, dvc = _stick_bwd(qc, kc, vc, tot, doc, name="stick_bwd", **C_KW)
    dqd, dkd, dvd = _causal_bwd(qd, kd, vd, od, dod, lse_d, name="mla_bwd", **D_KW)
    cos, sin = _rope_tables(D_HEADS + 1)
    nq = D_HEADS * half
    gq1 = jnp.transpose(dqd[..., D_NOPE:D_NOPE + half], (1, 0, 2)).reshape(1, S, nq)
    gq2 = jnp.transpose(dqd[..., D_NOPE + half:], (1, 0, 2)).reshape(1, S, nq)
    dq1, dq2 = _rope(gq1, gq2, cos[:, :nq], -sin[:, :nq], name="rope_bwd_q")
    dk1, dk2 = _rope(dkd[..., D_NOPE:D_NOPE + half], dkd[..., D_NOPE + half:], cos[:, nq:], -sin[:, nq:], name="rope_bwd_k")
    dq_full = jnp.concatenate([jnp.transpose(dqd[..., :D_NOPE], (1, 0, 2)), dq1.reshape(S, D_HEADS, half),
                               dq2.reshape(S, D_HEADS, half)], axis=-1).reshape(S, D_HEADS * (D_NOPE + D_ROPE)).astype(BF16)
    dkv_full = jnp.concatenate([jnp.transpose(dkd[..., :D_NOPE], (1, 0, 2)), jnp.transpose(dvd, (1, 0, 2))],
                               axis=-1).reshape(S, D_HEADS * (D_NOPE + D_V)).astype(BF16)
    dw_uq = _matmul_tn(cqn16, dq_full, shards=N_CHIPS, name="mla_dwuq")
    dw_ukv = _matmul_tn(ckvn16, dkv_full, shards=N_CHIPS, name="mla_dwukv")
    dcqn = _matmul(dq_full, w_uq, mode="nt", out_dtype=F32, name="mla_dcq")
    dckvn = _matmul(dkv_full, w_ukv, mode="nt", out_dtype=F32, name="mla_dckv")
    dcq, _, dqn_g, _ = _norm_bwd(dcqn, cq, qn_g, center=False, eps=RMS_EPS, name="q_rms_bwd")
    dckv, _, dkvn_g, _ = _norm_bwd(dckvn, ckv, kvn_g, center=False, eps=RMS_EPS, name="kv_rms_bwd")
    dh = jnp.concatenate([_from_heads(dqc), _from_heads(dkc), _from_heads(dvc), dcq, dckv, dk1, dk2], axis=-1)
    dh16 = _pad_cols(dh, ODD_IN // N_CHIPS, ODD_IN_PAD).astype(BF16)
    dw_in = _matmul_tn(x16, dh16, shards=N_CHIPS, name="odd_dwin")
    dx = _matmul(dh16, w_in, mode="nt", out_dtype=F32, epilogue="add", extra=du, alpha=ALPHA, name="odd_dx")
    return dx, dw_in, dqn_g[0], dkvn_g[0], dw_uq, dw_ukv, dw_out


WEIGHT_GROUPS = (("even_w_in",), ("even_w_out", "mlp_w1_0", "mlp_w2_0"), ("odd_w_in", "odd_w_uq", "odd_w_ukv", "odd_w_out"),
                 ("mlp_w1_1", "mlp_w2_1"))
GRAD_GROUPS = (("mlp_w2_1", "mlp_w1_1"), ("odd_w_out", "odd_w_uq", "odd_w_ukv", "odd_w_in"), ("mlp_w2_0", "mlp_w1_0"),
               ("even_w_out",), ("even_w_in",))


def _local_step(x, target, small, weights_for, send_grads, send_small, total_loss):
    def by_rows(t, rows):
        return t.reshape(N_CHIPS, rows // N_CHIPS, D_MODEL)

    x16 = x.astype(BF16)
    w = dict(weights_for(0, x16))
    y16, sv_e = _even_fwd(x16, w["even_w_in"], small["even_sinks"])
    w.update(weights_for(1, y16))
    mixed0 = _matmul(y16, w["even_w_out"], mode="nn", out_dtype=F32, name="even_out")
    x1, x1_16, u01 = _norm_fwd(x, mixed0, small["ln1_g"][0], small["ln1_b"][0], center=True, eps=LN_EPS, alpha=ALPHA, name="ln1_0")
    w1_0, w2_0 = w["mlp_w1_0"], w["mlp_w2_0"].reshape(1, D_FF, D_MODEL)
    x2, x2_16, sv_m0 = _mlp_fwd(x1, x1_16, w1_0, w2_0, small["ln2_g"][0], small["ln2_b"][0], 0)
    w.update(weights_for(2, x2_16))
    odd_w = (w["odd_w_in"], small["odd_q_norm_g"], small["odd_kv_norm_g"], w["odd_w_uq"], w["odd_w_ukv"],
             w["odd_w_out"].reshape(1, D_MODEL, D_MODEL))
    mixed1, sv_o = _odd_fwd(x2_16, *odd_w)
    x3, x3_16, u11 = _norm_fwd(x2, mixed1, small["ln1_g"][1], small["ln1_b"][1], center=True, eps=LN_EPS, alpha=ALPHA, name="ln1_1")
    w.update(weights_for(3, x3_16))
    w1_1, w2_1 = w["mlp_w1_1"], w["mlp_w2_1"].reshape(1, D_FF, D_MODEL)
    x4, _, sv_m1 = _mlp_fwd(x3, x3_16, w1_1, w2_1, small["ln2_g"][1], small["ln2_b"][1], 1)
    dy, loss_row = _loss_head(x4, target, name="loss_head")
    loss = total_loss(loss_row)

    dx3, dw1_1, dw2_1, dln2g_1, dln2b_1 = _mlp_bwd(dy, sv_m1, w1_1, w2_1, small["ln2_g"][1], 1, after=loss.reshape(1, 1))
    sent = send_grads(0, dict(mlp_w2_1=by_rows(dw2_1, D_FF), mlp_w1_1=dw1_1))
    du, du16, dln1g_1, dln1b_1 = _norm_bwd(dx3, u11, small["ln1_g"][1], center=True, eps=LN_EPS, name="ln1_1_bwd", after=sent)
    dx2, dwin_o, dqn_g, dkvn_g, dwuq, dwukv, dwout_o = _odd_bwd(du16, du, sv_o, *odd_w)
    sent = send_grads(1, dict(odd_w_out=by_rows(dwout_o, D_MODEL), odd_w_uq=dwuq, odd_w_ukv=dwukv, odd_w_in=dwin_o))
    dx1, dw1_0, dw2_0, dln2g_0, dln2b_0 = _mlp_bwd(dx2, sv_m0, w1_0, w2_0, small["ln2_g"][0], 0, after=sent)
    sent = send_grads(2, dict(mlp_w2_0=by_rows(dw2_0, D_FF), mlp_w1_0=dw1_0))
    du, du16, dln1g_0, dln1b_0 = _norm_bwd(dx1, u01, small["ln1_g"][0], center=True, eps=LN_EPS, name="ln1_0_bwd", after=sent)
    def send_last(dwin_e, dsinks):
        went = send_small(dict(even_sinks=dsinks, odd_q_norm_g=dqn_g, odd_kv_norm_g=dkvn_g,
                               ln1_g=jnp.concatenate([dln1g_0, dln1g_1]), ln1_b=jnp.concatenate([dln1b_0, dln1b_1]),
                               ln2_g=jnp.concatenate([dln2g_0, dln2g_1]), ln2_b=jnp.concatenate([dln2b_0, dln2b_1])))
        return send_grads(4, dict(even_w_in=dwin_e), went)

    dx0 = _even_bwd(du16, du, sv_e, w["even_w_in"], small["even_sinks"], w["even_w_out"],
                    lambda dwout_e: send_grads(3, dict(even_w_out=dwout_e)), send_last)
    return loss, dx0


SMALL_ORDER = ("ln1_g", "ln1_b", "ln2_g", "ln2_b", "even_sinks", "odd_q_norm_g", "odd_kv_norm_g")
SMALL_COLS = 2176


def _pack_small(parts):
    flat = jnp.concatenate([p.reshape(-1) for p in parts])
    return jnp.pad(flat, (0, 8 * SMALL_COLS - flat.shape[0])).reshape(8, SMALL_COLS)


def _unpack_small(packed, shapes):
    flat = packed.reshape(-1)
    out, pos = [], 0
    for s in shapes:
        n = math.prod(s)
        out.append(flat[pos:pos + n].reshape(s))
        pos += n
    return out


def kernel(x, even_w_in, even_sinks, even_w_out, odd_w_in, odd_q_norm_g, odd_kv_norm_g, odd_w_uq, odd_w_ukv, odd_w_out, ln1_g, ln1_b, mlp_w1, mlp_w2, ln2_g, ln2_b, loss_target, m_even_w_in, m_even_sinks, m_even_w_out, m_odd_w_in, m_odd_q_norm_g, m_odd_kv_norm_g, m_odd_w_uq, m_odd_w_ukv, m_odd_w_out, m_ln1_g, m_ln1_b, m_mlp_w1, m_mlp_w2, m_ln2_g, m_ln2_b, v_even_w_in, v_even_sinks, v_even_w_out, v_odd_w_in, v_odd_q_norm_g, v_odd_kv_norm_g, v_odd_w_uq, v_odd_w_ukv, v_odd_w_out, v_ln1_g, v_ln1_b, v_mlp_w1, v_mlp_w2, v_ln2_g, v_ln2_b):
    chip = 2 * lax.axis_index("x") + lax.axis_index("y")
    e_w, o_w = EVEN_IN // N_CHIPS, ODD_IN // N_CHIPS

    shards = dict(
        even_w_in=jnp.pad(even_w_in[0], ((0, 0), (0, EVEN_IN_PAD - e_w))), even_w_out=even_w_out[0],
        odd_w_in=jnp.pad(odd_w_in[0], ((0, 0), (0, ODD_IN_PAD - o_w))), odd_w_uq=odd_w_uq[0], odd_w_ukv=odd_w_ukv[0],
        odd_w_out=odd_w_out[0], mlp_w1_0=mlp_w1[0], mlp_w1_1=mlp_w1[1], mlp_w2_0=mlp_w2[0], mlp_w2_1=mlp_w2[1])
    names = list(shards)
    own16 = {n: shards[n].astype(BF16) for n in names}
    gather_started, token = [], own16[WEIGHT_GROUPS[0][0]]
    for g, group in enumerate(WEIGHT_GROUPS):
        lands = [jax.ShapeDtypeStruct((N_CHIPS,) + own16[n].shape, BF16) for n in group]
        gather_started.append(_copies_start([own16[n] for n in group], lands, token, SAME_CORE_OF_OTHER_CHIPS, _gather_src,
                                            _gather_dst, name=f"gather{g}_start"))
        token = gather_started[-1][-1]
    all_started = token

    forwarding = {}

    def forward(g, after):
        _, landed = _copies_wait(gather_started[g], after, SAME_CORE_OF_OTHER_CHIPS, _gather_src, _gather_landed,
                                 name=f"gather{g}_wait")
        forwarding[g] = _forward_start(landed, name=f"gather{g}_forward")

    def weights_for(g, after):
        if g not in forwarding:
            forward(g, all_started if g == 0 else after)
        full = _forward_wait(forwarding[g], after, name=f"gather{g}_forwarded")
        if 1 <= g < len(WEIGHT_GROUPS) - 1:
            forward(g + 1, after)
        return {n: lax.dynamic_update_slice(f, own16[n][None], (chip, 0, 0)) for n, f in zip(WEIGHT_GROUPS[g], full)}

    grads_started = {}

    def send_grads(g, shares, after=None):
        grads_started[g] = _reduce_scatter_start([shares[n] for n in GRAD_GROUPS[g]], after, name=f"grads{g}")
        return grads_started[g][-1]

    norm_rows = jnp.pad(jnp.concatenate([odd_q_norm_g[0], odd_kv_norm_g[0]])[None], ((0, 7), (0, 64)))
    norm_all = _all_gather8(norm_rows, name="gather_norm_gains")[0::16]
    qn_w, kvn_w = D_Q_RANK // N_CHIPS, D_KV_RANK // N_CHIPS
    small = dict(even_sinks=even_sinks[0], odd_q_norm_g=norm_all[:, :qn_w].reshape(D_Q_RANK),
                 odd_kv_norm_g=norm_all[:, qn_w:qn_w + kvn_w].reshape(D_KV_RANK), ln1_g=ln1_g, ln1_b=ln1_b, ln2_g=ln2_g, ln2_b=ln2_b)

    small_started = []

    def send_small(sm):
        pack = _pack_small([sm[n] for n in SMALL_ORDER])
        small_started.append(_copies_start([pack], [jax.ShapeDtypeStruct((8,) + pack.shape, F32)], pack, ALL_OTHER_DEVICES,
                                           _small_src, _small_dst, name="small_grads_start"))
        return small_started[0][-1]

    def small_sum(after):
        (pack,), (landed,) = _copies_wait(small_started[0], after, ALL_OTHER_DEVICES, _small_src, _small_landed,
                                          name="small_grads_wait")
        device = 4 * lax.axis_index("x") + 2 * lax.axis_index("y") + lax.axis_index("c")
        return _sum_slabs(lax.dynamic_update_slice(landed, pack[None], (device, 0, 0)), name="sum_small_grads")

    loss, grad_x = _local_step(x[0], loss_target[0], small, weights_for, send_grads, send_small,
                               lambda row: lax.psum(row[0, 0], ("x", "y", "c")))

    weights = dict(even_w_in=even_w_in, even_sinks=even_sinks, even_w_out=even_w_out, odd_w_in=odd_w_in, odd_q_norm_g=odd_q_norm_g,
                   odd_kv_norm_g=odd_kv_norm_g, odd_w_uq=odd_w_uq, odd_w_ukv=odd_w_ukv, odd_w_out=odd_w_out, ln1_g=ln1_g, ln1_b=ln1_b,
                   mlp_w1=mlp_w1, mlp_w2=mlp_w2, ln2_g=ln2_g, ln2_b=ln2_b)
    m_in = dict(even_w_in=m_even_w_in, even_sinks=m_even_sinks, even_w_out=m_even_w_out, odd_w_in=m_odd_w_in,
                odd_q_norm_g=m_odd_q_norm_g, odd_kv_norm_g=m_odd_kv_norm_g, odd_w_uq=m_odd_w_uq, odd_w_ukv=m_odd_w_ukv,
                odd_w_out=m_odd_w_out, ln1_g=m_ln1_g, ln1_b=m_ln1_b, mlp_w1=m_mlp_w1, mlp_w2=m_mlp_w2, ln2_g=m_ln2_g, ln2_b=m_ln2_b)
    v_in = dict(even_w_in=v_even_w_in, even_sinks=v_even_sinks, even_w_out=v_even_w_out, odd_w_in=v_odd_w_in,
                odd_q_norm_g=v_odd_q_norm_g, odd_kv_norm_g=v_odd_kv_norm_g, odd_w_uq=v_odd_w_uq, odd_w_ukv=v_odd_w_ukv,
                odd_w_out=v_odd_w_out, ln1_g=v_ln1_g, ln1_b=v_ln1_b, mlp_w1=v_mlp_w1, mlp_w2=v_mlp_w2, ln2_g=v_ln2_g, ln2_b=v_ln2_b)
    order = list(weights)
    updated = {}

    def update(n, g2d, layer=0, tag="", by_column=False):
        shape = weights[n].shape
        as3d = (1, math.prod(shape[:-1]), shape[-1]) if len(shape) == 2 else shape
        view = (lambda t: jnp.swapaxes(t.reshape(as3d), 1, 2)) if by_column else (lambda t: t.reshape(as3d))
        res = _adamw(view(weights[n]), g2d.T if by_column else g2d, view(m_in[n]), view(v_in[n]), layer=layer,
                     carry=updated.get(n), name=f"adamw_{n}{tag}")
        updated[n] = tuple(jnp.swapaxes(t, 1, 2) for t in res) if by_column else tuple(res)
        return res[0]

    after = grad_x
    for g, group in enumerate(GRAD_GROUPS):
        for n, r in zip(group, _reduce_scatter_finish(grads_started[g], after, name=f"grads{g}")):
            if n[:-2] in ("mlp_w1", "mlp_w2"):
                after = update(n[:-2], r, layer=int(n[-1]), tag=n[-2:])
            elif n == "even_w_in":
                after = update(n, r[:, :e_w], by_column=True)
            elif n == "odd_w_in":
                after = update(n, r[:, :o_w], by_column=True)
            else:
                after = update(n, r)
        if g == len(GRAD_GROUPS) - 2:
            g_ln1g, g_ln1b, g_ln2g, g_ln2b, g_sinks, g_qn, g_kvn = _unpack_small(
                small_sum(after), [(DEPTH, D_MODEL)] * 4 + [(1, A_Q_HEADS), (D_Q_RANK,), (D_KV_RANK,)])
            for n, gs in (("even_sinks", g_sinks), ("odd_q_norm_g", lax.dynamic_slice_in_dim(g_qn, chip * qn_w, qn_w)[None]),
                          ("odd_kv_norm_g", lax.dynamic_slice_in_dim(g_kvn, chip * kvn_w, kvn_w)[None]), ("ln1_g", g_ln1g),
                          ("ln1_b", g_ln1b), ("ln2_g", g_ln2g), ("ln2_b", g_ln2b)):
                update(n, gs)
    outs = [[updated[n][k].reshape(weights[n].shape) for n in order] for k in range(4)]
    return (loss, grad_x[None], *outs[0], *outs[1], *outs[2], *outs[3])
```

```python
import math

import jax
import jax.numpy as jnp
from jax import lax
from jax.experimental import pallas as pl
from jax.experimental.pallas import tpu as pltpu

F32 = jnp.float32
BF16 = jnp.bfloat16
MESH = pl.DeviceIdType.MESH

D_MODEL = 2048
SEQ = 2048
DEPTH = 2
HEAD_DIM = 64
A_Q_HEADS, A_KV_HEADS, A_WINDOW = 16, 2, 128
B_HEADS = 8
B_PATTERNS = ((128, 1), (512, 4), (2048, 16))
C_HEADS = 16
D_HEADS, D_Q_RANK, D_KV_RANK, D_NOPE, D_ROPE, D_V = 16, 512, 256, 64, 32, 64
ROPE_BASE = 10000.0
D_FF = 4 * D_MODEL
LN_EPS = 1e-5
RMS_EPS = 1e-6
ALPHA = (2 * DEPTH) ** 0.25
A_Q_W = A_Q_HEADS * HEAD_DIM
A_KV_W = A_KV_HEADS * HEAD_DIM
B_W = B_HEADS * HEAD_DIM
EVEN_IN = A_Q_W + 2 * A_KV_W + 3 * B_W * len(B_PATTERNS)
EVEN_OUT = A_Q_W + B_W
C_W = C_HEADS * HEAD_DIM
ODD_IN = 3 * C_W + D_Q_RANK + D_KV_RANK + D_ROPE
ADAM_LR, ADAM_B1, ADAM_B2, ADAM_EPS, ADAM_WD, ADAM_STEP = 0.001, 0.9, 0.999, 1e-08, 0.01, 10

N_CHIPS = 4
EVEN_IN_PAD = 1536
ODD_IN_PAD = 1024
ATT_BLK = 128
NEG = -1e30
V7X_VMEM_LIMIT = 48 * 1024 * 1024


def _params(*sem):
    return pltpu.CompilerParams(dimension_semantics=sem, vmem_limit_bytes=V7X_VMEM_LIMIT)


def _tile(n, cap):
    if n <= cap:
        return n
    t = cap - cap % 128
    while n % t:
        t -= 128
    return t


def _matmul(a, b, *, mode, out_dtype, name, epilogue=None, extra=None, alpha=1.0, after=None, tm=1024, tn=1024, tk=2048):
    if mode == "nn":
        M, K = a.shape
        P, _, Np = b.shape
        tm, tn, tk = _tile(M, tm), _tile(Np, tn), _tile(K, tk)
        nj = Np // tn
        grid = (M // tm, P * nj, K // tk)
        a_spec = pl.BlockSpec((tm, tk), lambda i, j, k: (i, k))
        b_spec = pl.BlockSpec((None, tk, tn), lambda i, j, k: (j // nj, k, j % nj))
        o_spec = pl.BlockSpec((tm, tn), lambda i, j, k: (i, j))
        out_shape = (M, P * Np)
        dims = (((1,), (0,)), ((), ()))
    elif mode == "nt":
        M, N = a.shape
        P, K, Np = b.shape
        tm, tn, tk = _tile(M, tm), _tile(K, tn), _tile(Np, tk)
        nkk = Np // tk
        grid = (M // tm, K // tn, P * nkk)
        a_spec = pl.BlockSpec((tm, tk), lambda i, j, k: (i, k))
        b_spec = pl.BlockSpec((None, tn, tk), lambda i, j, k: (k // nkk, j, k % nkk))
        o_spec = pl.BlockSpec((tm, tn), lambda i, j, k: (i, j))
        out_shape = (M, K)
        dims = (((1,), (1,)), ((), ()))
    else:
        raise ValueError(mode)
    n_k = grid[2]
    n_extra = 0 if extra is None else 1
    n_after = 0 if after is None else 1
    n_out = 2 if epilogue == "relu2" else 1

    def body(*refs):
        a_ref, b_ref = refs[:2]
        e_ref = refs[2] if n_extra else None
        outs = refs[2 + n_extra + n_after:2 + n_extra + n_after + n_out]
        part = lax.dot_general(a_ref[...], b_ref[...], dims, preferred_element_type=F32)

        def finish(r):
            if epilogue == "relu2":
                outs[0][...] = r.astype(outs[0].dtype)
                rp = jnp.maximum(r, 0.0)
                outs[1][...] = (rp * rp).astype(outs[1].dtype)
            elif epilogue == "drelu2":
                outs[0][...] = (r * (2.0 * jnp.maximum(e_ref[...].astype(F32), 0.0))).astype(outs[0].dtype)
            elif epilogue == "add":
                outs[0][...] = (r + alpha * e_ref[...].astype(F32)).astype(outs[0].dtype)
            else:
                outs[0][...] = r.astype(outs[0].dtype)

        if n_k == 1:
            finish(part)
        else:
            acc = refs[-1]
            k = pl.program_id(2)

            @pl.when(k == 0)
            def _():
                acc[...] = part

            @pl.when((k > 0) & (k < n_k - 1))
            def _():
                acc[...] += part

            @pl.when(k == n_k - 1)
            def _():
                finish(acc[...] + part)

    in_specs = [a_spec, b_spec] + ([o_spec] if n_extra else []) + ([pl.BlockSpec(memory_space=pl.ANY)] if n_after else [])
    shapes = [jax.ShapeDtypeStruct(out_shape, out_dtype)] * n_out
    res = pl.pallas_call(
        body, name=name, grid=grid, in_specs=in_specs,
        out_specs=[o_spec] * n_out, out_shape=shapes,
        scratch_shapes=[pltpu.VMEM((tm, tn), F32)] if n_k > 1 else [],
        compiler_params=_params("parallel", "parallel", "arbitrary"),
    )(a, b, *([extra] if n_extra else []), *([after] if n_after else []))
    return res if n_out > 1 else res[0]


def _matmul_tn(a, g, *, shards, name, tm=1024, tn=1024):
    M, K = a.shape
    P = shards
    Np = g.shape[1] // P
    tm, tn = _tile(K, tm), _tile(Np, tn)
    nj = Np // tn

    def body(a_ref, g_ref, o_ref):
        o_ref[...] = lax.dot_general(a_ref[...], g_ref[...], (((0,), (0,)), ((), ())), preferred_element_type=F32).astype(BF16)

    return pl.pallas_call(
        body, name=name, grid=(K // tm, P * nj),
        in_specs=[pl.BlockSpec((M, tm), lambda i, j: (0, i)), pl.BlockSpec((M, tn), lambda i, j: (0, j))],
        out_specs=pl.BlockSpec((None, tm, tn), lambda i, j: (j // nj, i, j % nj)),
        out_shape=jax.ShapeDtypeStruct((P, K, Np), BF16),
        compiler_params=_params("parallel", "parallel"),
    )(a, g)


def _norm_fwd(x, res, g, b, *, center, eps, alpha, name, rows=256):
    S, W = x.shape
    has_res = res is not None
    rows = _tile(S, rows)

    def body(*refs):
        x_ref = refs[0]
        r_ref = refs[1] if has_res else None
        g_ref = refs[1 + has_res]
        b_ref = refs[2 + has_res] if center else None
        y_ref, y16_ref, u_ref = refs[-3:]
        u = x_ref[...]
        if has_res:
            u = alpha * u + r_ref[...]
        if center:
            mu = jnp.mean(u, axis=1, keepdims=True)
            xc = u - mu
        else:
            xc = u
        var = jnp.mean(xc * xc, axis=1, keepdims=True)
        y = xc * lax.rsqrt(var + eps) * g_ref[...]
        if center:
            y = y + b_ref[...]
        y_ref[...] = y
        y16_ref[...] = y.astype(BF16)
        u_ref[...] = u

    row_spec = pl.BlockSpec((rows, W), lambda i: (i, 0))
    vec_spec = pl.BlockSpec((1, W), lambda i: (0, 0))
    ins = [x] + ([res] if has_res else []) + [g.reshape(1, W)] + ([b.reshape(1, W)] if center else [])
    specs = [row_spec] + ([row_spec] if has_res else []) + [vec_spec] + ([vec_spec] if center else [])
    return pl.pallas_call(
        body, name=name, grid=(S // rows,), in_specs=specs,
        out_specs=[row_spec, row_spec, row_spec],
        out_shape=[jax.ShapeDtypeStruct((S, W), F32), jax.ShapeDtypeStruct((S, W), BF16), jax.ShapeDtypeStruct((S, W), F32)],
        compiler_params=_params("parallel"),
    )(*ins)


def _norm_bwd(dy, u, g, *, center, eps, name, rows=256, after=None):
    S, W = u.shape
    rows = _tile(S, rows)

    def body(dy_ref, u_ref, g_ref, *rest):
        du_ref, du16_ref, dg_ref, db_ref = rest[-4:]
        i = pl.program_id(0)
        uu = u_ref[...]
        dyv = dy_ref[...]
        if center:
            xc = uu - jnp.mean(uu, axis=1, keepdims=True)
        else:
            xc = uu
        rstd = lax.rsqrt(jnp.mean(xc * xc, axis=1, keepdims=True) + eps)
        xhat = xc * rstd
        dxh = dyv * g_ref[...]
        c2 = jnp.mean(dxh * xhat, axis=1, keepdims=True)
        du = dxh - xhat * c2
        if center:
            du = du - jnp.mean(dxh, axis=1, keepdims=True)
        du = du * rstd
        du_ref[...] = du
        du16_ref[...] = du.astype(BF16)

        @pl.when(i == 0)
        def _():
            dg_ref[...] = jnp.zeros_like(dg_ref)
            db_ref[...] = jnp.zeros_like(db_ref)

        dg_ref[...] += jnp.sum(dyv * xhat, axis=0, keepdims=True)
        db_ref[...] += jnp.sum(dyv, axis=0, keepdims=True)

    row_spec = pl.BlockSpec((rows, W), lambda i: (i, 0))
    vec_spec = pl.BlockSpec((1, W), lambda i: (0, 0))
    return pl.pallas_call(
        body, name=name, grid=(S // rows,),
        in_specs=[row_spec, row_spec, vec_spec] + ([] if after is None else [pl.BlockSpec(memory_space=pl.ANY)]),
        out_specs=[row_spec, row_spec, vec_spec, vec_spec],
        out_shape=[jax.ShapeDtypeStruct((S, W), F32), jax.ShapeDtypeStruct((S, W), BF16),
                   jax.ShapeDtypeStruct((1, W), F32), jax.ShapeDtypeStruct((1, W), F32)],
        compiler_params=_params("arbitrary"),
    )(dy, u, g.reshape(1, W), *([] if after is None else [after]))


def _loss_head(y, target, *, name, rows=256):
    S, W = y.shape
    rows = _tile(S, rows)

    def body(y_ref, t_ref, dy_ref, l_ref):
        i = pl.program_id(0)
        e = y_ref[...] - t_ref[...]
        dy_ref[...] = e * (1.0 / W)

        @pl.when(i == 0)
        def _():
            l_ref[...] = jnp.zeros_like(l_ref)

        l_ref[...] += jnp.full(l_ref.shape, 0.5 / W * jnp.sum(e * e), F32)

    row_spec = pl.BlockSpec((rows, W), lambda i: (i, 0))
    return pl.pallas_call(
        body, name=name, grid=(S // rows,), in_specs=[row_spec, row_spec],
        out_specs=[row_spec, pl.BlockSpec((1, 128), lambda i: (0, 0))],
        out_shape=[jax.ShapeDtypeStruct((S, W), F32), jax.ShapeDtypeStruct((1, 128), F32)],
        compiler_params=_params("arbitrary"),
    )(y, target)


def _adamw(w, g, m, v, *, name, layer=0, carry=None, rows=256):
    L, R, C = w.shape
    rows = max(t for t in range(8, rows + 1, 8) if R % t == 0) if R % 8 == 0 else R
    c1 = 1.0 / (1.0 - ADAM_B1 ** ADAM_STEP)
    c2 = 1.0 / (1.0 - ADAM_B2 ** ADAM_STEP)

    def body(w_ref, g_ref, m_ref, v_ref, *rest):
        go_ref, d_ref, mo_ref, vo_ref = rest[-4:]
        gg = g_ref[...]
        mn = ADAM_B1 * m_ref[...] + (1.0 - ADAM_B1) * gg
        vn = ADAM_B2 * v_ref[...] + (1.0 - ADAM_B2) * (gg * gg)
        go_ref[...] = gg
        d_ref[...] = -ADAM_LR * ((mn * c1) / (jnp.sqrt(vn * c2) + ADAM_EPS) + ADAM_WD * w_ref[...])
        mo_ref[...] = mn
        vo_ref[...] = vn

    slab = pl.BlockSpec((None, rows, C), lambda i: (layer, i, 0))
    shp = jax.ShapeDtypeStruct((L, R, C), F32)
    carried = [] if carry is None else list(carry)
    return pl.pallas_call(
        body, name=name, grid=(R // rows,),
        in_specs=[slab, pl.BlockSpec((rows, C), lambda i: (i, 0)), slab, slab] + [pl.BlockSpec(memory_space=pl.ANY)] * len(carried),
        out_specs=[slab] * 4, out_shape=[shp] * 4, input_output_aliases={4 + k: k for k in range(len(carried))},
        compiler_params=_params("parallel"),
    )(w, g, m, v, *carried)


def _rope(x1, x2, cos, sin, *, name, rows=512):
    H, S, W = x1.shape
    rows = _tile(S, rows)

    def body(x1_ref, x2_ref, c_ref, s_ref, y1_ref, y2_ref):
        t1 = jnp.sum(x1_ref[...], axis=0)
        t2 = jnp.sum(x2_ref[...], axis=0)
        c = c_ref[...]
        s = s_ref[...]
        y1_ref[...] = t1 * c - t2 * s
        y2_ref[...] = t1 * s + t2 * c

    xs = pl.BlockSpec((H, rows, W), lambda i: (0, i, 0))
    ts = pl.BlockSpec((rows, W), lambda i: (i, 0))
    shp = jax.ShapeDtypeStruct((S, W), F32)
    return pl.pallas_call(
        body, name=name, grid=(S // rows,), in_specs=[xs, xs, ts, ts], out_specs=[ts, ts], out_shape=[shp, shp],
        compiler_params=_params("parallel"),
    )(x1, x2, cos, sin)


def _band_mask(blk, n, n_back):
    row = lax.broadcasted_iota(jnp.int32, (blk, 2 * blk), 0)
    col = lax.broadcasted_iota(jnp.int32, (blk, 2 * blk), 1)
    rel = blk + row - col
    return rel, (rel >= 0) & (rel <= n_back) & ((col >= blk) | (n >= 1))


def _window(ref, head, i, blk):
    before = pl.multiple_of(jnp.maximum(i - 1, 0) * blk, blk)
    return jnp.concatenate([ref[head, pl.ds(before, blk), :], ref[head, pl.ds(pl.multiple_of(i * blk, blk), blk), :]], axis=0)


def _window_add(ref, head, i, blk, update):
    before = pl.multiple_of(jnp.maximum(i - 1, 0) * blk, blk)
    ref[head, pl.ds(before, blk), :] += update[:blk]
    ref[head, pl.ds(pl.multiple_of(i * blk, blk), blk), :] += update[blk:]


def _band_fwd(q, k, v, slopes, sinks, *, hps, nb_seq, n_back, scale, dist_scale, name):
    blk = ATT_BLK
    Hq, T, d = q.shape
    Hk = k.shape[0]
    group = Hq // Hk
    kps = max(hps // group, 1)
    use_sink = sinks is not None

    def body(*refs):
        q_ref, k_ref, v_ref, slope_ref = refs[:4]
        sink_ref = refs[4] if use_sink else None
        o_ref, lse_ref = refs[-2:]
        i = pl.program_id(1)
        rel, valid = _band_mask(blk, i % nb_seq, n_back)
        relf = rel.astype(F32)
        for hh in range(hps):
            h = pl.program_id(0) * hps + hh
            kk = _window(k_ref, hh // group, i, blk)
            vv = _window(v_ref, hh // group, i, blk)
            s = lax.dot_general(q_ref[hh], kk, (((1,), (1,)), ((), ())), preferred_element_type=F32) * scale
            s = jnp.where(valid, s - (slope_ref[h] * dist_scale) * relf, NEG)
            m = jnp.max(s, axis=1, keepdims=True)
            if use_sink:
                m = jnp.maximum(m, sink_ref[h])
            p = jnp.where(valid, jnp.exp(s - m), 0.0)
            l = jnp.sum(p, axis=1, keepdims=True)
            if use_sink:
                l = l + jnp.exp(sink_ref[h] - m)
            o_ref[hh] = jnp.dot(p.astype(BF16), vv, preferred_element_type=F32) / l
            lse_ref[hh] = m + jnp.log(l)

    smem = pl.BlockSpec(memory_space=pltpu.SMEM)
    qs = pl.BlockSpec((hps, blk, d), lambda g, i: (g, i, 0))
    ks = pl.BlockSpec((kps, T, d), lambda g, i: (g, 0, 0))
    ins = [q, k, v, slopes] + ([sinks] if use_sink else [])
    return pl.pallas_call(
        body, name=name, grid=(Hq // hps, T // blk), in_specs=[qs, ks, ks, smem] + ([smem] if use_sink else []),
        out_specs=[qs, pl.BlockSpec((hps, blk, 1), lambda g, i: (g, i, 0))],
        out_shape=[jax.ShapeDtypeStruct((Hq, T, d), F32), jax.ShapeDtypeStruct((Hq, T, 1), F32)],
        compiler_params=_params("parallel", "parallel"),
    )(*ins)


def _band_bwd(q, k, v, o, do, lse, dlse, slopes, sinks, *, hps, nb_seq, n_back, scale, dist_scale, name):
    blk = ATT_BLK
    Hq, T, d = q.shape
    Hk = k.shape[0]
    group = Hq // Hk
    kps = max(hps // group, 1)
    use_sink, use_dlse = sinks is not None, dlse is not None

    def body(*refs):
        q_ref, k_ref, v_ref, o_ref, do_ref, lse_ref = refs[:6]
        pos = 6
        dlse_ref = refs[pos] if use_dlse else None
        pos += use_dlse
        slope_ref = refs[pos]
        pos += 1
        sink_ref = refs[pos] if use_sink else None
        pos += use_sink
        dq_ref, dk_ref, dv_ref = refs[pos:pos + 3]
        dsink_ref = refs[pos + 3] if use_sink else None
        i = pl.program_id(1)
        rel, valid = _band_mask(blk, i % nb_seq, n_back)
        relf = rel.astype(F32)

        @pl.when(i == 0)
        def _():
            dk_ref[...] = jnp.zeros_like(dk_ref)
            dv_ref[...] = jnp.zeros_like(dv_ref)
            if use_sink:
                dsink_ref[...] = jnp.zeros_like(dsink_ref)

        for kh in range(kps):
            dk_acc = jnp.zeros((2 * blk, d), F32)
            dv_acc = jnp.zeros((2 * blk, d), F32)
            kk = _window(k_ref, kh, i, blk)
            vv = _window(v_ref, kh, i, blk)
            for hh in range(kh * min(group, hps), (kh + 1) * min(group, hps)):
                h = pl.program_id(0) * hps + hh
                qb = q_ref[hh]
                dob = do_ref[hh]
                do16 = dob.astype(BF16)
                lse = lse_ref[hh]
                c = -jnp.sum(dob * o_ref[hh], axis=1, keepdims=True)
                if use_dlse:
                    c = c + dlse_ref[hh]
                s = lax.dot_general(qb, kk, (((1,), (1,)), ((), ())), preferred_element_type=F32) * scale
                s = jnp.where(valid, s - (slope_ref[h] * dist_scale) * relf, NEG)
                p = jnp.where(valid, jnp.exp(s - lse), 0.0)
                dp = lax.dot_general(do16, vv, (((1,), (1,)), ((), ())), preferred_element_type=F32)
                ds16 = (p * (dp + c) * scale).astype(BF16)
                dq_ref[hh] = jnp.dot(ds16, kk, preferred_element_type=F32)
                dk_acc = dk_acc + lax.dot_general(ds16, qb, (((0,), (0,)), ((), ())), preferred_element_type=F32)
                dv_acc = dv_acc + lax.dot_general(p.astype(BF16), do16, (((0,), (0,)), ((), ())), preferred_element_type=F32)
                if use_sink:
                    dsink_ref[hh] += jnp.full((1, 128), jnp.sum(jnp.exp(sink_ref[h] - lse) * c), F32)
            _window_add(dk_ref, kh, i, blk, dk_acc)
            _window_add(dv_ref, kh, i, blk, dv_acc)

    smem = pl.BlockSpec(memory_space=pltpu.SMEM)
    qs = pl.BlockSpec((hps, blk, d), lambda g, i: (g, i, 0))
    ls = pl.BlockSpec((hps, blk, 1), lambda g, i: (g, i, 0))
    ks = pl.BlockSpec((kps, T, d), lambda g, i: (g, 0, 0))
    in_specs = [qs, ks, ks, qs, qs, ls] + ([ls] if use_dlse else []) + [smem] + ([smem] if use_sink else [])
    ins = [q, k, v, o, do, lse] + ([dlse] if use_dlse else []) + [slopes] + ([sinks] if use_sink else [])
    out_specs = [qs, ks, ks]
    out_shape = [jax.ShapeDtypeStruct((Hq, T, d), F32), jax.ShapeDtypeStruct((Hk, T, d), F32),
                 jax.ShapeDtypeStruct((Hk, T, d), F32)]
    if use_sink:
        out_specs.append(pl.BlockSpec((hps, 1, 128), lambda g, i: (g, 0, 0)))
        out_shape.append(jax.ShapeDtypeStruct((Hq, 1, 128), F32))
    return pl.pallas_call(
        body, name=name, grid=(Hq // hps, T // blk), in_specs=in_specs, out_specs=out_specs, out_shape=out_shape,
        compiler_params=_params("parallel", "arbitrary"),
    )(*ins)


def _diagonal_mask(blk):
    return lax.broadcasted_iota(jnp.int32, (blk, blk), 0) >= lax.broadcasted_iota(jnp.int32, (blk, blk), 1)


def _causal_fwd(q, k, v, *, blk, hps, scale, name):
    H, T, dqk = q.shape
    dv = v.shape[2]

    def body(q_ref, k_ref, v_ref, o_ref, lse_ref):
        n = pl.program_id(1)
        qs = [q_ref[hh] for hh in range(hps)]

        def block(j, carry, valid):
            start = pl.multiple_of(j * blk, blk)
            out = []
            for hh in range(hps):
                m, l, acc = carry[hh]
                kb = k_ref[hh, pl.ds(start, blk), :]
                vb = v_ref[hh, pl.ds(start, blk), :]
                s = lax.dot_general(qs[hh], kb, (((1,), (1,)), ((), ())), preferred_element_type=F32) * scale
                if valid is not None:
                    s = jnp.where(valid, s, NEG)
                m_new = jnp.maximum(m, jnp.max(s, axis=1, keepdims=True))
                p = _keep(valid, jnp.exp(s - m_new))
                a = jnp.exp(m - m_new)
                out.append((m_new, a * l + jnp.sum(p, axis=1, keepdims=True),
                            a * acc + jnp.dot(p.astype(BF16), vb, preferred_element_type=F32)))
            return tuple(out)

        init = tuple((jnp.full((blk, 1), NEG, F32), jnp.zeros((blk, 1), F32), jnp.zeros((blk, dv), F32)) for _ in range(hps))
        res = block(n, lax.fori_loop(0, n, lambda j, carry: block(j, carry, None), init), _diagonal_mask(blk))
        for hh in range(hps):
            m, l, acc = res[hh]
            o_ref[hh] = acc / l
            lse_ref[hh] = m + jnp.log(l)

    qs_ = pl.BlockSpec((hps, blk, dqk), lambda g, i: (g, i, 0))
    return pl.pallas_call(
        body, name=name, grid=(H // hps, T // blk),
        in_specs=[qs_, pl.BlockSpec((hps, T, dqk), lambda g, i: (g, 0, 0)), pl.BlockSpec((hps, T, dv), lambda g, i: (g, 0, 0))],
        out_specs=[pl.BlockSpec((hps, blk, dv), lambda g, i: (g, i, 0)), pl.BlockSpec((hps, blk, 1), lambda g, i: (g, i, 0))],
        out_shape=[jax.ShapeDtypeStruct((H, T, dv), F32), jax.ShapeDtypeStruct((H, T, 1), F32)],
        compiler_params=_params("parallel", "parallel"),
    )(q, k, v)


def _causal_bwd(q, k, v, o, do, lse, *, blk, hps, scale, name):
    H, T, dqk = q.shape
    dv = v.shape[2]

    def body(q_ref, k_ref, v_ref, o_ref, do_ref, lse_ref, dq_ref, dk_ref, dv_ref):
        n = pl.program_id(1)

        @pl.when(n == 0)
        def _():
            dk_ref[...] = jnp.zeros_like(dk_ref)
            dv_ref[...] = jnp.zeros_like(dv_ref)

        qs = [q_ref[hh] for hh in range(hps)]
        do16 = [do_ref[hh].astype(BF16) for hh in range(hps)]
        lses = [lse_ref[hh] for hh in range(hps)]
        cs = [-jnp.sum(do_ref[hh] * o_ref[hh], axis=1, keepdims=True) for hh in range(hps)]

        def block(j, dqs, valid):
            start = pl.multiple_of(j * blk, blk)
            out = []
            for hh in range(hps):
                kb = k_ref[hh, pl.ds(start, blk), :]
                vb = v_ref[hh, pl.ds(start, blk), :]
                s = lax.dot_general(qs[hh], kb, (((1,), (1,)), ((), ())), preferred_element_type=F32) * scale
                if valid is not None:
                    s = jnp.where(valid, s, NEG)
                p = _keep(valid, jnp.exp(s - lses[hh]))
                dp = lax.dot_general(do16[hh], vb, (((1,), (1,)), ((), ())), preferred_element_type=F32)
                ds16 = (p * (dp + cs[hh]) * scale).astype(BF16)
                dk_ref[hh, pl.ds(start, blk), :] += lax.dot_general(ds16, qs[hh], (((0,), (0,)), ((), ())), preferred_element_type=F32)
                dv_ref[hh, pl.ds(start, blk), :] += lax.dot_general(p.astype(BF16), do16[hh], (((0,), (0,)), ((), ())),
                                                                    preferred_element_type=F32)
                out.append(dqs[hh] + jnp.dot(ds16, kb, preferred_element_type=F32))
            return tuple(out)

        init = tuple(jnp.zeros((blk, dqk), F32) for _ in range(hps))
        res = block(n, lax.fori_loop(0, n, lambda j, dqs: block(j, dqs, None), init), _diagonal_mask(blk))
        for hh in range(hps):
            dq_ref[hh] = res[hh]

    qs_ = pl.BlockSpec((hps, blk, dqk), lambda g, i: (g, i, 0))
    os_ = pl.BlockSpec((hps, blk, dv), lambda g, i: (g, i, 0))
    ls_ = pl.BlockSpec((hps, blk, 1), lambda g, i: (g, i, 0))
    ks_ = pl.BlockSpec((hps, T, dqk), lambda g, i: (g, 0, 0))
    vs_ = pl.BlockSpec((hps, T, dv), lambda g, i: (g, 0, 0))
    return pl.pallas_call(
        body, name=name, grid=(H // hps, T // blk), in_specs=[qs_, ks_, vs_, os_, os_, ls_], out_specs=[qs_, ks_, vs_],
        out_shape=[jax.ShapeDtypeStruct((H, T, dqk), F32), jax.ShapeDtypeStruct((H, T, dqk), F32),
                   jax.ShapeDtypeStruct((H, T, dv), F32)],
        compiler_params=_params("parallel", "arbitrary"),
    )(q, k, v, o, do, lse)


def _tri_sum(x, upper):
    hi = x.astype(BF16)
    lo = (x - hi.astype(F32)).astype(BF16)
    return jnp.dot(hi, upper, preferred_element_type=F32) + jnp.dot(lo, upper, preferred_element_type=F32)


def _keep(mask, x):
    return x if mask is None else jnp.where(mask, x, 0.0)


def _stick_terms(qb, kb, scale, strict):
    z = lax.dot_general(qb, kb, (((1,), (1,)), ((), ())), preferred_element_type=F32) * scale
    e = jnp.exp(-jnp.abs(z))
    lb = jnp.minimum(z, 0.0) - jnp.log(1.0 + e)
    return z, e, lb, _keep(strict, lb - z)


def _stick_fwd(q, k, v, *, blk, hps, scale, name):
    H, T, dh = q.shape

    def body(q_ref, k_ref, v_ref, o_ref, tot_ref):
        n = pl.program_id(1)
        qs = [q_ref[hh] for hh in range(hps)]
        r_i = lax.broadcasted_iota(jnp.int32, (blk, blk), 0)
        c_i = lax.broadcasted_iota(jnp.int32, (blk, blk), 1)
        upper = (r_i > c_i).astype(BF16)

        def block(j, carry, strict):
            start = pl.multiple_of(j * blk, blk)
            out = []
            for hh in range(hps):
                run, acc = carry[hh]
                kb = k_ref[hh, pl.ds(start, blk), :]
                vb = v_ref[hh, pl.ds(start, blk), :]
                _, _, lb, lk = _stick_terms(qs[hh], kb, scale, strict)
                w = _keep(strict, jnp.exp(lb + run + _tri_sum(lk, upper)))
                out.append((run + jnp.sum(lk, axis=1, keepdims=True), acc + jnp.dot(w.astype(BF16), vb, preferred_element_type=F32)))
            return tuple(out)

        init = tuple((jnp.zeros((blk, 1), F32), jnp.zeros((blk, dh), F32)) for _ in range(hps))
        res = lax.fori_loop(1, n + 1, lambda t, carry: block(n - t, carry, None), block(n, init, r_i > c_i))
        for hh in range(hps):
            tot_ref[hh], o_ref[hh] = res[hh]

    qs_ = pl.BlockSpec((hps, blk, dh), lambda g, i: (g, i, 0))
    ks_ = pl.BlockSpec((hps, T, dh), lambda g, i: (g, 0, 0))
    ts_ = pl.BlockSpec((hps, blk, 1), lambda g, i: (g, i, 0))
    return pl.pallas_call(
        body, name=name, grid=(H // hps, T // blk), in_specs=[qs_, ks_, ks_], out_specs=[qs_, ts_],
        out_shape=[jax.ShapeDtypeStruct((H, T, dh), F32), jax.ShapeDtypeStruct((H, T, 1), F32)],
        compiler_params=_params("parallel", "parallel"),
    )(q, k, v)


def _stick_bwd(q, k, v, tot, do, *, blk, hps, scale, name):
    H, T, dh = q.shape

    def body(q_ref, k_ref, v_ref, tot_ref, do_ref, dq_ref, dk_ref, dv_ref):
        n = pl.program_id(1)
        qs = [q_ref[hh] for hh in range(hps)]
        do16 = [do_ref[hh].astype(BF16) for hh in range(hps)]
        tots = [tot_ref[hh] for hh in range(hps)]
        r_i = lax.broadcasted_iota(jnp.int32, (blk, blk), 0)
        c_i = lax.broadcasted_iota(jnp.int32, (blk, blk), 1)
        upto = (r_i <= c_i).astype(BF16)
        below = (r_i < c_i).astype(BF16)

        @pl.when(n == 0)
        def _():
            dk_ref[...] = jnp.zeros_like(dk_ref)
            dv_ref[...] = jnp.zeros_like(dv_ref)

        def block(j, carry, strict):
            start = pl.multiple_of(j * blk, blk)
            out = []
            for hh in range(hps):
                run, grun, dq = carry[hh]
                kb = k_ref[hh, pl.ds(start, blk), :]
                vb = v_ref[hh, pl.ds(start, blk), :]
                z, e, lb, lk = _stick_terms(qs[hh], kb, scale, strict)
                w = _keep(strict, jnp.exp(lb + tots[hh] - (run + _tri_sum(lk, upto))))
                dw = lax.dot_general(do16[hh], vb, (((1,), (1,)), ((), ())), preferred_element_type=F32)
                g = w * dw
                before = grun + jnp.dot(g.astype(BF16), below, preferred_element_type=F32)
                inv = 1.0 / (1.0 + e)
                sig = jnp.where(z >= 0, inv, e * inv)
                dz16 = (_keep(strict, g * (1.0 - sig) - sig * before) * scale).astype(BF16)
                dk_ref[hh, pl.ds(start, blk), :] += lax.dot_general(dz16, qs[hh], (((0,), (0,)), ((), ())), preferred_element_type=F32)
                dv_ref[hh, pl.ds(start, blk), :] += lax.dot_general(w.astype(BF16), do16[hh], (((0,), (0,)), ((), ())),
                                                                    preferred_element_type=F32)
                out.append((run + jnp.sum(lk, axis=1, keepdims=True), grun + jnp.sum(g, axis=1, keepdims=True),
                            dq + jnp.dot(dz16, kb, preferred_element_type=F32)))
            return tuple(out)

        zero = jnp.zeros((blk, 1), F32)
        init = tuple((zero, zero, jnp.zeros((blk, dh), F32)) for _ in range(hps))
        res = block(n, lax.fori_loop(0, n, lambda j, carry: block(j, carry, None), init), r_i > c_i)
        for hh in range(hps):
            dq_ref[hh] = res[hh][2]

    qs_ = pl.BlockSpec((hps, blk, dh), lambda g, i: (g, i, 0))
    ks_ = pl.BlockSpec((hps, T, dh), lambda g, i: (g, 0, 0))
    ts_ = pl.BlockSpec((hps, blk, 1), lambda g, i: (g, i, 0))
    shp = jax.ShapeDtypeStruct((H, T, dh), F32)
    return pl.pallas_call(
        body, name=name, grid=(H // hps, T // blk), in_specs=[qs_, ks_, ks_, ts_, qs_], out_specs=[qs_, ks_, ks_],
        out_shape=[shp, shp, shp],
        compiler_params=_params("parallel", "arbitrary"),
    )(q, k, v, tot, do)


def _mix_fwd(outs, lses, *, name, rows=512):
    H, T, dh = outs[0].shape
    rows = _tile(T, rows)

    def body(o0, o1, o2, l0, l1, l2, y_ref):
        ls = [l0[...], l1[...], l2[...]]
        mx = jnp.maximum(jnp.maximum(ls[0], ls[1]), ls[2])
        es = [jnp.exp(x - mx) for x in ls]
        den = es[0] + es[1] + es[2]
        y_ref[...] = (es[0] * o0[...] + es[1] * o1[...] + es[2] * o2[...]) / den

    os_ = pl.BlockSpec((None, rows, dh), lambda h, i: (h, i, 0))
    ls_ = pl.BlockSpec((None, rows, 1), lambda h, i: (h, i, 0))
    return pl.pallas_call(
        body, name=name, grid=(H, T // rows), in_specs=[os_] * 3 + [ls_] * 3, out_specs=os_,
        out_shape=jax.ShapeDtypeStruct((H, T, dh), F32),
        compiler_params=_params("parallel", "parallel"),
    )(*outs, *lses)


def _mix_bwd(outs, lses, dy, *, name, rows=512):
    H, T, dh = outs[0].shape
    rows = _tile(T, rows)

    def body(o0, o1, o2, l0, l1, l2, dy_ref, d0, d1, d2, g0, g1, g2):
        ls = [l0[...], l1[...], l2[...]]
        mx = jnp.maximum(jnp.maximum(ls[0], ls[1]), ls[2])
        es = [jnp.exp(x - mx) for x in ls]
        den = es[0] + es[1] + es[2]
        mix = [e / den for e in es]
        dyv = dy_ref[...]
        dm = [jnp.sum(dyv * o[...], axis=1, keepdims=True) for o in (o0, o1, o2)]
        avg = mix[0] * dm[0] + mix[1] * dm[1] + mix[2] * dm[2]
        for d_ref, g_ref, w, t in zip((d0, d1, d2), (g0, g1, g2), mix, dm):
            d_ref[...] = w * dyv
            g_ref[...] = w * (t - avg)

    os_ = pl.BlockSpec((None, rows, dh), lambda h, i: (h, i, 0))
    ls_ = pl.BlockSpec((None, rows, 1), lambda h, i: (h, i, 0))
    so = jax.ShapeDtypeStruct((H, T, dh), F32)
    sl = jax.ShapeDtypeStruct((H, T, 1), F32)
    res = pl.pallas_call(
        body, name=name, grid=(H, T // rows), in_specs=[os_] * 3 + [ls_] * 3 + [os_], out_specs=[os_] * 3 + [ls_] * 3,
        out_shape=[so] * 3 + [sl] * 3,
        compiler_params=_params("parallel", "parallel"),
    )(*outs, *lses, dy)
    return res[:3], res[3:]


def _position():
    return lax.axis_index("x"), lax.axis_index("y"), lax.axis_index("c")


def _other_chips(x, y):
    return [(1 - x, y), (x, 1 - y), (1 - x, 1 - y)]


HBM_SPEC = pl.BlockSpec(memory_space=pltpu.HBM)
SEM_SPEC = pl.BlockSpec(memory_space=pltpu.SEMAPHORE)
ANY_SPEC = pl.BlockSpec(memory_space=pl.ANY)
SPLIT_COPY = pltpu.CompilerParams(has_side_effects=pltpu.SideEffectType.DATAFLOW_SIDE_EFFECTING)


def _in_hbm(a):
    return pltpu.with_memory_space_constraint(a, pltpu.HBM)


SAME_CORE_OF_OTHER_CHIPS = ((1, 0, 0), (0, 1, 0), (1, 1, 0))
ALL_OTHER_DEVICES = ((0, 0, 1), (1, 0, 0), (0, 1, 0), (1, 1, 0), (1, 0, 1), (0, 1, 1), (1, 1, 1))


def _peer_copies(srcs, lands, send_sems, recv_sems, relations, src_of, dst_of):
    me = _position()
    copies = []
    for w in range(len(srcs)):
        for k, flips in enumerate(relations):
            peer = tuple(1 - p if f else p for p, f in zip(me, flips))
            i = len(relations) * w + k
            copies.append(pltpu.make_async_remote_copy(
                src_ref=src_of(srcs[w], peer), dst_ref=dst_of(lands[w], k, peer, me), send_sem=send_sems[i],
                recv_sem=recv_sems[i], device_id=peer, device_id_type=MESH))
    return copies


def _copies_start(srcs, land_shapes, after, relations, src_of, dst_of, *, name):
    n = len(srcs)
    m = len(relations) * n

    def body(*refs):
        src_refs, land_refs = refs[:n], refs[n:2 * n]
        send_sems, recv_sems = refs[2 * n + 1:2 * n + 1 + m], refs[2 * n + 1 + m:2 * n + 1 + 2 * m]
        token = refs[-1]
        for cp in _peer_copies(src_refs, land_refs, send_sems, recv_sems, relations, src_of, dst_of):
            cp.start()
        token[...] = jnp.zeros_like(token)

    lands = [_in_hbm(lax.empty(s.shape, s.dtype)) for s in land_shapes]
    out_shape = ([pltpu.SemaphoreType.DMA(())] * (2 * m)
                 + [pltpu.HBM(a.shape, a.dtype) for a in srcs] + [pltpu.HBM(s.shape, s.dtype) for s in land_shapes]
                 + [jax.ShapeDtypeStruct((8, 128), F32)])
    res = pl.pallas_call(
        body, name=name, in_specs=[HBM_SPEC] * (2 * n) + [ANY_SPEC],
        out_specs=[SEM_SPEC] * (2 * m) + [HBM_SPEC] * (2 * n) + [pl.BlockSpec(memory_space=pltpu.VMEM)],
        out_shape=out_shape, input_output_aliases={i: 2 * m + i for i in range(2 * n)}, compiler_params=SPLIT_COPY,
    )(*[_in_hbm(a) for a in srcs], *lands, after)
    return (list(res[:m]), list(res[m:2 * m]), list(res[2 * m:2 * m + n]), list(res[2 * m + n:2 * m + 2 * n]), res[-1])


def _copies_wait(started, after, relations, src_of, dst_of, *, name):
    send_sems, recv_sems, srcs, lands, _ = started
    n = len(srcs)
    m = len(relations) * n

    def body(*refs):
        src_refs, land_refs = refs[:n], refs[n:2 * n]
        send_refs, recv_refs = refs[2 * n:2 * n + m], refs[2 * n + m:2 * n + 2 * m]
        for cp in _peer_copies(src_refs, land_refs, send_refs, recv_refs, relations, src_of, dst_of):
            cp.wait_send()
            cp.wait_recv()

    res = pl.pallas_call(
        body, name=name, in_specs=[HBM_SPEC] * (2 * n) + [SEM_SPEC] * (2 * m) + [ANY_SPEC], out_specs=[HBM_SPEC] * (2 * n),
        out_shape=[pltpu.HBM(a.shape, a.dtype) for a in srcs + lands],
        input_output_aliases={i: i for i in range(2 * n)}, compiler_params=SPLIT_COPY,
    )(*srcs, *lands, *send_sems, *recv_sems, after)
    return list(res[:n]), list(res[n:])


def _half_rows(ref, core):
    half = ref.shape[-2] // 2
    return pl.ds(core * half, half)


def _gather_src(src, peer):
    return src.at[_half_rows(src, peer[2]), :]


def _gather_dst(land, k, peer, me):
    return land.at[2 * me[0] + me[1], _half_rows(land, me[2]), :]


def _gather_landed(land, k, peer, me):
    return land.at[2 * peer[0] + peer[1], _half_rows(land, me[2]), :]


def _exchange_src(src, peer):
    return src.at[2 * peer[0] + peer[1], _half_rows(src, peer[2]), :]


def _exchange_dst(land, k, peer, me):
    return land.at[k]


def _small_src(src, peer):
    return src


def _small_dst(land, k, peer, me):
    return land.at[4 * me[0] + 2 * me[1] + me[2]]


def _small_landed(land, k, peer, me):
    return land.at[4 * peer[0] + 2 * peer[1] + peer[2]]


def _forward_copies(bufs, send_sems, recv_sems, core):
    x, y, c = _position()
    copies = []
    for w in range(len(bufs)):
        for j, chip in enumerate(_other_chips(x, y)):
            rows = _gather_landed(bufs[w], j, chip, (x, y, core))
            copies.append(pltpu.make_async_remote_copy(src_ref=rows, dst_ref=rows, send_sem=send_sems[3 * w + j],
                                                       recv_sem=recv_sems[3 * w + j], device_id=(x, y, 1 - c), device_id_type=MESH))
    return copies


def _forward_start(bufs, *, name):
    n = len(bufs)
    m = 3 * n

    def body(*refs):
        send_sems, recv_sems = refs[n:n + m], refs[n + m:n + 2 * m]
        for cp in _forward_copies(refs[:n], send_sems, recv_sems, lax.axis_index("c")):
            cp.start()

    res = pl.pallas_call(
        body, name=name, in_specs=[HBM_SPEC] * n, out_specs=[SEM_SPEC] * (2 * m) + [HBM_SPEC] * n,
        out_shape=[pltpu.SemaphoreType.DMA(())] * (2 * m) + [pltpu.HBM(a.shape, a.dtype) for a in bufs],
        input_output_aliases={i: 2 * m + i for i in range(n)}, compiler_params=SPLIT_COPY,
    )(*bufs)
    return list(res[:m]), list(res[m:2 * m]), list(res[2 * m:])


def _forward_wait(started, after, *, name):
    send_sems, recv_sems, bufs = started
    n = len(bufs)
    m = 3 * n

    def body(*refs):
        send_refs, recv_refs = refs[n:n + m], refs[n + m:n + 2 * m]
        c = lax.axis_index("c")
        for sent, got in zip(_forward_copies(refs[:n], send_refs, recv_refs, c),
                             _forward_copies(refs[:n], send_refs, recv_refs, 1 - c)):
            sent.wait_send()
            got.wait_recv()

    return list(pl.pallas_call(
        body, name=name, in_specs=[HBM_SPEC] * n + [SEM_SPEC] * (2 * m) + [ANY_SPEC], out_specs=[HBM_SPEC] * n,
        out_shape=[pltpu.HBM(a.shape, a.dtype) for a in bufs], input_output_aliases={i: i for i in range(n)},
        compiler_params=SPLIT_COPY,
    )(*bufs, *send_sems, *recv_sems, after))


def _share_halves(bufs, *, name):
    n = len(bufs)

    def body(*refs):
        ins, outs = refs[:n], refs[n:2 * n]
        send_sems, recv_sems = refs[2 * n:]
        x, y, c = _position()
        sends = []
        for w in range(n):
            sends.append(pltpu.make_async_remote_copy(src_ref=ins[w].at[c], dst_ref=outs[w].at[c], send_sem=send_sems.at[w],
                                                      recv_sem=recv_sems.at[w], device_id=(x, y, 1 - c), device_id_type=MESH))
            sends[-1].start()
        for w in range(n):
            pltpu.make_async_remote_copy(src_ref=ins[w].at[1 - c], dst_ref=outs[w].at[1 - c], send_sem=send_sems.at[w],
                                         recv_sem=recv_sems.at[w], device_id=(x, y, 1 - c), device_id_type=MESH).wait_recv()
        for cp in sends:
            cp.wait_send()

    hbm = pl.BlockSpec(memory_space=pl.ANY)
    return pl.pallas_call(
        body, name=name, in_specs=[hbm] * n, out_specs=[hbm] * n,
        out_shape=[jax.ShapeDtypeStruct(a.shape, a.dtype) for a in bufs],
        input_output_aliases={w: w for w in range(n)},
        scratch_shapes=[pltpu.SemaphoreType.DMA((n,)), pltpu.SemaphoreType.DMA((n,))],
    )(*bufs)


def _all_gather8(v, *, name):
    m, n = v.shape

    def body(v_ref, out_ref, send_sems, recv_sems, local_sem):
        x, y, c = _position()
        me, sibling = (x, y, c), (x, y, 1 - c)
        chips = _other_chips(x, y)

        def rows(px, py, pc):
            return out_ref.at[pl.ds((4 * px + 2 * py + pc) * m, m), :]

        def copy(k, block, to, src=None):
            return pltpu.make_async_remote_copy(src_ref=rows(*block) if src is None else src, dst_ref=rows(*block),
                                                send_sem=send_sems.at[k], recv_sem=recv_sems.at[k], device_id=to, device_id_type=MESH)

        mine = pltpu.make_async_copy(v_ref, rows(*me), local_sem)
        mine.start()
        first = [copy(0, me, sibling, src=v_ref)]
        first += [copy(1 + j, me, (*chip, c), src=v_ref) for j, chip in enumerate(chips)]
        for cp in first:
            cp.start()
        passed = [copy(4 + j, (*chip, c), sibling) for j, chip in enumerate(chips)]
        for j, chip in enumerate(chips):
            copy(1 + j, (*chip, c), me).wait_recv()
            passed[j].start()
        copy(0, sibling, me).wait_recv()
        for j, chip in enumerate(chips):
            copy(4 + j, (*chip, 1 - c), me).wait_recv()
        for cp in first + passed:
            cp.wait_send()
        mine.wait()

    vmem = pl.BlockSpec(memory_space=pltpu.VMEM)
    return pl.pallas_call(
        body, name=name, in_specs=[vmem], out_specs=vmem, out_shape=jax.ShapeDtypeStruct((8 * m, n), v.dtype),
        scratch_shapes=[pltpu.SemaphoreType.DMA((7,)), pltpu.SemaphoreType.DMA((7,)), pltpu.SemaphoreType.DMA],
    )(v)


def _sum_parts(own, landed, where, *, name, rows=256):
    n_peers, half, C = landed.shape
    rows = _tile(half, rows)
    nb = half // rows

    def body(where_ref, own_ref, land_ref, o_ref):
        acc = own_ref[...].astype(F32)
        for k in range(n_peers):
            acc = acc + land_ref[k].astype(F32)
        o_ref[...] = acc

    grid_spec = pltpu.PrefetchScalarGridSpec(
        num_scalar_prefetch=1, grid=(nb,),
        in_specs=[pl.BlockSpec((None, rows, C), lambda i, where_ref: (where_ref[0], where_ref[1] * nb + i, 0)),
                  pl.BlockSpec((n_peers, rows, C), lambda i, where_ref: (0, i, 0))],
        out_specs=pl.BlockSpec((None, rows, C), lambda i, where_ref: (where_ref[1], i, 0)))
    return pl.pallas_call(
        body, name=name, grid_spec=grid_spec, out_shape=jax.ShapeDtypeStruct((2, half, C), F32),
        compiler_params=_params("parallel"),
    )(where, own, landed)


def _sum_slabs(a, *, name, rows=256):
    P, R, C = a.shape
    rows = _tile(R, rows)

    def body(a_ref, o_ref):
        acc = a_ref[0].astype(F32)
        for p in range(1, P):
            acc = acc + a_ref[p].astype(F32)
        o_ref[...] = acc

    return pl.pallas_call(
        body, name=name, grid=(R // rows,), in_specs=[pl.BlockSpec((P, rows, C), lambda i: (0, i, 0))],
        out_specs=pl.BlockSpec((rows, C), lambda i: (i, 0)),
        out_shape=jax.ShapeDtypeStruct((R, C), F32), compiler_params=_params("parallel"),
    )(a)


def _reduce_scatter_start(grads, after=None, *, name):
    lands = [jax.ShapeDtypeStruct((len(ALL_OTHER_DEVICES), g.shape[1] // 2, g.shape[2]), g.dtype) for g in grads]
    return _copies_start(grads, lands, grads[0] if after is None else after, ALL_OTHER_DEVICES, _exchange_src, _exchange_dst,
                         name=name + "_start")


def _reduce_scatter_finish(started, after, *, name):
    core = lax.axis_index("c").astype(jnp.int32)
    chip = (2 * lax.axis_index("x") + lax.axis_index("y")).astype(jnp.int32)
    where = jnp.stack([chip, core])
    sums, landed = _copies_wait(started, after, ALL_OTHER_DEVICES, _exchange_src, _exchange_dst, name=name + "_wait")
    reduced = [_sum_parts(s, a, where, name=f"{name}_sum{w}") for w, (s, a) in enumerate(zip(sums, landed))]
    both = _share_halves(reduced, name=name + "_share")
    return [b.reshape(2 * b.shape[1], b.shape[2]) for b in both]


def _to_heads(t, heads, dil=1):
    S = t.shape[0]
    d = t.shape[1] // heads
    return jnp.transpose(t.reshape(S // dil, dil, heads, d), (2, 1, 0, 3)).reshape(heads, S, d)


def _from_heads(t, dil=1):
    heads, S, d = t.shape
    return jnp.transpose(t.reshape(heads, dil, S // dil, d), (2, 1, 0, 3)).reshape(S, heads * d)


def _unstride(t, dil):
    heads, S, d = t.shape
    return jnp.transpose(t.reshape(heads, dil, S // dil, d), (0, 2, 1, 3)).reshape(heads, S, d)


def _restride(t, dil):
    heads, S, d = t.shape
    return jnp.transpose(t.reshape(heads, S // dil, dil, d), (0, 2, 1, 3)).reshape(heads, S, d)


def _pad_cols(t, width, padded):
    S = t.shape[0]
    return jnp.pad(t.reshape(S, N_CHIPS, width), ((0, 0), (0, 0), (0, padded - width))).reshape(S, N_CHIPS * padded)


def _column_reader(t, width, padded):
    def take(lo, hi):
        pieces = []
        for p in range(N_CHIPS):
            a, b = max(lo, p * width), min(hi, (p + 1) * width)
            if a < b:
                pieces.append(t[:, p * padded + a - p * width:p * padded + b - p * width])
        return pieces[0] if len(pieces) == 1 else jnp.concatenate(pieces, axis=-1)
    return take


def _alibi(n):
    return 2.0 ** (-8.0 * jnp.arange(1, n + 1, dtype=F32) / n)


B_KW = [dict(hps=4, nb_seq=SEQ // d // ATT_BLK, n_back=w // d, scale=1.0 / math.sqrt(HEAD_DIM), dist_scale=d)
        for w, d in B_PATTERNS]
A_KW = dict(hps=A_Q_HEADS // A_KV_HEADS, nb_seq=SEQ // ATT_BLK, n_back=A_WINDOW - 1, scale=1.0 / math.sqrt(HEAD_DIM), dist_scale=1)
D_KW = dict(blk=512, hps=1, scale=1.0 / math.sqrt(D_NOPE + D_ROPE))
C_KW = dict(blk=512, hps=1, scale=1.0 / math.sqrt(HEAD_DIM))


def _rope_tables(reps):
    inv_freq = ROPE_BASE ** (-jnp.arange(0, D_ROPE, 2, dtype=F32) / D_ROPE)
    ang = jnp.arange(SEQ, dtype=F32)[:, None] * inv_freq[None, :]
    return jnp.tile(jnp.cos(ang), (1, reps)), jnp.tile(jnp.sin(ang), (1, reps))


def _mlp_fwd(x, x16, w1, w2, g, b, tag):
    hid, act = _matmul(x16, w1, mode="nn", out_dtype=BF16, epilogue="relu2", name=f"mlp{tag}_up")
    m = _matmul(act, w2, mode="nn", out_dtype=F32, name=f"mlp{tag}_down")
    y, y16, u = _norm_fwd(x, m, g, b, center=True, eps=LN_EPS, alpha=ALPHA, name=f"ln2_{tag}")
    return y, y16, (x16, hid, act, u)


def _mlp_bwd(dy, saved, w1, w2, g, tag, after=None):
    x16, hid, act, u = saved
    du, du16, dg, db = _norm_bwd(dy, u, g, center=True, eps=LN_EPS, name=f"ln2_{tag}_bwd", after=after)
    dw2 = _matmul_tn(act, du16, shards=1, name=f"mlp{tag}_dw2")
    dhid = _matmul(du16, w2, mode="nt", out_dtype=BF16, epilogue="drelu2", extra=hid, name=f"mlp{tag}_dact")
    dw1 = _matmul_tn(x16, dhid, shards=N_CHIPS, name=f"mlp{tag}_dw1")
    dx = _matmul(dhid, w1, mode="nt", out_dtype=F32, epilogue="add", extra=du, alpha=ALPHA, name=f"mlp{tag}_dx")
    return dx, dw1, dw2, dg, db


def _even_fwd(x16, w_in, sinks):
    h = _column_reader(_matmul(x16, w_in, mode="nn", out_dtype=BF16, name="even_in"), EVEN_IN // N_CHIPS, EVEN_IN_PAD)
    qa = _to_heads(h(0, A_Q_W), A_Q_HEADS)
    ka = _to_heads(h(A_Q_W, A_Q_W + A_KV_W), A_KV_HEADS)
    va = _to_heads(h(A_Q_W + A_KV_W, A_Q_W + 2 * A_KV_W), A_KV_HEADS)
    slopes_a, slopes_b = _alibi(A_Q_HEADS), _alibi(B_HEADS)
    oa, lse_a = _band_fwd(qa, ka, va, slopes_a, sinks, name="swa_fwd", **A_KW)
    qkv_b, outs, lses = [], [], []
    for gi, (_, dil) in enumerate(B_PATTERNS):
        base = A_Q_W + 2 * A_KV_W + gi * 3 * B_W
        q, k, v = (_to_heads(h(base + i * B_W, base + (i + 1) * B_W), B_HEADS, dil) for i in range(3))
        o, lse = _band_fwd(q, k, v, slopes_b, None, name=f"dil{gi}_fwd", **B_KW[gi])
        qkv_b.append((q, k, v, o, lse))
        outs.append(_unstride(o, dil))
        lses.append(_unstride(lse, dil))
    ob = _mix_fwd(outs, lses, name="dil_mix")
    y16 = jnp.concatenate([_from_heads(oa), _from_heads(ob)], axis=-1).astype(BF16)
    return y16, (x16, qa, ka, va, oa, lse_a, qkv_b, outs, lses, y16)


def _even_bwd(dmix16, du, saved, w_in, sinks, w_out, send_w_out, send_w_in):
    x16, qa, ka, va, oa, lse_a, qkv_b, outs, lses, y16 = saved
    slopes_a, slopes_b = _alibi(A_Q_HEADS), _alibi(B_HEADS)
    sent = send_w_out(_matmul_tn(y16, dmix16, shards=N_CHIPS, name="even_dwout"))
    dy = _matmul(dmix16, w_out, mode="nt", out_dtype=F32, name="even_dy", after=sent)
    doa = _to_heads(dy[:, :A_Q_W], A_Q_HEADS)
    dob = _to_heads(dy[:, A_Q_W:], B_HEADS)
    dqa, dka, dva, dsink = _band_bwd(qa, ka, va, oa, doa, lse_a, None, slopes_a, sinks, name="swa_bwd", **A_KW)
    douts, dlses = _mix_bwd(outs, lses, dob, name="dil_mix_bwd")
    parts = [_from_heads(dqa), _from_heads(dka), _from_heads(dva)]
    for gi, (_, dil) in enumerate(B_PATTERNS):
        q, k, v, o, lse = qkv_b[gi]
        dq, dk, dv = _band_bwd(q, k, v, o, _restride(douts[gi], dil), lse, _restride(dlses[gi], dil), slopes_b, None,
                               name=f"dil{gi}_bwd", **B_KW[gi])
        parts += [_from_heads(dq, dil), _from_heads(dk, dil), _from_heads(dv, dil)]
    dh16 = _pad_cols(jnp.concatenate(parts, axis=-1), EVEN_IN // N_CHIPS, EVEN_IN_PAD).astype(BF16)
    sent = send_w_in(_matmul_tn(x16, dh16, shards=N_CHIPS, name="even_dwin"), dsink[:, 0, 0])
    return _matmul(dh16, w_in, mode="nt", out_dtype=F32, epilogue="add", extra=du, alpha=ALPHA, name="even_dx", after=sent)


def _odd_fwd(x16, w_in, qn_g, kvn_g, w_uq, w_ukv, w_out):
    S = x16.shape[0]
    h = _column_reader(_matmul(x16, w_in, mode="nn", out_dtype=F32, name="odd_in"), ODD_IN // N_CHIPS, ODD_IN_PAD)
    qc, kc, vc = (_to_heads(h(i * C_W, (i + 1) * C_W).astype(BF16), C_HEADS) for i in range(3))
    cq = h(3 * C_W, 3 * C_W + D_Q_RANK)
    ckv = h(3 * C_W + D_Q_RANK, 3 * C_W + D_Q_RANK + D_KV_RANK)
    kr = h(3 * C_W + D_Q_RANK + D_KV_RANK, ODD_IN)
    oc, tot = _stick_fwd(qc, kc, vc, name="stick_fwd", **C_KW)
    _, cqn16, _ = _norm_fwd(cq, None, qn_g, None, center=False, eps=RMS_EPS, alpha=1.0, name="q_rms")
    _, ckvn16, _ = _norm_fwd(ckv, None, kvn_g, None, center=False, eps=RMS_EPS, alpha=1.0, name="kv_rms")
    q = _matmul(cqn16, w_uq, mode="nn", out_dtype=F32, name="mla_uq").reshape(S, D_HEADS, D_NOPE + D_ROPE)
    kv = _matmul(ckvn16, w_ukv, mode="nn", out_dtype=F32, name="mla_ukv").reshape(S, D_HEADS, D_NOPE + D_V)
    half = D_ROPE // 2
    q_rope = q[..., D_NOPE:]
    x1 = jnp.concatenate([q_rope[..., :half].reshape(S, D_HEADS * half), kr[:, :half]], axis=-1)
    x2 = jnp.concatenate([q_rope[..., half:].reshape(S, D_HEADS * half), kr[:, half:]], axis=-1)
    cos, sin = _rope_tables(D_HEADS + 1)
    y1, y2 = _rope(x1[None], x2[None], cos, sin, name="rope_fwd")
    nq = D_HEADS * half
    q_rot = jnp.concatenate([y1[:, :nq].reshape(S, D_HEADS, half), y2[:, :nq].reshape(S, D_HEADS, half)], axis=-1)
    kr_rot = jnp.concatenate([y1[:, nq:], y2[:, nq:]], axis=-1)
    qd = jnp.transpose(jnp.concatenate([q[..., :D_NOPE], q_rot], axis=-1), (1, 0, 2)).astype(BF16)
    kd = jnp.transpose(jnp.concatenate([kv[..., :D_NOPE], jnp.broadcast_to(kr_rot[:, None, :], (S, D_HEADS, D_ROPE))], axis=-1),
                       (1, 0, 2)).astype(BF16)
    vd = jnp.transpose(kv[..., D_NOPE:], (1, 0, 2)).astype(BF16)
    od, lse_d = _causal_fwd(qd, kd, vd, name="mla_fwd", **D_KW)
    y16 = jnp.concatenate([_from_heads(oc), _from_heads(od)], axis=-1).astype(BF16)
    mixed = _matmul(y16, w_out, mode="nn", out_dtype=F32, name="odd_out")
    return mixed, (x16, qc, kc, vc, tot, cq, ckv, cqn16, ckvn16, qd, kd, vd, od, lse_d, y16)


def _odd_bwd(dmix16, du, saved, w_in, qn_g, kvn_g, w_uq, w_ukv, w_out):
    x16, qc, kc, vc, tot, cq, ckv, cqn16, ckvn16, qd, kd, vd, od, lse_d, y16 = saved
    S = x16.shape[0]
    half = D_ROPE // 2
    dw_out = _matmul_tn(y16, dmix16, shards=1, name="odd_dwout")
    dy = _matmul(dmix16, w_out, mode="nt", out_dtype=F32, name="odd_dy")
    doc = _to_heads(dy[:, :C_W], C_HEADS)
    dod = _to_heads(dy[:, C_W:], D_HEADS)
    dqc, dkc, dvc = _stick_bwd(qc, kc, vc, tot, doc, name="stick_bwd", **C_KW)
    dqd, dkd, dvd = _causal_bwd(qd, kd, vd, od, dod, lse_d, name="mla_bwd", **D_KW)
    cos, sin = _rope_tables(D_HEADS + 1)
    nq = D_HEADS * half
    gq1 = jnp.transpose(dqd[..., D_NOPE:D_NOPE + half], (1, 0, 2)).reshape(1, S, nq)
    gq2 = jnp.transpose(dqd[..., D_NOPE + half:], (1, 0, 2)).reshape(1, S, nq)
    dq1, dq2 = _rope(gq1, gq2, cos[:, :nq], -sin[:, :nq], name="rope_bwd_q")
    dk1, dk2 = _rope(dkd[..., D_NOPE:D_NOPE + half], dkd[..., D_NOPE + half:], cos[:, nq:], -sin[:, nq:], name="rope_bwd_k")
    dq_full = jnp.concatenate([jnp.transpose(dqd[..., :D_NOPE], (1, 0, 2)), dq1.reshape(S, D_HEADS, half),
                               dq2.reshape(S, D_HEADS, half)], axis=-1).reshape(S, D_HEADS * (D_NOPE + D_ROPE)).astype(BF16)
    dkv_full = jnp.concatenate([jnp.transpose(dkd[..., :D_NOPE], (1, 0, 2)), jnp.transpose(dvd, (1, 0, 2))],
                               axis=-1).reshape(S, D_HEADS * (D_NOPE + D_V)).astype(BF16)
    dw_uq = _matmul_tn(cqn16, dq_full, shards=N_CHIPS, name="mla_dwuq")
    dw_ukv = _matmul_tn(ckvn16, dkv_full, shards=N_CHIPS, name="mla_dwukv")
    dcqn = _matmul(dq_full, w_uq, mode="nt", out_dtype=F32, name="mla_dcq")
    dckvn = _matmul(dkv_full, w_ukv, mode="nt", out_dtype=F32, name="mla_dckv")
    dcq, _, dqn_g, _ = _norm_bwd(dcqn, cq, qn_g, center=False, eps=RMS_EPS, name="q_rms_bwd")
    dckv, _, dkvn_g, _ = _norm_bwd(dckvn, ckv, kvn_g, center=False, eps=RMS_EPS, name="kv_rms_bwd")
    dh = jnp.concatenate([_from_heads(dqc), _from_heads(dkc), _from_heads(dvc), dcq, dckv, dk1, dk2], axis=-1)
    dh16 = _pad_cols(dh, ODD_IN // N_CHIPS, ODD_IN_PAD).astype(BF16)
    dw_in = _matmul_tn(x16, dh16, shards=N_CHIPS, name="odd_dwin")
    dx = _matmul(dh16, w_in, mode="nt", out_dtype=F32, epilogue="add", extra=du, alpha=ALPHA, name="odd_dx")
    return dx, dw_in, dqn_g[0], dkvn_g[0], dw_uq, dw_ukv, dw_out


WEIGHT_GROUPS = (("even_w_in",), ("even_w_out", "mlp_w1_0", "mlp_w2_0"), ("odd_w_in", "odd_w_uq", "odd_w_ukv", "odd_w_out"),
                 ("mlp_w1_1", "mlp_w2_1"))
GRAD_GROUPS = (("mlp_w2_1", "mlp_w1_1"), ("odd_w_out", "odd_w_uq", "odd_w_ukv", "odd_w_in"), ("mlp_w2_0", "mlp_w1_0"),
               ("even_w_out",), ("even_w_in",))


def _local_step(x, target, small, weights_for, send_grads, send_small, total_loss):
    def by_rows(t, rows):
        return t.reshape(N_CHIPS, rows // N_CHIPS, D_MODEL)

    x16 = x.astype(BF16)
    w = dict(weights_for(0, x16))
    y16, sv_e = _even_fwd(x16, w["even_w_in"], small["even_sinks"])
    w.update(weights_for(1, y16))
    mixed0 = _matmul(y16, w["even_w_out"], mode="nn", out_dtype=F32, name="even_out")
    x1, x1_16, u01 = _norm_fwd(x, mixed0, small["ln1_g"][0], small["ln1_b"][0], center=True, eps=LN_EPS, alpha=ALPHA, name="ln1_0")
    w1_0, w2_0 = w["mlp_w1_0"], w["mlp_w2_0"].reshape(1, D_FF, D_MODEL)
    x2, x2_16, sv_m0 = _mlp_fwd(x1, x1_16, w1_0, w2_0, small["ln2_g"][0], small["ln2_b"][0], 0)
    w.update(weights_for(2, x2_16))
    odd_w = (w["odd_w_in"], small["odd_q_norm_g"], small["odd_kv_norm_g"], w["odd_w_uq"], w["odd_w_ukv"],
             w["odd_w_out"].reshape(1, D_MODEL, D_MODEL))
    mixed1, sv_o = _odd_fwd(x2_16, *odd_w)
    x3, x3_16, u11 = _norm_fwd(x2, mixed1, small["ln1_g"][1], small["ln1_b"][1], center=True, eps=LN_EPS, alpha=ALPHA, name="ln1_1")
    w.update(weights_for(3, x3_16))
    w1_1, w2_1 = w["mlp_w1_1"], w["mlp_w2_1"].reshape(1, D_FF, D_MODEL)
    x4, _, sv_m1 = _mlp_fwd(x3, x3_16, w1_1, w2_1, small["ln2_g"][1], small["ln2_b"][1], 1)
    dy, loss_row = _loss_head(x4, target, name="loss_head")
    loss = total_loss(loss_row)

    dx3, dw1_1, dw2_1, dln2g_1, dln2b_1 = _mlp_bwd(dy, sv_m1, w1_1, w2_1, small["ln2_g"][1], 1, after=loss.reshape(1, 1))
    sent = send_grads(0, dict(mlp_w2_1=by_rows(dw2_1, D_FF), mlp_w1_1=dw1_1))
    du, du16, dln1g_1, dln1b_1 = _norm_bwd(dx3, u11, small["ln1_g"][1], center=True, eps=LN_EPS, name="ln1_1_bwd", after=sent)
    dx2, dwin_o, dqn_g, dkvn_g, dwuq, dwukv, dwout_o = _odd_bwd(du16, du, sv_o, *odd_w)
    sent = send_grads(1, dict(odd_w_out=by_rows(dwout_o, D_MODEL), odd_w_uq=dwuq, odd_w_ukv=dwukv, odd_w_in=dwin_o))
    dx1, dw1_0, dw2_0, dln2g_0, dln2b_0 = _mlp_bwd(dx2, sv_m0, w1_0, w2_0, small["ln2_g"][0], 0, after=sent)
    sent = send_grads(2, dict(mlp_w2_0=by_rows(dw2_0, D_FF), mlp_w1_0=dw1_0))
    du, du16, dln1g_0, dln1b_0 = _norm_bwd(dx1, u01, small["ln1_g"][0], center=True, eps=LN_EPS, name="ln1_0_bwd", after=sent)
    def send_last(dwin_e, dsinks):
        went = send_small(dict(even_sinks=dsinks, odd_q_norm_g=dqn_g, odd_kv_norm_g=dkvn_g,
                               ln1_g=jnp.concatenate([dln1g_0, dln1g_1]), ln1_b=jnp.concatenate([dln1b_0, dln1b_1]),
                               ln2_g=jnp.concatenate([dln2g_0, dln2g_1]), ln2_b=jnp.concatenate([dln2b_0, dln2b_1])))
        return send_grads(4, dict(even_w_in=dwin_e), went)

    dx0 = _even_bwd(du16, du, sv_e, w["even_w_in"], small["even_sinks"], w["even_w_out"],
                    lambda dwout_e: send_grads(3, dict(even_w_out=dwout_e)), send_last)
    return loss, dx0


SMALL_ORDER = ("ln1_g", "ln1_b", "ln2_g", "ln2_b", "even_sinks", "odd_q_norm_g", "odd_kv_norm_g")
SMALL_COLS = 2176


def _pack_small(parts):
    flat = jnp.concatenate([p.reshape(-1) for p in parts])
    return jnp.pad(flat, (0, 8 * SMALL_COLS - flat.shape[0])).reshape(8, SMALL_COLS)


def _unpack_small(packed, shapes):
    flat = packed.reshape(-1)
    out, pos = [], 0
    for s in shapes:
        n = math.prod(s)
        out.append(flat[pos:pos + n].reshape(s))
        pos += n
    return out


def kernel(x, even_w_in, even_sinks, even_w_out, odd_w_in, odd_q_norm_g, odd_kv_norm_g, odd_w_uq, odd_w_ukv, odd_w_out, ln1_g, ln1_b, mlp_w1, mlp_w2, ln2_g, ln2_b, loss_target, m_even_w_in, m_even_sinks, m_even_w_out, m_odd_w_in, m_odd_q_norm_g, m_odd_kv_norm_g, m_odd_w_uq, m_odd_w_ukv, m_odd_w_out, m_ln1_g, m_ln1_b, m_mlp_w1, m_mlp_w2, m_ln2_g, m_ln2_b, v_even_w_in, v_even_sinks, v_even_w_out, v_odd_w_in, v_odd_q_norm_g, v_odd_kv_norm_g, v_odd_w_uq, v_odd_w_ukv, v_odd_w_out, v_ln1_g, v_ln1_b, v_mlp_w1, v_mlp_w2, v_ln2_g, v_ln2_b):
    chip = 2 * lax.axis_index("x") + lax.axis_index("y")
    e_w, o_w = EVEN_IN // N_CHIPS, ODD_IN // N_CHIPS

    shards = dict(
        even_w_in=jnp.pad(even_w_in[0], ((0, 0), (0, EVEN_IN_PAD - e_w))), even_w_out=even_w_out[0],
        odd_w_in=jnp.pad(odd_w_in[0], ((0, 0), (0, ODD_IN_PAD - o_w))), odd_w_uq=odd_w_uq[0], odd_w_ukv=odd_w_ukv[0],
        odd_w_out=odd_w_out[0], mlp_w1_0=mlp_w1[0], mlp_w1_1=mlp_w1[1], mlp_w2_0=mlp_w2[0], mlp_w2_1=mlp_w2[1])
    names = list(shards)
    own16 = {n: shards[n].astype(BF16) for n in names}
    gather_started, token = [], own16[WEIGHT_GROUPS[0][0]]
    for g, group in enumerate(WEIGHT_GROUPS):
        lands = [jax.ShapeDtypeStruct((N_CHIPS,) + own16[n].shape, BF16) for n in group]
        gather_started.append(_copies_start([own16[n] for n in group], lands, token, SAME_CORE_OF_OTHER_CHIPS, _gather_src,
                                            _gather_dst, name=f"gather{g}_start"))
        token = gather_started[-1][-1]
    all_started = token

    forwarding = {}

    def forward(g, after):
        _, landed = _copies_wait(gather_started[g], after, SAME_CORE_OF_OTHER_CHIPS, _gather_src, _gather_landed,
                                 name=f"gather{g}_wait")
        forwarding[g] = _forward_start(landed, name=f"gather{g}_forward")

    def weights_for(g, after):
        if g not in forwarding:
            forward(g, all_started if g == 0 else after)
        full = _forward_wait(forwarding[g], after, name=f"gather{g}_forwarded")
        if 1 <= g < len(WEIGHT_GROUPS) - 1:
            forward(g + 1, after)
        return {n: lax.dynamic_update_slice(f, own16[n][None], (chip, 0, 0)) for n, f in zip(WEIGHT_GROUPS[g], full)}

    grads_started = {}

    def send_grads(g, shares, after=None):
        grads_started[g] = _reduce_scatter_start([shares[n] for n in GRAD_GROUPS[g]], after, name=f"grads{g}")
        return grads_started[g][-1]

    norm_rows = jnp.pad(jnp.concatenate([odd_q_norm_g[0], odd_kv_norm_g[0]])[None], ((0, 7), (0, 64)))
    norm_all = _all_gather8(norm_rows, name="gather_norm_gains")[0::16]
    qn_w, kvn_w = D_Q_RANK // N_CHIPS, D_KV_RANK // N_CHIPS
    small = dict(even_sinks=even_sinks[0], odd_q_norm_g=norm_all[:, :qn_w].reshape(D_Q_RANK),
                 odd_kv_norm_g=norm_all[:, qn_w:qn_w + kvn_w].reshape(D_KV_RANK), ln1_g=ln1_g, ln1_b=ln1_b, ln2_g=ln2_g, ln2_b=ln2_b)

    small_started = []

    def send_small(sm):
        pack = _pack_small([sm[n] for n in SMALL_ORDER])
        small_started.append(_copies_start([pack], [jax.ShapeDtypeStruct((8,) + pack.shape, F32)], pack, ALL_OTHER_DEVICES,
                                           _small_src, _small_dst, name="small_grads_start"))
        return small_started[0][-1]

    def small_sum(after):
        (pack,), (landed,) = _copies_wait(small_started[0], after, ALL_OTHER_DEVICES, _small_src, _small_landed,
                                          name="small_grads_wait")
        device = 4 * lax.axis_index("x") + 2 * lax.axis_index("y") + lax.axis_index("c")
        return _sum_slabs(lax.dynamic_update_slice(landed, pack[None], (device, 0, 0)), name="sum_small_grads")

    loss, grad_x = _local_step(x[0], loss_target[0], small, weights_for, send_grads, send_small,
                               lambda row: lax.psum(row[0, 0], ("x", "y", "c")))

    weights = dict(even_w_in=even_w_in, even_sinks=even_sinks, even_w_out=even_w_out, odd_w_in=odd_w_in, odd_q_norm_g=odd_q_norm_g,
                   odd_kv_norm_g=odd_kv_norm_g, odd_w_uq=odd_w_uq, odd_w_ukv=odd_w_ukv, odd_w_out=odd_w_out, ln1_g=ln1_g, ln1_b=ln1_b,
                   mlp_w1=mlp_w1, mlp_w2=mlp_w2, ln2_g=ln2_g, ln2_b=ln2_b)
    m_in = dict(even_w_in=m_even_w_in, even_sinks=m_even_sinks, even_w_out=m_even_w_out, odd_w_in=m_odd_w_in,
                odd_q_norm_g=m_odd_q_norm_g, odd_kv_norm_g=m_odd_kv_norm_g, odd_w_uq=m_odd_w_uq, odd_w_ukv=m_odd_w_ukv,
                odd_w_out=m_odd_w_out, ln1_g=m_ln1_g, ln1_b=m_ln1_b, mlp_w1=m_mlp_w1, mlp_w2=m_mlp_w2, ln2_g=m_ln2_g, ln2_b=m_ln2_b)
    v_in = dict(even_w_in=v_even_w_in, even_sinks=v_even_sinks, even_w_out=v_even_w_out, odd_w_in=v_odd_w_in,
                odd_q_norm_g=v_odd_q_norm_g, odd_kv_norm_g=v_odd_kv_norm_g, odd_w_uq=v_odd_w_uq, odd_w_ukv=v_odd_w_ukv,
                odd_w_out=v_odd_w_out, ln1_g=v_ln1_g, ln1_b=v_ln1_b, mlp_w1=v_mlp_w1, mlp_w2=v_mlp_w2, ln2_g=v_ln2_g, ln2_b=v_ln2_b)
    order = list(weights)
    updated = {}

    def update(n, g2d, layer=0, tag="", by_column=False):
        shape = weights[n].shape
        as3d = (1, math.prod(shape[:-1]), shape[-1]) if len(shape) == 2 else shape
        view = (lambda t: jnp.swapaxes(t.reshape(as3d), 1, 2)) if by_column else (lambda t: t.reshape(as3d))
        res = _adamw(view(weights[n]), g2d.T if by_column else g2d, view(m_in[n]), view(v_in[n]), layer=layer,
                     carry=updated.get(n), name=f"adamw_{n}{tag}")
        updated[n] = tuple(jnp.swapaxes(t, 1, 2) for t in res) if by_column else tuple(res)
        return res[0]

    after = grad_x
    for g, group in enumerate(GRAD_GROUPS):
        for n, r in zip(group, _reduce_scatter_finish(grads_started[g], after, name=f"grads{g}")):
            if n[:-2] in ("mlp_w1", "mlp_w2"):
                after = update(n[:-2], r, layer=int(n[-1]), tag=n[-2:])
            elif n == "even_w_in":
                after = update(n, r[:, :e_w], by_column=True)
            elif n == "odd_w_in":
                after = update(n, r[:, :o_w], by_column=True)
            else:
                after = update(n, r)
        if g == len(GRAD_GROUPS) - 2:
            g_ln1g, g_ln1b, g_ln2g, g_ln2b, g_sinks, g_qn, g_kvn = _unpack_small(
                small_sum(after), [(DEPTH, D_MODEL)] * 4 + [(1, A_Q_HEADS), (D_Q_RANK,), (D_KV_RANK,)])
            for n, gs in (("even_sinks", g_sinks), ("odd_q_norm_g", lax.dynamic_slice_in_dim(g_qn, chip * qn_w, qn_w)[None]),
                          ("odd_kv_norm_g", lax.dynamic_slice_in_dim(g_kvn, chip * kvn_w, kvn_w)[None]), ("ln1_g", g_ln1g),
                          ("ln1_b", g_ln1b), ("ln2_g", g_ln2g), ("ln2_b", g_ln2b)):
                update(n, gs)
    outs = [[updated[n][k].reshape(weights[n].shape) for n in order] for k in range(4)]
    return (loss, grad_x[None], *outs[0], *outs[1], *outs[2], *outs[3])
```

```python
import math

import jax
import jax.numpy as jnp
from jax import lax
from jax.experimental import pallas as pl
from jax.experimental.pallas import tpu as pltpu

F32 = jnp.float32
BF16 = jnp.bfloat16
MESH = pl.DeviceIdType.MESH

D_MODEL = 2048
SEQ = 2048
DEPTH = 2
HEAD_DIM = 64
A_Q_HEADS, A_KV_HEADS, A_WINDOW = 16, 2, 128
B_HEADS = 8
B_PATTERNS = ((128, 1), (512, 4), (2048, 16))
C_HEADS = 16
D_HEADS, D_Q_RANK, D_KV_RANK, D_NOPE, D_ROPE, D_V = 16, 512, 256, 64, 32, 64
ROPE_BASE = 10000.0
D_FF = 4 * D_MODEL
LN_EPS = 1e-5
RMS_EPS = 1e-6
ALPHA = (2 * DEPTH) ** 0.25
A_Q_W = A_Q_HEADS * HEAD_DIM
A_KV_W = A_KV_HEADS * HEAD_DIM
B_W = B_HEADS * HEAD_DIM
EVEN_IN = A_Q_W + 2 * A_KV_W + 3 * B_W * len(B_PATTERNS)
EVEN_OUT = A_Q_W + B_W
C_W = C_HEADS * HEAD_DIM
ODD_IN = 3 * C_W + D_Q_RANK + D_KV_RANK + D_ROPE
ADAM_LR, ADAM_B1, ADAM_B2, ADAM_EPS, ADAM_WD, ADAM_STEP = 0.001, 0.9, 0.999, 1e-08, 0.01, 10

N_CHIPS = 4
EVEN_IN_PAD = 1536
ODD_IN_PAD = 1024
ATT_BLK = 128
NEG = -1e30
V7X_VMEM_LIMIT = 48 * 1024 * 1024


def _params(*sem):
    return pltpu.CompilerParams(dimension_semantics=sem, vmem_limit_bytes=V7X_VMEM_LIMIT)


def _tile(n, cap):
    if n <= cap:
        return n
    t = cap - cap % 128
    while n % t:
        t -= 128
    return t


def _matmul(a, b, *, mode, out_dtype, name, epilogue=None, extra=None, alpha=1.0, after=None, tm=1024, tn=1024, tk=2048):
    if mode == "nn":
        M, K = a.shape
        P, _, Np = b.shape
        tm, tn, tk = _tile(M, tm), _tile(Np, tn), _tile(K, tk)
        nj = Np // tn
        grid = (M // tm, P * nj, K // tk)
        a_spec = pl.BlockSpec((tm, tk), lambda i, j, k: (i, k))
        b_spec = pl.BlockSpec((None, tk, tn), lambda i, j, k: (j // nj, k, j % nj))
        o_spec = pl.BlockSpec((tm, tn), lambda i, j, k: (i, j))
        out_shape = (M, P * Np)
        dims = (((1,), (0,)), ((), ()))
    elif mode == "nt":
        M, N = a.shape
        P, K, Np = b.shape
        tm, tn, tk = _tile(M, tm), _tile(K, tn), _tile(Np, tk)
        nkk = Np // tk
        grid = (M // tm, K // tn, P * nkk)
        a_spec = pl.BlockSpec((tm, tk), lambda i, j, k: (i, k))
        b_spec = pl.BlockSpec((None, tn, tk), lambda i, j, k: (k // nkk, j, k % nkk))
        o_spec = pl.BlockSpec((tm, tn), lambda i, j, k: (i, j))
        out_shape = (M, K)
        dims = (((1,), (1,)), ((), ()))
    else:
        raise ValueError(mode)
    n_k = grid[2]
    n_extra = 0 if extra is None else 1
    n_after = 0 if after is None else 1
    n_out = 2 if epilogue == "relu2" else 1

    def body(*refs):
        a_ref, b_ref = refs[:2]
        e_ref = refs[2] if n_extra else None
        outs = refs[2 + n_extra + n_after:2 + n_extra + n_after + n_out]
        part = lax.dot_general(a_ref[...], b_ref[...], dims, preferred_element_type=F32)

        def finish(r):
            if epilogue == "relu2":
                outs[0][...] = r.astype(outs[0].dtype)
                rp = jnp.maximum(r, 0.0)
                outs[1][...] = (rp * rp).astype(outs[1].dtype)
            elif epilogue == "drelu2":
                outs[0][...] = (r * (2.0 * jnp.maximum(e_ref[...].astype(F32), 0.0))).astype(outs[0].dtype)
            elif epilogue == "add":
                outs[0][...] = (r + alpha * e_ref[...].astype(F32)).astype(outs[0].dtype)
            else:
                outs[0][...] = r.astype(outs[0].dtype)

        if n_k == 1:
            finish(part)
        else:
            acc = refs[-1]
            k = pl.program_id(2)

            @pl.when(k == 0)
            def _():
                acc[...] = part

            @pl.when((k > 0) & (k < n_k - 1))
            def _():
                acc[...] += part

            @pl.when(k == n_k - 1)
            def _():
                finish(acc[...] + part)

    in_specs = [a_spec, b_spec] + ([o_spec] if n_extra else []) + ([pl.BlockSpec(memory_space=pl.ANY)] if n_after else [])
    shapes = [jax.ShapeDtypeStruct(out_shape, out_dtype)] * n_out
    res = pl.pallas_call(
        body, name=name, grid=grid, in_specs=in_specs,
        out_specs=[o_spec] * n_out, out_shape=shapes,
        scratch_shapes=[pltpu.VMEM((tm, tn), F32)] if n_k > 1 else [],
        compiler_params=_params("parallel", "parallel", "arbitrary"),
    )(a, b, *([extra] if n_extra else []), *([after] if n_after else []))
    return res if n_out > 1 else res[0]


def _matmul_tn(a, g, *, shards, name, tm=1024, tn=1024):
    M, K = a.shape
    P = shards
    Np = g.shape[1] // P
    tm, tn = _tile(K, tm), _tile(Np, tn)
    nj = Np // tn

    def body(a_ref, g_ref, o_ref):
        o_ref[...] = lax.dot_general(a_ref[...], g_ref[...], (((0,), (0,)), ((), ())), preferred_element_type=F32).astype(BF16)

    return pl.pallas_call(
        body, name=name, grid=(K // tm, P * nj),
        in_specs=[pl.BlockSpec((M, tm), lambda i, j: (0, i)), pl.BlockSpec((M, tn), lambda i, j: (0, j))],
        out_specs=pl.BlockSpec((None, tm, tn), lambda i, j: (j // nj, i, j % nj)),
        out_shape=jax.ShapeDtypeStruct((P, K, Np), BF16),
        compiler_params=_params("parallel", "parallel"),
    )(a, g)


def _norm_fwd(x, res, g, b, *, center, eps, alpha, name, rows=256):
    S, W = x.shape
    has_res = res is not None
    rows = _tile(S, rows)

    def body(*refs):
        x_ref = refs[0]
        r_ref = refs[1] if has_res else None
        g_ref = refs[1 + has_res]
        b_ref = refs[2 + has_res] if center else None
        y_ref, y16_ref, u_ref = refs[-3:]
        u = x_ref[...]
        if has_res:
            u = alpha * u + r_ref[...]
        if center:
            mu = jnp.mean(u, axis=1, keepdims=True)
            xc = u - mu
        else:
            xc = u
        var = jnp.mean(xc * xc, axis=1, keepdims=True)
        y = xc * lax.rsqrt(var + eps) * g_ref[...]
        if center:
            y = y + b_ref[...]
        y_ref[...] = y
        y16_ref[...] = y.astype(BF16)
        u_ref[...] = u

    row_spec = pl.BlockSpec((rows, W), lambda i: (i, 0))
    vec_spec = pl.BlockSpec((1, W), lambda i: (0, 0))
    ins = [x] + ([res] if has_res else []) + [g.reshape(1, W)] + ([b.reshape(1, W)] if center else [])
    specs = [row_spec] + ([row_spec] if has_res else []) + [vec_spec] + ([vec_spec] if center else [])
    return pl.pallas_call(
        body, name=name, grid=(S // rows,), in_specs=specs,
        out_specs=[row_spec, row_spec, row_spec],
        out_shape=[jax.ShapeDtypeStruct((S, W), F32), jax.ShapeDtypeStruct((S, W), BF16), jax.ShapeDtypeStruct((S, W), F32)],
        compiler_params=_params("parallel"),
    )(*ins)


def _norm_bwd(dy, u, g, *, center, eps, name, rows=256, after=None):
    S, W = u.shape
    rows = _tile(S, rows)

    def body(dy_ref, u_ref, g_ref, *rest):
        du_ref, du16_ref, dg_ref, db_ref = rest[-4:]
        i = pl.program_id(0)
        uu = u_ref[...]
        dyv = dy_ref[...]
        if center:
            xc = uu - jnp.mean(uu, axis=1, keepdims=True)
        else:
            xc = uu
        rstd = lax.rsqrt(jnp.mean(xc * xc, axis=1, keepdims=True) + eps)
        xhat = xc * rstd
        dxh = dyv * g_ref[...]
        c2 = jnp.mean(dxh * xhat, axis=1, keepdims=True)
        du = dxh - xhat * c2
        if center:
            du = du - jnp.mean(dxh, axis=1, keepdims=True)
        du = du * rstd
        du_ref[...] = du
        du16_ref[...] = du.astype(BF16)

        @pl.when(i == 0)
        def _():
            dg_ref[...] = jnp.zeros_like(dg_ref)
            db_ref[...] = jnp.zeros_like(db_ref)

        dg_ref[...] += jnp.sum(dyv * xhat, axis=0, keepdims=True)
        db_ref[...] += jnp.sum(dyv, axis=0, keepdims=True)

    row_spec = pl.BlockSpec((rows, W), lambda i: (i, 0))
    vec_spec = pl.BlockSpec((1, W), lambda i: (0, 0))
    return pl.pallas_call(
        body, name=name, grid=(S // rows,),
        in_specs=[row_spec, row_spec, vec_spec] + ([] if after is None else [pl.BlockSpec(memory_space=pl.ANY)]),
        out_specs=[row_spec, row_spec, vec_spec, vec_spec],
        out_shape=[jax.ShapeDtypeStruct((S, W), F32), jax.ShapeDtypeStruct((S, W), BF16),
                   jax.ShapeDtypeStruct((1, W), F32), jax.ShapeDtypeStruct((1, W), F32)],
        compiler_params=_params("arbitrary"),
    )(dy, u, g.reshape(1, W), *([] if after is None else [after]))


def _loss_head(y, target, *, name, rows=256):
    S, W = y.shape
    rows = _tile(S, rows)

    def body(y_ref, t_ref, dy_ref, l_ref):
        i = pl.program_id(0)
        e = y_ref[...] - t_ref[...]
        dy_ref[...] = e * (1.0 / W)

        @pl.when(i == 0)
        def _():
            l_ref[...] = jnp.zeros_like(l_ref)

        l_ref[...] += jnp.full(l_ref.shape, 0.5 / W * jnp.sum(e * e), F32)

    row_spec = pl.BlockSpec((rows, W), lambda i: (i, 0))
    return pl.pallas_call(
        body, name=name, grid=(S // rows,), in_specs=[row_spec, row_spec],
        out_specs=[row_spec, pl.BlockSpec((1, 128), lambda i: (0, 0))],
        out_shape=[jax.ShapeDtypeStruct((S, W), F32), jax.ShapeDtypeStruct((1, 128), F32)],
        compiler_params=_params("arbitrary"),
    )(y, target)


def _adamw(w, g, m, v, *, name, layer=0, carry=None, rows=256):
    L, R, C = w.shape
    rows = max(t for t in range(8, rows + 1, 8) if R % t == 0) if R % 8 == 0 else R
    c1 = 1.0 / (1.0 - ADAM_B1 ** ADAM_STEP)
    c2 = 1.0 / (1.0 - ADAM_B2 ** ADAM_STEP)

    def body(w_ref, g_ref, m_ref, v_ref, *rest):
        go_ref, d_ref, mo_ref, vo_ref = rest[-4:]
        gg = g_ref[...]
        mn = ADAM_B1 * m_ref[...] + (1.0 - ADAM_B1) * gg
        vn = ADAM_B2 * v_ref[...] + (1.0 - ADAM_B2) * (gg * gg)
        go_ref[...] = gg
        d_ref[...] = -ADAM_LR * ((mn * c1) / (jnp.sqrt(vn * c2) + ADAM_EPS) + ADAM_WD * w_ref[...])
        mo_ref[...] = mn
        vo_ref[...] = vn

    slab = pl.BlockSpec((None, rows, C), lambda i: (layer, i, 0))
    shp = jax.ShapeDtypeStruct((L, R, C), F32)
    carried = [] if carry is None else list(carry)
    return pl.pallas_call(
        body, name=name, grid=(R // rows,),
        in_specs=[slab, pl.BlockSpec((rows, C), lambda i: (i, 0)), slab, slab] + [pl.BlockSpec(memory_space=pl.ANY)] * len(carried),
        out_specs=[slab] * 4, out_shape=[shp] * 4, input_output_aliases={4 + k: k for k in range(len(carried))},
        compiler_params=_params("parallel"),
    )(w, g, m, v, *carried)


def _rope(x1, x2, cos, sin, *, name, rows=512):
    H, S, W = x1.shape
    rows = _tile(S, rows)

    def body(x1_ref, x2_ref, c_ref, s_ref, y1_ref, y2_ref):
        t1 = jnp.sum(x1_ref[...], axis=0)
        t2 = jnp.sum(x2_ref[...], axis=0)
        c = c_ref[...]
        s = s_ref[...]
        y1_ref[...] = t1 * c - t2 * s
        y2_ref[...] = t1 * s + t2 * c

    xs = pl.BlockSpec((H, rows, W), lambda i: (0, i, 0))
    ts = pl.BlockSpec((rows, W), lambda i: (i, 0))
    shp = jax.ShapeDtypeStruct((S, W), F32)
    return pl.pallas_call(
        body, name=name, grid=(S // rows,), in_specs=[xs, xs, ts, ts], out_specs=[ts, ts], out_shape=[shp, shp],
        compiler_params=_params("parallel"),
    )(x1, x2, cos, sin)


def _band_mask(blk, n, n_back):
    row = lax.broadcasted_iota(jnp.int32, (blk, 2 * blk), 0)
    col = lax.broadcasted_iota(jnp.int32, (blk, 2 * blk), 1)
    rel = blk + row - col
    return rel, (rel >= 0) & (rel <= n_back) & ((col >= blk) | (n >= 1))


def _window(ref, head, i, blk):
    before = pl.multiple_of(jnp.maximum(i - 1, 0) * blk, blk)
    return jnp.concatenate([ref[head, pl.ds(before, blk), :], ref[head, pl.ds(pl.multiple_of(i * blk, blk), blk), :]], axis=0)


def _window_add(ref, head, i, blk, update):
    before = pl.multiple_of(jnp.maximum(i - 1, 0) * blk, blk)
    ref[head, pl.ds(before, blk), :] += update[:blk]
    ref[head, pl.ds(pl.multiple_of(i * blk, blk), blk), :] += update[blk:]


def _band_fwd(q, k, v, slopes, sinks, *, hps, nb_seq, n_back, scale, dist_scale, name):
    blk = ATT_BLK
    Hq, T, d = q.shape
    Hk = k.shape[0]
    group = Hq // Hk
    kps = max(hps // group, 1)
    use_sink = sinks is not None

    def body(*refs):
        q_ref, k_ref, v_ref, slope_ref = refs[:4]
        sink_ref = refs[4] if use_sink else None
        o_ref, lse_ref = refs[-2:]
        i = pl.program_id(1)
        rel, valid = _band_mask(blk, i % nb_seq, n_back)
        relf = rel.astype(F32)
        for hh in range(hps):
            h = pl.program_id(0) * hps + hh
            kk = _window(k_ref, hh // group, i, blk)
            vv = _window(v_ref, hh // group, i, blk)
            s = lax.dot_general(q_ref[hh], kk, (((1,), (1,)), ((), ())), preferred_element_type=F32) * scale
            s = jnp.where(valid, s - (slope_ref[h] * dist_scale) * relf, NEG)
            m = jnp.max(s, axis=1, keepdims=True)
            if use_sink:
                m = jnp.maximum(m, sink_ref[h])
            p = jnp.where(valid, jnp.exp(s - m), 0.0)
            l = jnp.sum(p, axis=1, keepdims=True)
            if use_sink:
                l = l + jnp.exp(sink_ref[h] - m)
            o_ref[hh] = jnp.dot(p.astype(BF16), vv, preferred_element_type=F32) / l
            lse_ref[hh] = m + jnp.log(l)

    smem = pl.BlockSpec(memory_space=pltpu.SMEM)
    qs = pl.BlockSpec((hps, blk, d), lambda g, i: (g, i, 0))
    ks = pl.BlockSpec((kps, T, d), lambda g, i: (g, 0, 0))
    ins = [q, k, v, slopes] + ([sinks] if use_sink else [])
    return pl.pallas_call(
        body, name=name, grid=(Hq // hps, T // blk), in_specs=[qs, ks, ks, smem] + ([smem] if use_sink else []),
        out_specs=[qs, pl.BlockSpec((hps, blk, 1), lambda g, i: (g, i, 0))],
        out_shape=[jax.ShapeDtypeStruct((Hq, T, d), F32), jax.ShapeDtypeStruct((Hq, T, 1), F32)],
        compiler_params=_params("parallel", "parallel"),
    )(*ins)


def _band_bwd(q, k, v, o, do, lse, dlse, slopes, sinks, *, hps, nb_seq, n_back, scale, dist_scale, name):
    blk = ATT_BLK
    Hq, T, d = q.shape
    Hk = k.shape[0]
    group = Hq // Hk
    kps = max(hps // group, 1)
    use_sink, use_dlse = sinks is not None, dlse is not None

    def body(*refs):
        q_ref, k_ref, v_ref, o_ref, do_ref, lse_ref = refs[:6]
        pos = 6
        dlse_ref = refs[pos] if use_dlse else None
        pos += use_dlse
        slope_ref = refs[pos]
        pos += 1
        sink_ref = refs[pos] if use_sink else None
        pos += use_sink
        dq_ref, dk_ref, dv_ref = refs[pos:pos + 3]
        dsink_ref = refs[pos + 3] if use_sink else None
        i = pl.program_id(1)
        rel, valid = _band_mask(blk, i % nb_seq, n_back)
        relf = rel.astype(F32)

        @pl.when(i == 0)
        def _():
            dk_ref[...] = jnp.zeros_like(dk_ref)
            dv_ref[...] = jnp.zeros_like(dv_ref)
            if use_sink:
                dsink_ref[...] = jnp.zeros_like(dsink_ref)

        for kh in range(kps):
            dk_acc = jnp.zeros((2 * blk, d), F32)
            dv_acc = jnp.zeros((2 * blk, d), F32)
            kk = _window(k_ref, kh, i, blk)
            vv = _window(v_ref, kh, i, blk)
            for hh in range(kh * min(group, hps), (kh + 1) * min(group, hps)):
                h = pl.program_id(0) * hps + hh
                qb = q_ref[hh]
                dob = do_ref[hh]
                do16 = dob.astype(BF16)
                lse = lse_ref[hh]
                c = -jnp.sum(dob * o_ref[hh], axis=1, keepdims=True)
                if use_dlse:
                    c = c + dlse_ref[hh]
                s = lax.dot_general(qb, kk, (((1,), (1,)), ((), ())), preferred_element_type=F32) * scale
                s = jnp.where(valid, s - (slope_ref[h] * dist_scale) * relf, NEG)
                p = jnp.where(valid, jnp.exp(s - lse), 0.0)
                dp = lax.dot_general(do16, vv, (((1,), (1,)), ((), ())), preferred_element_type=F32)
                ds16 = (p * (dp + c) * scale).astype(BF16)
                dq_ref[hh] = jnp.dot(ds16, kk, preferred_element_type=F32)
                dk_acc = dk_acc + lax.dot_general(ds16, qb, (((0,), (0,)), ((), ())), preferred_element_type=F32)
                dv_acc = dv_acc + lax.dot_general(p.astype(BF16), do16, (((0,), (0,)), ((), ())), preferred_element_type=F32)
                if use_sink:
                    dsink_ref[hh] += jnp.full((1, 128), jnp.sum(jnp.exp(sink_ref[h] - lse) * c), F32)
            _window_add(dk_ref, kh, i, blk, dk_acc)
            _window_add(dv_ref, kh, i, blk, dv_acc)

    smem = pl.BlockSpec(memory_space=pltpu.SMEM)
    qs = pl.BlockSpec((hps, blk, d), lambda g, i: (g, i, 0))
    ls = pl.BlockSpec((hps, blk, 1), lambda g, i: (g, i, 0))
    ks = pl.BlockSpec((kps, T, d), lambda g, i: (g, 0, 0))
    in_specs = [qs, ks, ks, qs, qs, ls] + ([ls] if use_dlse else []) + [smem] + ([smem] if use_sink else [])
    ins = [q, k, v, o, do, lse] + ([dlse] if use_dlse else []) + [slopes] + ([sinks] if use_sink else [])
    out_specs = [qs, ks, ks]
    out_shape = [jax.ShapeDtypeStruct((Hq, T, d), F32), jax.ShapeDtypeStruct((Hk, T, d), F32),
                 jax.ShapeDtypeStruct((Hk, T, d), F32)]
    if use_sink:
        out_specs.append(pl.BlockSpec((hps, 1, 128), lambda g, i: (g, 0, 0)))
        out_shape.append(jax.ShapeDtypeStruct((Hq, 1, 128), F32))
    return pl.pallas_call(
        body, name=name, grid=(Hq // hps, T // blk), in_specs=in_specs, out_specs=out_specs, out_shape=out_shape,
        compiler_params=_params("parallel", "arbitrary"),
    )(*ins)


def _diagonal_mask(blk):
    return lax.broadcasted_iota(jnp.int32, (blk, blk), 0) >= lax.broadcasted_iota(jnp.int32, (blk, blk), 1)


def _causal_fwd(q, k, v, *, blk, hps, scale, name):
    H, T, dqk = q.shape
    dv = v.shape[2]

    def body(q_ref, k_ref, v_ref, o_ref, lse_ref):
        n = pl.program_id(1)
        qs = [q_ref[hh] for hh in range(hps)]

        def block(j, carry, valid):
            start = pl.multiple_of(j * blk, blk)
            out = []
            for hh in range(hps):
                m, l, acc = carry[hh]
                kb = k_ref[hh, pl.ds(start, blk), :]
                vb = v_ref[hh, pl.ds(start, blk), :]
                s = lax.dot_general(qs[hh], kb, (((1,), (1,)), ((), ())), preferred_element_type=F32) * scale
                if valid is not None:
                    s = jnp.where(valid, s, NEG)
                m_new = jnp.maximum(m, jnp.max(s, axis=1, keepdims=True))
                p = _keep(valid, jnp.exp(s - m_new))
                a = jnp.exp(m - m_new)
                out.append((m_new, a * l + jnp.sum(p, axis=1, keepdims=True),
                            a * acc + jnp.dot(p.astype(BF16), vb, preferred_element_type=F32)))
            return tuple(out)

        init = tuple((jnp.full((blk, 1), NEG, F32), jnp.zeros((blk, 1), F32), jnp.zeros((blk, dv), F32)) for _ in range(hps))
        res = block(n, lax.fori_loop(0, n, lambda j, carry: block(j, carry, None), init), _diagonal_mask(blk))
        for hh in range(hps):
            m, l, acc = res[hh]
            o_ref[hh] = acc / l
            lse_ref[hh] = m + jnp.log(l)

    qs_ = pl.BlockSpec((hps, blk, dqk), lambda g, i: (g, i, 0))
    return pl.pallas_call(
        body, name=name, grid=(H // hps, T // blk),
        in_specs=[qs_, pl.BlockSpec((hps, T, dqk), lambda g, i: (g, 0, 0)), pl.BlockSpec((hps, T, dv), lambda g, i: (g, 0, 0))],
        out_specs=[pl.BlockSpec((hps, blk, dv), lambda g, i: (g, i, 0)), pl.BlockSpec((hps, blk, 1), lambda g, i: (g, i, 0))],
        out_shape=[jax.ShapeDtypeStruct((H, T, dv), F32), jax.ShapeDtypeStruct((H, T, 1), F32)],
        compiler_params=_params("parallel", "parallel"),
    )(q, k, v)


def _causal_bwd(q, k, v, o, do, lse, *, blk, hps, scale, name):
    H, T, dqk = q.shape
    dv = v.shape[2]

    def body(q_ref, k_ref, v_ref, o_ref, do_ref, lse_ref, dq_ref, dk_ref, dv_ref):
        n = pl.program_id(1)

        @pl.when(n == 0)
        def _():
            dk_ref[...] = jnp.zeros_like(dk_ref)
            dv_ref[...] = jnp.zeros_like(dv_ref)

        qs = [q_ref[hh] for hh in range(hps)]
        do16 = [do_ref[hh].astype(BF16) for hh in range(hps)]
        lses = [lse_ref[hh] for hh in range(hps)]
        cs = [-jnp.sum(do_ref[hh] * o_ref[hh], axis=1, keepdims=True) for hh in range(hps)]

        def block(j, dqs, valid):
            start = pl.multiple_of(j * blk, blk)
            out = []
            for hh in range(hps):
                kb = k_ref[hh, pl.ds(start, blk), :]
                vb = v_ref[hh, pl.ds(start, blk), :]
                s = lax.dot_general(qs[hh], kb, (((1,), (1,)), ((), ())), preferred_element_type=F32) * scale
                if valid is not None:
                    s = jnp.where(valid, s, NEG)
                p = _keep(valid, jnp.exp(s - lses[hh]))
                dp = lax.dot_general(do16[hh], vb, (((1,), (1,)), ((), ())), preferred_element_type=F32)
                ds16 = (p * (dp + cs[hh]) * scale).astype(BF16)
                dk_ref[hh, pl.ds(start, blk), :] += lax.dot_general(ds16, qs[hh], (((0,), (0,)), ((), ())), preferred_element_type=F32)
                dv_ref[hh, pl.ds(start, blk), :] += lax.dot_general(p.astype(BF16), do16[hh], (((0,), (0,)), ((), ())),
                                                                    preferred_element_type=F32)
                out.append(dqs[hh] + jnp.dot(ds16, kb, preferred_element_type=F32))
            return tuple(out)

        init = tuple(jnp.zeros((blk, dqk), F32) for _ in range(hps))
        res = block(n, lax.fori_loop(0, n, lambda j, dqs: block(j, dqs, None), init), _diagonal_mask(blk))
        for hh in range(hps):
            dq_ref[hh] = res[hh]

    qs_ = pl.BlockSpec((hps, blk, dqk), lambda g, i: (g, i, 0))
    os_ = pl.BlockSpec((hps, blk, dv), lambda g, i: (g, i, 0))
    ls_ = pl.BlockSpec((hps, blk, 1), lambda g, i: (g, i, 0))
    ks_ = pl.BlockSpec((hps, T, dqk), lambda g, i: (g, 0, 0))
    vs_ = pl.BlockSpec((hps, T, dv), lambda g, i: (g, 0, 0))
    return pl.pallas_call(
        body, name=name, grid=(H // hps, T // blk), in_specs=[qs_, ks_, vs_, os_, os_, ls_], out_specs=[qs_, ks_, vs_],
        out_shape=[jax.ShapeDtypeStruct((H, T, dqk), F32), jax.ShapeDtypeStruct((H, T, dqk), F32),
                   jax.ShapeDtypeStruct((H, T, dv), F32)],
        compiler_params=_params("parallel", "arbitrary"),
    )(q, k, v, o, do, lse)


def _tri_sum(x, upper):
    hi = x.astype(BF16)
    lo = (x - hi.astype(F32)).astype(BF16)
    return jnp.dot(hi, upper, preferred_element_type=F32) + jnp.dot(lo, upper, preferred_element_type=F32)


def _keep(mask, x):
    return x if mask is None else jnp.where(mask, x, 0.0)


def _stick_terms(qb, kb, scale, strict):
    z = lax.dot_general(qb, kb, (((1,), (1,)), ((), ())), preferred_element_type=F32) * scale
    e = jnp.exp(-jnp.abs(z))
    lb = jnp.minimum(z, 0.0) - jnp.log(1.0 + e)
    return z, e, lb, _keep(strict, lb - z)


def _stick_fwd(q, k, v, *, blk, hps, scale, name):
    H, T, dh = q.shape

    def body(q_ref, k_ref, v_ref, o_ref, tot_ref):
        n = pl.program_id(1)
        qs = [q_ref[hh] for hh in range(hps)]
        r_i = lax.broadcasted_iota(jnp.int32, (blk, blk), 0)
        c_i = lax.broadcasted_iota(jnp.int32, (blk, blk), 1)
        upper = (r_i > c_i).astype(BF16)

        def block(j, carry, strict):
            start = pl.multiple_of(j * blk, blk)
            out = []
            for hh in range(hps):
                run, acc = carry[hh]
                kb = k_ref[hh, pl.ds(start, blk), :]
                vb = v_ref[hh, pl.ds(start, blk), :]
                _, _, lb, lk = _stick_terms(qs[hh], kb, scale, strict)
                w = _keep(strict, jnp.exp(lb + run + _tri_sum(lk, upper)))
                out.append((run + jnp.sum(lk, axis=1, keepdims=True), acc + jnp.dot(w.astype(BF16), vb, preferred_element_type=F32)))
            return tuple(out)

        init = tuple((jnp.zeros((blk, 1), F32), jnp.zeros((blk, dh), F32)) for _ in range(hps))
        res = lax.fori_loop(1, n + 1, lambda t, carry: block(n - t, carry, None), block(n, init, r_i > c_i))
        for hh in range(hps):
            tot_ref[hh], o_ref[hh] = res[hh]

    qs_ = pl.BlockSpec((hps, blk, dh), lambda g, i: (g, i, 0))
    ks_ = pl.BlockSpec((hps, T, dh), lambda g, i: (g, 0, 0))
    ts_ = pl.BlockSpec((hps, blk, 1), lambda g, i: (g, i, 0))
    return pl.pallas_call(
        body, name=name, grid=(H // hps, T // blk), in_specs=[qs_, ks_, ks_], out_specs=[qs_, ts_],
        out_shape=[jax.ShapeDtypeStruct((H, T, dh), F32), jax.ShapeDtypeStruct((H, T, 1), F32)],
        compiler_params=_params("parallel", "parallel"),
    )(q, k, v)


def _stick_bwd(q, k, v, tot, do, *, blk, hps, scale, name):
    H, T, dh = q.shape

    def body(q_ref, k_ref, v_ref, tot_ref, do_ref, dq_ref, dk_ref, dv_ref):
        n = pl.program_id(1)
        qs = [q_ref[hh] for hh in range(hps)]
        do16 = [do_ref[hh].astype(BF16) for hh in range(hps)]
        tots = [tot_ref[hh] for hh in range(hps)]
        r_i = lax.broadcasted_iota(jnp.int32, (blk, blk), 0)
        c_i = lax.broadcasted_iota(jnp.int32, (blk, blk), 1)
        upto = (r_i <= c_i).astype(BF16)
        below = (r_i < c_i).astype(BF16)

        @pl.when(n == 0)
        def _():
            dk_ref[...] = jnp.zeros_like(dk_ref)
            dv_ref[...] = jnp.zeros_like(dv_ref)

        def block(j, carry, strict):
            start = pl.multiple_of(j * blk, blk)
            out = []
            for hh in range(hps):
                run, grun, dq = carry[hh]
                kb = k_ref[hh, pl.ds(start, blk), :]
                vb = v_ref[hh, pl.ds(start, blk), :]
                z, e, lb, lk = _stick_terms(qs[hh], kb, scale, strict)
                w = _keep(strict, jnp.exp(lb + tots[hh] - (run + _tri_sum(lk, upto))))
                dw = lax.dot_general(do16[hh], vb, (((1,), (1,)), ((), ())), preferred_element_type=F32)
                g = w * dw
                before = grun + jnp.dot(g.astype(BF16), below, preferred_element_type=F32)
                inv = 1.0 / (1.0 + e)
                sig = jnp.where(z >= 0, inv, e * inv)
                dz16 = (_keep(strict, g * (1.0 - sig) - sig * before) * scale).astype(BF16)
                dk_ref[hh, pl.ds(start, blk), :] += lax.dot_general(dz16, qs[hh], (((0,), (0,)), ((), ())), preferred_element_type=F32)
                dv_ref[hh, pl.ds(start, blk), :] += lax.dot_general(w.astype(BF16), do16[hh], (((0,), (0,)), ((), ())),
                                                                    preferred_element_type=F32)
                out.append((run + jnp.sum(lk, axis=1, keepdims=True), grun + jnp.sum(g, axis=1, keepdims=True),
                            dq + jnp.dot(dz16, kb, preferred_element_type=F32)))
            return tuple(out)

        zero = jnp.zeros((blk, 1), F32)
        init = tuple((zero, zero, jnp.zeros((blk, dh), F32)) for _ in range(hps))
        res = block(n, lax.fori_loop(0, n, lambda j, carry: block(j, carry, None), init), r_i > c_i)
        for hh in range(hps):
            dq_ref[hh] = res[hh][2]

    qs_ = pl.BlockSpec((hps, blk, dh), lambda g, i: (g, i, 0))
    ks_ = pl.BlockSpec((hps, T, dh), lambda g, i: (g, 0, 0))
    ts_ = pl.BlockSpec((hps, blk, 1), lambda g, i: (g, i, 0))
    shp = jax.ShapeDtypeStruct((H, T, dh), F32)
    return pl.pallas_call(
        body, name=name, grid=(H // hps, T // blk), in_specs=[qs_, ks_, ks_, ts_, qs_], out_specs=[qs_, ks_, ks_],
        out_shape=[shp, shp, shp],
        compiler_params=_params("parallel", "arbitrary"),
    )(q, k, v, tot, do)


def _mix_fwd(outs, lses, *, name, rows=512):
    H, T, dh = outs[0].shape
    rows = _tile(T, rows)

    def body(o0, o1, o2, l0, l1, l2, y_ref):
        ls = [l0[...], l1[...], l2[...]]
        mx = jnp.maximum(jnp.maximum(ls[0], ls[1]), ls[2])
        es = [jnp.exp(x - mx) for x in ls]
        den = es[0] + es[1] + es[2]
        y_ref[...] = (es[0] * o0[...] + es[1] * o1[...] + es[2] * o2[...]) / den

    os_ = pl.BlockSpec((None, rows, dh), lambda h, i: (h, i, 0))
    ls_ = pl.BlockSpec((None, rows, 1), lambda h, i: (h, i, 0))
    return pl.pallas_call(
        body, name=name, grid=(H, T // rows), in_specs=[os_] * 3 + [ls_] * 3, out_specs=os_,
        out_shape=jax.ShapeDtypeStruct((H, T, dh), F32),
        compiler_params=_params("parallel", "parallel"),
    )(*outs, *lses)


def _mix_bwd(outs, lses, dy, *, name, rows=512):
    H, T, dh = outs[0].shape
    rows = _tile(T, rows)

    def body(o0, o1, o2, l0, l1, l2, dy_ref, d0, d1, d2, g0, g1, g2):
        ls = [l0[...], l1[...], l2[...]]
        mx = jnp.maximum(jnp.maximum(ls[0], ls[1]), ls[2])
        es = [jnp.exp(x - mx) for x in ls]
        den = es[0] + es[1] + es[2]
        mix = [e / den for e in es]
        dyv = dy_ref[...]
        dm = [jnp.sum(dyv * o[...], axis=1, keepdims=True) for o in (o0, o1, o2)]
        avg = mix[0] * dm[0] + mix[1] * dm[1] + mix[2] * dm[2]
        for d_ref, g_ref, w, t in zip((d0, d1, d2), (g0, g1, g2), mix, dm):
            d_ref[...] = w * dyv
            g_ref[...] = w * (t - avg)

    os_ = pl.BlockSpec((None, rows, dh), lambda h, i: (h, i, 0))
    ls_ = pl.BlockSpec((None, rows, 1), lambda h, i: (h, i, 0))
    so = jax.ShapeDtypeStruct((H, T, dh), F32)
    sl = jax.ShapeDtypeStruct((H, T, 1), F32)
    res = pl.pallas_call(
        body, name=name, grid=(H, T // rows), in_specs=[os_] * 3 + [ls_] * 3 + [os_], out_specs=[os_] * 3 + [ls_] * 3,
        out_shape=[so] * 3 + [sl] * 3,
        compiler_params=_params("parallel", "parallel"),
    )(*outs, *lses, dy)
    return res[:3], res[3:]


def _position():
    return lax.axis_index("x"), lax.axis_index("y"), lax.axis_index("c")


def _other_chips(x, y):
    return [(1 - x, y), (x, 1 - y), (1 - x, 1 - y)]


HBM_SPEC = pl.BlockSpec(memory_space=pltpu.HBM)
SEM_SPEC = pl.BlockSpec(memory_space=pltpu.SEMAPHORE)
ANY_SPEC = pl.BlockSpec(memory_space=pl.ANY)
SPLIT_COPY = pltpu.CompilerParams(has_side_effects=pltpu.SideEffectType.DATAFLOW_SIDE_EFFECTING)


def _in_hbm(a):
    return pltpu.with_memory_space_constraint(a, pltpu.HBM)


SAME_CORE_OF_OTHER_CHIPS = ((1, 0, 0), (0, 1, 0), (1, 1, 0))
ALL_OTHER_DEVICES = ((0, 0, 1), (1, 0, 0), (0, 1, 0), (1, 1, 0), (1, 0, 1), (0, 1, 1), (1, 1, 1))


def _peer_copies(srcs, lands, send_sems, recv_sems, relations, src_of, dst_of):
    me = _position()
    copies = []
    for w in range(len(srcs)):
        for k, flips in enumerate(relations):
            peer = tuple(1 - p if f else p for p, f in zip(me, flips))
            i = len(relations) * w + k
            copies.append(pltpu.make_async_remote_copy(
                src_ref=src_of(srcs[w], peer), dst_ref=dst_of(lands[w], k, peer, me), send_sem=send_sems[i],
                recv_sem=recv_sems[i], device_id=peer, device_id_type=MESH))
    return copies


def _copies_start(srcs, land_shapes, after, relations, src_of, dst_of, *, name):
    n = len(srcs)
    m = len(relations) * n

    def body(*refs):
        src_refs, land_refs = refs[:n], refs[n:2 * n]
        send_sems, recv_sems = refs[2 * n + 1:2 * n + 1 + m], refs[2 * n + 1 + m:2 * n + 1 + 2 * m]
        token = refs[-1]
        for cp in _peer_copies(src_refs, land_refs, send_sems, recv_sems, relations, src_of, dst_of):
            cp.start()
        token[...] = jnp.zeros_like(token)

    lands = [_in_hbm(lax.empty(s.shape, s.dtype)) for s in land_shapes]
    out_shape = ([pltpu.SemaphoreType.DMA(())] * (2 * m)
                 + [pltpu.HBM(a.shape, a.dtype) for a in srcs] + [pltpu.HBM(s.shape, s.dtype) for s in land_shapes]
                 + [jax.ShapeDtypeStruct((8, 128), F32)])
    res = pl.pallas_call(
        body, name=name, in_specs=[HBM_SPEC] * (2 * n) + [ANY_SPEC],
        out_specs=[SEM_SPEC] * (2 * m) + [HBM_SPEC] * (2 * n) + [pl.BlockSpec(memory_space=pltpu.VMEM)],
        out_shape=out_shape, input_output_aliases={i: 2 * m + i for i in range(2 * n)}, compiler_params=SPLIT_COPY,
    )(*[_in_hbm(a) for a in srcs], *lands, after)
    return (list(res[:m]), list(res[m:2 * m]), list(res[2 * m:2 * m + n]), list(res[2 * m + n:2 * m + 2 * n]), res[-1])


def _copies_wait(started, after, relations, src_of, dst_of, *, name):
    send_sems, recv_sems, srcs, lands, _ = started
    n = len(srcs)
    m = len(relations) * n

    def body(*refs):
        src_refs, land_refs = refs[:n], refs[n:2 * n]
        send_refs, recv_refs = refs[2 * n:2 * n + m], refs[2 * n + m:2 * n + 2 * m]
        for cp in _peer_copies(src_refs, land_refs, send_refs, recv_refs, relations, src_of, dst_of):
            cp.wait_send()
            cp.wait_recv()

    res = pl.pallas_call(
        body, name=name, in_specs=[HBM_SPEC] * (2 * n) + [SEM_SPEC] * (2 * m) + [ANY_SPEC], out_specs=[HBM_SPEC] * (2 * n),
        out_shape=[pltpu.HBM(a.shape, a.dtype) for a in srcs + lands],
        input_output_aliases={i: i for i in range(2 * n)}, compiler_params=SPLIT_COPY,
    )(*srcs, *lands, *send_sems, *recv_sems, after)
    return list(res[:n]), list(res[n:])


def _half_rows(ref, core):
    half = ref.shape[-2] // 2
    return pl.ds(core * half, half)


def _gather_src(src, peer):
    return src.at[_half_rows(src, peer[2]), :]


def _gather_dst(land, k, peer, me):
    return land.at[2 * me[0] + me[1], _half_rows(land, me[2]), :]


def _gather_landed(land, k, peer, me):
    return land.at[2 * peer[0] + peer[1], _half_rows(land, me[2]), :]


def _exchange_src(src, peer):
    return src.at[2 * peer[0] + peer[1], _half_rows(src, peer[2]), :]


def _exchange_dst(land, k, peer, me):
    return land.at[k]


def _small_src(src, peer):
    return src


def _small_dst(land, k, peer, me):
    return land.at[4 * me[0] + 2 * me[1] + me[2]]


def _small_landed(land, k, peer, me):
    return land.at[4 * peer[0] + 2 * peer[1] + peer[2]]


def _forward_copies(bufs, send_sems, recv_sems, core):
    x, y, c = _position()
    copies = []
    for w in range(len(bufs)):
        for j, chip in enumerate(_other_chips(x, y)):
            rows = _gather_landed(bufs[w], j, chip, (x, y, core))
            copies.append(pltpu.make_async_remote_copy(src_ref=rows, dst_ref=rows, send_sem=send_sems[3 * w + j],
                                                       recv_sem=recv_sems[3 * w + j], device_id=(x, y, 1 - c), device_id_type=MESH))
    return copies


def _forward_start(bufs, *, name):
    n = len(bufs)
    m = 3 * n

    def body(*refs):
        send_sems, recv_sems = refs[n:n + m], refs[n + m:n + 2 * m]
        for cp in _forward_copies(refs[:n], send_sems, recv_sems, lax.axis_index("c")):
            cp.start()

    res = pl.pallas_call(
        body, name=name, in_specs=[HBM_SPEC] * n, out_specs=[SEM_SPEC] * (2 * m) + [HBM_SPEC] * n,
        out_shape=[pltpu.SemaphoreType.DMA(())] * (2 * m) + [pltpu.HBM(a.shape, a.dtype) for a in bufs],
        input_output_aliases={i: 2 * m + i for i in range(n)}, compiler_params=SPLIT_COPY,
    )(*bufs)
    return list(res[:m]), list(res[m:2 * m]), list(res[2 * m:])


def _forward_wait(started, after, *, name):
    send_sems, recv_sems, bufs = started
    n = len(bufs)
    m = 3 * n

    def body(*refs):
        send_refs, recv_refs = refs[n:n + m], refs[n + m:n + 2 * m]
        c = lax.axis_index("c")
        for sent, got in zip(_forward_copies(refs[:n], send_refs, recv_refs, c),
                             _forward_copies(refs[:n], send_refs, recv_refs, 1 - c)):
            sent.wait_send()
            got.wait_recv()

    return list(pl.pallas_call(
        body, name=name, in_specs=[HBM_SPEC] * n + [SEM_SPEC] * (2 * m) + [ANY_SPEC], out_specs=[HBM_SPEC] * n,
        out_shape=[pltpu.HBM(a.shape, a.dtype) for a in bufs], input_output_aliases={i: i for i in range(n)},
        compiler_params=SPLIT_COPY,
    )(*bufs, *send_sems, *recv_sems, after))


def _share_halves(bufs, *, name):
    n = len(bufs)

    def body(*refs):
        ins, outs = refs[:n], refs[n:2 * n]
        send_sems, recv_sems = refs[2 * n:]
        x, y, c = _position()
        sends = []
        for w in range(n):
            sends.append(pltpu.make_async_remote_copy(src_ref=ins[w].at[c], dst_ref=outs[w].at[c], send_sem=send_sems.at[w],
                                                      recv_sem=recv_sems.at[w], device_id=(x, y, 1 - c), device_id_type=MESH))
            sends[-1].start()
        for w in range(n):
            pltpu.make_async_remote_copy(src_ref=ins[w].at[1 - c], dst_ref=outs[w].at[1 - c], send_sem=send_sems.at[w],
                                         recv_sem=recv_sems.at[w], device_id=(x, y, 1 - c), device_id_type=MESH).wait_recv()
        for cp in sends:
            cp.wait_send()

    hbm = pl.BlockSpec(memory_space=pl.ANY)
    return pl.pallas_call(
        body, name=name, in_specs=[hbm] * n, out_specs=[hbm] * n,
        out_shape=[jax.ShapeDtypeStruct(a.shape, a.dtype) for a in bufs],
        input_output_aliases={w: w for w in range(n)},
        scratch_shapes=[pltpu.SemaphoreType.DMA((n,)), pltpu.SemaphoreType.DMA((n,))],
    )(*bufs)


def _all_gather8(v, *, name):
    m, n = v.shape

    def body(v_ref, out_ref, send_sems, recv_sems, local_sem):
        x, y, c = _position()
        me, sibling = (x, y, c), (x, y, 1 - c)
        chips = _other_chips(x, y)

        def rows(px, py, pc):
            return out_ref.at[pl.ds((4 * px + 2 * py + pc) * m, m), :]

        def copy(k, block, to, src=None):
            return pltpu.make_async_remote_copy(src_ref=rows(*block) if src is None else src, dst_ref=rows(*block),
                                                send_sem=send_sems.at[k], recv_sem=recv_sems.at[k], device_id=to, device_id_type=MESH)

        mine = pltpu.make_async_copy(v_ref, rows(*me), local_sem)
        mine.start()
        first = [copy(0, me, sibling, src=v_ref)]
        first += [copy(1 + j, me, (*chip, c), src=v_ref) for j, chip in enumerate(chips)]
        for cp in first:
            cp.start()
        passed = [copy(4 + j, (*chip, c), sibling) for j, chip in enumerate(chips)]
        for j, chip in enumerate(chips):
            copy(1 + j, (*chip, c), me).wait_recv()
            passed[j].start()
        copy(0, sibling, me).wait_recv()
        for j, chip in enumerate(chips):
            copy(4 + j, (*chip, 1 - c), me).wait_recv()
        for cp in first + passed:
            cp.wait_send()
        mine.wait()

    vmem = pl.BlockSpec(memory_space=pltpu.VMEM)
    return pl.pallas_call(
        body, name=name, in_specs=[vmem], out_specs=vmem, out_shape=jax.ShapeDtypeStruct((8 * m, n), v.dtype),
        scratch_shapes=[pltpu.SemaphoreType.DMA((7,)), pltpu.SemaphoreType.DMA((7,)), pltpu.SemaphoreType.DMA],
    )(v)


def _sum_parts(own, landed, where, *, name, rows=256):
    n_peers, half, C = landed.shape
    rows = _tile(half, rows)
    nb = half // rows

    def body(where_ref, own_ref, land_ref, o_ref):
        acc = own_ref[...].astype(F32)
        for k in range(n_peers):
            acc = acc + land_ref[k].astype(F32)
        o_ref[...] = acc

    grid_spec = pltpu.PrefetchScalarGridSpec(
        num_scalar_prefetch=1, grid=(nb,),
        in_specs=[pl.BlockSpec((None, rows, C), lambda i, where_ref: (where_ref[0], where_ref[1] * nb + i, 0)),
                  pl.BlockSpec((n_peers, rows, C), lambda i, where_ref: (0, i, 0))],
        out_specs=pl.BlockSpec((None, rows, C), lambda i, where_ref: (where_ref[1], i, 0)))
    return pl.pallas_call(
        body, name=name, grid_spec=grid_spec, out_shape=jax.ShapeDtypeStruct((2, half, C), F32),
        compiler_params=_params("parallel"),
    )(where, own, landed)


def _sum_slabs(a, *, name, rows=256):
    P, R, C = a.shape
    rows = _tile(R, rows)

    def body(a_ref, o_ref):
        acc = a_ref[0].astype(F32)
        for p in range(1, P):
            acc = acc + a_ref[p].astype(F32)
        o_ref[...] = acc

    return pl.pallas_call(
        body, name=name, grid=(R // rows,), in_specs=[pl.BlockSpec((P, rows, C), lambda i: (0, i, 0))],
        out_specs=pl.BlockSpec((rows, C), lambda i: (i, 0)),
        out_shape=jax.ShapeDtypeStruct((R, C), F32), compiler_params=_params("parallel"),
    )(a)


def _reduce_scatter_start(grads, after=None, *, name):
    lands = [jax.ShapeDtypeStruct((len(ALL_OTHER_DEVICES), g.shape[1] // 2, g.shape[2]), g.dtype) for g in grads]
    return _copies_start(grads, lands, grads[0] if after is None else after, ALL_OTHER_DEVICES, _exchange_src, _exchange_dst,
                         name=name + "_start")


def _reduce_scatter_finish(started, after, *, name):
    core = lax.axis_index("c").astype(jnp.int32)
    chip = (2 * lax.axis_index("x") + lax.axis_index("y")).astype(jnp.int32)
    where = jnp.stack([chip, core])
    sums, landed = _copies_wait(started, after, ALL_OTHER_DEVICES, _exchange_src, _exchange_dst, name=name + "_wait")
    reduced = [_sum_parts(s, a, where, name=f"{name}_sum{w}") for w, (s, a) in enumerate(zip(sums, landed))]
    both = _share_halves(reduced, name=name + "_share")
    return [b.reshape(2 * b.shape[1], b.shape[2]) for b in both]


def _to_heads(t, heads, dil=1):
    S = t.shape[0]
    d = t.shape[1] // heads
    return jnp.transpose(t.reshape(S // dil, dil, heads, d), (2, 1, 0, 3)).reshape(heads, S, d)


def _from_heads(t, dil=1):
    heads, S, d = t.shape
    return jnp.transpose(t.reshape(heads, dil, S // dil, d), (2, 1, 0, 3)).reshape(S, heads * d)


def _unstride(t, dil):
    heads, S, d = t.shape
    return jnp.transpose(t.reshape(heads, dil, S // dil, d), (0, 2, 1, 3)).reshape(heads, S, d)


def _restride(t, dil):
    heads, S, d = t.shape
    return jnp.transpose(t.reshape(heads, S // dil, dil, d), (0, 2, 1, 3)).reshape(heads, S, d)


def _pad_cols(t, width, padded):
    S = t.shape[0]
    return jnp.pad(t.reshape(S, N_CHIPS, width), ((0, 0), (0, 0), (0, padded - width))).reshape(S, N_CHIPS * padded)


def _column_reader(t, width, padded):
    def take(lo, hi):
        pieces = []
        for p in range(N_CHIPS):
            a, b = max(lo, p * width), min(hi, (p + 1) * width)
            if a < b:
                pieces.append(t[:, p * padded + a - p * width:p * padded + b - p * width])
        return pieces[0] if len(pieces) == 1 else jnp.concatenate(pieces, axis=-1)
    return take


def _alibi(n):
    return 2.0 ** (-8.0 * jnp.arange(1, n + 1, dtype=F32) / n)


B_KW = [dict(hps=4, nb_seq=SEQ // d // ATT_BLK, n_back=w // d, scale=1.0 / math.sqrt(HEAD_DIM), dist_scale=d)
        for w, d in B_PATTERNS]
A_KW = dict(hps=A_Q_HEADS // A_KV_HEADS, nb_seq=SEQ // ATT_BLK, n_back=A_WINDOW - 1, scale=1.0 / math.sqrt(HEAD_DIM), dist_scale=1)
D_KW = dict(blk=1024, hps=1, scale=1.0 / math.sqrt(D_NOPE + D_ROPE))
C_KW = dict(blk=512, hps=1, scale=1.0 / math.sqrt(HEAD_DIM))


def _rope_tables(reps):
    inv_freq = ROPE_BASE ** (-jnp.arange(0, D_ROPE, 2, dtype=F32) / D_ROPE)
    ang = jnp.arange(SEQ, dtype=F32)[:, None] * inv_freq[None, :]
    return jnp.tile(jnp.cos(ang), (1, reps)), jnp.tile(jnp.sin(ang), (1, reps))


def _mlp_fwd(x, x16, w1, w2, g, b, tag):
    hid, act = _matmul(x16, w1, mode="nn", out_dtype=BF16, epilogue="relu2", name=f"mlp{tag}_up")
    m = _matmul(act, w2, mode="nn", out_dtype=F32, name=f"mlp{tag}_down")
    y, y16, u = _norm_fwd(x, m, g, b, center=True, eps=LN_EPS, alpha=ALPHA, name=f"ln2_{tag}")
    return y, y16, (x16, hid, act, u)


def _mlp_bwd(dy, saved, w1, w2, g, tag, after=None):
    x16, hid, act, u = saved
    du, du16, dg, db = _norm_bwd(dy, u, g, center=True, eps=LN_EPS, name=f"ln2_{tag}_bwd", after=after)
    dw2 = _matmul_tn(act, du16, shards=1, name=f"mlp{tag}_dw2")
    dhid = _matmul(du16, w2, mode="nt", out_dtype=BF16, epilogue="drelu2", extra=hid, name=f"mlp{tag}_dact")
    dw1 = _matmul_tn(x16, dhid, shards=N_CHIPS, name=f"mlp{tag}_dw1")
    dx = _matmul(dhid, w1, mode="nt", out_dtype=F32, epilogue="add", extra=du, alpha=ALPHA, name=f"mlp{tag}_dx")
    return dx, dw1, dw2, dg, db


def _even_fwd(x16, w_in, sinks):
    h = _column_reader(_matmul(x16, w_in, mode="nn", out_dtype=BF16, name="even_in"), EVEN_IN // N_CHIPS, EVEN_IN_PAD)
    qa = _to_heads(h(0, A_Q_W), A_Q_HEADS)
    ka = _to_heads(h(A_Q_W, A_Q_W + A_KV_W), A_KV_HEADS)
    va = _to_heads(h(A_Q_W + A_KV_W, A_Q_W + 2 * A_KV_W), A_KV_HEADS)
    slopes_a, slopes_b = _alibi(A_Q_HEADS), _alibi(B_HEADS)
    oa, lse_a = _band_fwd(qa, ka, va, slopes_a, sinks, name="swa_fwd", **A_KW)
    qkv_b, outs, lses = [], [], []
    for gi, (_, dil) in enumerate(B_PATTERNS):
        base = A_Q_W + 2 * A_KV_W + gi * 3 * B_W
        q, k, v = (_to_heads(h(base + i * B_W, base + (i + 1) * B_W), B_HEADS, dil) for i in range(3))
        o, lse = _band_fwd(q, k, v, slopes_b, None, name=f"dil{gi}_fwd", **B_KW[gi])
        qkv_b.append((q, k, v, o, lse))
        outs.append(_unstride(o, dil))
        lses.append(_unstride(lse, dil))
    ob = _mix_fwd(outs, lses, name="dil_mix")
    y16 = jnp.concatenate([_from_heads(oa), _from_heads(ob)], axis=-1).astype(BF16)
    return y16, (x16, qa, ka, va, oa, lse_a, qkv_b, outs, lses, y16)


def _even_bwd(dmix16, du, saved, w_in, sinks, w_out, send_w_out, send_w_in):
    x16, qa, ka, va, oa, lse_a, qkv_b, outs, lses, y16 = saved
    slopes_a, slopes_b = _alibi(A_Q_HEADS), _alibi(B_HEADS)
    sent = send_w_out(_matmul_tn(y16, dmix16, shards=N_CHIPS, name="even_dwout"))
    dy = _matmul(dmix16, w_out, mode="nt", out_dtype=F32, name="even_dy", after=sent)
    doa = _to_heads(dy[:, :A_Q_W], A_Q_HEADS)
    dob = _to_heads(dy[:, A_Q_W:], B_HEADS)
    dqa, dka, dva, dsink = _band_bwd(qa, ka, va, oa, doa, lse_a, None, slopes_a, sinks, name="swa_bwd", **A_KW)
    douts, dlses = _mix_bwd(outs, lses, dob, name="dil_mix_bwd")
    parts = [_from_heads(dqa), _from_heads(dka), _from_heads(dva)]
    for gi, (_, dil) in enumerate(B_PATTERNS):
        q, k, v, o, lse = qkv_b[gi]
        dq, dk, dv = _band_bwd(q, k, v, o, _restride(douts[gi], dil), lse, _restride(dlses[gi], dil), slopes_b, None,
                               name=f"dil{gi}_bwd", **B_KW[gi])
        parts += [_from_heads(dq, dil), _from_heads(dk, dil), _from_heads(dv, dil)]
    dh16 = _pad_cols(jnp.concatenate(parts, axis=-1), EVEN_IN // N_CHIPS, EVEN_IN_PAD).astype(BF16)
    sent = send_w_in(_matmul_tn(x16, dh16, shards=N_CHIPS, name="even_dwin"), dsink[:, 0, 0])
    return _matmul(dh16, w_in, mode="nt", out_dtype=F32, epilogue="add", extra=du, alpha=ALPHA, name="even_dx", after=sent)


def _odd_fwd(x16, w_in, qn_g, kvn_g, w_uq, w_ukv, w_out):
    S = x16.shape[0]
    h = _column_reader(_matmul(x16, w_in, mode="nn", out_dtype=F32, name="odd_in"), ODD_IN // N_CHIPS, ODD_IN_PAD)
    qc, kc, vc = (_to_heads(h(i * C_W, (i + 1) * C_W).astype(BF16), C_HEADS) for i in range(3))
    cq = h(3 * C_W, 3 * C_W + D_Q_RANK)
    ckv = h(3 * C_W + D_Q_RANK, 3 * C_W + D_Q_RANK + D_KV_RANK)
    kr = h(3 * C_W + D_Q_RANK + D_KV_RANK, ODD_IN)
    oc, tot = _stick_fwd(qc, kc, vc, name="stick_fwd", **C_KW)
    _, cqn16, _ = _norm_fwd(cq, None, qn_g, None, center=False, eps=RMS_EPS, alpha=1.0, name="q_rms")
    _, ckvn16, _ = _norm_fwd(ckv, None, kvn_g, None, center=False, eps=RMS_EPS, alpha=1.0, name="kv_rms")
    q = _matmul(cqn16, w_uq, mode="nn", out_dtype=F32, name="mla_uq").reshape(S, D_HEADS, D_NOPE + D_ROPE)
    kv = _matmul(ckvn16, w_ukv, mode="nn", out_dtype=F32, name="mla_ukv").reshape(S, D_HEADS, D_NOPE + D_V)
    half = D_ROPE // 2
    q_rope = q[..., D_NOPE:]
    x1 = jnp.concatenate([q_rope[..., :half].reshape(S, D_HEADS * half), kr[:, :half]], axis=-1)
    x2 = jnp.concatenate([q_rope[..., half:].reshape(S, D_HEADS * half), kr[:, half:]], axis=-1)
    cos, sin = _rope_tables(D_HEADS + 1)
    y1, y2 = _rope(x1[None], x2[None], cos, sin, name="rope_fwd")
    nq = D_HEADS * half
    q_rot = jnp.concatenate([y1[:, :nq].reshape(S, D_HEADS, half), y2[:, :nq].reshape(S, D_HEADS, half)], axis=-1)
    kr_rot = jnp.concatenate([y1[:, nq:], y2[:, nq:]], axis=-1)
    qd = jnp.transpose(jnp.concatenate([q[..., :D_NOPE], q_rot], axis=-1), (1, 0, 2)).astype(BF16)
    kd = jnp.transpose(jnp.concatenate([kv[..., :D_NOPE], jnp.broadcast_to(kr_rot[:, None, :], (S, D_HEADS, D_ROPE))], axis=-1),
                       (1, 0, 2)).astype(BF16)
    vd = jnp.transpose(kv[..., D_NOPE:], (1, 0, 2)).astype(BF16)
    od, lse_d = _causal_fwd(qd, kd, vd, name="mla_fwd", **D_KW)
    y16 = jnp.concatenate([_from_heads(oc), _from_heads(od)], axis=-1).astype(BF16)
    mixed = _matmul(y16, w_out, mode="nn", out_dtype=F32, name="odd_out")
    return mixed, (x16, qc, kc, vc, tot, cq, ckv, cqn16, ckvn16, qd, kd, vd, od, lse_d, y16)


def _odd_bwd(dmix16, du, saved, w_in, qn_g, kvn_g, w_uq, w_ukv, w_out):
    x16, qc, kc, vc, tot, cq, ckv, cqn16, ckvn16, qd, kd, vd, od, lse_d, y16 = saved
    S = x16.shape[0]
    half = D_ROPE // 2
    dw_out = _matmul_tn(y16, dmix16, shards=1, name="odd_dwout")
    dy = _matmul(dmix16, w_out, mode="nt", out_dtype=F32, name="odd_dy")
    doc = _to_heads(dy[:, :C_W], C_HEADS)
    dod = _to_heads(dy[:, C_W:], D_HEADS)
    dqc, dkc, dvc = _stick_bwd(qc, kc, vc, tot, doc, name="stick_bwd", **C_KW)
    dqd, dkd, dvd = _causal_bwd(qd, kd, vd, od, dod, lse_d, name="mla_bwd", **D_KW)
    cos, sin = _rope_tables(D_HEADS + 1)
    nq = D_HEADS * half
    gq1 = jnp.transpose(dqd[..., D_NOPE:D_NOPE + half], (1, 0, 2)).reshape(1, S, nq)
    gq2 = jnp.transpose(dqd[..., D_NOPE + half:], (1, 0, 2)).reshape(1, S, nq)
    dq1, dq2 = _rope(gq1, gq2, cos[:, :nq], -sin[:, :nq], name="rope_bwd_q")
    dk1, dk2 = _rope(dkd[..., D_NOPE:D_NOPE + half], dkd[..., D_NOPE + half:], cos[:, nq:], -sin[:, nq:], name="rope_bwd_k")
    dq_full = jnp.concatenate([jnp.transpose(dqd[..., :D_NOPE], (1, 0, 2)), dq1.reshape(S, D_HEADS, half),
                               dq2.reshape(S, D_HEADS, half)], axis=-1).reshape(S, D_HEADS * (D_NOPE + D_ROPE)).astype(BF16)
    dkv_full = jnp.concatenate([jnp.transpose(dkd[..., :D_NOPE], (1, 0, 2)), jnp.transpose(dvd, (1, 0, 2))],
                               axis=-1).reshape(S, D_HEADS * (D_NOPE + D_V)).astype(BF16)
    dw_uq = _matmul_tn(cqn16, dq_full, shards=N_CHIPS, name="mla_dwuq")
    dw_ukv = _matmul_tn(ckvn16, dkv_full, shards=N_CHIPS, name="mla_dwukv")
    dcqn = _matmul(dq_full, w_uq, mode="nt", out_dtype=F32, name="mla_dcq")
    dckvn = _matmul(dkv_full, w_ukv, mode="nt", out_dtype=F32, name="mla_dckv")
    dcq, _, dqn_g, _ = _norm_bwd(dcqn, cq, qn_g, center=False, eps=RMS_EPS, name="q_rms_bwd")
    dckv, _, dkvn_g, _ = _norm_bwd(dckvn, ckv, kvn_g, center=False, eps=RMS_EPS, name="kv_rms_bwd")
    dh = jnp.concatenate([_from_heads(dqc), _from_heads(dkc), _from_heads(dvc), dcq, dckv, dk1, dk2], axis=-1)
    dh16 = _pad_cols(dh, ODD_IN // N_CHIPS, ODD_IN_PAD).astype(BF16)
    dw_in = _matmul_tn(x16, dh16, shards=N_CHIPS, name="odd_dwin")
    dx = _matmul(dh16, w_in, mode="nt", out_dtype=F32, epilogue="add", extra=du, alpha=ALPHA, name="odd_dx")
    return dx, dw_in, dqn_g[0], dkvn_g[0], dw_uq, dw_ukv, dw_out


WEIGHT_GROUPS = (("even_w_in",), ("even_w_out", "mlp_w1_0", "mlp_w2_0"), ("odd_w_in", "odd_w_uq", "odd_w_ukv", "odd_w_out"),
                 ("mlp_w1_1", "mlp_w2_1"))
GRAD_GROUPS = (("mlp_w2_1", "mlp_w1_1"), ("odd_w_out", "odd_w_uq", "odd_w_ukv", "odd_w_in"), ("mlp_w2_0", "mlp_w1_0"),
               ("even_w_out",), ("even_w_in",))


def _local_step(x, target, small, weights_for, send_grads, send_small, total_loss):
    def by_rows(t, rows):
        return t.reshape(N_CHIPS, rows // N_CHIPS, D_MODEL)

    x16 = x.astype(BF16)
    w = dict(weights_for(0, x16))
    y16, sv_e = _even_fwd(x16, w["even_w_in"], small["even_sinks"])
    w.update(weights_for(1, y16))
    mixed0 = _matmul(y16, w["even_w_out"], mode="nn", out_dtype=F32, name="even_out")
    x1, x1_16, u01 = _norm_fwd(x, mixed0, small["ln1_g"][0], small["ln1_b"][0], center=True, eps=LN_EPS, alpha=ALPHA, name="ln1_0")
    w1_0, w2_0 = w["mlp_w1_0"], w["mlp_w2_0"].reshape(1, D_FF, D_MODEL)
    x2, x2_16, sv_m0 = _mlp_fwd(x1, x1_16, w1_0, w2_0, small["ln2_g"][0], small["ln2_b"][0], 0)
    w.update(weights_for(2, x2_16))
    odd_w = (w["odd_w_in"], small["odd_q_norm_g"], small["odd_kv_norm_g"], w["odd_w_uq"], w["odd_w_ukv"],
             w["odd_w_out"].reshape(1, D_MODEL, D_MODEL))
    mixed1, sv_o = _odd_fwd(x2_16, *odd_w)
    x3, x3_16, u11 = _norm_fwd(x2, mixed1, small["ln1_g"][1], small["ln1_b"][1], center=True, eps=LN_EPS, alpha=ALPHA, name="ln1_1")
    w.update(weights_for(3, x3_16))
    w1_1, w2_1 = w["mlp_w1_1"], w["mlp_w2_1"].reshape(1, D_FF, D_MODEL)
    x4, _, sv_m1 = _mlp_fwd(x3, x3_16, w1_1, w2_1, small["ln2_g"][1], small["ln2_b"][1], 1)
    dy, loss_row = _loss_head(x4, target, name="loss_head")
    loss = total_loss(loss_row)

    dx3, dw1_1, dw2_1, dln2g_1, dln2b_1 = _mlp_bwd(dy, sv_m1, w1_1, w2_1, small["ln2_g"][1], 1, after=loss.reshape(1, 1))
    sent = send_grads(0, dict(mlp_w2_1=by_rows(dw2_1, D_FF), mlp_w1_1=dw1_1))
    du, du16, dln1g_1, dln1b_1 = _norm_bwd(dx3, u11, small["ln1_g"][1], center=True, eps=LN_EPS, name="ln1_1_bwd", after=sent)
    dx2, dwin_o, dqn_g, dkvn_g, dwuq, dwukv, dwout_o = _odd_bwd(du16, du, sv_o, *odd_w)
    sent = send_grads(1, dict(odd_w_out=by_rows(dwout_o, D_MODEL), odd_w_uq=dwuq, odd_w_ukv=dwukv, odd_w_in=dwin_o))
    dx1, dw1_0, dw2_0, dln2g_0, dln2b_0 = _mlp_bwd(dx2, sv_m0, w1_0, w2_0, small["ln2_g"][0], 0, after=sent)
    sent = send_grads(2, dict(mlp_w2_0=by_rows(dw2_0, D_FF), mlp_w1_0=dw1_0))
    du, du16, dln1g_0, dln1b_0 = _norm_bwd(dx1, u01, small["ln1_g"][0], center=True, eps=LN_EPS, name="ln1_0_bwd", after=sent)
    def send_last(dwin_e, dsinks):
        went = send_small(dict(even_sinks=dsinks, odd_q_norm_g=dqn_g, odd_kv_norm_g=dkvn_g,
                               ln1_g=jnp.concatenate([dln1g_0, dln1g_1]), ln1_b=jnp.concatenate([dln1b_0, dln1b_1]),
                               ln2_g=jnp.concatenate([dln2g_0, dln2g_1]), ln2_b=jnp.concatenate([dln2b_0, dln2b_1])))
        return send_grads(4, dict(even_w_in=dwin_e), went)

    dx0 = _even_bwd(du16, du, sv_e, w["even_w_in"], small["even_sinks"], w["even_w_out"],
                    lambda dwout_e: send_grads(3, dict(even_w_out=dwout_e)), send_last)
    return loss, dx0


SMALL_ORDER = ("ln1_g", "ln1_b", "ln2_g", "ln2_b", "even_sinks", "odd_q_norm_g", "odd_kv_norm_g")
SMALL_COLS = 2176


def _pack_small(parts):
    flat = jnp.concatenate([p.reshape(-1) for p in parts])
    return jnp.pad(flat, (0, 8 * SMALL_COLS - flat.shape[0])).reshape(8, SMALL_COLS)


def _unpack_small(packed, shapes):
    flat = packed.reshape(-1)
    out, pos = [], 0
    for s in shapes:
        n = math.prod(s)
        out.append(flat[pos:pos + n].reshape(s))
        pos += n
    return out


def kernel(x, even_w_in, even_sinks, even_w_out, odd_w_in, odd_q_norm_g, odd_kv_norm_g, odd_w_uq, odd_w_ukv, odd_w_out, ln1_g, ln1_b, mlp_w1, mlp_w2, ln2_g, ln2_b, loss_target, m_even_w_in, m_even_sinks, m_even_w_out, m_odd_w_in, m_odd_q_norm_g, m_odd_kv_norm_g, m_odd_w_uq, m_odd_w_ukv, m_odd_w_out, m_ln1_g, m_ln1_b, m_mlp_w1, m_mlp_w2, m_ln2_g, m_ln2_b, v_even_w_in, v_even_sinks, v_even_w_out, v_odd_w_in, v_odd_q_norm_g, v_odd_kv_norm_g, v_odd_w_uq, v_odd_w_ukv, v_odd_w_out, v_ln1_g, v_ln1_b, v_mlp_w1, v_mlp_w2, v_ln2_g, v_ln2_b):
    chip = 2 * lax.axis_index("x") + lax.axis_index("y")
    e_w, o_w = EVEN_IN // N_CHIPS, ODD_IN // N_CHIPS

    shards = dict(
        even_w_in=jnp.pad(even_w_in[0], ((0, 0), (0, EVEN_IN_PAD - e_w))), even_w_out=even_w_out[0],
        odd_w_in=jnp.pad(odd_w_in[0], ((0, 0), (0, ODD_IN_PAD - o_w))), odd_w_uq=odd_w_uq[0], odd_w_ukv=odd_w_ukv[0],
        odd_w_out=odd_w_out[0], mlp_w1_0=mlp_w1[0], mlp_w1_1=mlp_w1[1], mlp_w2_0=mlp_w2[0], mlp_w2_1=mlp_w2[1])
    names = list(shards)
    own16 = {n: shards[n].astype(BF16) for n in names}
    gather_started, token = [], own16[WEIGHT_GROUPS[0][0]]
    for g, group in enumerate(WEIGHT_GROUPS):
        lands = [jax.ShapeDtypeStruct((N_CHIPS,) + own16[n].shape, BF16) for n in group]
        gather_started.append(_copies_start([own16[n] for n in group], lands, token, SAME_CORE_OF_OTHER_CHIPS, _gather_src,
                                            _gather_dst, name=f"gather{g}_start"))
        token = gather_started[-1][-1]
    all_started = token

    forwarding = {}

    def forward(g, after):
        _, landed = _copies_wait(gather_started[g], after, SAME_CORE_OF_OTHER_CHIPS, _gather_src, _gather_landed,
                                 name=f"gather{g}_wait")
        forwarding[g] = _forward_start(landed, name=f"gather{g}_forward")

    def weights_for(g, after):
        if g not in forwarding:
            forward(g, all_started if g == 0 else after)
        full = _forward_wait(forwarding[g], after, name=f"gather{g}_forwarded")
        if 1 <= g < len(WEIGHT_GROUPS) - 1:
            forward(g + 1, after)
        return {n: lax.dynamic_update_slice(f, own16[n][None], (chip, 0, 0)) for n, f in zip(WEIGHT_GROUPS[g], full)}

    grads_started = {}

    def send_grads(g, shares, after=None):
        grads_started[g] = _reduce_scatter_start([shares[n] for n in GRAD_GROUPS[g]], after, name=f"grads{g}")
        return grads_started[g][-1]

    norm_rows = jnp.pad(jnp.concatenate([odd_q_norm_g[0], odd_kv_norm_g[0]])[None], ((0, 7), (0, 64)))
    norm_all = _all_gather8(norm_rows, name="gather_norm_gains")[0::16]
    qn_w, kvn_w = D_Q_RANK // N_CHIPS, D_KV_RANK // N_CHIPS
    small = dict(even_sinks=even_sinks[0], odd_q_norm_g=norm_all[:, :qn_w].reshape(D_Q_RANK),
                 odd_kv_norm_g=norm_all[:, qn_w:qn_w + kvn_w].reshape(D_KV_RANK), ln1_g=ln1_g, ln1_b=ln1_b, ln2_g=ln2_g, ln2_b=ln2_b)

    small_started = []

    def send_small(sm):
        pack = _pack_small([sm[n] for n in SMALL_ORDER])
        small_started.append(_copies_start([pack], [jax.ShapeDtypeStruct((8,) + pack.shape, F32)], pack, ALL_OTHER_DEVICES,
                                           _small_src, _small_dst, name="small_grads_start"))
        return small_started[0][-1]

    def small_sum(after):
        (pack,), (landed,) = _copies_wait(small_started[0], after, ALL_OTHER_DEVICES, _small_src, _small_landed,
                                          name="small_grads_wait")
        device = 4 * lax.axis_index("x") + 2 * lax.axis_index("y") + lax.axis_index("c")
        return _sum_slabs(lax.dynamic_update_slice(landed, pack[None], (device, 0, 0)), name="sum_small_grads")

    loss, grad_x = _local_step(x[0], loss_target[0], small, weights_for, send_grads, send_small,
                               lambda row: lax.psum(row[0, 0], ("x", "y", "c")))

    weights = dict(even_w_in=even_w_in, even_sinks=even_sinks, even_w_out=even_w_out, odd_w_in=odd_w_in, odd_q_norm_g=odd_q_norm_g,
                   odd_kv_norm_g=odd_kv_norm_g, odd_w_uq=odd_w_uq, odd_w_ukv=odd_w_ukv, odd_w_out=odd_w_out, ln1_g=ln1_g, ln1_b=ln1_b,
                   mlp_w1=mlp_w1, mlp_w2=mlp_w2, ln2_g=ln2_g, ln2_b=ln2_b)
    m_in = dict(even_w_in=m_even_w_in, even_sinks=m_even_sinks, even_w_out=m_even_w_out, odd_w_in=m_odd_w_in,
                odd_q_norm_g=m_odd_q_norm_g, odd_kv_norm_g=m_odd_kv_norm_g, odd_w_uq=m_odd_w_uq, odd_w_ukv=m_odd_w_ukv,
                odd_w_out=m_odd_w_out, ln1_g=m_ln1_g, ln1_b=m_ln1_b, mlp_w1=m_mlp_w1, mlp_w2=m_mlp_w2, ln2_g=m_ln2_g, ln2_b=m_ln2_b)
    v_in = dict(even_w_in=v_even_w_in, even_sinks=v_even_sinks, even_w_out=v_even_w_out, odd_w_in=v_odd_w_in,
                odd_q_norm_g=v_odd_q_norm_g, odd_kv_norm_g=v_odd_kv_norm_g, odd_w_uq=v_odd_w_uq, odd_w_ukv=v_odd_w_ukv,
                odd_w_out=v_odd_w_out, ln1_g=v_ln1_g, ln1_b=v_ln1_b, mlp_w1=v_mlp_w1, mlp_w2=v_mlp_w2, ln2_g=v_ln2_g, ln2_b=v_ln2_b)
    order = list(weights)
    updated = {}

    def update(n, g2d, layer=0, tag="", by_column=False):
        shape = weights[n].shape
        as3d = (1, math.prod(shape[:-1]), shape[-1]) if len(shape) == 2 else shape
        view = (lambda t: jnp.swapaxes(t.reshape(as3d), 1, 2)) if by_column else (lambda t: t.reshape(as3d))
        res = _adamw(view(weights[n]), g2d.T if by_column else g2d, view(m_in[n]), view(v_in[n]), layer=layer,
                     carry=updated.get(n), name=f"adamw_{n}{tag}")
        updated[n] = tuple(jnp.swapaxes(t, 1, 2) for t in res) if by_column else tuple(res)
        return res[0]

    after = grad_x
    for g, group in enumerate(GRAD_GROUPS):
        for n, r in zip(group, _reduce_scatter_finish(grads_started[g], after, name=f"grads{g}")):
            if n[:-2] in ("mlp_w1", "mlp_w2"):
                after = update(n[:-2], r, layer=int(n[-1]), tag=n[-2:])
            elif n == "even_w_in":
                after = update(n, r[:, :e_w], by_column=True)
            elif n == "odd_w_in":
                after = update(n, r[:, :o_w], by_column=True)
            else:
                after = update(n, r)
        if g == len(GRAD_GROUPS) - 2:
            g_ln1g, g_ln1b, g_ln2g, g_ln2b, g_sinks, g_qn, g_kvn = _unpack_small(
                small_sum(after), [(DEPTH, D_MODEL)] * 4 + [(1, A_Q_HEADS), (D_Q_RANK,), (D_KV_RANK,)])
            for n, gs in (("even_sinks", g_sinks), ("odd_q_norm_g", lax.dynamic_slice_in_dim(g_qn, chip * qn_w, qn_w)[None]),
                          ("odd_kv_norm_g", lax.dynamic_slice_in_dim(g_kvn, chip * kvn_w, kvn_w)[None]), ("ln1_g", g_ln1g),
                          ("ln1_b", g_ln1b), ("ln2_g", g_ln2g), ("ln2_b", g_ln2b)):
                update(n, gs)
    outs = [[updated[n][k].reshape(weights[n].shape) for n in order] for k in range(4)]
    return (loss, grad_x[None], *outs[0], *outs[1], *outs[2], *outs[3])
```

```python
import math

import jax
import jax.numpy as jnp
from jax import lax
from jax.experimental import pallas as pl
from jax.experimental.pallas import tpu as pltpu

F32 = jnp.float32
BF16 = jnp.bfloat16
MESH = pl.DeviceIdType.MESH

D_MODEL = 2048
SEQ = 2048
DEPTH = 2
HEAD_DIM = 64
A_Q_HEADS, A_KV_HEADS, A_WINDOW = 16, 2, 128
B_HEADS = 8
B_PATTERNS = ((128, 1), (512, 4), (2048, 16))
C_HEADS = 16
D_HEADS, D_Q_RANK, D_KV_RANK, D_NOPE, D_ROPE, D_V = 16, 512, 256, 64, 32, 64
ROPE_BASE = 10000.0
D_FF = 4 * D_MODEL
LN_EPS = 1e-5
RMS_EPS = 1e-6
ALPHA = (2 * DEPTH) ** 0.25
A_Q_W = A_Q_HEADS * HEAD_DIM
A_KV_W = A_KV_HEADS * HEAD_DIM
B_W = B_HEADS * HEAD_DIM
EVEN_IN = A_Q_W + 2 * A_KV_W + 3 * B_W * len(B_PATTERNS)
EVEN_OUT = A_Q_W + B_W
C_W = C_HEADS * HEAD_DIM
ODD_IN = 3 * C_W + D_Q_RANK + D_KV_RANK + D_ROPE
ADAM_LR, ADAM_B1, ADAM_B2, ADAM_EPS, ADAM_WD, ADAM_STEP = 0.001, 0.9, 0.999, 1e-08, 0.01, 10

N_CHIPS = 4
EVEN_IN_PAD = 1536
ODD_IN_PAD = 1024
ATT_BLK = 128
NEG = -1e30
V7X_VMEM_LIMIT = 48 * 1024 * 1024


def _params(*sem):
    return pltpu.CompilerParams(dimension_semantics=sem, vmem_limit_bytes=V7X_VMEM_LIMIT)


def _tile(n, cap):
    if n <= cap:
        return n
    t = cap - cap % 128
    while n % t:
        t -= 128
    return t


def _matmul(a, b, *, mode, out_dtype, name, epilogue=None, extra=None, alpha=1.0, after=None, tm=1024, tn=1024, tk=2048):
    if mode == "nn":
        M, K = a.shape
        P, _, Np = b.shape
        tm, tn, tk = _tile(M, tm), _tile(Np, tn), _tile(K, tk)
        nj = Np // tn
        grid = (M // tm, P * nj, K // tk)
        a_spec = pl.BlockSpec((tm, tk), lambda i, j, k: (i, k))
        b_spec = pl.BlockSpec((None, tk, tn), lambda i, j, k: (j // nj, k, j % nj))
        o_spec = pl.BlockSpec((tm, tn), lambda i, j, k: (i, j))
        out_shape = (M, P * Np)
        dims = (((1,), (0,)), ((), ()))
    elif mode == "nt":
        M, N = a.shape
        P, K, Np = b.shape
        tm, tn, tk = _tile(M, tm), _tile(K, tn), _tile(Np, tk)
        nkk = Np // tk
        grid = (M // tm, K // tn, P * nkk)
        a_spec = pl.BlockSpec((tm, tk), lambda i, j, k: (i, k))
        b_spec = pl.BlockSpec((None, tn, tk), lambda i, j, k: (k // nkk, j, k % nkk))
        o_spec = pl.BlockSpec((tm, tn), lambda i, j, k: (i, j))
        out_shape = (M, K)
        dims = (((1,), (1,)), ((), ()))
    else:
        raise ValueError(mode)
    n_k = grid[2]
    n_extra = 0 if extra is None else 1
    n_after = 0 if after is None else 1
    n_out = 2 if epilogue == "relu2" else 1

    def body(*refs):
        a_ref, b_ref = refs[:2]
        e_ref = refs[2] if n_extra else None
        outs = refs[2 + n_extra + n_after:2 + n_extra + n_after + n_out]
        part = lax.dot_general(a_ref[...], b_ref[...], dims, preferred_element_type=F32)

        def finish(r):
            if epilogue == "relu2":
                outs[0][...] = r.astype(outs[0].dtype)
                rp = jnp.maximum(r, 0.0)
                outs[1][...] = (rp * rp).astype(outs[1].dtype)
            elif epilogue == "drelu2":
                outs[0][...] = (r * (2.0 * jnp.maximum(e_ref[...].astype(F32), 0.0))).astype(outs[0].dtype)
            elif epilogue == "add":
                outs[0][...] = (r + alpha * e_ref[...].astype(F32)).astype(outs[0].dtype)
            else:
                outs[0][...] = r.astype(outs[0].dtype)

        if n_k == 1:
            finish(part)
        else:
            acc = refs[-1]
            k = pl.program_id(2)

            @pl.when(k == 0)
            def _():
                acc[...] = part

            @pl.when((k > 0) & (k < n_k - 1))
            def _():
                acc[...] += part

            @pl.when(k == n_k - 1)
            def _():
                finish(acc[...] + part)

    in_specs = [a_spec, b_spec] + ([o_spec] if n_extra else []) + ([pl.BlockSpec(memory_space=pl.ANY)] if n_after else [])
    shapes = [jax.ShapeDtypeStruct(out_shape, out_dtype)] * n_out
    res = pl.pallas_call(
        body, name=name, grid=grid, in_specs=in_specs,
        out_specs=[o_spec] * n_out, out_shape=shapes,
        scratch_shapes=[pltpu.VMEM((tm, tn), F32)] if n_k > 1 else [],
        compiler_params=_params("parallel", "parallel", "arbitrary"),
    )(a, b, *([extra] if n_extra else []), *([after] if n_after else []))
    return res if n_out > 1 else res[0]


def _matmul_tn(a, g, *, shards, name, tm=1024, tn=1024):
    M, K = a.shape
    P = shards
    Np = g.shape[1] // P
    tm, tn = _tile(K, tm), _tile(Np, tn)
    nj = Np // tn

    def body(a_ref, g_ref, o_ref):
        o_ref[...] = lax.dot_general(a_ref[...], g_ref[...], (((0,), (0,)), ((), ())), preferred_element_type=F32).astype(BF16)

    return pl.pallas_call(
        body, name=name, grid=(K // tm, P * nj),
        in_specs=[pl.BlockSpec((M, tm), lambda i, j: (0, i)), pl.BlockSpec((M, tn), lambda i, j: (0, j))],
        out_specs=pl.BlockSpec((None, tm, tn), lambda i, j: (j // nj, i, j % nj)),
        out_shape=jax.ShapeDtypeStruct((P, K, Np), BF16),
        compiler_params=_params("parallel", "parallel"),
    )(a, g)


def _norm_fwd(x, res, g, b, *, center, eps, alpha, name, rows=256):
    S, W = x.shape
    has_res = res is not None
    rows = _tile(S, rows)

    def body(*refs):
        x_ref = refs[0]
        r_ref = refs[1] if has_res else None
        g_ref = refs[1 + has_res]
        b_ref = refs[2 + has_res] if center else None
        y_ref, y16_ref, u_ref = refs[-3:]
        u = x_ref[...]
        if has_res:
            u = alpha * u + r_ref[...]
        if center:
            mu = jnp.mean(u, axis=1, keepdims=True)
            xc = u - mu
        else:
            xc = u
        var = jnp.mean(xc * xc, axis=1, keepdims=True)
        y = xc * lax.rsqrt(var + eps) * g_ref[...]
        if center:
            y = y + b_ref[...]
        y_ref[...] = y
        y16_ref[...] = y.astype(BF16)
        u_ref[...] = u

    row_spec = pl.BlockSpec((rows, W), lambda i: (i, 0))
    vec_spec = pl.BlockSpec((1, W), lambda i: (0, 0))
    ins = [x] + ([res] if has_res else []) + [g.reshape(1, W)] + ([b.reshape(1, W)] if center else [])
    specs = [row_spec] + ([row_spec] if has_res else []) + [vec_spec] + ([vec_spec] if center else [])
    return pl.pallas_call(
        body, name=name, grid=(S // rows,), in_specs=specs,
        out_specs=[row_spec, row_spec, row_spec],
        out_shape=[jax.ShapeDtypeStruct((S, W), F32), jax.ShapeDtypeStruct((S, W), BF16), jax.ShapeDtypeStruct((S, W), F32)],
        compiler_params=_params("parallel"),
    )(*ins)


def _norm_bwd(dy, u, g, *, center, eps, name, rows=256, after=None):
    S, W = u.shape
    rows = _tile(S, rows)

    def body(dy_ref, u_ref, g_ref, *rest):
        du_ref, du16_ref, dg_ref, db_ref = rest[-4:]
        i = pl.program_id(0)
        uu = u_ref[...]
        dyv = dy_ref[...]
        if center:
            xc = uu - jnp.mean(uu, axis=1, keepdims=True)
        else:
            xc = uu
        rstd = lax.rsqrt(jnp.mean(xc * xc, axis=1, keepdims=True) + eps)
        xhat = xc * rstd
        dxh = dyv * g_ref[...]
        c2 = jnp.mean(dxh * xhat, axis=1, keepdims=True)
        du = dxh - xhat * c2
        if center:
            du = du - jnp.mean(dxh, axis=1, keepdims=True)
        du = du * rstd
        du_ref[...] = du
        du16_ref[...] = du.astype(BF16)

        @pl.when(i == 0)
        def _():
            dg_ref[...] = jnp.zeros_like(dg_ref)
            db_ref[...] = jnp.zeros_like(db_ref)

        dg_ref[...] += jnp.sum(dyv * xhat, axis=0, keepdims=True)
        db_ref[...] += jnp.sum(dyv, axis=0, keepdims=True)

    row_spec = pl.BlockSpec((rows, W), lambda i: (i, 0))
    vec_spec = pl.BlockSpec((1, W), lambda i: (0, 0))
    return pl.pallas_call(
        body, name=name, grid=(S // rows,),
        in_specs=[row_spec, row_spec, vec_spec] + ([] if after is None else [pl.BlockSpec(memory_space=pl.ANY)]),
        out_specs=[row_spec, row_spec, vec_spec, vec_spec],
        out_shape=[jax.ShapeDtypeStruct((S, W), F32), jax.ShapeDtypeStruct((S, W), BF16),
                   jax.ShapeDtypeStruct((1, W), F32), jax.ShapeDtypeStruct((1, W), F32)],
        compiler_params=_params("arbitrary"),
    )(dy, u, g.reshape(1, W), *([] if after is None else [after]))


def _loss_head(y, target, *, name, rows=256):
    S, W = y.shape
    rows = _tile(S, rows)

    def body(y_ref, t_ref, dy_ref, l_ref):
        i = pl.program_id(0)
        e = y_ref[...] - t_ref[...]
        dy_ref[...] = e * (1.0 / W)

        @pl.when(i == 0)
        def _():
            l_ref[...] = jnp.zeros_like(l_ref)

        l_ref[...] += jnp.full(l_ref.shape, 0.5 / W * jnp.sum(e * e), F32)

    row_spec = pl.BlockSpec((rows, W), lambda i: (i, 0))
    return pl.pallas_call(
        body, name=name, grid=(S // rows,), in_specs=[row_spec, row_spec],
        out_specs=[row_spec, pl.BlockSpec((1, 128), lambda i: (0, 0))],
        out_shape=[jax.ShapeDtypeStruct((S, W), F32), jax.ShapeDtypeStruct((1, 128), F32)],
        compiler_params=_params("arbitrary"),
    )(y, target)


def _adamw(w, g, m, v, *, name, layer=0, carry=None, rows=256):
    L, R, C = w.shape
    rows = max(t for t in range(8, rows + 1, 8) if R % t == 0) if R % 8 == 0 else R
    c1 = 1.0 / (1.0 - ADAM_B1 ** ADAM_STEP)
    c2 = 1.0 / (1.0 - ADAM_B2 ** ADAM_STEP)

    def body(w_ref, g_ref, m_ref, v_ref, *rest):
        go_ref, d_ref, mo_ref, vo_ref = rest[-4:]
        gg = g_ref[...]
        mn = ADAM_B1 * m_ref[...] + (1.0 - ADAM_B1) * gg
        vn = ADAM_B2 * v_ref[...] + (1.0 - ADAM_B2) * (gg * gg)
        go_ref[...] = gg
        d_ref[...] = -ADAM_LR * ((mn * c1) / (jnp.sqrt(vn * c2) + ADAM_EPS) + ADAM_WD * w_ref[...])
        mo_ref[...] = mn
        vo_ref[...] = vn

    slab = pl.BlockSpec((None, rows, C), lambda i: (layer, i, 0))
    shp = jax.ShapeDtypeStruct((L, R, C), F32)
    carried = [] if carry is None else list(carry)
    return pl.pallas_call(
        body, name=name, grid=(R // rows,),
        in_specs=[slab, pl.BlockSpec((rows, C), lambda i: (i, 0)), slab, slab] + [pl.BlockSpec(memory_space=pl.ANY)] * len(carried),
        out_specs=[slab] * 4, out_shape=[shp] * 4, input_output_aliases={4 + k: k for k in range(len(carried))},
        compiler_params=_params("parallel"),
    )(w, g, m, v, *carried)


def _rope(x1, x2, cos, sin, *, name, rows=512):
    H, S, W = x1.shape
    rows = _tile(S, rows)

    def body(x1_ref, x2_ref, c_ref, s_ref, y1_ref, y2_ref):
        t1 = jnp.sum(x1_ref[...], axis=0)
        t2 = jnp.sum(x2_ref[...], axis=0)
        c = c_ref[...]
        s = s_ref[...]
        y1_ref[...] = t1 * c - t2 * s
        y2_ref[...] = t1 * s + t2 * c

    xs = pl.BlockSpec((H, rows, W), lambda i: (0, i, 0))
    ts = pl.BlockSpec((rows, W), lambda i: (i, 0))
    shp = jax.ShapeDtypeStruct((S, W), F32)
    return pl.pallas_call(
        body, name=name, grid=(S // rows,), in_specs=[xs, xs, ts, ts], out_specs=[ts, ts], out_shape=[shp, shp],
        compiler_params=_params("parallel"),
    )(x1, x2, cos, sin)


def _band_mask(blk, n, n_back):
    row = lax.broadcasted_iota(jnp.int32, (blk, 2 * blk), 0)
    col = lax.broadcasted_iota(jnp.int32, (blk, 2 * blk), 1)
    rel = blk + row - col
    return rel, (rel >= 0) & (rel <= n_back) & ((col >= blk) | (n >= 1))


def _window(ref, head, i, blk):
    before = pl.multiple_of(jnp.maximum(i - 1, 0) * blk, blk)
    return jnp.concatenate([ref[head, pl.ds(before, blk), :], ref[head, pl.ds(pl.multiple_of(i * blk, blk), blk), :]], axis=0)


def _window_add(ref, head, i, blk, update):
    before = pl.multiple_of(jnp.maximum(i - 1, 0) * blk, blk)
    ref[head, pl.ds(before, blk), :] += update[:blk]
    ref[head, pl.ds(pl.multiple_of(i * blk, blk), blk), :] += update[blk:]


def _band_fwd(q, k, v, slopes, sinks, *, hps, nb_seq, n_back, scale, dist_scale, name):
    blk = ATT_BLK
    Hq, T, d = q.shape
    Hk = k.shape[0]
    group = Hq // Hk
    kps = max(hps // group, 1)
    use_sink = sinks is not None

    def body(*refs):
        q_ref, k_ref, v_ref, slope_ref = refs[:4]
        sink_ref = refs[4] if use_sink else None
        o_ref, lse_ref = refs[-2:]
        i = pl.program_id(1)
        rel, valid = _band_mask(blk, i % nb_seq, n_back)
        relf = rel.astype(F32)
        for hh in range(hps):
            h = pl.program_id(0) * hps + hh
            kk = _window(k_ref, hh // group, i, blk)
            vv = _window(v_ref, hh // group, i, blk)
            s = lax.dot_general(q_ref[hh], kk, (((1,), (1,)), ((), ())), preferred_element_type=F32) * scale
            s = jnp.where(valid, s - (slope_ref[h] * dist_scale) * relf, NEG)
            m = jnp.max(s, axis=1, keepdims=True)
            if use_sink:
                m = jnp.maximum(m, sink_ref[h])
            p = jnp.where(valid, jnp.exp(s - m), 0.0)
            l = jnp.sum(p, axis=1, keepdims=True)
            if use_sink:
                l = l + jnp.exp(sink_ref[h] - m)
            o_ref[hh] = jnp.dot(p.astype(BF16), vv, preferred_element_type=F32) / l
            lse_ref[hh] = m + jnp.log(l)

    smem = pl.BlockSpec(memory_space=pltpu.SMEM)
    qs = pl.BlockSpec((hps, blk, d), lambda g, i: (g, i, 0))
    ks = pl.BlockSpec((kps, T, d), lambda g, i: (g, 0, 0))
    ins = [q, k, v, slopes] + ([sinks] if use_sink else [])
    return pl.pallas_call(
        body, name=name, grid=(Hq // hps, T // blk), in_specs=[qs, ks, ks, smem] + ([smem] if use_sink else []),
        out_specs=[qs, pl.BlockSpec((hps, blk, 1), lambda g, i: (g, i, 0))],
        out_shape=[jax.ShapeDtypeStruct((Hq, T, d), F32), jax.ShapeDtypeStruct((Hq, T, 1), F32)],
        compiler_params=_params("parallel", "parallel"),
    )(*ins)


def _band_bwd(q, k, v, o, do, lse, dlse, slopes, sinks, *, hps, nb_seq, n_back, scale, dist_scale, name):
    blk = ATT_BLK
    Hq, T, d = q.shape
    Hk = k.shape[0]
    group = Hq // Hk
    kps = max(hps // group, 1)
    use_sink, use_dlse = sinks is not None, dlse is not None

    def body(*refs):
        q_ref, k_ref, v_ref, o_ref, do_ref, lse_ref = refs[:6]
        pos = 6
        dlse_ref = refs[pos] if use_dlse else None
        pos += use_dlse
        slope_ref = refs[pos]
        pos += 1
        sink_ref = refs[pos] if use_sink else None
        pos += use_sink
        dq_ref, dk_ref, dv_ref = refs[pos:pos + 3]
        dsink_ref = refs[pos + 3] if use_sink else None
        i = pl.program_id(1)
        rel, valid = _band_mask(blk, i % nb_seq, n_back)
        relf = rel.astype(F32)

        @pl.when(i == 0)
        def _():
            dk_ref[...] = jnp.zeros_like(dk_ref)
            dv_ref[...] = jnp.zeros_like(dv_ref)
            if use_sink:
                dsink_ref[...] = jnp.zeros_like(dsink_ref)

        for kh in range(kps):
            dk_acc = jnp.zeros((2 * blk, d), F32)
            dv_acc = jnp.zeros((2 * blk, d), F32)
            kk = _window(k_ref, kh, i, blk)
            vv = _window(v_ref, kh, i, blk)
            for hh in range(kh * min(group, hps), (kh + 1) * min(group, hps)):
                h = pl.program_id(0) * hps + hh
                qb = q_ref[hh]
                dob = do_ref[hh]
                do16 = dob.astype(BF16)
                lse = lse_ref[hh]
                c = -jnp.sum(dob * o_ref[hh], axis=1, keepdims=True)
                if use_dlse:
                    c = c + dlse_ref[hh]
                s = lax.dot_general(qb, kk, (((1,), (1,)), ((), ())), preferred_element_type=F32) * scale
                s = jnp.where(valid, s - (slope_ref[h] * dist_scale) * relf, NEG)
                p = jnp.where(valid, jnp.exp(s - lse), 0.0)
                dp = lax.dot_general(do16, vv, (((1,), (1,)), ((), ())), preferred_element_type=F32)
                ds16 = (p * (dp + c) * scale).astype(BF16)
                dq_ref[hh] = jnp.dot(ds16, kk, preferred_element_type=F32)
                dk_acc = dk_acc + lax.dot_general(ds16, qb, (((0,), (0,)), ((), ())), preferred_element_type=F32)
                dv_acc = dv_acc + lax.dot_general(p.astype(BF16), do16, (((0,), (0,)), ((), ())), preferred_element_type=F32)
                if use_sink:
                    dsink_ref[hh] += jnp.full((1, 128), jnp.sum(jnp.exp(sink_ref[h] - lse) * c), F32)
            _window_add(dk_ref, kh, i, blk, dk_acc)
            _window_add(dv_ref, kh, i, blk, dv_acc)

    smem = pl.BlockSpec(memory_space=pltpu.SMEM)
    qs = pl.BlockSpec((hps, blk, d), lambda g, i: (g, i, 0))
    ls = pl.BlockSpec((hps, blk, 1), lambda g, i: (g, i, 0))
    ks = pl.BlockSpec((kps, T, d), lambda g, i: (g, 0, 0))
    in_specs = [qs, ks, ks, qs, qs, ls] + ([ls] if use_dlse else []) + [smem] + ([smem] if use_sink else [])
    ins = [q, k, v, o, do, lse] + ([dlse] if use_dlse else []) + [slopes] + ([sinks] if use_sink else [])
    out_specs = [qs, ks, ks]
    out_shape = [jax.ShapeDtypeStruct((Hq, T, d), F32), jax.ShapeDtypeStruct((Hk, T, d), F32),
                 jax.ShapeDtypeStruct((Hk, T, d), F32)]
    if use_sink:
        out_specs.append(pl.BlockSpec((hps, 1, 128), lambda g, i: (g, 0, 0)))
        out_shape.append(jax.ShapeDtypeStruct((Hq, 1, 128), F32))
    return pl.pallas_call(
        body, name=name, grid=(Hq // hps, T // blk), in_specs=in_specs, out_specs=out_specs, out_shape=out_shape,
        compiler_params=_params("parallel", "arbitrary"),
    )(*ins)


def _diagonal_mask(blk):
    return lax.broadcasted_iota(jnp.int32, (blk, blk), 0) >= lax.broadcasted_iota(jnp.int32, (blk, blk), 1)


def _causal_fwd(q, k, v, *, blk, hps, scale, name):
    H, T, dqk = q.shape
    dv = v.shape[2]

    def body(q_ref, k_ref, v_ref, o_ref, lse_ref):
        n = pl.program_id(1)
        qs = [q_ref[hh] for hh in range(hps)]

        def block(j, carry, valid):
            start = pl.multiple_of(j * blk, blk)
            out = []
            for hh in range(hps):
                m, l, acc = carry[hh]
                kb = k_ref[hh, pl.ds(start, blk), :]
                vb = v_ref[hh, pl.ds(start, blk), :]
                s = lax.dot_general(qs[hh], kb, (((1,), (1,)), ((), ())), preferred_element_type=F32) * scale
                if valid is not None:
                    s = jnp.where(valid, s, NEG)
                m_new = jnp.maximum(m, jnp.max(s, axis=1, keepdims=True))
                p = _keep(valid, jnp.exp(s - m_new))
                a = jnp.exp(m - m_new)
                out.append((m_new, a * l + jnp.sum(p, axis=1, keepdims=True),
                            a * acc + jnp.dot(p.astype(BF16), vb, preferred_element_type=F32)))
            return tuple(out)

        init = tuple((jnp.full((blk, 1), NEG, F32), jnp.zeros((blk, 1), F32), jnp.zeros((blk, dv), F32)) for _ in range(hps))
        res = block(n, lax.fori_loop(0, n, lambda j, carry: block(j, carry, None), init), _diagonal_mask(blk))
        for hh in range(hps):
            m, l, acc = res[hh]
            o_ref[hh] = acc / l
            lse_ref[hh] = m + jnp.log(l)

    qs_ = pl.BlockSpec((hps, blk, dqk), lambda g, i: (g, i, 0))
    return pl.pallas_call(
        body, name=name, grid=(H // hps, T // blk),
        in_specs=[qs_, pl.BlockSpec((hps, T, dqk), lambda g, i: (g, 0, 0)), pl.BlockSpec((hps, T, dv), lambda g, i: (g, 0, 0))],
        out_specs=[pl.BlockSpec((hps, blk, dv), lambda g, i: (g, i, 0)), pl.BlockSpec((hps, blk, 1), lambda g, i: (g, i, 0))],
        out_shape=[jax.ShapeDtypeStruct((H, T, dv), F32), jax.ShapeDtypeStruct((H, T, 1), F32)],
        compiler_params=_params("parallel", "parallel"),
    )(q, k, v)


def _causal_bwd(q, k, v, o, do, lse, *, blk, hps, scale, name):
    H, T, dqk = q.shape
    dv = v.shape[2]

    def body(q_ref, k_ref, v_ref, o_ref, do_ref, lse_ref, dq_ref, dk_ref, dv_ref):
        n = pl.program_id(1)

        @pl.when(n == 0)
        def _():
            dk_ref[...] = jnp.zeros_like(dk_ref)
            dv_ref[...] = jnp.zeros_like(dv_ref)

        qs = [q_ref[hh] for hh in range(hps)]
        do16 = [do_ref[hh].astype(BF16) for hh in range(hps)]
        lses = [lse_ref[hh] for hh in range(hps)]
        cs = [-jnp.sum(do_ref[hh] * o_ref[hh], axis=1, keepdims=True) for hh in range(hps)]

        def block(j, dqs, valid):
            start = pl.multiple_of(j * blk, blk)
            out = []
            for hh in range(hps):
                kb = k_ref[hh, pl.ds(start, blk), :]
                vb = v_ref[hh, pl.ds(start, blk), :]
                s = lax.dot_general(qs[hh], kb, (((1,), (1,)), ((), ())), preferred_element_type=F32) * scale
                if valid is not None:
                    s = jnp.where(valid, s, NEG)
                p = _keep(valid, jnp.exp(s - lses[hh]))
                dp = lax.dot_general(do16[hh], vb, (((1,), (1,)), ((), ())), preferred_element_type=F32)
                ds16 = (p * (dp + cs[hh]) * scale).astype(BF16)
                dk_ref[hh, pl.ds(start, blk), :] += lax.dot_general(ds16, qs[hh], (((0,), (0,)), ((), ())), preferred_element_type=F32)
                dv_ref[hh, pl.ds(start, blk), :] += lax.dot_general(p.astype(BF16), do16[hh], (((0,), (0,)), ((), ())),
                                                                    preferred_element_type=F32)
                out.append(dqs[hh] + jnp.dot(ds16, kb, preferred_element_type=F32))
            return tuple(out)

        init = tuple(jnp.zeros((blk, dqk), F32) for _ in range(hps))
        res = block(n, lax.fori_loop(0, n, lambda j, dqs: block(j, dqs, None), init), _diagonal_mask(blk))
        for hh in range(hps):
            dq_ref[hh] = res[hh]

    qs_ = pl.BlockSpec((hps, blk, dqk), lambda g, i: (g, i, 0))
    os_ = pl.BlockSpec((hps, blk, dv), lambda g, i: (g, i, 0))
    ls_ = pl.BlockSpec((hps, blk, 1), lambda g, i: (g, i, 0))
    ks_ = pl.BlockSpec((hps, T, dqk), lambda g, i: (g, 0, 0))
    vs_ = pl.BlockSpec((hps, T, dv), lambda g, i: (g, 0, 0))
    return pl.pallas_call(
        body, name=name, grid=(H // hps, T // blk), in_specs=[qs_, ks_, vs_, os_, os_, ls_], out_specs=[qs_, ks_, vs_],
        out_shape=[jax.ShapeDtypeStruct((H, T, dqk), F32), jax.ShapeDtypeStruct((H, T, dqk), F32),
                   jax.ShapeDtypeStruct((H, T, dv), F32)],
        compiler_params=_params("parallel", "arbitrary"),
    )(q, k, v, o, do, lse)


def _tri_sum(x, upper):
    hi = x.astype(BF16)
    lo = (x - hi.astype(F32)).astype(BF16)
    return jnp.dot(hi, upper, preferred_element_type=F32) + jnp.dot(lo, upper, preferred_element_type=F32)


def _keep(mask, x):
    return x if mask is None else jnp.where(mask, x, 0.0)


def _stick_terms(qb, kb, strict):
    z = lax.dot_general(qb, kb, (((1,), (1,)), ((), ())), preferred_element_type=F32)
    e = jnp.exp(-jnp.abs(z))
    lb = jnp.minimum(z, 0.0) - jnp.log(1.0 + e)
    return z, e, lb, _keep(strict, lb - z)


def _stick_fwd(q, k, v, *, blk, hps, scale, name):
    H, T, dh = q.shape

    def body(q_ref, k_ref, v_ref, o_ref, tot_ref):
        n = pl.program_id(1)
        qs = [q_ref[hh] for hh in range(hps)]
        r_i = lax.broadcasted_iota(jnp.int32, (blk, blk), 0)
        c_i = lax.broadcasted_iota(jnp.int32, (blk, blk), 1)
        upper = (r_i > c_i).astype(BF16)

        def block(j, carry, strict):
            start = pl.multiple_of(j * blk, blk)
            out = []
            for hh in range(hps):
                run, acc = carry[hh]
                kb = k_ref[hh, pl.ds(start, blk), :]
                vb = v_ref[hh, pl.ds(start, blk), :]
                _, _, lb, lk = _stick_terms(qs[hh], kb, strict)
                w = _keep(strict, jnp.exp(lb + run + _tri_sum(lk, upper)))
                out.append((run + jnp.sum(lk, axis=1, keepdims=True), acc + jnp.dot(w.astype(BF16), vb, preferred_element_type=F32)))
            return tuple(out)

        init = tuple((jnp.zeros((blk, 1), F32), jnp.zeros((blk, dh), F32)) for _ in range(hps))
        res = lax.fori_loop(1, n + 1, lambda t, carry: block(n - t, carry, None), block(n, init, r_i > c_i))
        for hh in range(hps):
            tot_ref[hh], o_ref[hh] = res[hh]

    qs_ = pl.BlockSpec((hps, blk, dh), lambda g, i: (g, i, 0))
    ks_ = pl.BlockSpec((hps, T, dh), lambda g, i: (g, 0, 0))
    ts_ = pl.BlockSpec((hps, blk, 1), lambda g, i: (g, i, 0))
    return pl.pallas_call(
        body, name=name, grid=(H // hps, T // blk), in_specs=[qs_, ks_, ks_], out_specs=[qs_, ts_],
        out_shape=[jax.ShapeDtypeStruct((H, T, dh), F32), jax.ShapeDtypeStruct((H, T, 1), F32)],
        compiler_params=_params("parallel", "parallel"),
    )(q, k, v)


def _stick_bwd(q, k, v, tot, do, *, blk, hps, scale, name):
    H, T, dh = q.shape

    def body(q_ref, k_ref, v_ref, tot_ref, do_ref, dq_ref, dk_ref, dv_ref):
        n = pl.program_id(1)
        qs = [q_ref[hh] for hh in range(hps)]
        do16 = [do_ref[hh].astype(BF16) for hh in range(hps)]
        tots = [tot_ref[hh] for hh in range(hps)]
        r_i = lax.broadcasted_iota(jnp.int32, (blk, blk), 0)
        c_i = lax.broadcasted_iota(jnp.int32, (blk, blk), 1)
        upto = (r_i <= c_i).astype(BF16)
        below = (r_i < c_i).astype(BF16)

        @pl.when(n == 0)
        def _():
            dk_ref[...] = jnp.zeros_like(dk_ref)
            dv_ref[...] = jnp.zeros_like(dv_ref)

        def block(j, carry, strict):
            start = pl.multiple_of(j * blk, blk)
            out = []
            for hh in range(hps):
                run, grun, dq = carry[hh]
                kb = k_ref[hh, pl.ds(start, blk), :]
                vb = v_ref[hh, pl.ds(start, blk), :]
                z, e, lb, lk = _stick_terms(qs[hh], kb, strict)
                w = _keep(strict, jnp.exp(lb + tots[hh] - (run + _tri_sum(lk, upto))))
                dw = lax.dot_general(do16[hh], vb, (((1,), (1,)), ((), ())), preferred_element_type=F32)
                g = w * dw
                before = grun + jnp.dot(g.astype(BF16), below, preferred_element_type=F32)
                inv = 1.0 / (1.0 + e)
                sig = jnp.where(z >= 0, inv, e * inv)
                dz16 = _keep(strict, g * (1.0 - sig) - sig * before).astype(BF16)
                dk_ref[hh, pl.ds(start, blk), :] += lax.dot_general(dz16, qs[hh], (((0,), (0,)), ((), ())), preferred_element_type=F32)
                dv_ref[hh, pl.ds(start, blk), :] += lax.dot_general(w.astype(BF16), do16[hh], (((0,), (0,)), ((), ())),
                                                                    preferred_element_type=F32)
                out.append((run + jnp.sum(lk, axis=1, keepdims=True), grun + jnp.sum(g, axis=1, keepdims=True),
                            dq + jnp.dot(dz16, kb, preferred_element_type=F32)))
            return tuple(out)

        zero = jnp.zeros((blk, 1), F32)
        init = tuple((zero, zero, jnp.zeros((blk, dh), F32)) for _ in range(hps))
        res = block(n, lax.fori_loop(0, n, lambda j, carry: block(j, carry, None), init), r_i > c_i)
        for hh in range(hps):
            dq_ref[hh] = res[hh][2] * scale

    qs_ = pl.BlockSpec((hps, blk, dh), lambda g, i: (g, i, 0))
    ks_ = pl.BlockSpec((hps, T, dh), lambda g, i: (g, 0, 0))
    ts_ = pl.BlockSpec((hps, blk, 1), lambda g, i: (g, i, 0))
    shp = jax.ShapeDtypeStruct((H, T, dh), F32)
    return pl.pallas_call(
        body, name=name, grid=(H // hps, T // blk), in_specs=[qs_, ks_, ks_, ts_, qs_], out_specs=[qs_, ks_, ks_],
        out_shape=[shp, shp, shp],
        compiler_params=_params("parallel", "arbitrary"),
    )(q, k, v, tot, do)


def _mix_fwd(outs, lses, *, name, rows=512):
    H, T, dh = outs[0].shape
    rows = _tile(T, rows)

    def body(o0, o1, o2, l0, l1, l2, y_ref):
        ls = [l0[...], l1[...], l2[...]]
        mx = jnp.maximum(jnp.maximum(ls[0], ls[1]), ls[2])
        es = [jnp.exp(x - mx) for x in ls]
        den = es[0] + es[1] + es[2]
        y_ref[...] = (es[0] * o0[...] + es[1] * o1[...] + es[2] * o2[...]) / den

    os_ = pl.BlockSpec((None, rows, dh), lambda h, i: (h, i, 0))
    ls_ = pl.BlockSpec((None, rows, 1), lambda h, i: (h, i, 0))
    return pl.pallas_call(
        body, name=name, grid=(H, T // rows), in_specs=[os_] * 3 + [ls_] * 3, out_specs=os_,
        out_shape=jax.ShapeDtypeStruct((H, T, dh), F32),
        compiler_params=_params("parallel", "parallel"),
    )(*outs, *lses)


def _mix_bwd(outs, lses, dy, *, name, rows=512):
    H, T, dh = outs[0].shape
    rows = _tile(T, rows)

    def body(o0, o1, o2, l0, l1, l2, dy_ref, d0, d1, d2, g0, g1, g2):
        ls = [l0[...], l1[...], l2[...]]
        mx = jnp.maximum(jnp.maximum(ls[0], ls[1]), ls[2])
        es = [jnp.exp(x - mx) for x in ls]
        den = es[0] + es[1] + es[2]
        mix = [e / den for e in es]
        dyv = dy_ref[...]
        dm = [jnp.sum(dyv * o[...], axis=1, keepdims=True) for o in (o0, o1, o2)]
        avg = mix[0] * dm[0] + mix[1] * dm[1] + mix[2] * dm[2]
        for d_ref, g_ref, w, t in zip((d0, d1, d2), (g0, g1, g2), mix, dm):
            d_ref[...] = w * dyv
            g_ref[...] = w * (t - avg)

    os_ = pl.BlockSpec((None, rows, dh), lambda h, i: (h, i, 0))
    ls_ = pl.BlockSpec((None, rows, 1), lambda h, i: (h, i, 0))
    so = jax.ShapeDtypeStruct((H, T, dh), F32)
    sl = jax.ShapeDtypeStruct((H, T, 1), F32)
    res = pl.pallas_call(
        body, name=name, grid=(H, T // rows), in_specs=[os_] * 3 + [ls_] * 3 + [os_], out_specs=[os_] * 3 + [ls_] * 3,
        out_shape=[so] * 3 + [sl] * 3,
        compiler_params=_params("parallel", "parallel"),
    )(*outs, *lses, dy)
    return res[:3], res[3:]


def _position():
    return lax.axis_index("x"), lax.axis_index("y"), lax.axis_index("c")


def _other_chips(x, y):
    return [(1 - x, y), (x, 1 - y), (1 - x, 1 - y)]


HBM_SPEC = pl.BlockSpec(memory_space=pltpu.HBM)
SEM_SPEC = pl.BlockSpec(memory_space=pltpu.SEMAPHORE)
ANY_SPEC = pl.BlockSpec(memory_space=pl.ANY)
SPLIT_COPY = pltpu.CompilerParams(has_side_effects=pltpu.SideEffectType.DATAFLOW_SIDE_EFFECTING)


def _in_hbm(a):
    return pltpu.with_memory_space_constraint(a, pltpu.HBM)


SAME_CORE_OF_OTHER_CHIPS = ((1, 0, 0), (0, 1, 0), (1, 1, 0))
ALL_OTHER_DEVICES = ((0, 0, 1), (1, 0, 0), (0, 1, 0), (1, 1, 0), (1, 0, 1), (0, 1, 1), (1, 1, 1))


def _peer_copies(srcs, lands, send_sems, recv_sems, relations, src_of, dst_of):
    me = _position()
    copies = []
    for w in range(len(srcs)):
        for k, flips in enumerate(relations):
            peer = tuple(1 - p if f else p for p, f in zip(me, flips))
            i = len(relations) * w + k
            copies.append(pltpu.make_async_remote_copy(
                src_ref=src_of(srcs[w], peer), dst_ref=dst_of(lands[w], k, peer, me), send_sem=send_sems[i],
                recv_sem=recv_sems[i], device_id=peer, device_id_type=MESH))
    return copies


def _copies_start(srcs, land_shapes, after, relations, src_of, dst_of, *, name):
    n = len(srcs)
    m = len(relations) * n

    def body(*refs):
        src_refs, land_refs = refs[:n], refs[n:2 * n]
        send_sems, recv_sems = refs[2 * n + 1:2 * n + 1 + m], refs[2 * n + 1 + m:2 * n + 1 + 2 * m]
        token = refs[-1]
        for cp in _peer_copies(src_refs, land_refs, send_sems, recv_sems, relations, src_of, dst_of):
            cp.start()
        token[...] = jnp.zeros_like(token)

    lands = [_in_hbm(lax.empty(s.shape, s.dtype)) for s in land_shapes]
    out_shape = ([pltpu.SemaphoreType.DMA(())] * (2 * m)
                 + [pltpu.HBM(a.shape, a.dtype) for a in srcs] + [pltpu.HBM(s.shape, s.dtype) for s in land_shapes]
                 + [jax.ShapeDtypeStruct((8, 128), F32)])
    res = pl.pallas_call(
        body, name=name, in_specs=[HBM_SPEC] * (2 * n) + [ANY_SPEC],
        out_specs=[SEM_SPEC] * (2 * m) + [HBM_SPEC] * (2 * n) + [pl.BlockSpec(memory_space=pltpu.VMEM)],
        out_shape=out_shape, input_output_aliases={i: 2 * m + i for i in range(2 * n)}, compiler_params=SPLIT_COPY,
    )(*[_in_hbm(a) for a in srcs], *lands, after)
    return (list(res[:m]), list(res[m:2 * m]), list(res[2 * m:2 * m + n]), list(res[2 * m + n:2 * m + 2 * n]), res[-1])


def _copies_wait(started, after, relations, src_of, dst_of, *, name):
    send_sems, recv_sems, srcs, lands, _ = started
    n = len(srcs)
    m = len(relations) * n

    def body(*refs):
        src_refs, land_refs = refs[:n], refs[n:2 * n]
        send_refs, recv_refs = refs[2 * n:2 * n + m], refs[2 * n + m:2 * n + 2 * m]
        for cp in _peer_copies(src_refs, land_refs, send_refs, recv_refs, relations, src_of, dst_of):
            cp.wait_send()
            cp.wait_recv()

    res = pl.pallas_call(
        body, name=name, in_specs=[HBM_SPEC] * (2 * n) + [SEM_SPEC] * (2 * m) + [ANY_SPEC], out_specs=[HBM_SPEC] * (2 * n),
        out_shape=[pltpu.HBM(a.shape, a.dtype) for a in srcs + lands],
        input_output_aliases={i: i for i in range(2 * n)}, compiler_params=SPLIT_COPY,
    )(*srcs, *lands, *send_sems, *recv_sems, after)
    return list(res[:n]), list(res[n:])


def _half_rows(ref, core):
    half = ref.shape[-2] // 2
    return pl.ds(core * half, half)


def _gather_src(src, peer):
    return src.at[_half_rows(src, peer[2]), :]


def _gather_dst(land, k, peer, me):
    return land.at[2 * me[0] + me[1], _half_rows(land, me[2]), :]


def _gather_landed(land, k, peer, me):
    return land.at[2 * peer[0] + peer[1], _half_rows(land, me[2]), :]


def _exchange_src(src, peer):
    return src.at[2 * peer[0] + peer[1], _half_rows(src, peer[2]), :]


def _exchange_dst(land, k, peer, me):
    return land.at[k]


def _small_src(src, peer):
    return src


def _small_dst(land, k, peer, me):
    return land.at[4 * me[0] + 2 * me[1] + me[2]]


def _small_landed(land, k, peer, me):
    return land.at[4 * peer[0] + 2 * peer[1] + peer[2]]


def _forward_copies(bufs, send_sems, recv_sems, core):
    x, y, c = _position()
    copies = []
    for w in range(len(bufs)):
        for j, chip in enumerate(_other_chips(x, y)):
            rows = _gather_landed(bufs[w], j, chip, (x, y, core))
            copies.append(pltpu.make_async_remote_copy(src_ref=rows, dst_ref=rows, send_sem=send_sems[3 * w + j],
                                                       recv_sem=recv_sems[3 * w + j], device_id=(x, y, 1 - c), device_id_type=MESH))
    return copies


def _forward_start(bufs, *, name):
    n = len(bufs)
    m = 3 * n

    def body(*refs):
        send_sems, recv_sems = refs[n:n + m], refs[n + m:n + 2 * m]
        for cp in _forward_copies(refs[:n], send_sems, recv_sems, lax.axis_index("c")):
            cp.start()

    res = pl.pallas_call(
        body, name=name, in_specs=[HBM_SPEC] * n, out_specs=[SEM_SPEC] * (2 * m) + [HBM_SPEC] * n,
        out_shape=[pltpu.SemaphoreType.DMA(())] * (2 * m) + [pltpu.HBM(a.shape, a.dtype) for a in bufs],
        input_output_aliases={i: 2 * m + i for i in range(n)}, compiler_params=SPLIT_COPY,
    )(*bufs)
    return list(res[:m]), list(res[m:2 * m]), list(res[2 * m:])


def _forward_wait(started, after, *, name):
    send_sems, recv_sems, bufs = started
    n = len(bufs)
    m = 3 * n

    def body(*refs):
        send_refs, recv_refs = refs[n:n + m], refs[n + m:n + 2 * m]
        c = lax.axis_index("c")
        for sent, got in zip(_forward_copies(refs[:n], send_refs, recv_refs, c),
                             _forward_copies(refs[:n], send_refs, recv_refs, 1 - c)):
            sent.wait_send()
            got.wait_recv()

    return list(pl.pallas_call(
        body, name=name, in_specs=[HBM_SPEC] * n + [SEM_SPEC] * (2 * m) + [ANY_SPEC], out_specs=[HBM_SPEC] * n,
        out_shape=[pltpu.HBM(a.shape, a.dtype) for a in bufs], input_output_aliases={i: i for i in range(n)},
        compiler_params=SPLIT_COPY,
    )(*bufs, *send_sems, *recv_sems, after))


def _share_halves(bufs, *, name):
    n = len(bufs)

    def body(*refs):
        ins, outs = refs[:n], refs[n:2 * n]
        send_sems, recv_sems = refs[2 * n:]
        x, y, c = _position()
        sends = []
        for w in range(n):
            sends.append(pltpu.make_async_remote_copy(src_ref=ins[w].at[c], dst_ref=outs[w].at[c], send_sem=send_sems.at[w],
                                                      recv_sem=recv_sems.at[w], device_id=(x, y, 1 - c), device_id_type=MESH))
            sends[-1].start()
        for w in range(n):
            pltpu.make_async_remote_copy(src_ref=ins[w].at[1 - c], dst_ref=outs[w].at[1 - c], send_sem=send_sems.at[w],
                                         recv_sem=recv_sems.at[w], device_id=(x, y, 1 - c), device_id_type=MESH).wait_recv()
        for cp in sends:
            cp.wait_send()

    hbm = pl.BlockSpec(memory_space=pl.ANY)
    return pl.pallas_call(
        body, name=name, in_specs=[hbm] * n, out_specs=[hbm] * n,
        out_shape=[jax.ShapeDtypeStruct(a.shape, a.dtype) for a in bufs],
        input_output_aliases={w: w for w in range(n)},
        scratch_shapes=[pltpu.SemaphoreType.DMA((n,)), pltpu.SemaphoreType.DMA((n,))],
    )(*bufs)


def _all_gather8(v, *, name):
    m, n = v.shape

    def body(v_ref, out_ref, send_sems, recv_sems, local_sem):
        x, y, c = _position()
        me, sibling = (x, y, c), (x, y, 1 - c)
        chips = _other_chips(x, y)

        def rows(px, py, pc):
            return out_ref.at[pl.ds((4 * px + 2 * py + pc) * m, m), :]

        def copy(k, block, to, src=None):
            return pltpu.make_async_remote_copy(src_ref=rows(*block) if src is None else src, dst_ref=rows(*block),
                                                send_sem=send_sems.at[k], recv_sem=recv_sems.at[k], device_id=to, device_id_type=MESH)

        mine = pltpu.make_async_copy(v_ref, rows(*me), local_sem)
        mine.start()
        first = [copy(0, me, sibling, src=v_ref)]
        first += [copy(1 + j, me, (*chip, c), src=v_ref) for j, chip in enumerate(chips)]
        for cp in first:
            cp.start()
        passed = [copy(4 + j, (*chip, c), sibling) for j, chip in enumerate(chips)]
        for j, chip in enumerate(chips):
            copy(1 + j, (*chip, c), me).wait_recv()
            passed[j].start()
        copy(0, sibling, me).wait_recv()
        for j, chip in enumerate(chips):
            copy(4 + j, (*chip, 1 - c), me).wait_recv()
        for cp in first + passed:
            cp.wait_send()
        mine.wait()

    vmem = pl.BlockSpec(memory_space=pltpu.VMEM)
    return pl.pallas_call(
        body, name=name, in_specs=[vmem], out_specs=vmem, out_shape=jax.ShapeDtypeStruct((8 * m, n), v.dtype),
        scratch_shapes=[pltpu.SemaphoreType.DMA((7,)), pltpu.SemaphoreType.DMA((7,)), pltpu.SemaphoreType.DMA],
    )(v)


def _sum_parts(own, landed, where, *, name, rows=256):
    n_peers, half, C = landed.shape
    rows = _tile(half, rows)
    nb = half // rows

    def body(where_ref, own_ref, land_ref, o_ref):
        acc = own_ref[...].astype(F32)
        for k in range(n_peers):
            acc = acc + land_ref[k].astype(F32)
        o_ref[...] = acc

    grid_spec = pltpu.PrefetchScalarGridSpec(
        num_scalar_prefetch=1, grid=(nb,),
        in_specs=[pl.BlockSpec((None, rows, C), lambda i, where_ref: (where_ref[0], where_ref[1] * nb + i, 0)),
                  pl.BlockSpec((n_peers, rows, C), lambda i, where_ref: (0, i, 0))],
        out_specs=pl.BlockSpec((None, rows, C), lambda i, where_ref: (where_ref[1], i, 0)))
    return pl.pallas_call(
        body, name=name, grid_spec=grid_spec, out_shape=jax.ShapeDtypeStruct((2, half, C), F32),
        compiler_params=_params("parallel"),
    )(where, own, landed)


def _sum_slabs(a, *, name, rows=256):
    P, R, C = a.shape
    rows = _tile(R, rows)

    def body(a_ref, o_ref):
        acc = a_ref[0].astype(F32)
        for p in range(1, P):
            acc = acc + a_ref[p].astype(F32)
        o_ref[...] = acc

    return pl.pallas_call(
        body, name=name, grid=(R // rows,), in_specs=[pl.BlockSpec((P, rows, C), lambda i: (0, i, 0))],
        out_specs=pl.BlockSpec((rows, C), lambda i: (i, 0)),
        out_shape=jax.ShapeDtypeStruct((R, C), F32), compiler_params=_params("parallel"),
    )(a)


def _reduce_scatter_start(grads, after=None, *, name):
    lands = [jax.ShapeDtypeStruct((len(ALL_OTHER_DEVICES), g.shape[1] // 2, g.shape[2]), g.dtype) for g in grads]
    return _copies_start(grads, lands, grads[0] if after is None else after, ALL_OTHER_DEVICES, _exchange_src, _exchange_dst,
                         name=name + "_start")


def _reduce_scatter_finish(started, after, *, name):
    core = lax.axis_index("c").astype(jnp.int32)
    chip = (2 * lax.axis_index("x") + lax.axis_index("y")).astype(jnp.int32)
    where = jnp.stack([chip, core])
    sums, landed = _copies_wait(started, after, ALL_OTHER_DEVICES, _exchange_src, _exchange_dst, name=name + "_wait")
    reduced = [_sum_parts(s, a, where, name=f"{name}_sum{w}") for w, (s, a) in enumerate(zip(sums, landed))]
    both = _share_halves(reduced, name=name + "_share")
    return [b.reshape(2 * b.shape[1], b.shape[2]) for b in both]


def _to_heads(t, heads, dil=1):
    S = t.shape[0]
    d = t.shape[1] // heads
    return jnp.transpose(t.reshape(S // dil, dil, heads, d), (2, 1, 0, 3)).reshape(heads, S, d)


def _from_heads(t, dil=1):
    heads, S, d = t.shape
    return jnp.transpose(t.reshape(heads, dil, S // dil, d), (2, 1, 0, 3)).reshape(S, heads * d)


def _unstride(t, dil):
    heads, S, d = t.shape
    return jnp.transpose(t.reshape(heads, dil, S // dil, d), (0, 2, 1, 3)).reshape(heads, S, d)


def _restride(t, dil):
    heads, S, d = t.shape
    return jnp.transpose(t.reshape(heads, S // dil, dil, d), (0, 2, 1, 3)).reshape(heads, S, d)


def _pad_cols(t, width, padded):
    S = t.shape[0]
    return jnp.pad(t.reshape(S, N_CHIPS, width), ((0, 0), (0, 0), (0, padded - width))).reshape(S, N_CHIPS * padded)


def _column_reader(t, width, padded):
    def take(lo, hi):
        pieces = []
        for p in range(N_CHIPS):
            a, b = max(lo, p * width), min(hi, (p + 1) * width)
            if a < b:
                pieces.append(t[:, p * padded + a - p * width:p * padded + b - p * width])
        return pieces[0] if len(pieces) == 1 else jnp.concatenate(pieces, axis=-1)
    return take


def _alibi(n):
    return 2.0 ** (-8.0 * jnp.arange(1, n + 1, dtype=F32) / n)


B_KW = [dict(hps=4, nb_seq=SEQ // d // ATT_BLK, n_back=w // d, scale=1.0 / math.sqrt(HEAD_DIM), dist_scale=d)
        for w, d in B_PATTERNS]
A_KW = dict(hps=A_Q_HEADS // A_KV_HEADS, nb_seq=SEQ // ATT_BLK, n_back=A_WINDOW - 1, scale=1.0 / math.sqrt(HEAD_DIM), dist_scale=1)
D_KW = dict(blk=1024, hps=1, scale=1.0 / math.sqrt(D_NOPE + D_ROPE))
C_KW = dict(blk=512, hps=1, scale=1.0 / math.sqrt(HEAD_DIM))


def _rope_tables(reps):
    inv_freq = ROPE_BASE ** (-jnp.arange(0, D_ROPE, 2, dtype=F32) / D_ROPE)
    ang = jnp.arange(SEQ, dtype=F32)[:, None] * inv_freq[None, :]
    return jnp.tile(jnp.cos(ang), (1, reps)), jnp.tile(jnp.sin(ang), (1, reps))


def _mlp_fwd(x, x16, w1, w2, g, b, tag):
    hid, act = _matmul(x16, w1, mode="nn", out_dtype=BF16, epilogue="relu2", name=f"mlp{tag}_up")
    m = _matmul(act, w2, mode="nn", out_dtype=F32, name=f"mlp{tag}_down")
    y, y16, u = _norm_fwd(x, m, g, b, center=True, eps=LN_EPS, alpha=ALPHA, name=f"ln2_{tag}")
    return y, y16, (x16, hid, act, u)


def _mlp_bwd(dy, saved, w1, w2, g, tag, after=None):
    x16, hid, act, u = saved
    du, du16, dg, db = _norm_bwd(dy, u, g, center=True, eps=LN_EPS, name=f"ln2_{tag}_bwd", after=after)
    dw2 = _matmul_tn(act, du16, shards=1, name=f"mlp{tag}_dw2")
    dhid = _matmul(du16, w2, mode="nt", out_dtype=BF16, epilogue="drelu2", extra=hid, name=f"mlp{tag}_dact")
    dw1 = _matmul_tn(x16, dhid, shards=N_CHIPS, name=f"mlp{tag}_dw1")
    dx = _matmul(dhid, w1, mode="nt", out_dtype=F32, epilogue="add", extra=du, alpha=ALPHA, name=f"mlp{tag}_dx")
    return dx, dw1, dw2, dg, db


def _even_fwd(x16, w_in, sinks):
    h = _column_reader(_matmul(x16, w_in, mode="nn", out_dtype=BF16, name="even_in"), EVEN_IN // N_CHIPS, EVEN_IN_PAD)
    qa = _to_heads(h(0, A_Q_W), A_Q_HEADS)
    ka = _to_heads(h(A_Q_W, A_Q_W + A_KV_W), A_KV_HEADS)
    va = _to_heads(h(A_Q_W + A_KV_W, A_Q_W + 2 * A_KV_W), A_KV_HEADS)
    slopes_a, slopes_b = _alibi(A_Q_HEADS), _alibi(B_HEADS)
    oa, lse_a = _band_fwd(qa, ka, va, slopes_a, sinks, name="swa_fwd", **A_KW)
    qkv_b, outs, lses = [], [], []
    for gi, (_, dil) in enumerate(B_PATTERNS):
        base = A_Q_W + 2 * A_KV_W + gi * 3 * B_W
        q, k, v = (_to_heads(h(base + i * B_W, base + (i + 1) * B_W), B_HEADS, dil) for i in range(3))
        o, lse = _band_fwd(q, k, v, slopes_b, None, name=f"dil{gi}_fwd", **B_KW[gi])
        qkv_b.append((q, k, v, o, lse))
        outs.append(_unstride(o, dil))
        lses.append(_unstride(lse, dil))
    ob = _mix_fwd(outs, lses, name="dil_mix")
    y16 = jnp.concatenate([_from_heads(oa), _from_heads(ob)], axis=-1).astype(BF16)
    return y16, (x16, qa, ka, va, oa, lse_a, qkv_b, outs, lses, y16)


def _even_bwd(dmix16, du, saved, w_in, sinks, w_out, send_w_out, send_w_in):
    x16, qa, ka, va, oa, lse_a, qkv_b, outs, lses, y16 = saved
    slopes_a, slopes_b = _alibi(A_Q_HEADS), _alibi(B_HEADS)
    sent = send_w_out(_matmul_tn(y16, dmix16, shards=N_CHIPS, name="even_dwout"))
    dy = _matmul(dmix16, w_out, mode="nt", out_dtype=F32, name="even_dy", after=sent)
    doa = _to_heads(dy[:, :A_Q_W], A_Q_HEADS)
    dob = _to_heads(dy[:, A_Q_W:], B_HEADS)
    dqa, dka, dva, dsink = _band_bwd(qa, ka, va, oa, doa, lse_a, None, slopes_a, sinks, name="swa_bwd", **A_KW)
    douts, dlses = _mix_bwd(outs, lses, dob, name="dil_mix_bwd")
    parts = [_from_heads(dqa), _from_heads(dka), _from_heads(dva)]
    for gi, (_, dil) in enumerate(B_PATTERNS):
        q, k, v, o, lse = qkv_b[gi]
        dq, dk, dv = _band_bwd(q, k, v, o, _restride(douts[gi], dil), lse, _restride(dlses[gi], dil), slopes_b, None,
                               name=f"dil{gi}_bwd", **B_KW[gi])
        parts += [_from_heads(dq, dil), _from_heads(dk, dil), _from_heads(dv, dil)]
    dh16 = _pad_cols(jnp.concatenate(parts, axis=-1), EVEN_IN // N_CHIPS, EVEN_IN_PAD).astype(BF16)
    sent = send_w_in(_matmul_tn(x16, dh16, shards=N_CHIPS, name="even_dwin"), dsink[:, 0, 0])
    return _matmul(dh16, w_in, mode="nt", out_dtype=F32, epilogue="add", extra=du, alpha=ALPHA, name="even_dx", after=sent)


def _odd_fwd(x16, w_in, qn_g, kvn_g, w_uq, w_ukv, w_out):
    S = x16.shape[0]
    h = _column_reader(_matmul(x16, w_in, mode="nn", out_dtype=F32, name="odd_in"), ODD_IN // N_CHIPS, ODD_IN_PAD)
    qc, kc, vc = (_to_heads(h(i * C_W, (i + 1) * C_W).astype(BF16), C_HEADS) for i in range(3))
    qc = qc * C_KW["scale"]
    cq = h(3 * C_W, 3 * C_W + D_Q_RANK)
    ckv = h(3 * C_W + D_Q_RANK, 3 * C_W + D_Q_RANK + D_KV_RANK)
    kr = h(3 * C_W + D_Q_RANK + D_KV_RANK, ODD_IN)
    oc, tot = _stick_fwd(qc, kc, vc, name="stick_fwd", **C_KW)
    _, cqn16, _ = _norm_fwd(cq, None, qn_g, None, center=False, eps=RMS_EPS, alpha=1.0, name="q_rms")
    _, ckvn16, _ = _norm_fwd(ckv, None, kvn_g, None, center=False, eps=RMS_EPS, alpha=1.0, name="kv_rms")
    q = _matmul(cqn16, w_uq, mode="nn", out_dtype=F32, name="mla_uq").reshape(S, D_HEADS, D_NOPE + D_ROPE)
    kv = _matmul(ckvn16, w_ukv, mode="nn", out_dtype=F32, name="mla_ukv").reshape(S, D_HEADS, D_NOPE + D_V)
    half = D_ROPE // 2
    q_rope = q[..., D_NOPE:]
    x1 = jnp.concatenate([q_rope[..., :half].reshape(S, D_HEADS * half), kr[:, :half]], axis=-1)
    x2 = jnp.concatenate([q_rope[..., half:].reshape(S, D_HEADS * half), kr[:, half:]], axis=-1)
    cos, sin = _rope_tables(D_HEADS + 1)
    y1, y2 = _rope(x1[None], x2[None], cos, sin, name="rope_fwd")
    nq = D_HEADS * half
    q_rot = jnp.concatenate([y1[:, :nq].reshape(S, D_HEADS, half), y2[:, :nq].reshape(S, D_HEADS, half)], axis=-1)
    kr_rot = jnp.concatenate([y1[:, nq:], y2[:, nq:]], axis=-1)
    qd = jnp.transpose(jnp.concatenate([q[..., :D_NOPE], q_rot], axis=-1), (1, 0, 2)).astype(BF16)
    kd = jnp.transpose(jnp.concatenate([kv[..., :D_NOPE], jnp.broadcast_to(kr_rot[:, None, :], (S, D_HEADS, D_ROPE))], axis=-1),
                       (1, 0, 2)).astype(BF16)
    vd = jnp.transpose(kv[..., D_NOPE:], (1, 0, 2)).astype(BF16)
    od, lse_d = _causal_fwd(qd, kd, vd, name="mla_fwd", **D_KW)
    y16 = jnp.concatenate([_from_heads(oc), _from_heads(od)], axis=-1).astype(BF16)
    mixed = _matmul(y16, w_out, mode="nn", out_dtype=F32, name="odd_out")
    return mixed, (x16, qc, kc, vc, tot, cq, ckv, cqn16, ckvn16, qd, kd, vd, od, lse_d, y16)


def _odd_bwd(dmix16, du, saved, w_in, qn_g, kvn_g, w_uq, w_ukv, w_out):
    x16, qc, kc, vc, tot, cq, ckv, cqn16, ckvn16, qd, kd, vd, od, lse_d, y16 = saved
    S = x16.shape[0]
    half = D_ROPE // 2
    dw_out = _matmul_tn(y16, dmix16, shards=1, name="odd_dwout")
    dy = _matmul(dmix16, w_out, mode="nt", out_dtype=F32, name="odd_dy")
    doc = _to_heads(dy[:, :C_W], C_HEADS)
    dod = _to_heads(dy[:, C_W:], D_HEADS)
    dqc, dkc, dvc = _stick_bwd(qc, kc, vc, tot, doc, name="stick_bwd", **C_KW)
    dqd, dkd, dvd = _causal_bwd(qd, kd, vd, od, dod, lse_d, name="mla_bwd", **D_KW)
    cos, sin = _rope_tables(D_HEADS + 1)
    nq = D_HEADS * half
    gq1 = jnp.transpose(dqd[..., D_NOPE:D_NOPE + half], (1, 0, 2)).reshape(1, S, nq)
    gq2 = jnp.transpose(dqd[..., D_NOPE + half:], (1, 0, 2)).reshape(1, S, nq)
    dq1, dq2 = _rope(gq1, gq2, cos[:, :nq], -sin[:, :nq], name="rope_bwd_q")
    dk1, dk2 = _rope(dkd[..., D_NOPE:D_NOPE + half], dkd[..., D_NOPE + half:], cos[:, nq:], -sin[:, nq:], name="rope_bwd_k")
    dq_full = jnp.concatenate([jnp.transpose(dqd[..., :D_NOPE], (1, 0, 2)), dq1.reshape(S, D_HEADS, half),
                               dq2.reshape(S, D_HEADS, half)], axis=-1).reshape(S, D_HEADS * (D_NOPE + D_ROPE)).astype(BF16)
    dkv_full = jnp.concatenate([jnp.transpose(dkd[..., :D_NOPE], (1, 0, 2)), jnp.transpose(dvd, (1, 0, 2))],
                               axis=-1).reshape(S, D_HEADS * (D_NOPE + D_V)).astype(BF16)
    dw_uq = _matmul_tn(cqn16, dq_full, shards=N_CHIPS, name="mla_dwuq")
    dw_ukv = _matmul_tn(ckvn16, dkv_full, shards=N_CHIPS, name="mla_dwukv")
    dcqn = _matmul(dq_full, w_uq, mode="nt", out_dtype=F32, name="mla_dcq")
    dckvn = _matmul(dkv_full, w_ukv, mode="nt", out_dtype=F32, name="mla_dckv")
    dcq, _, dqn_g, _ = _norm_bwd(dcqn, cq, qn_g, center=False, eps=RMS_EPS, name="q_rms_bwd")
    dckv, _, dkvn_g, _ = _norm_bwd(dckvn, ckv, kvn_g, center=False, eps=RMS_EPS, name="kv_rms_bwd")
    dh = jnp.concatenate([_from_heads(dqc), _from_heads(dkc), _from_heads(dvc), dcq, dckv, dk1, dk2], axis=-1)
    dh16 = _pad_cols(dh, ODD_IN // N_CHIPS, ODD_IN_PAD).astype(BF16)
    dw_in = _matmul_tn(x16, dh16, shards=N_CHIPS, name="odd_dwin")
    dx = _matmul(dh16, w_in, mode="nt", out_dtype=F32, epilogue="add", extra=du, alpha=ALPHA, name="odd_dx")
    return dx, dw_in, dqn_g[0], dkvn_g[0], dw_uq, dw_ukv, dw_out


WEIGHT_GROUPS = (("even_w_in",), ("even_w_out", "mlp_w1_0", "mlp_w2_0"), ("odd_w_in", "odd_w_uq", "odd_w_ukv", "odd_w_out"),
                 ("mlp_w1_1", "mlp_w2_1"))
GRAD_GROUPS = (("mlp_w2_1", "mlp_w1_1"), ("odd_w_out", "odd_w_uq", "odd_w_ukv", "odd_w_in"), ("mlp_w2_0", "mlp_w1_0"),
               ("even_w_out",), ("even_w_in",))


def _local_step(x, target, small, weights_for, send_grads, send_small, total_loss):
    def by_rows(t, rows):
        return t.reshape(N_CHIPS, rows // N_CHIPS, D_MODEL)

    x16 = x.astype(BF16)
    w = dict(weights_for(0, x16))
    y16, sv_e = _even_fwd(x16, w["even_w_in"], small["even_sinks"])
    w.update(weights_for(1, y16))
    mixed0 = _matmul(y16, w["even_w_out"], mode="nn", out_dtype=F32, name="even_out")
    x1, x1_16, u01 = _norm_fwd(x, mixed0, small["ln1_g"][0], small["ln1_b"][0], center=True, eps=LN_EPS, alpha=ALPHA, name="ln1_0")
    w1_0, w2_0 = w["mlp_w1_0"], w["mlp_w2_0"].reshape(1, D_FF, D_MODEL)
    x2, x2_16, sv_m0 = _mlp_fwd(x1, x1_16, w1_0, w2_0, small["ln2_g"][0], small["ln2_b"][0], 0)
    w.update(weights_for(2, x2_16))
    odd_w = (w["odd_w_in"], small["odd_q_norm_g"], small["odd_kv_norm_g"], w["odd_w_uq"], w["odd_w_ukv"],
             w["odd_w_out"].reshape(1, D_MODEL, D_MODEL))
    mixed1, sv_o = _odd_fwd(x2_16, *odd_w)
    x3, x3_16, u11 = _norm_fwd(x2, mixed1, small["ln1_g"][1], small["ln1_b"][1], center=True, eps=LN_EPS, alpha=ALPHA, name="ln1_1")
    w.update(weights_for(3, x3_16))
    w1_1, w2_1 = w["mlp_w1_1"], w["mlp_w2_1"].reshape(1, D_FF, D_MODEL)
    x4, _, sv_m1 = _mlp_fwd(x3, x3_16, w1_1, w2_1, small["ln2_g"][1], small["ln2_b"][1], 1)
    dy, loss_row = _loss_head(x4, target, name="loss_head")
    loss = total_loss(loss_row)

    dx3, dw1_1, dw2_1, dln2g_1, dln2b_1 = _mlp_bwd(dy, sv_m1, w1_1, w2_1, small["ln2_g"][1], 1, after=loss.reshape(1, 1))
    sent = send_grads(0, dict(mlp_w2_1=by_rows(dw2_1, D_FF), mlp_w1_1=dw1_1))
    du, du16, dln1g_1, dln1b_1 = _norm_bwd(dx3, u11, small["ln1_g"][1], center=True, eps=LN_EPS, name="ln1_1_bwd", after=sent)
    dx2, dwin_o, dqn_g, dkvn_g, dwuq, dwukv, dwout_o = _odd_bwd(du16, du, sv_o, *odd_w)
    sent = send_grads(1, dict(odd_w_out=by_rows(dwout_o, D_MODEL), odd_w_uq=dwuq, odd_w_ukv=dwukv, odd_w_in=dwin_o))
    dx1, dw1_0, dw2_0, dln2g_0, dln2b_0 = _mlp_bwd(dx2, sv_m0, w1_0, w2_0, small["ln2_g"][0], 0, after=sent)
    sent = send_grads(2, dict(mlp_w2_0=by_rows(dw2_0, D_FF), mlp_w1_0=dw1_0))
    du, du16, dln1g_0, dln1b_0 = _norm_bwd(dx1, u01, small["ln1_g"][0], center=True, eps=LN_EPS, name="ln1_0_bwd", after=sent)
    def send_last(dwin_e, dsinks):
        went = send_small(dict(even_sinks=dsinks, odd_q_norm_g=dqn_g, odd_kv_norm_g=dkvn_g,
                               ln1_g=jnp.concatenate([dln1g_0, dln1g_1]), ln1_b=jnp.concatenate([dln1b_0, dln1b_1]),
                               ln2_g=jnp.concatenate([dln2g_0, dln2g_1]), ln2_b=jnp.concatenate([dln2b_0, dln2b_1])))
        return send_grads(4, dict(even_w_in=dwin_e), went)

    dx0 = _even_bwd(du16, du, sv_e, w["even_w_in"], small["even_sinks"], w["even_w_out"],
                    lambda dwout_e: send_grads(3, dict(even_w_out=dwout_e)), send_last)
    return loss, dx0


SMALL_ORDER = ("ln1_g", "ln1_b", "ln2_g", "ln2_b", "even_sinks", "odd_q_norm_g", "odd_kv_norm_g")
SMALL_COLS = 2176


def _pack_small(parts):
    flat = jnp.concatenate([p.reshape(-1) for p in parts])
    return jnp.pad(flat, (0, 8 * SMALL_COLS - flat.shape[0])).reshape(8, SMALL_COLS)


def _unpack_small(packed, shapes):
    flat = packed.reshape(-1)
    out, pos = [], 0
    for s in shapes:
        n = math.prod(s)
        out.append(flat[pos:pos + n].reshape(s))
        pos += n
    return out


def kernel(x, even_w_in, even_sinks, even_w_out, odd_w_in, odd_q_norm_g, odd_kv_norm_g, odd_w_uq, odd_w_ukv, odd_w_out, ln1_g, ln1_b, mlp_w1, mlp_w2, ln2_g, ln2_b, loss_target, m_even_w_in, m_even_sinks, m_even_w_out, m_odd_w_in, m_odd_q_norm_g, m_odd_kv_norm_g, m_odd_w_uq, m_odd_w_ukv, m_odd_w_out, m_ln1_g, m_ln1_b, m_mlp_w1, m_mlp_w2, m_ln2_g, m_ln2_b, v_even_w_in, v_even_sinks, v_even_w_out, v_odd_w_in, v_odd_q_norm_g, v_odd_kv_norm_g, v_odd_w_uq, v_odd_w_ukv, v_odd_w_out, v_ln1_g, v_ln1_b, v_mlp_w1, v_mlp_w2, v_ln2_g, v_ln2_b):
    chip = 2 * lax.axis_index("x") + lax.axis_index("y")
    e_w, o_w = EVEN_IN // N_CHIPS, ODD_IN // N_CHIPS

    shards = dict(
        even_w_in=jnp.pad(even_w_in[0], ((0, 0), (0, EVEN_IN_PAD - e_w))), even_w_out=even_w_out[0],
        odd_w_in=jnp.pad(odd_w_in[0], ((0, 0), (0, ODD_IN_PAD - o_w))), odd_w_uq=odd_w_uq[0], odd_w_ukv=odd_w_ukv[0],
        odd_w_out=odd_w_out[0], mlp_w1_0=mlp_w1[0], mlp_w1_1=mlp_w1[1], mlp_w2_0=mlp_w2[0], mlp_w2_1=mlp_w2[1])
    names = list(shards)
    own16 = {n: shards[n].astype(BF16) for n in names}
    gather_started, token = [], own16[WEIGHT_GROUPS[0][0]]
    for g, group in enumerate(WEIGHT_GROUPS):
        lands = [jax.ShapeDtypeStruct((N_CHIPS,) + own16[n].shape, BF16) for n in group]
        gather_started.append(_copies_start([own16[n] for n in group], lands, token, SAME_CORE_OF_OTHER_CHIPS, _gather_src,
                                            _gather_dst, name=f"gather{g}_start"))
        token = gather_started[-1][-1]
    all_started = token

    forwarding = {}

    def forward(g, after):
        _, landed = _copies_wait(gather_started[g], after, SAME_CORE_OF_OTHER_CHIPS, _gather_src, _gather_landed,
                                 name=f"gather{g}_wait")
        forwarding[g] = _forward_start(landed, name=f"gather{g}_forward")

    def weights_for(g, after):
        if g not in forwarding:
            forward(g, all_started if g == 0 else after)
        full = _forward_wait(forwarding[g], after, name=f"gather{g}_forwarded")
        if 1 <= g < len(WEIGHT_GROUPS) - 1:
            forward(g + 1, after)
        return {n: lax.dynamic_update_slice(f, own16[n][None], (chip, 0, 0)) for n, f in zip(WEIGHT_GROUPS[g], full)}

    grads_started = {}

    def send_grads(g, shares, after=None):
        grads_started[g] = _reduce_scatter_start([shares[n] for n in GRAD_GROUPS[g]], after, name=f"grads{g}")
        return grads_started[g][-1]

    norm_rows = jnp.pad(jnp.concatenate([odd_q_norm_g[0], odd_kv_norm_g[0]])[None], ((0, 7), (0, 64)))
    norm_all = _all_gather8(norm_rows, name="gather_norm_gains")[0::16]
    qn_w, kvn_w = D_Q_RANK // N_CHIPS, D_KV_RANK // N_CHIPS
    small = dict(even_sinks=even_sinks[0], odd_q_norm_g=norm_all[:, :qn_w].reshape(D_Q_RANK),
                 odd_kv_norm_g=norm_all[:, qn_w:qn_w + kvn_w].reshape(D_KV_RANK), ln1_g=ln1_g, ln1_b=ln1_b, ln2_g=ln2_g, ln2_b=ln2_b)

    small_started = []

    def send_small(sm):
        pack = _pack_small([sm[n] for n in SMALL_ORDER])
        small_started.append(_copies_start([pack], [jax.ShapeDtypeStruct((8,) + pack.shape, F32)], pack, ALL_OTHER_DEVICES,
                                           _small_src, _small_dst, name="small_grads_start"))
        return small_started[0][-1]

    def small_sum(after):
        (pack,), (landed,) = _copies_wait(small_started[0], after, ALL_OTHER_DEVICES, _small_src, _small_landed,
                                          name="small_grads_wait")
        device = 4 * lax.axis_index("x") + 2 * lax.axis_index("y") + lax.axis_index("c")
        return _sum_slabs(lax.dynamic_update_slice(landed, pack[None], (device, 0, 0)), name="sum_small_grads")

    loss, grad_x = _local_step(x[0], loss_target[0], small, weights_for, send_grads, send_small,
                               lambda row: lax.psum(row[0, 0], ("x", "y", "c")))

    weights = dict(even_w_in=even_w_in, even_sinks=even_sinks, even_w_out=even_w_out, odd_w_in=odd_w_in, odd_q_norm_g=odd_q_norm_g,
                   odd_kv_norm_g=odd_kv_norm_g, odd_w_uq=odd_w_uq, odd_w_ukv=odd_w_ukv, odd_w_out=odd_w_out, ln1_g=ln1_g, ln1_b=ln1_b,
                   mlp_w1=mlp_w1, mlp_w2=mlp_w2, ln2_g=ln2_g, ln2_b=ln2_b)
    m_in = dict(even_w_in=m_even_w_in, even_sinks=m_even_sinks, even_w_out=m_even_w_out, odd_w_in=m_odd_w_in,
                odd_q_norm_g=m_odd_q_norm_g, odd_kv_norm_g=m_odd_kv_norm_g, odd_w_uq=m_odd_w_uq, odd_w_ukv=m_odd_w_ukv,
                odd_w_out=m_odd_w_out, ln1_g=m_ln1_g, ln1_b=m_ln1_b, mlp_w1=m_mlp_w1, mlp_w2=m_mlp_w2, ln2_g=m_ln2_g, ln2_b=m_ln2_b)
    v_in = dict(even_w_in=v_even_w_in, even_sinks=v_even_sinks, even_w_out=v_even_w_out, odd_w_in=v_odd_w_in,
                odd_q_norm_g=v_odd_q_norm_g, odd_kv_norm_g=v_odd_kv_norm_g, odd_w_uq=v_odd_w_uq, odd_w_ukv=v_odd_w_ukv,
                odd_w_out=v_odd_w_out, ln1_g=v_ln1_g, ln1_b=v_ln1_b, mlp_w1=v_mlp_w1, mlp_w2=v_mlp_w2, ln2_g=v_ln2_g, ln2_b=v_ln2_b)
    order = list(weights)
    updated = {}

    def update(n, g2d, layer=0, tag="", by_column=False):
        shape = weights[n].shape
        as3d = (1, math.prod(shape[:-1]), shape[-1]) if len(shape) == 2 else shape
        view = (lambda t: jnp.swapaxes(t.reshape(as3d), 1, 2)) if by_column else (lambda t: t.reshape(as3d))
        res = _adamw(view(weights[n]), g2d.T if by_column else g2d, view(m_in[n]), view(v_in[n]), layer=layer,
                     carry=updated.get(n), name=f"adamw_{n}{tag}")
        updated[n] = tuple(jnp.swapaxes(t, 1, 2) for t in res) if by_column else tuple(res)
        return res[0]

    after = grad_x
    for g, group in enumerate(GRAD_GROUPS):
        for n, r in zip(group, _reduce_scatter_finish(grads_started[g], after, name=f"grads{g}")):
            if n[:-2] in ("mlp_w1", "mlp_w2"):
                after = update(n[:-2], r, layer=int(n[-1]), tag=n[-2:])
            elif n == "even_w_in":
                after = update(n, r[:, :e_w], by_column=True)
            elif n == "odd_w_in":
                after = update(n, r[:, :o_w], by_column=True)
            else:
                after = update(n, r)
        if g == len(GRAD_GROUPS) - 2:
            g_ln1g, g_ln1b, g_ln2g, g_ln2b, g_sinks, g_qn, g_kvn = _unpack_small(
                small_sum(after), [(DEPTH, D_MODEL)] * 4 + [(1, A_Q_HEADS), (D_Q_RANK,), (D_KV_RANK,)])
            for n, gs in (("even_sinks", g_sinks), ("odd_q_norm_g", lax.dynamic_slice_in_dim(g_qn, chip * qn_w, qn_w)[None]),
                          ("odd_kv_norm_g", lax.dynamic_slice_in_dim(g_kvn, chip * kvn_w, kvn_w)[None]), ("ln1_g", g_ln1g),
                          ("ln1_b", g_ln1b), ("ln2_g", g_ln2g), ("ln2_b", g_ln2b)):
                update(n, gs)
    outs = [[updated[n][k].reshape(weights[n].shape) for n in order] for k in range(4)]
    return (loss, grad_x[None], *outs[0], *outs[1], *outs[2], *outs[3])
```

```python
import math

import jax
import jax.numpy as jnp
from jax import lax
from jax.experimental import pallas as pl
from jax.experimental.pallas import tpu as pltpu

F32 = jnp.float32
BF16 = jnp.bfloat16
MESH = pl.DeviceIdType.MESH

D_MODEL = 2048
SEQ = 2048
DEPTH = 2
HEAD_DIM = 64
A_Q_HEADS, A_KV_HEADS, A_WINDOW = 16, 2, 128
B_HEADS = 8
B_PATTERNS = ((128, 1), (512, 4), (2048, 16))
C_HEADS = 16
D_HEADS, D_Q_RANK, D_KV_RANK, D_NOPE, D_ROPE, D_V = 16, 512, 256, 64, 32, 64
ROPE_BASE = 10000.0
D_FF = 4 * D_MODEL
LN_EPS = 1e-5
RMS_EPS = 1e-6
ALPHA = (2 * DEPTH) ** 0.25
A_Q_W = A_Q_HEADS * HEAD_DIM
A_KV_W = A_KV_HEADS * HEAD_DIM
B_W = B_HEADS * HEAD_DIM
EVEN_IN = A_Q_W + 2 * A_KV_W + 3 * B_W * len(B_PATTERNS)
EVEN_OUT = A_Q_W + B_W
C_W = C_HEADS * HEAD_DIM
ODD_IN = 3 * C_W + D_Q_RANK + D_KV_RANK + D_ROPE
ADAM_LR, ADAM_B1, ADAM_B2, ADAM_EPS, ADAM_WD, ADAM_STEP = 0.001, 0.9, 0.999, 1e-08, 0.01, 10

N_CHIPS = 4
EVEN_IN_PAD = 1536
ODD_IN_PAD = 1024
ATT_BLK = 128
NEG = -1e30
V7X_VMEM_LIMIT = 48 * 1024 * 1024


def _params(*sem):
    return pltpu.CompilerParams(dimension_semantics=sem, vmem_limit_bytes=V7X_VMEM_LIMIT)


def _tile(n, cap):
    if n <= cap:
        return n
    t = cap - cap % 128
    while n % t:
        t -= 128
    return t


def _matmul(a, b, *, mode, out_dtype, name, epilogue=None, extra=None, alpha=1.0, after=None, tm=1024, tn=1024, tk=2048):
    if mode == "nn":
        M, K = a.shape
        P, _, Np = b.shape
        tm, tn, tk = _tile(M, tm), _tile(Np, tn), _tile(K, tk)
        nj = Np // tn
        grid = (M // tm, P * nj, K // tk)
        a_spec = pl.BlockSpec((tm, tk), lambda i, j, k: (i, k))
        b_spec = pl.BlockSpec((None, tk, tn), lambda i, j, k: (j // nj, k, j % nj))
        o_spec = pl.BlockSpec((tm, tn), lambda i, j, k: (i, j))
        out_shape = (M, P * Np)
        dims = (((1,), (0,)), ((), ()))
    elif mode == "nt":
        M, N = a.shape
        P, K, Np = b.shape
        tm, tn, tk = _tile(M, tm), _tile(K, tn), _tile(Np, tk)
        nkk = Np // tk
        grid = (M // tm, K // tn, P * nkk)
        a_spec = pl.BlockSpec((tm, tk), lambda i, j, k: (i, k))
        b_spec = pl.BlockSpec((None, tn, tk), lambda i, j, k: (k // nkk, j, k % nkk))
        o_spec = pl.BlockSpec((tm, tn), lambda i, j, k: (i, j))
        out_shape = (M, K)
        dims = (((1,), (1,)), ((), ()))
    else:
        raise ValueError(mode)
    n_k = grid[2]
    n_extra = 0 if extra is None else 1
    n_after = 0 if after is None else 1
    n_out = 2 if epilogue == "relu2" else 1

    def body(*refs):
        a_ref, b_ref = refs[:2]
        e_ref = refs[2] if n_extra else None
        outs = refs[2 + n_extra + n_after:2 + n_extra + n_after + n_out]
        part = lax.dot_general(a_ref[...], b_ref[...], dims, preferred_element_type=F32)

        def finish(r):
            if epilogue == "relu2":
                outs[0][...] = r.astype(outs[0].dtype)
                rp = jnp.maximum(r, 0.0)
                outs[1][...] = (rp * rp).astype(outs[1].dtype)
            elif epilogue == "drelu2":
                outs[0][...] = (r * (2.0 * jnp.maximum(e_ref[...].astype(F32), 0.0))).astype(outs[0].dtype)
            elif epilogue == "add":
                outs[0][...] = (r + alpha * e_ref[...].astype(F32)).astype(outs[0].dtype)
            else:
                outs[0][...] = r.astype(outs[0].dtype)

        if n_k == 1:
            finish(part)
        else:
            acc = refs[-1]
            k = pl.program_id(2)

            @pl.when(k == 0)
            def _():
                acc[...] = part

            @pl.when((k > 0) & (k < n_k - 1))
            def _():
                acc[...] += part

            @pl.when(k == n_k - 1)
            def _():
                finish(acc[...] + part)

    in_specs = [a_spec, b_spec] + ([o_spec] if n_extra else []) + ([pl.BlockSpec(memory_space=pl.ANY)] if n_after else [])
    shapes = [jax.ShapeDtypeStruct(out_shape, out_dtype)] * n_out
    res = pl.pallas_call(
        body, name=name, grid=grid, in_specs=in_specs,
        out_specs=[o_spec] * n_out, out_shape=shapes,
        scratch_shapes=[pltpu.VMEM((tm, tn), F32)] if n_k > 1 else [],
        compiler_params=_params("parallel", "parallel", "arbitrary"),
    )(a, b, *([extra] if n_extra else []), *([after] if n_after else []))
    return res if n_out > 1 else res[0]


def _matmul_tn(a, g, *, shards, name, tm=1024, tn=1024):
    M, K = a.shape
    P = shards
    Np = g.shape[1] // P
    tm, tn = _tile(K, tm), _tile(Np, tn)
    nj = Np // tn

    def body(a_ref, g_ref, o_ref):
        o_ref[...] = lax.dot_general(a_ref[...], g_ref[...], (((0,), (0,)), ((), ())), preferred_element_type=F32).astype(BF16)

    return pl.pallas_call(
        body, name=name, grid=(K // tm, P * nj),
        in_specs=[pl.BlockSpec((M, tm), lambda i, j: (0, i)), pl.BlockSpec((M, tn), lambda i, j: (0, j))],
        out_specs=pl.BlockSpec((None, tm, tn), lambda i, j: (j // nj, i, j % nj)),
        out_shape=jax.ShapeDtypeStruct((P, K, Np), BF16),
        compiler_params=_params("parallel", "parallel"),
    )(a, g)


def _norm_fwd(x, res, g, b, *, center, eps, alpha, name, rows=256):
    S, W = x.shape
    has_res = res is not None
    rows = _tile(S, rows)

    def body(*refs):
        x_ref = refs[0]
        r_ref = refs[1] if has_res else None
        g_ref = refs[1 + has_res]
        b_ref = refs[2 + has_res] if center else None
        y_ref, y16_ref, u_ref = refs[-3:]
        u = x_ref[...]
        if has_res:
            u = alpha * u + r_ref[...]
        if center:
            mu = jnp.mean(u, axis=1, keepdims=True)
            xc = u - mu
        else:
            xc = u
        var = jnp.mean(xc * xc, axis=1, keepdims=True)
        y = xc * lax.rsqrt(var + eps) * g_ref[...]
        if center:
            y = y + b_ref[...]
        y_ref[...] = y
        y16_ref[...] = y.astype(BF16)
        u_ref[...] = u

    row_spec = pl.BlockSpec((rows, W), lambda i: (i, 0))
    vec_spec = pl.BlockSpec((1, W), lambda i: (0, 0))
    ins = [x] + ([res] if has_res else []) + [g.reshape(1, W)] + ([b.reshape(1, W)] if center else [])
    specs = [row_spec] + ([row_spec] if has_res else []) + [vec_spec] + ([vec_spec] if center else [])
    return pl.pallas_call(
        body, name=name, grid=(S // rows,), in_specs=specs,
        out_specs=[row_spec, row_spec, row_spec],
        out_shape=[jax.ShapeDtypeStruct((S, W), F32), jax.ShapeDtypeStruct((S, W), BF16), jax.ShapeDtypeStruct((S, W), F32)],
        compiler_params=_params("parallel"),
    )(*ins)


def _norm_bwd(dy, u, g, *, center, eps, name, rows=256, after=None):
    S, W = u.shape
    rows = _tile(S, rows)

    def body(dy_ref, u_ref, g_ref, *rest):
        du_ref, du16_ref, dg_ref, db_ref = rest[-4:]
        i = pl.program_id(0)
        uu = u_ref[...]
        dyv = dy_ref[...]
        if center:
            xc = uu - jnp.mean(uu, axis=1, keepdims=True)
        else:
            xc = uu
        rstd = lax.rsqrt(jnp.mean(xc * xc, axis=1, keepdims=True) + eps)
        xhat = xc * rstd
        dxh = dyv * g_ref[...]
        c2 = jnp.mean(dxh * xhat, axis=1, keepdims=True)
        du = dxh - xhat * c2
        if center:
            du = du - jnp.mean(dxh, axis=1, keepdims=True)
        du = du * rstd
        du_ref[...] = du
        du16_ref[...] = du.astype(BF16)

        @pl.when(i == 0)
        def _():
            dg_ref[...] = jnp.zeros_like(dg_ref)
            db_ref[...] = jnp.zeros_like(db_ref)

        dg_ref[...] += jnp.sum(dyv * xhat, axis=0, keepdims=True)
        db_ref[...] += jnp.sum(dyv, axis=0, keepdims=True)

    row_spec = pl.BlockSpec((rows, W), lambda i: (i, 0))
    vec_spec = pl.BlockSpec((1, W), lambda i: (0, 0))
    return pl.pallas_call(
        body, name=name, grid=(S // rows,),
        in_specs=[row_spec, row_spec, vec_spec] + ([] if after is None else [pl.BlockSpec(memory_space=pl.ANY)]),
        out_specs=[row_spec, row_spec, vec_spec, vec_spec],
        out_shape=[jax.ShapeDtypeStruct((S, W), F32), jax.ShapeDtypeStruct((S, W), BF16),
                   jax.ShapeDtypeStruct((1, W), F32), jax.ShapeDtypeStruct((1, W), F32)],
        compiler_params=_params("arbitrary"),
    )(dy, u, g.reshape(1, W), *([] if after is None else [after]))


def _loss_head(y, target, *, name, rows=256):
    S, W = y.shape
    rows = _tile(S, rows)

    def body(y_ref, t_ref, dy_ref, l_ref):
        i = pl.program_id(0)
        e = y_ref[...] - t_ref[...]
        dy_ref[...] = e * (1.0 / W)

        @pl.when(i == 0)
        def _():
            l_ref[...] = jnp.zeros_like(l_ref)

        l_ref[...] += jnp.full(l_ref.shape, 0.5 / W * jnp.sum(e * e), F32)

    row_spec = pl.BlockSpec((rows, W), lambda i: (i, 0))
    return pl.pallas_call(
        body, name=name, grid=(S // rows,), in_specs=[row_spec, row_spec],
        out_specs=[row_spec, pl.BlockSpec((1, 128), lambda i: (0, 0))],
        out_shape=[jax.ShapeDtypeStruct((S, W), F32), jax.ShapeDtypeStruct((1, 128), F32)],
        compiler_params=_params("arbitrary"),
    )(y, target)


def _adamw(w, g, m, v, *, name, layer=0, carry=None, rows=256):
    L, R, C = w.shape
    rows = max(t for t in range(8, rows + 1, 8) if R % t == 0) if R % 8 == 0 else R
    c1 = 1.0 / (1.0 - ADAM_B1 ** ADAM_STEP)
    c2 = 1.0 / (1.0 - ADAM_B2 ** ADAM_STEP)

    def body(w_ref, g_ref, m_ref, v_ref, *rest):
        go_ref, d_ref, mo_ref, vo_ref = rest[-4:]
        gg = g_ref[...]
        mn = ADAM_B1 * m_ref[...] + (1.0 - ADAM_B1) * gg
        vn = ADAM_B2 * v_ref[...] + (1.0 - ADAM_B2) * (gg * gg)
        go_ref[...] = gg
        d_ref[...] = -ADAM_LR * ((mn * c1) / (jnp.sqrt(vn * c2) + ADAM_EPS) + ADAM_WD * w_ref[...])
        mo_ref[...] = mn
        vo_ref[...] = vn

    slab = pl.BlockSpec((None, rows, C), lambda i: (layer, i, 0))
    shp = jax.ShapeDtypeStruct((L, R, C), F32)
    carried = [] if carry is None else list(carry)
    return pl.pallas_call(
        body, name=name, grid=(R // rows,),
        in_specs=[slab, pl.BlockSpec((rows, C), lambda i: (i, 0)), slab, slab] + [pl.BlockSpec(memory_space=pl.ANY)] * len(carried),
        out_specs=[slab] * 4, out_shape=[shp] * 4, input_output_aliases={4 + k: k for k in range(len(carried))},
        compiler_params=_params("parallel"),
    )(w, g, m, v, *carried)


def _rope(x1, x2, cos, sin, *, name, rows=512):
    H, S, W = x1.shape
    rows = _tile(S, rows)

    def body(x1_ref, x2_ref, c_ref, s_ref, y1_ref, y2_ref):
        t1 = jnp.sum(x1_ref[...], axis=0)
        t2 = jnp.sum(x2_ref[...], axis=0)
        c = c_ref[...]
        s = s_ref[...]
        y1_ref[...] = t1 * c - t2 * s
        y2_ref[...] = t1 * s + t2 * c

    xs = pl.BlockSpec((H, rows, W), lambda i: (0, i, 0))
    ts = pl.BlockSpec((rows, W), lambda i: (i, 0))
    shp = jax.ShapeDtypeStruct((S, W), F32)
    return pl.pallas_call(
        body, name=name, grid=(S // rows,), in_specs=[xs, xs, ts, ts], out_specs=[ts, ts], out_shape=[shp, shp],
        compiler_params=_params("parallel"),
    )(x1, x2, cos, sin)


def _band_mask(blk, n, n_back):
    row = lax.broadcasted_iota(jnp.int32, (blk, 2 * blk), 0)
    col = lax.broadcasted_iota(jnp.int32, (blk, 2 * blk), 1)
    rel = blk + row - col
    return rel, (rel >= 0) & (rel <= n_back) & ((col >= blk) | (n >= 1))


def _window(ref, head, i, blk):
    before = pl.multiple_of(jnp.maximum(i - 1, 0) * blk, blk)
    return jnp.concatenate([ref[head, pl.ds(before, blk), :], ref[head, pl.ds(pl.multiple_of(i * blk, blk), blk), :]], axis=0)


def _window_add(ref, head, i, blk, update):
    before = pl.multiple_of(jnp.maximum(i - 1, 0) * blk, blk)
    ref[head, pl.ds(before, blk), :] += update[:blk]
    ref[head, pl.ds(pl.multiple_of(i * blk, blk), blk), :] += update[blk:]


def _band_fwd(q, k, v, slopes, sinks, *, hps, nb_seq, n_back, scale, dist_scale, name):
    blk = ATT_BLK
    Hq, T, d = q.shape
    Hk = k.shape[0]
    group = Hq // Hk
    kps = max(hps // group, 1)
    use_sink = sinks is not None

    def body(*refs):
        q_ref, k_ref, v_ref, slope_ref = refs[:4]
        sink_ref = refs[4] if use_sink else None
        o_ref, lse_ref = refs[-2:]
        i = pl.program_id(1)
        rel, valid = _band_mask(blk, i % nb_seq, n_back)
        relf = rel.astype(F32)
        for hh in range(hps):
            h = pl.program_id(0) * hps + hh
            kk = _window(k_ref, hh // group, i, blk)
            vv = _window(v_ref, hh // group, i, blk)
            s = lax.dot_general(q_ref[hh], kk, (((1,), (1,)), ((), ())), preferred_element_type=F32)
            s = jnp.where(valid, s - (slope_ref[h] * dist_scale) * relf, NEG)
            m = jnp.max(s, axis=1, keepdims=True)
            if use_sink:
                m = jnp.maximum(m, sink_ref[h])
            p = jnp.where(valid, jnp.exp(s - m), 0.0)
            l = jnp.sum(p, axis=1, keepdims=True)
            if use_sink:
                l = l + jnp.exp(sink_ref[h] - m)
            o_ref[hh] = jnp.dot(p.astype(BF16), vv, preferred_element_type=F32) / l
            lse_ref[hh] = m + jnp.log(l)

    smem = pl.BlockSpec(memory_space=pltpu.SMEM)
    qs = pl.BlockSpec((hps, blk, d), lambda g, i: (g, i, 0))
    ks = pl.BlockSpec((kps, T, d), lambda g, i: (g, 0, 0))
    ins = [q, k, v, slopes] + ([sinks] if use_sink else [])
    return pl.pallas_call(
        body, name=name, grid=(Hq // hps, T // blk), in_specs=[qs, ks, ks, smem] + ([smem] if use_sink else []),
        out_specs=[qs, pl.BlockSpec((hps, blk, 1), lambda g, i: (g, i, 0))],
        out_shape=[jax.ShapeDtypeStruct((Hq, T, d), F32), jax.ShapeDtypeStruct((Hq, T, 1), F32)],
        compiler_params=_params("parallel", "parallel"),
    )(*ins)


def _band_bwd(q, k, v, o, do, lse, dlse, slopes, sinks, *, hps, nb_seq, n_back, scale, dist_scale, name):
    blk = ATT_BLK
    Hq, T, d = q.shape
    Hk = k.shape[0]
    group = Hq // Hk
    kps = max(hps // group, 1)
    use_sink, use_dlse = sinks is not None, dlse is not None

    def body(*refs):
        q_ref, k_ref, v_ref, o_ref, do_ref, lse_ref = refs[:6]
        pos = 6
        dlse_ref = refs[pos] if use_dlse else None
        pos += use_dlse
        slope_ref = refs[pos]
        pos += 1
        sink_ref = refs[pos] if use_sink else None
        pos += use_sink
        dq_ref, dk_ref, dv_ref = refs[pos:pos + 3]
        dsink_ref = refs[pos + 3] if use_sink else None
        i = pl.program_id(1)
        rel, valid = _band_mask(blk, i % nb_seq, n_back)
        relf = rel.astype(F32)

        @pl.when(i == 0)
        def _():
            dk_ref[...] = jnp.zeros_like(dk_ref)
            dv_ref[...] = jnp.zeros_like(dv_ref)
            if use_sink:
                dsink_ref[...] = jnp.zeros_like(dsink_ref)

        for kh in range(kps):
            dk_acc = jnp.zeros((2 * blk, d), F32)
            dv_acc = jnp.zeros((2 * blk, d), F32)
            kk = _window(k_ref, kh, i, blk)
            vv = _window(v_ref, kh, i, blk)
            for hh in range(kh * min(group, hps), (kh + 1) * min(group, hps)):
                h = pl.program_id(0) * hps + hh
                qb = q_ref[hh]
                dob = do_ref[hh]
                do16 = dob.astype(BF16)
                lse = lse_ref[hh]
                c = -jnp.sum(dob * o_ref[hh], axis=1, keepdims=True)
                if use_dlse:
                    c = c + dlse_ref[hh]
                s = lax.dot_general(qb, kk, (((1,), (1,)), ((), ())), preferred_element_type=F32)
                s = jnp.where(valid, s - (slope_ref[h] * dist_scale) * relf, NEG)
                p = jnp.where(valid, jnp.exp(s - lse), 0.0)
                dp = lax.dot_general(do16, vv, (((1,), (1,)), ((), ())), preferred_element_type=F32)
                ds16 = (p * (dp + c)).astype(BF16)
                dq_ref[hh] = jnp.dot(ds16, kk, preferred_element_type=F32) * scale
                dk_acc = dk_acc + lax.dot_general(ds16, qb, (((0,), (0,)), ((), ())), preferred_element_type=F32)
                dv_acc = dv_acc + lax.dot_general(p.astype(BF16), do16, (((0,), (0,)), ((), ())), preferred_element_type=F32)
                if use_sink:
                    dsink_ref[hh] += jnp.full((1, 128), jnp.sum(jnp.exp(sink_ref[h] - lse) * c), F32)
            _window_add(dk_ref, kh, i, blk, dk_acc)
            _window_add(dv_ref, kh, i, blk, dv_acc)

    smem = pl.BlockSpec(memory_space=pltpu.SMEM)
    qs = pl.BlockSpec((hps, blk, d), lambda g, i: (g, i, 0))
    ls = pl.BlockSpec((hps, blk, 1), lambda g, i: (g, i, 0))
    ks = pl.BlockSpec((kps, T, d), lambda g, i: (g, 0, 0))
    in_specs = [qs, ks, ks, qs, qs, ls] + ([ls] if use_dlse else []) + [smem] + ([smem] if use_sink else [])
    ins = [q, k, v, o, do, lse] + ([dlse] if use_dlse else []) + [slopes] + ([sinks] if use_sink else [])
    out_specs = [qs, ks, ks]
    out_shape = [jax.ShapeDtypeStruct((Hq, T, d), F32), jax.ShapeDtypeStruct((Hk, T, d), F32),
                 jax.ShapeDtypeStruct((Hk, T, d), F32)]
    if use_sink:
        out_specs.append(pl.BlockSpec((hps, 1, 128), lambda g, i: (g, 0, 0)))
        out_shape.append(jax.ShapeDtypeStruct((Hq, 1, 128), F32))
    return pl.pallas_call(
        body, name=name, grid=(Hq // hps, T // blk), in_specs=in_specs, out_specs=out_specs, out_shape=out_shape,
        compiler_params=_params("parallel", "arbitrary"),
    )(*ins)


def _diagonal_mask(blk):
    return lax.broadcasted_iota(jnp.int32, (blk, blk), 0) >= lax.broadcasted_iota(jnp.int32, (blk, blk), 1)


def _causal_fwd(q, k, v, *, blk, hps, scale, name):
    H, T, dqk = q.shape
    dv = v.shape[2]

    def body(q_ref, k_ref, v_ref, o_ref, lse_ref):
        n = pl.program_id(1)
        qs = [q_ref[hh] for hh in range(hps)]

        def block(j, carry, valid):
            start = pl.multiple_of(j * blk, blk)
            out = []
            for hh in range(hps):
                m, l, acc = carry[hh]
                kb = k_ref[hh, pl.ds(start, blk), :]
                vb = v_ref[hh, pl.ds(start, blk), :]
                s = lax.dot_general(qs[hh], kb, (((1,), (1,)), ((), ())), preferred_element_type=F32) * scale
                if valid is not None:
                    s = jnp.where(valid, s, NEG)
                m_new = jnp.maximum(m, jnp.max(s, axis=1, keepdims=True))
                p = _keep(valid, jnp.exp(s - m_new))
                a = jnp.exp(m - m_new)
                out.append((m_new, a * l + jnp.sum(p, axis=1, keepdims=True),
                            a * acc + jnp.dot(p.astype(BF16), vb, preferred_element_type=F32)))
            return tuple(out)

        init = tuple((jnp.full((blk, 1), NEG, F32), jnp.zeros((blk, 1), F32), jnp.zeros((blk, dv), F32)) for _ in range(hps))
        res = block(n, lax.fori_loop(0, n, lambda j, carry: block(j, carry, None), init), _diagonal_mask(blk))
        for hh in range(hps):
            m, l, acc = res[hh]
            o_ref[hh] = acc / l
            lse_ref[hh] = m + jnp.log(l)

    qs_ = pl.BlockSpec((hps, blk, dqk), lambda g, i: (g, i, 0))
    return pl.pallas_call(
        body, name=name, grid=(H // hps, T // blk),
        in_specs=[qs_, pl.BlockSpec((hps, T, dqk), lambda g, i: (g, 0, 0)), pl.BlockSpec((hps, T, dv), lambda g, i: (g, 0, 0))],
        out_specs=[pl.BlockSpec((hps, blk, dv), lambda g, i: (g, i, 0)), pl.BlockSpec((hps, blk, 1), lambda g, i: (g, i, 0))],
        out_shape=[jax.ShapeDtypeStruct((H, T, dv), F32), jax.ShapeDtypeStruct((H, T, 1), F32)],
        compiler_params=_params("parallel", "parallel"),
    )(q, k, v)


def _causal_bwd(q, k, v, o, do, lse, *, blk, hps, scale, name):
    H, T, dqk = q.shape
    dv = v.shape[2]

    def body(q_ref, k_ref, v_ref, o_ref, do_ref, lse_ref, dq_ref, dk_ref, dv_ref):
        n = pl.program_id(1)

        @pl.when(n == 0)
        def _():
            dk_ref[...] = jnp.zeros_like(dk_ref)
            dv_ref[...] = jnp.zeros_like(dv_ref)

        qs = [q_ref[hh] for hh in range(hps)]
        do16 = [do_ref[hh].astype(BF16) for hh in range(hps)]
        lses = [lse_ref[hh] for hh in range(hps)]
        cs = [-jnp.sum(do_ref[hh] * o_ref[hh], axis=1, keepdims=True) for hh in range(hps)]

        def block(j, dqs, valid):
            start = pl.multiple_of(j * blk, blk)
            out = []
            for hh in range(hps):
                kb = k_ref[hh, pl.ds(start, blk), :]
                vb = v_ref[hh, pl.ds(start, blk), :]
                s = lax.dot_general(qs[hh], kb, (((1,), (1,)), ((), ())), preferred_element_type=F32) * scale
                if valid is not None:
                    s = jnp.where(valid, s, NEG)
                p = _keep(valid, jnp.exp(s - lses[hh]))
                dp = lax.dot_general(do16[hh], vb, (((1,), (1,)), ((), ())), preferred_element_type=F32)
                ds16 = (p * (dp + cs[hh]) * scale).astype(BF16)
                dk_ref[hh, pl.ds(start, blk), :] += lax.dot_general(ds16, qs[hh], (((0,), (0,)), ((), ())), preferred_element_type=F32)
                dv_ref[hh, pl.ds(start, blk), :] += lax.dot_general(p.astype(BF16), do16[hh], (((0,), (0,)), ((), ())),
                                                                    preferred_element_type=F32)
                out.append(dqs[hh] + jnp.dot(ds16, kb, preferred_element_type=F32))
            return tuple(out)

        init = tuple(jnp.zeros((blk, dqk), F32) for _ in range(hps))
        res = block(n, lax.fori_loop(0, n, lambda j, dqs: block(j, dqs, None), init), _diagonal_mask(blk))
        for hh in range(hps):
            dq_ref[hh] = res[hh]

    qs_ = pl.BlockSpec((hps, blk, dqk), lambda g, i: (g, i, 0))
    os_ = pl.BlockSpec((hps, blk, dv), lambda g, i: (g, i, 0))
    ls_ = pl.BlockSpec((hps, blk, 1), lambda g, i: (g, i, 0))
    ks_ = pl.BlockSpec((hps, T, dqk), lambda g, i: (g, 0, 0))
    vs_ = pl.BlockSpec((hps, T, dv), lambda g, i: (g, 0, 0))
    return pl.pallas_call(
        body, name=name, grid=(H // hps, T // blk), in_specs=[qs_, ks_, vs_, os_, os_, ls_], out_specs=[qs_, ks_, vs_],
        out_shape=[jax.ShapeDtypeStruct((H, T, dqk), F32), jax.ShapeDtypeStruct((H, T, dqk), F32),
                   jax.ShapeDtypeStruct((H, T, dv), F32)],
        compiler_params=_params("parallel", "arbitrary"),
    )(q, k, v, o, do, lse)


def _tri_sum(x, upper):
    hi = x.astype(BF16)
    lo = (x - hi.astype(F32)).astype(BF16)
    return jnp.dot(hi, upper, preferred_element_type=F32) + jnp.dot(lo, upper, preferred_element_type=F32)


def _keep(mask, x):
    return x if mask is None else jnp.where(mask, x, 0.0)


def _stick_terms(qb, kb, strict):
    z = lax.dot_general(qb, kb, (((1,), (1,)), ((), ())), preferred_element_type=F32)
    e = jnp.exp(-jnp.abs(z))
    lb = jnp.minimum(z, 0.0) - jnp.log(1.0 + e)
    return z, e, lb, _keep(strict, lb - z)


def _stick_fwd(q, k, v, *, blk, hps, scale, name):
    H, T, dh = q.shape

    def body(q_ref, k_ref, v_ref, o_ref, tot_ref):
        n = pl.program_id(1)
        qs = [q_ref[hh] for hh in range(hps)]
        r_i = lax.broadcasted_iota(jnp.int32, (blk, blk), 0)
        c_i = lax.broadcasted_iota(jnp.int32, (blk, blk), 1)
        upper = (r_i > c_i).astype(BF16)

        def block(j, carry, strict):
            start = pl.multiple_of(j * blk, blk)
            out = []
            for hh in range(hps):
                run, acc = carry[hh]
                kb = k_ref[hh, pl.ds(start, blk), :]
                vb = v_ref[hh, pl.ds(start, blk), :]
                _, _, lb, lk = _stick_terms(qs[hh], kb, strict)
                w = _keep(strict, jnp.exp(lb + run + _tri_sum(lk, upper)))
                out.append((run + jnp.sum(lk, axis=1, keepdims=True), acc + jnp.dot(w.astype(BF16), vb, preferred_element_type=F32)))
            return tuple(out)

        init = tuple((jnp.zeros((blk, 1), F32), jnp.zeros((blk, dh), F32)) for _ in range(hps))
        res = lax.fori_loop(1, n + 1, lambda t, carry: block(n - t, carry, None), block(n, init, r_i > c_i))
        for hh in range(hps):
            tot_ref[hh], o_ref[hh] = res[hh]

    qs_ = pl.BlockSpec((hps, blk, dh), lambda g, i: (g, i, 0))
    ks_ = pl.BlockSpec((hps, T, dh), lambda g, i: (g, 0, 0))
    ts_ = pl.BlockSpec((hps, blk, 1), lambda g, i: (g, i, 0))
    return pl.pallas_call(
        body, name=name, grid=(H // hps, T // blk), in_specs=[qs_, ks_, ks_], out_specs=[qs_, ts_],
        out_shape=[jax.ShapeDtypeStruct((H, T, dh), F32), jax.ShapeDtypeStruct((H, T, 1), F32)],
        compiler_params=_params("parallel", "parallel"),
    )(q, k, v)


def _stick_bwd(q, k, v, tot, do, *, blk, hps, scale, name):
    H, T, dh = q.shape

    def body(q_ref, k_ref, v_ref, tot_ref, do_ref, dq_ref, dk_ref, dv_ref):
        n = pl.program_id(1)
        qs = [q_ref[hh] for hh in range(hps)]
        do16 = [do_ref[hh].astype(BF16) for hh in range(hps)]
        tots = [tot_ref[hh] for hh in range(hps)]
        r_i = lax.broadcasted_iota(jnp.int32, (blk, blk), 0)
        c_i = lax.broadcasted_iota(jnp.int32, (blk, blk), 1)
        upto = (r_i <= c_i).astype(BF16)
        below = (r_i < c_i).astype(BF16)

        @pl.when(n == 0)
        def _():
            dk_ref[...] = jnp.zeros_like(dk_ref)
            dv_ref[...] = jnp.zeros_like(dv_ref)

        def block(j, carry, strict):
            start = pl.multiple_of(j * blk, blk)
            out = []
            for hh in range(hps):
                run, grun, dq = carry[hh]
                kb = k_ref[hh, pl.ds(start, blk), :]
                vb = v_ref[hh, pl.ds(start, blk), :]
                z, e, lb, lk = _stick_terms(qs[hh], kb, strict)
                w = _keep(strict, jnp.exp(lb + tots[hh] - (run + _tri_sum(lk, upto))))
                dw = lax.dot_general(do16[hh], vb, (((1,), (1,)), ((), ())), preferred_element_type=F32)
                g = w * dw
                before = grun + jnp.dot(g.astype(BF16), below, preferred_element_type=F32)
                inv = 1.0 / (1.0 + e)
                sig = jnp.where(z >= 0, inv, e * inv)
                dz16 = _keep(strict, g * (1.0 - sig) - sig * before).astype(BF16)
                dk_ref[hh, pl.ds(start, blk), :] += lax.dot_general(dz16, qs[hh], (((0,), (0,)), ((), ())), preferred_element_type=F32)
                dv_ref[hh, pl.ds(start, blk), :] += lax.dot_general(w.astype(BF16), do16[hh], (((0,), (0,)), ((), ())),
                                                                    preferred_element_type=F32)
                out.append((run + jnp.sum(lk, axis=1, keepdims=True), grun + jnp.sum(g, axis=1, keepdims=True),
                            dq + jnp.dot(dz16, kb, preferred_element_type=F32)))
            return tuple(out)

        zero = jnp.zeros((blk, 1), F32)
        init = tuple((zero, zero, jnp.zeros((blk, dh), F32)) for _ in range(hps))
        res = block(n, lax.fori_loop(0, n, lambda j, carry: block(j, carry, None), init), r_i > c_i)
        for hh in range(hps):
            dq_ref[hh] = res[hh][2] * scale

    qs_ = pl.BlockSpec((hps, blk, dh), lambda g, i: (g, i, 0))
    ks_ = pl.BlockSpec((hps, T, dh), lambda g, i: (g, 0, 0))
    ts_ = pl.BlockSpec((hps, blk, 1), lambda g, i: (g, i, 0))
    shp = jax.ShapeDtypeStruct((H, T, dh), F32)
    return pl.pallas_call(
        body, name=name, grid=(H // hps, T // blk), in_specs=[qs_, ks_, ks_, ts_, qs_], out_specs=[qs_, ks_, ks_],
        out_shape=[shp, shp, shp],
        compiler_params=_params("parallel", "arbitrary"),
    )(q, k, v, tot, do)


def _mix_fwd(outs, lses, *, name, rows=512):
    H, T, dh = outs[0].shape
    rows = _tile(T, rows)

    def body(o0, o1, o2, l0, l1, l2, y_ref):
        ls = [l0[...], l1[...], l2[...]]
        mx = jnp.maximum(jnp.maximum(ls[0], ls[1]), ls[2])
        es = [jnp.exp(x - mx) for x in ls]
        den = es[0] + es[1] + es[2]
        y_ref[...] = (es[0] * o0[...] + es[1] * o1[...] + es[2] * o2[...]) / den

    os_ = pl.BlockSpec((None, rows, dh), lambda h, i: (h, i, 0))
    ls_ = pl.BlockSpec((None, rows, 1), lambda h, i: (h, i, 0))
    return pl.pallas_call(
        body, name=name, grid=(H, T // rows), in_specs=[os_] * 3 + [ls_] * 3, out_specs=os_,
        out_shape=jax.ShapeDtypeStruct((H, T, dh), F32),
        compiler_params=_params("parallel", "parallel"),
    )(*outs, *lses)


def _mix_bwd(outs, lses, dy, *, name, rows=512):
    H, T, dh = outs[0].shape
    rows = _tile(T, rows)

    def body(o0, o1, o2, l0, l1, l2, dy_ref, d0, d1, d2, g0, g1, g2):
        ls = [l0[...], l1[...], l2[...]]
        mx = jnp.maximum(jnp.maximum(ls[0], ls[1]), ls[2])
        es = [jnp.exp(x - mx) for x in ls]
        den = es[0] + es[1] + es[2]
        mix = [e / den for e in es]
        dyv = dy_ref[...]
        dm = [jnp.sum(dyv * o[...], axis=1, keepdims=True) for o in (o0, o1, o2)]
        avg = mix[0] * dm[0] + mix[1] * dm[1] + mix[2] * dm[2]
        for d_ref, g_ref, w, t in zip((d0, d1, d2), (g0, g1, g2), mix, dm):
            d_ref[...] = w * dyv
            g_ref[...] = w * (t - avg)

    os_ = pl.BlockSpec((None, rows, dh), lambda h, i: (h, i, 0))
    ls_ = pl.BlockSpec((None, rows, 1), lambda h, i: (h, i, 0))
    so = jax.ShapeDtypeStruct((H, T, dh), F32)
    sl = jax.ShapeDtypeStruct((H, T, 1), F32)
    res = pl.pallas_call(
        body, name=name, grid=(H, T // rows), in_specs=[os_] * 3 + [ls_] * 3 + [os_], out_specs=[os_] * 3 + [ls_] * 3,
        out_shape=[so] * 3 + [sl] * 3,
        compiler_params=_params("parallel", "parallel"),
    )(*outs, *lses, dy)
    return res[:3], res[3:]


def _position():
    return lax.axis_index("x"), lax.axis_index("y"), lax.axis_index("c")


def _other_chips(x, y):
    return [(1 - x, y), (x, 1 - y), (1 - x, 1 - y)]


HBM_SPEC = pl.BlockSpec(memory_space=pltpu.HBM)
SEM_SPEC = pl.BlockSpec(memory_space=pltpu.SEMAPHORE)
ANY_SPEC = pl.BlockSpec(memory_space=pl.ANY)
SPLIT_COPY = pltpu.CompilerParams(has_side_effects=pltpu.SideEffectType.DATAFLOW_SIDE_EFFECTING)


def _in_hbm(a):
    return pltpu.with_memory_space_constraint(a, pltpu.HBM)


SAME_CORE_OF_OTHER_CHIPS = ((1, 0, 0), (0, 1, 0), (1, 1, 0))
ALL_OTHER_DEVICES = ((0, 0, 1), (1, 0, 0), (0, 1, 0), (1, 1, 0), (1, 0, 1), (0, 1, 1), (1, 1, 1))


def _peer_copies(srcs, lands, send_sems, recv_sems, relations, src_of, dst_of):
    me = _position()
    copies = []
    for w in range(len(srcs)):
        for k, flips in enumerate(relations):
            peer = tuple(1 - p if f else p for p, f in zip(me, flips))
            i = len(relations) * w + k
            copies.append(pltpu.make_async_remote_copy(
                src_ref=src_of(srcs[w], peer), dst_ref=dst_of(lands[w], k, peer, me), send_sem=send_sems[i],
                recv_sem=recv_sems[i], device_id=peer, device_id_type=MESH))
    return copies


def _copies_start(srcs, land_shapes, after, relations, src_of, dst_of, *, name):
    n = len(srcs)
    m = len(relations) * n

    def body(*refs):
        src_refs, land_refs = refs[:n], refs[n:2 * n]
        send_sems, recv_sems = refs[2 * n + 1:2 * n + 1 + m], refs[2 * n + 1 + m:2 * n + 1 + 2 * m]
        token = refs[-1]
        for cp in _peer_copies(src_refs, land_refs, send_sems, recv_sems, relations, src_of, dst_of):
            cp.start()
        token[...] = jnp.zeros_like(token)

    lands = [_in_hbm(lax.empty(s.shape, s.dtype)) for s in land_shapes]
    out_shape = ([pltpu.SemaphoreType.DMA(())] * (2 * m)
                 + [pltpu.HBM(a.shape, a.dtype) for a in srcs] + [pltpu.HBM(s.shape, s.dtype) for s in land_shapes]
                 + [jax.ShapeDtypeStruct((8, 128), F32)])
    res = pl.pallas_call(
        body, name=name, in_specs=[HBM_SPEC] * (2 * n) + [ANY_SPEC],
        out_specs=[SEM_SPEC] * (2 * m) + [HBM_SPEC] * (2 * n) + [pl.BlockSpec(memory_space=pltpu.VMEM)],
        out_shape=out_shape, input_output_aliases={i: 2 * m + i for i in range(2 * n)}, compiler_params=SPLIT_COPY,
    )(*[_in_hbm(a) for a in srcs], *lands, after)
    return (list(res[:m]), list(res[m:2 * m]), list(res[2 * m:2 * m + n]), list(res[2 * m + n:2 * m + 2 * n]), res[-1])


def _copies_wait(started, after, relations, src_of, dst_of, *, name):
    send_sems, recv_sems, srcs, lands, _ = started
    n = len(srcs)
    m = len(relations) * n

    def body(*refs):
        src_refs, land_refs = refs[:n], refs[n:2 * n]
        send_refs, recv_refs = refs[2 * n:2 * n + m], refs[2 * n + m:2 * n + 2 * m]
        for cp in _peer_copies(src_refs, land_refs, send_refs, recv_refs, relations, src_of, dst_of):
            cp.wait_send()
            cp.wait_recv()

    res = pl.pallas_call(
        body, name=name, in_specs=[HBM_SPEC] * (2 * n) + [SEM_SPEC] * (2 * m) + [ANY_SPEC], out_specs=[HBM_SPEC] * (2 * n),
        out_shape=[pltpu.HBM(a.shape, a.dtype) for a in srcs + lands],
        input_output_aliases={i: i for i in range(2 * n)}, compiler_params=SPLIT_COPY,
    )(*srcs, *lands, *send_sems, *recv_sems, after)
    return list(res[:n]), list(res[n:])


def _half_rows(ref, core):
    half = ref.shape[-2] // 2
    return pl.ds(core * half, half)


def _gather_src(src, peer):
    return src.at[_half_rows(src, peer[2]), :]


def _gather_dst(land, k, peer, me):
    return land.at[2 * me[0] + me[1], _half_rows(land, me[2]), :]


def _gather_landed(land, k, peer, me):
    return land.at[2 * peer[0] + peer[1], _half_rows(land, me[2]), :]


def _exchange_src(src, peer):
    return src.at[2 * peer[0] + peer[1], _half_rows(src, peer[2]), :]


def _exchange_dst(land, k, peer, me):
    return land.at[k]


def _small_src(src, peer):
    return src


def _small_dst(land, k, peer, me):
    return land.at[4 * me[0] + 2 * me[1] + me[2]]


def _small_landed(land, k, peer, me):
    return land.at[4 * peer[0] + 2 * peer[1] + peer[2]]


def _forward_copies(bufs, send_sems, recv_sems, core):
    x, y, c = _position()
    copies = []
    for w in range(len(bufs)):
        for j, chip in enumerate(_other_chips(x, y)):
            rows = _gather_landed(bufs[w], j, chip, (x, y, core))
            copies.append(pltpu.make_async_remote_copy(src_ref=rows, dst_ref=rows, send_sem=send_sems[3 * w + j],
                                                       recv_sem=recv_sems[3 * w + j], device_id=(x, y, 1 - c), device_id_type=MESH))
    return copies


def _forward_start(bufs, *, name):
    n = len(bufs)
    m = 3 * n

    def body(*refs):
        send_sems, recv_sems = refs[n:n + m], refs[n + m:n + 2 * m]
        for cp in _forward_copies(refs[:n], send_sems, recv_sems, lax.axis_index("c")):
            cp.start()

    res = pl.pallas_call(
        body, name=name, in_specs=[HBM_SPEC] * n, out_specs=[SEM_SPEC] * (2 * m) + [HBM_SPEC] * n,
        out_shape=[pltpu.SemaphoreType.DMA(())] * (2 * m) + [pltpu.HBM(a.shape, a.dtype) for a in bufs],
        input_output_aliases={i: 2 * m + i for i in range(n)}, compiler_params=SPLIT_COPY,
    )(*bufs)
    return list(res[:m]), list(res[m:2 * m]), list(res[2 * m:])


def _forward_wait(started, after, *, name):
    send_sems, recv_sems, bufs = started
    n = len(bufs)
    m = 3 * n

    def body(*refs):
        send_refs, recv_refs = refs[n:n + m], refs[n + m:n + 2 * m]
        c = lax.axis_index("c")
        for sent, got in zip(_forward_copies(refs[:n], send_refs, recv_refs, c),
                             _forward_copies(refs[:n], send_refs, recv_refs, 1 - c)):
            sent.wait_send()
            got.wait_recv()

    return list(pl.pallas_call(
        body, name=name, in_specs=[HBM_SPEC] * n + [SEM_SPEC] * (2 * m) + [ANY_SPEC], out_specs=[HBM_SPEC] * n,
        out_shape=[pltpu.HBM(a.shape, a.dtype) for a in bufs], input_output_aliases={i: i for i in range(n)},
        compiler_params=SPLIT_COPY,
    )(*bufs, *send_sems, *recv_sems, after))


def _share_halves(bufs, *, name):
    n = len(bufs)

    def body(*refs):
        ins, outs = refs[:n], refs[n:2 * n]
        send_sems, recv_sems = refs[2 * n:]
        x, y, c = _position()
        sends = []
        for w in range(n):
            sends.append(pltpu.make_async_remote_copy(src_ref=ins[w].at[c], dst_ref=outs[w].at[c], send_sem=send_sems.at[w],
                                                      recv_sem=recv_sems.at[w], device_id=(x, y, 1 - c), device_id_type=MESH))
            sends[-1].start()
        for w in range(n):
            pltpu.make_async_remote_copy(src_ref=ins[w].at[1 - c], dst_ref=outs[w].at[1 - c], send_sem=send_sems.at[w],
                                         recv_sem=recv_sems.at[w], device_id=(x, y, 1 - c), device_id_type=MESH).wait_recv()
        for cp in sends:
            cp.wait_send()

    hbm = pl.BlockSpec(memory_space=pl.ANY)
    return pl.pallas_call(
        body, name=name, in_specs=[hbm] * n, out_specs=[hbm] * n,
        out_shape=[jax.ShapeDtypeStruct(a.shape, a.dtype) for a in bufs],
        input_output_aliases={w: w for w in range(n)},
        scratch_shapes=[pltpu.SemaphoreType.DMA((n,)), pltpu.SemaphoreType.DMA((n,))],
    )(*bufs)


def _all_gather8(v, *, name):
    m, n = v.shape

    def body(v_ref, out_ref, send_sems, recv_sems, local_sem):
        x, y, c = _position()
        me, sibling = (x, y, c), (x, y, 1 - c)
        chips = _other_chips(x, y)

        def rows(px, py, pc):
            return out_ref.at[pl.ds((4 * px + 2 * py + pc) * m, m), :]

        def copy(k, block, to, src=None):
            return pltpu.make_async_remote_copy(src_ref=rows(*block) if src is None else src, dst_ref=rows(*block),
                                                send_sem=send_sems.at[k], recv_sem=recv_sems.at[k], device_id=to, device_id_type=MESH)

        mine = pltpu.make_async_copy(v_ref, rows(*me), local_sem)
        mine.start()
        first = [copy(0, me, sibling, src=v_ref)]
        first += [copy(1 + j, me, (*chip, c), src=v_ref) for j, chip in enumerate(chips)]
        for cp in first:
            cp.start()
        passed = [copy(4 + j, (*chip, c), sibling) for j, chip in enumerate(chips)]
        for j, chip in enumerate(chips):
            copy(1 + j, (*chip, c), me).wait_recv()
            passed[j].start()
        copy(0, sibling, me).wait_recv()
        for j, chip in enumerate(chips):
            copy(4 + j, (*chip, 1 - c), me).wait_recv()
        for cp in first + passed:
            cp.wait_send()
        mine.wait()

    vmem = pl.BlockSpec(memory_space=pltpu.VMEM)
    return pl.pallas_call(
        body, name=name, in_specs=[vmem], out_specs=vmem, out_shape=jax.ShapeDtypeStruct((8 * m, n), v.dtype),
        scratch_shapes=[pltpu.SemaphoreType.DMA((7,)), pltpu.SemaphoreType.DMA((7,)), pltpu.SemaphoreType.DMA],
    )(v)


def _sum_parts(own, landed, where, *, name, rows=256):
    n_peers, half, C = landed.shape
    rows = _tile(half, rows)
    nb = half // rows

    def body(where_ref, own_ref, land_ref, o_ref):
        acc = own_ref[...].astype(F32)
        for k in range(n_peers):
            acc = acc + land_ref[k].astype(F32)
        o_ref[...] = acc

    grid_spec = pltpu.PrefetchScalarGridSpec(
        num_scalar_prefetch=1, grid=(nb,),
        in_specs=[pl.BlockSpec((None, rows, C), lambda i, where_ref: (where_ref[0], where_ref[1] * nb + i, 0)),
                  pl.BlockSpec((n_peers, rows, C), lambda i, where_ref: (0, i, 0))],
        out_specs=pl.BlockSpec((None, rows, C), lambda i, where_ref: (where_ref[1], i, 0)))
    return pl.pallas_call(
        body, name=name, grid_spec=grid_spec, out_shape=jax.ShapeDtypeStruct((2, half, C), F32),
        compiler_params=_params("parallel"),
    )(where, own, landed)


def _sum_slabs(a, *, name, rows=256):
    P, R, C = a.shape
    rows = _tile(R, rows)

    def body(a_ref, o_ref):
        acc = a_ref[0].astype(F32)
        for p in range(1, P):
            acc = acc + a_ref[p].astype(F32)
        o_ref[...] = acc

    return pl.pallas_call(
        body, name=name, grid=(R // rows,), in_specs=[pl.BlockSpec((P, rows, C), lambda i: (0, i, 0))],
        out_specs=pl.BlockSpec((rows, C), lambda i: (i, 0)),
        out_shape=jax.ShapeDtypeStruct((R, C), F32), compiler_params=_params("parallel"),
    )(a)


def _reduce_scatter_start(grads, after=None, *, name):
    lands = [jax.ShapeDtypeStruct((len(ALL_OTHER_DEVICES), g.shape[1] // 2, g.shape[2]), g.dtype) for g in grads]
    return _copies_start(grads, lands, grads[0] if after is None else after, ALL_OTHER_DEVICES, _exchange_src, _exchange_dst,
                         name=name + "_start")


def _reduce_scatter_finish(started, after, *, name):
    core = lax.axis_index("c").astype(jnp.int32)
    chip = (2 * lax.axis_index("x") + lax.axis_index("y")).astype(jnp.int32)
    where = jnp.stack([chip, core])
    sums, landed = _copies_wait(started, after, ALL_OTHER_DEVICES, _exchange_src, _exchange_dst, name=name + "_wait")
    reduced = [_sum_parts(s, a, where, name=f"{name}_sum{w}") for w, (s, a) in enumerate(zip(sums, landed))]
    both = _share_halves(reduced, name=name + "_share")
    return [b.reshape(2 * b.shape[1], b.shape[2]) for b in both]


def _to_heads(t, heads, dil=1):
    S = t.shape[0]
    d = t.shape[1] // heads
    return jnp.transpose(t.reshape(S // dil, dil, heads, d), (2, 1, 0, 3)).reshape(heads, S, d)


def _from_heads(t, dil=1):
    heads, S, d = t.shape
    return jnp.transpose(t.reshape(heads, dil, S // dil, d), (2, 1, 0, 3)).reshape(S, heads * d)


def _unstride(t, dil):
    heads, S, d = t.shape
    return jnp.transpose(t.reshape(heads, dil, S // dil, d), (0, 2, 1, 3)).reshape(heads, S, d)


def _restride(t, dil):
    heads, S, d = t.shape
    return jnp.transpose(t.reshape(heads, S // dil, dil, d), (0, 2, 1, 3)).reshape(heads, S, d)


def _pad_cols(t, width, padded):
    S = t.shape[0]
    return jnp.pad(t.reshape(S, N_CHIPS, width), ((0, 0), (0, 0), (0, padded - width))).reshape(S, N_CHIPS * padded)


def _column_reader(t, width, padded):
    def take(lo, hi):
        pieces = []
        for p in range(N_CHIPS):
            a, b = max(lo, p * width), min(hi, (p + 1) * width)
            if a < b:
                pieces.append(t[:, p * padded + a - p * width:p * padded + b - p * width])
        return pieces[0] if len(pieces) == 1 else jnp.concatenate(pieces, axis=-1)
    return take


def _alibi(n):
    return 2.0 ** (-8.0 * jnp.arange(1, n + 1, dtype=F32) / n)


B_KW = [dict(hps=4, nb_seq=SEQ // d // ATT_BLK, n_back=w // d, scale=1.0 / math.sqrt(HEAD_DIM), dist_scale=d)
        for w, d in B_PATTERNS]
A_KW = dict(hps=A_Q_HEADS // A_KV_HEADS, nb_seq=SEQ // ATT_BLK, n_back=A_WINDOW - 1, scale=1.0 / math.sqrt(HEAD_DIM), dist_scale=1)
D_KW = dict(blk=1024, hps=1, scale=1.0 / math.sqrt(D_NOPE + D_ROPE))
C_KW = dict(blk=512, hps=1, scale=1.0 / math.sqrt(HEAD_DIM))


def _rope_tables(reps):
    inv_freq = ROPE_BASE ** (-jnp.arange(0, D_ROPE, 2, dtype=F32) / D_ROPE)
    ang = jnp.arange(SEQ, dtype=F32)[:, None] * inv_freq[None, :]
    return jnp.tile(jnp.cos(ang), (1, reps)), jnp.tile(jnp.sin(ang), (1, reps))


def _mlp_fwd(x, x16, w1, w2, g, b, tag):
    hid, act = _matmul(x16, w1, mode="nn", out_dtype=BF16, epilogue="relu2", name=f"mlp{tag}_up")
    m = _matmul(act, w2, mode="nn", out_dtype=F32, name=f"mlp{tag}_down")
    y, y16, u = _norm_fwd(x, m, g, b, center=True, eps=LN_EPS, alpha=ALPHA, name=f"ln2_{tag}")
    return y, y16, (x16, hid, act, u)


def _mlp_bwd(dy, saved, w1, w2, g, tag, after=None):
    x16, hid, act, u = saved
    du, du16, dg, db = _norm_bwd(dy, u, g, center=True, eps=LN_EPS, name=f"ln2_{tag}_bwd", after=after)
    dw2 = _matmul_tn(act, du16, shards=1, name=f"mlp{tag}_dw2")
    dhid = _matmul(du16, w2, mode="nt", out_dtype=BF16, epilogue="drelu2", extra=hid, name=f"mlp{tag}_dact")
    dw1 = _matmul_tn(x16, dhid, shards=N_CHIPS, name=f"mlp{tag}_dw1")
    dx = _matmul(dhid, w1, mode="nt", out_dtype=F32, epilogue="add", extra=du, alpha=ALPHA, name=f"mlp{tag}_dx")
    return dx, dw1, dw2, dg, db


def _even_fwd(x16, w_in, sinks):
    h = _column_reader(_matmul(x16, w_in, mode="nn", out_dtype=BF16, name="even_in"), EVEN_IN // N_CHIPS, EVEN_IN_PAD)
    qa = _to_heads(h(0, A_Q_W), A_Q_HEADS) * A_KW["scale"]
    ka = _to_heads(h(A_Q_W, A_Q_W + A_KV_W), A_KV_HEADS)
    va = _to_heads(h(A_Q_W + A_KV_W, A_Q_W + 2 * A_KV_W), A_KV_HEADS)
    slopes_a, slopes_b = _alibi(A_Q_HEADS), _alibi(B_HEADS)
    oa, lse_a = _band_fwd(qa, ka, va, slopes_a, sinks, name="swa_fwd", **A_KW)
    qkv_b, outs, lses = [], [], []
    for gi, (_, dil) in enumerate(B_PATTERNS):
        base = A_Q_W + 2 * A_KV_W + gi * 3 * B_W
        q, k, v = (_to_heads(h(base + i * B_W, base + (i + 1) * B_W), B_HEADS, dil) for i in range(3))
        q = q * B_KW[gi]["scale"]
        o, lse = _band_fwd(q, k, v, slopes_b, None, name=f"dil{gi}_fwd", **B_KW[gi])
        qkv_b.append((q, k, v, o, lse))
        outs.append(_unstride(o, dil))
        lses.append(_unstride(lse, dil))
    ob = _mix_fwd(outs, lses, name="dil_mix")
    y16 = jnp.concatenate([_from_heads(oa), _from_heads(ob)], axis=-1).astype(BF16)
    return y16, (x16, qa, ka, va, oa, lse_a, qkv_b, outs, lses, y16)


def _even_bwd(dmix16, du, saved, w_in, sinks, w_out, send_w_out, send_w_in):
    x16, qa, ka, va, oa, lse_a, qkv_b, outs, lses, y16 = saved
    slopes_a, slopes_b = _alibi(A_Q_HEADS), _alibi(B_HEADS)
    sent = send_w_out(_matmul_tn(y16, dmix16, shards=N_CHIPS, name="even_dwout"))
    dy = _matmul(dmix16, w_out, mode="nt", out_dtype=F32, name="even_dy", after=sent)
    doa = _to_heads(dy[:, :A_Q_W], A_Q_HEADS)
    dob = _to_heads(dy[:, A_Q_W:], B_HEADS)
    dqa, dka, dva, dsink = _band_bwd(qa, ka, va, oa, doa, lse_a, None, slopes_a, sinks, name="swa_bwd", **A_KW)
    douts, dlses = _mix_bwd(outs, lses, dob, name="dil_mix_bwd")
    parts = [_from_heads(dqa), _from_heads(dka), _from_heads(dva)]
    for gi, (_, dil) in enumerate(B_PATTERNS):
        q, k, v, o, lse = qkv_b[gi]
        dq, dk, dv = _band_bwd(q, k, v, o, _restride(douts[gi], dil), lse, _restride(dlses[gi], dil), slopes_b, None,
                               name=f"dil{gi}_bwd", **B_KW[gi])
        parts += [_from_heads(dq, dil), _from_heads(dk, dil), _from_heads(dv, dil)]
    dh16 = _pad_cols(jnp.concatenate(parts, axis=-1), EVEN_IN // N_CHIPS, EVEN_IN_PAD).astype(BF16)
    sent = send_w_in(_matmul_tn(x16, dh16, shards=N_CHIPS, name="even_dwin"), dsink[:, 0, 0])
    return _matmul(dh16, w_in, mode="nt", out_dtype=F32, epilogue="add", extra=du, alpha=ALPHA, name="even_dx", after=sent)


def _odd_fwd(x16, w_in, qn_g, kvn_g, w_uq, w_ukv, w_out):
    S = x16.shape[0]
    h = _column_reader(_matmul(x16, w_in, mode="nn", out_dtype=F32, name="odd_in"), ODD_IN // N_CHIPS, ODD_IN_PAD)
    qc, kc, vc = (_to_heads(h(i * C_W, (i + 1) * C_W).astype(BF16), C_HEADS) for i in range(3))
    qc = qc * C_KW["scale"]
    cq = h(3 * C_W, 3 * C_W + D_Q_RANK)
    ckv = h(3 * C_W + D_Q_RANK, 3 * C_W + D_Q_RANK + D_KV_RANK)
    kr = h(3 * C_W + D_Q_RANK + D_KV_RANK, ODD_IN)
    oc, tot = _stick_fwd(qc, kc, vc, name="stick_fwd", **C_KW)
    _, cqn16, _ = _norm_fwd(cq, None, qn_g, None, center=False, eps=RMS_EPS, alpha=1.0, name="q_rms")
    _, ckvn16, _ = _norm_fwd(ckv, None, kvn_g, None, center=False, eps=RMS_EPS, alpha=1.0, name="kv_rms")
    q = _matmul(cqn16, w_uq, mode="nn", out_dtype=F32, name="mla_uq").reshape(S, D_HEADS, D_NOPE + D_ROPE)
    kv = _matmul(ckvn16, w_ukv, mode="nn", out_dtype=F32, name="mla_ukv").reshape(S, D_HEADS, D_NOPE + D_V)
    half = D_ROPE // 2
    q_rope = q[..., D_NOPE:]
    x1 = jnp.concatenate([q_rope[..., :half].reshape(S, D_HEADS * half), kr[:, :half]], axis=-1)
    x2 = jnp.concatenate([q_rope[..., half:].reshape(S, D_HEADS * half), kr[:, half:]], axis=-1)
    cos, sin = _rope_tables(D_HEADS + 1)
    y1, y2 = _rope(x1[None], x2[None], cos, sin, name="rope_fwd")
    nq = D_HEADS * half
    q_rot = jnp.concatenate([y1[:, :nq].reshape(S, D_HEADS, half), y2[:, :nq].reshape(S, D_HEADS, half)], axis=-1)
    kr_rot = jnp.concatenate([y1[:, nq:], y2[:, nq:]], axis=-1)
    qd = jnp.transpose(jnp.concatenate([q[..., :D_NOPE], q_rot], axis=-1), (1, 0, 2)).astype(BF16)
    kd = jnp.transpose(jnp.concatenate([kv[..., :D_NOPE], jnp.broadcast_to(kr_rot[:, None, :], (S, D_HEADS, D_ROPE))], axis=-1),
                       (1, 0, 2)).astype(BF16)
    vd = jnp.transpose(kv[..., D_NOPE:], (1, 0, 2)).astype(BF16)
    od, lse_d = _causal_fwd(qd, kd, vd, name="mla_fwd", **D_KW)
    y16 = jnp.concatenate([_from_heads(oc), _from_heads(od)], axis=-1).astype(BF16)
    mixed = _matmul(y16, w_out, mode="nn", out_dtype=F32, name="odd_out")
    return mixed, (x16, qc, kc, vc, tot, cq, ckv, cqn16, ckvn16, qd, kd, vd, od, lse_d, y16)


def _odd_bwd(dmix16, du, saved, w_in, qn_g, kvn_g, w_uq, w_ukv, w_out):
    x16, qc, kc, vc, tot, cq, ckv, cqn16, ckvn16, qd, kd, vd, od, lse_d, y16 = saved
    S = x16.shape[0]
    half = D_ROPE // 2
    dw_out = _matmul_tn(y16, dmix16, shards=1, name="odd_dwout")
    dy = _matmul(dmix16, w_out, mode="nt", out_dtype=F32, name="odd_dy")
    doc = _to_heads(dy[:, :C_W], C_HEADS)
    dod = _to_heads(dy[:, C_W:], D_HEADS)
    dqc, dkc, dvc = _stick_bwd(qc, kc, vc, tot, doc, name="stick_bwd", **C_KW)
    dqd, dkd, dvd = _causal_bwd(qd, kd, vd, od, dod, lse_d, name="mla_bwd", **D_KW)
    cos, sin = _rope_tables(D_HEADS + 1)
    nq = D_HEADS * half
    gq1 = jnp.transpose(dqd[..., D_NOPE:D_NOPE + half], (1, 0, 2)).reshape(1, S, nq)
    gq2 = jnp.transpose(dqd[..., D_NOPE + half:], (1, 0, 2)).reshape(1, S, nq)
    dq1, dq2 = _rope(gq1, gq2, cos[:, :nq], -sin[:, :nq], name="rope_bwd_q")
    dk1, dk2 = _rope(dkd[..., D_NOPE:D_NOPE + half], dkd[..., D_NOPE + half:], cos[:, nq:], -sin[:, nq:], name="rope_bwd_k")
    dq_full = jnp.concatenate([jnp.transpose(dqd[..., :D_NOPE], (1, 0, 2)), dq1.reshape(S, D_HEADS, half),
                               dq2.reshape(S, D_HEADS, half)], axis=-1).reshape(S, D_HEADS * (D_NOPE + D_ROPE)).astype(BF16)
    dkv_full = jnp.concatenate([jnp.transpose(dkd[..., :D_NOPE], (1, 0, 2)), jnp.transpose(dvd, (1, 0, 2))],
                               axis=-1).reshape(S, D_HEADS * (D_NOPE + D_V)).astype(BF16)
    dw_uq = _matmul_tn(cqn16, dq_full, shards=N_CHIPS, name="mla_dwuq")
    dw_ukv = _matmul_tn(ckvn16, dkv_full, shards=N_CHIPS, name="mla_dwukv")
    dcqn = _matmul(dq_full, w_uq, mode="nt", out_dtype=F32, name="mla_dcq")
    dckvn = _matmul(dkv_full, w_ukv, mode="nt", out_dtype=F32, name="mla_dckv")
    dcq, _, dqn_g, _ = _norm_bwd(dcqn, cq, qn_g, center=False, eps=RMS_EPS, name="q_rms_bwd")
    dckv, _, dkvn_g, _ = _norm_bwd(dckvn, ckv, kvn_g, center=False, eps=RMS_EPS, name="kv_rms_bwd")
    dh = jnp.concatenate([_from_heads(dqc), _from_heads(dkc), _from_heads(dvc), dcq, dckv, dk1, dk2], axis=-1)
    dh16 = _pad_cols(dh, ODD_IN // N_CHIPS, ODD_IN_PAD).astype(BF16)
    dw_in = _matmul_tn(x16, dh16, shards=N_CHIPS, name="odd_dwin")
    dx = _matmul(dh16, w_in, mode="nt", out_dtype=F32, epilogue="add", extra=du, alpha=ALPHA, name="odd_dx")
    return dx, dw_in, dqn_g[0], dkvn_g[0], dw_uq, dw_ukv, dw_out


WEIGHT_GROUPS = (("even_w_in",), ("even_w_out", "mlp_w1_0", "mlp_w2_0"), ("odd_w_in", "odd_w_uq", "odd_w_ukv", "odd_w_out"),
                 ("mlp_w1_1", "mlp_w2_1"))
GRAD_GROUPS = (("mlp_w2_1", "mlp_w1_1"), ("odd_w_out", "odd_w_uq", "odd_w_ukv", "odd_w_in"), ("mlp_w2_0", "mlp_w1_0"),
               ("even_w_out",), ("even_w_in",))


def _local_step(x, target, small, weights_for, send_grads, send_small, total_loss):
    def by_rows(t, rows):
        return t.reshape(N_CHIPS, rows // N_CHIPS, D_MODEL)

    x16 = x.astype(BF16)
    w = dict(weights_for(0, x16))
    y16, sv_e = _even_fwd(x16, w["even_w_in"], small["even_sinks"])
    w.update(weights_for(1, y16))
    mixed0 = _matmul(y16, w["even_w_out"], mode="nn", out_dtype=F32, name="even_out")
    x1, x1_16, u01 = _norm_fwd(x, mixed0, small["ln1_g"][0], small["ln1_b"][0], center=True, eps=LN_EPS, alpha=ALPHA, name="ln1_0")
    w1_0, w2_0 = w["mlp_w1_0"], w["mlp_w2_0"].reshape(1, D_FF, D_MODEL)
    x2, x2_16, sv_m0 = _mlp_fwd(x1, x1_16, w1_0, w2_0, small["ln2_g"][0], small["ln2_b"][0], 0)
    w.update(weights_for(2, x2_16))
    odd_w = (w["odd_w_in"], small["odd_q_norm_g"], small["odd_kv_norm_g"], w["odd_w_uq"], w["odd_w_ukv"],
             w["odd_w_out"].reshape(1, D_MODEL, D_MODEL))
    mixed1, sv_o = _odd_fwd(x2_16, *odd_w)
    x3, x3_16, u11 = _norm_fwd(x2, mixed1, small["ln1_g"][1], small["ln1_b"][1], center=True, eps=LN_EPS, alpha=ALPHA, name="ln1_1")
    w.update(weights_for(3, x3_16))
    w1_1, w2_1 = w["mlp_w1_1"], w["mlp_w2_1"].reshape(1, D_FF, D_MODEL)
    x4, _, sv_m1 = _mlp_fwd(x3, x3_16, w1_1, w2_1, small["ln2_g"][1], small["ln2_b"][1], 1)
    dy, loss_row = _loss_head(x4, target, name="loss_head")
    loss = total_loss(loss_row)

    dx3, dw1_1, dw2_1, dln2g_1, dln2b_1 = _mlp_bwd(dy, sv_m1, w1_1, w2_1, small["ln2_g"][1], 1, after=loss.reshape(1, 1))
    sent = send_grads(0, dict(mlp_w2_1=by_rows(dw2_1, D_FF), mlp_w1_1=dw1_1))
    du, du16, dln1g_1, dln1b_1 = _norm_bwd(dx3, u11, small["ln1_g"][1], center=True, eps=LN_EPS, name="ln1_1_bwd", after=sent)
    dx2, dwin_o, dqn_g, dkvn_g, dwuq, dwukv, dwout_o = _odd_bwd(du16, du, sv_o, *odd_w)
    sent = send_grads(1, dict(odd_w_out=by_rows(dwout_o, D_MODEL), odd_w_uq=dwuq, odd_w_ukv=dwukv, odd_w_in=dwin_o))
    dx1, dw1_0, dw2_0, dln2g_0, dln2b_0 = _mlp_bwd(dx2, sv_m0, w1_0, w2_0, small["ln2_g"][0], 0, after=sent)
    sent = send_grads(2, dict(mlp_w2_0=by_rows(dw2_0, D_FF), mlp_w1_0=dw1_0))
    du, du16, dln1g_0, dln1b_0 = _norm_bwd(dx1, u01, small["ln1_g"][0], center=True, eps=LN_EPS, name="ln1_0_bwd", after=sent)
    def send_last(dwin_e, dsinks):
        went = send_small(dict(even_sinks=dsinks, odd_q_norm_g=dqn_g, odd_kv_norm_g=dkvn_g,
                               ln1_g=jnp.concatenate([dln1g_0, dln1g_1]), ln1_b=jnp.concatenate([dln1b_0, dln1b_1]),
                               ln2_g=jnp.concatenate([dln2g_0, dln2g_1]), ln2_b=jnp.concatenate([dln2b_0, dln2b_1])))
        return send_grads(4, dict(even_w_in=dwin_e), went)

    dx0 = _even_bwd(du16, du, sv_e, w["even_w_in"], small["even_sinks"], w["even_w_out"],
                    lambda dwout_e: send_grads(3, dict(even_w_out=dwout_e)), send_last)
    return loss, dx0


SMALL_ORDER = ("ln1_g", "ln1_b", "ln2_g", "ln2_b", "even_sinks", "odd_q_norm_g", "odd_kv_norm_g")
SMALL_COLS = 2176


def _pack_small(parts):
    flat = jnp.concatenate([p.reshape(-1) for p in parts])
    return jnp.pad(flat, (0, 8 * SMALL_COLS - flat.shape[0])).reshape(8, SMALL_COLS)


def _unpack_small(packed, shapes):
    flat = packed.reshape(-1)
    out, pos = [], 0
    for s in shapes:
        n = math.prod(s)
        out.append(flat[pos:pos + n].reshape(s))
        pos += n
    return out


def kernel(x, even_w_in, even_sinks, even_w_out, odd_w_in, odd_q_norm_g, odd_kv_norm_g, odd_w_uq, odd_w_ukv, odd_w_out, ln1_g, ln1_b, mlp_w1, mlp_w2, ln2_g, ln2_b, loss_target, m_even_w_in, m_even_sinks, m_even_w_out, m_odd_w_in, m_odd_q_norm_g, m_odd_kv_norm_g, m_odd_w_uq, m_odd_w_ukv, m_odd_w_out, m_ln1_g, m_ln1_b, m_mlp_w1, m_mlp_w2, m_ln2_g, m_ln2_b, v_even_w_in, v_even_sinks, v_even_w_out, v_odd_w_in, v_odd_q_norm_g, v_odd_kv_norm_g, v_odd_w_uq, v_odd_w_ukv, v_odd_w_out, v_ln1_g, v_ln1_b, v_mlp_w1, v_mlp_w2, v_ln2_g, v_ln2_b):
    chip = 2 * lax.axis_index("x") + lax.axis_index("y")
    e_w, o_w = EVEN_IN // N_CHIPS, ODD_IN // N_CHIPS

    shards = dict(
        even_w_in=jnp.pad(even_w_in[0], ((0, 0), (0, EVEN_IN_PAD - e_w))), even_w_out=even_w_out[0],
        odd_w_in=jnp.pad(odd_w_in[0], ((0, 0), (0, ODD_IN_PAD - o_w))), odd_w_uq=odd_w_uq[0], odd_w_ukv=odd_w_ukv[0],
        odd_w_out=odd_w_out[0], mlp_w1_0=mlp_w1[0], mlp_w1_1=mlp_w1[1], mlp_w2_0=mlp_w2[0], mlp_w2_1=mlp_w2[1])
    names = list(shards)
    own16 = {n: shards[n].astype(BF16) for n in names}
    gather_started, token = [], own16[WEIGHT_GROUPS[0][0]]
    for g, group in enumerate(WEIGHT_GROUPS):
        lands = [jax.ShapeDtypeStruct((N_CHIPS,) + own16[n].shape, BF16) for n in group]
        gather_started.append(_copies_start([own16[n] for n in group], lands, token, SAME_CORE_OF_OTHER_CHIPS, _gather_src,
                                            _gather_dst, name=f"gather{g}_start"))
        token = gather_started[-1][-1]
    all_started = token

    forwarding = {}

    def forward(g, after):
        _, landed = _copies_wait(gather_started[g], after, SAME_CORE_OF_OTHER_CHIPS, _gather_src, _gather_landed,
                                 name=f"gather{g}_wait")
        forwarding[g] = _forward_start(landed, name=f"gather{g}_forward")

    def weights_for(g, after):
        if g not in forwarding:
            forward(g, all_started if g == 0 else after)
        full = _forward_wait(forwarding[g], after, name=f"gather{g}_forwarded")
        if 1 <= g < len(WEIGHT_GROUPS) - 1:
            forward(g + 1, after)
        return {n: lax.dynamic_update_slice(f, own16[n][None], (chip, 0, 0)) for n, f in zip(WEIGHT_GROUPS[g], full)}

    grads_started = {}

    def send_grads(g, shares, after=None):
        grads_started[g] = _reduce_scatter_start([shares[n] for n in GRAD_GROUPS[g]], after, name=f"grads{g}")
        return grads_started[g][-1]

    norm_rows = jnp.pad(jnp.concatenate([odd_q_norm_g[0], odd_kv_norm_g[0]])[None], ((0, 7), (0, 64)))
    norm_all = _all_gather8(norm_rows, name="gather_norm_gains")[0::16]
    qn_w, kvn_w = D_Q_RANK // N_CHIPS, D_KV_RANK // N_CHIPS
    small = dict(even_sinks=even_sinks[0], odd_q_norm_g=norm_all[:, :qn_w].reshape(D_Q_RANK),
                 odd_kv_norm_g=norm_all[:, qn_w:qn_w + kvn_w].reshape(D_KV_RANK), ln1_g=ln1_g, ln1_b=ln1_b, ln2_g=ln2_g, ln2_b=ln2_b)

    small_started = []

    def send_small(sm):
        pack = _pack_small([sm[n] for n in SMALL_ORDER])
        small_started.append(_copies_start([pack], [jax.ShapeDtypeStruct((8,) + pack.shape, F32)], pack, ALL_OTHER_DEVICES,
                                           _small_src, _small_dst, name="small_grads_start"))
        return small_started[0][-1]

    def small_sum(after):
        (pack,), (landed,) = _copies_wait(small_started[0], after, ALL_OTHER_DEVICES, _small_src, _small_landed,
                                          name="small_grads_wait")
        device = 4 * lax.axis_index("x") + 2 * lax.axis_index("y") + lax.axis_index("c")
        return _sum_slabs(lax.dynamic_update_slice(landed, pack[None], (device, 0, 0)), name="sum_small_grads")

    loss, grad_x = _local_step(x[0], loss_target[0], small, weights_for, send_grads, send_small,
                               lambda row: lax.psum(row[0, 0], ("x", "y", "c")))

    weights = dict(even_w_in=even_w_in, even_sinks=even_sinks, even_w_out=even_w_out, odd_w_in=odd_w_in, odd_q_norm_g=odd_q_norm_g,
                   odd_kv_norm_g=odd_kv_norm_g, odd_w_uq=odd_w_uq, odd_w_ukv=odd_w_ukv, odd_w_out=odd_w_out, ln1_g=ln1_g, ln1_b=ln1_b,
                   mlp_w1=mlp_w1, mlp_w2=mlp_w2, ln2_g=ln2_g, ln2_b=ln2_b)
    m_in = dict(even_w_in=m_even_w_in, even_sinks=m_even_sinks, even_w_out=m_even_w_out, odd_w_in=m_odd_w_in,
                odd_q_norm_g=m_odd_q_norm_g, odd_kv_norm_g=m_odd_kv_norm_g, odd_w_uq=m_odd_w_uq, odd_w_ukv=m_odd_w_ukv,
                odd_w_out=m_odd_w_out, ln1_g=m_ln1_g, ln1_b=m_ln1_b, mlp_w1=m_mlp_w1, mlp_w2=m_mlp_w2, ln2_g=m_ln2_g, ln2_b=m_ln2_b)
    v_in = dict(even_w_in=v_even_w_in, even_sinks=v_even_sinks, even_w_out=v_even_w_out, odd_w_in=v_odd_w_in,
                odd_q_norm_g=v_odd_q_norm_g, odd_kv_norm_g=v_odd_kv_norm_g, odd_w_uq=v_odd_w_uq, odd_w_ukv=v_odd_w_ukv,
                odd_w_out=v_odd_w_out, ln1_g=v_ln1_g, ln1_b=v_ln1_b, mlp_w1=v_mlp_w1, mlp_w2=v_mlp_w2, ln2_g=v_ln2_g, ln2_b=v_ln2_b)
    order = list(weights)
    updated = {}

    def update(n, g2d, layer=0, tag="", by_column=False):
        shape = weights[n].shape
        as3d = (1, math.prod(shape[:-1]), shape[-1]) if len(shape) == 2 else shape
        view = (lambda t: jnp.swapaxes(t.reshape(as3d), 1, 2)) if by_column else (lambda t: t.reshape(as3d))
        res = _adamw(view(weights[n]), g2d.T if by_column else g2d, view(m_in[n]), view(v_in[n]), layer=layer,
                     carry=updated.get(n), name=f"adamw_{n}{tag}")
        updated[n] = tuple(jnp.swapaxes(t, 1, 2) for t in res) if by_column else tuple(res)
        return res[0]

    after = grad_x
    for g, group in enumerate(GRAD_GROUPS):
        for n, r in zip(group, _reduce_scatter_finish(grads_started[g], after, name=f"grads{g}")):
            if n[:-2] in ("mlp_w1", "mlp_w2"):
                after = update(n[:-2], r, layer=int(n[-1]), tag=n[-2:])
            elif n == "even_w_in":
                after = update(n, r[:, :e_w], by_column=True)
            elif n == "odd_w_in":
                after = update(n, r[:, :o_w], by_column=True)
            else:
                after = update(n, r)
        if g == len(GRAD_GROUPS) - 2:
            g_ln1g, g_ln1b, g_ln2g, g_ln2b, g_sinks, g_qn, g_kvn = _unpack_small(
                small_sum(after), [(DEPTH, D_MODEL)] * 4 + [(1, A_Q_HEADS), (D_Q_RANK,), (D_KV_RANK,)])
            for n, gs in (("even_sinks", g_sinks), ("odd_q_norm_g", lax.dynamic_slice_in_dim(g_qn, chip * qn_w, qn_w)[None]),
                          ("odd_kv_norm_g", lax.dynamic_slice_in_dim(g_kvn, chip * kvn_w, kvn_w)[None]), ("ln1_g", g_ln1g),
                          ("ln1_b", g_ln1b), ("ln2_g", g_ln2g), ("ln2_b", g_ln2b)):
                update(n, gs)
    outs = [[updated[n][k].reshape(weights[n].shape) for n in order] for k in range(4)]
    return (loss, grad_x[None], *outs[0], *outs[1], *outs[2], *outs[3])
```
